```python
import jax
import jax.numpy as jnp
from jax import lax
import numpy as np

D_MODEL = 2048
BATCH = 4
SEQ = 4096
DEPTH = 2

F32 = jnp.float32
RMS_EPS = 1e-6
MASK_VALUE = -1e30
LB_FLOOR = 1e-30
GRID_W = 64
MEM_TOKENS = 256

NA_HEADS = 8
NA_HEAD_DIM = 64
NA_WIDTH = NA_HEADS * NA_HEAD_DIM
NA_WIN_ROWS = 8
NA_WIN_COLS = 16

GLA_HEADS = 4
GLA_HEAD_K = 64
GLA_HEAD_V = 128
GLA_KEY_WIDTH = GLA_HEADS * GLA_HEAD_K
GLA_VAL_WIDTH = GLA_HEADS * GLA_HEAD_V
GLA_GATE_RANK = 16
GLA_GATE_NORMALIZER = 16.0

GDN_HEADS = 4
GDN_HEAD_K = 128
GDN_HEAD_V = 128
GDN_KEY_WIDTH = GDN_HEADS * GDN_HEAD_K
GDN_VAL_WIDTH = GDN_HEADS * GDN_HEAD_V
GDN_CONV = 5
GDN_CHUNK = 64

HGRN_HEADS = 4
HGRN_HEAD_K = 128
HGRN_HEAD_V = 128
HGRN_KEY_WIDTH = HGRN_HEADS * HGRN_HEAD_K
HGRN_VAL_WIDTH = HGRN_HEADS * HGRN_HEAD_V

LIN_CHUNK = 32

MEM_HEADS = 4
MEM_HEAD_DIM = 128
MEM_WIDTH = MEM_HEADS * MEM_HEAD_DIM

N_BRANCH = 5
BRANCH_WIDTH = 512

D_FF = 5632
N_EXPERTS = 8
MOE_TOP_K = 2
D_FF_EXPERT = 7168
MOE_BLOCK = 256
N_DENSE = (DEPTH + 1) // 2
N_MOE = DEPTH // 2

IN_WIDTHS = (
    NA_WIDTH, NA_WIDTH, NA_WIDTH,
    GLA_KEY_WIDTH, GLA_KEY_WIDTH, GLA_VAL_WIDTH,
    2 * GLA_GATE_RANK, GLA_VAL_WIDTH,
    2 * GDN_KEY_WIDTH + GDN_VAL_WIDTH,
    2 * GDN_HEADS, 2 * GDN_HEADS, GDN_VAL_WIDTH,
    HGRN_KEY_WIDTH, 2 * HGRN_KEY_WIDTH, HGRN_VAL_WIDTH, HGRN_VAL_WIDTH,
    MEM_WIDTH,
    N_BRANCH * D_MODEL,
)
P_IN = sum(IN_WIDTHS)

kernel_name = "hybrid_bidir_na_gla_gdn_hgrn2_mem_moe"


def rms_norm(x, gain, eps=RMS_EPS):
    xf = x.astype(F32)
    y = xf * lax.rsqrt(jnp.mean(xf * xf, axis=-1, keepdims=True) + eps)
    return (y * gain.astype(F32)).astype(x.dtype)


def l2_norm(x, eps=1e-6):
    xf = x.astype(F32)
    return xf * lax.rsqrt(jnp.sum(xf * xf, axis=-1, keepdims=True) + eps)


def _rev(t):
    return jnp.flip(t, axis=1)


def centred_depthwise_conv(x, w):
    width = w.shape[0]
    return lax.conv_general_dilated(
        x, w[:, None, :], window_strides=(1,), padding=[(width // 2, width // 2)],
        dimension_numbers=("NWC", "WIO", "NWC"), feature_group_count=x.shape[-1])


def chunk_gated_linear_attention(q, k, v, log_g):
    B, S, H, K = q.shape
    V = v.shape[-1]
    C = LIN_CHUNK
    n = S // C

    def chunks(t):
        return t.reshape(B, n, C, H, t.shape[-1]).transpose(1, 0, 3, 2, 4)

    q, k, v, log_g = chunks(q), chunks(k), chunks(v), chunks(log_g)
    b = jnp.cumsum(log_g, axis=-2)
    b_ref = b[..., C // 2:C // 2 + 1, :]
    incl = jnp.tril(jnp.ones((C, C), dtype=bool))
    scores = jnp.einsum("nbhik,nbhjk->nbhij", q * jnp.exp(b - b_ref), k * jnp.exp(b_ref - b))
    o_intra = jnp.einsum("nbhij,nbhjv->nbhiv", jnp.where(incl, scores, 0.0), v)
    q_dec = q * jnp.exp(b)
    k_dec = k * jnp.exp(b[..., -1:, :] - b)
    g_tot = jnp.exp(b[..., -1, :])

    def step(state, xs):
        q_c, k_c, v_c, g_c = xs
        o_c = jnp.einsum("bhik,bhkv->bhiv", q_c, state)
        state = state * g_c[..., None] + jnp.einsum("bhjk,bhjv->bhkv", k_c, v_c)
        return state, o_c

    _, o_inter = lax.scan(step, jnp.zeros((B, H, K, V), F32), (q_dec, k_dec, v, g_tot))
    o = o_intra + o_inter
    return o.transpose(1, 0, 3, 2, 4).reshape(B, S, H, V)


def chunk_gated_delta_rule(q, k, v, log_alpha, beta):
    B, S, H, K = q.shape
    V = v.shape[-1]
    C = GDN_CHUNK
    n = S // C

    def chunks(t):
        return t.reshape(B, n, C, H, t.shape[-1]).transpose(1, 0, 3, 2, 4)

    q, k, v = chunks(q), chunks(k), chunks(v)
    g = jnp.cumsum(chunks(log_alpha[..., None])[..., 0], axis=-1)
    beta = chunks(beta[..., None])
    incl = jnp.tril(jnp.ones((C, C), dtype=bool))
    strict = jnp.tril(jnp.ones((C, C), dtype=bool), -1)
    diff = g[..., :, None] - g[..., None, :]
    decay = jnp.where(incl, jnp.exp(jnp.where(incl, diff, 0.0)), 0.0)
    k_beta = k * beta
    a = jnp.where(strict, jnp.einsum("nbhik,nbhjk->nbhij", k_beta, k) * decay, 0.0)
    rhs = jnp.concatenate([v * beta, k_beta * jnp.exp(g)[..., None]], axis=-1)
    sol = lax.linalg.triangular_solve(a + jnp.eye(C, dtype=F32), rhs, left_side=True, lower=True)
    u, w = sol[..., :V], sol[..., V:]
    attn = jnp.einsum("nbhik,nbhjk->nbhij", q, k) * decay
    q_dec = q * jnp.exp(g)[..., None]
    k_dec = k * jnp.exp(g[..., -1:] - g)[..., None]
    g_tot = jnp.exp(g[..., -1])

    def step(state, xs):
        u_c, w_c, attn_c, q_c, k_c, g_c = xs
        v_new = u_c - jnp.einsum("bhck,bhkv->bhcv", w_c, state)
        o_c = jnp.einsum("bhck,bhkv->bhcv", q_c, state) + jnp.einsum("bhij,bhjv->bhiv", attn_c, v_new)
        state = state * g_c[..., None, None] + jnp.einsum("bhck,bhcv->bhkv", k_c, v_new)
        return state, o_c

    _, o = lax.scan(step, jnp.zeros((B, H, K, V), F32), (u, w, attn, q_dec, k_dec, g_tot))
    return o.transpose(1, 0, 3, 2, 4).reshape(B, S, H, V)


def neighbourhood_attention(q, k, v, q_gain, k_gain, rel_bias):
    B, S, _ = q.shape
    rows = S // GRID_W
    win_rows = min(NA_WIN_ROWS, rows)

    def grid(t):
        return t.reshape(B, rows, GRID_W, NA_HEADS, NA_HEAD_DIM)

    q = rms_norm(grid(q), q_gain).astype(F32) * (NA_HEAD_DIM ** -0.5)
    k = rms_norm(grid(k), k_gain).astype(F32)
    v = grid(v).astype(F32)
    r = jnp.arange(rows)
    c = jnp.arange(GRID_W)
    row_idx = jnp.clip(r - win_rows // 2, 0, rows - win_rows)[:, None] + jnp.arange(win_rows)[None, :]
    col_start = jnp.clip(c - NA_WIN_COLS // 2, 0, GRID_W - NA_WIN_COLS)
    col_in = (c[None, :] >= col_start[:, None]) & (c[None, :] < col_start[:, None] + NA_WIN_COLS)
    k_band = k[:, row_idx]
    v_band = v[:, row_idx]
    s = jnp.einsum("brqhd,brikhd->bhrqik", q, k_band)
    dr = row_idx - r[:, None] + (NA_WIN_ROWS - 1)
    dc = jnp.clip(c[None, :] - c[:, None], 1 - NA_WIN_COLS, NA_WIN_COLS - 1) + (NA_WIN_COLS - 1)
    bias = rel_bias.astype(F32)[:, dr[:, None, :, None], dc[None, :, None, :]]
    s = jnp.where(col_in[:, None, :], s + bias[None], MASK_VALUE)
    p = jax.nn.softmax(s, axis=(-2, -1))
    o = jnp.einsum("bhrqik,brikhd->brqhd", p, v_band)
    return o.reshape(B, S, NA_WIDTH)


def gla_branch(q, k, v, gate_lr, out_gate, w_gate_up, b_gate, norm_gain):
    B, S, _ = q.shape
    q = q.astype(F32).reshape(B, S, GLA_HEADS, GLA_HEAD_K) * (GLA_HEAD_K ** -0.5)
    k = k.astype(F32).reshape(B, S, GLA_HEADS, GLA_HEAD_K)
    v = v.astype(F32).reshape(B, S, GLA_HEADS, GLA_HEAD_V)
    lr = gate_lr.astype(F32).reshape(B, S, 2, GLA_GATE_RANK)
    gk = jnp.einsum("bsdr,drk->bsdk", lr, w_gate_up.astype(F32)) + b_gate.astype(F32)
    log_g = (jax.nn.log_sigmoid(gk) / GLA_GATE_NORMALIZER).reshape(B, S, 2, GLA_HEADS, GLA_HEAD_K)
    o = (chunk_gated_linear_attention(q, k, v, log_g[:, :, 0])
         + _rev(chunk_gated_linear_attention(_rev(q), _rev(k), _rev(v), _rev(log_g[:, :, 1]))))
    o = rms_norm(o, norm_gain) * jax.nn.silu(out_gate.astype(F32)).reshape(B, S, GLA_HEADS, GLA_HEAD_V)
    return o.reshape(B, S, GLA_VAL_WIDTH)


def gated_deltanet_branch(qkv, a, b, out_gate, conv_w, a_log, dt_bias, norm_gain):
    B, S, _ = qkv.shape
    qkv = jax.nn.silu(centred_depthwise_conv(qkv.astype(F32), conv_w.astype(F32)))
    q, k, v = jnp.split(qkv, [GDN_KEY_WIDTH, 2 * GDN_KEY_WIDTH], axis=-1)
    q = l2_norm(q.reshape(B, S, GDN_HEADS, GDN_HEAD_K)) * (GDN_HEAD_K ** -0.5)
    k = l2_norm(k.reshape(B, S, GDN_HEADS, GDN_HEAD_K))
    v = v.reshape(B, S, GDN_HEADS, GDN_HEAD_V)
    a = a.astype(F32).reshape(B, S, 2, GDN_HEADS)
    b = b.astype(F32).reshape(B, S, 2, GDN_HEADS)
    log_alpha = -jnp.exp(a_log.astype(F32)) * jax.nn.softplus(a + dt_bias.astype(F32))
    beta = jax.nn.sigmoid(b)
    o = (chunk_gated_delta_rule(q, k, v, log_alpha[:, :, 0], beta[:, :, 0])
         + _rev(chunk_gated_delta_rule(_rev(q), _rev(k), _rev(v), _rev(log_alpha[:, :, 1]), _rev(beta[:, :, 1]))))
    o = rms_norm(o, norm_gain) * jax.nn.silu(out_gate.astype(F32)).reshape(B, S, GDN_HEADS, GDN_HEAD_V)
    return o.reshape(B, S, GDN_VAL_WIDTH)


def hgrn2_branch(q, f_pre, i, out_gate, lower_bound, norm_gain):
    B, S, _ = q.shape
    q = jax.nn.silu(q.astype(F32)).reshape(B, S, HGRN_HEADS, HGRN_HEAD_K)
    z = f_pre.astype(F32).reshape(B, S, 2, HGRN_KEY_WIDTH)
    lb = lower_bound.astype(F32)
    log_f = jnp.logaddexp(jnp.log(jnp.maximum(lb, LB_FLOOR)), jnp.log1p(-lb) + jax.nn.log_sigmoid(z))
    k_in = (1.0 - lb) * jax.nn.sigmoid(-z)
    log_f = log_f.reshape(B, S, 2, HGRN_HEADS, HGRN_HEAD_K)
    k_in = k_in.reshape(B, S, 2, HGRN_HEADS, HGRN_HEAD_K)
    v = i.astype(F32).reshape(B, S, HGRN_HEADS, HGRN_HEAD_V)
    o = (chunk_gated_linear_attention(q, k_in[:, :, 0], v, log_f[:, :, 0])
         + _rev(chunk_gated_linear_attention(_rev(q), _rev(k_in[:, :, 1]), _rev(v), _rev(log_f[:, :, 1]))))
    o = rms_norm(o, norm_gain) * jax.nn.sigmoid(out_gate.astype(F32)).reshape(B, S, HGRN_HEADS, HGRN_HEAD_V)
    return o.reshape(B, S, HGRN_VAL_WIDTH)


def memory_cross_attention(q, mem, g_mem, w_mem_kv, q_gain, k_gain):
    B, S, _ = q.shape
    M = mem.shape[1]
    q = rms_norm(q.reshape(B, S, MEM_HEADS, MEM_HEAD_DIM), q_gain).astype(F32)
    kv = rms_norm(mem, g_mem) @ w_mem_kv
    k, v = jnp.split(kv, 2, axis=-1)
    k = rms_norm(k.reshape(B, M, MEM_HEADS, MEM_HEAD_DIM), k_gain).astype(F32)
    v = v.reshape(B, M, MEM_HEADS, MEM_HEAD_DIM).astype(F32)
    s = jnp.einsum("bshd,bmhd->bhsm", q, k) * (MEM_HEAD_DIM ** -0.5)
    p = jax.nn.softmax(s, axis=-1)
    o = jnp.einsum("bhsm,bmhd->bshd", p, v)
    return o.reshape(B, S, MEM_WIDTH)


def swiglu(h, w_gate, w_up, w_down):
    return (jax.nn.silu(h @ w_gate) * (h @ w_up)) @ w_down


def moe_swiglu(h, w_router, b_router, w_gate, w_up, w_down):
    N, D = h.shape
    E = w_router.shape[-1]
    logits = h.astype(F32) @ w_router.astype(F32) + b_router.astype(F32)
    top_logits, top_e = lax.top_k(logits, MOE_TOP_K)
    top_w = jax.nn.softmax(top_logits, axis=-1)
    nk = N * MOE_TOP_K
    n_blocks = -(-nk // MOE_BLOCK) + E
    flat_e = top_e.reshape(nk).astype(jnp.int32)
    flat_tok = jnp.repeat(jnp.arange(N, dtype=jnp.int32), MOE_TOP_K)
    order = jnp.argsort(flat_e)
    sorted_e = flat_e[order]
    counts = jnp.bincount(flat_e, length=E).astype(jnp.int32)
    padded = (counts + MOE_BLOCK - 1) // MOE_BLOCK * MOE_BLOCK
    start = jnp.cumsum(counts) - counts
    pad_end = jnp.cumsum(padded)
    pad_start = pad_end - padded
    slot_sorted = (pad_start[sorted_e] + jnp.arange(nk, dtype=jnp.int32) - start[sorted_e]).astype(jnp.int32)
    slot = jnp.zeros((nk,), jnp.int32).at[order].set(slot_sorted)
    slot_tok = jnp.full((n_blocks * MOE_BLOCK,), N, jnp.int32).at[slot].set(flat_tok)
    block_e = jnp.minimum(jnp.searchsorted(pad_end, jnp.arange(n_blocks, dtype=jnp.int32) * MOE_BLOCK, side="right"), E - 1)
    h_pad = jnp.concatenate([h, jnp.zeros((1, D), h.dtype)], axis=0)
    xb = h_pad[slot_tok].reshape(n_blocks, MOE_BLOCK, D)

    def expert_block(args):
        xs, e = args
        return (jax.nn.silu(xs @ w_gate[e]) * (xs @ w_up[e])) @ w_down[e]

    yb = lax.map(expert_block, (xb, block_e)).reshape(n_blocks * MOE_BLOCK, D)
    y = yb[slot].reshape(N, MOE_TOP_K, D)
    return jnp.einsum("nkd,nk->nd", y, top_w.astype(y.dtype))


def setup_inputs(seed: int = 0) -> dict:
    key = jax.random.key(seed)
    ks = jax.random.split(key, 32)

    def w(i, shape, fan_in):
        return jax.random.normal(ks[i], shape, F32) * (fan_in ** -0.5)

    def gain(i, shape):
        return 1.0 + 0.05 * jax.random.normal(ks[i], shape, F32)

    a_log = jnp.log(jax.random.uniform(ks[12], (DEPTH, 2, GDN_HEADS), F32, 1.0, 16.0))
    dt = jnp.exp(jax.random.uniform(ks[13], (DEPTH, 2, GDN_HEADS), F32, float(np.log(1e-3)), float(np.log(1e-1))))
    dt_bias = dt + jnp.log(-jnp.expm1(-dt))
    return {
        "x": jax.random.normal(ks[0], (BATCH, SEQ, D_MODEL), F32),
        "mem": jax.random.normal(ks[1], (BATCH, MEM_TOKENS, D_MODEL), F32),
        "g_mix": gain(2, (DEPTH, D_MODEL)),
        "w_in": w(3, (DEPTH, D_MODEL, P_IN), D_MODEL),
        "na_q_gain": gain(4, (DEPTH, NA_HEAD_DIM)),
        "na_k_gain": gain(5, (DEPTH, NA_HEAD_DIM)),
        "na_rel_bias": 0.1 * jax.random.normal(ks[6], (DEPTH, NA_HEADS, 2 * NA_WIN_ROWS - 1, 2 * NA_WIN_COLS - 1), F32),
        "gla_w_gate_up": w(7, (DEPTH, 2, GLA_GATE_RANK, GLA_KEY_WIDTH), GLA_GATE_RANK),
        "gla_b_gate": 0.1 * jax.random.normal(ks[8], (DEPTH, 2, GLA_KEY_WIDTH), F32),
        "gla_norm_gain": gain(9, (DEPTH, GLA_HEAD_V)),
        "gdn_conv_w": w(10, (DEPTH, GDN_CONV, 2 * GDN_KEY_WIDTH + GDN_VAL_WIDTH), GDN_CONV),
        "gdn_a_log": a_log,
        "gdn_dt_bias": dt_bias,
        "gdn_norm_gain": gain(11, (DEPTH, GDN_HEAD_V)),
        "hgrn_lb_raw": jax.random.normal(ks[14], (DEPTH, 2, HGRN_KEY_WIDTH), F32),
        "hgrn_norm_gain": gain(15, (DEPTH, HGRN_HEAD_V)),
        "g_mem": gain(16, (DEPTH, D_MODEL)),
        "w_mem_kv": w(17, (DEPTH, D_MODEL, 2 * MEM_WIDTH), D_MODEL),
        "mem_q_gain": gain(18, (DEPTH, MEM_HEAD_DIM)),
        "mem_k_gain": gain(19, (DEPTH, MEM_HEAD_DIM)),
        "w_branch": w(20, (DEPTH, N_BRANCH, BRANCH_WIDTH, D_MODEL), BRANCH_WIDTH),
        "w_out": w(21, (DEPTH, D_MODEL, D_MODEL), D_MODEL),
        "g_ffn": gain(22, (DEPTH, D_MODEL)),
        "ffn_w_gate": w(23, (N_DENSE, D_MODEL, D_FF), D_MODEL),
        "ffn_w_up": w(24, (N_DENSE, D_MODEL, D_FF), D_MODEL),
        "ffn_w_down": w(25, (N_DENSE, D_FF, D_MODEL), D_FF),
        "moe_w_router": w(26, (N_MOE, D_MODEL, N_EXPERTS), D_MODEL),
        "moe_b_router": 0.01 * jax.random.normal(ks[27], (N_MOE, N_EXPERTS), F32),
        "moe_w_gate": w(28, (N_MOE, N_EXPERTS, D_MODEL, D_FF_EXPERT), D_MODEL),
        "moe_w_up": w(29, (N_MOE, N_EXPERTS, D_MODEL, D_FF_EXPERT), D_MODEL),
        "moe_w_down": w(30, (N_MOE, N_EXPERTS, D_FF_EXPERT, D_MODEL), D_FF_EXPERT),
    }


def reference(x, mem, g_mix, w_in, na_q_gain, na_k_gain, na_rel_bias, gla_w_gate_up, gla_b_gate,
              gla_norm_gain, gdn_conv_w, gdn_a_log, gdn_dt_bias, gdn_norm_gain, hgrn_lb_raw,
              hgrn_norm_gain, g_mem, w_mem_kv, mem_q_gain, mem_k_gain, w_branch, w_out, g_ffn,
              ffn_w_gate, ffn_w_up, ffn_w_down, moe_w_router, moe_b_router, moe_w_gate, moe_w_up,
              moe_w_down):
    B, S, D = x.shape
    split_at = np.cumsum(IN_WIDTHS)[:-1].tolist()
    lb_w = jax.nn.softmax(hgrn_lb_raw.astype(F32), axis=0)
    hgrn_lb = jnp.cumsum(lb_w, axis=0) - lb_w[0:1]
    for layer in range(DEPTH):
        h = rms_norm(x, g_mix[layer])
        (na_q, na_k, na_v, gla_q, gla_k, gla_v, gla_lr, gla_og, gdn_qkv, gdn_a, gdn_b, gdn_og,
         hg_q, hg_f, hg_i, hg_og, mem_q, gate_logits) = jnp.split(h @ w_in[layer], split_at, axis=-1)
        branches = (
            neighbourhood_attention(na_q, na_k, na_v, na_q_gain[layer], na_k_gain[layer], na_rel_bias[layer]),
            gla_branch(gla_q, gla_k, gla_v, gla_lr, gla_og, gla_w_gate_up[layer], gla_b_gate[layer], gla_norm_gain[layer]),
            gated_deltanet_branch(gdn_qkv, gdn_a, gdn_b, gdn_og, gdn_conv_w[layer], gdn_a_log[layer],
                                  gdn_dt_bias[layer], gdn_norm_gain[layer]),
            hgrn2_branch(hg_q, hg_f, hg_i, hg_og, hgrn_lb[layer], hgrn_norm_gain[layer]),
            memory_cross_attention(mem_q, mem, g_mem[layer], w_mem_kv[layer], mem_q_gain[layer], mem_k_gain[layer]),
        )
        gates = jax.nn.sigmoid(gate_logits.astype(F32)).reshape(B, S, N_BRANCH, D).astype(x.dtype)
        merged = jnp.zeros_like(x)
        for n in range(N_BRANCH):
            merged = merged + gates[:, :, n] * (branches[n].astype(x.dtype) @ w_branch[layer, n])
        x = x + merged @ w_out[layer]

        h = rms_norm(x, g_ffn[layer])
        if layer % 2 == 0:
            j = layer // 2
            f = swiglu(h, ffn_w_gate[j], ffn_w_up[j], ffn_w_down[j])
        else:
            j = layer // 2
            f = moe_swiglu(h.reshape(B * S, D), moe_w_router[j], moe_b_router[j], moe_w_gate[j],
                           moe_w_up[j], moe_w_down[j]).reshape(B, S, D)
        x = x + f
    return x
```

```python
import functools

import jax
import jax.numpy as jnp
import numpy as np
from jax import lax
from jax.experimental import pallas as pl
from jax.experimental.pallas import tpu as pltpu

F32 = jnp.float32
BF16 = jnp.bfloat16

D_MODEL = 2048
DEPTH = 2
RMS_EPS = 1e-6
MASK_VALUE = -1e30
LB_FLOOR = 1e-30
GRID_W = 64

NA_HEADS = 8
NA_HEAD_DIM = 64
NA_WIDTH = 512
NA_WIN_ROWS = 8
NA_WIN_COLS = 16

GLA_HEADS = 4
GLA_HEAD_K = 64
GLA_HEAD_V = 128
GLA_KEY_WIDTH = 256
GLA_VAL_WIDTH = 512
GLA_GATE_RANK = 16
GLA_GATE_NORMALIZER = 16.0

GDN_HEADS = 4
GDN_HEAD_K = 128
GDN_HEAD_V = 128
GDN_KEY_WIDTH = 512
GDN_VAL_WIDTH = 512
GDN_CHUNK = 64

HGRN_HEADS = 4
HGRN_HEAD_K = 128
HGRN_HEAD_V = 128
HGRN_KEY_WIDTH = 512
HGRN_VAL_WIDTH = 512

LIN_CHUNK = 32

MEM_HEADS = 4
MEM_HEAD_DIM = 128
MEM_WIDTH = 512

N_BRANCH = 5
BRANCH_WIDTH = 512
N_EXPERTS = 8
MOE_TOP_K = 2

IN_WIDTHS = (
    NA_WIDTH, NA_WIDTH, NA_WIDTH,
    GLA_KEY_WIDTH, GLA_KEY_WIDTH, GLA_VAL_WIDTH,
    2 * GLA_GATE_RANK, GLA_VAL_WIDTH,
    2 * GDN_KEY_WIDTH + GDN_VAL_WIDTH,
    2 * GDN_HEADS, 2 * GDN_HEADS, GDN_VAL_WIDTH,
    HGRN_KEY_WIDTH, 2 * HGRN_KEY_WIDTH, HGRN_VAL_WIDTH, HGRN_VAL_WIDTH,
    MEM_WIDTH,
    N_BRANCH * D_MODEL,
)
P_IN = sum(IN_WIDTHS)

V7X_VMEM_BYTES = 64 * 1024 * 1024
VMEM_LIMIT_BYTES = V7X_VMEM_BYTES - 8 * 1024 * 1024
LANES = 128
MOE_TILE = 512


def _params(*semantics):
    return pltpu.CompilerParams(dimension_semantics=semantics, vmem_limit_bytes=VMEM_LIMIT_BYTES)


def _rms_norm_rows(x, gain):
    ms = jnp.mean(x * x, axis=-1, keepdims=True)
    return x * lax.rsqrt(ms + RMS_EPS) * gain


def _rms_matmul_kernel(x_ref, g_ref, w_ref, o_ref, h_ref):
    @pl.when(pl.program_id(1) == 0)
    def _():
        h_ref[...] = _rms_norm_rows(x_ref[...], g_ref[...]).astype(BF16)

    o_ref[...] = jnp.dot(h_ref[...], w_ref[...], preferred_element_type=F32).astype(o_ref.dtype)


def rms_matmul(x, gain, w, *, tm, tn, out_dtype=F32):
    m, k = x.shape
    n = w.shape[1]
    return pl.pallas_call(
        _rms_matmul_kernel,
        out_shape=jax.ShapeDtypeStruct((m, n), out_dtype),
        grid=(m // tm, n // tn),
        in_specs=[
            pl.BlockSpec((tm, k), lambda i, j: (i, 0)),
            pl.BlockSpec((1, k), lambda i, j: (0, 0)),
            pl.BlockSpec((k, tn), lambda i, j: (0, j)),
        ],
        out_specs=pl.BlockSpec((tm, tn), lambda i, j: (i, j)),
        scratch_shapes=[pltpu.VMEM((tm, k), BF16)],
        compiler_params=_params("parallel", "arbitrary"),
        name="rms_matmul",
    )(x, gain.reshape(1, k), w)


def _rms_swiglu_kernel(x_ref, g_ref, wg_ref, wu_ref, o_ref, h_ref):
    @pl.when(pl.program_id(1) == 0)
    def _():
        h_ref[...] = _rms_norm_rows(x_ref[...], g_ref[...]).astype(BF16)

    h = h_ref[...]
    a = jnp.dot(h, wg_ref[...], preferred_element_type=F32)
    b = jnp.dot(h, wu_ref[...], preferred_element_type=F32)
    o_ref[...] = (a * jax.nn.sigmoid(a) * b).astype(o_ref.dtype)


def rms_swiglu_up(x, gain, wg, wu, *, tm, tn):
    m, k = x.shape
    n = wg.shape[1]
    return pl.pallas_call(
        _rms_swiglu_kernel,
        out_shape=jax.ShapeDtypeStruct((m, n), BF16),
        grid=(m // tm, n // tn),
        in_specs=[
            pl.BlockSpec((tm, k), lambda i, j: (i, 0)),
            pl.BlockSpec((1, k), lambda i, j: (0, 0)),
            pl.BlockSpec((k, tn), lambda i, j: (0, j)),
            pl.BlockSpec((k, tn), lambda i, j: (0, j)),
        ],
        out_specs=pl.BlockSpec((tm, tn), lambda i, j: (i, j)),
        scratch_shapes=[pltpu.VMEM((tm, k), BF16)],
        compiler_params=_params("parallel", "arbitrary"),
        name="rms_swiglu_up",
    )(x, gain.reshape(1, k), wg, wu)


def _matmul_residual_kernel(a_ref, w_ref, r_ref, o_ref):
    o_ref[...] = r_ref[...] + jnp.dot(a_ref[...], w_ref[...], preferred_element_type=F32)


def matmul_residual(a, w, res, *, tm, tn):
    m, k = a.shape
    n = w.shape[1]
    return pl.pallas_call(
        _matmul_residual_kernel,
        out_shape=jax.ShapeDtypeStruct((m, n), F32),
        grid=(m // tm, n // tn),
        in_specs=[
            pl.BlockSpec((tm, k), lambda i, j: (i, 0)),
            pl.BlockSpec((k, tn), lambda i, j: (0, j)),
            pl.BlockSpec((tm, tn), lambda i, j: (i, j)),
        ],
        out_specs=pl.BlockSpec((tm, tn), lambda i, j: (i, j)),
        compiler_params=_params("parallel", "arbitrary"),
        name="matmul_residual",
    )(a, w, res)


def _merge_kernel(*refs):
    br_refs = refs[:N_BRANCH]
    gl_refs = refs[N_BRANCH:2 * N_BRANCH]
    wb_ref = refs[2 * N_BRANCH]
    o_ref = refs[2 * N_BRANCH + 1]
    acc = None
    for n in range(N_BRANCH):
        y = jnp.dot(br_refs[n][...], wb_ref[n], preferred_element_type=F32)
        t = jax.nn.sigmoid(gl_refs[n][...].astype(F32)) * y
        acc = t if acc is None else acc + t
    o_ref[...] = acc.astype(o_ref.dtype)


def merge_branches(branches, gate_logits, gate_col0, w_branch, *, tm, tn):
    m = branches[0].shape[0]
    d = D_MODEL
    tiles_per_branch = d // tn
    tile0 = gate_col0 // tn
    in_specs = [pl.BlockSpec((tm, BRANCH_WIDTH), lambda i, j: (i, 0)) for _ in range(N_BRANCH)]
    in_specs += [
        pl.BlockSpec((tm, tn), functools.partial(lambda i, j, n: (i, tile0 + n * tiles_per_branch + j), n=n))
        for n in range(N_BRANCH)
    ]
    in_specs += [pl.BlockSpec((N_BRANCH, BRANCH_WIDTH, tn), lambda i, j: (0, 0, j))]
    return pl.pallas_call(
        _merge_kernel,
        out_shape=jax.ShapeDtypeStruct((m, d), BF16),
        grid=(m // tm, d // tn),
        in_specs=in_specs,
        out_specs=pl.BlockSpec((tm, tn), lambda i, j: (i, j)),
        compiler_params=_params("parallel", "arbitrary"),
        name="merge_branches",
    )(*branches, *([gate_logits] * N_BRANCH), w_branch)


def _router_kernel(x_ref, g_ref, w_ref, b_ref, o_ref):
    h = _rms_norm_rows(x_ref[...], g_ref[...])
    o_ref[...] = jnp.dot(h, w_ref[...], preferred_element_type=F32,
                         precision=lax.Precision.HIGHEST) + b_ref[...]


def router_logits(x, gain, w_router, b_router, *, tm):
    m, k = x.shape
    e = w_router.shape[1]
    w_pad = jnp.zeros((k, LANES), F32).at[:, :e].set(w_router.astype(F32))
    b_pad = jnp.zeros((1, LANES), F32).at[0, :e].set(b_router.astype(F32))
    out = pl.pallas_call(
        _router_kernel,
        out_shape=jax.ShapeDtypeStruct((m, LANES), F32),
        grid=(m // tm,),
        in_specs=[
            pl.BlockSpec((tm, k), lambda i: (i, 0)),
            pl.BlockSpec((1, k), lambda i: (0, 0)),
            pl.BlockSpec((k, LANES), lambda i: (0, 0)),
            pl.BlockSpec((1, LANES), lambda i: (0, 0)),
        ],
        out_specs=pl.BlockSpec((tm, LANES), lambda i: (i, 0)),
        compiler_params=_params("parallel"),
        name="router_logits",
    )(x, gain.reshape(1, k), w_pad, b_pad)
    return out[:, :e]


def _moe_kernel(tile_e_ref, n_used_ref, x_ref, wg_ref, wu_ref, wd_ref, o_ref, acc_ref):
    i = pl.program_id(0)
    j = pl.program_id(1)
    last = pl.num_programs(1) - 1
    used = i < n_used_ref[0]

    @pl.when(jnp.logical_and(used, j == 0))
    def _():
        acc_ref[...] = jnp.zeros_like(acc_ref)

    @pl.when(used)
    def _():
        x = x_ref[...]
        a = jnp.dot(x, wg_ref[0], preferred_element_type=F32)
        b = jnp.dot(x, wu_ref[0], preferred_element_type=F32)
        act = (a * jax.nn.sigmoid(a) * b).astype(BF16)
        acc_ref[...] += jnp.dot(act, wd_ref[0], preferred_element_type=F32)

    @pl.when(jnp.logical_and(used, j == last))
    def _():
        o_ref[...] = acc_ref[...]

    @pl.when(jnp.logical_and(jnp.logical_not(used), j == last))
    def _():
        o_ref[...] = jnp.zeros_like(o_ref)


def moe_experts(xb, tile_e, n_used, wg, wu, wd, *, tm, tf):
    rows, d = xb.shape
    ff = wg.shape[2]
    n_tiles = rows // tm

    def x_map(i, j, te, nu):
        return (jnp.minimum(i, nu[0] - 1), 0)

    def up_map(i, j, te, nu):
        keep = i < nu[0]
        return (te[i], 0, jnp.where(keep, j, ff // tf - 1))

    def down_map(i, j, te, nu):
        keep = i < nu[0]
        return (te[i], jnp.where(keep, j, ff // tf - 1), 0)

    grid_spec = pltpu.PrefetchScalarGridSpec(
        num_scalar_prefetch=2,
        grid=(n_tiles, ff // tf),
        in_specs=[
            pl.BlockSpec((tm, d), x_map),
            pl.BlockSpec((1, d, tf), up_map),
            pl.BlockSpec((1, d, tf), up_map),
            pl.BlockSpec((1, tf, d), down_map),
        ],
        out_specs=pl.BlockSpec((tm, d), lambda i, j, te, nu: (i, 0)),
        scratch_shapes=[pltpu.VMEM((tm, d), F32)],
    )
    return pl.pallas_call(
        _moe_kernel,
        out_shape=jax.ShapeDtypeStruct((rows, d), F32),
        grid_spec=grid_spec,
        compiler_params=_params("arbitrary", "arbitrary"),
        name="moe_experts",
    )(tile_e, n_used, xb, wg, wu, wd)


def moe_layer(x2d, gain, w_router, b_router, wg, wu, wd):
    n, d = x2d.shape
    e = N_EXPERTS
    logits = router_logits(x2d, gain, w_router, b_router, tm=1024)
    top_logits, top_e = lax.top_k(logits, MOE_TOP_K)
    top_w = jax.nn.softmax(top_logits, axis=-1)
    nk = n * MOE_TOP_K
    n_tiles = -(-nk // MOE_TILE) + e
    flat_e = top_e.reshape(nk).astype(jnp.int32)
    flat_tok = jnp.repeat(jnp.arange(n, dtype=jnp.int32), MOE_TOP_K)
    onehot = (flat_e[:, None] == jnp.arange(e, dtype=jnp.int32)[None, :]).astype(jnp.int32)
    rank = jnp.take_along_axis(jnp.cumsum(onehot, axis=0), flat_e[:, None], axis=1)[:, 0] - 1
    counts = jnp.sum(onehot, axis=0)
    padded = (counts + MOE_TILE - 1) // MOE_TILE * MOE_TILE
    pad_end = jnp.cumsum(padded)
    pad_start = pad_end - padded
    slot = (pad_start[flat_e] + rank).astype(jnp.int32)
    slot_tok = jnp.full((n_tiles * MOE_TILE,), n, jnp.int32).at[slot].set(flat_tok)
    tile_start = jnp.arange(n_tiles, dtype=jnp.int32) * MOE_TILE
    tile_e = jnp.minimum(jnp.searchsorted(pad_end, tile_start, side="right"), e - 1).astype(jnp.int32)
    n_used = (pad_end[-1] // MOE_TILE).astype(jnp.int32).reshape(1)
    tile_e = jnp.where(tile_start < pad_end[-1], tile_e, tile_e[jnp.maximum(n_used[0] - 1, 0)])

    h = rms_only(x2d, gain)
    h_pad = jnp.concatenate([h, jnp.zeros((1, d), h.dtype)], axis=0)
    xb = h_pad[slot_tok]
    yb = moe_experts(xb, tile_e, n_used, wg, wu, wd, tm=MOE_TILE, tf=512)
    y = yb[slot].reshape(n, MOE_TOP_K, d)
    return x2d + jnp.einsum("nkd,nk->nd", y, top_w.astype(y.dtype))


def _rms_kernel(x_ref, g_ref, o_ref):
    o_ref[...] = _rms_norm_rows(x_ref[...], g_ref[...]).astype(o_ref.dtype)


def rms_only(x, gain, *, tm=1024):
    m, k = x.shape
    return pl.pallas_call(
        _rms_kernel,
        out_shape=jax.ShapeDtypeStruct((m, k), BF16),
        grid=(m // tm,),
        in_specs=[pl.BlockSpec((tm, k), lambda i: (i, 0)), pl.BlockSpec((1, k), lambda i: (0, 0))],
        out_specs=pl.BlockSpec((tm, k), lambda i: (i, 0)),
        compiler_params=_params("parallel"),
        name="rms_only",
    )(x, gain.reshape(1, k))


def _rms_norm(x, gain, eps=RMS_EPS):
    xf = x.astype(F32)
    y = xf * lax.rsqrt(jnp.mean(xf * xf, axis=-1, keepdims=True) + eps)
    return (y * gain.astype(F32)).astype(x.dtype)


def _l2_norm(x, eps=1e-6):
    xf = x.astype(F32)
    return xf * lax.rsqrt(jnp.sum(xf * xf, axis=-1, keepdims=True) + eps)


def _rev(t):
    return jnp.flip(t, axis=1)


def _centred_depthwise_conv(x, w):
    width = w.shape[0]
    return lax.conv_general_dilated(
        x, w[:, None, :], window_strides=(1,), padding=[(width // 2, width // 2)],
        dimension_numbers=("NWC", "WIO", "NWC"), feature_group_count=x.shape[-1])


def _chunk_gla(q, k, v, log_g):
    B, S, H, K = q.shape
    V = v.shape[-1]
    C = LIN_CHUNK
    n = S // C

    def chunks(t):
        return t.reshape(B, n, C, H, t.shape[-1]).transpose(1, 0, 3, 2, 4)

    q, k, v, log_g = chunks(q), chunks(k), chunks(v), chunks(log_g)
    b = jnp.cumsum(log_g, axis=-2)
    b_ref = b[..., C // 2:C // 2 + 1, :]
    incl = jnp.tril(jnp.ones((C, C), dtype=bool))
    scores = jnp.einsum("nbhik,nbhjk->nbhij", q * jnp.exp(b - b_ref), k * jnp.exp(b_ref - b))
    o_intra = jnp.einsum("nbhij,nbhjv->nbhiv", jnp.where(incl, scores, 0.0), v)
    q_dec = q * jnp.exp(b)
    k_dec = k * jnp.exp(b[..., -1:, :] - b)
    g_tot = jnp.exp(b[..., -1, :])

    def step(state, xs):
        q_c, k_c, v_c, g_c = xs
        o_c = jnp.einsum("bhik,bhkv->bhiv", q_c, state)
        state = state * g_c[..., None] + jnp.einsum("bhjk,bhjv->bhkv", k_c, v_c)
        return state, o_c

    _, o_inter = lax.scan(step, jnp.zeros((B, H, K, V), F32), (q_dec, k_dec, v, g_tot))
    o = o_intra + o_inter
    return o.transpose(1, 0, 3, 2, 4).reshape(B, S, H, V)


def _chunk_gdn(q, k, v, log_alpha, beta):
    B, S, H, K = q.shape
    V = v.shape[-1]
    C = GDN_CHUNK
    n = S // C

    def chunks(t):
        return t.reshape(B, n, C, H, t.shape[-1]).transpose(1, 0, 3, 2, 4)

    q, k, v = chunks(q), chunks(k), chunks(v)
    g = jnp.cumsum(chunks(log_alpha[..., None])[..., 0], axis=-1)
    beta = chunks(beta[..., None])
    incl = jnp.tril(jnp.ones((C, C), dtype=bool))
    strict = jnp.tril(jnp.ones((C, C), dtype=bool), -1)
    diff = g[..., :, None] - g[..., None, :]
    decay = jnp.where(incl, jnp.exp(jnp.where(incl, diff, 0.0)), 0.0)
    k_beta = k * beta
    a = jnp.where(strict, jnp.einsum("nbhik,nbhjk->nbhij", k_beta, k) * decay, 0.0)
    rhs = jnp.concatenate([v * beta, k_beta * jnp.exp(g)[..., None]], axis=-1)
    sol = lax.linalg.triangular_solve(a + jnp.eye(C, dtype=F32), rhs, left_side=True, lower=True)
    u, w = sol[..., :V], sol[..., V:]
    attn = jnp.einsum("nbhik,nbhjk->nbhij", q, k) * decay
    q_dec = q * jnp.exp(g)[..., None]
    k_dec = k * jnp.exp(g[..., -1:] - g)[..., None]
    g_tot = jnp.exp(g[..., -1])

    def step(state, xs):
        u_c, w_c, attn_c, q_c, k_c, g_c = xs
        v_new = u_c - jnp.einsum("bhck,bhkv->bhcv", w_c, state)
        o_c = jnp.einsum("bhck,bhkv->bhcv", q_c, state) + jnp.einsum("bhij,bhjv->bhiv", attn_c, v_new)
        state = state * g_c[..., None, None] + jnp.einsum("bhck,bhcv->bhkv", k_c, v_new)
        return state, o_c

    _, o = lax.scan(step, jnp.zeros((B, H, K, V), F32), (u, w, attn, q_dec, k_dec, g_tot))
    return o.transpose(1, 0, 3, 2, 4).reshape(B, S, H, V)


def _neighbourhood_attention(q, k, v, q_gain, k_gain, rel_bias):
    B, S, _ = q.shape
    rows = S // GRID_W
    win_rows = min(NA_WIN_ROWS, rows)

    def grid(t):
        return t.reshape(B, rows, GRID_W, NA_HEADS, NA_HEAD_DIM)

    q = _rms_norm(grid(q), q_gain).astype(F32) * (NA_HEAD_DIM ** -0.5)
    k = _rms_norm(grid(k), k_gain).astype(F32)
    v = grid(v).astype(F32)
    r = jnp.arange(rows)
    c = jnp.arange(GRID_W)
    row_idx = jnp.clip(r - win_rows // 2, 0, rows - win_rows)[:, None] + jnp.arange(win_rows)[None, :]
    col_start = jnp.clip(c - NA_WIN_COLS // 2, 0, GRID_W - NA_WIN_COLS)
    col_in = (c[None, :] >= col_start[:, None]) & (c[None, :] < col_start[:, None] + NA_WIN_COLS)
    k_band = k[:, row_idx]
    v_band = v[:, row_idx]
    s = jnp.einsum("brqhd,brikhd->bhrqik", q, k_band)
    dr = row_idx - r[:, None] + (NA_WIN_ROWS - 1)
    dc = jnp.clip(c[None, :] - c[:, None], 1 - NA_WIN_COLS, NA_WIN_COLS - 1) + (NA_WIN_COLS - 1)
    bias = rel_bias.astype(F32)[:, dr[:, None, :, None], dc[None, :, None, :]]
    s = jnp.where(col_in[:, None, :], s + bias[None], MASK_VALUE)
    p = jax.nn.softmax(s, axis=(-2, -1))
    o = jnp.einsum("bhrqik,brikhd->brqhd", p, v_band)
    return o.reshape(B, S, NA_WIDTH)


def _gla_branch(q, k, v, gate_lr, out_gate, w_gate_up, b_gate, norm_gain):
    B, S, _ = q.shape
    q = q.astype(F32).reshape(B, S, GLA_HEADS, GLA_HEAD_K) * (GLA_HEAD_K ** -0.5)
    k = k.astype(F32).reshape(B, S, GLA_HEADS, GLA_HEAD_K)
    v = v.astype(F32).reshape(B, S, GLA_HEADS, GLA_HEAD_V)
    lr = gate_lr.astype(F32).reshape(B, S, 2, GLA_GATE_RANK)
    gk = jnp.einsum("bsdr,drk->bsdk", lr, w_gate_up.astype(F32)) + b_gate.astype(F32)
    log_g = (jax.nn.log_sigmoid(gk) / GLA_GATE_NORMALIZER).reshape(B, S, 2, GLA_HEADS, GLA_HEAD_K)
    o = (_chunk_gla(q, k, v, log_g[:, :, 0])
         + _rev(_chunk_gla(_rev(q), _rev(k), _rev(v), _rev(log_g[:, :, 1]))))
    o = _rms_norm(o, norm_gain) * jax.nn.silu(out_gate.astype(F32)).reshape(B, S, GLA_HEADS, GLA_HEAD_V)
    return o.reshape(B, S, GLA_VAL_WIDTH)


def _gdn_branch(qkv, a, b, out_gate, conv_w, a_log, dt_bias, norm_gain):
    B, S, _ = qkv.shape
    qkv = jax.nn.silu(_centred_depthwise_conv(qkv.astype(F32), conv_w.astype(F32)))
    q, k, v = jnp.split(qkv, [GDN_KEY_WIDTH, 2 * GDN_KEY_WIDTH], axis=-1)
    q = _l2_norm(q.reshape(B, S, GDN_HEADS, GDN_HEAD_K)) * (GDN_HEAD_K ** -0.5)
    k = _l2_norm(k.reshape(B, S, GDN_HEADS, GDN_HEAD_K))
    v = v.reshape(B, S, GDN_HEADS, GDN_HEAD_V)
    a = a.astype(F32).reshape(B, S, 2, GDN_HEADS)
    b = b.astype(F32).reshape(B, S, 2, GDN_HEADS)
    log_alpha = -jnp.exp(a_log.astype(F32)) * jax.nn.softplus(a + dt_bias.astype(F32))
    beta = jax.nn.sigmoid(b)
    o = (_chunk_gdn(q, k, v, log_alpha[:, :, 0], beta[:, :, 0])
         + _rev(_chunk_gdn(_rev(q), _rev(k), _rev(v), _rev(log_alpha[:, :, 1]), _rev(beta[:, :, 1]))))
    o = _rms_norm(o, norm_gain) * jax.nn.silu(out_gate.astype(F32)).reshape(B, S, GDN_HEADS, GDN_HEAD_V)
    return o.reshape(B, S, GDN_VAL_WIDTH)


def _hgrn2_branch(q, f_pre, i, out_gate, lower_bound, norm_gain):
    B, S, _ = q.shape
    q = jax.nn.silu(q.astype(F32)).reshape(B, S, HGRN_HEADS, HGRN_HEAD_K)
    z = f_pre.astype(F32).reshape(B, S, 2, HGRN_KEY_WIDTH)
    lb = lower_bound.astype(F32)
    log_f = jnp.logaddexp(jnp.log(jnp.maximum(lb, LB_FLOOR)), jnp.log1p(-lb) + jax.nn.log_sigmoid(z))
    k_in = (1.0 - lb) * jax.nn.sigmoid(-z)
    log_f = log_f.reshape(B, S, 2, HGRN_HEADS, HGRN_HEAD_K)
    k_in = k_in.reshape(B, S, 2, HGRN_HEADS, HGRN_HEAD_K)
    v = i.astype(F32).reshape(B, S, HGRN_HEADS, HGRN_HEAD_V)
    o = (_chunk_gla(q, k_in[:, :, 0], v, log_f[:, :, 0])
         + _rev(_chunk_gla(_rev(q), _rev(k_in[:, :, 1]), _rev(v), _rev(log_f[:, :, 1]))))
    o = _rms_norm(o, norm_gain) * jax.nn.sigmoid(out_gate.astype(F32)).reshape(B, S, HGRN_HEADS, HGRN_HEAD_V)
    return o.reshape(B, S, HGRN_VAL_WIDTH)


def _memory_cross_attention(q, kv, q_gain, k_gain):
    B, S, _ = q.shape
    M = kv.shape[1]
    q = _rms_norm(q.reshape(B, S, MEM_HEADS, MEM_HEAD_DIM), q_gain).astype(F32)
    k, v = jnp.split(kv, 2, axis=-1)
    k = _rms_norm(k.reshape(B, M, MEM_HEADS, MEM_HEAD_DIM), k_gain).astype(F32)
    v = v.reshape(B, M, MEM_HEADS, MEM_HEAD_DIM).astype(F32)
    s = jnp.einsum("bshd,bmhd->bhsm", q, k) * (MEM_HEAD_DIM ** -0.5)
    p = jax.nn.softmax(s, axis=-1)
    o = jnp.einsum("bhsm,bmhd->bshd", p, v)
    return o.reshape(B, S, MEM_WIDTH)


def kernel(x, mem, g_mix, w_in, na_q_gain, na_k_gain, na_rel_bias, gla_w_gate_up, gla_b_gate, gla_norm_gain, gdn_conv_w, gdn_a_log, gdn_dt_bias, gdn_norm_gain, hgrn_lb_raw, hgrn_norm_gain, g_mem, w_mem_kv, mem_q_gain, mem_k_gain, w_branch, w_out, g_ffn, ffn_w_gate, ffn_w_up, ffn_w_down, moe_w_router, moe_b_router, moe_w_gate, moe_w_up, moe_w_down):
    B, S, D = x.shape
    n_tok = B * S
    split_at = np.cumsum(IN_WIDTHS)[:-1].tolist()
    gate_col0 = P_IN - N_BRANCH * D
    p_pad = -(-P_IN // 512) * 512
    lb_w = jax.nn.softmax(hgrn_lb_raw.astype(F32), axis=0)
    hgrn_lb = jnp.cumsum(lb_w, axis=0) - lb_w[0:1]
    x2 = x.reshape(n_tok, D)
    mem2 = mem.reshape(B * mem.shape[1], D)
    for layer in range(DEPTH):
        w_in_l = jnp.pad(w_in[layer].astype(BF16), ((0, 0), (0, p_pad - P_IN)))
        proj = rms_matmul(x2, g_mix[layer], w_in_l, tm=1024, tn=512, out_dtype=F32)
        proj3 = proj[:, :P_IN].reshape(B, S, P_IN)
        (na_q, na_k, na_v, gla_q, gla_k, gla_v, gla_lr, gla_og, gdn_qkv, gdn_a, gdn_b, gdn_og,
         hg_q, hg_f, hg_i, hg_og, mem_q, _) = jnp.split(proj3, split_at, axis=-1)
        kv = rms_matmul(mem2, g_mem[layer], w_mem_kv[layer].astype(BF16), tm=1024, tn=512,
                        out_dtype=F32).reshape(B, -1, 2 * MEM_WIDTH)
        branches = (
            _neighbourhood_attention(na_q, na_k, na_v, na_q_gain[layer], na_k_gain[layer], na_rel_bias[layer]),
            _gla_branch(gla_q, gla_k, gla_v, gla_lr, gla_og, gla_w_gate_up[layer], gla_b_gate[layer],
                        gla_norm_gain[layer]),
            _gdn_branch(gdn_qkv, gdn_a, gdn_b, gdn_og, gdn_conv_w[layer], gdn_a_log[layer],
                        gdn_dt_bias[layer], gdn_norm_gain[layer]),
            _hgrn2_branch(hg_q, hg_f, hg_i, hg_og, hgrn_lb[layer], hgrn_norm_gain[layer]),
            _memory_cross_attention(mem_q, kv, mem_q_gain[layer], mem_k_gain[layer]),
        )
        branches = [b.reshape(n_tok, BRANCH_WIDTH).astype(BF16) for b in branches]
        gates_in = proj[:, gate_col0:gate_col0 + N_BRANCH * D]
        merged = merge_branches(branches, gates_in, 0, w_branch[layer].astype(BF16), tm=1024, tn=512)
        x2 = matmul_residual(merged, w_out[layer].astype(BF16), x2, tm=1024, tn=512)

        j = layer // 2
        if layer % 2 == 0:
            act = rms_swiglu_up(x2, g_ffn[layer], ffn_w_gate[j].astype(BF16), ffn_w_up[j].astype(BF16),
                                tm=1024, tn=512)
            x2 = matmul_residual(act, ffn_w_down[j].astype(BF16), x2, tm=512, tn=512)
        else:
            x2 = moe_layer(x2, g_ffn[layer], moe_w_router[j], moe_b_router[j], moe_w_gate[j].astype(BF16),
                           moe_w_up[j].astype(BF16), moe_w_down[j].astype(BF16))
    return x2.reshape(B, S, D)
```

```python
import functools

import jax
import jax.numpy as jnp
import numpy as np
from jax import lax
from jax.experimental import pallas as pl
from jax.experimental.pallas import tpu as pltpu

F32 = jnp.float32
BF16 = jnp.bfloat16

D_MODEL = 2048
DEPTH = 2
RMS_EPS = 1e-6
MASK_VALUE = -1e30
LB_FLOOR = 1e-30
GRID_W = 64

NA_HEADS = 8
NA_HEAD_DIM = 64
NA_WIDTH = 512
NA_WIN_ROWS = 8
NA_WIN_COLS = 16

GLA_HEADS = 4
GLA_HEAD_K = 64
GLA_HEAD_V = 128
GLA_KEY_WIDTH = 256
GLA_VAL_WIDTH = 512
GLA_GATE_RANK = 16
GLA_GATE_NORMALIZER = 16.0

GDN_HEADS = 4
GDN_HEAD_K = 128
GDN_HEAD_V = 128
GDN_KEY_WIDTH = 512
GDN_VAL_WIDTH = 512
GDN_CHUNK = 64

HGRN_HEADS = 4
HGRN_HEAD_K = 128
HGRN_HEAD_V = 128
HGRN_KEY_WIDTH = 512
HGRN_VAL_WIDTH = 512

LIN_CHUNK = 32

MEM_HEADS = 4
MEM_HEAD_DIM = 128
MEM_WIDTH = 512

N_BRANCH = 5
BRANCH_WIDTH = 512
N_EXPERTS = 8
MOE_TOP_K = 2

IN_WIDTHS = (
    NA_WIDTH, NA_WIDTH, NA_WIDTH,
    GLA_KEY_WIDTH, GLA_KEY_WIDTH, GLA_VAL_WIDTH,
    2 * GLA_GATE_RANK, GLA_VAL_WIDTH,
    2 * GDN_KEY_WIDTH + GDN_VAL_WIDTH,
    2 * GDN_HEADS, 2 * GDN_HEADS, GDN_VAL_WIDTH,
    HGRN_KEY_WIDTH, 2 * HGRN_KEY_WIDTH, HGRN_VAL_WIDTH, HGRN_VAL_WIDTH,
    MEM_WIDTH,
    N_BRANCH * D_MODEL,
)
P_IN = sum(IN_WIDTHS)

V7X_VMEM_BYTES = 64 * 1024 * 1024
VMEM_LIMIT_BYTES = V7X_VMEM_BYTES - 8 * 1024 * 1024
LANES = 128
MOE_TILE = 512


def _params(*semantics):
    return pltpu.CompilerParams(dimension_semantics=semantics, vmem_limit_bytes=VMEM_LIMIT_BYTES)


def _rms_norm_rows(x, gain):
    ms = jnp.mean(x * x, axis=-1, keepdims=True)
    return x * lax.rsqrt(ms + RMS_EPS) * gain


def _rms_matmul_kernel(x_ref, g_ref, w_ref, o_ref, h_ref):
    @pl.when(pl.program_id(1) == 0)
    def _():
        h_ref[...] = _rms_norm_rows(x_ref[...], g_ref[...]).astype(BF16)

    o_ref[...] = jnp.dot(h_ref[...], w_ref[...], preferred_element_type=F32).astype(o_ref.dtype)


def rms_matmul(x, gain, w, *, tm, tn, out_dtype=F32):
    m, k = x.shape
    n = w.shape[1]
    return pl.pallas_call(
        _rms_matmul_kernel,
        out_shape=jax.ShapeDtypeStruct((m, n), out_dtype),
        grid=(m // tm, n // tn),
        in_specs=[
            pl.BlockSpec((tm, k), lambda i, j: (i, 0)),
            pl.BlockSpec((1, k), lambda i, j: (0, 0)),
            pl.BlockSpec((k, tn), lambda i, j: (0, j)),
        ],
        out_specs=pl.BlockSpec((tm, tn), lambda i, j: (i, j)),
        scratch_shapes=[pltpu.VMEM((tm, k), BF16)],
        compiler_params=_params("parallel", "arbitrary"),
        name="rms_matmul",
    )(x, gain.reshape(1, k), w)


def _rms_swiglu_kernel(x_ref, g_ref, wg_ref, wu_ref, o_ref, h_ref):
    @pl.when(pl.program_id(1) == 0)
    def _():
        h_ref[...] = _rms_norm_rows(x_ref[...], g_ref[...]).astype(BF16)

    h = h_ref[...]
    a = jnp.dot(h, wg_ref[...], preferred_element_type=F32)
    b = jnp.dot(h, wu_ref[...], preferred_element_type=F32)
    o_ref[...] = (a * jax.nn.sigmoid(a) * b).astype(o_ref.dtype)


def rms_swiglu_up(x, gain, wg, wu, *, tm, tn):
    m, k = x.shape
    n = wg.shape[1]
    return pl.pallas_call(
        _rms_swiglu_kernel,
        out_shape=jax.ShapeDtypeStruct((m, n), BF16),
        grid=(m // tm, n // tn),
        in_specs=[
            pl.BlockSpec((tm, k), lambda i, j: (i, 0)),
            pl.BlockSpec((1, k), lambda i, j: (0, 0)),
            pl.BlockSpec((k, tn), lambda i, j: (0, j)),
            pl.BlockSpec((k, tn), lambda i, j: (0, j)),
        ],
        out_specs=pl.BlockSpec((tm, tn), lambda i, j: (i, j)),
        scratch_shapes=[pltpu.VMEM((tm, k), BF16)],
        compiler_params=_params("parallel", "arbitrary"),
        name="rms_swiglu_up",
    )(x, gain.reshape(1, k), wg, wu)


def _matmul_residual_kernel(a_ref, w_ref, r_ref, o_ref):
    o_ref[...] = r_ref[...] + jnp.dot(a_ref[...], w_ref[...], preferred_element_type=F32)


def matmul_residual(a, w, res, *, tm, tn):
    m, k = a.shape
    n = w.shape[1]
    return pl.pallas_call(
        _matmul_residual_kernel,
        out_shape=jax.ShapeDtypeStruct((m, n), F32),
        grid=(m // tm, n // tn),
        in_specs=[
            pl.BlockSpec((tm, k), lambda i, j: (i, 0)),
            pl.BlockSpec((k, tn), lambda i, j: (0, j)),
            pl.BlockSpec((tm, tn), lambda i, j: (i, j)),
        ],
        out_specs=pl.BlockSpec((tm, tn), lambda i, j: (i, j)),
        compiler_params=_params("parallel", "arbitrary"),
        name="matmul_residual",
    )(a, w, res)


def _merge_kernel(*refs):
    br_refs = refs[:N_BRANCH]
    gl_refs = refs[N_BRANCH:2 * N_BRANCH]
    wb_ref = refs[2 * N_BRANCH]
    o_ref = refs[2 * N_BRANCH + 1]
    acc = None
    for n in range(N_BRANCH):
        y = jnp.dot(br_refs[n][...], wb_ref[n], preferred_element_type=F32)
        t = jax.nn.sigmoid(gl_refs[n][...].astype(F32)) * y
        acc = t if acc is None else acc + t
    o_ref[...] = acc.astype(o_ref.dtype)


def merge_branches(branches, gate_logits, gate_col0, w_branch, *, tm, tn):
    m = branches[0].shape[0]
    d = D_MODEL
    tiles_per_branch = d // tn
    tile0 = gate_col0 // tn
    in_specs = [pl.BlockSpec((tm, BRANCH_WIDTH), lambda i, j: (i, 0)) for _ in range(N_BRANCH)]
    in_specs += [
        pl.BlockSpec((tm, tn), functools.partial(lambda i, j, n: (i, tile0 + n * tiles_per_branch + j), n=n))
        for n in range(N_BRANCH)
    ]
    in_specs += [pl.BlockSpec((N_BRANCH, BRANCH_WIDTH, tn), lambda i, j: (0, 0, j))]
    return pl.pallas_call(
        _merge_kernel,
        out_shape=jax.ShapeDtypeStruct((m, d), BF16),
        grid=(m // tm, d // tn),
        in_specs=in_specs,
        out_specs=pl.BlockSpec((tm, tn), lambda i, j: (i, j)),
        compiler_params=_params("parallel", "arbitrary"),
        name="merge_branches",
    )(*branches, *([gate_logits] * N_BRANCH), w_branch)


def _router_kernel(x_ref, g_ref, w_ref, b_ref, o_ref):
    h = _rms_norm_rows(x_ref[...], g_ref[...])
    o_ref[...] = jnp.dot(h, w_ref[...], preferred_element_type=F32,
                         precision=lax.Precision.HIGHEST) + b_ref[...]


def router_logits(x, gain, w_router, b_router, *, tm):
    m, k = x.shape
    e = w_router.shape[1]
    w_pad = jnp.zeros((k, LANES), F32).at[:, :e].set(w_router.astype(F32))
    b_pad = jnp.zeros((1, LANES), F32).at[0, :e].set(b_router.astype(F32))
    out = pl.pallas_call(
        _router_kernel,
        out_shape=jax.ShapeDtypeStruct((m, LANES), F32),
        grid=(m // tm,),
        in_specs=[
            pl.BlockSpec((tm, k), lambda i: (i, 0)),
            pl.BlockSpec((1, k), lambda i: (0, 0)),
            pl.BlockSpec((k, LANES), lambda i: (0, 0)),
            pl.BlockSpec((1, LANES), lambda i: (0, 0)),
        ],
        out_specs=pl.BlockSpec((tm, LANES), lambda i: (i, 0)),
        compiler_params=_params("parallel"),
        name="router_logits",
    )(x, gain.reshape(1, k), w_pad, b_pad)
    return out[:, :e]


def _moe_kernel(tile_e_ref, n_used_ref, x_ref, wg_ref, wu_ref, wd_ref, o_ref, acc_ref):
    i = pl.program_id(0)
    j = pl.program_id(1)
    last = pl.num_programs(1) - 1
    used = i < n_used_ref[0]

    @pl.when(jnp.logical_and(used, j == 0))
    def _():
        acc_ref[...] = jnp.zeros_like(acc_ref)

    @pl.when(used)
    def _():
        x = x_ref[...]
        a = jnp.dot(x, wg_ref[0], preferred_element_type=F32)
        b = jnp.dot(x, wu_ref[0], preferred_element_type=F32)
        act = (a * jax.nn.sigmoid(a) * b).astype(BF16)
        acc_ref[...] += jnp.dot(act, wd_ref[0], preferred_element_type=F32)

    @pl.when(jnp.logical_and(used, j == last))
    def _():
        o_ref[...] = acc_ref[...]

    @pl.when(jnp.logical_and(jnp.logical_not(used), j == last))
    def _():
        o_ref[...] = jnp.zeros_like(o_ref)


def moe_experts(xb, tile_e, n_used, wg, wu, wd, *, tm, tf):
    rows, d = xb.shape
    ff = wg.shape[2]
    n_tiles = rows // tm

    def x_map(i, j, te, nu):
        return (jnp.minimum(i, nu[0] - 1), 0)

    def up_map(i, j, te, nu):
        keep = i < nu[0]
        return (te[i], 0, jnp.where(keep, j, ff // tf - 1))

    def down_map(i, j, te, nu):
        keep = i < nu[0]
        return (te[i], jnp.where(keep, j, ff // tf - 1), 0)

    grid_spec = pltpu.PrefetchScalarGridSpec(
        num_scalar_prefetch=2,
        grid=(n_tiles, ff // tf),
        in_specs=[
            pl.BlockSpec((tm, d), x_map),
            pl.BlockSpec((1, d, tf), up_map),
            pl.BlockSpec((1, d, tf), up_map),
            pl.BlockSpec((1, tf, d), down_map),
        ],
        out_specs=pl.BlockSpec((tm, d), lambda i, j, te, nu: (i, 0)),
        scratch_shapes=[pltpu.VMEM((tm, d), F32)],
    )
    return pl.pallas_call(
        _moe_kernel,
        out_shape=jax.ShapeDtypeStruct((rows, d), F32),
        grid_spec=grid_spec,
        compiler_params=_params("arbitrary", "arbitrary"),
        name="moe_experts",
    )(tile_e, n_used, xb, wg, wu, wd)


def moe_layer(x2d, gain, w_router, b_router, wg, wu, wd):
    n, d = x2d.shape
    e = N_EXPERTS
    logits = router_logits(x2d, gain, w_router, b_router, tm=1024)
    top_logits, top_e = lax.top_k(logits, MOE_TOP_K)
    top_w = jax.nn.softmax(top_logits, axis=-1)
    nk = n * MOE_TOP_K
    n_tiles = -(-nk // MOE_TILE) + e
    flat_e = top_e.reshape(nk).astype(jnp.int32)
    flat_tok = jnp.repeat(jnp.arange(n, dtype=jnp.int32), MOE_TOP_K)
    onehot = (flat_e[:, None] == jnp.arange(e, dtype=jnp.int32)[None, :]).astype(jnp.int32)
    rank = jnp.take_along_axis(jnp.cumsum(onehot, axis=0), flat_e[:, None], axis=1)[:, 0] - 1
    counts = jnp.sum(onehot, axis=0)
    padded = (counts + MOE_TILE - 1) // MOE_TILE * MOE_TILE
    pad_end = jnp.cumsum(padded)
    pad_start = pad_end - padded
    slot = (pad_start[flat_e] + rank).astype(jnp.int32)
    slot_tok = jnp.full((n_tiles * MOE_TILE,), n, jnp.int32).at[slot].set(flat_tok)
    tile_start = jnp.arange(n_tiles, dtype=jnp.int32) * MOE_TILE
    tile_e = jnp.minimum(jnp.searchsorted(pad_end, tile_start, side="right"), e - 1).astype(jnp.int32)
    n_used = (pad_end[-1] // MOE_TILE).astype(jnp.int32).reshape(1)
    tile_e = jnp.where(tile_start < pad_end[-1], tile_e, tile_e[jnp.maximum(n_used[0] - 1, 0)])

    h = rms_only(x2d, gain)
    h_pad = jnp.concatenate([h, jnp.zeros((1, d), h.dtype)], axis=0)
    xb = h_pad[slot_tok]
    yb = moe_experts(xb, tile_e, n_used, wg, wu, wd, tm=MOE_TILE, tf=512)
    y = yb[slot].reshape(n, MOE_TOP_K, d)
    return x2d + jnp.einsum("nkd,nk->nd", y, top_w.astype(y.dtype))


def _rms_kernel(x_ref, g_ref, o_ref):
    o_ref[...] = _rms_norm_rows(x_ref[...], g_ref[...]).astype(o_ref.dtype)


def rms_only(x, gain, *, tm=1024):
    m, k = x.shape
    return pl.pallas_call(
        _rms_kernel,
        out_shape=jax.ShapeDtypeStruct((m, k), BF16),
        grid=(m // tm,),
        in_specs=[pl.BlockSpec((tm, k), lambda i: (i, 0)), pl.BlockSpec((1, k), lambda i: (0, 0))],
        out_specs=pl.BlockSpec((tm, k), lambda i: (i, 0)),
        compiler_params=_params("parallel"),
        name="rms_only",
    )(x, gain.reshape(1, k))


_SRC = dict(zip(
    ("na_q", "na_k", "na_v", "gla_q", "gla_k", "gla_v", "gla_lr", "gla_og", "gdn_qkv", "gdn_a", "gdn_b",
     "gdn_og", "hg_q", "hg_f", "hg_i", "hg_og", "mem_q", "gates"),
    zip(np.cumsum((0,) + IN_WIDTHS[:-1]).tolist(), IN_WIDTHS)))
_PB_ORDER = ("na_q", "na_k", "na_v", "gla_q", "gla_k", "gla_v", "gla_og", "gdn_qkv", "gdn_og", "hg_q", "hg_i",
             "hg_og", "mem_q", "gates")
_PF_ORDER = ("hg_f", "gla_lr", "gdn_a", "gdn_b")
PB_COL = {}
_c = 0
for _name in _PB_ORDER:
    PB_COL[_name] = _c
    _c += _SRC[_name][1]
PB_WIDTH = _c
PF_COL = {}
_c = 0
for _name in _PF_ORDER:
    PF_COL[_name] = _c
    _c += _SRC[_name][1]
PF_WIDTH = -(-_c // LANES) * LANES
PF_SMALL_COL = PF_COL["gla_lr"]
GDN_A_LANE = PF_COL["gdn_a"] - PF_SMALL_COL
GDN_B_LANE = PF_COL["gdn_b"] - PF_SMALL_COL


def _split_w_in(w):
    wb = jnp.concatenate([w[:, _SRC[n][0]:_SRC[n][0] + _SRC[n][1]] for n in _PB_ORDER], axis=1)
    wf = jnp.concatenate([w[:, _SRC[n][0]:_SRC[n][0] + _SRC[n][1]] for n in _PF_ORDER], axis=1)
    wf = jnp.pad(wf, ((0, 0), (0, PF_WIDTH - wf.shape[1])))
    return wb.astype(BF16), wf.astype(BF16)


def _segment_rms(x, gain, seg_ones, seg_width):
    sq = x * x
    hi = sq.astype(BF16)
    lo = (sq - hi.astype(F32)).astype(BF16)
    ss = (jnp.dot(hi, seg_ones, preferred_element_type=F32)
          + jnp.dot(lo, seg_ones, preferred_element_type=F32))
    return x * lax.rsqrt(ss * (1.0 / seg_width) + RMS_EPS) * gain


NA_ROWS_PER_STEP = 8
NA_BAND = NA_WIN_ROWS * GRID_W


def _na_bias_table(rel_bias):
    c = np.arange(GRID_W)
    dc = np.clip(c[None, :] - c[:, None], 1 - NA_WIN_COLS, NA_WIN_COLS - 1) + (NA_WIN_COLS - 1)
    col_start = np.clip(c - NA_WIN_COLS // 2, 0, GRID_W - NA_WIN_COLS)
    col_in = (c[None, :] >= col_start[:, None]) & (c[None, :] < col_start[:, None] + NA_WIN_COLS)
    cfg = np.arange(NA_WIN_ROWS)[:, None]
    dr = np.arange(NA_WIN_ROWS)[None, :] - cfg + (NA_WIN_ROWS - 1)
    t = rel_bias.astype(F32)[:, dr][:, :, :, dc]
    t = jnp.where(col_in[None, None, None], t, MASK_VALUE)
    return t.transpose(1, 0, 3, 2, 4).reshape(NA_WIN_ROWS, NA_HEADS, GRID_W, NA_BAND)


def _na_kernel(q_ref, k_ref, v_ref, qg_ref, kg_ref, seg_ref, bias_ref, o_ref, kn_ref):
    step = pl.program_id(1)
    rows_total = k_ref.shape[0] // GRID_W
    seg = seg_ref[...]

    @pl.when(step == 0)
    def _():
        def norm_keys(t, carry):
            rows = pl.ds(pl.multiple_of(t * 256, 256), 256)
            kn_ref[rows, :] = _segment_rms(k_ref[rows, :].astype(F32), kg_ref[...], seg, NA_HEAD_DIM).astype(BF16)
            return carry
        lax.fori_loop(0, k_ref.shape[0] // 256, norm_keys, 0)

    lane = lax.broadcasted_iota(jnp.int32, (1, LANES), 1)
    low_half = lane < NA_HEAD_DIM

    def one_row(rr, carry):
        r = step * NA_ROWS_PER_STEP + rr
        row_start = jnp.clip(r - NA_WIN_ROWS // 2, 0, rows_total - NA_WIN_ROWS)
        cfg = r - row_start
        qrows = pl.ds(pl.multiple_of(rr * GRID_W, GRID_W), GRID_W)
        band = pl.ds(pl.multiple_of(row_start * GRID_W, GRID_W), NA_BAND)
        qn = (_segment_rms(q_ref[qrows, :].astype(F32), qg_ref[...], seg, NA_HEAD_DIM)
              * (NA_HEAD_DIM ** -0.5)).astype(BF16)
        for pair in range(NA_HEADS // 2):
            cols = slice(pair * LANES, (pair + 1) * LANES)
            qp = qn[:, cols]
            kp = kn_ref[band, cols]
            vp = v_ref[band, cols]
            outs = []
            for half in range(2):
                keep = low_half if half == 0 else jnp.logical_not(low_half)
                qm = jnp.where(keep, qp, jnp.zeros_like(qp))
                s = lax.dot_general(qm, kp, (((1,), (1,)), ((), ())), preferred_element_type=F32)
                s = s + bias_ref[cfg, 2 * pair + half]
                m = jnp.max(s, axis=-1, keepdims=True)
                e = jnp.exp(s - m)
                l = jnp.sum(e, axis=-1, keepdims=True)
                o = jnp.dot(e.astype(BF16), vp, preferred_element_type=F32)
                outs.append(o / l)
            o_ref[qrows, cols] = jnp.where(low_half, outs[0], outs[1]).astype(o_ref.dtype)
        return carry

    lax.fori_loop(0, NA_ROWS_PER_STEP, one_row, 0)


def neighbourhood_attention(pb, batch, q_gain, k_gain, rel_bias):
    m = pb.shape[0]
    s = m // batch
    tq = NA_ROWS_PER_STEP * GRID_W
    steps = s // tq
    qg = jnp.tile(q_gain.astype(F32), NA_HEADS).reshape(1, NA_WIDTH)
    kg = jnp.tile(k_gain.astype(F32), NA_HEADS).reshape(1, NA_WIDTH)
    seg = jnp.asarray(np.kron(np.eye(NA_HEADS), np.ones((NA_HEAD_DIM, NA_HEAD_DIM))), BF16)
    bias = _na_bias_table(rel_bias)
    cq, ck, cv = (PB_COL[n] // NA_WIDTH for n in ("na_q", "na_k", "na_v"))
    return pl.pallas_call(
        _na_kernel,
        out_shape=jax.ShapeDtypeStruct((m, NA_WIDTH), BF16),
        grid=(batch, steps),
        in_specs=[
            pl.BlockSpec((tq, NA_WIDTH), lambda b, t: (b * steps + t, cq)),
            pl.BlockSpec((s, NA_WIDTH), lambda b, t: (b, ck)),
            pl.BlockSpec((s, NA_WIDTH), lambda b, t: (b, cv)),
            pl.BlockSpec((1, NA_WIDTH), lambda b, t: (0, 0)),
            pl.BlockSpec((1, NA_WIDTH), lambda b, t: (0, 0)),
            pl.BlockSpec((NA_WIDTH, NA_WIDTH), lambda b, t: (0, 0)),
            pl.BlockSpec((NA_WIN_ROWS, NA_HEADS, GRID_W, NA_BAND), lambda b, t: (0, 0, 0, 0)),
        ],
        out_specs=pl.BlockSpec((tq, NA_WIDTH), lambda b, t: (b * steps + t, 0)),
        scratch_shapes=[pltpu.VMEM((s, NA_WIDTH), BF16)],
        compiler_params=_params("parallel", "arbitrary"),
        name="neighbourhood_attention",
    )(pb, pb, pb, qg, kg, seg, bias)


def _mem_attn_kernel(q_ref, kv_ref, qg_ref, kg_ref, o_ref, kn_ref):
    @pl.when(pl.program_id(1) == 0)
    def _():
        for h in range(MEM_HEADS):
            cols = slice(h * MEM_HEAD_DIM, (h + 1) * MEM_HEAD_DIM)
            kn_ref[:, cols] = _rms_norm_rows(kv_ref[:, cols].astype(F32), kg_ref[...]).astype(BF16)

    for h in range(MEM_HEADS):
        cols = slice(h * MEM_HEAD_DIM, (h + 1) * MEM_HEAD_DIM)
        qn = _rms_norm_rows(q_ref[:, cols].astype(F32), qg_ref[...]).astype(BF16)
        s = lax.dot_general(qn, kn_ref[:, cols], (((1,), (1,)), ((), ())),
                            preferred_element_type=F32) * (MEM_HEAD_DIM ** -0.5)
        m = jnp.max(s, axis=-1, keepdims=True)
        e = jnp.exp(s - m)
        l = jnp.sum(e, axis=-1, keepdims=True)
        vh = kv_ref[:, MEM_WIDTH + h * MEM_HEAD_DIM:MEM_WIDTH + (h + 1) * MEM_HEAD_DIM]
        o = jnp.dot(e.astype(BF16), vh, preferred_element_type=F32)
        o_ref[:, cols] = (o / l).astype(o_ref.dtype)


def memory_cross_attention(pb, kv, batch, q_gain, k_gain, *, tq=512):
    m = pb.shape[0]
    steps = m // batch // tq
    n_mem = kv.shape[0] // batch
    cq = PB_COL["mem_q"] // MEM_WIDTH
    return pl.pallas_call(
        _mem_attn_kernel,
        out_shape=jax.ShapeDtypeStruct((m, MEM_WIDTH), BF16),
        grid=(batch, steps),
        in_specs=[
            pl.BlockSpec((tq, MEM_WIDTH), lambda b, t: (b * steps + t, cq)),
            pl.BlockSpec((n_mem, 2 * MEM_WIDTH), lambda b, t: (b, 0)),
            pl.BlockSpec((1, MEM_HEAD_DIM), lambda b, t: (0, 0)),
            pl.BlockSpec((1, MEM_HEAD_DIM), lambda b, t: (0, 0)),
        ],
        out_specs=pl.BlockSpec((tq, MEM_WIDTH), lambda b, t: (b * steps + t, 0)),
        scratch_shapes=[pltpu.VMEM((n_mem, MEM_WIDTH), BF16)],
        compiler_params=_params("parallel", "arbitrary"),
        name="memory_cross_attention",
    )(pb, kv, q_gain.astype(F32).reshape(1, MEM_HEAD_DIM), k_gain.astype(F32).reshape(1, MEM_HEAD_DIM))


LIN_BLOCK = 512
HEAD_V = 128


def _log_sigmoid(x):
    return jnp.minimum(x, 0.0) - jnp.log1p(jnp.exp(-jnp.abs(x)))


def _logaddexp(a, b):
    return jnp.maximum(a, b) + jnp.log1p(jnp.exp(-jnp.abs(a - b)))


def _dot_f32(a, b):
    return jnp.dot(a, b, preferred_element_type=F32, precision=lax.Precision.HIGHEST)


def _lin_chunk(qc, kc, vc, lg, s_ref, *, reverse, heads):
    c, w = qc.shape
    dk = w // heads
    ii = lax.broadcasted_iota(jnp.int32, (c, c), 0)
    jj = lax.broadcasted_iota(jnp.int32, (c, c), 1)
    causal = (jj >= ii) if reverse else (jj <= ii)
    b = _dot_f32(causal.astype(F32), lg)
    mid = c - 1 - c // 2 if reverse else c // 2
    end = 0 if reverse else c - 1
    b_mid = b[mid:mid + 1, :]
    b_end = b[end:end + 1, :]
    qe = qc * jnp.exp(b - b_mid)
    ke = (kc * jnp.exp(b_mid - b)).astype(BF16)
    q_dec = qc * jnp.exp(b)
    k_dec = kc * jnp.exp(b_end - b)
    g_tot = jnp.exp(b_end)

    lane = lax.broadcasted_iota(jnp.int32, (1, w), 1)
    head_mask = [((lane >= h * dk) & (lane < (h + 1) * dk)).astype(F32) for h in range(heads)]
    q4 = jnp.concatenate([qe * head_mask[h] for h in range(heads)], axis=0).astype(BF16)
    q4d = jnp.concatenate([q_dec * head_mask[h] for h in range(heads)], axis=0).astype(BF16)
    k4 = jnp.concatenate([k_dec * head_mask[h] for h in range(heads)], axis=0).astype(BF16)
    v4 = jnp.concatenate([vc[:, h * HEAD_V:(h + 1) * HEAD_V] for h in range(heads)], axis=0)

    scores = lax.dot_general(q4, ke, (((1,), (1,)), ((), ())), preferred_element_type=F32)
    ri = lax.broadcasted_iota(jnp.int32, (heads * c, c), 0) % c
    cj = lax.broadcasted_iota(jnp.int32, (heads * c, c), 1)
    keep = (cj >= ri) if reverse else (cj <= ri)
    scores = jnp.where(keep, scores, 0.0).astype(BF16)
    state = s_ref[...]
    o_inter = lax.dot_general(q4d, state.astype(BF16), (((1,), (1,)), ((), ())),
                              preferred_element_type=F32)
    outs = []
    for h in range(heads):
        rows = slice(h * c, (h + 1) * c)
        o_intra = jnp.dot(scores[rows, :], vc[:, h * HEAD_V:(h + 1) * HEAD_V], preferred_element_type=F32)
        outs.append(o_intra + o_inter[rows, :])
    s_ref[...] = state * g_tot + lax.dot_general(v4, k4, (((0,), (0,)), ((), ())), preferred_element_type=F32)
    return jnp.concatenate(outs, axis=1)


def _gla_inputs(refs, rows, direction, params):
    q_ref, k_ref, v_ref, g_ref = refs
    wpad_ref, bias_ref = params
    qc = q_ref[rows, :].astype(F32) * (GLA_HEAD_K ** -0.5)
    kc = k_ref[rows, :].astype(F32)
    gk = _dot_f32(g_ref[rows, :], wpad_ref[direction]) + bias_ref[direction]
    lg = _log_sigmoid(gk) * (1.0 / GLA_GATE_NORMALIZER)
    return qc, kc, v_ref[rows, :], lg


def _hgrn_inputs(refs, rows, direction, params):
    q_ref, v_ref, z_ref = refs
    lb_ref, log_lb_ref, log1m_lb_ref = params
    qr = q_ref[rows, :].astype(F32)
    qc = qr * jax.nn.sigmoid(qr)
    z = z_ref[rows, :]
    lg = _logaddexp(log_lb_ref[direction], log1m_lb_ref[direction] + _log_sigmoid(z))
    kc = (1.0 - lb_ref[direction]) * jax.nn.sigmoid(-z)
    return qc, kc, v_ref[rows, :], lg


def _bidir_lin_kernel(*refs, load_inputs, n_in, n_params, heads, chunk):
    fwd_refs = refs[:n_in]
    bwd_refs = refs[n_in:2 * n_in]
    params = refs[2 * n_in:2 * n_in + n_params]
    of_ref, ob_ref, sf_ref, sb_ref = refs[2 * n_in + n_params:]

    @pl.when(pl.program_id(1) == 0)
    def _():
        sf_ref[...] = jnp.zeros_like(sf_ref)
        sb_ref[...] = jnp.zeros_like(sb_ref)

    n_chunks = of_ref.shape[0] // chunk

    def body(c, carry):
        rows = pl.ds(pl.multiple_of(c * chunk, chunk), chunk)
        of_ref[rows, :] = _lin_chunk(*load_inputs(fwd_refs, rows, 0, params), sf_ref, reverse=False, heads=heads)
        rows = pl.ds(pl.multiple_of((n_chunks - 1 - c) * chunk, chunk), chunk)
        ob_ref[rows, :] = _lin_chunk(*load_inputs(bwd_refs, rows, 1, params), sb_ref, reverse=True, heads=heads)
        return carry

    lax.fori_loop(0, n_chunks, body, 0)


def _bidir_lin_call(name, load_inputs, arrays, col_blocks, widths, params, batch, heads, key_width):
    m = arrays[0].shape[0]
    nb = m // batch // LIN_BLOCK
    out_w = heads * HEAD_V

    def spec(width, col, rev):
        if rev:
            return pl.BlockSpec((LIN_BLOCK, width), lambda b, t: (b * nb + nb - 1 - t, col))
        return pl.BlockSpec((LIN_BLOCK, width), lambda b, t: (b * nb + t, col))

    in_specs = [spec(w, c[0], False) for w, c in zip(widths, col_blocks)]
    in_specs += [spec(w, c[1], True) for w, c in zip(widths, col_blocks)]
    in_specs += [pl.BlockSpec(p.shape, functools.partial(lambda b, t, nd: (0,) * nd, nd=p.ndim)) for p in params]
    kern = functools.partial(_bidir_lin_kernel, load_inputs=load_inputs, n_in=len(arrays), n_params=len(params),
                             heads=heads, chunk=LIN_CHUNK)
    return pl.pallas_call(
        kern,
        out_shape=(jax.ShapeDtypeStruct((m, out_w), F32), jax.ShapeDtypeStruct((m, out_w), F32)),
        grid=(batch, nb),
        in_specs=in_specs,
        out_specs=(spec(out_w, 0, False), spec(out_w, 0, True)),
        scratch_shapes=[pltpu.VMEM((HEAD_V, key_width), F32), pltpu.VMEM((HEAD_V, key_width), F32)],
        compiler_params=_params("parallel", "arbitrary"),
        name=name,
    )(*arrays, *arrays, *params)


def _finish_kernel(of_ref, ob_ref, og_ref, gain_ref, o_ref, *, heads, silu_gate):
    for h in range(heads):
        cols = slice(h * HEAD_V, (h + 1) * HEAD_V)
        y = _rms_norm_rows(of_ref[:, cols] + ob_ref[:, cols], gain_ref[...])
        g = og_ref[:, cols].astype(F32)
        gate = jax.nn.sigmoid(g)
        if silu_gate:
            gate = g * gate
        o_ref[:, cols] = (y * gate).astype(o_ref.dtype)


def finish_branch(o_fwd, o_bwd, pb, og_name, gain, *, silu_gate, tm=512):
    m, w = o_fwd.shape
    heads = w // HEAD_V
    cg = PB_COL[og_name] // w
    return pl.pallas_call(
        functools.partial(_finish_kernel, heads=heads, silu_gate=silu_gate),
        out_shape=jax.ShapeDtypeStruct((m, w), BF16),
        grid=(m // tm,),
        in_specs=[
            pl.BlockSpec((tm, w), lambda i: (i, 0)),
            pl.BlockSpec((tm, w), lambda i: (i, 0)),
            pl.BlockSpec((tm, w), lambda i: (i, cg)),
            pl.BlockSpec((1, HEAD_V), lambda i: (0, 0)),
        ],
        out_specs=pl.BlockSpec((tm, w), lambda i: (i, 0)),
        compiler_params=_params("parallel"),
        name="finish_branch",
    )(o_fwd, o_bwd, pb, gain.astype(F32).reshape(1, HEAD_V))


def gla_branch(pb, pf, batch, w_gate_up, b_gate, norm_gain):
    wpad = jnp.zeros((2, LANES, GLA_KEY_WIDTH), F32)
    for d in range(2):
        wpad = wpad.at[d, d * GLA_GATE_RANK:(d + 1) * GLA_GATE_RANK, :].set(w_gate_up[d].astype(F32))
    bias = b_gate.astype(F32).reshape(2, 1, GLA_KEY_WIDTH)
    cols = [(PB_COL["gla_q"] // GLA_KEY_WIDTH,) * 2, (PB_COL["gla_k"] // GLA_KEY_WIDTH,) * 2,
            (PB_COL["gla_v"] // GLA_VAL_WIDTH,) * 2, (PF_SMALL_COL // LANES,) * 2]
    o_f, o_b = _bidir_lin_call("gla_scan", _gla_inputs, [pb, pb, pb, pf], cols,
                               [GLA_KEY_WIDTH, GLA_KEY_WIDTH, GLA_VAL_WIDTH, LANES], [wpad, bias],
                               batch, GLA_HEADS, GLA_KEY_WIDTH)
    return finish_branch(o_f, o_b, pb, "gla_og", norm_gain, silu_gate=True)


def hgrn2_branch(pb, pf, batch, lower_bound, norm_gain):
    lb = lower_bound.astype(F32).reshape(2, 1, HGRN_KEY_WIDTH)
    log_lb = jnp.log(jnp.maximum(lb, LB_FLOOR))
    log1m_lb = jnp.log1p(-lb)
    zc = PF_COL["hg_f"] // HGRN_KEY_WIDTH
    cols = [(PB_COL["hg_q"] // HGRN_KEY_WIDTH,) * 2, (PB_COL["hg_i"] // HGRN_VAL_WIDTH,) * 2, (zc, zc + 1)]
    o_f, o_b = _bidir_lin_call("hgrn2_scan", _hgrn_inputs, [pb, pb, pf], cols,
                               [HGRN_KEY_WIDTH, HGRN_VAL_WIDTH, HGRN_KEY_WIDTH], [lb, log_lb, log1m_lb],
                               batch, HGRN_HEADS, HGRN_KEY_WIDTH)
    return finish_branch(o_f, o_b, pb, "hg_og", norm_gain, silu_gate=False)


GDN_CONV_WIDTH = 5
GDN_QKV_WIDTH = 2 * GDN_KEY_WIDTH + GDN_VAL_WIDTH
GDN_HALO = 16


def _gdn_prep_kernel(prev_ref, cur_ref, next_ref, w_ref, o_ref, xp_ref, *, blocks_per_seq):
    i = pl.program_id(0)
    t = cur_ref.shape[0]
    pos = i % blocks_per_seq
    prev = prev_ref[...].astype(F32)
    nxt = next_ref[...].astype(F32)
    xp_ref[0:GDN_HALO, :] = jnp.where(pos == 0, jnp.zeros_like(prev), prev)
    xp_ref[GDN_HALO:GDN_HALO + t, :] = cur_ref[...].astype(F32)
    xp_ref[GDN_HALO + t:, :] = jnp.where(pos == blocks_per_seq - 1, jnp.zeros_like(nxt), nxt)
    half = GDN_CONV_WIDTH // 2
    for g in range(GDN_QKV_WIDTH // LANES):
        cols = slice(g * LANES, (g + 1) * LANES)
        acc = None
        for j in range(GDN_CONV_WIDTH):
            term = xp_ref[GDN_HALO - half + j:GDN_HALO - half + j + t, cols] * w_ref[j:j + 1, cols]
            acc = term if acc is None else acc + term
        y = acc * jax.nn.sigmoid(acc)
        if g < 2 * GDN_HEADS:
            y = y * lax.rsqrt(jnp.sum(y * y, axis=-1, keepdims=True) + 1e-6)
            if g < GDN_HEADS:
                y = y * (GDN_HEAD_K ** -0.5)
        o_ref[:, cols] = y.astype(o_ref.dtype)


def gdn_prep(pb, batch, conv_w, *, t=512):
    m = pb.shape[0]
    blocks_per_seq = m // batch // t
    halo_per_block = t // GDN_HALO
    col = PB_COL["gdn_qkv"] // GDN_QKV_WIDTH
    last_halo = m // GDN_HALO - 1
    return pl.pallas_call(
        functools.partial(_gdn_prep_kernel, blocks_per_seq=blocks_per_seq),
        out_shape=jax.ShapeDtypeStruct((m, GDN_QKV_WIDTH), BF16),
        grid=(m // t,),
        in_specs=[
            pl.BlockSpec((GDN_HALO, GDN_QKV_WIDTH), lambda i: (jnp.maximum(i * halo_per_block - 1, 0), col)),
            pl.BlockSpec((t, GDN_QKV_WIDTH), lambda i: (i, col)),
            pl.BlockSpec((GDN_HALO, GDN_QKV_WIDTH),
                         lambda i: (jnp.minimum((i + 1) * halo_per_block, last_halo), col)),
            pl.BlockSpec((GDN_CONV_WIDTH, GDN_QKV_WIDTH), lambda i: (0, 0)),
        ],
        out_specs=pl.BlockSpec((t, GDN_QKV_WIDTH), lambda i: (i, 0)),
        scratch_shapes=[pltpu.VMEM((t + 2 * GDN_HALO, GDN_QKV_WIDTH), F32)],
        compiler_params=_params("parallel"),
        name="gdn_prep",
    )(pb, pb, pb, conv_w.astype(F32))


def _unit_triangular_inverse(a):
    c = a.shape[0]
    ii = lax.broadcasted_iota(jnp.int32, (c, c), 0)
    jj = lax.broadcasted_iota(jnp.int32, (c, c), 1)
    eye = (ii == jj).astype(F32)

    def same_block(s):
        return (ii // s) == (jj // s)

    d = jnp.where(same_block(8), a, 0.0)
    d2 = _dot_f32(d, d)
    d4 = _dot_f32(d2, d2)
    t = _dot_f32(_dot_f32(eye - d, eye + d2), eye + d4)
    s = 8
    while s < c:
        e = jnp.where(same_block(2 * s) & jnp.logical_not(same_block(s)), a, 0.0)
        t = t - _dot_f32(t, _dot_f32(e, t))
        s *= 2
    return t


def _softplus(x):
    return jnp.maximum(x, 0.0) + jnp.log1p(jnp.exp(-jnp.abs(x)))


def _gdn_chunk(qkv, small, a_scale, dt_bias, sel_a, sel_b, s_ref, *, reverse):
    c = qkv.shape[0]
    ii = lax.broadcasted_iota(jnp.int32, (c, c), 0)
    jj = lax.broadcasted_iota(jnp.int32, (c, c), 1)
    incl = (jj >= ii) if reverse else (jj <= ii)
    strict = (jj > ii) if reverse else (jj < ii)
    end = 0 if reverse else c - 1

    log_alpha = a_scale * _softplus(small + dt_bias)
    g_all = _dot_f32(incl.astype(F32), log_alpha)
    g_col = _dot_f32(g_all, sel_a)
    la_col = _dot_f32(log_alpha, sel_a)
    incl_t = (ii >= jj) if reverse else (ii <= jj)
    incl_t = jnp.concatenate([incl_t.astype(F32), jnp.zeros((c, HEAD_V - c), F32)], axis=1)
    g_row = _dot_f32(jnp.ones((c, c), F32), la_col * jnp.concatenate([incl_t] * GDN_HEADS, axis=1))
    beta_col = _dot_f32(jax.nn.sigmoid(small), sel_b)

    outs = []
    for h in range(GDN_HEADS):
        hs = slice(h * HEAD_V, (h + 1) * HEAD_V)
        q = qkv[:, hs]
        k = qkv[:, GDN_KEY_WIDTH + h * HEAD_V:GDN_KEY_WIDTH + (h + 1) * HEAD_V]
        v = qkv[:, 2 * GDN_KEY_WIDTH + h * HEAD_V:2 * GDN_KEY_WIDTH + (h + 1) * HEAD_V].astype(F32)
        kf = k.astype(F32)
        gc = g_col[:, hs]
        beta = beta_col[:, hs]
        diff = gc[:, :c] - g_row[:, h * HEAD_V:h * HEAD_V + c]
        decay = jnp.where(incl, jnp.exp(jnp.where(incl, diff, 0.0)), 0.0)
        k_beta = kf * beta
        kk = lax.dot_general(k_beta.astype(BF16), k, (((1,), (1,)), ((), ())), preferred_element_type=F32)
        t_inv = _unit_triangular_inverse(jnp.where(strict, kk * decay, 0.0))
        eg = jnp.exp(gc)
        sol = _dot_f32(t_inv, jnp.concatenate([v * beta, k_beta * eg], axis=1))
        u, w = sol[:, :HEAD_V], sol[:, HEAD_V:]
        attn = lax.dot_general(q, k, (((1,), (1,)), ((), ())), preferred_element_type=F32) * decay
        g_end = gc[end:end + 1, :]
        q_dec = (q.astype(F32) * eg).astype(BF16)
        k_dec = (kf * jnp.exp(g_end - gc)).astype(BF16)
        state = s_ref[h]
        state_b = state.astype(BF16)
        v_new = u - jnp.dot(w.astype(BF16), state_b, preferred_element_type=F32)
        v_new_b = v_new.astype(BF16)
        outs.append(jnp.dot(q_dec, state_b, preferred_element_type=F32)
                    + jnp.dot(attn.astype(BF16), v_new_b, preferred_element_type=F32))
        s_ref[h] = state * jnp.exp(g_end) + lax.dot_general(k_dec, v_new_b, (((0,), (0,)), ((), ())),
                                                             preferred_element_type=F32)
    return jnp.concatenate(outs, axis=1)


def _gdn_scan_kernel(qf_ref, gf_ref, qb_ref, gb_ref, a_ref, dtb_ref, sela_ref, selb_ref,
                     of_ref, ob_ref, sf_ref, sb_ref):
    @pl.when(pl.program_id(1) == 0)
    def _():
        sf_ref[...] = jnp.zeros_like(sf_ref)
        sb_ref[...] = jnp.zeros_like(sb_ref)

    n_chunks = of_ref.shape[0] // GDN_CHUNK

    def body(c, carry):
        rows = pl.ds(pl.multiple_of(c * GDN_CHUNK, GDN_CHUNK), GDN_CHUNK)
        of_ref[rows, :] = _gdn_chunk(qf_ref[rows, :], gf_ref[rows, :], a_ref[...], dtb_ref[...],
                                     sela_ref[0], selb_ref[0], sf_ref, reverse=False)
        rows = pl.ds(pl.multiple_of((n_chunks - 1 - c) * GDN_CHUNK, GDN_CHUNK), GDN_CHUNK)
        ob_ref[rows, :] = _gdn_chunk(qb_ref[rows, :], gb_ref[rows, :], a_ref[...], dtb_ref[...],
                                     sela_ref[1], selb_ref[1], sb_ref, reverse=True)
        return carry

    lax.fori_loop(0, n_chunks, body, 0)


def gated_deltanet_branch(pb, pf, batch, conv_w, a_log, dt_bias, norm_gain):
    m = pb.shape[0]
    nb = m // batch // LIN_BLOCK
    qkv = gdn_prep(pb, batch, conv_w)
    n_gate = 2 * GDN_HEADS
    a_scale = jnp.zeros((1, LANES), F32).at[0, GDN_A_LANE:GDN_A_LANE + n_gate].set(
        -jnp.exp(a_log.astype(F32)).reshape(n_gate))
    dtb = jnp.zeros((1, LANES), F32).at[0, GDN_A_LANE:GDN_A_LANE + n_gate].set(dt_bias.astype(F32).reshape(n_gate))
    sel = np.zeros((2, 2, LANES, GDN_VAL_WIDTH), np.float32)
    for d in range(2):
        for h in range(GDN_HEADS):
            sel[0, d, GDN_A_LANE + d * GDN_HEADS + h, h * HEAD_V:(h + 1) * HEAD_V] = 1.0
            sel[1, d, GDN_B_LANE + d * GDN_HEADS + h, h * HEAD_V:(h + 1) * HEAD_V] = 1.0
    small_col = PF_SMALL_COL // LANES

    def fwd(width, col):
        return pl.BlockSpec((LIN_BLOCK, width), lambda b, t: (b * nb + t, col))

    def bwd(width, col):
        return pl.BlockSpec((LIN_BLOCK, width), lambda b, t: (b * nb + nb - 1 - t, col))

    def whole(shape):
        return pl.BlockSpec(shape, functools.partial(lambda b, t, nd: (0,) * nd, nd=len(shape)))

    o_f, o_b = pl.pallas_call(
        _gdn_scan_kernel,
        out_shape=(jax.ShapeDtypeStruct((m, GDN_VAL_WIDTH), F32), jax.ShapeDtypeStruct((m, GDN_VAL_WIDTH), F32)),
        grid=(batch, nb),
        in_specs=[fwd(GDN_QKV_WIDTH, 0), fwd(LANES, small_col), bwd(GDN_QKV_WIDTH, 0), bwd(LANES, small_col),
                  whole((1, LANES)), whole((1, LANES)), whole((2, LANES, GDN_VAL_WIDTH)),
                  whole((2, LANES, GDN_VAL_WIDTH))],
        out_specs=(fwd(GDN_VAL_WIDTH, 0), bwd(GDN_VAL_WIDTH, 0)),
        scratch_shapes=[pltpu.VMEM((GDN_HEADS, GDN_HEAD_K, GDN_HEAD_V), F32),
                        pltpu.VMEM((GDN_HEADS, GDN_HEAD_K, GDN_HEAD_V), F32)],
        compiler_params=_params("parallel", "arbitrary"),
        name="gdn_scan",
    )(qkv, pf, qkv, pf, a_scale, dtb, jnp.asarray(sel[0]), jnp.asarray(sel[1]))
    return finish_branch(o_f, o_b, pb, "gdn_og", norm_gain, silu_gate=True)


def _rms_norm(x, gain, eps=RMS_EPS):
    xf = x.astype(F32)
    y = xf * lax.rsqrt(jnp.mean(xf * xf, axis=-1, keepdims=True) + eps)
    return (y * gain.astype(F32)).astype(x.dtype)


def _l2_norm(x, eps=1e-6):
    xf = x.astype(F32)
    return xf * lax.rsqrt(jnp.sum(xf * xf, axis=-1, keepdims=True) + eps)


def _rev(t):
    return jnp.flip(t, axis=1)


def _centred_depthwise_conv(x, w):
    width = w.shape[0]
    return lax.conv_general_dilated(
        x, w[:, None, :], window_strides=(1,), padding=[(width // 2, width // 2)],
        dimension_numbers=("NWC", "WIO", "NWC"), feature_group_count=x.shape[-1])


def _chunk_gla(q, k, v, log_g):
    B, S, H, K = q.shape
    V = v.shape[-1]
    C = LIN_CHUNK
    n = S // C

    def chunks(t):
        return t.reshape(B, n, C, H, t.shape[-1]).transpose(1, 0, 3, 2, 4)

    q, k, v, log_g = chunks(q), chunks(k), chunks(v), chunks(log_g)
    b = jnp.cumsum(log_g, axis=-2)
    b_ref = b[..., C // 2:C // 2 + 1, :]
    incl = jnp.tril(jnp.ones((C, C), dtype=bool))
    scores = jnp.einsum("nbhik,nbhjk->nbhij", q * jnp.exp(b - b_ref), k * jnp.exp(b_ref - b))
    o_intra = jnp.einsum("nbhij,nbhjv->nbhiv", jnp.where(incl, scores, 0.0), v)
    q_dec = q * jnp.exp(b)
    k_dec = k * jnp.exp(b[..., -1:, :] - b)
    g_tot = jnp.exp(b[..., -1, :])

    def step(state, xs):
        q_c, k_c, v_c, g_c = xs
        o_c = jnp.einsum("bhik,bhkv->bhiv", q_c, state)
        state = state * g_c[..., None] + jnp.einsum("bhjk,bhjv->bhkv", k_c, v_c)
        return state, o_c

    _, o_inter = lax.scan(step, jnp.zeros((B, H, K, V), F32), (q_dec, k_dec, v, g_tot))
    o = o_intra + o_inter
    return o.transpose(1, 0, 3, 2, 4).reshape(B, S, H, V)


def _chunk_gdn(q, k, v, log_alpha, beta):
    B, S, H, K = q.shape
    V = v.shape[-1]
    C = GDN_CHUNK
    n = S // C

    def chunks(t):
        return t.reshape(B, n, C, H, t.shape[-1]).transpose(1, 0, 3, 2, 4)

    q, k, v = chunks(q), chunks(k), chunks(v)
    g = jnp.cumsum(chunks(log_alpha[..., None])[..., 0], axis=-1)
    beta = chunks(beta[..., None])
    incl = jnp.tril(jnp.ones((C, C), dtype=bool))
    strict = jnp.tril(jnp.ones((C, C), dtype=bool), -1)
    diff = g[..., :, None] - g[..., None, :]
    decay = jnp.where(incl, jnp.exp(jnp.where(incl, diff, 0.0)), 0.0)
    k_beta = k * beta
    a = jnp.where(strict, jnp.einsum("nbhik,nbhjk->nbhij", k_beta, k) * decay, 0.0)
    rhs = jnp.concatenate([v * beta, k_beta * jnp.exp(g)[..., None]], axis=-1)
    sol = lax.linalg.triangular_solve(a + jnp.eye(C, dtype=F32), rhs, left_side=True, lower=True)
    u, w = sol[..., :V], sol[..., V:]
    attn = jnp.einsum("nbhik,nbhjk->nbhij", q, k) * decay
    q_dec = q * jnp.exp(g)[..., None]
    k_dec = k * jnp.exp(g[..., -1:] - g)[..., None]
    g_tot = jnp.exp(g[..., -1])

    def step(state, xs):
        u_c, w_c, attn_c, q_c, k_c, g_c = xs
        v_new = u_c - jnp.einsum("bhck,bhkv->bhcv", w_c, state)
        o_c = jnp.einsum("bhck,bhkv->bhcv", q_c, state) + jnp.einsum("bhij,bhjv->bhiv", attn_c, v_new)
        state = state * g_c[..., None, None] + jnp.einsum("bhck,bhcv->bhkv", k_c, v_new)
        return state, o_c

    _, o = lax.scan(step, jnp.zeros((B, H, K, V), F32), (u, w, attn, q_dec, k_dec, g_tot))
    return o.transpose(1, 0, 3, 2, 4).reshape(B, S, H, V)


def _neighbourhood_attention(q, k, v, q_gain, k_gain, rel_bias):
    B, S, _ = q.shape
    rows = S // GRID_W
    win_rows = min(NA_WIN_ROWS, rows)

    def grid(t):
        return t.reshape(B, rows, GRID_W, NA_HEADS, NA_HEAD_DIM)

    q = _rms_norm(grid(q), q_gain).astype(F32) * (NA_HEAD_DIM ** -0.5)
    k = _rms_norm(grid(k), k_gain).astype(F32)
    v = grid(v).astype(F32)
    r = jnp.arange(rows)
    c = jnp.arange(GRID_W)
    row_idx = jnp.clip(r - win_rows // 2, 0, rows - win_rows)[:, None] + jnp.arange(win_rows)[None, :]
    col_start = jnp.clip(c - NA_WIN_COLS // 2, 0, GRID_W - NA_WIN_COLS)
    col_in = (c[None, :] >= col_start[:, None]) & (c[None, :] < col_start[:, None] + NA_WIN_COLS)
    k_band = k[:, row_idx]
    v_band = v[:, row_idx]
    s = jnp.einsum("brqhd,brikhd->bhrqik", q, k_band)
    dr = row_idx - r[:, None] + (NA_WIN_ROWS - 1)
    dc = jnp.clip(c[None, :] - c[:, None], 1 - NA_WIN_COLS, NA_WIN_COLS - 1) + (NA_WIN_COLS - 1)
    bias = rel_bias.astype(F32)[:, dr[:, None, :, None], dc[None, :, None, :]]
    s = jnp.where(col_in[:, None, :], s + bias[None], MASK_VALUE)
    p = jax.nn.softmax(s, axis=(-2, -1))
    o = jnp.einsum("bhrqik,brikhd->brqhd", p, v_band)
    return o.reshape(B, S, NA_WIDTH)


def _gla_branch(q, k, v, gate_lr, out_gate, w_gate_up, b_gate, norm_gain):
    B, S, _ = q.shape
    q = q.astype(F32).reshape(B, S, GLA_HEADS, GLA_HEAD_K) * (GLA_HEAD_K ** -0.5)
    k = k.astype(F32).reshape(B, S, GLA_HEADS, GLA_HEAD_K)
    v = v.astype(F32).reshape(B, S, GLA_HEADS, GLA_HEAD_V)
    lr = gate_lr.astype(F32).reshape(B, S, 2, GLA_GATE_RANK)
    gk = jnp.einsum("bsdr,drk->bsdk", lr, w_gate_up.astype(F32)) + b_gate.astype(F32)
    log_g = (jax.nn.log_sigmoid(gk) / GLA_GATE_NORMALIZER).reshape(B, S, 2, GLA_HEADS, GLA_HEAD_K)
    o = (_chunk_gla(q, k, v, log_g[:, :, 0])
         + _rev(_chunk_gla(_rev(q), _rev(k), _rev(v), _rev(log_g[:, :, 1]))))
    o = _rms_norm(o, norm_gain) * jax.nn.silu(out_gate.astype(F32)).reshape(B, S, GLA_HEADS, GLA_HEAD_V)
    return o.reshape(B, S, GLA_VAL_WIDTH)


def _gdn_branch(qkv, a, b, out_gate, conv_w, a_log, dt_bias, norm_gain):
    B, S, _ = qkv.shape
    qkv = jax.nn.silu(_centred_depthwise_conv(qkv.astype(F32), conv_w.astype(F32)))
    q, k, v = jnp.split(qkv, [GDN_KEY_WIDTH, 2 * GDN_KEY_WIDTH], axis=-1)
    q = _l2_norm(q.reshape(B, S, GDN_HEADS, GDN_HEAD_K)) * (GDN_HEAD_K ** -0.5)
    k = _l2_norm(k.reshape(B, S, GDN_HEADS, GDN_HEAD_K))
    v = v.reshape(B, S, GDN_HEADS, GDN_HEAD_V)
    a = a.astype(F32).reshape(B, S, 2, GDN_HEADS)
    b = b.astype(F32).reshape(B, S, 2, GDN_HEADS)
    log_alpha = -jnp.exp(a_log.astype(F32)) * jax.nn.softplus(a + dt_bias.astype(F32))
    beta = jax.nn.sigmoid(b)
    o = (_chunk_gdn(q, k, v, log_alpha[:, :, 0], beta[:, :, 0])
         + _rev(_chunk_gdn(_rev(q), _rev(k), _rev(v), _rev(log_alpha[:, :, 1]), _rev(beta[:, :, 1]))))
    o = _rms_norm(o, norm_gain) * jax.nn.silu(out_gate.astype(F32)).reshape(B, S, GDN_HEADS, GDN_HEAD_V)
    return o.reshape(B, S, GDN_VAL_WIDTH)


def _hgrn2_branch(q, f_pre, i, out_gate, lower_bound, norm_gain):
    B, S, _ = q.shape
    q = jax.nn.silu(q.astype(F32)).reshape(B, S, HGRN_HEADS, HGRN_HEAD_K)
    z = f_pre.astype(F32).reshape(B, S, 2, HGRN_KEY_WIDTH)
    lb = lower_bound.astype(F32)
    log_f = jnp.logaddexp(jnp.log(jnp.maximum(lb, LB_FLOOR)), jnp.log1p(-lb) + jax.nn.log_sigmoid(z))
    k_in = (1.0 - lb) * jax.nn.sigmoid(-z)
    log_f = log_f.reshape(B, S, 2, HGRN_HEADS, HGRN_HEAD_K)
    k_in = k_in.reshape(B, S, 2, HGRN_HEADS, HGRN_HEAD_K)
    v = i.astype(F32).reshape(B, S, HGRN_HEADS, HGRN_HEAD_V)
    o = (_chunk_gla(q, k_in[:, :, 0], v, log_f[:, :, 0])
         + _rev(_chunk_gla(_rev(q), _rev(k_in[:, :, 1]), _rev(v), _rev(log_f[:, :, 1]))))
    o = _rms_norm(o, norm_gain) * jax.nn.sigmoid(out_gate.astype(F32)).reshape(B, S, HGRN_HEADS, HGRN_HEAD_V)
    return o.reshape(B, S, HGRN_VAL_WIDTH)


def _memory_cross_attention(q, kv, q_gain, k_gain):
    B, S, _ = q.shape
    M = kv.shape[1]
    q = _rms_norm(q.reshape(B, S, MEM_HEADS, MEM_HEAD_DIM), q_gain).astype(F32)
    k, v = jnp.split(kv, 2, axis=-1)
    k = _rms_norm(k.reshape(B, M, MEM_HEADS, MEM_HEAD_DIM), k_gain).astype(F32)
    v = v.reshape(B, M, MEM_HEADS, MEM_HEAD_DIM).astype(F32)
    s = jnp.einsum("bshd,bmhd->bhsm", q, k) * (MEM_HEAD_DIM ** -0.5)
    p = jax.nn.softmax(s, axis=-1)
    o = jnp.einsum("bhsm,bmhd->bshd", p, v)
    return o.reshape(B, S, MEM_WIDTH)


def kernel(x, mem, g_mix, w_in, na_q_gain, na_k_gain, na_rel_bias, gla_w_gate_up, gla_b_gate, gla_norm_gain, gdn_conv_w, gdn_a_log, gdn_dt_bias, gdn_norm_gain, hgrn_lb_raw, hgrn_norm_gain, g_mem, w_mem_kv, mem_q_gain, mem_k_gain, w_branch, w_out, g_ffn, ffn_w_gate, ffn_w_up, ffn_w_down, moe_w_router, moe_b_router, moe_w_gate, moe_w_up, moe_w_down):
    B, S, D = x.shape
    n_tok = B * S
    lb_w = jax.nn.softmax(hgrn_lb_raw.astype(F32), axis=0)
    hgrn_lb = jnp.cumsum(lb_w, axis=0) - lb_w[0:1]
    x2 = x.reshape(n_tok, D)
    mem2 = mem.reshape(B * mem.shape[1], D)
    for layer in range(DEPTH):
        wb, wf = _split_w_in(w_in[layer])
        pb = rms_matmul(x2, g_mix[layer], wb, tm=1024, tn=512, out_dtype=BF16)
        pf = rms_matmul(x2, g_mix[layer], wf, tm=1024, tn=PF_WIDTH // 3, out_dtype=F32)
        kv = rms_matmul(mem2, g_mem[layer], w_mem_kv[layer].astype(BF16), tm=mem2.shape[0], tn=512,
                        out_dtype=BF16)
        branches = [
            neighbourhood_attention(pb, B, na_q_gain[layer], na_k_gain[layer], na_rel_bias[layer]),
            gla_branch(pb, pf, B, gla_w_gate_up[layer], gla_b_gate[layer], gla_norm_gain[layer]),
            gated_deltanet_branch(pb, pf, B, gdn_conv_w[layer], gdn_a_log[layer], gdn_dt_bias[layer],
                                  gdn_norm_gain[layer]),
            hgrn2_branch(pb, pf, B, hgrn_lb[layer], hgrn_norm_gain[layer]),
            memory_cross_attention(pb, kv, B, mem_q_gain[layer], mem_k_gain[layer]),
        ]
        merged = merge_branches(branches, pb, PB_COL["gates"], w_branch[layer].astype(BF16), tm=1024, tn=512)
        x2 = matmul_residual(merged, w_out[layer].astype(BF16), x2, tm=1024, tn=512)

        j = layer // 2
        if layer % 2 == 0:
            act = rms_swiglu_up(x2, g_ffn[layer], ffn_w_gate[j].astype(BF16), ffn_w_up[j].astype(BF16),
                                tm=1024, tn=512)
            x2 = matmul_residual(act, ffn_w_down[j].astype(BF16), x2, tm=512, tn=512)
        else:
            x2 = moe_layer(x2, g_ffn[layer], moe_w_router[j], moe_b_router[j], moe_w_gate[j].astype(BF16),
                           moe_w_up[j].astype(BF16), moe_w_down[j].astype(BF16))
    return x2.reshape(B, S, D)
```

```python
import functools

import jax
import jax.numpy as jnp
import numpy as np
from jax import lax
from jax.experimental import pallas as pl
from jax.experimental.pallas import tpu as pltpu

F32 = jnp.float32
BF16 = jnp.bfloat16

D_MODEL = 2048
DEPTH = 2
RMS_EPS = 1e-6
MASK_VALUE = -1e30
LB_FLOOR = 1e-30
GRID_W = 64

NA_HEADS = 8
NA_HEAD_DIM = 64
NA_WIDTH = 512
NA_WIN_ROWS = 8
NA_WIN_COLS = 16

GLA_HEADS = 4
GLA_HEAD_K = 64
GLA_HEAD_V = 128
GLA_KEY_WIDTH = 256
GLA_VAL_WIDTH = 512
GLA_GATE_RANK = 16
GLA_GATE_NORMALIZER = 16.0

GDN_HEADS = 4
GDN_HEAD_K = 128
GDN_HEAD_V = 128
GDN_KEY_WIDTH = 512
GDN_VAL_WIDTH = 512
GDN_CHUNK = 64

HGRN_HEADS = 4
HGRN_HEAD_K = 128
HGRN_HEAD_V = 128
HGRN_KEY_WIDTH = 512
HGRN_VAL_WIDTH = 512

LIN_CHUNK = 32

MEM_HEADS = 4
MEM_HEAD_DIM = 128
MEM_WIDTH = 512

N_BRANCH = 5
BRANCH_WIDTH = 512
N_EXPERTS = 8
MOE_TOP_K = 2

IN_WIDTHS = (
    NA_WIDTH, NA_WIDTH, NA_WIDTH,
    GLA_KEY_WIDTH, GLA_KEY_WIDTH, GLA_VAL_WIDTH,
    2 * GLA_GATE_RANK, GLA_VAL_WIDTH,
    2 * GDN_KEY_WIDTH + GDN_VAL_WIDTH,
    2 * GDN_HEADS, 2 * GDN_HEADS, GDN_VAL_WIDTH,
    HGRN_KEY_WIDTH, 2 * HGRN_KEY_WIDTH, HGRN_VAL_WIDTH, HGRN_VAL_WIDTH,
    MEM_WIDTH,
    N_BRANCH * D_MODEL,
)
P_IN = sum(IN_WIDTHS)

V7X_VMEM_BYTES = 64 * 1024 * 1024
VMEM_LIMIT_BYTES = V7X_VMEM_BYTES - 8 * 1024 * 1024
LANES = 128
MOE_TILE = 512


def _params(*semantics):
    return pltpu.CompilerParams(dimension_semantics=semantics, vmem_limit_bytes=VMEM_LIMIT_BYTES)


def _rms_norm_rows(x, gain):
    ms = jnp.mean(x * x, axis=-1, keepdims=True)
    return x * lax.rsqrt(ms + RMS_EPS) * gain


def _rms_matmul_kernel(x_ref, g_ref, w_ref, o_ref, h_ref):
    @pl.when(pl.program_id(1) == 0)
    def _():
        h_ref[...] = _rms_norm_rows(x_ref[...], g_ref[...]).astype(BF16)

    o_ref[...] = jnp.dot(h_ref[...], w_ref[...], preferred_element_type=F32).astype(o_ref.dtype)


def rms_matmul(x, gain, w, *, tm, tn, out_dtype=F32):
    m, k = x.shape
    n = w.shape[1]
    return pl.pallas_call(
        _rms_matmul_kernel,
        out_shape=jax.ShapeDtypeStruct((m, n), out_dtype),
        grid=(m // tm, n // tn),
        in_specs=[
            pl.BlockSpec((tm, k), lambda i, j: (i, 0)),
            pl.BlockSpec((1, k), lambda i, j: (0, 0)),
            pl.BlockSpec((k, tn), lambda i, j: (0, j)),
        ],
        out_specs=pl.BlockSpec((tm, tn), lambda i, j: (i, j)),
        scratch_shapes=[pltpu.VMEM((tm, k), BF16)],
        compiler_params=_params("parallel", "arbitrary"),
        name="rms_matmul",
    )(x, gain.reshape(1, k), w)


def _rms_swiglu_kernel(x_ref, g_ref, wg_ref, wu_ref, o_ref, h_ref):
    @pl.when(pl.program_id(1) == 0)
    def _():
        h_ref[...] = _rms_norm_rows(x_ref[...], g_ref[...]).astype(BF16)

    h = h_ref[...]
    a = jnp.dot(h, wg_ref[...], preferred_element_type=F32)
    b = jnp.dot(h, wu_ref[...], preferred_element_type=F32)
    o_ref[...] = (a * jax.nn.sigmoid(a) * b).astype(o_ref.dtype)


def rms_swiglu_up(x, gain, wg, wu, *, tm, tn):
    m, k = x.shape
    n = wg.shape[1]
    return pl.pallas_call(
        _rms_swiglu_kernel,
        out_shape=jax.ShapeDtypeStruct((m, n), BF16),
        grid=(m // tm, n // tn),
        in_specs=[
            pl.BlockSpec((tm, k), lambda i, j: (i, 0)),
            pl.BlockSpec((1, k), lambda i, j: (0, 0)),
            pl.BlockSpec((k, tn), lambda i, j: (0, j)),
            pl.BlockSpec((k, tn), lambda i, j: (0, j)),
        ],
        out_specs=pl.BlockSpec((tm, tn), lambda i, j: (i, j)),
        scratch_shapes=[pltpu.VMEM((tm, k), BF16)],
        compiler_params=_params("parallel", "arbitrary"),
        name="rms_swiglu_up",
    )(x, gain.reshape(1, k), wg, wu)


def _matmul_residual_kernel(a_ref, w_ref, r_ref, o_ref):
    o_ref[...] = r_ref[...] + jnp.dot(a_ref[...], w_ref[...], preferred_element_type=F32)


def matmul_residual(a, w, res, *, tm, tn):
    m, k = a.shape
    n = w.shape[1]
    return pl.pallas_call(
        _matmul_residual_kernel,
        out_shape=jax.ShapeDtypeStruct((m, n), F32),
        grid=(m // tm, n // tn),
        in_specs=[
            pl.BlockSpec((tm, k), lambda i, j: (i, 0)),
            pl.BlockSpec((k, tn), lambda i, j: (0, j)),
            pl.BlockSpec((tm, tn), lambda i, j: (i, j)),
        ],
        out_specs=pl.BlockSpec((tm, tn), lambda i, j: (i, j)),
        compiler_params=_params("parallel", "arbitrary"),
        name="matmul_residual",
    )(a, w, res)


def _merge_kernel(*refs):
    br_refs = refs[:N_BRANCH]
    gl_refs = refs[N_BRANCH:2 * N_BRANCH]
    wb_ref = refs[2 * N_BRANCH]
    o_ref = refs[2 * N_BRANCH + 1]
    acc = None
    for n in range(N_BRANCH):
        y = jnp.dot(br_refs[n][...], wb_ref[n], preferred_element_type=F32)
        t = jax.nn.sigmoid(gl_refs[n][...].astype(F32)) * y
        acc = t if acc is None else acc + t
    o_ref[...] = acc.astype(o_ref.dtype)


def merge_branches(branches, gate_logits, gate_col0, w_branch, *, tm, tn):
    m = branches[0].shape[0]
    d = D_MODEL
    tiles_per_branch = d // tn
    tile0 = gate_col0 // tn
    in_specs = [pl.BlockSpec((tm, BRANCH_WIDTH), lambda i, j: (i, 0)) for _ in range(N_BRANCH)]
    in_specs += [
        pl.BlockSpec((tm, tn), functools.partial(lambda i, j, n: (i, tile0 + n * tiles_per_branch + j), n=n))
        for n in range(N_BRANCH)
    ]
    in_specs += [pl.BlockSpec((N_BRANCH, BRANCH_WIDTH, tn), lambda i, j: (0, 0, j))]
    return pl.pallas_call(
        _merge_kernel,
        out_shape=jax.ShapeDtypeStruct((m, d), BF16),
        grid=(m // tm, d // tn),
        in_specs=in_specs,
        out_specs=pl.BlockSpec((tm, tn), lambda i, j: (i, j)),
        compiler_params=_params("parallel", "arbitrary"),
        name="merge_branches",
    )(*branches, *([gate_logits] * N_BRANCH), w_branch)


def _router_kernel(x_ref, g_ref, w_ref, b_ref, o_ref):
    h = _rms_norm_rows(x_ref[...], g_ref[...])
    o_ref[...] = jnp.dot(h, w_ref[...], preferred_element_type=F32,
                         precision=lax.Precision.HIGHEST) + b_ref[...]


def router_logits(x, gain, w_router, b_router, *, tm):
    m, k = x.shape
    e = w_router.shape[1]
    w_pad = jnp.zeros((k, LANES), F32).at[:, :e].set(w_router.astype(F32))
    b_pad = jnp.zeros((1, LANES), F32).at[0, :e].set(b_router.astype(F32))
    out = pl.pallas_call(
        _router_kernel,
        out_shape=jax.ShapeDtypeStruct((m, LANES), F32),
        grid=(m // tm,),
        in_specs=[
            pl.BlockSpec((tm, k), lambda i: (i, 0)),
            pl.BlockSpec((1, k), lambda i: (0, 0)),
            pl.BlockSpec((k, LANES), lambda i: (0, 0)),
            pl.BlockSpec((1, LANES), lambda i: (0, 0)),
        ],
        out_specs=pl.BlockSpec((tm, LANES), lambda i: (i, 0)),
        compiler_params=_params("parallel"),
        name="router_logits",
    )(x, gain.reshape(1, k), w_pad, b_pad)
    return out[:, :e]


def _moe_kernel(tile_e_ref, n_used_ref, x_ref, wg_ref, wu_ref, wd_ref, o_ref, acc_ref):
    i = pl.program_id(0)
    j = pl.program_id(1)
    last = pl.num_programs(1) - 1
    used = i < n_used_ref[0]

    @pl.when(jnp.logical_and(used, j == 0))
    def _():
        acc_ref[...] = jnp.zeros_like(acc_ref)

    @pl.when(used)
    def _():
        x = x_ref[...]
        a = jnp.dot(x, wg_ref[0], preferred_element_type=F32)
        b = jnp.dot(x, wu_ref[0], preferred_element_type=F32)
        act = (a * jax.nn.sigmoid(a) * b).astype(BF16)
        acc_ref[...] += jnp.dot(act, wd_ref[0], preferred_element_type=F32)

    @pl.when(jnp.logical_and(used, j == last))
    def _():
        o_ref[...] = acc_ref[...]

    @pl.when(jnp.logical_and(jnp.logical_not(used), j == last))
    def _():
        o_ref[...] = jnp.zeros_like(o_ref)


def moe_experts(xb, tile_e, n_used, wg, wu, wd, *, tm, tf):
    rows, d = xb.shape
    ff = wg.shape[2]
    n_tiles = rows // tm

    def x_map(i, j, te, nu):
        return (jnp.minimum(i, nu[0] - 1), 0)

    def up_map(i, j, te, nu):
        keep = i < nu[0]
        return (te[i], 0, jnp.where(keep, j, ff // tf - 1))

    def down_map(i, j, te, nu):
        keep = i < nu[0]
        return (te[i], jnp.where(keep, j, ff // tf - 1), 0)

    grid_spec = pltpu.PrefetchScalarGridSpec(
        num_scalar_prefetch=2,
        grid=(n_tiles, ff // tf),
        in_specs=[
            pl.BlockSpec((tm, d), x_map),
            pl.BlockSpec((1, d, tf), up_map),
            pl.BlockSpec((1, d, tf), up_map),
            pl.BlockSpec((1, tf, d), down_map),
        ],
        out_specs=pl.BlockSpec((tm, d), lambda i, j, te, nu: (i, 0)),
        scratch_shapes=[pltpu.VMEM((tm, d), F32)],
    )
    return pl.pallas_call(
        _moe_kernel,
        out_shape=jax.ShapeDtypeStruct((rows, d), F32),
        grid_spec=grid_spec,
        compiler_params=_params("arbitrary", "arbitrary"),
        name="moe_experts",
    )(tile_e, n_used, xb, wg, wu, wd)


def moe_layer(x2d, gain, w_router, b_router, wg, wu, wd):
    n, d = x2d.shape
    e = N_EXPERTS
    logits = router_logits(x2d, gain, w_router, b_router, tm=1024)
    top_logits, top_e = lax.top_k(logits, MOE_TOP_K)
    top_w = jax.nn.softmax(top_logits, axis=-1)
    nk = n * MOE_TOP_K
    n_tiles = -(-nk // MOE_TILE) + e
    flat_e = top_e.reshape(nk).astype(jnp.int32)
    flat_tok = jnp.repeat(jnp.arange(n, dtype=jnp.int32), MOE_TOP_K)
    onehot = (flat_e[:, None] == jnp.arange(e, dtype=jnp.int32)[None, :]).astype(jnp.int32)
    rank = jnp.take_along_axis(jnp.cumsum(onehot, axis=0), flat_e[:, None], axis=1)[:, 0] - 1
    counts = jnp.sum(onehot, axis=0)
    padded = (counts + MOE_TILE - 1) // MOE_TILE * MOE_TILE
    pad_end = jnp.cumsum(padded)
    pad_start = pad_end - padded
    slot = (pad_start[flat_e] + rank).astype(jnp.int32)
    slot_tok = jnp.full((n_tiles * MOE_TILE,), n, jnp.int32).at[slot].set(flat_tok)
    tile_start = jnp.arange(n_tiles, dtype=jnp.int32) * MOE_TILE
    tile_e = jnp.minimum(jnp.searchsorted(pad_end, tile_start, side="right"), e - 1).astype(jnp.int32)
    n_used = (pad_end[-1] // MOE_TILE).astype(jnp.int32).reshape(1)
    tile_e = jnp.where(tile_start < pad_end[-1], tile_e, tile_e[jnp.maximum(n_used[0] - 1, 0)])

    h = rms_only(x2d, gain)
    h_pad = jnp.concatenate([h, jnp.zeros((1, d), h.dtype)], axis=0)
    xb = h_pad[slot_tok]
    yb = moe_experts(xb, tile_e, n_used, wg, wu, wd, tm=MOE_TILE, tf=512)
    y = yb[slot].reshape(n, MOE_TOP_K, d)
    return x2d + jnp.einsum("nkd,nk->nd", y, top_w.astype(y.dtype))


def _rms_kernel(x_ref, g_ref, o_ref):
    o_ref[...] = _rms_norm_rows(x_ref[...], g_ref[...]).astype(o_ref.dtype)


def rms_only(x, gain, *, tm=1024):
    m, k = x.shape
    return pl.pallas_call(
        _rms_kernel,
        out_shape=jax.ShapeDtypeStruct((m, k), BF16),
        grid=(m // tm,),
        in_specs=[pl.BlockSpec((tm, k), lambda i: (i, 0)), pl.BlockSpec((1, k), lambda i: (0, 0))],
        out_specs=pl.BlockSpec((tm, k), lambda i: (i, 0)),
        compiler_params=_params("parallel"),
        name="rms_only",
    )(x, gain.reshape(1, k))


_SRC = dict(zip(
    ("na_q", "na_k", "na_v", "gla_q", "gla_k", "gla_v", "gla_lr", "gla_og", "gdn_qkv", "gdn_a", "gdn_b",
     "gdn_og", "hg_q", "hg_f", "hg_i", "hg_og", "mem_q", "gates"),
    zip(np.cumsum((0,) + IN_WIDTHS[:-1]).tolist(), IN_WIDTHS)))
_PB_ORDER = ("na_q", "na_k", "na_v", "gla_q", "gla_k", "gla_v", "gla_og", "gdn_qkv", "gdn_og", "hg_q", "hg_i",
             "hg_og", "mem_q", "gates")
_PF_ORDER = ("hg_f", "gla_lr", "gdn_a", "gdn_b")
PB_COL = {}
_c = 0
for _name in _PB_ORDER:
    PB_COL[_name] = _c
    _c += _SRC[_name][1]
PB_WIDTH = _c
PF_COL = {}
_c = 0
for _name in _PF_ORDER:
    PF_COL[_name] = _c
    _c += _SRC[_name][1]
PF_WIDTH = -(-_c // LANES) * LANES
PF_SMALL_COL = PF_COL["gla_lr"]
GDN_A_LANE = PF_COL["gdn_a"] - PF_SMALL_COL
GDN_B_LANE = PF_COL["gdn_b"] - PF_SMALL_COL


def _split_w_in(w):
    wb = jnp.concatenate([w[:, _SRC[n][0]:_SRC[n][0] + _SRC[n][1]] for n in _PB_ORDER], axis=1)
    wf = jnp.concatenate([w[:, _SRC[n][0]:_SRC[n][0] + _SRC[n][1]] for n in _PF_ORDER], axis=1)
    wf = jnp.pad(wf, ((0, 0), (0, PF_WIDTH - wf.shape[1])))
    return wb.astype(BF16), wf.astype(BF16)


def _segment_rms(x, gain, seg_ones, seg_width):
    sq = x * x
    hi = sq.astype(BF16)
    lo = (sq - hi.astype(F32)).astype(BF16)
    ss = (jnp.dot(hi, seg_ones, preferred_element_type=F32)
          + jnp.dot(lo, seg_ones, preferred_element_type=F32))
    return x * lax.rsqrt(ss * (1.0 / seg_width) + RMS_EPS) * gain


NA_ROWS_PER_STEP = 8
NA_BAND = NA_WIN_ROWS * GRID_W


def _na_bias_table(rel_bias):
    c = np.arange(GRID_W)
    dc = np.clip(c[None, :] - c[:, None], 1 - NA_WIN_COLS, NA_WIN_COLS - 1) + (NA_WIN_COLS - 1)
    col_start = np.clip(c - NA_WIN_COLS // 2, 0, GRID_W - NA_WIN_COLS)
    col_in = (c[None, :] >= col_start[:, None]) & (c[None, :] < col_start[:, None] + NA_WIN_COLS)
    cfg = np.arange(NA_WIN_ROWS)[:, None]
    dr = np.arange(NA_WIN_ROWS)[None, :] - cfg + (NA_WIN_ROWS - 1)
    t = rel_bias.astype(F32)[:, dr][:, :, :, dc]
    t = jnp.where(col_in[None, None, None], t, MASK_VALUE)
    return t.transpose(1, 0, 3, 2, 4).reshape(NA_WIN_ROWS, NA_HEADS, GRID_W, NA_BAND)


def _na_kernel(q_ref, k_ref, v_ref, qg_ref, kg_ref, seg_ref, bias_ref, o_ref, kn_ref):
    step = pl.program_id(1)
    rows_total = k_ref.shape[0] // GRID_W
    seg = seg_ref[...]

    @pl.when(step == 0)
    def _():
        def norm_keys(t, carry):
            rows = pl.ds(pl.multiple_of(t * 256, 256), 256)
            kn_ref[rows, :] = _segment_rms(k_ref[rows, :].astype(F32), kg_ref[...], seg, NA_HEAD_DIM).astype(BF16)
            return carry
        lax.fori_loop(0, k_ref.shape[0] // 256, norm_keys, 0)

    lane = lax.broadcasted_iota(jnp.int32, (1, LANES), 1)
    low_half = lane < NA_HEAD_DIM

    def one_row(rr, carry):
        r = step * NA_ROWS_PER_STEP + rr
        row_start = jnp.clip(r - NA_WIN_ROWS // 2, 0, rows_total - NA_WIN_ROWS)
        cfg = r - row_start
        qrows = pl.ds(pl.multiple_of(rr * GRID_W, GRID_W), GRID_W)
        band = pl.ds(pl.multiple_of(row_start * GRID_W, GRID_W), NA_BAND)
        qn = (_segment_rms(q_ref[qrows, :].astype(F32), qg_ref[...], seg, NA_HEAD_DIM)
              * (NA_HEAD_DIM ** -0.5)).astype(BF16)
        for pair in range(NA_HEADS // 2):
            cols = slice(pair * LANES, (pair + 1) * LANES)
            qp = qn[:, cols]
            kp = kn_ref[band, cols]
            vp = v_ref[band, cols]
            outs = []
            for half in range(2):
                keep = low_half if half == 0 else jnp.logical_not(low_half)
                qm = jnp.where(keep, qp, jnp.zeros_like(qp))
                s = lax.dot_general(qm, kp, (((1,), (1,)), ((), ())), preferred_element_type=F32)
                s = s + bias_ref[cfg, 2 * pair + half]
                m = jnp.max(s, axis=-1, keepdims=True)
                e = jnp.exp(s - m)
                l = jnp.sum(e, axis=-1, keepdims=True)
                o = jnp.dot(e.astype(BF16), vp, preferred_element_type=F32)
                outs.append(o / l)
            o_ref[qrows, cols] = jnp.where(low_half, outs[0], outs[1]).astype(o_ref.dtype)
        return carry

    lax.fori_loop(0, NA_ROWS_PER_STEP, one_row, 0)


def neighbourhood_attention(pb, batch, q_gain, k_gain, rel_bias):
    m = pb.shape[0]
    s = m // batch
    tq = NA_ROWS_PER_STEP * GRID_W
    steps = s // tq
    qg = jnp.tile(q_gain.astype(F32), NA_HEADS).reshape(1, NA_WIDTH)
    kg = jnp.tile(k_gain.astype(F32), NA_HEADS).reshape(1, NA_WIDTH)
    seg = jnp.asarray(np.kron(np.eye(NA_HEADS), np.ones((NA_HEAD_DIM, NA_HEAD_DIM))), BF16)
    bias = _na_bias_table(rel_bias)
    cq, ck, cv = (PB_COL[n] // NA_WIDTH for n in ("na_q", "na_k", "na_v"))
    return pl.pallas_call(
        _na_kernel,
        out_shape=jax.ShapeDtypeStruct((m, NA_WIDTH), BF16),
        grid=(batch, steps),
        in_specs=[
            pl.BlockSpec((tq, NA_WIDTH), lambda b, t: (b * steps + t, cq)),
            pl.BlockSpec((s, NA_WIDTH), lambda b, t: (b, ck)),
            pl.BlockSpec((s, NA_WIDTH), lambda b, t: (b, cv)),
            pl.BlockSpec((1, NA_WIDTH), lambda b, t: (0, 0)),
            pl.BlockSpec((1, NA_WIDTH), lambda b, t: (0, 0)),
            pl.BlockSpec((NA_WIDTH, NA_WIDTH), lambda b, t: (0, 0)),
            pl.BlockSpec((NA_WIN_ROWS, NA_HEADS, GRID_W, NA_BAND), lambda b, t: (0, 0, 0, 0)),
        ],
        out_specs=pl.BlockSpec((tq, NA_WIDTH), lambda b, t: (b * steps + t, 0)),
        scratch_shapes=[pltpu.VMEM((s, NA_WIDTH), BF16)],
        compiler_params=_params("parallel", "arbitrary"),
        name="neighbourhood_attention",
    )(pb, pb, pb, qg, kg, seg, bias)


def _mem_attn_kernel(q_ref, kv_ref, qg_ref, kg_ref, o_ref, kn_ref):
    @pl.when(pl.program_id(1) == 0)
    def _():
        for h in range(MEM_HEADS):
            cols = slice(h * MEM_HEAD_DIM, (h + 1) * MEM_HEAD_DIM)
            kn_ref[:, cols] = _rms_norm_rows(kv_ref[:, cols].astype(F32), kg_ref[...]).astype(BF16)

    for h in range(MEM_HEADS):
        cols = slice(h * MEM_HEAD_DIM, (h + 1) * MEM_HEAD_DIM)
        qn = _rms_norm_rows(q_ref[:, cols].astype(F32), qg_ref[...]).astype(BF16)
        s = lax.dot_general(qn, kn_ref[:, cols], (((1,), (1,)), ((), ())),
                            preferred_element_type=F32) * (MEM_HEAD_DIM ** -0.5)
        m = jnp.max(s, axis=-1, keepdims=True)
        e = jnp.exp(s - m)
        l = jnp.sum(e, axis=-1, keepdims=True)
        vh = kv_ref[:, MEM_WIDTH + h * MEM_HEAD_DIM:MEM_WIDTH + (h + 1) * MEM_HEAD_DIM]
        o = jnp.dot(e.astype(BF16), vh, preferred_element_type=F32)
        o_ref[:, cols] = (o / l).astype(o_ref.dtype)


def memory_cross_attention(pb, kv, batch, q_gain, k_gain, *, tq=512):
    m = pb.shape[0]
    steps = m // batch // tq
    n_mem = kv.shape[0] // batch
    cq = PB_COL["mem_q"] // MEM_WIDTH
    return pl.pallas_call(
        _mem_attn_kernel,
        out_shape=jax.ShapeDtypeStruct((m, MEM_WIDTH), BF16),
        grid=(batch, steps),
        in_specs=[
            pl.BlockSpec((tq, MEM_WIDTH), lambda b, t: (b * steps + t, cq)),
            pl.BlockSpec((n_mem, 2 * MEM_WIDTH), lambda b, t: (b, 0)),
            pl.BlockSpec((1, MEM_HEAD_DIM), lambda b, t: (0, 0)),
            pl.BlockSpec((1, MEM_HEAD_DIM), lambda b, t: (0, 0)),
        ],
        out_specs=pl.BlockSpec((tq, MEM_WIDTH), lambda b, t: (b * steps + t, 0)),
        scratch_shapes=[pltpu.VMEM((n_mem, MEM_WIDTH), BF16)],
        compiler_params=_params("parallel", "arbitrary"),
        name="memory_cross_attention",
    )(pb, kv, q_gain.astype(F32).reshape(1, MEM_HEAD_DIM), k_gain.astype(F32).reshape(1, MEM_HEAD_DIM))


LIN_BLOCK = 512
HEAD_V = 128


def _log_sigmoid(x):
    return jnp.minimum(x, 0.0) - jnp.log1p(jnp.exp(-jnp.abs(x)))


def _logaddexp(a, b):
    return jnp.maximum(a, b) + jnp.log1p(jnp.exp(-jnp.abs(a - b)))


def _split_bf16(x, terms):
    parts = []
    for _ in range(terms):
        p = x.astype(BF16)
        parts.append(p)
        x = x - p.astype(F32)
    return parts


def _dot_f32(a, b):
    a_hi, a_lo = _split_bf16(a, 2)
    b_hi, b_lo = _split_bf16(b, 2)
    return (jnp.dot(a_hi, b_hi, preferred_element_type=F32)
            + (jnp.dot(a_hi, b_lo, preferred_element_type=F32) + jnp.dot(a_lo, b_hi, preferred_element_type=F32)))


def _cumsum_rows(mask, x):
    m = jnp.where(mask, 1.0, 0.0).astype(BF16)
    hi, mid, lo = _split_bf16(x, 3)
    return (jnp.dot(m, hi, preferred_element_type=F32)
            + (jnp.dot(m, mid, preferred_element_type=F32) + jnp.dot(m, lo, preferred_element_type=F32)))


def _lin_chunk(qc, kc, vc, lg, s_ref, *, reverse, heads):
    c, w = qc.shape
    dk = w // heads
    ii = lax.broadcasted_iota(jnp.int32, (c, c), 0)
    jj = lax.broadcasted_iota(jnp.int32, (c, c), 1)
    causal = (jj >= ii) if reverse else (jj <= ii)
    b = _cumsum_rows(causal, lg)
    mid = c - 1 - c // 2 if reverse else c // 2
    end = 0 if reverse else c - 1
    b_mid = b[mid:mid + 1, :]
    b_end = b[end:end + 1, :]
    qe = qc * jnp.exp(b - b_mid)
    ke = (kc * jnp.exp(b_mid - b)).astype(BF16)
    q_dec = qc * jnp.exp(b)
    k_dec = kc * jnp.exp(b_end - b)
    g_tot = jnp.exp(b_end)

    lane = lax.broadcasted_iota(jnp.int32, (1, w), 1)
    head_mask = [((lane >= h * dk) & (lane < (h + 1) * dk)).astype(F32) for h in range(heads)]
    q4 = jnp.concatenate([qe * head_mask[h] for h in range(heads)], axis=0).astype(BF16)
    q4d = jnp.concatenate([q_dec * head_mask[h] for h in range(heads)], axis=0).astype(BF16)
    k4 = jnp.concatenate([k_dec * head_mask[h] for h in range(heads)], axis=0).astype(BF16)
    v4 = jnp.concatenate([vc[:, h * HEAD_V:(h + 1) * HEAD_V] for h in range(heads)], axis=0)

    scores = lax.dot_general(q4, ke, (((1,), (1,)), ((), ())), preferred_element_type=F32)
    ri = lax.broadcasted_iota(jnp.int32, (heads * c, c), 0) % c
    cj = lax.broadcasted_iota(jnp.int32, (heads * c, c), 1)
    keep = (cj >= ri) if reverse else (cj <= ri)
    scores = jnp.where(keep, scores, 0.0).astype(BF16)
    state = s_ref[...]
    o_inter = lax.dot_general(q4d, state.astype(BF16), (((1,), (1,)), ((), ())),
                              preferred_element_type=F32)
    outs = []
    for h in range(heads):
        rows = slice(h * c, (h + 1) * c)
        o_intra = jnp.dot(scores[rows, :], vc[:, h * HEAD_V:(h + 1) * HEAD_V], preferred_element_type=F32)
        outs.append(o_intra + o_inter[rows, :])
    s_ref[...] = state * g_tot + lax.dot_general(v4, k4, (((0,), (0,)), ((), ())), preferred_element_type=F32)
    return jnp.concatenate(outs, axis=1)


def _gla_inputs(refs, rows, direction, params):
    q_ref, k_ref, v_ref, g_ref = refs
    wpad_ref, bias_ref = params
    qc = q_ref[rows, :].astype(F32) * (GLA_HEAD_K ** -0.5)
    kc = k_ref[rows, :].astype(F32)
    gk = _dot_f32(g_ref[rows, :], wpad_ref[direction]) + bias_ref[direction]
    lg = _log_sigmoid(gk) * (1.0 / GLA_GATE_NORMALIZER)
    return qc, kc, v_ref[rows, :], lg


def _hgrn_inputs(refs, rows, direction, params):
    q_ref, v_ref, z_ref = refs
    lb_ref, log_lb_ref, log1m_lb_ref = params
    qr = q_ref[rows, :].astype(F32)
    qc = qr * jax.nn.sigmoid(qr)
    z = z_ref[rows, :]
    lg = _logaddexp(log_lb_ref[direction], log1m_lb_ref[direction] + _log_sigmoid(z))
    kc = (1.0 - lb_ref[direction]) * jax.nn.sigmoid(-z)
    return qc, kc, v_ref[rows, :], lg


def _bidir_lin_kernel(*refs, load_inputs, n_in, n_params, heads, chunk):
    fwd_refs = refs[:n_in]
    bwd_refs = refs[n_in:2 * n_in]
    params = refs[2 * n_in:2 * n_in + n_params]
    of_ref, ob_ref, sf_ref, sb_ref = refs[2 * n_in + n_params:]

    @pl.when(pl.program_id(1) == 0)
    def _():
        sf_ref[...] = jnp.zeros_like(sf_ref)
        sb_ref[...] = jnp.zeros_like(sb_ref)

    n_chunks = of_ref.shape[0] // chunk

    def body(c, carry):
        rows = pl.ds(pl.multiple_of(c * chunk, chunk), chunk)
        of_ref[rows, :] = _lin_chunk(*load_inputs(fwd_refs, rows, 0, params), sf_ref, reverse=False, heads=heads)
        rows = pl.ds(pl.multiple_of((n_chunks - 1 - c) * chunk, chunk), chunk)
        ob_ref[rows, :] = _lin_chunk(*load_inputs(bwd_refs, rows, 1, params), sb_ref, reverse=True, heads=heads)
        return carry

    lax.fori_loop(0, n_chunks, body, 0)


def _bidir_lin_call(name, load_inputs, arrays, col_blocks, widths, params, batch, heads, key_width):
    m = arrays[0].shape[0]
    nb = m // batch // LIN_BLOCK
    out_w = heads * HEAD_V

    def spec(width, col, rev):
        if rev:
            return pl.BlockSpec((LIN_BLOCK, width), lambda b, t: (b * nb + nb - 1 - t, col))
        return pl.BlockSpec((LIN_BLOCK, width), lambda b, t: (b * nb + t, col))

    in_specs = [spec(w, c[0], False) for w, c in zip(widths, col_blocks)]
    in_specs += [spec(w, c[1], True) for w, c in zip(widths, col_blocks)]
    in_specs += [pl.BlockSpec(p.shape, functools.partial(lambda b, t, nd: (0,) * nd, nd=p.ndim)) for p in params]
    kern = functools.partial(_bidir_lin_kernel, load_inputs=load_inputs, n_in=len(arrays), n_params=len(params),
                             heads=heads, chunk=LIN_CHUNK)
    return pl.pallas_call(
        kern,
        out_shape=(jax.ShapeDtypeStruct((m, out_w), F32), jax.ShapeDtypeStruct((m, out_w), F32)),
        grid=(batch, nb),
        in_specs=in_specs,
        out_specs=(spec(out_w, 0, False), spec(out_w, 0, True)),
        scratch_shapes=[pltpu.VMEM((HEAD_V, key_width), F32), pltpu.VMEM((HEAD_V, key_width), F32)],
        compiler_params=_params("parallel", "arbitrary"),
        name=name,
    )(*arrays, *arrays, *params)


def _finish_kernel(of_ref, ob_ref, og_ref, gain_ref, o_ref, *, heads, silu_gate):
    for h in range(heads):
        cols = slice(h * HEAD_V, (h + 1) * HEAD_V)
        y = _rms_norm_rows(of_ref[:, cols] + ob_ref[:, cols], gain_ref[...])
        g = og_ref[:, cols].astype(F32)
        gate = jax.nn.sigmoid(g)
        if silu_gate:
            gate = g * gate
        o_ref[:, cols] = (y * gate).astype(o_ref.dtype)


def finish_branch(o_fwd, o_bwd, pb, og_name, gain, *, silu_gate, tm=512):
    m, w = o_fwd.shape
    heads = w // HEAD_V
    cg = PB_COL[og_name] // w
    return pl.pallas_call(
        functools.partial(_finish_kernel, heads=heads, silu_gate=silu_gate),
        out_shape=jax.ShapeDtypeStruct((m, w), BF16),
        grid=(m // tm,),
        in_specs=[
            pl.BlockSpec((tm, w), lambda i: (i, 0)),
            pl.BlockSpec((tm, w), lambda i: (i, 0)),
            pl.BlockSpec((tm, w), lambda i: (i, cg)),
            pl.BlockSpec((1, HEAD_V), lambda i: (0, 0)),
        ],
        out_specs=pl.BlockSpec((tm, w), lambda i: (i, 0)),
        compiler_params=_params("parallel"),
        name="finish_branch",
    )(o_fwd, o_bwd, pb, gain.astype(F32).reshape(1, HEAD_V))


def gla_branch(pb, pf, batch, w_gate_up, b_gate, norm_gain):
    wpad = jnp.zeros((2, LANES, GLA_KEY_WIDTH), F32)
    for d in range(2):
        wpad = wpad.at[d, d * GLA_GATE_RANK:(d + 1) * GLA_GATE_RANK, :].set(w_gate_up[d].astype(F32))
    bias = b_gate.astype(F32).reshape(2, 1, GLA_KEY_WIDTH)
    cols = [(PB_COL["gla_q"] // GLA_KEY_WIDTH,) * 2, (PB_COL["gla_k"] // GLA_KEY_WIDTH,) * 2,
            (PB_COL["gla_v"] // GLA_VAL_WIDTH,) * 2, (PF_SMALL_COL // LANES,) * 2]
    o_f, o_b = _bidir_lin_call("gla_scan", _gla_inputs, [pb, pb, pb, pf], cols,
                               [GLA_KEY_WIDTH, GLA_KEY_WIDTH, GLA_VAL_WIDTH, LANES], [wpad, bias],
                               batch, GLA_HEADS, GLA_KEY_WIDTH)
    return finish_branch(o_f, o_b, pb, "gla_og", norm_gain, silu_gate=True)


def hgrn2_branch(pb, pf, batch, lower_bound, norm_gain):
    lb = lower_bound.astype(F32).reshape(2, 1, HGRN_KEY_WIDTH)
    log_lb = jnp.log(jnp.maximum(lb, LB_FLOOR))
    log1m_lb = jnp.log1p(-lb)
    zc = PF_COL["hg_f"] // HGRN_KEY_WIDTH
    cols = [(PB_COL["hg_q"] // HGRN_KEY_WIDTH,) * 2, (PB_COL["hg_i"] // HGRN_VAL_WIDTH,) * 2, (zc, zc + 1)]
    o_f, o_b = _bidir_lin_call("hgrn2_scan", _hgrn_inputs, [pb, pb, pf], cols,
                               [HGRN_KEY_WIDTH, HGRN_VAL_WIDTH, HGRN_KEY_WIDTH], [lb, log_lb, log1m_lb],
                               batch, HGRN_HEADS, HGRN_KEY_WIDTH)
    return finish_branch(o_f, o_b, pb, "hg_og", norm_gain, silu_gate=False)


GDN_CONV_WIDTH = 5
GDN_QKV_WIDTH = 2 * GDN_KEY_WIDTH + GDN_VAL_WIDTH
GDN_HALO = 16


def _gdn_prep_kernel(prev_ref, cur_ref, next_ref, w_ref, o_ref, xp_ref, *, blocks_per_seq):
    i = pl.program_id(0)
    t = cur_ref.shape[0]
    pos = i % blocks_per_seq
    prev = prev_ref[...].astype(F32)
    nxt = next_ref[...].astype(F32)
    xp_ref[0:GDN_HALO, :] = jnp.where(pos == 0, jnp.zeros_like(prev), prev)
    xp_ref[GDN_HALO:GDN_HALO + t, :] = cur_ref[...].astype(F32)
    xp_ref[GDN_HALO + t:, :] = jnp.where(pos == blocks_per_seq - 1, jnp.zeros_like(nxt), nxt)
    half = GDN_CONV_WIDTH // 2
    for g in range(GDN_QKV_WIDTH // LANES):
        cols = slice(g * LANES, (g + 1) * LANES)
        acc = None
        for j in range(GDN_CONV_WIDTH):
            term = xp_ref[GDN_HALO - half + j:GDN_HALO - half + j + t, cols] * w_ref[j:j + 1, cols]
            acc = term if acc is None else acc + term
        y = acc * jax.nn.sigmoid(acc)
        if g < 2 * GDN_HEADS:
            y = y * lax.rsqrt(jnp.sum(y * y, axis=-1, keepdims=True) + 1e-6)
            if g < GDN_HEADS:
                y = y * (GDN_HEAD_K ** -0.5)
        o_ref[:, cols] = y.astype(o_ref.dtype)


def gdn_prep(pb, batch, conv_w, *, t=512):
    m = pb.shape[0]
    blocks_per_seq = m // batch // t
    halo_per_block = t // GDN_HALO
    col = PB_COL["gdn_qkv"] // GDN_QKV_WIDTH
    last_halo = m // GDN_HALO - 1
    return pl.pallas_call(
        functools.partial(_gdn_prep_kernel, blocks_per_seq=blocks_per_seq),
        out_shape=jax.ShapeDtypeStruct((m, GDN_QKV_WIDTH), BF16),
        grid=(m // t,),
        in_specs=[
            pl.BlockSpec((GDN_HALO, GDN_QKV_WIDTH), lambda i: (jnp.maximum(i * halo_per_block - 1, 0), col)),
            pl.BlockSpec((t, GDN_QKV_WIDTH), lambda i: (i, col)),
            pl.BlockSpec((GDN_HALO, GDN_QKV_WIDTH),
                         lambda i: (jnp.minimum((i + 1) * halo_per_block, last_halo), col)),
            pl.BlockSpec((GDN_CONV_WIDTH, GDN_QKV_WIDTH), lambda i: (0, 0)),
        ],
        out_specs=pl.BlockSpec((t, GDN_QKV_WIDTH), lambda i: (i, 0)),
        scratch_shapes=[pltpu.VMEM((t + 2 * GDN_HALO, GDN_QKV_WIDTH), F32)],
        compiler_params=_params("parallel"),
        name="gdn_prep",
    )(pb, pb, pb, conv_w.astype(F32))


def _unit_triangular_inverse(a):
    c = a.shape[0]
    ii = lax.broadcasted_iota(jnp.int32, (c, c), 0)
    jj = lax.broadcasted_iota(jnp.int32, (c, c), 1)
    eye = (ii == jj).astype(F32)

    def same_block(s):
        return (ii // s) == (jj // s)

    d = jnp.where(same_block(8), a, 0.0)
    d2 = _dot_f32(d, d)
    d4 = _dot_f32(d2, d2)
    t = _dot_f32(_dot_f32(eye - d, eye + d2), eye + d4)
    s = 8
    while s < c:
        e = jnp.where(same_block(2 * s) & jnp.logical_not(same_block(s)), a, 0.0)
        t = t - _dot_f32(t, _dot_f32(e, t))
        s *= 2
    return t


def _softplus(x):
    return jnp.maximum(x, 0.0) + jnp.log1p(jnp.exp(-jnp.abs(x)))


def _gdn_chunk(qkv, small, a_scale, dt_bias, s_ref, *, direction):
    c = qkv.shape[0]
    reverse = direction == 1
    ii = lax.broadcasted_iota(jnp.int32, (c, c), 0)
    jj = lax.broadcasted_iota(jnp.int32, (c, c), 1)
    incl = (jj >= ii) if reverse else (jj <= ii)
    strict = (jj > ii) if reverse else (jj < ii)
    end = 0 if reverse else c - 1

    log_alpha = a_scale * _softplus(small + dt_bias)
    g_all = _cumsum_rows(incl, log_alpha)
    g_all_t = jnp.concatenate([g_all, jnp.zeros((LANES - c, LANES), F32)], axis=0).T
    beta_all = jax.nn.sigmoid(small)

    outs = []
    for h in range(GDN_HEADS):
        hs = slice(h * HEAD_V, (h + 1) * HEAD_V)
        q = qkv[:, hs]
        k = qkv[:, GDN_KEY_WIDTH + h * HEAD_V:GDN_KEY_WIDTH + (h + 1) * HEAD_V]
        v = qkv[:, 2 * GDN_KEY_WIDTH + h * HEAD_V:2 * GDN_KEY_WIDTH + (h + 1) * HEAD_V].astype(F32)
        kf = k.astype(F32)
        lane_a = GDN_A_LANE + direction * GDN_HEADS + h
        lane_b = GDN_B_LANE + direction * GDN_HEADS + h
        gc = jnp.broadcast_to(g_all[:, lane_a:lane_a + 1], (c, HEAD_V))
        beta = jnp.broadcast_to(beta_all[:, lane_b:lane_b + 1], (c, HEAD_V))
        diff = gc[:, :c] - jnp.broadcast_to(g_all_t[lane_a:lane_a + 1, :c], (c, c))
        decay = jnp.where(incl, jnp.exp(jnp.where(incl, diff, 0.0)), 0.0)
        k_beta = kf * beta
        kk = lax.dot_general(k_beta.astype(BF16), k, (((1,), (1,)), ((), ())), preferred_element_type=F32)
        t_inv = _unit_triangular_inverse(jnp.where(strict, kk * decay, 0.0))
        eg = jnp.exp(gc)
        sol = _dot_f32(t_inv, jnp.concatenate([v * beta, k_beta * eg], axis=1))
        u, w = sol[:, :HEAD_V], sol[:, HEAD_V:]
        attn = lax.dot_general(q, k, (((1,), (1,)), ((), ())), preferred_element_type=F32) * decay
        g_end = gc[end:end + 1, :]
        q_dec = (q.astype(F32) * eg).astype(BF16)
        k_dec = (kf * jnp.exp(g_end - gc)).astype(BF16)
        state = s_ref[h]
        state_b = state.astype(BF16)
        v_new = u - jnp.dot(w.astype(BF16), state_b, preferred_element_type=F32)
        v_new_b = v_new.astype(BF16)
        outs.append(jnp.dot(q_dec, state_b, preferred_element_type=F32)
                    + jnp.dot(attn.astype(BF16), v_new_b, preferred_element_type=F32))
        s_ref[h] = state * jnp.exp(g_end) + lax.dot_general(k_dec, v_new_b, (((0,), (0,)), ((), ())),
                                                             preferred_element_type=F32)
    return jnp.concatenate(outs, axis=1)


def _gdn_scan_kernel(qf_ref, gf_ref, qb_ref, gb_ref, a_ref, dtb_ref, of_ref, ob_ref, sf_ref, sb_ref):
    @pl.when(pl.program_id(1) == 0)
    def _():
        sf_ref[...] = jnp.zeros_like(sf_ref)
        sb_ref[...] = jnp.zeros_like(sb_ref)

    n_chunks = of_ref.shape[0] // GDN_CHUNK

    def body(c, carry):
        rows = pl.ds(pl.multiple_of(c * GDN_CHUNK, GDN_CHUNK), GDN_CHUNK)
        of_ref[rows, :] = _gdn_chunk(qf_ref[rows, :], gf_ref[rows, :], a_ref[...], dtb_ref[...], sf_ref,
                                     direction=0)
        rows = pl.ds(pl.multiple_of((n_chunks - 1 - c) * GDN_CHUNK, GDN_CHUNK), GDN_CHUNK)
        ob_ref[rows, :] = _gdn_chunk(qb_ref[rows, :], gb_ref[rows, :], a_ref[...], dtb_ref[...], sb_ref,
                                     direction=1)
        return carry

    lax.fori_loop(0, n_chunks, body, 0)


def gated_deltanet_branch(pb, pf, batch, conv_w, a_log, dt_bias, norm_gain):
    m = pb.shape[0]
    nb = m // batch // LIN_BLOCK
    qkv = gdn_prep(pb, batch, conv_w)
    n_gate = 2 * GDN_HEADS
    a_scale = jnp.zeros((1, LANES), F32).at[0, GDN_A_LANE:GDN_A_LANE + n_gate].set(
        -jnp.exp(a_log.astype(F32)).reshape(n_gate))
    dtb = jnp.zeros((1, LANES), F32).at[0, GDN_A_LANE:GDN_A_LANE + n_gate].set(dt_bias.astype(F32).reshape(n_gate))
    small_col = PF_SMALL_COL // LANES

    def fwd(width, col):
        return pl.BlockSpec((LIN_BLOCK, width), lambda b, t: (b * nb + t, col))

    def bwd(width, col):
        return pl.BlockSpec((LIN_BLOCK, width), lambda b, t: (b * nb + nb - 1 - t, col))

    def whole(shape):
        return pl.BlockSpec(shape, functools.partial(lambda b, t, nd: (0,) * nd, nd=len(shape)))

    o_f, o_b = pl.pallas_call(
        _gdn_scan_kernel,
        out_shape=(jax.ShapeDtypeStruct((m, GDN_VAL_WIDTH), F32), jax.ShapeDtypeStruct((m, GDN_VAL_WIDTH), F32)),
        grid=(batch, nb),
        in_specs=[fwd(GDN_QKV_WIDTH, 0), fwd(LANES, small_col), bwd(GDN_QKV_WIDTH, 0), bwd(LANES, small_col),
                  whole((1, LANES)), whole((1, LANES))],
        out_specs=(fwd(GDN_VAL_WIDTH, 0), bwd(GDN_VAL_WIDTH, 0)),
        scratch_shapes=[pltpu.VMEM((GDN_HEADS, GDN_HEAD_K, GDN_HEAD_V), F32),
                        pltpu.VMEM((GDN_HEADS, GDN_HEAD_K, GDN_HEAD_V), F32)],
        compiler_params=_params("parallel", "arbitrary"),
        name="gdn_scan",
    )(qkv, pf, qkv, pf, a_scale, dtb)
    return finish_branch(o_f, o_b, pb, "gdn_og", norm_gain, silu_gate=True)


GDN_PACK = GDN_HEADS * GDN_CHUNK
GDN_WY_BLOCK = 512


def _stack_heads(x, width):
    heads = x.shape[1] // width
    lane = lax.broadcasted_iota(jnp.int32, (1, x.shape[1]), 1)
    return jnp.concatenate(
        [jnp.where((lane >= h * width) & (lane < (h + 1) * width), x, 0.0).astype(BF16) for h in range(heads)],
        axis=0)


def _packed_mm(x, y):
    return jnp.dot(x.astype(BF16), _stack_heads(y, GDN_CHUNK), preferred_element_type=F32)


def _packed_inverses(mats):
    c = GDN_CHUNK
    ii = lax.broadcasted_iota(jnp.int32, (c, GDN_PACK), 0)
    jj = lax.broadcasted_iota(jnp.int32, (c, GDN_PACK), 1) % c
    eye = (ii == jj).astype(F32)

    def same_block(s):
        return (ii // s) == (jj // s)

    ds = [jnp.where(same_block(8), a, 0.0) for a in mats]
    d2s = [_packed_mm(d, d) for d in ds]
    d4s = [_packed_mm(d2, d2) for d2 in d2s]
    ts = [_packed_mm(eye - d, eye + d2) for d, d2 in zip(ds, d2s)]
    ts = [_packed_mm(t, eye + d4) for t, d4 in zip(ts, d4s)]
    s = 8
    while s < c:
        off = same_block(2 * s) & jnp.logical_not(same_block(s))
        ets = [_packed_mm(jnp.where(off, a, 0.0), t) for a, t in zip(mats, ts)]
        ts = [t - _packed_mm(t, et) for t, et in zip(ts, ets)]
        s *= 2
    return ts


def _gdn_wy_kernel(qkv_ref, small_ref, a_ref, dtb_ref, selg_ref, selk_ref, selb_ref, *out_refs):
    c = GDN_CHUNK
    n_chunks = qkv_ref.shape[0] // c
    ii = lax.broadcasted_iota(jnp.int32, (c, c), 0)
    jj = lax.broadcasted_iota(jnp.int32, (c, c), 1)
    pi = lax.broadcasted_iota(jnp.int32, (c, GDN_PACK), 0)
    pj = lax.broadcasted_iota(jnp.int32, (c, GDN_PACK), 1) % c
    eye_p = (pi == pj).astype(F32)
    ones_cc = jnp.ones((c, c), BF16)

    problems = [(ch, d) for ch in range(n_chunks) for d in range(2)]
    chunk_in = []
    for ch in range(n_chunks):
        rows = slice(ch * c, (ch + 1) * c)
        qkv = qkv_ref[rows, :]
        small = small_ref[rows, :]
        kf = qkv[:, GDN_KEY_WIDTH:2 * GDN_KEY_WIDTH].astype(F32)
        chunk_in.append(dict(
            qf=qkv[:, :GDN_KEY_WIDTH].astype(F32), kf=kf, vf=qkv[:, 2 * GDN_KEY_WIDTH:].astype(F32),
            kbd=_stack_heads(kf, HEAD_V),
            log_alpha=a_ref[...] * _softplus(small + dtb_ref[...]),
            beta_all=jax.nn.sigmoid(small)))

    def sel3(x, sel):
        hi, mid, lo = _split_bf16(x, 3)
        return (jnp.dot(hi, sel, preferred_element_type=F32)
                + (jnp.dot(mid, sel, preferred_element_type=F32) + jnp.dot(lo, sel, preferred_element_type=F32)))

    g_all = [_cumsum_rows((jj >= ii) if d else (jj <= ii), chunk_in[ch]["log_alpha"]) for ch, d in problems]
    g_pack = [sel3(g, selg_ref[d]) for g, (ch, d) in zip(g_all, problems)]
    g_wide = [sel3(g, selk_ref[d]) for g, (ch, d) in zip(g_all, problems)]
    beta_w = [sel3(chunk_in[ch]["beta_all"], selb_ref[d]) for ch, d in problems]
    g_rowp = []
    for gp in g_pack:
        hi, mid, lo = _split_bf16(gp * eye_p, 3)
        g_rowp.append(jnp.dot(ones_cc, hi, preferred_element_type=F32)
                      + (jnp.dot(ones_cc, mid, preferred_element_type=F32)
                         + jnp.dot(ones_cc, lo, preferred_element_type=F32)))
    decays, k_betas = [], []
    for gp, gr, bw, (ch, d) in zip(g_pack, g_rowp, beta_w, problems):
        incl = (pj >= pi) if d else (pj <= pi)
        decays.append(jnp.where(incl, jnp.exp(jnp.where(incl, gp - gr, 0.0)), 0.0))
        k_betas.append(chunk_in[ch]["kf"] * bw)
    kq = [lax.dot_general(jnp.concatenate([kb, chunk_in[ch]["qf"]], axis=0).astype(BF16), chunk_in[ch]["kbd"],
                          (((1,), (1,)), ((), ())), preferred_element_type=F32)
          for kb, (ch, d) in zip(k_betas, problems)]
    a_mats = []
    for x, dec, (ch, d) in zip(kq, decays, problems):
        strict = (pj > pi) if d else (pj < pi)
        a_mats.append(jnp.where(strict, x[:c] * dec, 0.0))
    t_invs = _packed_inverses(a_mats)

    for idx, (ch, d) in enumerate(problems):
        u_ref, w_ref, attn_ref, qd_ref, kd_ref, gt_ref = out_refs[6 * d:6 * d + 6]
        rows = slice(ch * c, (ch + 1) * c)
        cin = chunk_in[ch]
        gw = g_wide[idx]
        eg = jnp.exp(gw)
        t_b = t_invs[idx].astype(BF16)
        u_ref[rows, :] = jnp.dot(t_b, _stack_heads(cin["vf"] * beta_w[idx], HEAD_V), preferred_element_type=F32)
        w_ref[rows, :] = jnp.dot(t_b, _stack_heads(k_betas[idx] * eg, HEAD_V),
                                 preferred_element_type=F32).astype(w_ref.dtype)
        attn_ref[rows, :] = (kq[idx][c:] * decays[idx]).astype(attn_ref.dtype)
        end = 0 if d else c - 1
        g_end = gw[end:end + 1, :]
        qd_ref[rows, :] = (cin["qf"] * eg).astype(qd_ref.dtype)
        kd_ref[rows, :] = (cin["kf"] * jnp.exp(g_end - gw)).astype(kd_ref.dtype)
        gt_ref[ch:ch + 1, :] = jnp.exp(g_end)


def gdn_wy(qkv, pf, a_scale, dtb):
    m = qkv.shape[0]
    t = GDN_WY_BLOCK
    cpb = t // GDN_CHUNK
    selg = np.zeros((2, LANES, GDN_PACK), np.float32)
    selk = np.zeros((2, LANES, GDN_VAL_WIDTH), np.float32)
    selb = np.zeros((2, LANES, GDN_VAL_WIDTH), np.float32)
    for d in range(2):
        for h in range(GDN_HEADS):
            selg[d, GDN_A_LANE + d * GDN_HEADS + h, h * GDN_CHUNK:(h + 1) * GDN_CHUNK] = 1.0
            selk[d, GDN_A_LANE + d * GDN_HEADS + h, h * HEAD_V:(h + 1) * HEAD_V] = 1.0
            selb[d, GDN_B_LANE + d * GDN_HEADS + h, h * HEAD_V:(h + 1) * HEAD_V] = 1.0
    wide = GDN_VAL_WIDTH
    out_shape, out_specs = [], []
    for _ in range(2):
        for width, dt in ((wide, F32), (wide, BF16), (GDN_PACK, BF16), (wide, BF16), (wide, BF16)):
            out_shape.append(jax.ShapeDtypeStruct((m, width), dt))
            out_specs.append(pl.BlockSpec((t, width), lambda i: (i, 0)))
        out_shape.append(jax.ShapeDtypeStruct((m // GDN_CHUNK, wide), F32))
        out_specs.append(pl.BlockSpec((cpb, wide), lambda i: (i, 0)))
    return pl.pallas_call(
        _gdn_wy_kernel,
        out_shape=tuple(out_shape),
        grid=(m // t,),
        in_specs=[
            pl.BlockSpec((t, GDN_QKV_WIDTH), lambda i: (i, 0)),
            pl.BlockSpec((t, LANES), lambda i: (i, PF_SMALL_COL // LANES)),
            pl.BlockSpec((1, LANES), lambda i: (0, 0)),
            pl.BlockSpec((1, LANES), lambda i: (0, 0)),
            pl.BlockSpec((2, LANES, GDN_PACK), lambda i: (0, 0, 0)),
            pl.BlockSpec((2, LANES, wide), lambda i: (0, 0, 0)),
            pl.BlockSpec((2, LANES, wide), lambda i: (0, 0, 0)),
        ],
        out_specs=tuple(out_specs),
        compiler_params=_params("parallel"),
        name="gdn_wy",
    )(qkv, pf, a_scale, dtb, jnp.asarray(selg, BF16), jnp.asarray(selk, BF16), jnp.asarray(selb, BF16))


GDN_PAIR = 2 * HEAD_V


def _gdn_state_step(u, w, attn, qd, kd, gt, s_ref):
    pairs = GDN_HEADS // 2
    ri = lax.broadcasted_iota(jnp.int32, (GDN_PAIR, GDN_PAIR), 0) // HEAD_V
    ci = lax.broadcasted_iota(jnp.int32, (GDN_PAIR, GDN_PAIR), 1) // HEAD_V
    diag = ri == ci
    ws, qs = [], []
    states = [s_ref[p] for p in range(pairs)]
    for p in range(pairs):
        cols = slice(p * GDN_PAIR, (p + 1) * GDN_PAIR)
        sb = states[p].astype(BF16)
        ws.append(jnp.dot(w[:, cols], sb, preferred_element_type=F32))
        qs.append(jnp.dot(qd[:, cols], sb, preferred_element_type=F32))
    v_new = u - jnp.concatenate(ws, axis=1)
    o = jnp.concatenate(qs, axis=1) + jnp.dot(attn, _stack_heads(v_new, HEAD_V), preferred_element_type=F32)
    v_new_b = v_new.astype(BF16)
    for p in range(pairs):
        cols = slice(p * GDN_PAIR, (p + 1) * GDN_PAIR)
        upd = lax.dot_general(kd[:, cols], v_new_b[:, cols], (((0,), (0,)), ((), ())), preferred_element_type=F32)
        s_ref[p] = states[p] * gt[:, cols] + jnp.where(diag, upd, 0.0)
    return o


def _gdn_scan2_kernel(*refs):
    fwd, bwd = refs[0:6], refs[6:12]
    of_ref, ob_ref, sf_ref, sb_ref = refs[12:]

    @pl.when(pl.program_id(1) == 0)
    def _():
        sf_ref[...] = jnp.zeros_like(sf_ref)
        sb_ref[...] = jnp.zeros_like(sb_ref)

    n_chunks = of_ref.shape[0] // GDN_CHUNK

    def load(group, ch):
        rows = pl.ds(pl.multiple_of(ch * GDN_CHUNK, GDN_CHUNK), GDN_CHUNK)
        return [r[rows, :] for r in group[:5]] + [group[5][pl.ds(ch, 1), :]], rows

    def body(c, carry):
        args, rows = load(fwd, c)
        of_ref[rows, :] = _gdn_state_step(*args, sf_ref)
        args, rows = load(bwd, n_chunks - 1 - c)
        ob_ref[rows, :] = _gdn_state_step(*args, sb_ref)
        return carry

    lax.fori_loop(0, n_chunks, body, 0)


def gated_deltanet_branch(pb, pf, batch, conv_w, a_log, dt_bias, norm_gain):
    m = pb.shape[0]
    nb = m // batch // LIN_BLOCK
    cpb = LIN_BLOCK // GDN_CHUNK
    qkv = gdn_prep(pb, batch, conv_w)
    n_gate = 2 * GDN_HEADS
    a_scale = jnp.zeros((1, LANES), F32).at[0, GDN_A_LANE:GDN_A_LANE + n_gate].set(
        -jnp.exp(a_log.astype(F32)).reshape(n_gate))
    dtb = jnp.zeros((1, LANES), F32).at[0, GDN_A_LANE:GDN_A_LANE + n_gate].set(dt_bias.astype(F32).reshape(n_gate))
    wy = gdn_wy(qkv, pf, a_scale, dtb)
    widths = (GDN_VAL_WIDTH, GDN_VAL_WIDTH, GDN_PACK, GDN_VAL_WIDTH, GDN_VAL_WIDTH)

    def fwd(rows, width):
        return pl.BlockSpec((rows, width), lambda b, t: (b * nb + t, 0))

    def bwd(rows, width):
        return pl.BlockSpec((rows, width), lambda b, t: (b * nb + nb - 1 - t, 0))

    in_specs = [fwd(LIN_BLOCK, w) for w in widths] + [fwd(cpb, GDN_VAL_WIDTH)]
    in_specs += [bwd(LIN_BLOCK, w) for w in widths] + [bwd(cpb, GDN_VAL_WIDTH)]
    state = pltpu.VMEM((GDN_HEADS // 2, GDN_PAIR, GDN_PAIR), F32)
    o_f, o_b = pl.pallas_call(
        _gdn_scan2_kernel,
        out_shape=(jax.ShapeDtypeStruct((m, GDN_VAL_WIDTH), F32), jax.ShapeDtypeStruct((m, GDN_VAL_WIDTH), F32)),
        grid=(batch, nb),
        in_specs=in_specs,
        out_specs=(fwd(LIN_BLOCK, GDN_VAL_WIDTH), bwd(LIN_BLOCK, GDN_VAL_WIDTH)),
        scratch_shapes=[state, state],
        compiler_params=_params("parallel", "arbitrary"),
        name="gdn_scan",
    )(*wy)
    return finish_branch(o_f, o_b, pb, "gdn_og", norm_gain, silu_gate=True)


def _rms_norm(x, gain, eps=RMS_EPS):
    xf = x.astype(F32)
    y = xf * lax.rsqrt(jnp.mean(xf * xf, axis=-1, keepdims=True) + eps)
    return (y * gain.astype(F32)).astype(x.dtype)


def _l2_norm(x, eps=1e-6):
    xf = x.astype(F32)
    return xf * lax.rsqrt(jnp.sum(xf * xf, axis=-1, keepdims=True) + eps)


def _rev(t):
    return jnp.flip(t, axis=1)


def _centred_depthwise_conv(x, w):
    width = w.shape[0]
    return lax.conv_general_dilated(
        x, w[:, None, :], window_strides=(1,), padding=[(width // 2, width // 2)],
        dimension_numbers=("NWC", "WIO", "NWC"), feature_group_count=x.shape[-1])


def _chunk_gla(q, k, v, log_g):
    B, S, H, K = q.shape
    V = v.shape[-1]
    C = LIN_CHUNK
    n = S // C

    def chunks(t):
        return t.reshape(B, n, C, H, t.shape[-1]).transpose(1, 0, 3, 2, 4)

    q, k, v, log_g = chunks(q), chunks(k), chunks(v), chunks(log_g)
    b = jnp.cumsum(log_g, axis=-2)
    b_ref = b[..., C // 2:C // 2 + 1, :]
    incl = jnp.tril(jnp.ones((C, C), dtype=bool))
    scores = jnp.einsum("nbhik,nbhjk->nbhij", q * jnp.exp(b - b_ref), k * jnp.exp(b_ref - b))
    o_intra = jnp.einsum("nbhij,nbhjv->nbhiv", jnp.where(incl, scores, 0.0), v)
    q_dec = q * jnp.exp(b)
    k_dec = k * jnp.exp(b[..., -1:, :] - b)
    g_tot = jnp.exp(b[..., -1, :])

    def step(state, xs):
        q_c, k_c, v_c, g_c = xs
        o_c = jnp.einsum("bhik,bhkv->bhiv", q_c, state)
        state = state * g_c[..., None] + jnp.einsum("bhjk,bhjv->bhkv", k_c, v_c)
        return state, o_c

    _, o_inter = lax.scan(step, jnp.zeros((B, H, K, V), F32), (q_dec, k_dec, v, g_tot))
    o = o_intra + o_inter
    return o.transpose(1, 0, 3, 2, 4).reshape(B, S, H, V)


def _chunk_gdn(q, k, v, log_alpha, beta):
    B, S, H, K = q.shape
    V = v.shape[-1]
    C = GDN_CHUNK
    n = S // C

    def chunks(t):
        return t.reshape(B, n, C, H, t.shape[-1]).transpose(1, 0, 3, 2, 4)

    q, k, v = chunks(q), chunks(k), chunks(v)
    g = jnp.cumsum(chunks(log_alpha[..., None])[..., 0], axis=-1)
    beta = chunks(beta[..., None])
    incl = jnp.tril(jnp.ones((C, C), dtype=bool))
    strict = jnp.tril(jnp.ones((C, C), dtype=bool), -1)
    diff = g[..., :, None] - g[..., None, :]
    decay = jnp.where(incl, jnp.exp(jnp.where(incl, diff, 0.0)), 0.0)
    k_beta = k * beta
    a = jnp.where(strict, jnp.einsum("nbhik,nbhjk->nbhij", k_beta, k) * decay, 0.0)
    rhs = jnp.concatenate([v * beta, k_beta * jnp.exp(g)[..., None]], axis=-1)
    sol = lax.linalg.triangular_solve(a + jnp.eye(C, dtype=F32), rhs, left_side=True, lower=True)
    u, w = sol[..., :V], sol[..., V:]
    attn = jnp.einsum("nbhik,nbhjk->nbhij", q, k) * decay
    q_dec = q * jnp.exp(g)[..., None]
    k_dec = k * jnp.exp(g[..., -1:] - g)[..., None]
    g_tot = jnp.exp(g[..., -1])

    def step(state, xs):
        u_c, w_c, attn_c, q_c, k_c, g_c = xs
        v_new = u_c - jnp.einsum("bhck,bhkv->bhcv", w_c, state)
        o_c = jnp.einsum("bhck,bhkv->bhcv", q_c, state) + jnp.einsum("bhij,bhjv->bhiv", attn_c, v_new)
        state = state * g_c[..., None, None] + jnp.einsum("bhck,bhcv->bhkv", k_c, v_new)
        return state, o_c

    _, o = lax.scan(step, jnp.zeros((B, H, K, V), F32), (u, w, attn, q_dec, k_dec, g_tot))
    return o.transpose(1, 0, 3, 2, 4).reshape(B, S, H, V)


def _neighbourhood_attention(q, k, v, q_gain, k_gain, rel_bias):
    B, S, _ = q.shape
    rows = S // GRID_W
    win_rows = min(NA_WIN_ROWS, rows)

    def grid(t):
        return t.reshape(B, rows, GRID_W, NA_HEADS, NA_HEAD_DIM)

    q = _rms_norm(grid(q), q_gain).astype(F32) * (NA_HEAD_DIM ** -0.5)
    k = _rms_norm(grid(k), k_gain).astype(F32)
    v = grid(v).astype(F32)
    r = jnp.arange(rows)
    c = jnp.arange(GRID_W)
    row_idx = jnp.clip(r - win_rows // 2, 0, rows - win_rows)[:, None] + jnp.arange(win_rows)[None, :]
    col_start = jnp.clip(c - NA_WIN_COLS // 2, 0, GRID_W - NA_WIN_COLS)
    col_in = (c[None, :] >= col_start[:, None]) & (c[None, :] < col_start[:, None] + NA_WIN_COLS)
    k_band = k[:, row_idx]
    v_band = v[:, row_idx]
    s = jnp.einsum("brqhd,brikhd->bhrqik", q, k_band)
    dr = row_idx - r[:, None] + (NA_WIN_ROWS - 1)
    dc = jnp.clip(c[None, :] - c[:, None], 1 - NA_WIN_COLS, NA_WIN_COLS - 1) + (NA_WIN_COLS - 1)
    bias = rel_bias.astype(F32)[:, dr[:, None, :, None], dc[None, :, None, :]]
    s = jnp.where(col_in[:, None, :], s + bias[None], MASK_VALUE)
    p = jax.nn.softmax(s, axis=(-2, -1))
    o = jnp.einsum("bhrqik,brikhd->brqhd", p, v_band)
    return o.reshape(B, S, NA_WIDTH)


def _gla_branch(q, k, v, gate_lr, out_gate, w_gate_up, b_gate, norm_gain):
    B, S, _ = q.shape
    q = q.astype(F32).reshape(B, S, GLA_HEADS, GLA_HEAD_K) * (GLA_HEAD_K ** -0.5)
    k = k.astype(F32).reshape(B, S, GLA_HEADS, GLA_HEAD_K)
    v = v.astype(F32).reshape(B, S, GLA_HEADS, GLA_HEAD_V)
    lr = gate_lr.astype(F32).reshape(B, S, 2, GLA_GATE_RANK)
    gk = jnp.einsum("bsdr,drk->bsdk", lr, w_gate_up.astype(F32)) + b_gate.astype(F32)
    log_g = (jax.nn.log_sigmoid(gk) / GLA_GATE_NORMALIZER).reshape(B, S, 2, GLA_HEADS, GLA_HEAD_K)
    o = (_chunk_gla(q, k, v, log_g[:, :, 0])
         + _rev(_chunk_gla(_rev(q), _rev(k), _rev(v), _rev(log_g[:, :, 1]))))
    o = _rms_norm(o, norm_gain) * jax.nn.silu(out_gate.astype(F32)).reshape(B, S, GLA_HEADS, GLA_HEAD_V)
    return o.reshape(B, S, GLA_VAL_WIDTH)


def _gdn_branch(qkv, a, b, out_gate, conv_w, a_log, dt_bias, norm_gain):
    B, S, _ = qkv.shape
    qkv = jax.nn.silu(_centred_depthwise_conv(qkv.astype(F32), conv_w.astype(F32)))
    q, k, v = jnp.split(qkv, [GDN_KEY_WIDTH, 2 * GDN_KEY_WIDTH], axis=-1)
    q = _l2_norm(q.reshape(B, S, GDN_HEADS, GDN_HEAD_K)) * (GDN_HEAD_K ** -0.5)
    k = _l2_norm(k.reshape(B, S, GDN_HEADS, GDN_HEAD_K))
    v = v.reshape(B, S, GDN_HEADS, GDN_HEAD_V)
    a = a.astype(F32).reshape(B, S, 2, GDN_HEADS)
    b = b.astype(F32).reshape(B, S, 2, GDN_HEADS)
    log_alpha = -jnp.exp(a_log.astype(F32)) * jax.nn.softplus(a + dt_bias.astype(F32))
    beta = jax.nn.sigmoid(b)
    o = (_chunk_gdn(q, k, v, log_alpha[:, :, 0], beta[:, :, 0])
         + _rev(_chunk_gdn(_rev(q), _rev(k), _rev(v), _rev(log_alpha[:, :, 1]), _rev(beta[:, :, 1]))))
    o = _rms_norm(o, norm_gain) * jax.nn.silu(out_gate.astype(F32)).reshape(B, S, GDN_HEADS, GDN_HEAD_V)
    return o.reshape(B, S, GDN_VAL_WIDTH)


def _hgrn2_branch(q, f_pre, i, out_gate, lower_bound, norm_gain):
    B, S, _ = q.shape
    q = jax.nn.silu(q.astype(F32)).reshape(B, S, HGRN_HEADS, HGRN_HEAD_K)
    z = f_pre.astype(F32).reshape(B, S, 2, HGRN_KEY_WIDTH)
    lb = lower_bound.astype(F32)
    log_f = jnp.logaddexp(jnp.log(jnp.maximum(lb, LB_FLOOR)), jnp.log1p(-lb) + jax.nn.log_sigmoid(z))
    k_in = (1.0 - lb) * jax.nn.sigmoid(-z)
    log_f = log_f.reshape(B, S, 2, HGRN_HEADS, HGRN_HEAD_K)
    k_in = k_in.reshape(B, S, 2, HGRN_HEADS, HGRN_HEAD_K)
    v = i.astype(F32).reshape(B, S, HGRN_HEADS, HGRN_HEAD_V)
    o = (_chunk_gla(q, k_in[:, :, 0], v, log_f[:, :, 0])
         + _rev(_chunk_gla(_rev(q), _rev(k_in[:, :, 1]), _rev(v), _rev(log_f[:, :, 1]))))
    o = _rms_norm(o, norm_gain) * jax.nn.sigmoid(out_gate.astype(F32)).reshape(B, S, HGRN_HEADS, HGRN_HEAD_V)
    return o.reshape(B, S, HGRN_VAL_WIDTH)


def _memory_cross_attention(q, kv, q_gain, k_gain):
    B, S, _ = q.shape
    M = kv.shape[1]
    q = _rms_norm(q.reshape(B, S, MEM_HEADS, MEM_HEAD_DIM), q_gain).astype(F32)
    k, v = jnp.split(kv, 2, axis=-1)
    k = _rms_norm(k.reshape(B, M, MEM_HEADS, MEM_HEAD_DIM), k_gain).astype(F32)
    v = v.reshape(B, M, MEM_HEADS, MEM_HEAD_DIM).astype(F32)
    s = jnp.einsum("bshd,bmhd->bhsm", q, k) * (MEM_HEAD_DIM ** -0.5)
    p = jax.nn.softmax(s, axis=-1)
    o = jnp.einsum("bhsm,bmhd->bshd", p, v)
    return o.reshape(B, S, MEM_WIDTH)


def kernel(x, mem, g_mix, w_in, na_q_gain, na_k_gain, na_rel_bias, gla_w_gate_up, gla_b_gate, gla_norm_gain, gdn_conv_w, gdn_a_log, gdn_dt_bias, gdn_norm_gain, hgrn_lb_raw, hgrn_norm_gain, g_mem, w_mem_kv, mem_q_gain, mem_k_gain, w_branch, w_out, g_ffn, ffn_w_gate, ffn_w_up, ffn_w_down, moe_w_router, moe_b_router, moe_w_gate, moe_w_up, moe_w_down):
    B, S, D = x.shape
    n_tok = B * S
    lb_w = jax.nn.softmax(hgrn_lb_raw.astype(F32), axis=0)
    hgrn_lb = jnp.cumsum(lb_w, axis=0) - lb_w[0:1]
    x2 = x.reshape(n_tok, D)
    mem2 = mem.reshape(B * mem.shape[1], D)
    for layer in range(DEPTH):
        wb, wf = _split_w_in(w_in[layer])
        pb = rms_matmul(x2, g_mix[layer], wb, tm=1024, tn=512, out_dtype=BF16)
        pf = rms_matmul(x2, g_mix[layer], wf, tm=1024, tn=PF_WIDTH // 3, out_dtype=F32)
        kv = rms_matmul(mem2, g_mem[layer], w_mem_kv[layer].astype(BF16), tm=mem2.shape[0], tn=512,
                        out_dtype=BF16)
        branches = [
            neighbourhood_attention(pb, B, na_q_gain[layer], na_k_gain[layer], na_rel_bias[layer]),
            gla_branch(pb, pf, B, gla_w_gate_up[layer], gla_b_gate[layer], gla_norm_gain[layer]),
            gated_deltanet_branch(pb, pf, B, gdn_conv_w[layer], gdn_a_log[layer], gdn_dt_bias[layer],
                                  gdn_norm_gain[layer]),
            hgrn2_branch(pb, pf, B, hgrn_lb[layer], hgrn_norm_gain[layer]),
            memory_cross_attention(pb, kv, B, mem_q_gain[layer], mem_k_gain[layer]),
        ]
        merged = merge_branches(branches, pb, PB_COL["gates"], w_branch[layer].astype(BF16), tm=1024, tn=512)
        x2 = matmul_residual(merged, w_out[layer].astype(BF16), x2, tm=1024, tn=512)

        j = layer // 2
        if layer % 2 == 0:
            act = rms_swiglu_up(x2, g_ffn[layer], ffn_w_gate[j].astype(BF16), ffn_w_up[j].astype(BF16),
                                tm=1024, tn=512)
            x2 = matmul_residual(act, ffn_w_down[j].astype(BF16), x2, tm=512, tn=512)
        else:
            x2 = moe_layer(x2, g_ffn[layer], moe_w_router[j], moe_b_router[j], moe_w_gate[j].astype(BF16),
                           moe_w_up[j].astype(BF16), moe_w_down[j].astype(BF16))
    return x2.reshape(B, S, D)
```

```python
import functools

import jax
import jax.numpy as jnp
import numpy as np
from jax import lax
from jax.experimental import pallas as pl
from jax.experimental.pallas import tpu as pltpu

F32 = jnp.float32
BF16 = jnp.bfloat16

D_MODEL = 2048
DEPTH = 2
RMS_EPS = 1e-6
MASK_VALUE = -1e30
LB_FLOOR = 1e-30
GRID_W = 64

NA_HEADS = 8
NA_HEAD_DIM = 64
NA_WIDTH = 512
NA_WIN_ROWS = 8
NA_WIN_COLS = 16

GLA_HEADS = 4
GLA_HEAD_K = 64
GLA_HEAD_V = 128
GLA_KEY_WIDTH = 256
GLA_VAL_WIDTH = 512
GLA_GATE_RANK = 16
GLA_GATE_NORMALIZER = 16.0

GDN_HEADS = 4
GDN_HEAD_K = 128
GDN_HEAD_V = 128
GDN_KEY_WIDTH = 512
GDN_VAL_WIDTH = 512
GDN_CHUNK = 64

HGRN_HEADS = 4
HGRN_HEAD_K = 128
HGRN_HEAD_V = 128
HGRN_KEY_WIDTH = 512
HGRN_VAL_WIDTH = 512

LIN_CHUNK = 32

MEM_HEADS = 4
MEM_HEAD_DIM = 128
MEM_WIDTH = 512

N_BRANCH = 5
BRANCH_WIDTH = 512
N_EXPERTS = 8
MOE_TOP_K = 2

IN_WIDTHS = (
    NA_WIDTH, NA_WIDTH, NA_WIDTH,
    GLA_KEY_WIDTH, GLA_KEY_WIDTH, GLA_VAL_WIDTH,
    2 * GLA_GATE_RANK, GLA_VAL_WIDTH,
    2 * GDN_KEY_WIDTH + GDN_VAL_WIDTH,
    2 * GDN_HEADS, 2 * GDN_HEADS, GDN_VAL_WIDTH,
    HGRN_KEY_WIDTH, 2 * HGRN_KEY_WIDTH, HGRN_VAL_WIDTH, HGRN_VAL_WIDTH,
    MEM_WIDTH,
    N_BRANCH * D_MODEL,
)
P_IN = sum(IN_WIDTHS)

V7X_VMEM_BYTES = 64 * 1024 * 1024
VMEM_LIMIT_BYTES = V7X_VMEM_BYTES - 8 * 1024 * 1024
LANES = 128


def _params(*semantics):
    return pltpu.CompilerParams(dimension_semantics=semantics, vmem_limit_bytes=VMEM_LIMIT_BYTES)


def _rms_norm_rows(x, gain):
    ms = jnp.mean(x * x, axis=-1, keepdims=True)
    return x * lax.rsqrt(ms + RMS_EPS) * gain


def _rms_matmul_kernel(x_ref, g_ref, w_ref, o_ref, h_ref):
    @pl.when(pl.program_id(1) == 0)
    def _():
        h_ref[...] = _rms_norm_rows(x_ref[...], g_ref[...]).astype(BF16)

    o_ref[...] = jnp.dot(h_ref[...], w_ref[...], preferred_element_type=F32).astype(o_ref.dtype)


def rms_matmul(x, gain, w, *, tm, tn, out_dtype=F32):
    m, k = x.shape
    n = w.shape[1]
    return pl.pallas_call(
        _rms_matmul_kernel,
        out_shape=jax.ShapeDtypeStruct((m, n), out_dtype),
        grid=(m // tm, n // tn),
        in_specs=[
            pl.BlockSpec((tm, k), lambda i, j: (i, 0)),
            pl.BlockSpec((1, k), lambda i, j: (0, 0)),
            pl.BlockSpec((k, tn), lambda i, j: (0, j)),
        ],
        out_specs=pl.BlockSpec((tm, tn), lambda i, j: (i, j)),
        scratch_shapes=[pltpu.VMEM((tm, k), BF16)],
        compiler_params=_params("parallel", "arbitrary"),
        name="rms_matmul",
    )(x, gain.reshape(1, k), w)


def _rms_swiglu_kernel(x_ref, g_ref, wg_ref, wu_ref, o_ref, h_ref):
    @pl.when(pl.program_id(1) == 0)
    def _():
        h_ref[...] = _rms_norm_rows(x_ref[...], g_ref[...]).astype(BF16)

    h = h_ref[...]
    a = jnp.dot(h, wg_ref[...], preferred_element_type=F32)
    b = jnp.dot(h, wu_ref[...], preferred_element_type=F32)
    o_ref[...] = (a * jax.nn.sigmoid(a) * b).astype(o_ref.dtype)


def rms_swiglu_up(x, gain, wg, wu, *, tm, tn):
    m, k = x.shape
    n = wg.shape[1]
    return pl.pallas_call(
        _rms_swiglu_kernel,
        out_shape=jax.ShapeDtypeStruct((m, n), BF16),
        grid=(m // tm, n // tn),
        in_specs=[
            pl.BlockSpec((tm, k), lambda i, j: (i, 0)),
            pl.BlockSpec((1, k), lambda i, j: (0, 0)),
            pl.BlockSpec((k, tn), lambda i, j: (0, j)),
            pl.BlockSpec((k, tn), lambda i, j: (0, j)),
        ],
        out_specs=pl.BlockSpec((tm, tn), lambda i, j: (i, j)),
        scratch_shapes=[pltpu.VMEM((tm, k), BF16)],
        compiler_params=_params("parallel", "arbitrary"),
        name="rms_swiglu_up",
    )(x, gain.reshape(1, k), wg, wu)


def _matmul_residual_kernel(a_ref, w_ref, r_ref, o_ref):
    o_ref[...] = r_ref[...] + jnp.dot(a_ref[...], w_ref[...], preferred_element_type=F32)


def matmul_residual(a, w, res, *, tm, tn):
    m, k = a.shape
    n = w.shape[1]
    return pl.pallas_call(
        _matmul_residual_kernel,
        out_shape=jax.ShapeDtypeStruct((m, n), F32),
        grid=(m // tm, n // tn),
        in_specs=[
            pl.BlockSpec((tm, k), lambda i, j: (i, 0)),
            pl.BlockSpec((k, tn), lambda i, j: (0, j)),
            pl.BlockSpec((tm, tn), lambda i, j: (i, j)),
        ],
        out_specs=pl.BlockSpec((tm, tn), lambda i, j: (i, j)),
        compiler_params=_params("parallel", "arbitrary"),
        name="matmul_residual",
    )(a, w, res)


def _merge_kernel(*refs):
    br_refs = refs[:N_BRANCH]
    gl_refs = refs[N_BRANCH:2 * N_BRANCH]
    wb_ref = refs[2 * N_BRANCH]
    o_ref = refs[2 * N_BRANCH + 1]
    acc = None
    for n in range(N_BRANCH):
        y = jnp.dot(br_refs[n][...], wb_ref[n], preferred_element_type=F32)
        t = jax.nn.sigmoid(gl_refs[n][...].astype(F32)) * y
        acc = t if acc is None else acc + t
    o_ref[...] = acc.astype(o_ref.dtype)


def merge_branches(branches, gate_logits, gate_col0, w_branch, *, tm, tn):
    m = branches[0].shape[0]
    d = D_MODEL
    tiles_per_branch = d // tn
    tile0 = gate_col0 // tn
    in_specs = [pl.BlockSpec((tm, BRANCH_WIDTH), lambda i, j: (i, 0)) for _ in range(N_BRANCH)]
    in_specs += [
        pl.BlockSpec((tm, tn), functools.partial(lambda i, j, n: (i, tile0 + n * tiles_per_branch + j), n=n))
        for n in range(N_BRANCH)
    ]
    in_specs += [pl.BlockSpec((N_BRANCH, BRANCH_WIDTH, tn), lambda i, j: (0, 0, j))]
    return pl.pallas_call(
        _merge_kernel,
        out_shape=jax.ShapeDtypeStruct((m, d), BF16),
        grid=(m // tm, d // tn),
        in_specs=in_specs,
        out_specs=pl.BlockSpec((tm, tn), lambda i, j: (i, j)),
        compiler_params=_params("parallel", "arbitrary"),
        name="merge_branches",
    )(*branches, *([gate_logits] * N_BRANCH), w_branch)


def _router_kernel(x_ref, g_ref, w_ref, b_ref, o_ref, *, n_experts):
    h = _rms_norm_rows(x_ref[...], g_ref[...])
    logits = _dot_f32(h, w_ref[...]) + b_ref[...]
    lane = lax.broadcasted_iota(jnp.int32, logits.shape, 1).astype(F32)
    neg = -jnp.inf
    lm = jnp.where(lane < n_experts, logits, neg)
    m1 = jnp.max(lm, axis=-1, keepdims=True)
    i1 = jnp.min(jnp.where(lm == m1, lane, float(LANES)), axis=-1, keepdims=True)
    lm2 = jnp.where(lane == i1, neg, lm)
    m2 = jnp.max(lm2, axis=-1, keepdims=True)
    i2 = jnp.min(jnp.where(lm2 == m2, lane, float(LANES)), axis=-1, keepdims=True)
    t = jnp.exp(m2 - m1)
    den = 1.0 + t
    out = jnp.where(lane == 0, 1.0 / den, jnp.where(lane == 1, t / den, jnp.where(lane == 2, i1, i2)))
    o_ref[...] = jnp.where(lane < 4, out, 0.0)


def router_top2(x, gain, w_router, b_router, *, tm):
    m, k = x.shape
    e = w_router.shape[1]
    w_pad = jnp.zeros((k, LANES), F32).at[:, :e].set(w_router.astype(F32))
    b_pad = jnp.zeros((1, LANES), F32).at[0, :e].set(b_router.astype(F32))
    return pl.pallas_call(
        functools.partial(_router_kernel, n_experts=e),
        out_shape=jax.ShapeDtypeStruct((m, LANES), F32),
        grid=(m // tm,),
        in_specs=[
            pl.BlockSpec((tm, k), lambda i: (i, 0)),
            pl.BlockSpec((1, k), lambda i: (0, 0)),
            pl.BlockSpec((k, LANES), lambda i: (0, 0)),
            pl.BlockSpec((1, LANES), lambda i: (0, 0)),
        ],
        out_specs=pl.BlockSpec((tm, LANES), lambda i: (i, 0)),
        compiler_params=_params("parallel"),
        name="router_top2",
    )(x, gain.reshape(1, k), w_pad, b_pad)


MOE_TILE = 1024
MOE_SUB = 256
MOE_FF_TILE = 256


def _moe_kernel(tile_e_ref, tile_rows_ref, n_used_ref, x_ref, wg_ref, wu_ref, wd_ref, o_ref,
                acc_ref, wg_s, wu_s, wd_s):
    i = pl.program_id(0)
    j = pl.program_id(1)
    last = pl.num_programs(1) - 1
    valid = tile_rows_ref[i]
    subs = [slice(s, s + MOE_SUB) for s in range(0, MOE_TILE, MOE_SUB)]

    @pl.when(valid > 0)
    def _():
        wg_s[...] = wg_ref[0].astype(BF16)
        wu_s[...] = wu_ref[0].astype(BF16)
        wd_s[...] = wd_ref[0].astype(BF16)

    for rows in subs:
        filled = rows.start < valid

        @pl.when(filled)
        def _(rows=rows):
            x = x_ref[rows, :]
            a = jnp.dot(x, wg_s[...], preferred_element_type=F32)
            b = jnp.dot(x, wu_s[...], preferred_element_type=F32)
            act = (a * jax.nn.sigmoid(a) * b).astype(BF16)
            part = jnp.dot(act, wd_s[...], preferred_element_type=F32)

            @pl.when(j == 0)
            def _():
                acc_ref[rows, :] = part

            @pl.when(j > 0)
            def _():
                acc_ref[rows, :] += part

        @pl.when(jnp.logical_and(filled, j == last))
        def _(rows=rows):
            o_ref[rows, :] = acc_ref[rows, :].astype(o_ref.dtype)

        @pl.when(jnp.logical_and(jnp.logical_not(filled), j == last))
        def _(rows=rows):
            o_ref[rows, :] = jnp.zeros((MOE_SUB, o_ref.shape[1]), o_ref.dtype)


def moe_experts(xb, tile_e, tile_rows, n_used, wg, wu, wd):
    rows, d = xb.shape
    ff = wg.shape[2]
    tm, tf = MOE_TILE, MOE_FF_TILE
    n_tiles = rows // tm
    last_j = ff // tf - 1

    def x_map(i, j, te, tr, nu):
        return (jnp.minimum(i, nu[0] - 1), 0)

    def up_map(i, j, te, tr, nu):
        return (te[i], 0, jnp.where(i < nu[0], j, last_j))

    def down_map(i, j, te, tr, nu):
        return (te[i], jnp.where(i < nu[0], j, last_j), 0)

    grid_spec = pltpu.PrefetchScalarGridSpec(
        num_scalar_prefetch=3,
        grid=(n_tiles, ff // tf),
        in_specs=[
            pl.BlockSpec((tm, d), x_map),
            pl.BlockSpec((1, d, tf), up_map),
            pl.BlockSpec((1, d, tf), up_map),
            pl.BlockSpec((1, tf, d), down_map),
        ],
        out_specs=pl.BlockSpec((tm, d), lambda i, j, te, tr, nu: (i, 0)),
        scratch_shapes=[pltpu.VMEM((tm, d), F32), pltpu.VMEM((d, tf), BF16), pltpu.VMEM((d, tf), BF16),
                        pltpu.VMEM((tf, d), BF16)],
    )
    return pl.pallas_call(
        _moe_kernel,
        out_shape=jax.ShapeDtypeStruct((rows, d), BF16),
        grid_spec=grid_spec,
        compiler_params=_params("arbitrary", "arbitrary"),
        name="moe_experts",
    )(tile_e, tile_rows, n_used, xb, wg, wu, wd)


def _moe_combine_kernel(x_ref, y_ref, r_ref, o_ref):
    d = x_ref.shape[1]
    w = r_ref[...]
    o_ref[...] = (x_ref[...] + w[:, 0:1] * y_ref[:, :d].astype(F32) + w[:, 1:2] * y_ref[:, d:].astype(F32))


def moe_combine(x2d, y2, route, *, tm=512):
    n, d = x2d.shape
    return pl.pallas_call(
        _moe_combine_kernel,
        out_shape=jax.ShapeDtypeStruct((n, d), F32),
        grid=(n // tm,),
        in_specs=[pl.BlockSpec((tm, d), lambda i: (i, 0)), pl.BlockSpec((tm, 2 * d), lambda i: (i, 0)),
                  pl.BlockSpec((tm, LANES), lambda i: (i, 0))],
        out_specs=pl.BlockSpec((tm, d), lambda i: (i, 0)),
        compiler_params=_params("parallel"),
        name="moe_combine",
    )(x2d, y2, route)


def moe_layer(x2d, gain, w_router, b_router, wg, wu, wd):
    n, d = x2d.shape
    e = N_EXPERTS
    route = router_top2(x2d, gain, w_router, b_router, tm=1024)
    top_e = route[:, 2:2 + MOE_TOP_K].astype(jnp.int32)
    nk = n * MOE_TOP_K
    n_tiles = -(-nk // MOE_TILE) + e
    flat_e = top_e.reshape(nk)
    flat_tok = jnp.repeat(jnp.arange(n, dtype=jnp.int32), MOE_TOP_K)
    onehot = (flat_e[:, None] == jnp.arange(e, dtype=jnp.int32)[None, :]).astype(jnp.int32)
    rank = jnp.take_along_axis(jnp.cumsum(onehot, axis=0), flat_e[:, None], axis=1)[:, 0] - 1
    counts = jnp.sum(onehot, axis=0)
    padded = (counts + MOE_TILE - 1) // MOE_TILE * MOE_TILE
    pad_end = jnp.cumsum(padded)
    pad_start = pad_end - padded
    slot = (pad_start[flat_e] + rank).astype(jnp.int32)
    slot_tok = jnp.full((n_tiles * MOE_TILE,), n, jnp.int32).at[slot].set(flat_tok)
    tile_start = jnp.arange(n_tiles, dtype=jnp.int32) * MOE_TILE
    tile_e = jnp.minimum(jnp.searchsorted(pad_end, tile_start, side="right"), e - 1).astype(jnp.int32)
    tile_rows = jnp.clip(pad_start[tile_e] + counts[tile_e] - tile_start, 0, MOE_TILE).astype(jnp.int32)
    tile_rows = jnp.where(tile_start < pad_end[-1], tile_rows, 0)
    n_used = (pad_end[-1] // MOE_TILE).astype(jnp.int32).reshape(1)
    tile_e = jnp.where(tile_start < pad_end[-1], tile_e, tile_e[jnp.maximum(n_used[0] - 1, 0)])

    h = rms_only(x2d, gain)
    h_pad = jnp.concatenate([h, jnp.zeros((1, d), h.dtype)], axis=0)
    xb = h_pad[slot_tok]
    yb = moe_experts(xb, tile_e, tile_rows, n_used, wg, wu, wd)
    return moe_combine(x2d, yb[slot].reshape(n, MOE_TOP_K * d), route)


def _rms_kernel(x_ref, g_ref, o_ref):
    o_ref[...] = _rms_norm_rows(x_ref[...], g_ref[...]).astype(o_ref.dtype)


def rms_only(x, gain, *, tm=1024):
    m, k = x.shape
    return pl.pallas_call(
        _rms_kernel,
        out_shape=jax.ShapeDtypeStruct((m, k), BF16),
        grid=(m // tm,),
        in_specs=[pl.BlockSpec((tm, k), lambda i: (i, 0)), pl.BlockSpec((1, k), lambda i: (0, 0))],
        out_specs=pl.BlockSpec((tm, k), lambda i: (i, 0)),
        compiler_params=_params("parallel"),
        name="rms_only",
    )(x, gain.reshape(1, k))


_SRC = dict(zip(
    ("na_q", "na_k", "na_v", "gla_q", "gla_k", "gla_v", "gla_lr", "gla_og", "gdn_qkv", "gdn_a", "gdn_b",
     "gdn_og", "hg_q", "hg_f", "hg_i", "hg_og", "mem_q", "gates"),
    zip(np.cumsum((0,) + IN_WIDTHS[:-1]).tolist(), IN_WIDTHS)))
_PB_ORDER = ("na_q", "na_k", "na_v", "gla_q", "gla_k", "gla_v", "gla_og", "gdn_qkv", "gdn_og", "hg_q", "hg_i",
             "hg_og", "mem_q", "gates")
_PF_ORDER = ("hg_f", "gla_lr", "gdn_a", "gdn_b")
PB_COL = {}
_c = 0
for _name in _PB_ORDER:
    PB_COL[_name] = _c
    _c += _SRC[_name][1]
PB_WIDTH = _c
PF_COL = {}
_c = 0
for _name in _PF_ORDER:
    PF_COL[_name] = _c
    _c += _SRC[_name][1]
PF_WIDTH = -(-_c // LANES) * LANES
PF_SMALL_COL = PF_COL["gla_lr"]
GDN_A_LANE = PF_COL["gdn_a"] - PF_SMALL_COL
GDN_B_LANE = PF_COL["gdn_b"] - PF_SMALL_COL


def _split_w_in(w):
    wb = jnp.concatenate([w[:, _SRC[n][0]:_SRC[n][0] + _SRC[n][1]] for n in _PB_ORDER], axis=1)
    wf = jnp.concatenate([w[:, _SRC[n][0]:_SRC[n][0] + _SRC[n][1]] for n in _PF_ORDER], axis=1)
    wf = jnp.pad(wf, ((0, 0), (0, PF_WIDTH - wf.shape[1])))
    return wb.astype(BF16), wf.astype(BF16)


def _segment_rms(x, gain, seg_ones, seg_width):
    sq = x * x
    hi = sq.astype(BF16)
    lo = (sq - hi.astype(F32)).astype(BF16)
    ss = (jnp.dot(hi, seg_ones, preferred_element_type=F32)
          + jnp.dot(lo, seg_ones, preferred_element_type=F32))
    return x * lax.rsqrt(ss * (1.0 / seg_width) + RMS_EPS) * gain


NA_ROWS_PER_STEP = 8
NA_BAND = NA_WIN_ROWS * GRID_W


def _na_bias_table(rel_bias):
    c = np.arange(GRID_W)
    dc = np.clip(c[None, :] - c[:, None], 1 - NA_WIN_COLS, NA_WIN_COLS - 1) + (NA_WIN_COLS - 1)
    col_start = np.clip(c - NA_WIN_COLS // 2, 0, GRID_W - NA_WIN_COLS)
    col_in = (c[None, :] >= col_start[:, None]) & (c[None, :] < col_start[:, None] + NA_WIN_COLS)
    cfg = np.arange(NA_WIN_ROWS)[:, None]
    dr = np.arange(NA_WIN_ROWS)[None, :] - cfg + (NA_WIN_ROWS - 1)
    t = rel_bias.astype(F32)[:, dr][:, :, :, dc]
    t = jnp.where(col_in[None, None, None], t, MASK_VALUE)
    return t.transpose(1, 0, 3, 2, 4).reshape(NA_WIN_ROWS, NA_HEADS, GRID_W, NA_BAND)


def _na_kernel(q_ref, k_ref, v_ref, qg_ref, kg_ref, seg_ref, bias_ref, o_ref, kn_ref):
    step = pl.program_id(1)
    rows_total = k_ref.shape[0] // GRID_W
    seg = seg_ref[...]

    @pl.when(step == 0)
    def _():
        def norm_keys(t, carry):
            rows = pl.ds(pl.multiple_of(t * 256, 256), 256)
            kn_ref[rows, :] = _segment_rms(k_ref[rows, :].astype(F32), kg_ref[...], seg, NA_HEAD_DIM).astype(BF16)
            return carry
        lax.fori_loop(0, k_ref.shape[0] // 256, norm_keys, 0)

    lane = lax.broadcasted_iota(jnp.int32, (1, LANES), 1)
    low_half = lane < NA_HEAD_DIM

    def one_row(rr, carry):
        r = step * NA_ROWS_PER_STEP + rr
        row_start = jnp.clip(r - NA_WIN_ROWS // 2, 0, rows_total - NA_WIN_ROWS)
        cfg = r - row_start
        qrows = pl.ds(pl.multiple_of(rr * GRID_W, GRID_W), GRID_W)
        band = pl.ds(pl.multiple_of(row_start * GRID_W, GRID_W), NA_BAND)
        qn = (_segment_rms(q_ref[qrows, :].astype(F32), qg_ref[...], seg, NA_HEAD_DIM)
              * (NA_HEAD_DIM ** -0.5)).astype(BF16)
        heads = [(pair, half) for pair in range(NA_HEADS // 2) for half in range(2)]
        scores = []
        for pair, half in heads:
            cols = slice(pair * LANES, (pair + 1) * LANES)
            qp = qn[:, cols]
            keep = low_half if half == 0 else jnp.logical_not(low_half)
            qm = jnp.where(keep, qp, jnp.zeros_like(qp))
            scores.append(lax.dot_general(qm, kn_ref[band, cols], (((1,), (1,)), ((), ())),
                                          preferred_element_type=F32))
        exps, sums = [], []
        for s, (pair, half) in zip(scores, heads):
            s = s + bias_ref[cfg, 2 * pair + half]
            e = jnp.exp(s - jnp.max(s, axis=-1, keepdims=True))
            sums.append(jnp.sum(e, axis=-1, keepdims=True))
            exps.append(e.astype(BF16))
        outs = [jnp.dot(e, v_ref[band, slice(pair * LANES, (pair + 1) * LANES)], preferred_element_type=F32) / l
                for e, l, (pair, half) in zip(exps, sums, heads)]
        for pair in range(NA_HEADS // 2):
            cols = slice(pair * LANES, (pair + 1) * LANES)
            o_ref[qrows, cols] = jnp.where(low_half, outs[2 * pair], outs[2 * pair + 1]).astype(o_ref.dtype)
        return carry

    lax.fori_loop(0, NA_ROWS_PER_STEP, one_row, 0)


def neighbourhood_attention(pb, batch, q_gain, k_gain, rel_bias):
    m = pb.shape[0]
    s = m // batch
    tq = NA_ROWS_PER_STEP * GRID_W
    steps = s // tq
    qg = jnp.tile(q_gain.astype(F32), NA_HEADS).reshape(1, NA_WIDTH)
    kg = jnp.tile(k_gain.astype(F32), NA_HEADS).reshape(1, NA_WIDTH)
    seg = jnp.asarray(np.kron(np.eye(NA_HEADS), np.ones((NA_HEAD_DIM, NA_HEAD_DIM))), BF16)
    bias = _na_bias_table(rel_bias)
    cq, ck, cv = (PB_COL[n] // NA_WIDTH for n in ("na_q", "na_k", "na_v"))
    return pl.pallas_call(
        _na_kernel,
        out_shape=jax.ShapeDtypeStruct((m, NA_WIDTH), BF16),
        grid=(batch, steps),
        in_specs=[
            pl.BlockSpec((tq, NA_WIDTH), lambda b, t: (b * steps + t, cq)),
            pl.BlockSpec((s, NA_WIDTH), lambda b, t: (b, ck)),
            pl.BlockSpec((s, NA_WIDTH), lambda b, t: (b, cv)),
            pl.BlockSpec((1, NA_WIDTH), lambda b, t: (0, 0)),
            pl.BlockSpec((1, NA_WIDTH), lambda b, t: (0, 0)),
            pl.BlockSpec((NA_WIDTH, NA_WIDTH), lambda b, t: (0, 0)),
            pl.BlockSpec((NA_WIN_ROWS, NA_HEADS, GRID_W, NA_BAND), lambda b, t: (0, 0, 0, 0)),
        ],
        out_specs=pl.BlockSpec((tq, NA_WIDTH), lambda b, t: (b * steps + t, 0)),
        scratch_shapes=[pltpu.VMEM((s, NA_WIDTH), BF16)],
        compiler_params=_params("parallel", "arbitrary"),
        name="neighbourhood_attention",
    )(pb, pb, pb, qg, kg, seg, bias)


def _mem_attn_kernel(q_ref, kv_ref, qg_ref, kg_ref, o_ref, kn_ref):
    @pl.when(pl.program_id(1) == 0)
    def _():
        for h in range(MEM_HEADS):
            cols = slice(h * MEM_HEAD_DIM, (h + 1) * MEM_HEAD_DIM)
            kn_ref[:, cols] = _rms_norm_rows(kv_ref[:, cols].astype(F32), kg_ref[...]).astype(BF16)

    head_cols = [slice(h * MEM_HEAD_DIM, (h + 1) * MEM_HEAD_DIM) for h in range(MEM_HEADS)]
    qns = [_rms_norm_rows(q_ref[:, cols].astype(F32), qg_ref[...]).astype(BF16) for cols in head_cols]
    scores = [lax.dot_general(qn, kn_ref[:, cols], (((1,), (1,)), ((), ())), preferred_element_type=F32)
              for qn, cols in zip(qns, head_cols)]
    exps, sums = [], []
    for s in scores:
        s = s * (MEM_HEAD_DIM ** -0.5)
        e = jnp.exp(s - jnp.max(s, axis=-1, keepdims=True))
        sums.append(jnp.sum(e, axis=-1, keepdims=True))
        exps.append(e.astype(BF16))
    outs = [jnp.dot(e, kv_ref[:, MEM_WIDTH + cols.start:MEM_WIDTH + cols.stop], preferred_element_type=F32)
            for e, cols in zip(exps, head_cols)]
    for o, l, cols in zip(outs, sums, head_cols):
        o_ref[:, cols] = (o / l).astype(o_ref.dtype)


def memory_cross_attention(pb, kv, batch, q_gain, k_gain, *, tq=512):
    m = pb.shape[0]
    steps = m // batch // tq
    n_mem = kv.shape[0] // batch
    cq = PB_COL["mem_q"] // MEM_WIDTH
    return pl.pallas_call(
        _mem_attn_kernel,
        out_shape=jax.ShapeDtypeStruct((m, MEM_WIDTH), BF16),
        grid=(batch, steps),
        in_specs=[
            pl.BlockSpec((tq, MEM_WIDTH), lambda b, t: (b * steps + t, cq)),
            pl.BlockSpec((n_mem, 2 * MEM_WIDTH), lambda b, t: (b, 0)),
            pl.BlockSpec((1, MEM_HEAD_DIM), lambda b, t: (0, 0)),
            pl.BlockSpec((1, MEM_HEAD_DIM), lambda b, t: (0, 0)),
        ],
        out_specs=pl.BlockSpec((tq, MEM_WIDTH), lambda b, t: (b * steps + t, 0)),
        scratch_shapes=[pltpu.VMEM((n_mem, MEM_WIDTH), BF16)],
        compiler_params=_params("parallel", "arbitrary"),
        name="memory_cross_attention",
    )(pb, kv, q_gain.astype(F32).reshape(1, MEM_HEAD_DIM), k_gain.astype(F32).reshape(1, MEM_HEAD_DIM))


LIN_BLOCK = 512
HEAD_V = 128


def _log1p_exp_neg(t):
    return jnp.log(1.0 + jnp.exp(-t))


def _log_sigmoid(x):
    return jnp.minimum(x, 0.0) - _log1p_exp_neg(jnp.abs(x))


def _logaddexp(a, b):
    return jnp.maximum(a, b) + _log1p_exp_neg(jnp.abs(a - b))


def _split_bf16(x, terms):
    parts = []
    for _ in range(terms):
        p = x.astype(BF16)
        parts.append(p)
        x = x - p.astype(F32)
    return parts


def _dot_f32(a, b):
    a_hi, a_lo = _split_bf16(a, 2)
    b_hi, b_lo = _split_bf16(b, 2)
    return (jnp.dot(a_hi, b_hi, preferred_element_type=F32)
            + (jnp.dot(a_hi, b_lo, preferred_element_type=F32) + jnp.dot(a_lo, b_hi, preferred_element_type=F32)))


def _cumsum_rows(mask, x):
    m = jnp.where(mask, 1.0, 0.0).astype(BF16)
    hi, mid, lo = _split_bf16(x, 3)
    return (jnp.dot(m, hi, preferred_element_type=F32)
            + (jnp.dot(m, mid, preferred_element_type=F32) + jnp.dot(m, lo, preferred_element_type=F32)))


def _lin_chunk(qc, kc, vc, lg, s_ref, *, reverse, heads):
    c, w = qc.shape
    dk = w // heads
    ii = lax.broadcasted_iota(jnp.int32, (c, c), 0)
    jj = lax.broadcasted_iota(jnp.int32, (c, c), 1)
    causal = (jj >= ii) if reverse else (jj <= ii)
    b = _cumsum_rows(causal, lg)
    mid = c - 1 - c // 2 if reverse else c // 2
    end = 0 if reverse else c - 1
    b_mid = b[mid:mid + 1, :]
    b_end = b[end:end + 1, :]
    qe = qc * jnp.exp(b - b_mid)
    ke = (kc * jnp.exp(b_mid - b)).astype(BF16)
    q_dec = qc * jnp.exp(b)
    k_dec = kc * jnp.exp(b_end - b)
    g_tot = jnp.exp(b_end)

    lane = lax.broadcasted_iota(jnp.int32, (1, w), 1)
    head_mask = [((lane >= h * dk) & (lane < (h + 1) * dk)).astype(F32) for h in range(heads)]
    q4 = jnp.concatenate([qe * head_mask[h] for h in range(heads)], axis=0).astype(BF16)
    q4d = jnp.concatenate([q_dec * head_mask[h] for h in range(heads)], axis=0).astype(BF16)
    k4 = jnp.concatenate([k_dec * head_mask[h] for h in range(heads)], axis=0).astype(BF16)
    v4 = jnp.concatenate([vc[:, h * HEAD_V:(h + 1) * HEAD_V] for h in range(heads)], axis=0)

    scores = lax.dot_general(q4, ke, (((1,), (1,)), ((), ())), preferred_element_type=F32)
    ri = lax.broadcasted_iota(jnp.int32, (heads * c, c), 0) % c
    cj = lax.broadcasted_iota(jnp.int32, (heads * c, c), 1)
    keep = (cj >= ri) if reverse else (cj <= ri)
    scores = jnp.where(keep, scores, 0.0).astype(BF16)
    state = s_ref[...]
    o_inter = lax.dot_general(q4d, state.astype(BF16), (((1,), (1,)), ((), ())),
                              preferred_element_type=F32)
    outs = []
    for h in range(heads):
        rows = slice(h * c, (h + 1) * c)
        o_intra = jnp.dot(scores[rows, :], vc[:, h * HEAD_V:(h + 1) * HEAD_V], preferred_element_type=F32)
        outs.append(o_intra + o_inter[rows, :])
    s_ref[...] = state * g_tot + lax.dot_general(v4, k4, (((0,), (0,)), ((), ())), preferred_element_type=F32)
    return jnp.concatenate(outs, axis=1)


def _gla_inputs(refs, rows, direction, params):
    q_ref, k_ref, v_ref, g_ref = refs
    wpad_ref, bias_ref = params
    qc = q_ref[rows, :].astype(F32) * (GLA_HEAD_K ** -0.5)
    kc = k_ref[rows, :].astype(F32)
    gk = _dot_f32(g_ref[rows, :], wpad_ref[direction]) + bias_ref[direction]
    lg = _log_sigmoid(gk) * (1.0 / GLA_GATE_NORMALIZER)
    return qc, kc, v_ref[rows, :], lg


def _hgrn_inputs(refs, rows, direction, params):
    q_ref, v_ref, z_ref = refs
    lb_ref, log_lb_ref, log1m_lb_ref = params
    qr = q_ref[rows, :].astype(F32)
    qc = qr * jax.nn.sigmoid(qr)
    z = z_ref[rows, :]
    lg = _logaddexp(log_lb_ref[direction], log1m_lb_ref[direction] + _log_sigmoid(z))
    kc = (1.0 - lb_ref[direction]) * jax.nn.sigmoid(-z)
    return qc, kc, v_ref[rows, :], lg


def _bidir_lin_kernel(*refs, load_inputs, n_in, n_params, heads, chunk):
    fwd_refs = refs[:n_in]
    bwd_refs = refs[n_in:2 * n_in]
    params = refs[2 * n_in:2 * n_in + n_params]
    of_ref, ob_ref, sf_ref, sb_ref = refs[2 * n_in + n_params:]

    @pl.when(pl.program_id(1) == 0)
    def _():
        sf_ref[...] = jnp.zeros_like(sf_ref)
        sb_ref[...] = jnp.zeros_like(sb_ref)

    n_chunks = of_ref.shape[0] // chunk

    def body(c, carry):
        rows = pl.ds(pl.multiple_of(c * chunk, chunk), chunk)
        of_ref[rows, :] = _lin_chunk(*load_inputs(fwd_refs, rows, 0, params), sf_ref, reverse=False, heads=heads)
        rows = pl.ds(pl.multiple_of((n_chunks - 1 - c) * chunk, chunk), chunk)
        ob_ref[rows, :] = _lin_chunk(*load_inputs(bwd_refs, rows, 1, params), sb_ref, reverse=True, heads=heads)
        return carry

    lax.fori_loop(0, n_chunks, body, 0)


def _bidir_lin_call(name, load_inputs, arrays, col_blocks, widths, params, batch, heads, key_width):
    m = arrays[0].shape[0]
    nb = m // batch // LIN_BLOCK
    out_w = heads * HEAD_V

    def spec(width, col, rev):
        if rev:
            return pl.BlockSpec((LIN_BLOCK, width), lambda b, t: (b * nb + nb - 1 - t, col))
        return pl.BlockSpec((LIN_BLOCK, width), lambda b, t: (b * nb + t, col))

    in_specs = [spec(w, c[0], False) for w, c in zip(widths, col_blocks)]
    in_specs += [spec(w, c[1], True) for w, c in zip(widths, col_blocks)]
    in_specs += [pl.BlockSpec(p.shape, functools.partial(lambda b, t, nd: (0,) * nd, nd=p.ndim)) for p in params]
    kern = functools.partial(_bidir_lin_kernel, load_inputs=load_inputs, n_in=len(arrays), n_params=len(params),
                             heads=heads, chunk=LIN_CHUNK)
    return pl.pallas_call(
        kern,
        out_shape=(jax.ShapeDtypeStruct((m, out_w), F32), jax.ShapeDtypeStruct((m, out_w), F32)),
        grid=(batch, nb),
        in_specs=in_specs,
        out_specs=(spec(out_w, 0, False), spec(out_w, 0, True)),
        scratch_shapes=[pltpu.VMEM((HEAD_V, key_width), F32), pltpu.VMEM((HEAD_V, key_width), F32)],
        compiler_params=_params("parallel", "arbitrary"),
        name=name,
    )(*arrays, *arrays, *params)


def _finish_kernel(of_ref, ob_ref, og_ref, gain_ref, o_ref, *, heads, silu_gate):
    for h in range(heads):
        cols = slice(h * HEAD_V, (h + 1) * HEAD_V)
        y = _rms_norm_rows(of_ref[:, cols] + ob_ref[:, cols], gain_ref[...])
        g = og_ref[:, cols].astype(F32)
        gate = jax.nn.sigmoid(g)
        if silu_gate:
            gate = g * gate
        o_ref[:, cols] = (y * gate).astype(o_ref.dtype)


def finish_branch(o_fwd, o_bwd, pb, og_name, gain, *, silu_gate, tm=512):
    m, w = o_fwd.shape
    heads = w // HEAD_V
    cg = PB_COL[og_name] // w
    return pl.pallas_call(
        functools.partial(_finish_kernel, heads=heads, silu_gate=silu_gate),
        out_shape=jax.ShapeDtypeStruct((m, w), BF16),
        grid=(m // tm,),
        in_specs=[
            pl.BlockSpec((tm, w), lambda i: (i, 0)),
            pl.BlockSpec((tm, w), lambda i: (i, 0)),
            pl.BlockSpec((tm, w), lambda i: (i, cg)),
            pl.BlockSpec((1, HEAD_V), lambda i: (0, 0)),
        ],
        out_specs=pl.BlockSpec((tm, w), lambda i: (i, 0)),
        compiler_params=_params("parallel"),
        name="finish_branch",
    )(o_fwd, o_bwd, pb, gain.astype(F32).reshape(1, HEAD_V))


def gla_branch(pb, pf, batch, w_gate_up, b_gate, norm_gain):
    wpad = jnp.zeros((2, LANES, GLA_KEY_WIDTH), F32)
    for d in range(2):
        wpad = wpad.at[d, d * GLA_GATE_RANK:(d + 1) * GLA_GATE_RANK, :].set(w_gate_up[d].astype(F32))
    bias = b_gate.astype(F32).reshape(2, 1, GLA_KEY_WIDTH)
    cols = [(PB_COL["gla_q"] // GLA_KEY_WIDTH,) * 2, (PB_COL["gla_k"] // GLA_KEY_WIDTH,) * 2,
            (PB_COL["gla_v"] // GLA_VAL_WIDTH,) * 2, (PF_SMALL_COL // LANES,) * 2]
    o_f, o_b = _bidir_lin_call("gla_scan", _gla_inputs, [pb, pb, pb, pf], cols,
                               [GLA_KEY_WIDTH, GLA_KEY_WIDTH, GLA_VAL_WIDTH, LANES], [wpad, bias],
                               batch, GLA_HEADS, GLA_KEY_WIDTH)
    return finish_branch(o_f, o_b, pb, "gla_og", norm_gain, silu_gate=True)


def hgrn2_branch(pb, pf, batch, lower_bound, norm_gain):
    lb = lower_bound.astype(F32).reshape(2, 1, HGRN_KEY_WIDTH)
    log_lb = jnp.log(jnp.maximum(lb, LB_FLOOR))
    log1m_lb = jnp.log1p(-lb)
    zc = PF_COL["hg_f"] // HGRN_KEY_WIDTH
    cols = [(PB_COL["hg_q"] // HGRN_KEY_WIDTH,) * 2, (PB_COL["hg_i"] // HGRN_VAL_WIDTH,) * 2, (zc, zc + 1)]
    o_f, o_b = _bidir_lin_call("hgrn2_scan", _hgrn_inputs, [pb, pb, pf], cols,
                               [HGRN_KEY_WIDTH, HGRN_VAL_WIDTH, HGRN_KEY_WIDTH], [lb, log_lb, log1m_lb],
                               batch, HGRN_HEADS, HGRN_KEY_WIDTH)
    return finish_branch(o_f, o_b, pb, "hg_og", norm_gain, silu_gate=False)


GDN_CONV_WIDTH = 5
GDN_QKV_WIDTH = 2 * GDN_KEY_WIDTH + GDN_VAL_WIDTH
GDN_HALO = 16


def _gdn_prep_kernel(prev_ref, cur_ref, next_ref, w_ref, o_ref, xp_ref, *, blocks_per_seq):
    i = pl.program_id(0)
    t = cur_ref.shape[0]
    pos = i % blocks_per_seq
    prev = prev_ref[...].astype(F32)
    nxt = next_ref[...].astype(F32)
    xp_ref[0:GDN_HALO, :] = jnp.where(pos == 0, jnp.zeros_like(prev), prev)
    xp_ref[GDN_HALO:GDN_HALO + t, :] = cur_ref[...].astype(F32)
    xp_ref[GDN_HALO + t:, :] = jnp.where(pos == blocks_per_seq - 1, jnp.zeros_like(nxt), nxt)
    half = GDN_CONV_WIDTH // 2
    for g in range(GDN_QKV_WIDTH // LANES):
        cols = slice(g * LANES, (g + 1) * LANES)
        acc = None
        for j in range(GDN_CONV_WIDTH):
            term = xp_ref[GDN_HALO - half + j:GDN_HALO - half + j + t, cols] * w_ref[j:j + 1, cols]
            acc = term if acc is None else acc + term
        y = acc * jax.nn.sigmoid(acc)
        if g < 2 * GDN_HEADS:
            y = y * lax.rsqrt(jnp.sum(y * y, axis=-1, keepdims=True) + 1e-6)
            if g < GDN_HEADS:
                y = y * (GDN_HEAD_K ** -0.5)
        o_ref[:, cols] = y.astype(o_ref.dtype)


def gdn_prep(pb, batch, conv_w, *, t=512):
    m = pb.shape[0]
    blocks_per_seq = m // batch // t
    halo_per_block = t // GDN_HALO
    col = PB_COL["gdn_qkv"] // GDN_QKV_WIDTH
    last_halo = m // GDN_HALO - 1
    return pl.pallas_call(
        functools.partial(_gdn_prep_kernel, blocks_per_seq=blocks_per_seq),
        out_shape=jax.ShapeDtypeStruct((m, GDN_QKV_WIDTH), BF16),
        grid=(m // t,),
        in_specs=[
            pl.BlockSpec((GDN_HALO, GDN_QKV_WIDTH), lambda i: (jnp.maximum(i * halo_per_block - 1, 0), col)),
            pl.BlockSpec((t, GDN_QKV_WIDTH), lambda i: (i, col)),
            pl.BlockSpec((GDN_HALO, GDN_QKV_WIDTH),
                         lambda i: (jnp.minimum((i + 1) * halo_per_block, last_halo), col)),
            pl.BlockSpec((GDN_CONV_WIDTH, GDN_QKV_WIDTH), lambda i: (0, 0)),
        ],
        out_specs=pl.BlockSpec((t, GDN_QKV_WIDTH), lambda i: (i, 0)),
        scratch_shapes=[pltpu.VMEM((t + 2 * GDN_HALO, GDN_QKV_WIDTH), F32)],
        compiler_params=_params("parallel"),
        name="gdn_prep",
    )(pb, pb, pb, conv_w.astype(F32))


def _unit_triangular_inverse(a):
    c = a.shape[0]
    ii = lax.broadcasted_iota(jnp.int32, (c, c), 0)
    jj = lax.broadcasted_iota(jnp.int32, (c, c), 1)
    eye = (ii == jj).astype(F32)

    def same_block(s):
        return (ii // s) == (jj // s)

    d = jnp.where(same_block(8), a, 0.0)
    d2 = _dot_f32(d, d)
    d4 = _dot_f32(d2, d2)
    t = _dot_f32(_dot_f32(eye - d, eye + d2), eye + d4)
    s = 8
    while s < c:
        e = jnp.where(same_block(2 * s) & jnp.logical_not(same_block(s)), a, 0.0)
        t = t - _dot_f32(t, _dot_f32(e, t))
        s *= 2
    return t


def _softplus(x):
    return jnp.maximum(x, 0.0) + _log1p_exp_neg(jnp.abs(x))


def _gdn_chunk(qkv, small, a_scale, dt_bias, s_ref, *, direction):
    c = qkv.shape[0]
    reverse = direction == 1
    ii = lax.broadcasted_iota(jnp.int32, (c, c), 0)
    jj = lax.broadcasted_iota(jnp.int32, (c, c), 1)
    incl = (jj >= ii) if reverse else (jj <= ii)
    strict = (jj > ii) if reverse else (jj < ii)
    end = 0 if reverse else c - 1

    log_alpha = a_scale * _softplus(small + dt_bias)
    g_all = _cumsum_rows(incl, log_alpha)
    g_all_t = jnp.concatenate([g_all, jnp.zeros((LANES - c, LANES), F32)], axis=0).T
    beta_all = jax.nn.sigmoid(small)

    outs = []
    for h in range(GDN_HEADS):
        hs = slice(h * HEAD_V, (h + 1) * HEAD_V)
        q = qkv[:, hs]
        k = qkv[:, GDN_KEY_WIDTH + h * HEAD_V:GDN_KEY_WIDTH + (h + 1) * HEAD_V]
        v = qkv[:, 2 * GDN_KEY_WIDTH + h * HEAD_V:2 * GDN_KEY_WIDTH + (h + 1) * HEAD_V].astype(F32)
        kf = k.astype(F32)
        lane_a = GDN_A_LANE + direction * GDN_HEADS + h
        lane_b = GDN_B_LANE + direction * GDN_HEADS + h
        gc = jnp.broadcast_to(g_all[:, lane_a:lane_a + 1], (c, HEAD_V))
        beta = jnp.broadcast_to(beta_all[:, lane_b:lane_b + 1], (c, HEAD_V))
        diff = gc[:, :c] - jnp.broadcast_to(g_all_t[lane_a:lane_a + 1, :c], (c, c))
        decay = jnp.where(incl, jnp.exp(jnp.where(incl, diff, 0.0)), 0.0)
        k_beta = kf * beta
        kk = lax.dot_general(k_beta.astype(BF16), k, (((1,), (1,)), ((), ())), preferred_element_type=F32)
        t_inv = _unit_triangular_inverse(jnp.where(strict, kk * decay, 0.0))
        eg = jnp.exp(gc)
        sol = _dot_f32(t_inv, jnp.concatenate([v * beta, k_beta * eg], axis=1))
        u, w = sol[:, :HEAD_V], sol[:, HEAD_V:]
        attn = lax.dot_general(q, k, (((1,), (1,)), ((), ())), preferred_element_type=F32) * decay
        g_end = gc[end:end + 1, :]
        q_dec = (q.astype(F32) * eg).astype(BF16)
        k_dec = (kf * jnp.exp(g_end - gc)).astype(BF16)
        state = s_ref[h]
        state_b = state.astype(BF16)
        v_new = u - jnp.dot(w.astype(BF16), state_b, preferred_element_type=F32)
        v_new_b = v_new.astype(BF16)
        outs.append(jnp.dot(q_dec, state_b, preferred_element_type=F32)
                    + jnp.dot(attn.astype(BF16), v_new_b, preferred_element_type=F32))
        s_ref[h] = state * jnp.exp(g_end) + lax.dot_general(k_dec, v_new_b, (((0,), (0,)), ((), ())),
                                                             preferred_element_type=F32)
    return jnp.concatenate(outs, axis=1)


def _gdn_scan_kernel(qf_ref, gf_ref, qb_ref, gb_ref, a_ref, dtb_ref, of_ref, ob_ref, sf_ref, sb_ref):
    @pl.when(pl.program_id(1) == 0)
    def _():
        sf_ref[...] = jnp.zeros_like(sf_ref)
        sb_ref[...] = jnp.zeros_like(sb_ref)

    n_chunks = of_ref.shape[0] // GDN_CHUNK

    def body(c, carry):
        rows = pl.ds(pl.multiple_of(c * GDN_CHUNK, GDN_CHUNK), GDN_CHUNK)
        of_ref[rows, :] = _gdn_chunk(qf_ref[rows, :], gf_ref[rows, :], a_ref[...], dtb_ref[...], sf_ref,
                                     direction=0)
        rows = pl.ds(pl.multiple_of((n_chunks - 1 - c) * GDN_CHUNK, GDN_CHUNK), GDN_CHUNK)
        ob_ref[rows, :] = _gdn_chunk(qb_ref[rows, :], gb_ref[rows, :], a_ref[...], dtb_ref[...], sb_ref,
                                     direction=1)
        return carry

    lax.fori_loop(0, n_chunks, body, 0)


def gated_deltanet_branch(pb, pf, batch, conv_w, a_log, dt_bias, norm_gain):
    m = pb.shape[0]
    nb = m // batch // LIN_BLOCK
    qkv = gdn_prep(pb, batch, conv_w)
    n_gate = 2 * GDN_HEADS
    a_scale = jnp.zeros((1, LANES), F32).at[0, GDN_A_LANE:GDN_A_LANE + n_gate].set(
        -jnp.exp(a_log.astype(F32)).reshape(n_gate))
    dtb = jnp.zeros((1, LANES), F32).at[0, GDN_A_LANE:GDN_A_LANE + n_gate].set(dt_bias.astype(F32).reshape(n_gate))
    small_col = PF_SMALL_COL // LANES

    def fwd(width, col):
        return pl.BlockSpec((LIN_BLOCK, width), lambda b, t: (b * nb + t, col))

    def bwd(width, col):
        return pl.BlockSpec((LIN_BLOCK, width), lambda b, t: (b * nb + nb - 1 - t, col))

    def whole(shape):
        return pl.BlockSpec(shape, functools.partial(lambda b, t, nd: (0,) * nd, nd=len(shape)))

    o_f, o_b = pl.pallas_call(
        _gdn_scan_kernel,
        out_shape=(jax.ShapeDtypeStruct((m, GDN_VAL_WIDTH), F32), jax.ShapeDtypeStruct((m, GDN_VAL_WIDTH), F32)),
        grid=(batch, nb),
        in_specs=[fwd(GDN_QKV_WIDTH, 0), fwd(LANES, small_col), bwd(GDN_QKV_WIDTH, 0), bwd(LANES, small_col),
                  whole((1, LANES)), whole((1, LANES))],
        out_specs=(fwd(GDN_VAL_WIDTH, 0), bwd(GDN_VAL_WIDTH, 0)),
        scratch_shapes=[pltpu.VMEM((GDN_HEADS, GDN_HEAD_K, GDN_HEAD_V), F32),
                        pltpu.VMEM((GDN_HEADS, GDN_HEAD_K, GDN_HEAD_V), F32)],
        compiler_params=_params("parallel", "arbitrary"),
        name="gdn_scan",
    )(qkv, pf, qkv, pf, a_scale, dtb)
    return finish_branch(o_f, o_b, pb, "gdn_og", norm_gain, silu_gate=True)


GDN_PACK = GDN_HEADS * GDN_CHUNK
GDN_WY_BLOCK = 512


def _stack_heads(x, width):
    heads = x.shape[1] // width
    lane = lax.broadcasted_iota(jnp.int32, (1, x.shape[1]), 1)
    return jnp.concatenate(
        [jnp.where((lane >= h * width) & (lane < (h + 1) * width), x, 0.0).astype(BF16) for h in range(heads)],
        axis=0)


def _packed_mm(x, y):
    return jnp.dot(x.astype(BF16), _stack_heads(y, GDN_CHUNK), preferred_element_type=F32)


def _packed_inverses(mats):
    c = GDN_CHUNK
    ii = lax.broadcasted_iota(jnp.int32, (c, GDN_PACK), 0)
    jj = lax.broadcasted_iota(jnp.int32, (c, GDN_PACK), 1) % c
    eye = (ii == jj).astype(F32)

    def same_block(s):
        return (ii // s) == (jj // s)

    ds = [jnp.where(same_block(8), a, 0.0) for a in mats]
    d2s = [_packed_mm(d, d) for d in ds]
    d4s = [_packed_mm(d2, d2) for d2 in d2s]
    ts = [_packed_mm(eye - d, eye + d2) for d, d2 in zip(ds, d2s)]
    ts = [_packed_mm(t, eye + d4) for t, d4 in zip(ts, d4s)]
    s = 8
    while s < c:
        off = same_block(2 * s) & jnp.logical_not(same_block(s))
        ets = [_packed_mm(jnp.where(off, a, 0.0), t) for a, t in zip(mats, ts)]
        ts = [t - _packed_mm(t, et) for t, et in zip(ts, ets)]
        s *= 2
    return ts


def _gdn_wy_kernel(qkv_ref, small_ref, a_ref, dtb_ref, selg_ref, selk_ref, selb_ref, *out_refs):
    c = GDN_CHUNK
    n_chunks = qkv_ref.shape[0] // c
    ii = lax.broadcasted_iota(jnp.int32, (c, c), 0)
    jj = lax.broadcasted_iota(jnp.int32, (c, c), 1)
    pi = lax.broadcasted_iota(jnp.int32, (c, GDN_PACK), 0)
    pj = lax.broadcasted_iota(jnp.int32, (c, GDN_PACK), 1) % c
    eye_p = (pi == pj).astype(F32)
    ones_cc = jnp.ones((c, c), BF16)

    problems = [(ch, d) for ch in range(n_chunks) for d in range(2)]
    chunk_in = []
    for ch in range(n_chunks):
        rows = slice(ch * c, (ch + 1) * c)
        qkv = qkv_ref[rows, :]
        small = small_ref[rows, :]
        kf = qkv[:, GDN_KEY_WIDTH:2 * GDN_KEY_WIDTH].astype(F32)
        chunk_in.append(dict(
            qf=qkv[:, :GDN_KEY_WIDTH].astype(F32), kf=kf, vf=qkv[:, 2 * GDN_KEY_WIDTH:].astype(F32),
            kbd=_stack_heads(kf, HEAD_V),
            log_alpha=a_ref[...] * _softplus(small + dtb_ref[...]),
            beta_all=jax.nn.sigmoid(small)))

    def sel3(x, sel):
        hi, mid, lo = _split_bf16(x, 3)
        return (jnp.dot(hi, sel, preferred_element_type=F32)
                + (jnp.dot(mid, sel, preferred_element_type=F32) + jnp.dot(lo, sel, preferred_element_type=F32)))

    g_all = [_cumsum_rows((jj >= ii) if d else (jj <= ii), chunk_in[ch]["log_alpha"]) for ch, d in problems]
    g_pack = [sel3(g, selg_ref[d]) for g, (ch, d) in zip(g_all, problems)]
    g_wide = [sel3(g, selk_ref[d]) for g, (ch, d) in zip(g_all, problems)]
    beta_w = [sel3(chunk_in[ch]["beta_all"], selb_ref[d]) for ch, d in problems]
    g_rowp = []
    for gp in g_pack:
        hi, mid, lo = _split_bf16(gp * eye_p, 3)
        g_rowp.append(jnp.dot(ones_cc, hi, preferred_element_type=F32)
                      + (jnp.dot(ones_cc, mid, preferred_element_type=F32)
                         + jnp.dot(ones_cc, lo, preferred_element_type=F32)))
    decays, k_betas = [], []
    for gp, gr, bw, (ch, d) in zip(g_pack, g_rowp, beta_w, problems):
        incl = (pj >= pi) if d else (pj <= pi)
        decays.append(jnp.where(incl, jnp.exp(jnp.where(incl, gp - gr, 0.0)), 0.0))
        k_betas.append(chunk_in[ch]["kf"] * bw)
    kq = [lax.dot_general(jnp.concatenate([kb, chunk_in[ch]["qf"]], axis=0).astype(BF16), chunk_in[ch]["kbd"],
                          (((1,), (1,)), ((), ())), preferred_element_type=F32)
          for kb, (ch, d) in zip(k_betas, problems)]
    a_mats = []
    for x, dec, (ch, d) in zip(kq, decays, problems):
        strict = (pj > pi) if d else (pj < pi)
        a_mats.append(jnp.where(strict, x[:c] * dec, 0.0))
    t_invs = _packed_inverses(a_mats)

    for idx, (ch, d) in enumerate(problems):
        u_ref, w_ref, attn_ref, qd_ref, kd_ref, gt_ref = out_refs[6 * d:6 * d + 6]
        rows = slice(ch * c, (ch + 1) * c)
        cin = chunk_in[ch]
        gw = g_wide[idx]
        eg = jnp.exp(gw)
        t_b = t_invs[idx].astype(BF16)
        u_ref[rows, :] = jnp.dot(t_b, _stack_heads(cin["vf"] * beta_w[idx], HEAD_V), preferred_element_type=F32)
        w_ref[rows, :] = jnp.dot(t_b, _stack_heads(k_betas[idx] * eg, HEAD_V),
                                 preferred_element_type=F32).astype(w_ref.dtype)
        attn_ref[rows, :] = (kq[idx][c:] * decays[idx]).astype(attn_ref.dtype)
        end = 0 if d else c - 1
        g_end = gw[end:end + 1, :]
        qd_ref[rows, :] = (cin["qf"] * eg).astype(qd_ref.dtype)
        kd_ref[rows, :] = (cin["kf"] * jnp.exp(g_end - gw)).astype(kd_ref.dtype)
        gt_ref[ch:ch + 1, :] = jnp.exp(g_end)


def gdn_wy(qkv, pf, a_scale, dtb):
    m = qkv.shape[0]
    t = GDN_WY_BLOCK
    cpb = t // GDN_CHUNK
    selg = np.zeros((2, LANES, GDN_PACK), np.float32)
    selk = np.zeros((2, LANES, GDN_VAL_WIDTH), np.float32)
    selb = np.zeros((2, LANES, GDN_VAL_WIDTH), np.float32)
    for d in range(2):
        for h in range(GDN_HEADS):
            selg[d, GDN_A_LANE + d * GDN_HEADS + h, h * GDN_CHUNK:(h + 1) * GDN_CHUNK] = 1.0
            selk[d, GDN_A_LANE + d * GDN_HEADS + h, h * HEAD_V:(h + 1) * HEAD_V] = 1.0
            selb[d, GDN_B_LANE + d * GDN_HEADS + h, h * HEAD_V:(h + 1) * HEAD_V] = 1.0
    wide = GDN_VAL_WIDTH
    out_shape, out_specs = [], []
    for _ in range(2):
        for width, dt in ((wide, F32), (wide, BF16), (GDN_PACK, BF16), (wide, BF16), (wide, BF16)):
            out_shape.append(jax.ShapeDtypeStruct((m, width), dt))
            out_specs.append(pl.BlockSpec((t, width), lambda i: (i, 0)))
        out_shape.append(jax.ShapeDtypeStruct((m // GDN_CHUNK, wide), F32))
        out_specs.append(pl.BlockSpec((cpb, wide), lambda i: (i, 0)))
    return pl.pallas_call(
        _gdn_wy_kernel,
        out_shape=tuple(out_shape),
        grid=(m // t,),
        in_specs=[
            pl.BlockSpec((t, GDN_QKV_WIDTH), lambda i: (i, 0)),
            pl.BlockSpec((t, LANES), lambda i: (i, PF_SMALL_COL // LANES)),
            pl.BlockSpec((1, LANES), lambda i: (0, 0)),
            pl.BlockSpec((1, LANES), lambda i: (0, 0)),
            pl.BlockSpec((2, LANES, GDN_PACK), lambda i: (0, 0, 0)),
            pl.BlockSpec((2, LANES, wide), lambda i: (0, 0, 0)),
            pl.BlockSpec((2, LANES, wide), lambda i: (0, 0, 0)),
        ],
        out_specs=tuple(out_specs),
        compiler_params=_params("parallel"),
        name="gdn_wy",
    )(qkv, pf, a_scale, dtb, jnp.asarray(selg, BF16), jnp.asarray(selk, BF16), jnp.asarray(selb, BF16))


GDN_PAIR = 2 * HEAD_V


def _gdn_scan2_kernel(*refs):
    groups = (refs[0:6], refs[6:12])
    out_refs = refs[12:14]
    state_refs = refs[14:16]

    @pl.when(pl.program_id(1) == 0)
    def _():
        for s_ref in state_refs:
            s_ref[...] = jnp.zeros_like(s_ref)

    n_chunks = out_refs[0].shape[0] // GDN_CHUNK
    pairs = GDN_HEADS // 2
    pair_cols = [slice(p * GDN_PAIR, (p + 1) * GDN_PAIR) for p in range(pairs)]
    ri = lax.broadcasted_iota(jnp.int32, (GDN_PAIR, GDN_PAIR), 0) // HEAD_V
    ci = lax.broadcasted_iota(jnp.int32, (GDN_PAIR, GDN_PAIR), 1) // HEAD_V
    diag = ri == ci

    def body(c, carry):
        chunks = (c, n_chunks - 1 - c)
        rows = [pl.ds(pl.multiple_of(ch * GDN_CHUNK, GDN_CHUNK), GDN_CHUNK) for ch in chunks]
        states = [[s_ref[p] for p in range(pairs)] for s_ref in state_refs]
        states_b = [[s.astype(BF16) for s in st] for st in states]
        ws = [[jnp.dot(groups[g][1][rows[g], cols], states_b[g][p], preferred_element_type=F32)
               for p, cols in enumerate(pair_cols)] for g in range(2)]
        qs = [[jnp.dot(groups[g][3][rows[g], cols], states_b[g][p], preferred_element_type=F32)
               for p, cols in enumerate(pair_cols)] for g in range(2)]
        v_new = [groups[g][0][rows[g], :] - jnp.concatenate(ws[g], axis=1) for g in range(2)]
        av = [jnp.dot(groups[g][2][rows[g], :], _stack_heads(v_new[g], HEAD_V), preferred_element_type=F32)
              for g in range(2)]
        v_new_b = [v.astype(BF16) for v in v_new]
        upd = [[lax.dot_general(groups[g][4][rows[g], cols], v_new_b[g][:, cols], (((0,), (0,)), ((), ())),
                                preferred_element_type=F32) for cols in pair_cols] for g in range(2)]
        for g in range(2):
            out_refs[g][rows[g], :] = jnp.concatenate(qs[g], axis=1) + av[g]
            gt = groups[g][5][pl.ds(chunks[g], 1), :]
            for p, cols in enumerate(pair_cols):
                state_refs[g][p] = states[g][p] * gt[:, cols] + jnp.where(diag, upd[g][p], 0.0)
        return carry

    lax.fori_loop(0, n_chunks, body, 0)


def gated_deltanet_branch(pb, pf, batch, conv_w, a_log, dt_bias, norm_gain):
    m = pb.shape[0]
    nb = m // batch // LIN_BLOCK
    cpb = LIN_BLOCK // GDN_CHUNK
    qkv = gdn_prep(pb, batch, conv_w)
    n_gate = 2 * GDN_HEADS
    a_scale = jnp.zeros((1, LANES), F32).at[0, GDN_A_LANE:GDN_A_LANE + n_gate].set(
        -jnp.exp(a_log.astype(F32)).reshape(n_gate))
    dtb = jnp.zeros((1, LANES), F32).at[0, GDN_A_LANE:GDN_A_LANE + n_gate].set(dt_bias.astype(F32).reshape(n_gate))
    wy = gdn_wy(qkv, pf, a_scale, dtb)
    widths = (GDN_VAL_WIDTH, GDN_VAL_WIDTH, GDN_PACK, GDN_VAL_WIDTH, GDN_VAL_WIDTH)

    def fwd(rows, width):
        return pl.BlockSpec((rows, width), lambda b, t: (b * nb + t, 0))

    def bwd(rows, width):
        return pl.BlockSpec((rows, width), lambda b, t: (b * nb + nb - 1 - t, 0))

    in_specs = [fwd(LIN_BLOCK, w) for w in widths] + [fwd(cpb, GDN_VAL_WIDTH)]
    in_specs += [bwd(LIN_BLOCK, w) for w in widths] + [bwd(cpb, GDN_VAL_WIDTH)]
    state = pltpu.VMEM((GDN_HEADS // 2, GDN_PAIR, GDN_PAIR), F32)
    o_f, o_b = pl.pallas_call(
        _gdn_scan2_kernel,
        out_shape=(jax.ShapeDtypeStruct((m, GDN_VAL_WIDTH), F32), jax.ShapeDtypeStruct((m, GDN_VAL_WIDTH), F32)),
        grid=(batch, nb),
        in_specs=in_specs,
        out_specs=(fwd(LIN_BLOCK, GDN_VAL_WIDTH), bwd(LIN_BLOCK, GDN_VAL_WIDTH)),
        scratch_shapes=[state, state],
        compiler_params=_params("parallel", "arbitrary"),
        name="gdn_scan",
    )(*wy)
    return finish_branch(o_f, o_b, pb, "gdn_og", norm_gain, silu_gate=True)


def _dot3(m, x):
    hi, mid, lo = _split_bf16(x, 3)
    return (jnp.dot(m, hi, preferred_element_type=F32)
            + (jnp.dot(m, mid, preferred_element_type=F32) + jnp.dot(m, lo, preferred_element_type=F32)))


def _lin_masks(block, chunk):
    i = np.arange(block)[:, None]
    j = np.arange(block)[None, :]
    same = (i // chunk) == (j // chunk)
    big = np.zeros((2, 3, block, block), np.float32)
    ends = np.zeros((2, block // chunk, block), np.float32)
    for d in range(2):
        mid = chunk - 1 - chunk // 2 if d else chunk // 2
        end = 0 if d else chunk - 1
        big[d, 0] = same & ((j >= i) if d else (j <= i))
        big[d, 1] = j == (i // chunk) * chunk + mid
        big[d, 2] = j == (i // chunk) * chunk + end
        ends[d] = j == np.arange(block // chunk)[:, None] * chunk + end
    return jnp.asarray(big, BF16), jnp.asarray(ends, BF16)


LIN_CUM_ROWS = 256
LIN_SCORE_ROWS = 128


def _chunk_causal(n, chunk, reverse):
    i = lax.broadcasted_iota(jnp.int32, (n, n), 0)
    j = lax.broadcasted_iota(jnp.int32, (n, n), 1)
    return ((i // chunk) == (j // chunk)) & ((j >= i) if reverse else (j <= i))


def _lin_intra_kernel(*refs, load_inputs, n_in, n_params, heads):
    dir_refs = (refs[:n_in], refs[n_in:2 * n_in])
    params = refs[2 * n_in:2 * n_in + n_params]
    out_refs = refs[2 * n_in + n_params:]
    c = LIN_CHUNK
    dirs = (0, 1)
    loaded = [load_inputs(dir_refs[d], slice(None), d, params) for d in dirs]
    t, w = loaded[0][0].shape
    dk = w // heads
    nc = t // c
    cums = [jnp.where(_chunk_causal(LIN_CUM_ROWS, c, d == 1), 1.0, 0.0).astype(BF16) for d in dirs]
    bs = [jnp.concatenate([_dot3(cums[d], loaded[d][3][r:r + LIN_CUM_ROWS, :])
                           for r in range(0, t, LIN_CUM_ROWS)], axis=0) for d in dirs]
    qes, kes = [], []
    for d in dirs:
        oi_ref, qd_ref, kd_ref, gt_ref = out_refs[4 * d:4 * d + 4]
        qc, kc, vc, lg = loaded[d]
        b = bs[d]
        b3 = b.reshape(nc, c, w)
        mid = c - 1 - c // 2 if d else c // 2
        end = 0 if d else c - 1
        b_mid = jnp.broadcast_to(b3[:, mid:mid + 1, :], (nc, c, w)).reshape(t, w)
        b_end = jnp.broadcast_to(b3[:, end:end + 1, :], (nc, c, w)).reshape(t, w)
        qes.append((qc * jnp.exp(b - b_mid)).astype(BF16))
        kes.append((kc * jnp.exp(b_mid - b)).astype(BF16))
        qd_ref[...] = (qc * jnp.exp(b)).astype(qd_ref.dtype)
        kd_ref[...] = (kc * jnp.exp(b_end - b)).astype(kd_ref.dtype)
        gt_ref[...] = jnp.exp(b3[:, end, :])
    keeps = [_chunk_causal(LIN_SCORE_ROWS, c, d == 1) for d in dirs]
    lane = lax.broadcasted_iota(jnp.int32, (1, LANES), 1)
    for h in range(heads):
        win = slice((h * dk) // LANES * LANES, (h * dk) // LANES * LANES + LANES)
        lo = h * dk - win.start
        vcols = slice(h * HEAD_V, (h + 1) * HEAD_V)
        tiles = [(slice(r, r + LIN_SCORE_ROWS), d) for r in range(0, t, LIN_SCORE_ROWS) for d in dirs]
        scores = []
        for rows, d in tiles:
            qh = qes[d][rows, win]
            if dk < LANES:
                qh = jnp.where((lane >= lo) & (lane < lo + dk), qh, jnp.zeros_like(qh))
            scores.append(lax.dot_general(qh, kes[d][rows, win], (((1,), (1,)), ((), ())),
                                          preferred_element_type=F32))
        probs = [jnp.where(keeps[d], s, 0.0).astype(BF16) for s, (rows, d) in zip(scores, tiles)]
        for p, (rows, d) in zip(probs, tiles):
            out_refs[4 * d][rows, vcols] = jnp.dot(p, loaded[d][2][rows, vcols], preferred_element_type=F32)


def _lin_scan_kernel(*refs, heads, chunk, unroll):
    groups = (refs[0:5], refs[5:10])
    out_refs = refs[10:12]
    state_refs = refs[12:14]

    @pl.when(pl.program_id(1) == 0)
    def _():
        for s_ref in state_refs:
            s_ref[...] = jnp.zeros_like(s_ref)

    n_chunks = out_refs[0].shape[0] // chunk
    w = state_refs[0].shape[1]
    dk = w // heads
    lane = lax.broadcasted_iota(jnp.int32, (1, w), 1)
    masks = [(lane >= h * dk) & (lane < (h + 1) * dk) for h in range(heads)]

    def stack(x):
        return jnp.concatenate([jnp.where(m, x, jnp.zeros_like(x)) for m in masks], axis=0)

    def body(it, carry):
        steps = []
        for u in range(unroll):
            c = it * unroll + u
            steps += [(0, c), (1, n_chunks - 1 - c)]
        prepared = []
        for g, ch in steps:
            rows = pl.ds(pl.multiple_of(ch * chunk, chunk), chunk)
            oi_ref, qd_ref, kd_ref, v_ref, gt_ref = groups[g]
            vc = v_ref[rows, :]
            v4 = jnp.concatenate([vc[:, h * HEAD_V:(h + 1) * HEAD_V] for h in range(heads)], axis=0)
            upd = lax.dot_general(v4, stack(kd_ref[rows, :]), (((0,), (0,)), ((), ())),
                                  preferred_element_type=F32)
            prepared.append((rows, stack(qd_ref[rows, :]), upd, gt_ref[pl.ds(ch, 1), :]))
        states = [s_ref[...] for s_ref in state_refs]
        for (g, ch), (rows, q4, upd, gt) in zip(steps, prepared):
            o_inter = lax.dot_general(q4, states[g].astype(BF16), (((1,), (1,)), ((), ())),
                                      preferred_element_type=F32)
            out_refs[g][rows, :] = groups[g][0][rows, :] + jnp.concatenate(
                [o_inter[h * chunk:(h + 1) * chunk, :] for h in range(heads)], axis=1)
            states[g] = states[g] * gt + upd
        for s_ref, st in zip(state_refs, states):
            s_ref[...] = st
        return carry

    lax.fori_loop(0, n_chunks // unroll, body, 0)


def _bidir_lin_call(name, load_inputs, arrays, col_blocks, widths, params, batch, heads, key_width, v_col):
    m = arrays[0].shape[0]
    t = LIN_BLOCK
    nb = m // batch // t
    cpb = t // LIN_CHUNK
    out_w = heads * HEAD_V
    n_in = len(arrays)

    in_specs, operands = [], []
    for d in range(2):
        for a, wd, cb in zip(arrays, widths, col_blocks):
            in_specs.append(pl.BlockSpec((t, wd), functools.partial(lambda i, c: (i, c), c=cb[d])))
            operands.append(a)
    for p in params:
        in_specs.append(pl.BlockSpec(p.shape, functools.partial(lambda i, nd: (0,) * nd, nd=p.ndim)))
    out_shape, out_specs = [], []
    for _ in range(2):
        for rows_total, rows_blk, width, dt in ((m, t, out_w, F32), (m, t, key_width, BF16),
                                                (m, t, key_width, BF16), (m // LIN_CHUNK, cpb, key_width, F32)):
            out_shape.append(jax.ShapeDtypeStruct((rows_total, width), dt))
            out_specs.append(pl.BlockSpec((rows_blk, width), lambda i: (i, 0)))

    intra = pl.pallas_call(
        functools.partial(_lin_intra_kernel, load_inputs=load_inputs, n_in=n_in, n_params=len(params),
                          heads=heads),
        out_shape=tuple(out_shape),
        grid=(m // t,),
        in_specs=in_specs,
        out_specs=tuple(out_specs),
        compiler_params=_params("parallel"),
        name=name + "_intra",
    )(*operands, *params)

    def fwd(rows, width, col=0):
        return pl.BlockSpec((rows, width), lambda b, s: (b * nb + s, col))

    def bwd(rows, width, col=0):
        return pl.BlockSpec((rows, width), lambda b, s: (b * nb + nb - 1 - s, col))

    scan_specs, scan_ops = [], []
    for d, mk in enumerate((fwd, bwd)):
        oi, qd, kd, gt = intra[4 * d:4 * d + 4]
        scan_specs += [mk(t, out_w), mk(t, key_width), mk(t, key_width), mk(t, out_w, v_col), mk(cpb, key_width)]
        scan_ops += [oi, qd, kd, arrays[0], gt]
    state = pltpu.VMEM((HEAD_V, key_width), F32)
    return pl.pallas_call(
        functools.partial(_lin_scan_kernel, heads=heads, chunk=LIN_CHUNK, unroll=4),
        out_shape=(jax.ShapeDtypeStruct((m, out_w), F32), jax.ShapeDtypeStruct((m, out_w), F32)),
        grid=(batch, nb),
        in_specs=scan_specs,
        out_specs=(fwd(t, out_w), bwd(t, out_w)),
        scratch_shapes=[state, state],
        compiler_params=_params("parallel", "arbitrary"),
        name=name + "_scan",
    )(*scan_ops)


def gla_branch(pb, pf, batch, w_gate_up, b_gate, norm_gain):
    wpad = jnp.zeros((2, LANES, GLA_KEY_WIDTH), F32)
    for d in range(2):
        wpad = wpad.at[d, d * GLA_GATE_RANK:(d + 1) * GLA_GATE_RANK, :].set(w_gate_up[d].astype(F32))
    bias = b_gate.astype(F32).reshape(2, 1, GLA_KEY_WIDTH)
    v_col = PB_COL["gla_v"] // GLA_VAL_WIDTH
    cols = [(PB_COL["gla_q"] // GLA_KEY_WIDTH,) * 2, (PB_COL["gla_k"] // GLA_KEY_WIDTH,) * 2,
            (v_col,) * 2, (PF_SMALL_COL // LANES,) * 2]
    o_f, o_b = _bidir_lin_call("gla", _gla_inputs, [pb, pb, pb, pf], cols,
                               [GLA_KEY_WIDTH, GLA_KEY_WIDTH, GLA_VAL_WIDTH, LANES], [wpad, bias],
                               batch, GLA_HEADS, GLA_KEY_WIDTH, v_col)
    return finish_branch(o_f, o_b, pb, "gla_og", norm_gain, silu_gate=True)


def hgrn2_branch(pb, pf, batch, lower_bound, norm_gain):
    lb = lower_bound.astype(F32).reshape(2, 1, HGRN_KEY_WIDTH)
    log_lb = jnp.log(jnp.maximum(lb, LB_FLOOR))
    log1m_lb = jnp.log1p(-lb)
    zc = PF_COL["hg_f"] // HGRN_KEY_WIDTH
    v_col = PB_COL["hg_i"] // HGRN_VAL_WIDTH
    cols = [(PB_COL["hg_q"] // HGRN_KEY_WIDTH,) * 2, (v_col,) * 2, (zc, zc + 1)]
    o_f, o_b = _bidir_lin_call("hgrn2", _hgrn_inputs, [pb, pb, pf], cols,
                               [HGRN_KEY_WIDTH, HGRN_VAL_WIDTH, HGRN_KEY_WIDTH], [lb, log_lb, log1m_lb],
                               batch, HGRN_HEADS, HGRN_KEY_WIDTH, v_col)
    return finish_branch(o_f, o_b, pb, "hg_og", norm_gain, silu_gate=False)


def _rms_norm(x, gain, eps=RMS_EPS):
    xf = x.astype(F32)
    y = xf * lax.rsqrt(jnp.mean(xf * xf, axis=-1, keepdims=True) + eps)
    return (y * gain.astype(F32)).astype(x.dtype)


def _l2_norm(x, eps=1e-6):
    xf = x.astype(F32)
    return xf * lax.rsqrt(jnp.sum(xf * xf, axis=-1, keepdims=True) + eps)


def _rev(t):
    return jnp.flip(t, axis=1)


def _centred_depthwise_conv(x, w):
    width = w.shape[0]
    return lax.conv_general_dilated(
        x, w[:, None, :], window_strides=(1,), padding=[(width // 2, width // 2)],
        dimension_numbers=("NWC", "WIO", "NWC"), feature_group_count=x.shape[-1])


def _chunk_gla(q, k, v, log_g):
    B, S, H, K = q.shape
    V = v.shape[-1]
    C = LIN_CHUNK
    n = S // C

    def chunks(t):
        return t.reshape(B, n, C, H, t.shape[-1]).transpose(1, 0, 3, 2, 4)

    q, k, v, log_g = chunks(q), chunks(k), chunks(v), chunks(log_g)
    b = jnp.cumsum(log_g, axis=-2)
    b_ref = b[..., C // 2:C // 2 + 1, :]
    incl = jnp.tril(jnp.ones((C, C), dtype=bool))
    scores = jnp.einsum("nbhik,nbhjk->nbhij", q * jnp.exp(b - b_ref), k * jnp.exp(b_ref - b))
    o_intra = jnp.einsum("nbhij,nbhjv->nbhiv", jnp.where(incl, scores, 0.0), v)
    q_dec = q * jnp.exp(b)
    k_dec = k * jnp.exp(b[..., -1:, :] - b)
    g_tot = jnp.exp(b[..., -1, :])

    def step(state, xs):
        q_c, k_c, v_c, g_c = xs
        o_c = jnp.einsum("bhik,bhkv->bhiv", q_c, state)
        state = state * g_c[..., None] + jnp.einsum("bhjk,bhjv->bhkv", k_c, v_c)
        return state, o_c

    _, o_inter = lax.scan(step, jnp.zeros((B, H, K, V), F32), (q_dec, k_dec, v, g_tot))
    o = o_intra + o_inter
    return o.transpose(1, 0, 3, 2, 4).reshape(B, S, H, V)


def _chunk_gdn(q, k, v, log_alpha, beta):
    B, S, H, K = q.shape
    V = v.shape[-1]
    C = GDN_CHUNK
    n = S // C

    def chunks(t):
        return t.reshape(B, n, C, H, t.shape[-1]).transpose(1, 0, 3, 2, 4)

    q, k, v = chunks(q), chunks(k), chunks(v)
    g = jnp.cumsum(chunks(log_alpha[..., None])[..., 0], axis=-1)
    beta = chunks(beta[..., None])
    incl = jnp.tril(jnp.ones((C, C), dtype=bool))
    strict = jnp.tril(jnp.ones((C, C), dtype=bool), -1)
    diff = g[..., :, None] - g[..., None, :]
    decay = jnp.where(incl, jnp.exp(jnp.where(incl, diff, 0.0)), 0.0)
    k_beta = k * beta
    a = jnp.where(strict, jnp.einsum("nbhik,nbhjk->nbhij", k_beta, k) * decay, 0.0)
    rhs = jnp.concatenate([v * beta, k_beta * jnp.exp(g)[..., None]], axis=-1)
    sol = lax.linalg.triangular_solve(a + jnp.eye(C, dtype=F32), rhs, left_side=True, lower=True)
    u, w = sol[..., :V], sol[..., V:]
    attn = jnp.einsum("nbhik,nbhjk->nbhij", q, k) * decay
    q_dec = q * jnp.exp(g)[..., None]
    k_dec = k * jnp.exp(g[..., -1:] - g)[..., None]
    g_tot = jnp.exp(g[..., -1])

    def step(state, xs):
        u_c, w_c, attn_c, q_c, k_c, g_c = xs
        v_new = u_c - jnp.einsum("bhck,bhkv->bhcv", w_c, state)
        o_c = jnp.einsum("bhck,bhkv->bhcv", q_c, state) + jnp.einsum("bhij,bhjv->bhiv", attn_c, v_new)
        state = state * g_c[..., None, None] + jnp.einsum("bhck,bhcv->bhkv", k_c, v_new)
        return state, o_c

    _, o = lax.scan(step, jnp.zeros((B, H, K, V), F32), (u, w, attn, q_dec, k_dec, g_tot))
    return o.transpose(1, 0, 3, 2, 4).reshape(B, S, H, V)


def _neighbourhood_attention(q, k, v, q_gain, k_gain, rel_bias):
    B, S, _ = q.shape
    rows = S // GRID_W
    win_rows = min(NA_WIN_ROWS, rows)

    def grid(t):
        return t.reshape(B, rows, GRID_W, NA_HEADS, NA_HEAD_DIM)

    q = _rms_norm(grid(q), q_gain).astype(F32) * (NA_HEAD_DIM ** -0.5)
    k = _rms_norm(grid(k), k_gain).astype(F32)
    v = grid(v).astype(F32)
    r = jnp.arange(rows)
    c = jnp.arange(GRID_W)
    row_idx = jnp.clip(r - win_rows // 2, 0, rows - win_rows)[:, None] + jnp.arange(win_rows)[None, :]
    col_start = jnp.clip(c - NA_WIN_COLS // 2, 0, GRID_W - NA_WIN_COLS)
    col_in = (c[None, :] >= col_start[:, None]) & (c[None, :] < col_start[:, None] + NA_WIN_COLS)
    k_band = k[:, row_idx]
    v_band = v[:, row_idx]
    s = jnp.einsum("brqhd,brikhd->bhrqik", q, k_band)
    dr = row_idx - r[:, None] + (NA_WIN_ROWS - 1)
    dc = jnp.clip(c[None, :] - c[:, None], 1 - NA_WIN_COLS, NA_WIN_COLS - 1) + (NA_WIN_COLS - 1)
    bias = rel_bias.astype(F32)[:, dr[:, None, :, None], dc[None, :, None, :]]
    s = jnp.where(col_in[:, None, :], s + bias[None], MASK_VALUE)
    p = jax.nn.softmax(s, axis=(-2, -1))
    o = jnp.einsum("bhrqik,brikhd->brqhd", p, v_band)
    return o.reshape(B, S, NA_WIDTH)


def _gla_branch(q, k, v, gate_lr, out_gate, w_gate_up, b_gate, norm_gain):
    B, S, _ = q.shape
    q = q.astype(F32).reshape(B, S, GLA_HEADS, GLA_HEAD_K) * (GLA_HEAD_K ** -0.5)
    k = k.astype(F32).reshape(B, S, GLA_HEADS, GLA_HEAD_K)
    v = v.astype(F32).reshape(B, S, GLA_HEADS, GLA_HEAD_V)
    lr = gate_lr.astype(F32).reshape(B, S, 2, GLA_GATE_RANK)
    gk = jnp.einsum("bsdr,drk->bsdk", lr, w_gate_up.astype(F32)) + b_gate.astype(F32)
    log_g = (jax.nn.log_sigmoid(gk) / GLA_GATE_NORMALIZER).reshape(B, S, 2, GLA_HEADS, GLA_HEAD_K)
    o = (_chunk_gla(q, k, v, log_g[:, :, 0])
         + _rev(_chunk_gla(_rev(q), _rev(k), _rev(v), _rev(log_g[:, :, 1]))))
    o = _rms_norm(o, norm_gain) * jax.nn.silu(out_gate.astype(F32)).reshape(B, S, GLA_HEADS, GLA_HEAD_V)
    return o.reshape(B, S, GLA_VAL_WIDTH)


def _gdn_branch(qkv, a, b, out_gate, conv_w, a_log, dt_bias, norm_gain):
    B, S, _ = qkv.shape
    qkv = jax.nn.silu(_centred_depthwise_conv(qkv.astype(F32), conv_w.astype(F32)))
    q, k, v = jnp.split(qkv, [GDN_KEY_WIDTH, 2 * GDN_KEY_WIDTH], axis=-1)
    q = _l2_norm(q.reshape(B, S, GDN_HEADS, GDN_HEAD_K)) * (GDN_HEAD_K ** -0.5)
    k = _l2_norm(k.reshape(B, S, GDN_HEADS, GDN_HEAD_K))
    v = v.reshape(B, S, GDN_HEADS, GDN_HEAD_V)
    a = a.astype(F32).reshape(B, S, 2, GDN_HEADS)
    b = b.astype(F32).reshape(B, S, 2, GDN_HEADS)
    log_alpha = -jnp.exp(a_log.astype(F32)) * jax.nn.softplus(a + dt_bias.astype(F32))
    beta = jax.nn.sigmoid(b)
    o = (_chunk_gdn(q, k, v, log_alpha[:, :, 0], beta[:, :, 0])
         + _rev(_chunk_gdn(_rev(q), _rev(k), _rev(v), _rev(log_alpha[:, :, 1]), _rev(beta[:, :, 1]))))
    o = _rms_norm(o, norm_gain) * jax.nn.silu(out_gate.astype(F32)).reshape(B, S, GDN_HEADS, GDN_HEAD_V)
    return o.reshape(B, S, GDN_VAL_WIDTH)


def _hgrn2_branch(q, f_pre, i, out_gate, lower_bound, norm_gain):
    B, S, _ = q.shape
    q = jax.nn.silu(q.astype(F32)).reshape(B, S, HGRN_HEADS, HGRN_HEAD_K)
    z = f_pre.astype(F32).reshape(B, S, 2, HGRN_KEY_WIDTH)
    lb = lower_bound.astype(F32)
    log_f = jnp.logaddexp(jnp.log(jnp.maximum(lb, LB_FLOOR)), jnp.log1p(-lb) + jax.nn.log_sigmoid(z))
    k_in = (1.0 - lb) * jax.nn.sigmoid(-z)
    log_f = log_f.reshape(B, S, 2, HGRN_HEADS, HGRN_HEAD_K)
    k_in = k_in.reshape(B, S, 2, HGRN_HEADS, HGRN_HEAD_K)
    v = i.astype(F32).reshape(B, S, HGRN_HEADS, HGRN_HEAD_V)
    o = (_chunk_gla(q, k_in[:, :, 0], v, log_f[:, :, 0])
         + _rev(_chunk_gla(_rev(q), _rev(k_in[:, :, 1]), _rev(v), _rev(log_f[:, :, 1]))))
    o = _rms_norm(o, norm_gain) * jax.nn.sigmoid(out_gate.astype(F32)).reshape(B, S, HGRN_HEADS, HGRN_HEAD_V)
    return o.reshape(B, S, HGRN_VAL_WIDTH)


def _memory_cross_attention(q, kv, q_gain, k_gain):
    B, S, _ = q.shape
    M = kv.shape[1]
    q = _rms_norm(q.reshape(B, S, MEM_HEADS, MEM_HEAD_DIM), q_gain).astype(F32)
    k, v = jnp.split(kv, 2, axis=-1)
    k = _rms_norm(k.reshape(B, M, MEM_HEADS, MEM_HEAD_DIM), k_gain).astype(F32)
    v = v.reshape(B, M, MEM_HEADS, MEM_HEAD_DIM).astype(F32)
    s = jnp.einsum("bshd,bmhd->bhsm", q, k) * (MEM_HEAD_DIM ** -0.5)
    p = jax.nn.softmax(s, axis=-1)
    o = jnp.einsum("bhsm,bmhd->bshd", p, v)
    return o.reshape(B, S, MEM_WIDTH)


def kernel(x, mem, g_mix, w_in, na_q_gain, na_k_gain, na_rel_bias, gla_w_gate_up, gla_b_gate, gla_norm_gain, gdn_conv_w, gdn_a_log, gdn_dt_bias, gdn_norm_gain, hgrn_lb_raw, hgrn_norm_gain, g_mem, w_mem_kv, mem_q_gain, mem_k_gain, w_branch, w_out, g_ffn, ffn_w_gate, ffn_w_up, ffn_w_down, moe_w_router, moe_b_router, moe_w_gate, moe_w_up, moe_w_down):
    B, S, D = x.shape
    n_tok = B * S
    lb_w = jax.nn.softmax(hgrn_lb_raw.astype(F32), axis=0)
    hgrn_lb = jnp.cumsum(lb_w, axis=0) - lb_w[0:1]
    x2 = x.reshape(n_tok, D)
    mem2 = mem.reshape(B * mem.shape[1], D)
    for layer in range(DEPTH):
        wb, wf = _split_w_in(w_in[layer])
        pb = rms_matmul(x2, g_mix[layer], wb, tm=1024, tn=512, out_dtype=BF16)
        pf = rms_matmul(x2, g_mix[layer], wf, tm=1024, tn=PF_WIDTH // 3, out_dtype=F32)
        kv = rms_matmul(mem2, g_mem[layer], w_mem_kv[layer].astype(BF16), tm=mem2.shape[0], tn=512,
                        out_dtype=BF16)
        branches = [
            neighbourhood_attention(pb, B, na_q_gain[layer], na_k_gain[layer], na_rel_bias[layer]),
            gla_branch(pb, pf, B, gla_w_gate_up[layer], gla_b_gate[layer], gla_norm_gain[layer]),
            gated_deltanet_branch(pb, pf, B, gdn_conv_w[layer], gdn_a_log[layer], gdn_dt_bias[layer],
                                  gdn_norm_gain[layer]),
            hgrn2_branch(pb, pf, B, hgrn_lb[layer], hgrn_norm_gain[layer]),
            memory_cross_attention(pb, kv, B, mem_q_gain[layer], mem_k_gain[layer]),
        ]
        merged = merge_branches(branches, pb, PB_COL["gates"], w_branch[layer].astype(BF16), tm=1024, tn=512)
        x2 = matmul_residual(merged, w_out[layer].astype(BF16), x2, tm=1024, tn=512)

        j = layer // 2
        if layer % 2 == 0:
            act = rms_swiglu_up(x2, g_ffn[layer], ffn_w_gate[j].astype(BF16), ffn_w_up[j].astype(BF16),
                                tm=1024, tn=512)
            x2 = matmul_residual(act, ffn_w_down[j].astype(BF16), x2, tm=512, tn=512)
        else:
            x2 = moe_layer(x2, g_ffn[layer], moe_w_router[j], moe_b_router[j], moe_w_gate[j], moe_w_up[j],
                           moe_w_down[j])
    return x2.reshape(B, S, D)
```

```python
import functools

import jax
import jax.numpy as jnp
import numpy as np
from jax import lax
from jax.experimental import pallas as pl
from jax.experimental.pallas import tpu as pltpu

F32 = jnp.float32
BF16 = jnp.bfloat16

D_MODEL = 2048
DEPTH = 2
RMS_EPS = 1e-6
MASK_VALUE = -1e30
LB_FLOOR = 1e-30
GRID_W = 64

NA_HEADS = 8
NA_HEAD_DIM = 64
NA_WIDTH = 512
NA_WIN_ROWS = 8
NA_WIN_COLS = 16

GLA_HEADS = 4
GLA_HEAD_K = 64
GLA_HEAD_V = 128
GLA_KEY_WIDTH = 256
GLA_VAL_WIDTH = 512
GLA_GATE_RANK = 16
GLA_GATE_NORMALIZER = 16.0

GDN_HEADS = 4
GDN_HEAD_K = 128
GDN_HEAD_V = 128
GDN_KEY_WIDTH = 512
GDN_VAL_WIDTH = 512
GDN_CHUNK = 64

HGRN_HEADS = 4
HGRN_HEAD_K = 128
HGRN_HEAD_V = 128
HGRN_KEY_WIDTH = 512
HGRN_VAL_WIDTH = 512

LIN_CHUNK = 32

MEM_HEADS = 4
MEM_HEAD_DIM = 128
MEM_WIDTH = 512

N_BRANCH = 5
BRANCH_WIDTH = 512
N_EXPERTS = 8
MOE_TOP_K = 2

IN_WIDTHS = (
    NA_WIDTH, NA_WIDTH, NA_WIDTH,
    GLA_KEY_WIDTH, GLA_KEY_WIDTH, GLA_VAL_WIDTH,
    2 * GLA_GATE_RANK, GLA_VAL_WIDTH,
    2 * GDN_KEY_WIDTH + GDN_VAL_WIDTH,
    2 * GDN_HEADS, 2 * GDN_HEADS, GDN_VAL_WIDTH,
    HGRN_KEY_WIDTH, 2 * HGRN_KEY_WIDTH, HGRN_VAL_WIDTH, HGRN_VAL_WIDTH,
    MEM_WIDTH,
    N_BRANCH * D_MODEL,
)
P_IN = sum(IN_WIDTHS)

V7X_VMEM_BYTES = 64 * 1024 * 1024
VMEM_LIMIT_BYTES = V7X_VMEM_BYTES - 8 * 1024 * 1024
LANES = 128


def _params(*semantics):
    return pltpu.CompilerParams(dimension_semantics=semantics, vmem_limit_bytes=VMEM_LIMIT_BYTES)


def _rms_norm_rows(x, gain):
    ms = jnp.mean(x * x, axis=-1, keepdims=True)
    return x * lax.rsqrt(ms + RMS_EPS) * gain


def _rms_matmul_kernel(x_ref, g_ref, w_ref, o_ref, h_ref):
    @pl.when(pl.program_id(1) == 0)
    def _():
        h_ref[...] = _rms_norm_rows(x_ref[...], g_ref[...]).astype(BF16)

    o_ref[...] = jnp.dot(h_ref[...], w_ref[...], preferred_element_type=F32).astype(o_ref.dtype)


def rms_matmul(x, gain, w, *, tm, tn, out_dtype=F32):
    m, k = x.shape
    n = w.shape[1]
    return pl.pallas_call(
        _rms_matmul_kernel,
        out_shape=jax.ShapeDtypeStruct((m, n), out_dtype),
        grid=(m // tm, n // tn),
        in_specs=[
            pl.BlockSpec((tm, k), lambda i, j: (i, 0)),
            pl.BlockSpec((1, k), lambda i, j: (0, 0)),
            pl.BlockSpec((k, tn), lambda i, j: (0, j)),
        ],
        out_specs=pl.BlockSpec((tm, tn), lambda i, j: (i, j)),
        scratch_shapes=[pltpu.VMEM((tm, k), BF16)],
        compiler_params=_params("parallel", "arbitrary"),
        name="rms_matmul",
    )(x, gain.reshape(1, k), w)


def _rms_swiglu_kernel(x_ref, g_ref, wg_ref, wu_ref, o_ref, h_ref):
    @pl.when(pl.program_id(1) == 0)
    def _():
        h_ref[...] = _rms_norm_rows(x_ref[...], g_ref[...]).astype(BF16)

    h = h_ref[...]
    a = jnp.dot(h, wg_ref[...], preferred_element_type=F32)
    b = jnp.dot(h, wu_ref[...], preferred_element_type=F32)
    o_ref[...] = (a * jax.nn.sigmoid(a) * b).astype(o_ref.dtype)


def rms_swiglu_up(x, gain, wg, wu, *, tm, tn):
    m, k = x.shape
    n = wg.shape[1]
    return pl.pallas_call(
        _rms_swiglu_kernel,
        out_shape=jax.ShapeDtypeStruct((m, n), BF16),
        grid=(m // tm, n // tn),
        in_specs=[
            pl.BlockSpec((tm, k), lambda i, j: (i, 0)),
            pl.BlockSpec((1, k), lambda i, j: (0, 0)),
            pl.BlockSpec((k, tn), lambda i, j: (0, j)),
            pl.BlockSpec((k, tn), lambda i, j: (0, j)),
        ],
        out_specs=pl.BlockSpec((tm, tn), lambda i, j: (i, j)),
        scratch_shapes=[pltpu.VMEM((tm, k), BF16)],
        compiler_params=_params("parallel", "arbitrary"),
        name="rms_swiglu_up",
    )(x, gain.reshape(1, k), wg, wu)


def _matmul_residual_kernel(a_ref, w_ref, r_ref, o_ref):
    o_ref[...] = r_ref[...] + jnp.dot(a_ref[...], w_ref[...], preferred_element_type=F32)


def matmul_residual(a, w, res, *, tm, tn):
    m, k = a.shape
    n = w.shape[1]
    return pl.pallas_call(
        _matmul_residual_kernel,
        out_shape=jax.ShapeDtypeStruct((m, n), F32),
        grid=(m // tm, n // tn),
        in_specs=[
            pl.BlockSpec((tm, k), lambda i, j: (i, 0)),
            pl.BlockSpec((k, tn), lambda i, j: (0, j)),
            pl.BlockSpec((tm, tn), lambda i, j: (i, j)),
        ],
        out_specs=pl.BlockSpec((tm, tn), lambda i, j: (i, j)),
        compiler_params=_params("parallel", "arbitrary"),
        name="matmul_residual",
    )(a, w, res)


class RawBranch:
    def __init__(self, o_fwd, o_bwd, og_name, gain, silu_gate):
        self.o_fwd, self.o_bwd, self.og_name, self.gain, self.silu_gate = o_fwd, o_bwd, og_name, gain, silu_gate


def _merge_kernel(*refs, raw):
    pos = 0
    br = []
    for kind in raw:
        width = 1 if kind is None else 4
        br.append(refs[pos:pos + width])
        pos += width
    gl_refs = refs[pos:pos + N_BRANCH]
    wb_ref, o_ref, fin_ref = refs[pos + N_BRANCH:pos + N_BRANCH + 3]
    raw_slot = {n: s for s, n in enumerate(n for n, kind in enumerate(raw) if kind is not None)}

    @pl.when(pl.program_id(1) == 0)
    def _():
        for n, slot in raw_slot.items():
            of_ref, ob_ref, og_ref, gain_ref = br[n]
            for h in range(BRANCH_WIDTH // LANES):
                cols = slice(h * LANES, (h + 1) * LANES)
                y = _rms_norm_rows(of_ref[:, cols] + ob_ref[:, cols], gain_ref[...])
                g = og_ref[:, cols].astype(F32)
                gate = jax.nn.sigmoid(g)
                if raw[n]:
                    gate = g * gate
                fin_ref[slot, :, cols] = (y * gate).astype(fin_ref.dtype)

    acc = None
    for n in range(N_BRANCH):
        b = br[n][0][...] if raw[n] is None else fin_ref[raw_slot[n]]
        y = jnp.dot(b, wb_ref[n], preferred_element_type=F32)
        t = jax.nn.sigmoid(gl_refs[n][...].astype(F32)) * y
        acc = t if acc is None else acc + t
    o_ref[...] = acc.astype(o_ref.dtype)


def merge_branches(branches, pb, w_branch, *, tm, tn):
    m = pb.shape[0]
    d = D_MODEL
    tiles_per_branch = d // tn
    tile0 = PB_COL["gates"] // tn
    row_block = pl.BlockSpec((tm, BRANCH_WIDTH), lambda i, j: (i, 0))
    in_specs, operands, raw = [], [], []
    for b in branches:
        if isinstance(b, RawBranch):
            og_col = PB_COL[b.og_name] // BRANCH_WIDTH
            in_specs += [row_block, row_block,
                         pl.BlockSpec((tm, BRANCH_WIDTH), functools.partial(lambda i, j, c: (i, c), c=og_col)),
                         pl.BlockSpec((1, LANES), lambda i, j: (0, 0))]
            operands += [b.o_fwd, b.o_bwd, pb, b.gain.astype(F32).reshape(1, LANES)]
            raw.append(b.silu_gate)
        else:
            in_specs.append(row_block)
            operands.append(b)
            raw.append(None)
    in_specs += [
        pl.BlockSpec((tm, tn), functools.partial(lambda i, j, n: (i, tile0 + n * tiles_per_branch + j), n=n))
        for n in range(N_BRANCH)
    ]
    in_specs += [pl.BlockSpec((N_BRANCH, BRANCH_WIDTH, tn), lambda i, j: (0, 0, j))]
    n_raw = sum(kind is not None for kind in raw)
    return pl.pallas_call(
        functools.partial(_merge_kernel, raw=tuple(raw)),
        out_shape=jax.ShapeDtypeStruct((m, d), BF16),
        grid=(m // tm, d // tn),
        in_specs=in_specs,
        out_specs=pl.BlockSpec((tm, tn), lambda i, j: (i, j)),
        scratch_shapes=[pltpu.VMEM((max(n_raw, 1), tm, BRANCH_WIDTH), BF16)],
        compiler_params=_params("parallel", "arbitrary"),
        name="merge_branches",
    )(*operands, *([pb] * N_BRANCH), w_branch)


def _router_kernel(x_ref, g_ref, w_ref, b_ref, o_ref, h_ref, cnt_ref, run_ref, *, n_experts):
    @pl.when(pl.program_id(0) == 0)
    def _():
        run_ref[...] = jnp.zeros_like(run_ref)

    h = _rms_norm_rows(x_ref[...], g_ref[...])
    h_ref[...] = h.astype(h_ref.dtype)
    logits = _dot_f32(h, w_ref[...]) + b_ref[...]
    lane = lax.broadcasted_iota(jnp.int32, logits.shape, 1).astype(F32)
    neg = -jnp.inf
    lm = jnp.where(lane < n_experts, logits, neg)
    m1 = jnp.max(lm, axis=-1, keepdims=True)
    i1 = jnp.min(jnp.where(lm == m1, lane, float(LANES)), axis=-1, keepdims=True)
    lm2 = jnp.where(lane == i1, neg, lm)
    m2 = jnp.max(lm2, axis=-1, keepdims=True)
    i2 = jnp.min(jnp.where(lm2 == m2, lane, float(LANES)), axis=-1, keepdims=True)
    t = jnp.exp(m2 - m1)
    den = 1.0 + t

    tm = logits.shape[0]
    before = (lax.broadcasted_iota(jnp.int32, (tm, tm), 1)
              < lax.broadcasted_iota(jnp.int32, (tm, tm), 0))
    before = jnp.where(before, 1.0, 0.0).astype(BF16)
    pick1 = lane == i1
    pick2 = lane == i2
    oh1 = jnp.where(pick1, 1.0, 0.0)
    oh2 = jnp.where(pick2, 1.0, 0.0)
    pre1 = jnp.dot(before, oh1.astype(BF16), preferred_element_type=F32)
    pre2 = jnp.dot(before, oh2.astype(BF16), preferred_element_type=F32)
    tot1 = jnp.sum(oh1, axis=0, keepdims=True)
    tot2 = jnp.sum(oh2, axis=0, keepdims=True)
    run = run_ref[...]
    rank1 = jnp.sum(jnp.where(pick1, pre1 + run, 0.0), axis=-1, keepdims=True)
    rank2 = jnp.sum(jnp.where(pick2, pre2 + (run + tot1), 0.0), axis=-1, keepdims=True)
    run = run + tot1 + tot2
    run_ref[...] = run
    cnt_ref[...] = jnp.broadcast_to(run, cnt_ref.shape)

    out = jnp.where(lane == 0, 1.0 / den, jnp.where(lane == 1, t / den, jnp.where(lane == 2, i1, i2)))
    out = jnp.where(lane == 4, rank1, jnp.where(lane == 5, rank2, out))
    o_ref[...] = jnp.where(lane < 6, out, 0.0)


def router_top2(x, gain, w_router, b_router, *, tm=512):
    m, k = x.shape
    e = w_router.shape[1]
    w_pad = jnp.zeros((k, LANES), F32).at[:, :e].set(w_router.astype(F32))
    b_pad = jnp.zeros((1, LANES), F32).at[0, :e].set(b_router.astype(F32))
    route, h, cnt = pl.pallas_call(
        functools.partial(_router_kernel, n_experts=e),
        out_shape=(jax.ShapeDtypeStruct((m, LANES), F32), jax.ShapeDtypeStruct((m, k), BF16),
                   jax.ShapeDtypeStruct((8, LANES), F32)),
        grid=(m // tm,),
        in_specs=[
            pl.BlockSpec((tm, k), lambda i: (i, 0)),
            pl.BlockSpec((1, k), lambda i: (0, 0)),
            pl.BlockSpec((k, LANES), lambda i: (0, 0)),
            pl.BlockSpec((1, LANES), lambda i: (0, 0)),
        ],
        out_specs=(pl.BlockSpec((tm, LANES), lambda i: (i, 0)), pl.BlockSpec((tm, k), lambda i: (i, 0)),
                   pl.BlockSpec((8, LANES), lambda i: (0, 0))),
        scratch_shapes=[pltpu.VMEM((1, LANES), F32)],
        compiler_params=_params("arbitrary"),
        name="router_top2",
    )(x, gain.reshape(1, k), w_pad, b_pad)
    return route, h, cnt[0, :e].astype(jnp.int32)


MOE_TILE = 1024
MOE_SUB = 256
MOE_FF_TILE = 256


def _moe_kernel(tile_e_ref, tile_rows_ref, n_used_ref, x_ref, wg_ref, wu_ref, wd_ref, o_ref,
                acc_ref, wg_s, wu_s, wd_s):
    i = pl.program_id(0)
    j = pl.program_id(1)
    last = pl.num_programs(1) - 1
    valid = tile_rows_ref[i]
    n_sub = (valid + (MOE_SUB - 1)) // MOE_SUB

    @pl.when(valid > 0)
    def _():
        wg_s[...] = wg_ref[0].astype(BF16)
        wu_s[...] = wu_ref[0].astype(BF16)
        wd_s[...] = wd_ref[0].astype(BF16)

    for k in range(1, MOE_TILE // MOE_SUB + 1):
        rows = slice(0, k * MOE_SUB)

        @pl.when(n_sub == k)
        def _(rows=rows):
            x = x_ref[rows, :]
            a = jnp.dot(x, wg_s[...], preferred_element_type=F32)
            b = jnp.dot(x, wu_s[...], preferred_element_type=F32)
            act = (a * jax.nn.sigmoid(a) * b).astype(BF16)
            part = jnp.dot(act, wd_s[...], preferred_element_type=F32)

            @pl.when(j == 0)
            def _():
                acc_ref[rows, :] = part

            @pl.when(j > 0)
            def _():
                acc_ref[rows, :] += part

    for s in range(0, MOE_TILE, MOE_SUB):
        rows = slice(s, s + MOE_SUB)
        filled = s < valid

        @pl.when(jnp.logical_and(filled, j == last))
        def _(rows=rows):
            o_ref[rows, :] = acc_ref[rows, :].astype(o_ref.dtype)

        @pl.when(jnp.logical_and(jnp.logical_not(filled), j == last))
        def _(rows=rows):
            o_ref[rows, :] = jnp.zeros((MOE_SUB, o_ref.shape[1]), o_ref.dtype)


def moe_experts(xb, tile_e, tile_rows, n_used, wg, wu, wd):
    rows, d = xb.shape
    ff = wg.shape[2]
    tm, tf = MOE_TILE, MOE_FF_TILE
    n_tiles = rows // tm
    last_j = ff // tf - 1

    def x_map(i, j, te, tr, nu):
        return (jnp.minimum(i, nu[0] - 1), 0)

    def up_map(i, j, te, tr, nu):
        return (te[i], 0, jnp.where(i < nu[0], j, last_j))

    def down_map(i, j, te, tr, nu):
        return (te[i], jnp.where(i < nu[0], j, last_j), 0)

    grid_spec = pltpu.PrefetchScalarGridSpec(
        num_scalar_prefetch=3,
        grid=(n_tiles, ff // tf),
        in_specs=[
            pl.BlockSpec((tm, d), x_map),
            pl.BlockSpec((1, d, tf), up_map),
            pl.BlockSpec((1, d, tf), up_map),
            pl.BlockSpec((1, tf, d), down_map),
        ],
        out_specs=pl.BlockSpec((tm, d), lambda i, j, te, tr, nu: (i, 0)),
        scratch_shapes=[pltpu.VMEM((tm, d), F32), pltpu.VMEM((d, tf), BF16), pltpu.VMEM((d, tf), BF16),
                        pltpu.VMEM((tf, d), BF16)],
    )
    return pl.pallas_call(
        _moe_kernel,
        out_shape=jax.ShapeDtypeStruct((rows, d), BF16),
        grid_spec=grid_spec,
        compiler_params=_params("arbitrary", "arbitrary"),
        name="moe_experts",
    )(tile_e, tile_rows, n_used, xb, wg, wu, wd)


def _moe_combine_kernel(x_ref, y_ref, r_ref, o_ref):
    d = x_ref.shape[1]
    w = r_ref[...]
    o_ref[...] = (x_ref[...] + w[:, 0:1] * y_ref[:, :d].astype(F32) + w[:, 1:2] * y_ref[:, d:].astype(F32))


def moe_combine(x2d, y2, route, *, tm=512):
    n, d = x2d.shape
    return pl.pallas_call(
        _moe_combine_kernel,
        out_shape=jax.ShapeDtypeStruct((n, d), F32),
        grid=(n // tm,),
        in_specs=[pl.BlockSpec((tm, d), lambda i: (i, 0)), pl.BlockSpec((tm, 2 * d), lambda i: (i, 0)),
                  pl.BlockSpec((tm, LANES), lambda i: (i, 0))],
        out_specs=pl.BlockSpec((tm, d), lambda i: (i, 0)),
        compiler_params=_params("parallel"),
        name="moe_combine",
    )(x2d, y2, route)


def moe_layer(x2d, gain, w_router, b_router, wg, wu, wd):
    n, d = x2d.shape
    e = N_EXPERTS
    route, h, counts = router_top2(x2d, gain, w_router, b_router)
    nk = n * MOE_TOP_K
    n_tiles = -(-nk // MOE_TILE) + e
    flat_e = route[:, 2:2 + MOE_TOP_K].astype(jnp.int32).reshape(nk)
    rank = route[:, 4:4 + MOE_TOP_K].astype(jnp.int32).reshape(nk)
    flat_tok = jnp.repeat(jnp.arange(n, dtype=jnp.int32), MOE_TOP_K)
    padded = (counts + MOE_TILE - 1) // MOE_TILE * MOE_TILE
    pad_end = jnp.cumsum(padded)
    pad_start = pad_end - padded
    slot = (pad_start[flat_e] + rank).astype(jnp.int32)
    slot_tok = jnp.zeros((n_tiles * MOE_TILE,), jnp.int32).at[slot].set(flat_tok)
    tile_start = jnp.arange(n_tiles, dtype=jnp.int32) * MOE_TILE
    tile_e = jnp.minimum(jnp.searchsorted(pad_end, tile_start, side="right"), e - 1).astype(jnp.int32)
    tile_rows = jnp.clip(pad_start[tile_e] + counts[tile_e] - tile_start, 0, MOE_TILE).astype(jnp.int32)
    tile_rows = jnp.where(tile_start < pad_end[-1], tile_rows, 0)
    n_used = (pad_end[-1] // MOE_TILE).astype(jnp.int32).reshape(1)
    tile_e = jnp.where(tile_start < pad_end[-1], tile_e, tile_e[jnp.maximum(n_used[0] - 1, 0)])

    xb = h[slot_tok]
    yb = moe_experts(xb, tile_e, tile_rows, n_used, wg, wu, wd)
    return moe_combine(x2d, yb[slot].reshape(n, MOE_TOP_K * d), route)


def _rms_kernel(x_ref, g_ref, o_ref):
    o_ref[...] = _rms_norm_rows(x_ref[...], g_ref[...]).astype(o_ref.dtype)


def rms_only(x, gain, *, tm=1024):
    m, k = x.shape
    return pl.pallas_call(
        _rms_kernel,
        out_shape=jax.ShapeDtypeStruct((m, k), BF16),
        grid=(m // tm,),
        in_specs=[pl.BlockSpec((tm, k), lambda i: (i, 0)), pl.BlockSpec((1, k), lambda i: (0, 0))],
        out_specs=pl.BlockSpec((tm, k), lambda i: (i, 0)),
        compiler_params=_params("parallel"),
        name="rms_only",
    )(x, gain.reshape(1, k))


_SRC = dict(zip(
    ("na_q", "na_k", "na_v", "gla_q", "gla_k", "gla_v", "gla_lr", "gla_og", "gdn_qkv", "gdn_a", "gdn_b",
     "gdn_og", "hg_q", "hg_f", "hg_i", "hg_og", "mem_q", "gates"),
    zip(np.cumsum((0,) + IN_WIDTHS[:-1]).tolist(), IN_WIDTHS)))
_PB_ORDER = ("na_q", "na_k", "na_v", "gla_q", "gla_k", "gla_v", "gla_og", "gdn_qkv", "gdn_og", "hg_q", "hg_i",
             "hg_og", "mem_q", "gates")
_PF_ORDER = ("hg_f", "gla_lr", "gdn_a", "gdn_b")
PB_COL = {}
_c = 0
for _name in _PB_ORDER:
    PB_COL[_name] = _c
    _c += _SRC[_name][1]
PB_WIDTH = _c
PF_COL = {}
_c = 0
for _name in _PF_ORDER:
    PF_COL[_name] = _c
    _c += _SRC[_name][1]
PF_WIDTH = -(-_c // LANES) * LANES
PF_SMALL_COL = PF_COL["gla_lr"]
GDN_A_LANE = PF_COL["gdn_a"] - PF_SMALL_COL
GDN_B_LANE = PF_COL["gdn_b"] - PF_SMALL_COL


def _split_w_in(w):
    wb = jnp.concatenate([w[:, _SRC[n][0]:_SRC[n][0] + _SRC[n][1]] for n in _PB_ORDER], axis=1)
    wf = jnp.concatenate([w[:, _SRC[n][0]:_SRC[n][0] + _SRC[n][1]] for n in _PF_ORDER], axis=1)
    wf = jnp.pad(wf, ((0, 0), (0, PF_WIDTH - wf.shape[1])))
    return wb.astype(BF16), wf.astype(BF16)


def _segment_rms(x, gain, seg_ones, seg_width):
    sq = x * x
    hi = sq.astype(BF16)
    lo = (sq - hi.astype(F32)).astype(BF16)
    ss = (jnp.dot(hi, seg_ones, preferred_element_type=F32)
          + jnp.dot(lo, seg_ones, preferred_element_type=F32))
    return x * lax.rsqrt(ss * (1.0 / seg_width) + RMS_EPS) * gain


NA_ROWS_PER_STEP = 8
NA_BAND = NA_WIN_ROWS * GRID_W


def _na_bias_table(rel_bias):
    c = np.arange(GRID_W)
    dc = np.clip(c[None, :] - c[:, None], 1 - NA_WIN_COLS, NA_WIN_COLS - 1) + (NA_WIN_COLS - 1)
    col_start = np.clip(c - NA_WIN_COLS // 2, 0, GRID_W - NA_WIN_COLS)
    col_in = (c[None, :] >= col_start[:, None]) & (c[None, :] < col_start[:, None] + NA_WIN_COLS)
    cfg = np.arange(NA_WIN_ROWS)[:, None]
    dr = np.arange(NA_WIN_ROWS)[None, :] - cfg + (NA_WIN_ROWS - 1)
    t = rel_bias.astype(F32)[:, dr][:, :, :, dc]
    t = jnp.where(col_in[None, None, None], t, MASK_VALUE)
    return t.transpose(1, 0, 3, 2, 4).reshape(NA_WIN_ROWS, NA_HEADS, GRID_W, NA_BAND)


def _na_kernel(q_ref, k_ref, v_ref, qg_ref, kg_ref, seg_ref, bias_ref, o_ref, kn_ref):
    step = pl.program_id(1)
    rows_total = k_ref.shape[0] // GRID_W
    seg = seg_ref[...]

    @pl.when(step == 0)
    def _():
        def norm_keys(t, carry):
            rows = pl.ds(pl.multiple_of(t * 256, 256), 256)
            kn_ref[rows, :] = _segment_rms(k_ref[rows, :].astype(F32), kg_ref[...], seg, NA_HEAD_DIM).astype(BF16)
            return carry
        lax.fori_loop(0, k_ref.shape[0] // 256, norm_keys, 0)

    lane = lax.broadcasted_iota(jnp.int32, (1, LANES), 1)
    low_half = lane < NA_HEAD_DIM

    def one_row(rr, carry):
        r = step * NA_ROWS_PER_STEP + rr
        row_start = jnp.clip(r - NA_WIN_ROWS // 2, 0, rows_total - NA_WIN_ROWS)
        cfg = r - row_start
        qrows = pl.ds(pl.multiple_of(rr * GRID_W, GRID_W), GRID_W)
        band = pl.ds(pl.multiple_of(row_start * GRID_W, GRID_W), NA_BAND)
        qn = (_segment_rms(q_ref[qrows, :].astype(F32), qg_ref[...], seg, NA_HEAD_DIM)
              * (NA_HEAD_DIM ** -0.5)).astype(BF16)
        heads = [(pair, half) for pair in range(NA_HEADS // 2) for half in range(2)]
        scores = []
        for pair, half in heads:
            cols = slice(pair * LANES, (pair + 1) * LANES)
            qp = qn[:, cols]
            keep = low_half if half == 0 else jnp.logical_not(low_half)
            qm = jnp.where(keep, qp, jnp.zeros_like(qp))
            scores.append(lax.dot_general(qm, kn_ref[band, cols], (((1,), (1,)), ((), ())),
                                          preferred_element_type=F32))
        exps, sums = [], []
        for s, (pair, half) in zip(scores, heads):
            s = s + bias_ref[cfg, 2 * pair + half]
            e = jnp.exp(s - jnp.max(s, axis=-1, keepdims=True))
            sums.append(jnp.sum(e, axis=-1, keepdims=True))
            exps.append(e.astype(BF16))
        outs = [jnp.dot(e, v_ref[band, slice(pair * LANES, (pair + 1) * LANES)], preferred_element_type=F32) / l
                for e, l, (pair, half) in zip(exps, sums, heads)]
        for pair in range(NA_HEADS // 2):
            cols = slice(pair * LANES, (pair + 1) * LANES)
            o_ref[qrows, cols] = jnp.where(low_half, outs[2 * pair], outs[2 * pair + 1]).astype(o_ref.dtype)
        return carry

    lax.fori_loop(0, NA_ROWS_PER_STEP, one_row, 0)


def neighbourhood_attention(pb, batch, q_gain, k_gain, rel_bias):
    m = pb.shape[0]
    s = m // batch
    tq = NA_ROWS_PER_STEP * GRID_W
    steps = s // tq
    qg = jnp.tile(q_gain.astype(F32), NA_HEADS).reshape(1, NA_WIDTH)
    kg = jnp.tile(k_gain.astype(F32), NA_HEADS).reshape(1, NA_WIDTH)
    seg = jnp.asarray(np.kron(np.eye(NA_HEADS), np.ones((NA_HEAD_DIM, NA_HEAD_DIM))), BF16)
    bias = _na_bias_table(rel_bias)
    cq, ck, cv = (PB_COL[n] // NA_WIDTH for n in ("na_q", "na_k", "na_v"))
    return pl.pallas_call(
        _na_kernel,
        out_shape=jax.ShapeDtypeStruct((m, NA_WIDTH), BF16),
        grid=(batch, steps),
        in_specs=[
            pl.BlockSpec((tq, NA_WIDTH), lambda b, t: (b * steps + t, cq)),
            pl.BlockSpec((s, NA_WIDTH), lambda b, t: (b, ck)),
            pl.BlockSpec((s, NA_WIDTH), lambda b, t: (b, cv)),
            pl.BlockSpec((1, NA_WIDTH), lambda b, t: (0, 0)),
            pl.BlockSpec((1, NA_WIDTH), lambda b, t: (0, 0)),
            pl.BlockSpec((NA_WIDTH, NA_WIDTH), lambda b, t: (0, 0)),
            pl.BlockSpec((NA_WIN_ROWS, NA_HEADS, GRID_W, NA_BAND), lambda b, t: (0, 0, 0, 0)),
        ],
        out_specs=pl.BlockSpec((tq, NA_WIDTH), lambda b, t: (b * steps + t, 0)),
        scratch_shapes=[pltpu.VMEM((s, NA_WIDTH), BF16)],
        compiler_params=_params("parallel", "arbitrary"),
        name="neighbourhood_attention",
    )(pb, pb, pb, qg, kg, seg, bias)


def _mem_attn_kernel(q_ref, kv_ref, qg_ref, kg_ref, o_ref, kn_ref):
    @pl.when(pl.program_id(1) == 0)
    def _():
        for h in range(MEM_HEADS):
            cols = slice(h * MEM_HEAD_DIM, (h + 1) * MEM_HEAD_DIM)
            kn_ref[:, cols] = _rms_norm_rows(kv_ref[:, cols].astype(F32), kg_ref[...]).astype(BF16)

    head_cols = [slice(h * MEM_HEAD_DIM, (h + 1) * MEM_HEAD_DIM) for h in range(MEM_HEADS)]
    qns = [_rms_norm_rows(q_ref[:, cols].astype(F32), qg_ref[...]).astype(BF16) for cols in head_cols]
    scores = [lax.dot_general(qn, kn_ref[:, cols], (((1,), (1,)), ((), ())), preferred_element_type=F32)
              for qn, cols in zip(qns, head_cols)]
    exps, sums = [], []
    for s in scores:
        s = s * (MEM_HEAD_DIM ** -0.5)
        e = jnp.exp(s - jnp.max(s, axis=-1, keepdims=True))
        sums.append(jnp.sum(e, axis=-1, keepdims=True))
        exps.append(e.astype(BF16))
    outs = [jnp.dot(e, kv_ref[:, MEM_WIDTH + cols.start:MEM_WIDTH + cols.stop], preferred_element_type=F32)
            for e, cols in zip(exps, head_cols)]
    for o, l, cols in zip(outs, sums, head_cols):
        o_ref[:, cols] = (o / l).astype(o_ref.dtype)


def memory_cross_attention(pb, kv, batch, q_gain, k_gain, *, tq=512):
    m = pb.shape[0]
    steps = m // batch // tq
    n_mem = kv.shape[0] // batch
    cq = PB_COL["mem_q"] // MEM_WIDTH
    return pl.pallas_call(
        _mem_attn_kernel,
        out_shape=jax.ShapeDtypeStruct((m, MEM_WIDTH), BF16),
        grid=(batch, steps),
        in_specs=[
            pl.BlockSpec((tq, MEM_WIDTH), lambda b, t: (b * steps + t, cq)),
            pl.BlockSpec((n_mem, 2 * MEM_WIDTH), lambda b, t: (b, 0)),
            pl.BlockSpec((1, MEM_HEAD_DIM), lambda b, t: (0, 0)),
            pl.BlockSpec((1, MEM_HEAD_DIM), lambda b, t: (0, 0)),
        ],
        out_specs=pl.BlockSpec((tq, MEM_WIDTH), lambda b, t: (b * steps + t, 0)),
        scratch_shapes=[pltpu.VMEM((n_mem, MEM_WIDTH), BF16)],
        compiler_params=_params("parallel", "arbitrary"),
        name="memory_cross_attention",
    )(pb, kv, q_gain.astype(F32).reshape(1, MEM_HEAD_DIM), k_gain.astype(F32).reshape(1, MEM_HEAD_DIM))


LIN_BLOCK = 512
HEAD_V = 128


def _log1p_exp_neg(t):
    return jnp.log(1.0 + jnp.exp(-t))


def _log_sigmoid(x):
    return jnp.minimum(x, 0.0) - _log1p_exp_neg(jnp.abs(x))


def _logaddexp(a, b):
    return jnp.maximum(a, b) + _log1p_exp_neg(jnp.abs(a - b))


def _split_bf16(x, terms):
    parts = []
    for _ in range(terms):
        p = x.astype(BF16)
        parts.append(p)
        x = x - p.astype(F32)
    return parts


def _dot_f32(a, b):
    a_hi, a_lo = _split_bf16(a, 2)
    b_hi, b_lo = _split_bf16(b, 2)
    return (jnp.dot(a_hi, b_hi, preferred_element_type=F32)
            + (jnp.dot(a_hi, b_lo, preferred_element_type=F32) + jnp.dot(a_lo, b_hi, preferred_element_type=F32)))


def _cumsum_rows(mask, x):
    m = jnp.where(mask, 1.0, 0.0).astype(BF16)
    hi, mid, lo = _split_bf16(x, 3)
    return (jnp.dot(m, hi, preferred_element_type=F32)
            + (jnp.dot(m, mid, preferred_element_type=F32) + jnp.dot(m, lo, preferred_element_type=F32)))


def _lin_chunk(qc, kc, vc, lg, s_ref, *, reverse, heads):
    c, w = qc.shape
    dk = w // heads
    ii = lax.broadcasted_iota(jnp.int32, (c, c), 0)
    jj = lax.broadcasted_iota(jnp.int32, (c, c), 1)
    causal = (jj >= ii) if reverse else (jj <= ii)
    b = _cumsum_rows(causal, lg)
    mid = c - 1 - c // 2 if reverse else c // 2
    end = 0 if reverse else c - 1
    b_mid = b[mid:mid + 1, :]
    b_end = b[end:end + 1, :]
    qe = qc * jnp.exp(b - b_mid)
    ke = (kc * jnp.exp(b_mid - b)).astype(BF16)
    q_dec = qc * jnp.exp(b)
    k_dec = kc * jnp.exp(b_end - b)
    g_tot = jnp.exp(b_end)

    lane = lax.broadcasted_iota(jnp.int32, (1, w), 1)
    head_mask = [((lane >= h * dk) & (lane < (h + 1) * dk)).astype(F32) for h in range(heads)]
    q4 = jnp.concatenate([qe * head_mask[h] for h in range(heads)], axis=0).astype(BF16)
    q4d = jnp.concatenate([q_dec * head_mask[h] for h in range(heads)], axis=0).astype(BF16)
    k4 = jnp.concatenate([k_dec * head_mask[h] for h in range(heads)], axis=0).astype(BF16)
    v4 = jnp.concatenate([vc[:, h * HEAD_V:(h + 1) * HEAD_V] for h in range(heads)], axis=0)

    scores = lax.dot_general(q4, ke, (((1,), (1,)), ((), ())), preferred_element_type=F32)
    ri = lax.broadcasted_iota(jnp.int32, (heads * c, c), 0) % c
    cj = lax.broadcasted_iota(jnp.int32, (heads * c, c), 1)
    keep = (cj >= ri) if reverse else (cj <= ri)
    scores = jnp.where(keep, scores, 0.0).astype(BF16)
    state = s_ref[...]
    o_inter = lax.dot_general(q4d, state.astype(BF16), (((1,), (1,)), ((), ())),
                              preferred_element_type=F32)
    outs = []
    for h in range(heads):
        rows = slice(h * c, (h + 1) * c)
        o_intra = jnp.dot(scores[rows, :], vc[:, h * HEAD_V:(h + 1) * HEAD_V], preferred_element_type=F32)
        outs.append(o_intra + o_inter[rows, :])
    s_ref[...] = state * g_tot + lax.dot_general(v4, k4, (((0,), (0,)), ((), ())), preferred_element_type=F32)
    return jnp.concatenate(outs, axis=1)


def _gla_inputs(refs, rows, direction, params):
    q_ref, k_ref, v_ref, g_ref = refs
    wpad_ref, bias_ref = params
    qc = q_ref[rows, :].astype(F32) * (GLA_HEAD_K ** -0.5)
    kc = k_ref[rows, :].astype(F32)
    gk = _dot_f32(g_ref[rows, :], wpad_ref[direction]) + bias_ref[direction]
    lg = _log_sigmoid(gk) * (1.0 / GLA_GATE_NORMALIZER)
    return qc, kc, v_ref[rows, :], lg


def _hgrn_inputs(refs, rows, direction, params):
    q_ref, v_ref, z_ref = refs
    lb_ref, log_lb_ref, log1m_lb_ref = params
    qr = q_ref[rows, :].astype(F32)
    qc = qr * jax.nn.sigmoid(qr)
    z = z_ref[rows, :]
    lg = _logaddexp(log_lb_ref[direction], log1m_lb_ref[direction] + _log_sigmoid(z))
    kc = (1.0 - lb_ref[direction]) * jax.nn.sigmoid(-z)
    return qc, kc, v_ref[rows, :], lg


def _bidir_lin_kernel(*refs, load_inputs, n_in, n_params, heads, chunk):
    fwd_refs = refs[:n_in]
    bwd_refs = refs[n_in:2 * n_in]
    params = refs[2 * n_in:2 * n_in + n_params]
    of_ref, ob_ref, sf_ref, sb_ref = refs[2 * n_in + n_params:]

    @pl.when(pl.program_id(1) == 0)
    def _():
        sf_ref[...] = jnp.zeros_like(sf_ref)
        sb_ref[...] = jnp.zeros_like(sb_ref)

    n_chunks = of_ref.shape[0] // chunk

    def body(c, carry):
        rows = pl.ds(pl.multiple_of(c * chunk, chunk), chunk)
        of_ref[rows, :] = _lin_chunk(*load_inputs(fwd_refs, rows, 0, params), sf_ref, reverse=False, heads=heads)
        rows = pl.ds(pl.multiple_of((n_chunks - 1 - c) * chunk, chunk), chunk)
        ob_ref[rows, :] = _lin_chunk(*load_inputs(bwd_refs, rows, 1, params), sb_ref, reverse=True, heads=heads)
        return carry

    lax.fori_loop(0, n_chunks, body, 0)


def _bidir_lin_call(name, load_inputs, arrays, col_blocks, widths, params, batch, heads, key_width):
    m = arrays[0].shape[0]
    nb = m // batch // LIN_BLOCK
    out_w = heads * HEAD_V

    def spec(width, col, rev):
        if rev:
            return pl.BlockSpec((LIN_BLOCK, width), lambda b, t: (b * nb + nb - 1 - t, col))
        return pl.BlockSpec((LIN_BLOCK, width), lambda b, t: (b * nb + t, col))

    in_specs = [spec(w, c[0], False) for w, c in zip(widths, col_blocks)]
    in_specs += [spec(w, c[1], True) for w, c in zip(widths, col_blocks)]
    in_specs += [pl.BlockSpec(p.shape, functools.partial(lambda b, t, nd: (0,) * nd, nd=p.ndim)) for p in params]
    kern = functools.partial(_bidir_lin_kernel, load_inputs=load_inputs, n_in=len(arrays), n_params=len(params),
                             heads=heads, chunk=LIN_CHUNK)
    return pl.pallas_call(
        kern,
        out_shape=(jax.ShapeDtypeStruct((m, out_w), F32), jax.ShapeDtypeStruct((m, out_w), F32)),
        grid=(batch, nb),
        in_specs=in_specs,
        out_specs=(spec(out_w, 0, False), spec(out_w, 0, True)),
        scratch_shapes=[pltpu.VMEM((HEAD_V, key_width), F32), pltpu.VMEM((HEAD_V, key_width), F32)],
        compiler_params=_params("parallel", "arbitrary"),
        name=name,
    )(*arrays, *arrays, *params)


def _finish_kernel(of_ref, ob_ref, og_ref, gain_ref, o_ref, *, heads, silu_gate):
    for h in range(heads):
        cols = slice(h * HEAD_V, (h + 1) * HEAD_V)
        y = _rms_norm_rows(of_ref[:, cols] + ob_ref[:, cols], gain_ref[...])
        g = og_ref[:, cols].astype(F32)
        gate = jax.nn.sigmoid(g)
        if silu_gate:
            gate = g * gate
        o_ref[:, cols] = (y * gate).astype(o_ref.dtype)


def finish_branch(o_fwd, o_bwd, pb, og_name, gain, *, silu_gate, tm=512):
    m, w = o_fwd.shape
    heads = w // HEAD_V
    cg = PB_COL[og_name] // w
    return pl.pallas_call(
        functools.partial(_finish_kernel, heads=heads, silu_gate=silu_gate),
        out_shape=jax.ShapeDtypeStruct((m, w), BF16),
        grid=(m // tm,),
        in_specs=[
            pl.BlockSpec((tm, w), lambda i: (i, 0)),
            pl.BlockSpec((tm, w), lambda i: (i, 0)),
            pl.BlockSpec((tm, w), lambda i: (i, cg)),
            pl.BlockSpec((1, HEAD_V), lambda i: (0, 0)),
        ],
        out_specs=pl.BlockSpec((tm, w), lambda i: (i, 0)),
        compiler_params=_params("parallel"),
        name="finish_branch",
    )(o_fwd, o_bwd, pb, gain.astype(F32).reshape(1, HEAD_V))


def gla_branch(pb, pf, batch, w_gate_up, b_gate, norm_gain):
    wpad = jnp.zeros((2, LANES, GLA_KEY_WIDTH), F32)
    for d in range(2):
        wpad = wpad.at[d, d * GLA_GATE_RANK:(d + 1) * GLA_GATE_RANK, :].set(w_gate_up[d].astype(F32))
    bias = b_gate.astype(F32).reshape(2, 1, GLA_KEY_WIDTH)
    cols = [(PB_COL["gla_q"] // GLA_KEY_WIDTH,) * 2, (PB_COL["gla_k"] // GLA_KEY_WIDTH,) * 2,
            (PB_COL["gla_v"] // GLA_VAL_WIDTH,) * 2, (PF_SMALL_COL // LANES,) * 2]
    o_f, o_b = _bidir_lin_call("gla_scan", _gla_inputs, [pb, pb, pb, pf], cols,
                               [GLA_KEY_WIDTH, GLA_KEY_WIDTH, GLA_VAL_WIDTH, LANES], [wpad, bias],
                               batch, GLA_HEADS, GLA_KEY_WIDTH)
    return finish_branch(o_f, o_b, pb, "gla_og", norm_gain, silu_gate=True)


def hgrn2_branch(pb, pf, batch, lower_bound, norm_gain):
    lb = lower_bound.astype(F32).reshape(2, 1, HGRN_KEY_WIDTH)
    log_lb = jnp.log(jnp.maximum(lb, LB_FLOOR))
    log1m_lb = jnp.log1p(-lb)
    zc = PF_COL["hg_f"] // HGRN_KEY_WIDTH
    cols = [(PB_COL["hg_q"] // HGRN_KEY_WIDTH,) * 2, (PB_COL["hg_i"] // HGRN_VAL_WIDTH,) * 2, (zc, zc + 1)]
    o_f, o_b = _bidir_lin_call("hgrn2_scan", _hgrn_inputs, [pb, pb, pf], cols,
                               [HGRN_KEY_WIDTH, HGRN_VAL_WIDTH, HGRN_KEY_WIDTH], [lb, log_lb, log1m_lb],
                               batch, HGRN_HEADS, HGRN_KEY_WIDTH)
    return finish_branch(o_f, o_b, pb, "hg_og", norm_gain, silu_gate=False)


GDN_CONV_WIDTH = 5
GDN_QKV_WIDTH = 2 * GDN_KEY_WIDTH + GDN_VAL_WIDTH
GDN_HALO = 16


def _gdn_prep_kernel(prev_ref, cur_ref, next_ref, w_ref, o_ref, xp_ref, *, blocks_per_seq):
    i = pl.program_id(0)
    t = cur_ref.shape[0]
    pos = i % blocks_per_seq
    prev = prev_ref[...].astype(F32)
    nxt = next_ref[...].astype(F32)
    xp_ref[0:GDN_HALO, :] = jnp.where(pos == 0, jnp.zeros_like(prev), prev)
    xp_ref[GDN_HALO:GDN_HALO + t, :] = cur_ref[...].astype(F32)
    xp_ref[GDN_HALO + t:, :] = jnp.where(pos == blocks_per_seq - 1, jnp.zeros_like(nxt), nxt)
    half = GDN_CONV_WIDTH // 2
    for g in range(GDN_QKV_WIDTH // LANES):
        cols = slice(g * LANES, (g + 1) * LANES)
        acc = None
        for j in range(GDN_CONV_WIDTH):
            term = xp_ref[GDN_HALO - half + j:GDN_HALO - half + j + t, cols] * w_ref[j:j + 1, cols]
            acc = term if acc is None else acc + term
        y = acc * jax.nn.sigmoid(acc)
        if g < 2 * GDN_HEADS:
            y = y * lax.rsqrt(jnp.sum(y * y, axis=-1, keepdims=True) + 1e-6)
            if g < GDN_HEADS:
                y = y * (GDN_HEAD_K ** -0.5)
        o_ref[:, cols] = y.astype(o_ref.dtype)


def gdn_prep(pb, batch, conv_w, *, t=512):
    m = pb.shape[0]
    blocks_per_seq = m // batch // t
    halo_per_block = t // GDN_HALO
    col = PB_COL["gdn_qkv"] // GDN_QKV_WIDTH
    last_halo = m // GDN_HALO - 1
    return pl.pallas_call(
        functools.partial(_gdn_prep_kernel, blocks_per_seq=blocks_per_seq),
        out_shape=jax.ShapeDtypeStruct((m, GDN_QKV_WIDTH), BF16),
        grid=(m // t,),
        in_specs=[
            pl.BlockSpec((GDN_HALO, GDN_QKV_WIDTH), lambda i: (jnp.maximum(i * halo_per_block - 1, 0), col)),
            pl.BlockSpec((t, GDN_QKV_WIDTH), lambda i: (i, col)),
            pl.BlockSpec((GDN_HALO, GDN_QKV_WIDTH),
                         lambda i: (jnp.minimum((i + 1) * halo_per_block, last_halo), col)),
            pl.BlockSpec((GDN_CONV_WIDTH, GDN_QKV_WIDTH), lambda i: (0, 0)),
        ],
        out_specs=pl.BlockSpec((t, GDN_QKV_WIDTH), lambda i: (i, 0)),
        scratch_shapes=[pltpu.VMEM((t + 2 * GDN_HALO, GDN_QKV_WIDTH), F32)],
        compiler_params=_params("parallel"),
        name="gdn_prep",
    )(pb, pb, pb, conv_w.astype(F32))


def _unit_triangular_inverse(a):
    c = a.shape[0]
    ii = lax.broadcasted_iota(jnp.int32, (c, c), 0)
    jj = lax.broadcasted_iota(jnp.int32, (c, c), 1)
    eye = (ii == jj).astype(F32)

    def same_block(s):
        return (ii // s) == (jj // s)

    d = jnp.where(same_block(8), a, 0.0)
    d2 = _dot_f32(d, d)
    d4 = _dot_f32(d2, d2)
    t = _dot_f32(_dot_f32(eye - d, eye + d2), eye + d4)
    s = 8
    while s < c:
        e = jnp.where(same_block(2 * s) & jnp.logical_not(same_block(s)), a, 0.0)
        t = t - _dot_f32(t, _dot_f32(e, t))
        s *= 2
    return t


def _softplus(x):
    return jnp.maximum(x, 0.0) + _log1p_exp_neg(jnp.abs(x))


def _gdn_chunk(qkv, small, a_scale, dt_bias, s_ref, *, direction):
    c = qkv.shape[0]
    reverse = direction == 1
    ii = lax.broadcasted_iota(jnp.int32, (c, c), 0)
    jj = lax.broadcasted_iota(jnp.int32, (c, c), 1)
    incl = (jj >= ii) if reverse else (jj <= ii)
    strict = (jj > ii) if reverse else (jj < ii)
    end = 0 if reverse else c - 1

    log_alpha = a_scale * _softplus(small + dt_bias)
    g_all = _cumsum_rows(incl, log_alpha)
    g_all_t = jnp.concatenate([g_all, jnp.zeros((LANES - c, LANES), F32)], axis=0).T
    beta_all = jax.nn.sigmoid(small)

    outs = []
    for h in range(GDN_HEADS):
        hs = slice(h * HEAD_V, (h + 1) * HEAD_V)
        q = qkv[:, hs]
        k = qkv[:, GDN_KEY_WIDTH + h * HEAD_V:GDN_KEY_WIDTH + (h + 1) * HEAD_V]
        v = qkv[:, 2 * GDN_KEY_WIDTH + h * HEAD_V:2 * GDN_KEY_WIDTH + (h + 1) * HEAD_V].astype(F32)
        kf = k.astype(F32)
        lane_a = GDN_A_LANE + direction * GDN_HEADS + h
        lane_b = GDN_B_LANE + direction * GDN_HEADS + h
        gc = jnp.broadcast_to(g_all[:, lane_a:lane_a + 1], (c, HEAD_V))
        beta = jnp.broadcast_to(beta_all[:, lane_b:lane_b + 1], (c, HEAD_V))
        diff = gc[:, :c] - jnp.broadcast_to(g_all_t[lane_a:lane_a + 1, :c], (c, c))
        decay = jnp.where(incl, jnp.exp(jnp.where(incl, diff, 0.0)), 0.0)
        k_beta = kf * beta
        kk = lax.dot_general(k_beta.astype(BF16), k, (((1,), (1,)), ((), ())), preferred_element_type=F32)
        t_inv = _unit_triangular_inverse(jnp.where(strict, kk * decay, 0.0))
        eg = jnp.exp(gc)
        sol = _dot_f32(t_inv, jnp.concatenate([v * beta, k_beta * eg], axis=1))
        u, w = sol[:, :HEAD_V], sol[:, HEAD_V:]
        attn = lax.dot_general(q, k, (((1,), (1,)), ((), ())), preferred_element_type=F32) * decay
        g_end = gc[end:end + 1, :]
        q_dec = (q.astype(F32) * eg).astype(BF16)
        k_dec = (kf * jnp.exp(g_end - gc)).astype(BF16)
        state = s_ref[h]
        state_b = state.astype(BF16)
        v_new = u - jnp.dot(w.astype(BF16), state_b, preferred_element_type=F32)
        v_new_b = v_new.astype(BF16)
        outs.append(jnp.dot(q_dec, state_b, preferred_element_type=F32)
                    + jnp.dot(attn.astype(BF16), v_new_b, preferred_element_type=F32))
        s_ref[h] = state * jnp.exp(g_end) + lax.dot_general(k_dec, v_new_b, (((0,), (0,)), ((), ())),
                                                             preferred_element_type=F32)
    return jnp.concatenate(outs, axis=1)


def _gdn_scan_kernel(qf_ref, gf_ref, qb_ref, gb_ref, a_ref, dtb_ref, of_ref, ob_ref, sf_ref, sb_ref):
    @pl.when(pl.program_id(1) == 0)
    def _():
        sf_ref[...] = jnp.zeros_like(sf_ref)
        sb_ref[...] = jnp.zeros_like(sb_ref)

    n_chunks = of_ref.shape[0] // GDN_CHUNK

    def body(c, carry):
        rows = pl.ds(pl.multiple_of(c * GDN_CHUNK, GDN_CHUNK), GDN_CHUNK)
        of_ref[rows, :] = _gdn_chunk(qf_ref[rows, :], gf_ref[rows, :], a_ref[...], dtb_ref[...], sf_ref,
                                     direction=0)
        rows = pl.ds(pl.multiple_of((n_chunks - 1 - c) * GDN_CHUNK, GDN_CHUNK), GDN_CHUNK)
        ob_ref[rows, :] = _gdn_chunk(qb_ref[rows, :], gb_ref[rows, :], a_ref[...], dtb_ref[...], sb_ref,
                                     direction=1)
        return carry

    lax.fori_loop(0, n_chunks, body, 0)


def gated_deltanet_branch(pb, pf, batch, conv_w, a_log, dt_bias, norm_gain):
    m = pb.shape[0]
    nb = m // batch // LIN_BLOCK
    qkv = gdn_prep(pb, batch, conv_w)
    n_gate = 2 * GDN_HEADS
    a_scale = jnp.zeros((1, LANES), F32).at[0, GDN_A_LANE:GDN_A_LANE + n_gate].set(
        -jnp.exp(a_log.astype(F32)).reshape(n_gate))
    dtb = jnp.zeros((1, LANES), F32).at[0, GDN_A_LANE:GDN_A_LANE + n_gate].set(dt_bias.astype(F32).reshape(n_gate))
    small_col = PF_SMALL_COL // LANES

    def fwd(width, col):
        return pl.BlockSpec((LIN_BLOCK, width), lambda b, t: (b * nb + t, col))

    def bwd(width, col):
        return pl.BlockSpec((LIN_BLOCK, width), lambda b, t: (b * nb + nb - 1 - t, col))

    def whole(shape):
        return pl.BlockSpec(shape, functools.partial(lambda b, t, nd: (0,) * nd, nd=len(shape)))

    o_f, o_b = pl.pallas_call(
        _gdn_scan_kernel,
        out_shape=(jax.ShapeDtypeStruct((m, GDN_VAL_WIDTH), F32), jax.ShapeDtypeStruct((m, GDN_VAL_WIDTH), F32)),
        grid=(batch, nb),
        in_specs=[fwd(GDN_QKV_WIDTH, 0), fwd(LANES, small_col), bwd(GDN_QKV_WIDTH, 0), bwd(LANES, small_col),
                  whole((1, LANES)), whole((1, LANES))],
        out_specs=(fwd(GDN_VAL_WIDTH, 0), bwd(GDN_VAL_WIDTH, 0)),
        scratch_shapes=[pltpu.VMEM((GDN_HEADS, GDN_HEAD_K, GDN_HEAD_V), F32),
                        pltpu.VMEM((GDN_HEADS, GDN_HEAD_K, GDN_HEAD_V), F32)],
        compiler_params=_params("parallel", "arbitrary"),
        name="gdn_scan",
    )(qkv, pf, qkv, pf, a_scale, dtb)
    return finish_branch(o_f, o_b, pb, "gdn_og", norm_gain, silu_gate=True)


GDN_PACK = GDN_HEADS * GDN_CHUNK
GDN_WY_BLOCK = 512


def _stack_heads(x, width):
    heads = x.shape[1] // width
    lane = lax.broadcasted_iota(jnp.int32, (1, x.shape[1]), 1)
    return jnp.concatenate(
        [jnp.where((lane >= h * width) & (lane < (h + 1) * width), x, 0.0).astype(BF16) for h in range(heads)],
        axis=0)


def _packed_mm(x, y):
    return jnp.dot(x.astype(BF16), _stack_heads(y, GDN_CHUNK), preferred_element_type=F32)


def _packed_inverses(mats):
    c = GDN_CHUNK
    ii = lax.broadcasted_iota(jnp.int32, (c, GDN_PACK), 0)
    jj = lax.broadcasted_iota(jnp.int32, (c, GDN_PACK), 1) % c
    eye = (ii == jj).astype(F32)

    def same_block(s):
        return (ii // s) == (jj // s)

    ds = [jnp.where(same_block(8), a, 0.0) for a in mats]
    d2s = [_packed_mm(d, d) for d in ds]
    d4s = [_packed_mm(d2, d2) for d2 in d2s]
    ts = [_packed_mm(eye - d, eye + d2) for d, d2 in zip(ds, d2s)]
    ts = [_packed_mm(t, eye + d4) for t, d4 in zip(ts, d4s)]
    s = 8
    while s < c:
        off = same_block(2 * s) & jnp.logical_not(same_block(s))
        ets = [_packed_mm(jnp.where(off, a, 0.0), t) for a, t in zip(mats, ts)]
        ts = [t - _packed_mm(t, et) for t, et in zip(ts, ets)]
        s *= 2
    return ts


def _gdn_wy_kernel(qkv_ref, small_ref, a_ref, dtb_ref, selg_ref, selk_ref, selb_ref, *out_refs):
    c = GDN_CHUNK
    n_chunks = qkv_ref.shape[0] // c
    ii = lax.broadcasted_iota(jnp.int32, (c, c), 0)
    jj = lax.broadcasted_iota(jnp.int32, (c, c), 1)
    pi = lax.broadcasted_iota(jnp.int32, (c, GDN_PACK), 0)
    pj = lax.broadcasted_iota(jnp.int32, (c, GDN_PACK), 1) % c
    eye_p = (pi == pj).astype(F32)
    ones_cc = jnp.ones((c, c), BF16)

    problems = [(ch, d) for ch in range(n_chunks) for d in range(2)]
    chunk_in = []
    for ch in range(n_chunks):
        rows = slice(ch * c, (ch + 1) * c)
        qkv = qkv_ref[rows, :]
        small = small_ref[rows, :]
        kf = qkv[:, GDN_KEY_WIDTH:2 * GDN_KEY_WIDTH].astype(F32)
        chunk_in.append(dict(
            qf=qkv[:, :GDN_KEY_WIDTH].astype(F32), kf=kf, vf=qkv[:, 2 * GDN_KEY_WIDTH:].astype(F32),
            kbd=_stack_heads(kf, HEAD_V),
            log_alpha=a_ref[...] * _softplus(small + dtb_ref[...]),
            beta_all=jax.nn.sigmoid(small)))

    def sel3(x, sel):
        hi, mid, lo = _split_bf16(x, 3)
        return (jnp.dot(hi, sel, preferred_element_type=F32)
                + (jnp.dot(mid, sel, preferred_element_type=F32) + jnp.dot(lo, sel, preferred_element_type=F32)))

    g_all = [_cumsum_rows((jj >= ii) if d else (jj <= ii), chunk_in[ch]["log_alpha"]) for ch, d in problems]
    g_pack = [sel3(g, selg_ref[d]) for g, (ch, d) in zip(g_all, problems)]
    g_wide = [sel3(g, selk_ref[d]) for g, (ch, d) in zip(g_all, problems)]
    beta_w = [sel3(chunk_in[ch]["beta_all"], selb_ref[d]) for ch, d in problems]
    g_rowp = []
    for gp in g_pack:
        hi, mid, lo = _split_bf16(gp * eye_p, 3)
        g_rowp.append(jnp.dot(ones_cc, hi, preferred_element_type=F32)
                      + (jnp.dot(ones_cc, mid, preferred_element_type=F32)
                         + jnp.dot(ones_cc, lo, preferred_element_type=F32)))
    decays, k_betas = [], []
    for gp, gr, bw, (ch, d) in zip(g_pack, g_rowp, beta_w, problems):
        incl = (pj >= pi) if d else (pj <= pi)
        decays.append(jnp.where(incl, jnp.exp(jnp.where(incl, gp - gr, 0.0)), 0.0))
        k_betas.append(chunk_in[ch]["kf"] * bw)
    kq = [lax.dot_general(jnp.concatenate([kb, chunk_in[ch]["qf"]], axis=0).astype(BF16), chunk_in[ch]["kbd"],
                          (((1,), (1,)), ((), ())), preferred_element_type=F32)
          for kb, (ch, d) in zip(k_betas, problems)]
    a_mats = []
    for x, dec, (ch, d) in zip(kq, decays, problems):
        strict = (pj > pi) if d else (pj < pi)
        a_mats.append(jnp.where(strict, x[:c] * dec, 0.0))
    t_invs = _packed_inverses(a_mats)

    for idx, (ch, d) in enumerate(problems):
        u_ref, w_ref, attn_ref, qd_ref, kd_ref, gt_ref = out_refs[6 * d:6 * d + 6]
        rows = slice(ch * c, (ch + 1) * c)
        cin = chunk_in[ch]
        gw = g_wide[idx]
        eg = jnp.exp(gw)
        t_b = t_invs[idx].astype(BF16)
        u_ref[rows, :] = jnp.dot(t_b, _stack_heads(cin["vf"] * beta_w[idx], HEAD_V), preferred_element_type=F32)
        w_ref[rows, :] = jnp.dot(t_b, _stack_heads(k_betas[idx] * eg, HEAD_V),
                                 preferred_element_type=F32).astype(w_ref.dtype)
        attn_ref[rows, :] = (kq[idx][c:] * decays[idx]).astype(attn_ref.dtype)
        end = 0 if d else c - 1
        g_end = gw[end:end + 1, :]
        qd_ref[rows, :] = (cin["qf"] * eg).astype(qd_ref.dtype)
        kd_ref[rows, :] = (cin["kf"] * jnp.exp(g_end - gw)).astype(kd_ref.dtype)
        gt_ref[ch:ch + 1, :] = jnp.exp(g_end)


def gdn_wy(qkv, pf, a_scale, dtb):
    m = qkv.shape[0]
    t = GDN_WY_BLOCK
    cpb = t // GDN_CHUNK
    selg = np.zeros((2, LANES, GDN_PACK), np.float32)
    selk = np.zeros((2, LANES, GDN_VAL_WIDTH), np.float32)
    selb = np.zeros((2, LANES, GDN_VAL_WIDTH), np.float32)
    for d in range(2):
        for h in range(GDN_HEADS):
            selg[d, GDN_A_LANE + d * GDN_HEADS + h, h * GDN_CHUNK:(h + 1) * GDN_CHUNK] = 1.0
            selk[d, GDN_A_LANE + d * GDN_HEADS + h, h * HEAD_V:(h + 1) * HEAD_V] = 1.0
            selb[d, GDN_B_LANE + d * GDN_HEADS + h, h * HEAD_V:(h + 1) * HEAD_V] = 1.0
    wide = GDN_VAL_WIDTH
    out_shape, out_specs = [], []
    for _ in range(2):
        for width, dt in ((wide, F32), (wide, BF16), (GDN_PACK, BF16), (wide, BF16), (wide, BF16)):
            out_shape.append(jax.ShapeDtypeStruct((m, width), dt))
            out_specs.append(pl.BlockSpec((t, width), lambda i: (i, 0)))
        out_shape.append(jax.ShapeDtypeStruct((m // GDN_CHUNK, wide), F32))
        out_specs.append(pl.BlockSpec((cpb, wide), lambda i: (i, 0)))
    return pl.pallas_call(
        _gdn_wy_kernel,
        out_shape=tuple(out_shape),
        grid=(m // t,),
        in_specs=[
            pl.BlockSpec((t, GDN_QKV_WIDTH), lambda i: (i, 0)),
            pl.BlockSpec((t, LANES), lambda i: (i, PF_SMALL_COL // LANES)),
            pl.BlockSpec((1, LANES), lambda i: (0, 0)),
            pl.BlockSpec((1, LANES), lambda i: (0, 0)),
            pl.BlockSpec((2, LANES, GDN_PACK), lambda i: (0, 0, 0)),
            pl.BlockSpec((2, LANES, wide), lambda i: (0, 0, 0)),
            pl.BlockSpec((2, LANES, wide), lambda i: (0, 0, 0)),
        ],
        out_specs=tuple(out_specs),
        compiler_params=_params("parallel"),
        name="gdn_wy",
    )(qkv, pf, a_scale, dtb, jnp.asarray(selg, BF16), jnp.asarray(selk, BF16), jnp.asarray(selb, BF16))


GDN_PAIR = 2 * HEAD_V


def _gdn_scan2_kernel(*refs):
    groups = (refs[0:6], refs[6:12])
    out_refs = refs[12:14]
    state_refs = refs[14:16]

    @pl.when(pl.program_id(1) == 0)
    def _():
        for s_ref in state_refs:
            s_ref[...] = jnp.zeros_like(s_ref)

    n_chunks = out_refs[0].shape[0] // GDN_CHUNK
    pairs = GDN_HEADS // 2
    pair_cols = [slice(p * GDN_PAIR, (p + 1) * GDN_PAIR) for p in range(pairs)]
    ri = lax.broadcasted_iota(jnp.int32, (GDN_PAIR, GDN_PAIR), 0) // HEAD_V
    ci = lax.broadcasted_iota(jnp.int32, (GDN_PAIR, GDN_PAIR), 1) // HEAD_V
    diag = ri == ci

    def body(c, carry):
        chunks = (c, n_chunks - 1 - c)
        rows = [pl.ds(pl.multiple_of(ch * GDN_CHUNK, GDN_CHUNK), GDN_CHUNK) for ch in chunks]
        states = [[s_ref[p] for p in range(pairs)] for s_ref in state_refs]
        states_b = [[s.astype(BF16) for s in st] for st in states]
        ws = [[jnp.dot(groups[g][1][rows[g], cols], states_b[g][p], preferred_element_type=F32)
               for p, cols in enumerate(pair_cols)] for g in range(2)]
        qs = [[jnp.dot(groups[g][3][rows[g], cols], states_b[g][p], preferred_element_type=F32)
               for p, cols in enumerate(pair_cols)] for g in range(2)]
        v_new = [groups[g][0][rows[g], :] - jnp.concatenate(ws[g], axis=1) for g in range(2)]
        av = [jnp.dot(groups[g][2][rows[g], :], _stack_heads(v_new[g], HEAD_V), preferred_element_type=F32)
              for g in range(2)]
        v_new_b = [v.astype(BF16) for v in v_new]
        upd = [[lax.dot_general(groups[g][4][rows[g], cols], v_new_b[g][:, cols], (((0,), (0,)), ((), ())),
                                preferred_element_type=F32) for cols in pair_cols] for g in range(2)]
        for g in range(2):
            out_refs[g][rows[g], :] = jnp.concatenate(qs[g], axis=1) + av[g]
            gt = groups[g][5][pl.ds(chunks[g], 1), :]
            for p, cols in enumerate(pair_cols):
                state_refs[g][p] = states[g][p] * gt[:, cols] + jnp.where(diag, upd[g][p], 0.0)
        return carry

    lax.fori_loop(0, n_chunks, body, 0)


def gated_deltanet_branch(pb, pf, batch, conv_w, a_log, dt_bias, norm_gain):
    m = pb.shape[0]
    nb = m // batch // LIN_BLOCK
    cpb = LIN_BLOCK // GDN_CHUNK
    qkv = gdn_prep(pb, batch, conv_w)
    n_gate = 2 * GDN_HEADS
    a_scale = jnp.zeros((1, LANES), F32).at[0, GDN_A_LANE:GDN_A_LANE + n_gate].set(
        -jnp.exp(a_log.astype(F32)).reshape(n_gate))
    dtb = jnp.zeros((1, LANES), F32).at[0, GDN_A_LANE:GDN_A_LANE + n_gate].set(dt_bias.astype(F32).reshape(n_gate))
    wy = gdn_wy(qkv, pf, a_scale, dtb)
    widths = (GDN_VAL_WIDTH, GDN_VAL_WIDTH, GDN_PACK, GDN_VAL_WIDTH, GDN_VAL_WIDTH)

    def fwd(rows, width):
        return pl.BlockSpec((rows, width), lambda b, t: (b * nb + t, 0))

    def bwd(rows, width):
        return pl.BlockSpec((rows, width), lambda b, t: (b * nb + nb - 1 - t, 0))

    in_specs = [fwd(LIN_BLOCK, w) for w in widths] + [fwd(cpb, GDN_VAL_WIDTH)]
    in_specs += [bwd(LIN_BLOCK, w) for w in widths] + [bwd(cpb, GDN_VAL_WIDTH)]
    state = pltpu.VMEM((GDN_HEADS // 2, GDN_PAIR, GDN_PAIR), F32)
    o_f, o_b = pl.pallas_call(
        _gdn_scan2_kernel,
        out_shape=(jax.ShapeDtypeStruct((m, GDN_VAL_WIDTH), F32), jax.ShapeDtypeStruct((m, GDN_VAL_WIDTH), F32)),
        grid=(batch, nb),
        in_specs=in_specs,
        out_specs=(fwd(LIN_BLOCK, GDN_VAL_WIDTH), bwd(LIN_BLOCK, GDN_VAL_WIDTH)),
        scratch_shapes=[state, state],
        compiler_params=_params("parallel", "arbitrary"),
        name="gdn_scan",
    )(*wy)
    return RawBranch(o_f, o_b, "gdn_og", norm_gain, True)


def _dot3(m, x):
    hi, mid, lo = _split_bf16(x, 3)
    return (jnp.dot(m, hi, preferred_element_type=F32)
            + (jnp.dot(m, mid, preferred_element_type=F32) + jnp.dot(m, lo, preferred_element_type=F32)))


def _lin_masks(block, chunk):
    i = np.arange(block)[:, None]
    j = np.arange(block)[None, :]
    same = (i // chunk) == (j // chunk)
    big = np.zeros((2, 3, block, block), np.float32)
    ends = np.zeros((2, block // chunk, block), np.float32)
    for d in range(2):
        mid = chunk - 1 - chunk // 2 if d else chunk // 2
        end = 0 if d else chunk - 1
        big[d, 0] = same & ((j >= i) if d else (j <= i))
        big[d, 1] = j == (i // chunk) * chunk + mid
        big[d, 2] = j == (i // chunk) * chunk + end
        ends[d] = j == np.arange(block // chunk)[:, None] * chunk + end
    return jnp.asarray(big, BF16), jnp.asarray(ends, BF16)


LIN_CUM_ROWS = 256
LIN_SCORE_ROWS = 128


def _chunk_causal(n, chunk, reverse):
    i = lax.broadcasted_iota(jnp.int32, (n, n), 0)
    j = lax.broadcasted_iota(jnp.int32, (n, n), 1)
    return ((i // chunk) == (j // chunk)) & ((j >= i) if reverse else (j <= i))


def _lin_intra_kernel(*refs, load_inputs, n_in, n_params, heads):
    dir_refs = (refs[:n_in], refs[n_in:2 * n_in])
    params = refs[2 * n_in:2 * n_in + n_params]
    out_refs = refs[2 * n_in + n_params:]
    c = LIN_CHUNK
    dirs = (0, 1)
    loaded = [load_inputs(dir_refs[d], slice(None), d, params) for d in dirs]
    t, w = loaded[0][0].shape
    dk = w // heads
    nc = t // c
    cums = [jnp.where(_chunk_causal(LIN_CUM_ROWS, c, d == 1), 1.0, 0.0).astype(BF16) for d in dirs]
    bs = [jnp.concatenate([_dot3(cums[d], loaded[d][3][r:r + LIN_CUM_ROWS, :])
                           for r in range(0, t, LIN_CUM_ROWS)], axis=0) for d in dirs]
    qes, kes = [], []
    for d in dirs:
        oi_ref, qd_ref, kd_ref, gt_ref = out_refs[4 * d:4 * d + 4]
        qc, kc, vc, lg = loaded[d]
        b = bs[d]
        b3 = b.reshape(nc, c, w)
        mid = c - 1 - c // 2 if d else c // 2
        end = 0 if d else c - 1
        b_mid = jnp.broadcast_to(b3[:, mid:mid + 1, :], (nc, c, w)).reshape(t, w)
        b_end = jnp.broadcast_to(b3[:, end:end + 1, :], (nc, c, w)).reshape(t, w)
        qes.append((qc * jnp.exp(b - b_mid)).astype(BF16))
        kes.append((kc * jnp.exp(b_mid - b)).astype(BF16))
        qd_ref[...] = (qc * jnp.exp(b)).astype(qd_ref.dtype)
        kd_ref[...] = (kc * jnp.exp(b_end - b)).astype(kd_ref.dtype)
        gt_ref[...] = jnp.exp(b3[:, end, :])
    keeps = [_chunk_causal(LIN_SCORE_ROWS, c, d == 1) for d in dirs]
    lane = lax.broadcasted_iota(jnp.int32, (1, LANES), 1)
    for h in range(heads):
        win = slice((h * dk) // LANES * LANES, (h * dk) // LANES * LANES + LANES)
        lo = h * dk - win.start
        vcols = slice(h * HEAD_V, (h + 1) * HEAD_V)
        tiles = [(slice(r, r + LIN_SCORE_ROWS), d) for r in range(0, t, LIN_SCORE_ROWS) for d in dirs]
        scores = []
        for rows, d in tiles:
            qh = qes[d][rows, win]
            if dk < LANES:
                qh = jnp.where((lane >= lo) & (lane < lo + dk), qh, jnp.zeros_like(qh))
            scores.append(lax.dot_general(qh, kes[d][rows, win], (((1,), (1,)), ((), ())),
                                          preferred_element_type=F32))
        probs = [jnp.where(keeps[d], s, 0.0).astype(BF16) for s, (rows, d) in zip(scores, tiles)]
        for p, (rows, d) in zip(probs, tiles):
            out_refs[4 * d][rows, vcols] = jnp.dot(p, loaded[d][2][rows, vcols], preferred_element_type=F32)


def _lin_scan_kernel(*refs, heads, chunk, unroll):
    groups = (refs[0:5], refs[5:10])
    out_refs = refs[10:12]
    state_refs = refs[12:14]

    @pl.when(pl.program_id(1) == 0)
    def _():
        for s_ref in state_refs:
            s_ref[...] = jnp.zeros_like(s_ref)

    n_chunks = out_refs[0].shape[0] // chunk
    w = state_refs[0].shape[1]
    dk = w // heads
    lane = lax.broadcasted_iota(jnp.int32, (1, w), 1)
    masks = [(lane >= h * dk) & (lane < (h + 1) * dk) for h in range(heads)]

    def stack(x):
        return jnp.concatenate([jnp.where(m, x, jnp.zeros_like(x)) for m in masks], axis=0)

    def body(it, carry):
        steps = []
        for u in range(unroll):
            c = it * unroll + u
            steps += [(0, c), (1, n_chunks - 1 - c)]
        prepared = []
        for g, ch in steps:
            rows = pl.ds(pl.multiple_of(ch * chunk, chunk), chunk)
            oi_ref, qd_ref, kd_ref, v_ref, gt_ref = groups[g]
            vc = v_ref[rows, :]
            v4 = jnp.concatenate([vc[:, h * HEAD_V:(h + 1) * HEAD_V] for h in range(heads)], axis=0)
            upd = lax.dot_general(v4, stack(kd_ref[rows, :]), (((0,), (0,)), ((), ())),
                                  preferred_element_type=F32)
            prepared.append((rows, stack(qd_ref[rows, :]), upd, gt_ref[pl.ds(ch, 1), :]))
        states = [s_ref[...] for s_ref in state_refs]
        for (g, ch), (rows, q4, upd, gt) in zip(steps, prepared):
            o_inter = lax.dot_general(q4, states[g].astype(BF16), (((1,), (1,)), ((), ())),
                                      preferred_element_type=F32)
            out_refs[g][rows, :] = groups[g][0][rows, :] + jnp.concatenate(
                [o_inter[h * chunk:(h + 1) * chunk, :] for h in range(heads)], axis=1)
            states[g] = states[g] * gt + upd
        for s_ref, st in zip(state_refs, states):
            s_ref[...] = st
        return carry

    lax.fori_loop(0, n_chunks // unroll, body, 0)


def _bidir_lin_call(name, load_inputs, arrays, col_blocks, widths, params, batch, heads, key_width, v_col):
    m = arrays[0].shape[0]
    t = LIN_BLOCK
    nb = m // batch // t
    cpb = t // LIN_CHUNK
    out_w = heads * HEAD_V
    n_in = len(arrays)

    in_specs, operands = [], []
    for d in range(2):
        for a, wd, cb in zip(arrays, widths, col_blocks):
            in_specs.append(pl.BlockSpec((t, wd), functools.partial(lambda i, c: (i, c), c=cb[d])))
            operands.append(a)
    for p in params:
        in_specs.append(pl.BlockSpec(p.shape, functools.partial(lambda i, nd: (0,) * nd, nd=p.ndim)))
    out_shape, out_specs = [], []
    for _ in range(2):
        for rows_total, rows_blk, width, dt in ((m, t, out_w, F32), (m, t, key_width, BF16),
                                                (m, t, key_width, BF16), (m // LIN_CHUNK, cpb, key_width, F32)):
            out_shape.append(jax.ShapeDtypeStruct((rows_total, width), dt))
            out_specs.append(pl.BlockSpec((rows_blk, width), lambda i: (i, 0)))

    intra = pl.pallas_call(
        functools.partial(_lin_intra_kernel, load_inputs=load_inputs, n_in=n_in, n_params=len(params),
                          heads=heads),
        out_shape=tuple(out_shape),
        grid=(m // t,),
        in_specs=in_specs,
        out_specs=tuple(out_specs),
        compiler_params=_params("parallel"),
        name=name + "_intra",
    )(*operands, *params)

    def fwd(rows, width, col=0):
        return pl.BlockSpec((rows, width), lambda b, s: (b * nb + s, col))

    def bwd(rows, width, col=0):
        return pl.BlockSpec((rows, width), lambda b, s: (b * nb + nb - 1 - s, col))

    scan_specs, scan_ops = [], []
    for d, mk in enumerate((fwd, bwd)):
        oi, qd, kd, gt = intra[4 * d:4 * d + 4]
        scan_specs += [mk(t, out_w), mk(t, key_width), mk(t, key_width), mk(t, out_w, v_col), mk(cpb, key_width)]
        scan_ops += [oi, qd, kd, arrays[0], gt]
    state = pltpu.VMEM((HEAD_V, key_width), F32)
    return pl.pallas_call(
        functools.partial(_lin_scan_kernel, heads=heads, chunk=LIN_CHUNK, unroll=4),
        out_shape=(jax.ShapeDtypeStruct((m, out_w), F32), jax.ShapeDtypeStruct((m, out_w), F32)),
        grid=(batch, nb),
        in_specs=scan_specs,
        out_specs=(fwd(t, out_w), bwd(t, out_w)),
        scratch_shapes=[state, state],
        compiler_params=_params("parallel", "arbitrary"),
        name=name + "_scan",
    )(*scan_ops)


def gla_branch(pb, pf, batch, w_gate_up, b_gate, norm_gain):
    wpad = jnp.zeros((2, LANES, GLA_KEY_WIDTH), F32)
    for d in range(2):
        wpad = wpad.at[d, d * GLA_GATE_RANK:(d + 1) * GLA_GATE_RANK, :].set(w_gate_up[d].astype(F32))
    bias = b_gate.astype(F32).reshape(2, 1, GLA_KEY_WIDTH)
    v_col = PB_COL["gla_v"] // GLA_VAL_WIDTH
    cols = [(PB_COL["gla_q"] // GLA_KEY_WIDTH,) * 2, (PB_COL["gla_k"] // GLA_KEY_WIDTH,) * 2,
            (v_col,) * 2, (PF_SMALL_COL // LANES,) * 2]
    o_f, o_b = _bidir_lin_call("gla", _gla_inputs, [pb, pb, pb, pf], cols,
                               [GLA_KEY_WIDTH, GLA_KEY_WIDTH, GLA_VAL_WIDTH, LANES], [wpad, bias],
                               batch, GLA_HEADS, GLA_KEY_WIDTH, v_col)
    return RawBranch(o_f, o_b, "gla_og", norm_gain, True)


def hgrn2_branch(pb, pf, batch, lower_bound, norm_gain):
    lb = lower_bound.astype(F32).reshape(2, 1, HGRN_KEY_WIDTH)
    log_lb = jnp.log(jnp.maximum(lb, LB_FLOOR))
    log1m_lb = jnp.log1p(-lb)
    zc = PF_COL["hg_f"] // HGRN_KEY_WIDTH
    v_col = PB_COL["hg_i"] // HGRN_VAL_WIDTH
    cols = [(PB_COL["hg_q"] // HGRN_KEY_WIDTH,) * 2, (v_col,) * 2, (zc, zc + 1)]
    o_f, o_b = _bidir_lin_call("hgrn2", _hgrn_inputs, [pb, pb, pf], cols,
                               [HGRN_KEY_WIDTH, HGRN_VAL_WIDTH, HGRN_KEY_WIDTH], [lb, log_lb, log1m_lb],
                               batch, HGRN_HEADS, HGRN_KEY_WIDTH, v_col)
    return RawBranch(o_f, o_b, "hg_og", norm_gain, False)


def _rms_norm(x, gain, eps=RMS_EPS):
    xf = x.astype(F32)
    y = xf * lax.rsqrt(jnp.mean(xf * xf, axis=-1, keepdims=True) + eps)
    return (y * gain.astype(F32)).astype(x.dtype)


def _l2_norm(x, eps=1e-6):
    xf = x.astype(F32)
    return xf * lax.rsqrt(jnp.sum(xf * xf, axis=-1, keepdims=True) + eps)


def _rev(t):
    return jnp.flip(t, axis=1)


def _centred_depthwise_conv(x, w):
    width = w.shape[0]
    return lax.conv_general_dilated(
        x, w[:, None, :], window_strides=(1,), padding=[(width // 2, width // 2)],
        dimension_numbers=("NWC", "WIO", "NWC"), feature_group_count=x.shape[-1])


def _chunk_gla(q, k, v, log_g):
    B, S, H, K = q.shape
    V = v.shape[-1]
    C = LIN_CHUNK
    n = S // C

    def chunks(t):
        return t.reshape(B, n, C, H, t.shape[-1]).transpose(1, 0, 3, 2, 4)

    q, k, v, log_g = chunks(q), chunks(k), chunks(v), chunks(log_g)
    b = jnp.cumsum(log_g, axis=-2)
    b_ref = b[..., C // 2:C // 2 + 1, :]
    incl = jnp.tril(jnp.ones((C, C), dtype=bool))
    scores = jnp.einsum("nbhik,nbhjk->nbhij", q * jnp.exp(b - b_ref), k * jnp.exp(b_ref - b))
    o_intra = jnp.einsum("nbhij,nbhjv->nbhiv", jnp.where(incl, scores, 0.0), v)
    q_dec = q * jnp.exp(b)
    k_dec = k * jnp.exp(b[..., -1:, :] - b)
    g_tot = jnp.exp(b[..., -1, :])

    def step(state, xs):
        q_c, k_c, v_c, g_c = xs
        o_c = jnp.einsum("bhik,bhkv->bhiv", q_c, state)
        state = state * g_c[..., None] + jnp.einsum("bhjk,bhjv->bhkv", k_c, v_c)
        return state, o_c

    _, o_inter = lax.scan(step, jnp.zeros((B, H, K, V), F32), (q_dec, k_dec, v, g_tot))
    o = o_intra + o_inter
    return o.transpose(1, 0, 3, 2, 4).reshape(B, S, H, V)


def _chunk_gdn(q, k, v, log_alpha, beta):
    B, S, H, K = q.shape
    V = v.shape[-1]
    C = GDN_CHUNK
    n = S // C

    def chunks(t):
        return t.reshape(B, n, C, H, t.shape[-1]).transpose(1, 0, 3, 2, 4)

    q, k, v = chunks(q), chunks(k), chunks(v)
    g = jnp.cumsum(chunks(log_alpha[..., None])[..., 0], axis=-1)
    beta = chunks(beta[..., None])
    incl = jnp.tril(jnp.ones((C, C), dtype=bool))
    strict = jnp.tril(jnp.ones((C, C), dtype=bool), -1)
    diff = g[..., :, None] - g[..., None, :]
    decay = jnp.where(incl, jnp.exp(jnp.where(incl, diff, 0.0)), 0.0)
    k_beta = k * beta
    a = jnp.where(strict, jnp.einsum("nbhik,nbhjk->nbhij", k_beta, k) * decay, 0.0)
    rhs = jnp.concatenate([v * beta, k_beta * jnp.exp(g)[..., None]], axis=-1)
    sol = lax.linalg.triangular_solve(a + jnp.eye(C, dtype=F32), rhs, left_side=True, lower=True)
    u, w = sol[..., :V], sol[..., V:]
    attn = jnp.einsum("nbhik,nbhjk->nbhij", q, k) * decay
    q_dec = q * jnp.exp(g)[..., None]
    k_dec = k * jnp.exp(g[..., -1:] - g)[..., None]
    g_tot = jnp.exp(g[..., -1])

    def step(state, xs):
        u_c, w_c, attn_c, q_c, k_c, g_c = xs
        v_new = u_c - jnp.einsum("bhck,bhkv->bhcv", w_c, state)
        o_c = jnp.einsum("bhck,bhkv->bhcv", q_c, state) + jnp.einsum("bhij,bhjv->bhiv", attn_c, v_new)
        state = state * g_c[..., None, None] + jnp.einsum("bhck,bhcv->bhkv", k_c, v_new)
        return state, o_c

    _, o = lax.scan(step, jnp.zeros((B, H, K, V), F32), (u, w, attn, q_dec, k_dec, g_tot))
    return o.transpose(1, 0, 3, 2, 4).reshape(B, S, H, V)


def _neighbourhood_attention(q, k, v, q_gain, k_gain, rel_bias):
    B, S, _ = q.shape
    rows = S // GRID_W
    win_rows = min(NA_WIN_ROWS, rows)

    def grid(t):
        return t.reshape(B, rows, GRID_W, NA_HEADS, NA_HEAD_DIM)

    q = _rms_norm(grid(q), q_gain).astype(F32) * (NA_HEAD_DIM ** -0.5)
    k = _rms_norm(grid(k), k_gain).astype(F32)
    v = grid(v).astype(F32)
    r = jnp.arange(rows)
    c = jnp.arange(GRID_W)
    row_idx = jnp.clip(r - win_rows // 2, 0, rows - win_rows)[:, None] + jnp.arange(win_rows)[None, :]
    col_start = jnp.clip(c - NA_WIN_COLS // 2, 0, GRID_W - NA_WIN_COLS)
    col_in = (c[None, :] >= col_start[:, None]) & (c[None, :] < col_start[:, None] + NA_WIN_COLS)
    k_band = k[:, row_idx]
    v_band = v[:, row_idx]
    s = jnp.einsum("brqhd,brikhd->bhrqik", q, k_band)
    dr = row_idx - r[:, None] + (NA_WIN_ROWS - 1)
    dc = jnp.clip(c[None, :] - c[:, None], 1 - NA_WIN_COLS, NA_WIN_COLS - 1) + (NA_WIN_COLS - 1)
    bias = rel_bias.astype(F32)[:, dr[:, None, :, None], dc[None, :, None, :]]
    s = jnp.where(col_in[:, None, :], s + bias[None], MASK_VALUE)
    p = jax.nn.softmax(s, axis=(-2, -1))
    o = jnp.einsum("bhrqik,brikhd->brqhd", p, v_band)
    return o.reshape(B, S, NA_WIDTH)


def _gla_branch(q, k, v, gate_lr, out_gate, w_gate_up, b_gate, norm_gain):
    B, S, _ = q.shape
    q = q.astype(F32).reshape(B, S, GLA_HEADS, GLA_HEAD_K) * (GLA_HEAD_K ** -0.5)
    k = k.astype(F32).reshape(B, S, GLA_HEADS, GLA_HEAD_K)
    v = v.astype(F32).reshape(B, S, GLA_HEADS, GLA_HEAD_V)
    lr = gate_lr.astype(F32).reshape(B, S, 2, GLA_GATE_RANK)
    gk = jnp.einsum("bsdr,drk->bsdk", lr, w_gate_up.astype(F32)) + b_gate.astype(F32)
    log_g = (jax.nn.log_sigmoid(gk) / GLA_GATE_NORMALIZER).reshape(B, S, 2, GLA_HEADS, GLA_HEAD_K)
    o = (_chunk_gla(q, k, v, log_g[:, :, 0])
         + _rev(_chunk_gla(_rev(q), _rev(k), _rev(v), _rev(log_g[:, :, 1]))))
    o = _rms_norm(o, norm_gain) * jax.nn.silu(out_gate.astype(F32)).reshape(B, S, GLA_HEADS, GLA_HEAD_V)
    return o.reshape(B, S, GLA_VAL_WIDTH)


def _gdn_branch(qkv, a, b, out_gate, conv_w, a_log, dt_bias, norm_gain):
    B, S, _ = qkv.shape
    qkv = jax.nn.silu(_centred_depthwise_conv(qkv.astype(F32), conv_w.astype(F32)))
    q, k, v = jnp.split(qkv, [GDN_KEY_WIDTH, 2 * GDN_KEY_WIDTH], axis=-1)
    q = _l2_norm(q.reshape(B, S, GDN_HEADS, GDN_HEAD_K)) * (GDN_HEAD_K ** -0.5)
    k = _l2_norm(k.reshape(B, S, GDN_HEADS, GDN_HEAD_K))
    v = v.reshape(B, S, GDN_HEADS, GDN_HEAD_V)
    a = a.astype(F32).reshape(B, S, 2, GDN_HEADS)
    b = b.astype(F32).reshape(B, S, 2, GDN_HEADS)
    log_alpha = -jnp.exp(a_log.astype(F32)) * jax.nn.softplus(a + dt_bias.astype(F32))
    beta = jax.nn.sigmoid(b)
    o = (_chunk_gdn(q, k, v, log_alpha[:, :, 0], beta[:, :, 0])
         + _rev(_chunk_gdn(_rev(q), _rev(k), _rev(v), _rev(log_alpha[:, :, 1]), _rev(beta[:, :, 1]))))
    o = _rms_norm(o, norm_gain) * jax.nn.silu(out_gate.astype(F32)).reshape(B, S, GDN_HEADS, GDN_HEAD_V)
    return o.reshape(B, S, GDN_VAL_WIDTH)


def _hgrn2_branch(q, f_pre, i, out_gate, lower_bound, norm_gain):
    B, S, _ = q.shape
    q = jax.nn.silu(q.astype(F32)).reshape(B, S, HGRN_HEADS, HGRN_HEAD_K)
    z = f_pre.astype(F32).reshape(B, S, 2, HGRN_KEY_WIDTH)
    lb = lower_bound.astype(F32)
    log_f = jnp.logaddexp(jnp.log(jnp.maximum(lb, LB_FLOOR)), jnp.log1p(-lb) + jax.nn.log_sigmoid(z))
    k_in = (1.0 - lb) * jax.nn.sigmoid(-z)
    log_f = log_f.reshape(B, S, 2, HGRN_HEADS, HGRN_HEAD_K)
    k_in = k_in.reshape(B, S, 2, HGRN_HEADS, HGRN_HEAD_K)
    v = i.astype(F32).reshape(B, S, HGRN_HEADS, HGRN_HEAD_V)
    o = (_chunk_gla(q, k_in[:, :, 0], v, log_f[:, :, 0])
         + _rev(_chunk_gla(_rev(q), _rev(k_in[:, :, 1]), _rev(v), _rev(log_f[:, :, 1]))))
    o = _rms_norm(o, norm_gain) * jax.nn.sigmoid(out_gate.astype(F32)).reshape(B, S, HGRN_HEADS, HGRN_HEAD_V)
    return o.reshape(B, S, HGRN_VAL_WIDTH)


def _memory_cross_attention(q, kv, q_gain, k_gain):
    B, S, _ = q.shape
    M = kv.shape[1]
    q = _rms_norm(q.reshape(B, S, MEM_HEADS, MEM_HEAD_DIM), q_gain).astype(F32)
    k, v = jnp.split(kv, 2, axis=-1)
    k = _rms_norm(k.reshape(B, M, MEM_HEADS, MEM_HEAD_DIM), k_gain).astype(F32)
    v = v.reshape(B, M, MEM_HEADS, MEM_HEAD_DIM).astype(F32)
    s = jnp.einsum("bshd,bmhd->bhsm", q, k) * (MEM_HEAD_DIM ** -0.5)
    p = jax.nn.softmax(s, axis=-1)
    o = jnp.einsum("bhsm,bmhd->bshd", p, v)
    return o.reshape(B, S, MEM_WIDTH)


def kernel(x, mem, g_mix, w_in, na_q_gain, na_k_gain, na_rel_bias, gla_w_gate_up, gla_b_gate, gla_norm_gain, gdn_conv_w, gdn_a_log, gdn_dt_bias, gdn_norm_gain, hgrn_lb_raw, hgrn_norm_gain, g_mem, w_mem_kv, mem_q_gain, mem_k_gain, w_branch, w_out, g_ffn, ffn_w_gate, ffn_w_up, ffn_w_down, moe_w_router, moe_b_router, moe_w_gate, moe_w_up, moe_w_down):
    B, S, D = x.shape
    n_tok = B * S
    lb_w = jax.nn.softmax(hgrn_lb_raw.astype(F32), axis=0)
    hgrn_lb = jnp.cumsum(lb_w, axis=0) - lb_w[0:1]
    x2 = x.reshape(n_tok, D)
    mem2 = mem.reshape(B * mem.shape[1], D)
    for layer in range(DEPTH):
        wb, wf = _split_w_in(w_in[layer])
        pb = rms_matmul(x2, g_mix[layer], wb, tm=1024, tn=512, out_dtype=BF16)
        pf = rms_matmul(x2, g_mix[layer], wf, tm=1024, tn=PF_WIDTH // 3, out_dtype=F32)
        kv = rms_matmul(mem2, g_mem[layer], w_mem_kv[layer].astype(BF16), tm=mem2.shape[0], tn=512,
                        out_dtype=BF16)
        branches = [
            neighbourhood_attention(pb, B, na_q_gain[layer], na_k_gain[layer], na_rel_bias[layer]),
            gla_branch(pb, pf, B, gla_w_gate_up[layer], gla_b_gate[layer], gla_norm_gain[layer]),
            gated_deltanet_branch(pb, pf, B, gdn_conv_w[layer], gdn_a_log[layer], gdn_dt_bias[layer],
                                  gdn_norm_gain[layer]),
            hgrn2_branch(pb, pf, B, hgrn_lb[layer], hgrn_norm_gain[layer]),
            memory_cross_attention(pb, kv, B, mem_q_gain[layer], mem_k_gain[layer]),
        ]
        merged = merge_branches(branches, pb, w_branch[layer].astype(BF16), tm=512, tn=512)
        x2 = matmul_residual(merged, w_out[layer].astype(BF16), x2, tm=1024, tn=512)

        j = layer // 2
        if layer % 2 == 0:
            act = rms_swiglu_up(x2, g_ffn[layer], ffn_w_gate[j].astype(BF16), ffn_w_up[j].astype(BF16),
                                tm=1024, tn=512)
            x2 = matmul_residual(act, ffn_w_down[j].astype(BF16), x2, tm=512, tn=512)
        else:
            x2 = moe_layer(x2, g_ffn[layer], moe_w_router[j], moe_b_router[j], moe_w_gate[j], moe_w_up[j],
                           moe_w_down[j])
    return x2.reshape(B, S, D)
```

```python
import functools

import jax
import jax.numpy as jnp
import numpy as np
from jax import lax
from jax.experimental import pallas as pl
from jax.experimental.pallas import tpu as pltpu

F32 = jnp.float32
BF16 = jnp.bfloat16

D_MODEL = 2048
DEPTH = 2
RMS_EPS = 1e-6
MASK_VALUE = -1e30
LB_FLOOR = 1e-30
GRID_W = 64

NA_HEADS = 8
NA_HEAD_DIM = 64
NA_WIDTH = 512
NA_WIN_ROWS = 8
NA_WIN_COLS = 16

GLA_HEADS = 4
GLA_HEAD_K = 64
GLA_HEAD_V = 128
GLA_KEY_WIDTH = 256
GLA_VAL_WIDTH = 512
GLA_GATE_RANK = 16
GLA_GATE_NORMALIZER = 16.0

GDN_HEADS = 4
GDN_HEAD_K = 128
GDN_HEAD_V = 128
GDN_KEY_WIDTH = 512
GDN_VAL_WIDTH = 512
GDN_CHUNK = 64

HGRN_HEADS = 4
HGRN_HEAD_K = 128
HGRN_HEAD_V = 128
HGRN_KEY_WIDTH = 512
HGRN_VAL_WIDTH = 512

LIN_CHUNK = 32

MEM_HEADS = 4
MEM_HEAD_DIM = 128
MEM_WIDTH = 512

N_BRANCH = 5
BRANCH_WIDTH = 512
N_EXPERTS = 8
MOE_TOP_K = 2

IN_WIDTHS = (
    NA_WIDTH, NA_WIDTH, NA_WIDTH,
    GLA_KEY_WIDTH, GLA_KEY_WIDTH, GLA_VAL_WIDTH,
    2 * GLA_GATE_RANK, GLA_VAL_WIDTH,
    2 * GDN_KEY_WIDTH + GDN_VAL_WIDTH,
    2 * GDN_HEADS, 2 * GDN_HEADS, GDN_VAL_WIDTH,
    HGRN_KEY_WIDTH, 2 * HGRN_KEY_WIDTH, HGRN_VAL_WIDTH, HGRN_VAL_WIDTH,
    MEM_WIDTH,
    N_BRANCH * D_MODEL,
)
P_IN = sum(IN_WIDTHS)

V7X_VMEM_BYTES = 64 * 1024 * 1024
VMEM_LIMIT_BYTES = V7X_VMEM_BYTES - 8 * 1024 * 1024
LANES = 128


def _params(*semantics):
    return pltpu.CompilerParams(dimension_semantics=semantics, vmem_limit_bytes=VMEM_LIMIT_BYTES)


def _sigmoid(x):
    return 0.5 * jnp.tanh(0.5 * x) + 0.5


def _rms_norm_rows(x, gain):
    ms = jnp.mean(x * x, axis=-1, keepdims=True)
    return x * lax.rsqrt(ms + RMS_EPS) * gain


def _rms_matmul_kernel(x_ref, g_ref, w_ref, o_ref, h_ref):
    @pl.when(pl.program_id(1) == 0)
    def _():
        h_ref[...] = _rms_norm_rows(x_ref[...], g_ref[...]).astype(BF16)

    o_ref[...] = jnp.dot(h_ref[...], w_ref[...], preferred_element_type=F32).astype(o_ref.dtype)


def rms_matmul(x, gain, w, *, tm, tn, out_dtype=F32):
    m, k = x.shape
    n = w.shape[1]
    return pl.pallas_call(
        _rms_matmul_kernel,
        out_shape=jax.ShapeDtypeStruct((m, n), out_dtype),
        grid=(m // tm, n // tn),
        in_specs=[
            pl.BlockSpec((tm, k), lambda i, j: (i, 0)),
            pl.BlockSpec((1, k), lambda i, j: (0, 0)),
            pl.BlockSpec((k, tn), lambda i, j: (0, j)),
        ],
        out_specs=pl.BlockSpec((tm, tn), lambda i, j: (i, j)),
        scratch_shapes=[pltpu.VMEM((tm, k), BF16)],
        compiler_params=_params("parallel", "arbitrary"),
        name="rms_matmul",
    )(x, gain.reshape(1, k), w)


def _rms_swiglu_kernel(x_ref, g_ref, wg_ref, wu_ref, o_ref, h_ref):
    @pl.when(pl.program_id(1) == 0)
    def _():
        h_ref[...] = _rms_norm_rows(x_ref[...], g_ref[...]).astype(BF16)

    h = h_ref[...]
    a = jnp.dot(h, wg_ref[...], preferred_element_type=F32)
    b = jnp.dot(h, wu_ref[...], preferred_element_type=F32)
    o_ref[...] = (a * _sigmoid(a) * b).astype(o_ref.dtype)


def rms_swiglu_up(x, gain, wg, wu, *, tm, tn):
    m, k = x.shape
    n = wg.shape[1]
    return pl.pallas_call(
        _rms_swiglu_kernel,
        out_shape=jax.ShapeDtypeStruct((m, n), BF16),
        grid=(m // tm, n // tn),
        in_specs=[
            pl.BlockSpec((tm, k), lambda i, j: (i, 0)),
            pl.BlockSpec((1, k), lambda i, j: (0, 0)),
            pl.BlockSpec((k, tn), lambda i, j: (0, j)),
            pl.BlockSpec((k, tn), lambda i, j: (0, j)),
        ],
        out_specs=pl.BlockSpec((tm, tn), lambda i, j: (i, j)),
        scratch_shapes=[pltpu.VMEM((tm, k), BF16)],
        compiler_params=_params("parallel", "arbitrary"),
        name="rms_swiglu_up",
    )(x, gain.reshape(1, k), wg, wu)


def _matmul_residual_kernel(a_ref, w_ref, r_ref, o_ref):
    o_ref[...] = r_ref[...] + jnp.dot(a_ref[...], w_ref[...], preferred_element_type=F32)


def matmul_residual(a, w, res, *, tm, tn):
    m, k = a.shape
    n = w.shape[1]
    return pl.pallas_call(
        _matmul_residual_kernel,
        out_shape=jax.ShapeDtypeStruct((m, n), F32),
        grid=(m // tm, n // tn),
        in_specs=[
            pl.BlockSpec((tm, k), lambda i, j: (i, 0)),
            pl.BlockSpec((k, tn), lambda i, j: (0, j)),
            pl.BlockSpec((tm, tn), lambda i, j: (i, j)),
        ],
        out_specs=pl.BlockSpec((tm, tn), lambda i, j: (i, j)),
        compiler_params=_params("parallel", "arbitrary"),
        name="matmul_residual",
    )(a, w, res)


class RawBranch:
    def __init__(self, o_fwd, o_bwd, og_name, gain, silu_gate):
        self.o_fwd, self.o_bwd, self.og_name, self.gain, self.silu_gate = o_fwd, o_bwd, og_name, gain, silu_gate


def _merge_kernel(*refs, raw):
    pos = 0
    br = []
    for kind in raw:
        width = 1 if kind is None else 4
        br.append(refs[pos:pos + width])
        pos += width
    gl_refs = refs[pos:pos + N_BRANCH]
    wb_ref, o_ref, fin_ref = refs[pos + N_BRANCH:pos + N_BRANCH + 3]
    raw_slot = {n: s for s, n in enumerate(n for n, kind in enumerate(raw) if kind is not None)}

    @pl.when(pl.program_id(1) == 0)
    def _():
        for n, slot in raw_slot.items():
            of_ref, ob_ref, og_ref, gain_ref = br[n]
            for h in range(BRANCH_WIDTH // LANES):
                cols = slice(h * LANES, (h + 1) * LANES)
                y = _rms_norm_rows(of_ref[:, cols] + ob_ref[:, cols], gain_ref[...])
                g = og_ref[:, cols].astype(F32)
                gate = _sigmoid(g)
                if raw[n]:
                    gate = g * gate
                fin_ref[slot, :, cols] = (y * gate).astype(fin_ref.dtype)

    acc = None
    for n in range(N_BRANCH):
        b = br[n][0][...] if raw[n] is None else fin_ref[raw_slot[n]]
        y = jnp.dot(b, wb_ref[n], preferred_element_type=F32)
        t = _sigmoid(gl_refs[n][...].astype(F32)) * y
        acc = t if acc is None else acc + t
    o_ref[...] = acc.astype(o_ref.dtype)


def merge_branches(branches, pb, w_branch, *, tm, tn):
    m = pb.shape[0]
    d = D_MODEL
    tiles_per_branch = d // tn
    tile0 = PB_COL["gates"] // tn
    row_block = pl.BlockSpec((tm, BRANCH_WIDTH), lambda i, j: (i, 0))
    in_specs, operands, raw = [], [], []
    for b in branches:
        if isinstance(b, RawBranch):
            og_col = PB_COL[b.og_name] // BRANCH_WIDTH
            in_specs += [row_block, row_block,
                         pl.BlockSpec((tm, BRANCH_WIDTH), functools.partial(lambda i, j, c: (i, c), c=og_col)),
                         pl.BlockSpec((1, LANES), lambda i, j: (0, 0))]
            operands += [b.o_fwd, b.o_bwd, pb, b.gain.astype(F32).reshape(1, LANES)]
            raw.append(b.silu_gate)
        else:
            in_specs.append(row_block)
            operands.append(b)
            raw.append(None)
    in_specs += [
        pl.BlockSpec((tm, tn), functools.partial(lambda i, j, n: (i, tile0 + n * tiles_per_branch + j), n=n))
        for n in range(N_BRANCH)
    ]
    in_specs += [pl.BlockSpec((N_BRANCH, BRANCH_WIDTH, tn), lambda i, j: (0, 0, j))]
    n_raw = sum(kind is not None for kind in raw)
    return pl.pallas_call(
        functools.partial(_merge_kernel, raw=tuple(raw)),
        out_shape=jax.ShapeDtypeStruct((m, d), BF16),
        grid=(m // tm, d // tn),
        in_specs=in_specs,
        out_specs=pl.BlockSpec((tm, tn), lambda i, j: (i, j)),
        scratch_shapes=[pltpu.VMEM((max(n_raw, 1), tm, BRANCH_WIDTH), BF16)],
        compiler_params=_params("parallel", "arbitrary"),
        name="merge_branches",
    )(*operands, *([pb] * N_BRANCH), w_branch)


def _router_kernel(x_ref, g_ref, w_ref, b_ref, o_ref, h_ref, cnt_ref, run_ref, *, n_experts):
    @pl.when(pl.program_id(0) == 0)
    def _():
        run_ref[...] = jnp.zeros_like(run_ref)

    h = _rms_norm_rows(x_ref[...], g_ref[...])
    h_ref[...] = h.astype(h_ref.dtype)
    logits = _dot_f32(h, w_ref[...]) + b_ref[...]
    lane = lax.broadcasted_iota(jnp.int32, logits.shape, 1).astype(F32)
    neg = -jnp.inf
    lm = jnp.where(lane < n_experts, logits, neg)
    m1 = jnp.max(lm, axis=-1, keepdims=True)
    i1 = jnp.min(jnp.where(lm == m1, lane, float(LANES)), axis=-1, keepdims=True)
    lm2 = jnp.where(lane == i1, neg, lm)
    m2 = jnp.max(lm2, axis=-1, keepdims=True)
    i2 = jnp.min(jnp.where(lm2 == m2, lane, float(LANES)), axis=-1, keepdims=True)
    t = jnp.exp(m2 - m1)
    den = 1.0 + t

    tm = logits.shape[0]
    before = (lax.broadcasted_iota(jnp.int32, (tm, tm), 1)
              < lax.broadcasted_iota(jnp.int32, (tm, tm), 0))
    before = jnp.where(before, 1.0, 0.0).astype(BF16)
    pick1 = lane == i1
    pick2 = lane == i2
    oh1 = jnp.where(pick1, 1.0, 0.0)
    oh2 = jnp.where(pick2, 1.0, 0.0)
    pre1 = jnp.dot(before, oh1.astype(BF16), preferred_element_type=F32)
    pre2 = jnp.dot(before, oh2.astype(BF16), preferred_element_type=F32)
    tot1 = jnp.sum(oh1, axis=0, keepdims=True)
    tot2 = jnp.sum(oh2, axis=0, keepdims=True)
    run = run_ref[...]
    rank1 = jnp.sum(jnp.where(pick1, pre1 + run, 0.0), axis=-1, keepdims=True)
    rank2 = jnp.sum(jnp.where(pick2, pre2 + (run + tot1), 0.0), axis=-1, keepdims=True)
    run = run + tot1 + tot2
    run_ref[...] = run
    cnt_ref[...] = jnp.broadcast_to(run, cnt_ref.shape)

    out = jnp.where(lane == 0, 1.0 / den, jnp.where(lane == 1, t / den, jnp.where(lane == 2, i1, i2)))
    out = jnp.where(lane == 4, rank1, jnp.where(lane == 5, rank2, out))
    o_ref[...] = jnp.where(lane < 6, out, 0.0)


def router_top2(x, gain, w_router, b_router, *, tm=512):
    m, k = x.shape
    e = w_router.shape[1]
    w_pad = jnp.zeros((k, LANES), F32).at[:, :e].set(w_router.astype(F32))
    b_pad = jnp.zeros((1, LANES), F32).at[0, :e].set(b_router.astype(F32))
    route, h, cnt = pl.pallas_call(
        functools.partial(_router_kernel, n_experts=e),
        out_shape=(jax.ShapeDtypeStruct((m, LANES), F32), jax.ShapeDtypeStruct((m, k), BF16),
                   jax.ShapeDtypeStruct((8, LANES), F32)),
        grid=(m // tm,),
        in_specs=[
            pl.BlockSpec((tm, k), lambda i: (i, 0)),
            pl.BlockSpec((1, k), lambda i: (0, 0)),
            pl.BlockSpec((k, LANES), lambda i: (0, 0)),
            pl.BlockSpec((1, LANES), lambda i: (0, 0)),
        ],
        out_specs=(pl.BlockSpec((tm, LANES), lambda i: (i, 0)), pl.BlockSpec((tm, k), lambda i: (i, 0)),
                   pl.BlockSpec((8, LANES), lambda i: (0, 0))),
        scratch_shapes=[pltpu.VMEM((1, LANES), F32)],
        compiler_params=_params("arbitrary"),
        name="router_top2",
    )(x, gain.reshape(1, k), w_pad, b_pad)
    return route, h, cnt[0, :e].astype(jnp.int32)


MOE_TILE = 1024
MOE_SUB = 256
MOE_FF_TILE = 512


def _moe_kernel(tile_e_ref, tile_rows_ref, n_used_ref, x_ref, wg_ref, wu_ref, wd_ref, o_ref, acc_ref):
    i = pl.program_id(0)
    j = pl.program_id(1)
    last = pl.num_programs(1) - 1
    valid = tile_rows_ref[i]
    n_sub = (valid + (MOE_SUB - 1)) // MOE_SUB

    for k in range(1, MOE_TILE // MOE_SUB + 1):
        rows = slice(0, k * MOE_SUB)

        @pl.when(n_sub == k)
        def _(rows=rows):
            x = x_ref[rows, :]
            a = jnp.dot(x, wg_ref[0].astype(BF16), preferred_element_type=F32)
            b = jnp.dot(x, wu_ref[0].astype(BF16), preferred_element_type=F32)
            act = (a * _sigmoid(a) * b).astype(BF16)
            part = jnp.dot(act, wd_ref[0].astype(BF16), preferred_element_type=F32)

            @pl.when(j == 0)
            def _():
                acc_ref[rows, :] = part

            @pl.when(j > 0)
            def _():
                acc_ref[rows, :] += part

    for s in range(0, MOE_TILE, MOE_SUB):
        rows = slice(s, s + MOE_SUB)
        filled = s < valid

        @pl.when(jnp.logical_and(filled, j == last))
        def _(rows=rows):
            o_ref[rows, :] = acc_ref[rows, :].astype(o_ref.dtype)

        @pl.when(jnp.logical_and(jnp.logical_not(filled), j == last))
        def _(rows=rows):
            o_ref[rows, :] = jnp.zeros((MOE_SUB, o_ref.shape[1]), o_ref.dtype)


def moe_experts(xb, tile_e, tile_rows, n_used, wg, wu, wd):
    rows, d = xb.shape
    ff = wg.shape[2]
    tm, tf = MOE_TILE, MOE_FF_TILE
    n_tiles = rows // tm
    last_j = ff // tf - 1

    def x_map(i, j, te, tr, nu):
        return (jnp.minimum(i, nu[0] - 1), 0)

    def up_map(i, j, te, tr, nu):
        return (te[i], 0, jnp.where(i < nu[0], j, last_j))

    def down_map(i, j, te, tr, nu):
        return (te[i], jnp.where(i < nu[0], j, last_j), 0)

    grid_spec = pltpu.PrefetchScalarGridSpec(
        num_scalar_prefetch=3,
        grid=(n_tiles, ff // tf),
        in_specs=[
            pl.BlockSpec((tm, d), x_map, pipeline_mode=pl.Buffered(1)),
            pl.BlockSpec((1, d, tf), up_map),
            pl.BlockSpec((1, d, tf), up_map),
            pl.BlockSpec((1, tf, d), down_map),
        ],
        out_specs=pl.BlockSpec((tm, d), lambda i, j, te, tr, nu: (i, 0)),
        scratch_shapes=[pltpu.VMEM((tm, d), F32)],
    )
    return pl.pallas_call(
        _moe_kernel,
        out_shape=jax.ShapeDtypeStruct((rows, d), BF16),
        grid_spec=grid_spec,
        compiler_params=_params("arbitrary", "arbitrary"),
        name="moe_experts",
    )(tile_e, tile_rows, n_used, xb, wg, wu, wd)


def _moe_combine_kernel(x_ref, y0_ref, y1_ref, r_ref, o_ref):
    w = r_ref[...]
    o_ref[...] = x_ref[...] + w[:, 0:1] * y0_ref[...].astype(F32) + w[:, 1:2] * y1_ref[...].astype(F32)


def moe_combine(x2d, y0, y1, route, *, tm=512):
    n, d = x2d.shape
    row_block = pl.BlockSpec((tm, d), lambda i: (i, 0))
    return pl.pallas_call(
        _moe_combine_kernel,
        out_shape=jax.ShapeDtypeStruct((n, d), F32),
        grid=(n // tm,),
        in_specs=[row_block, row_block, row_block, pl.BlockSpec((tm, LANES), lambda i: (i, 0))],
        out_specs=row_block,
        compiler_params=_params("parallel"),
        name="moe_combine",
    )(x2d, y0, y1, route)


def moe_layer(x2d, gain, w_router, b_router, wg, wu, wd):
    n, d = x2d.shape
    e = N_EXPERTS
    route, h, counts = router_top2(x2d, gain, w_router, b_router)
    nk = n * MOE_TOP_K
    n_tiles = -(-nk // MOE_TILE) + e
    flat_e = route[:, 2:2 + MOE_TOP_K].astype(jnp.int32).reshape(nk)
    rank = route[:, 4:4 + MOE_TOP_K].astype(jnp.int32).reshape(nk)
    flat_tok = jnp.repeat(jnp.arange(n, dtype=jnp.int32), MOE_TOP_K)
    padded = (counts + MOE_TILE - 1) // MOE_TILE * MOE_TILE
    pad_end = jnp.cumsum(padded)
    pad_start = pad_end - padded
    slot = (pad_start[flat_e] + rank).astype(jnp.int32)
    slot_tok = jnp.zeros((n_tiles * MOE_TILE,), jnp.int32).at[slot].set(flat_tok)
    tile_start = jnp.arange(n_tiles, dtype=jnp.int32) * MOE_TILE
    tile_e = jnp.minimum(jnp.searchsorted(pad_end, tile_start, side="right"), e - 1).astype(jnp.int32)
    tile_rows = jnp.clip(pad_start[tile_e] + counts[tile_e] - tile_start, 0, MOE_TILE).astype(jnp.int32)
    tile_rows = jnp.where(tile_start < pad_end[-1], tile_rows, 0)
    n_used = (pad_end[-1] // MOE_TILE).astype(jnp.int32).reshape(1)
    tile_e = jnp.where(tile_start < pad_end[-1], tile_e, tile_e[jnp.maximum(n_used[0] - 1, 0)])

    xb = h[slot_tok]
    yb = moe_experts(xb, tile_e, tile_rows, n_used, wg, wu, wd)
    slot2 = slot.reshape(n, MOE_TOP_K)
    return moe_combine(x2d, yb[slot2[:, 0]], yb[slot2[:, 1]], route)


def _rms_kernel(x_ref, g_ref, o_ref):
    o_ref[...] = _rms_norm_rows(x_ref[...], g_ref[...]).astype(o_ref.dtype)


def rms_only(x, gain, *, tm=1024):
    m, k = x.shape
    return pl.pallas_call(
        _rms_kernel,
        out_shape=jax.ShapeDtypeStruct((m, k), BF16),
        grid=(m // tm,),
        in_specs=[pl.BlockSpec((tm, k), lambda i: (i, 0)), pl.BlockSpec((1, k), lambda i: (0, 0))],
        out_specs=pl.BlockSpec((tm, k), lambda i: (i, 0)),
        compiler_params=_params("parallel"),
        name="rms_only",
    )(x, gain.reshape(1, k))


_SRC = dict(zip(
    ("na_q", "na_k", "na_v", "gla_q", "gla_k", "gla_v", "gla_lr", "gla_og", "gdn_qkv", "gdn_a", "gdn_b",
     "gdn_og", "hg_q", "hg_f", "hg_i", "hg_og", "mem_q", "gates"),
    zip(np.cumsum((0,) + IN_WIDTHS[:-1]).tolist(), IN_WIDTHS)))
_PB_ORDER = ("na_q", "na_k", "na_v", "gla_q", "gla_k", "gla_v", "gla_og", "gdn_qkv", "gdn_og", "hg_q", "hg_i",
             "hg_og", "mem_q", "gates")
_PF_ORDER = ("hg_f", "gla_lr", "gdn_a", "gdn_b")
PB_COL = {}
_c = 0
for _name in _PB_ORDER:
    PB_COL[_name] = _c
    _c += _SRC[_name][1]
PB_WIDTH = _c
PF_COL = {}
_c = 0
for _name in _PF_ORDER:
    PF_COL[_name] = _c
    _c += _SRC[_name][1]
PF_WIDTH = -(-_c // LANES) * LANES
PF_SMALL_COL = PF_COL["gla_lr"]
GDN_A_LANE = PF_COL["gdn_a"] - PF_SMALL_COL
GDN_B_LANE = PF_COL["gdn_b"] - PF_SMALL_COL


def _split_w_in(w):
    wb = jnp.concatenate([w[:, _SRC[n][0]:_SRC[n][0] + _SRC[n][1]] for n in _PB_ORDER], axis=1)
    wf = jnp.concatenate([w[:, _SRC[n][0]:_SRC[n][0] + _SRC[n][1]] for n in _PF_ORDER], axis=1)
    wf = jnp.pad(wf, ((0, 0), (0, PF_WIDTH - wf.shape[1])))
    return wb.astype(BF16), wf.astype(BF16)


def _segment_rms(x, gain, seg_ones, seg_width):
    sq = x * x
    hi = sq.astype(BF16)
    lo = (sq - hi.astype(F32)).astype(BF16)
    ss = (jnp.dot(hi, seg_ones, preferred_element_type=F32)
          + jnp.dot(lo, seg_ones, preferred_element_type=F32))
    return x * lax.rsqrt(ss * (1.0 / seg_width) + RMS_EPS) * gain


NA_ROWS_PER_STEP = 8
NA_BAND = NA_WIN_ROWS * GRID_W


def _na_bias_table(rel_bias):
    c = np.arange(GRID_W)
    dc = np.clip(c[None, :] - c[:, None], 1 - NA_WIN_COLS, NA_WIN_COLS - 1) + (NA_WIN_COLS - 1)
    col_start = np.clip(c - NA_WIN_COLS // 2, 0, GRID_W - NA_WIN_COLS)
    col_in = (c[None, :] >= col_start[:, None]) & (c[None, :] < col_start[:, None] + NA_WIN_COLS)
    cfg = np.arange(NA_WIN_ROWS)[:, None]
    dr = np.arange(NA_WIN_ROWS)[None, :] - cfg + (NA_WIN_ROWS - 1)
    t = rel_bias.astype(F32)[:, dr][:, :, :, dc]
    t = jnp.where(col_in[None, None, None], t, MASK_VALUE)
    return t.transpose(1, 0, 3, 2, 4).reshape(NA_WIN_ROWS, NA_HEADS, GRID_W, NA_BAND)


def _na_kernel(q_ref, k_ref, v_ref, qg_ref, kg_ref, seg_ref, bias_ref, o_ref, kn_ref):
    step = pl.program_id(1)
    rows_total = k_ref.shape[0] // GRID_W
    seg = seg_ref[...]

    @pl.when(step == 0)
    def _():
        def norm_keys(t, carry):
            rows = pl.ds(pl.multiple_of(t * 256, 256), 256)
            kn_ref[rows, :] = _segment_rms(k_ref[rows, :].astype(F32), kg_ref[...], seg, NA_HEAD_DIM).astype(BF16)
            return carry
        lax.fori_loop(0, k_ref.shape[0] // 256, norm_keys, 0)

    lane = lax.broadcasted_iota(jnp.int32, (1, LANES), 1)
    low_half = lane < NA_HEAD_DIM

    def one_row(rr, carry):
        r = step * NA_ROWS_PER_STEP + rr
        row_start = jnp.clip(r - NA_WIN_ROWS // 2, 0, rows_total - NA_WIN_ROWS)
        cfg = r - row_start
        qrows = pl.ds(pl.multiple_of(rr * GRID_W, GRID_W), GRID_W)
        band = pl.ds(pl.multiple_of(row_start * GRID_W, GRID_W), NA_BAND)
        qn = (_segment_rms(q_ref[qrows, :].astype(F32), qg_ref[...], seg, NA_HEAD_DIM)
              * (NA_HEAD_DIM ** -0.5)).astype(BF16)
        heads = [(pair, half) for pair in range(NA_HEADS // 2) for half in range(2)]
        scores = []
        for pair, half in heads:
            cols = slice(pair * LANES, (pair + 1) * LANES)
            qp = qn[:, cols]
            keep = low_half if half == 0 else jnp.logical_not(low_half)
            qm = jnp.where(keep, qp, jnp.zeros_like(qp))
            scores.append(lax.dot_general(qm, kn_ref[band, cols], (((1,), (1,)), ((), ())),
                                          preferred_element_type=F32))
        exps, sums = [], []
        for s, (pair, half) in zip(scores, heads):
            s = s + bias_ref[cfg, 2 * pair + half]
            e = jnp.exp(s - jnp.max(s, axis=-1, keepdims=True))
            sums.append(jnp.sum(e, axis=-1, keepdims=True))
            exps.append(e.astype(BF16))
        outs = [jnp.dot(e, v_ref[band, slice(pair * LANES, (pair + 1) * LANES)], preferred_element_type=F32) / l
                for e, l, (pair, half) in zip(exps, sums, heads)]
        for pair in range(NA_HEADS // 2):
            cols = slice(pair * LANES, (pair + 1) * LANES)
            o_ref[qrows, cols] = jnp.where(low_half, outs[2 * pair], outs[2 * pair + 1]).astype(o_ref.dtype)
        return carry

    lax.fori_loop(0, NA_ROWS_PER_STEP, one_row, 0)


def neighbourhood_attention(pb, batch, q_gain, k_gain, rel_bias):
    m = pb.shape[0]
    s = m // batch
    tq = NA_ROWS_PER_STEP * GRID_W
    steps = s // tq
    qg = jnp.tile(q_gain.astype(F32), NA_HEADS).reshape(1, NA_WIDTH)
    kg = jnp.tile(k_gain.astype(F32), NA_HEADS).reshape(1, NA_WIDTH)
    seg = jnp.asarray(np.kron(np.eye(NA_HEADS), np.ones((NA_HEAD_DIM, NA_HEAD_DIM))), BF16)
    bias = _na_bias_table(rel_bias)
    cq, ck, cv = (PB_COL[n] // NA_WIDTH for n in ("na_q", "na_k", "na_v"))
    return pl.pallas_call(
        _na_kernel,
        out_shape=jax.ShapeDtypeStruct((m, NA_WIDTH), BF16),
        grid=(batch, steps),
        in_specs=[
            pl.BlockSpec((tq, NA_WIDTH), lambda b, t: (b * steps + t, cq)),
            pl.BlockSpec((s, NA_WIDTH), lambda b, t: (b, ck)),
            pl.BlockSpec((s, NA_WIDTH), lambda b, t: (b, cv)),
            pl.BlockSpec((1, NA_WIDTH), lambda b, t: (0, 0)),
            pl.BlockSpec((1, NA_WIDTH), lambda b, t: (0, 0)),
            pl.BlockSpec((NA_WIDTH, NA_WIDTH), lambda b, t: (0, 0)),
            pl.BlockSpec((NA_WIN_ROWS, NA_HEADS, GRID_W, NA_BAND), lambda b, t: (0, 0, 0, 0)),
        ],
        out_specs=pl.BlockSpec((tq, NA_WIDTH), lambda b, t: (b * steps + t, 0)),
        scratch_shapes=[pltpu.VMEM((s, NA_WIDTH), BF16)],
        compiler_params=_params("parallel", "arbitrary"),
        name="neighbourhood_attention",
    )(pb, pb, pb, qg, kg, seg, bias)


def _mem_attn_kernel(q_ref, kv_ref, qg_ref, kg_ref, o_ref, kn_ref):
    @pl.when(pl.program_id(1) == 0)
    def _():
        for h in range(MEM_HEADS):
            cols = slice(h * MEM_HEAD_DIM, (h + 1) * MEM_HEAD_DIM)
            kn_ref[:, cols] = _rms_norm_rows(kv_ref[:, cols].astype(F32), kg_ref[...]).astype(BF16)

    head_cols = [slice(h * MEM_HEAD_DIM, (h + 1) * MEM_HEAD_DIM) for h in range(MEM_HEADS)]
    qns = [_rms_norm_rows(q_ref[:, cols].astype(F32), qg_ref[...]).astype(BF16) for cols in head_cols]
    scores = [lax.dot_general(qn, kn_ref[:, cols], (((1,), (1,)), ((), ())), preferred_element_type=F32)
              for qn, cols in zip(qns, head_cols)]
    exps, sums = [], []
    for s in scores:
        s = s * (MEM_HEAD_DIM ** -0.5)
        e = jnp.exp(s - jnp.max(s, axis=-1, keepdims=True))
        sums.append(jnp.sum(e, axis=-1, keepdims=True))
        exps.append(e.astype(BF16))
    outs = [jnp.dot(e, kv_ref[:, MEM_WIDTH + cols.start:MEM_WIDTH + cols.stop], preferred_element_type=F32)
            for e, cols in zip(exps, head_cols)]
    for o, l, cols in zip(outs, sums, head_cols):
        o_ref[:, cols] = (o / l).astype(o_ref.dtype)


def memory_cross_attention(pb, kv, batch, q_gain, k_gain, *, tq=512):
    m = pb.shape[0]
    steps = m // batch // tq
    n_mem = kv.shape[0] // batch
    cq = PB_COL["mem_q"] // MEM_WIDTH
    return pl.pallas_call(
        _mem_attn_kernel,
        out_shape=jax.ShapeDtypeStruct((m, MEM_WIDTH), BF16),
        grid=(batch, steps),
        in_specs=[
            pl.BlockSpec((tq, MEM_WIDTH), lambda b, t: (b * steps + t, cq)),
            pl.BlockSpec((n_mem, 2 * MEM_WIDTH), lambda b, t: (b, 0)),
            pl.BlockSpec((1, MEM_HEAD_DIM), lambda b, t: (0, 0)),
            pl.BlockSpec((1, MEM_HEAD_DIM), lambda b, t: (0, 0)),
        ],
        out_specs=pl.BlockSpec((tq, MEM_WIDTH), lambda b, t: (b * steps + t, 0)),
        scratch_shapes=[pltpu.VMEM((n_mem, MEM_WIDTH), BF16)],
        compiler_params=_params("parallel", "arbitrary"),
        name="memory_cross_attention",
    )(pb, kv, q_gain.astype(F32).reshape(1, MEM_HEAD_DIM), k_gain.astype(F32).reshape(1, MEM_HEAD_DIM))


LIN_BLOCK = 512
HEAD_V = 128


def _log1p_exp_neg(t):
    return jnp.log(1.0 + jnp.exp(-t))


def _log_sigmoid(x):
    return jnp.minimum(x, 0.0) - _log1p_exp_neg(jnp.abs(x))


def _logaddexp(a, b):
    return jnp.maximum(a, b) + _log1p_exp_neg(jnp.abs(a - b))


def _split_bf16(x, terms):
    parts = []
    for _ in range(terms):
        p = x.astype(BF16)
        parts.append(p)
        x = x - p.astype(F32)
    return parts


def _dot_f32(a, b):
    a_hi, a_lo = _split_bf16(a, 2)
    b_hi, b_lo = _split_bf16(b, 2)
    return (jnp.dot(a_hi, b_hi, preferred_element_type=F32)
            + (jnp.dot(a_hi, b_lo, preferred_element_type=F32) + jnp.dot(a_lo, b_hi, preferred_element_type=F32)))


def _cumsum_rows(mask, x):
    m = jnp.where(mask, 1.0, 0.0).astype(BF16)
    hi, mid, lo = _split_bf16(x, 3)
    return (jnp.dot(m, hi, preferred_element_type=F32)
            + (jnp.dot(m, mid, preferred_element_type=F32) + jnp.dot(m, lo, preferred_element_type=F32)))


def _lin_chunk(qc, kc, vc, lg, s_ref, *, reverse, heads):
    c, w = qc.shape
    dk = w // heads
    ii = lax.broadcasted_iota(jnp.int32, (c, c), 0)
    jj = lax.broadcasted_iota(jnp.int32, (c, c), 1)
    causal = (jj >= ii) if reverse else (jj <= ii)
    b = _cumsum_rows(causal, lg)
    mid = c - 1 - c // 2 if reverse else c // 2
    end = 0 if reverse else c - 1
    b_mid = b[mid:mid + 1, :]
    b_end = b[end:end + 1, :]
    qe = qc * jnp.exp(b - b_mid)
    ke = (kc * jnp.exp(b_mid - b)).astype(BF16)
    q_dec = qc * jnp.exp(b)
    k_dec = kc * jnp.exp(b_end - b)
    g_tot = jnp.exp(b_end)

    lane = lax.broadcasted_iota(jnp.int32, (1, w), 1)
    head_mask = [((lane >= h * dk) & (lane < (h + 1) * dk)).astype(F32) for h in range(heads)]
    q4 = jnp.concatenate([qe * head_mask[h] for h in range(heads)], axis=0).astype(BF16)
    q4d = jnp.concatenate([q_dec * head_mask[h] for h in range(heads)], axis=0).astype(BF16)
    k4 = jnp.concatenate([k_dec * head_mask[h] for h in range(heads)], axis=0).astype(BF16)
    v4 = jnp.concatenate([vc[:, h * HEAD_V:(h + 1) * HEAD_V] for h in range(heads)], axis=0)

    scores = lax.dot_general(q4, ke, (((1,), (1,)), ((), ())), preferred_element_type=F32)
    ri = lax.broadcasted_iota(jnp.int32, (heads * c, c), 0) % c
    cj = lax.broadcasted_iota(jnp.int32, (heads * c, c), 1)
    keep = (cj >= ri) if reverse else (cj <= ri)
    scores = jnp.where(keep, scores, 0.0).astype(BF16)
    state = s_ref[...]
    o_inter = lax.dot_general(q4d, state.astype(BF16), (((1,), (1,)), ((), ())),
                              preferred_element_type=F32)
    outs = []
    for h in range(heads):
        rows = slice(h * c, (h + 1) * c)
        o_intra = jnp.dot(scores[rows, :], vc[:, h * HEAD_V:(h + 1) * HEAD_V], preferred_element_type=F32)
        outs.append(o_intra + o_inter[rows, :])
    s_ref[...] = state * g_tot + lax.dot_general(v4, k4, (((0,), (0,)), ((), ())), preferred_element_type=F32)
    return jnp.concatenate(outs, axis=1)


def _gla_inputs(refs, rows, direction, params):
    q_ref, k_ref, v_ref, g_ref = refs
    wpad_ref, bias_ref = params
    qc = q_ref[rows, :].astype(F32) * (GLA_HEAD_K ** -0.5)
    kc = k_ref[rows, :].astype(F32)
    gk = _dot_f32(g_ref[rows, :], wpad_ref[direction]) + bias_ref[direction]
    lg = _log_sigmoid(gk) * (1.0 / GLA_GATE_NORMALIZER)
    return qc, kc, v_ref[rows, :], lg


def _hgrn_inputs(refs, rows, direction, params):
    q_ref, v_ref, z_ref = refs
    lb_ref, log_lb_ref, log1m_lb_ref = params
    qr = q_ref[rows, :].astype(F32)
    qc = qr * _sigmoid(qr)
    z = z_ref[rows, :]
    lg = _logaddexp(log_lb_ref[direction], log1m_lb_ref[direction] + _log_sigmoid(z))
    kc = (1.0 - lb_ref[direction]) * _sigmoid(-z)
    return qc, kc, v_ref[rows, :], lg


def _bidir_lin_kernel(*refs, load_inputs, n_in, n_params, heads, chunk):
    fwd_refs = refs[:n_in]
    bwd_refs = refs[n_in:2 * n_in]
    params = refs[2 * n_in:2 * n_in + n_params]
    of_ref, ob_ref, sf_ref, sb_ref = refs[2 * n_in + n_params:]

    @pl.when(pl.program_id(1) == 0)
    def _():
        sf_ref[...] = jnp.zeros_like(sf_ref)
        sb_ref[...] = jnp.zeros_like(sb_ref)

    n_chunks = of_ref.shape[0] // chunk

    def body(c, carry):
        rows = pl.ds(pl.multiple_of(c * chunk, chunk), chunk)
        of_ref[rows, :] = _lin_chunk(*load_inputs(fwd_refs, rows, 0, params), sf_ref, reverse=False, heads=heads)
        rows = pl.ds(pl.multiple_of((n_chunks - 1 - c) * chunk, chunk), chunk)
        ob_ref[rows, :] = _lin_chunk(*load_inputs(bwd_refs, rows, 1, params), sb_ref, reverse=True, heads=heads)
        return carry

    lax.fori_loop(0, n_chunks, body, 0)


def _bidir_lin_call(name, load_inputs, arrays, col_blocks, widths, params, batch, heads, key_width):
    m = arrays[0].shape[0]
    nb = m // batch // LIN_BLOCK
    out_w = heads * HEAD_V

    def spec(width, col, rev):
        if rev:
            return pl.BlockSpec((LIN_BLOCK, width), lambda b, t: (b * nb + nb - 1 - t, col))
        return pl.BlockSpec((LIN_BLOCK, width), lambda b, t: (b * nb + t, col))

    in_specs = [spec(w, c[0], False) for w, c in zip(widths, col_blocks)]
    in_specs += [spec(w, c[1], True) for w, c in zip(widths, col_blocks)]
    in_specs += [pl.BlockSpec(p.shape, functools.partial(lambda b, t, nd: (0,) * nd, nd=p.ndim)) for p in params]
    kern = functools.partial(_bidir_lin_kernel, load_inputs=load_inputs, n_in=len(arrays), n_params=len(params),
                             heads=heads, chunk=LIN_CHUNK)
    return pl.pallas_call(
        kern,
        out_shape=(jax.ShapeDtypeStruct((m, out_w), F32), jax.ShapeDtypeStruct((m, out_w), F32)),
        grid=(batch, nb),
        in_specs=in_specs,
        out_specs=(spec(out_w, 0, False), spec(out_w, 0, True)),
        scratch_shapes=[pltpu.VMEM((HEAD_V, key_width), F32), pltpu.VMEM((HEAD_V, key_width), F32)],
        compiler_params=_params("parallel", "arbitrary"),
        name=name,
    )(*arrays, *arrays, *params)


def _finish_kernel(of_ref, ob_ref, og_ref, gain_ref, o_ref, *, heads, silu_gate):
    for h in range(heads):
        cols = slice(h * HEAD_V, (h + 1) * HEAD_V)
        y = _rms_norm_rows(of_ref[:, cols] + ob_ref[:, cols], gain_ref[...])
        g = og_ref[:, cols].astype(F32)
        gate = _sigmoid(g)
        if silu_gate:
            gate = g * gate
        o_ref[:, cols] = (y * gate).astype(o_ref.dtype)


def finish_branch(o_fwd, o_bwd, pb, og_name, gain, *, silu_gate, tm=512):
    m, w = o_fwd.shape
    heads = w // HEAD_V
    cg = PB_COL[og_name] // w
    return pl.pallas_call(
        functools.partial(_finish_kernel, heads=heads, silu_gate=silu_gate),
        out_shape=jax.ShapeDtypeStruct((m, w), BF16),
        grid=(m // tm,),
        in_specs=[
            pl.BlockSpec((tm, w), lambda i: (i, 0)),
            pl.BlockSpec((tm, w), lambda i: (i, 0)),
            pl.BlockSpec((tm, w), lambda i: (i, cg)),
            pl.BlockSpec((1, HEAD_V), lambda i: (0, 0)),
        ],
        out_specs=pl.BlockSpec((tm, w), lambda i: (i, 0)),
        compiler_params=_params("parallel"),
        name="finish_branch",
    )(o_fwd, o_bwd, pb, gain.astype(F32).reshape(1, HEAD_V))


def gla_branch(pb, pf, batch, w_gate_up, b_gate, norm_gain):
    wpad = jnp.zeros((2, LANES, GLA_KEY_WIDTH), F32)
    for d in range(2):
        wpad = wpad.at[d, d * GLA_GATE_RANK:(d + 1) * GLA_GATE_RANK, :].set(w_gate_up[d].astype(F32))
    bias = b_gate.astype(F32).reshape(2, 1, GLA_KEY_WIDTH)
    cols = [(PB_COL["gla_q"] // GLA_KEY_WIDTH,) * 2, (PB_COL["gla_k"] // GLA_KEY_WIDTH,) * 2,
            (PB_COL["gla_v"] // GLA_VAL_WIDTH,) * 2, (PF_SMALL_COL // LANES,) * 2]
    o_f, o_b = _bidir_lin_call("gla_scan", _gla_inputs, [pb, pb, pb, pf], cols,
                               [GLA_KEY_WIDTH, GLA_KEY_WIDTH, GLA_VAL_WIDTH, LANES], [wpad, bias],
                               batch, GLA_HEADS, GLA_KEY_WIDTH)
    return finish_branch(o_f, o_b, pb, "gla_og", norm_gain, silu_gate=True)


def hgrn2_branch(pb, pf, batch, lower_bound, norm_gain):
    lb = lower_bound.astype(F32).reshape(2, 1, HGRN_KEY_WIDTH)
    log_lb = jnp.log(jnp.maximum(lb, LB_FLOOR))
    log1m_lb = jnp.log1p(-lb)
    zc = PF_COL["hg_f"] // HGRN_KEY_WIDTH
    cols = [(PB_COL["hg_q"] // HGRN_KEY_WIDTH,) * 2, (PB_COL["hg_i"] // HGRN_VAL_WIDTH,) * 2, (zc, zc + 1)]
    o_f, o_b = _bidir_lin_call("hgrn2_scan", _hgrn_inputs, [pb, pb, pf], cols,
                               [HGRN_KEY_WIDTH, HGRN_VAL_WIDTH, HGRN_KEY_WIDTH], [lb, log_lb, log1m_lb],
                               batch, HGRN_HEADS, HGRN_KEY_WIDTH)
    return finish_branch(o_f, o_b, pb, "hg_og", norm_gain, silu_gate=False)


GDN_CONV_WIDTH = 5
GDN_QKV_WIDTH = 2 * GDN_KEY_WIDTH + GDN_VAL_WIDTH
GDN_HALO = 16


def _gdn_prep_kernel(prev_ref, cur_ref, next_ref, w_ref, o_ref, xp_ref, *, blocks_per_seq):
    i = pl.program_id(0)
    t = cur_ref.shape[0]
    pos = i % blocks_per_seq
    prev = prev_ref[...].astype(F32)
    nxt = next_ref[...].astype(F32)
    xp_ref[0:GDN_HALO, :] = jnp.where(pos == 0, jnp.zeros_like(prev), prev)
    xp_ref[GDN_HALO:GDN_HALO + t, :] = cur_ref[...].astype(F32)
    xp_ref[GDN_HALO + t:, :] = jnp.where(pos == blocks_per_seq - 1, jnp.zeros_like(nxt), nxt)
    half = GDN_CONV_WIDTH // 2
    for g in range(GDN_QKV_WIDTH // LANES):
        cols = slice(g * LANES, (g + 1) * LANES)
        acc = None
        for j in range(GDN_CONV_WIDTH):
            term = xp_ref[GDN_HALO - half + j:GDN_HALO - half + j + t, cols] * w_ref[j:j + 1, cols]
            acc = term if acc is None else acc + term
        y = acc * _sigmoid(acc)
        if g < 2 * GDN_HEADS:
            y = y * lax.rsqrt(jnp.sum(y * y, axis=-1, keepdims=True) + 1e-6)
            if g < GDN_HEADS:
                y = y * (GDN_HEAD_K ** -0.5)
        o_ref[:, cols] = y.astype(o_ref.dtype)


def gdn_prep(pb, batch, conv_w, *, t=512):
    m = pb.shape[0]
    blocks_per_seq = m // batch // t
    halo_per_block = t // GDN_HALO
    col = PB_COL["gdn_qkv"] // GDN_QKV_WIDTH
    last_halo = m // GDN_HALO - 1
    return pl.pallas_call(
        functools.partial(_gdn_prep_kernel, blocks_per_seq=blocks_per_seq),
        out_shape=jax.ShapeDtypeStruct((m, GDN_QKV_WIDTH), BF16),
        grid=(m // t,),
        in_specs=[
            pl.BlockSpec((GDN_HALO, GDN_QKV_WIDTH), lambda i: (jnp.maximum(i * halo_per_block - 1, 0), col)),
            pl.BlockSpec((t, GDN_QKV_WIDTH), lambda i: (i, col)),
            pl.BlockSpec((GDN_HALO, GDN_QKV_WIDTH),
                         lambda i: (jnp.minimum((i + 1) * halo_per_block, last_halo), col)),
            pl.BlockSpec((GDN_CONV_WIDTH, GDN_QKV_WIDTH), lambda i: (0, 0)),
        ],
        out_specs=pl.BlockSpec((t, GDN_QKV_WIDTH), lambda i: (i, 0)),
        scratch_shapes=[pltpu.VMEM((t + 2 * GDN_HALO, GDN_QKV_WIDTH), F32)],
        compiler_params=_params("parallel"),
        name="gdn_prep",
    )(pb, pb, pb, conv_w.astype(F32))


def _unit_triangular_inverse(a):
    c = a.shape[0]
    ii = lax.broadcasted_iota(jnp.int32, (c, c), 0)
    jj = lax.broadcasted_iota(jnp.int32, (c, c), 1)
    eye = (ii == jj).astype(F32)

    def same_block(s):
        return (ii // s) == (jj // s)

    d = jnp.where(same_block(8), a, 0.0)
    d2 = _dot_f32(d, d)
    d4 = _dot_f32(d2, d2)
    t = _dot_f32(_dot_f32(eye - d, eye + d2), eye + d4)
    s = 8
    while s < c:
        e = jnp.where(same_block(2 * s) & jnp.logical_not(same_block(s)), a, 0.0)
        t = t - _dot_f32(t, _dot_f32(e, t))
        s *= 2
    return t


def _softplus(x):
    return jnp.maximum(x, 0.0) + _log1p_exp_neg(jnp.abs(x))


def _gdn_chunk(qkv, small, a_scale, dt_bias, s_ref, *, direction):
    c = qkv.shape[0]
    reverse = direction == 1
    ii = lax.broadcasted_iota(jnp.int32, (c, c), 0)
    jj = lax.broadcasted_iota(jnp.int32, (c, c), 1)
    incl = (jj >= ii) if reverse else (jj <= ii)
    strict = (jj > ii) if reverse else (jj < ii)
    end = 0 if reverse else c - 1

    log_alpha = a_scale * _softplus(small + dt_bias)
    g_all = _cumsum_rows(incl, log_alpha)
    g_all_t = jnp.concatenate([g_all, jnp.zeros((LANES - c, LANES), F32)], axis=0).T
    beta_all = _sigmoid(small)

    outs = []
    for h in range(GDN_HEADS):
        hs = slice(h * HEAD_V, (h + 1) * HEAD_V)
        q = qkv[:, hs]
        k = qkv[:, GDN_KEY_WIDTH + h * HEAD_V:GDN_KEY_WIDTH + (h + 1) * HEAD_V]
        v = qkv[:, 2 * GDN_KEY_WIDTH + h * HEAD_V:2 * GDN_KEY_WIDTH + (h + 1) * HEAD_V].astype(F32)
        kf = k.astype(F32)
        lane_a = GDN_A_LANE + direction * GDN_HEADS + h
        lane_b = GDN_B_LANE + direction * GDN_HEADS + h
        gc = jnp.broadcast_to(g_all[:, lane_a:lane_a + 1], (c, HEAD_V))
        beta = jnp.broadcast_to(beta_all[:, lane_b:lane_b + 1], (c, HEAD_V))
        diff = gc[:, :c] - jnp.broadcast_to(g_all_t[lane_a:lane_a + 1, :c], (c, c))
        decay = jnp.where(incl, jnp.exp(jnp.where(incl, diff, 0.0)), 0.0)
        k_beta = kf * beta
        kk = lax.dot_general(k_beta.astype(BF16), k, (((1,), (1,)), ((), ())), preferred_element_type=F32)
        t_inv = _unit_triangular_inverse(jnp.where(strict, kk * decay, 0.0))
        eg = jnp.exp(gc)
        sol = _dot_f32(t_inv, jnp.concatenate([v * beta, k_beta * eg], axis=1))
        u, w = sol[:, :HEAD_V], sol[:, HEAD_V:]
        attn = lax.dot_general(q, k, (((1,), (1,)), ((), ())), preferred_element_type=F32) * decay
        g_end = gc[end:end + 1, :]
        q_dec = (q.astype(F32) * eg).astype(BF16)
        k_dec = (kf * jnp.exp(g_end - gc)).astype(BF16)
        state = s_ref[h]
        state_b = state.astype(BF16)
        v_new = u - jnp.dot(w.astype(BF16), state_b, preferred_element_type=F32)
        v_new_b = v_new.astype(BF16)
        outs.append(jnp.dot(q_dec, state_b, preferred_element_type=F32)
                    + jnp.dot(attn.astype(BF16), v_new_b, preferred_element_type=F32))
        s_ref[h] = state * jnp.exp(g_end) + lax.dot_general(k_dec, v_new_b, (((0,), (0,)), ((), ())),
                                                             preferred_element_type=F32)
    return jnp.concatenate(outs, axis=1)


def _gdn_scan_kernel(qf_ref, gf_ref, qb_ref, gb_ref, a_ref, dtb_ref, of_ref, ob_ref, sf_ref, sb_ref):
    @pl.when(pl.program_id(1) == 0)
    def _():
        sf_ref[...] = jnp.zeros_like(sf_ref)
        sb_ref[...] = jnp.zeros_like(sb_ref)

    n_chunks = of_ref.shape[0] // GDN_CHUNK

    def body(c, carry):
        rows = pl.ds(pl.multiple_of(c * GDN_CHUNK, GDN_CHUNK), GDN_CHUNK)
        of_ref[rows, :] = _gdn_chunk(qf_ref[rows, :], gf_ref[rows, :], a_ref[...], dtb_ref[...], sf_ref,
                                     direction=0)
        rows = pl.ds(pl.multiple_of((n_chunks - 1 - c) * GDN_CHUNK, GDN_CHUNK), GDN_CHUNK)
        ob_ref[rows, :] = _gdn_chunk(qb_ref[rows, :], gb_ref[rows, :], a_ref[...], dtb_ref[...], sb_ref,
                                     direction=1)
        return carry

    lax.fori_loop(0, n_chunks, body, 0)


def gated_deltanet_branch(pb, pf, batch, conv_w, a_log, dt_bias, norm_gain):
    m = pb.shape[0]
    nb = m // batch // LIN_BLOCK
    qkv = gdn_prep(pb, batch, conv_w)
    n_gate = 2 * GDN_HEADS
    a_scale = jnp.zeros((1, LANES), F32).at[0, GDN_A_LANE:GDN_A_LANE + n_gate].set(
        -jnp.exp(a_log.astype(F32)).reshape(n_gate))
    dtb = jnp.zeros((1, LANES), F32).at[0, GDN_A_LANE:GDN_A_LANE + n_gate].set(dt_bias.astype(F32).reshape(n_gate))
    small_col = PF_SMALL_COL // LANES

    def fwd(width, col):
        return pl.BlockSpec((LIN_BLOCK, width), lambda b, t: (b * nb + t, col))

    def bwd(width, col):
        return pl.BlockSpec((LIN_BLOCK, width), lambda b, t: (b * nb + nb - 1 - t, col))

    def whole(shape):
        return pl.BlockSpec(shape, functools.partial(lambda b, t, nd: (0,) * nd, nd=len(shape)))

    o_f, o_b = pl.pallas_call(
        _gdn_scan_kernel,
        out_shape=(jax.ShapeDtypeStruct((m, GDN_VAL_WIDTH), F32), jax.ShapeDtypeStruct((m, GDN_VAL_WIDTH), F32)),
        grid=(batch, nb),
        in_specs=[fwd(GDN_QKV_WIDTH, 0), fwd(LANES, small_col), bwd(GDN_QKV_WIDTH, 0), bwd(LANES, small_col),
                  whole((1, LANES)), whole((1, LANES))],
        out_specs=(fwd(GDN_VAL_WIDTH, 0), bwd(GDN_VAL_WIDTH, 0)),
        scratch_shapes=[pltpu.VMEM((GDN_HEADS, GDN_HEAD_K, GDN_HEAD_V), F32),
                        pltpu.VMEM((GDN_HEADS, GDN_HEAD_K, GDN_HEAD_V), F32)],
        compiler_params=_params("parallel", "arbitrary"),
        name="gdn_scan",
    )(qkv, pf, qkv, pf, a_scale, dtb)
    return finish_branch(o_f, o_b, pb, "gdn_og", norm_gain, silu_gate=True)


GDN_PACK = GDN_HEADS * GDN_CHUNK
GDN_WY_BLOCK = 512


def _stack_heads(x, width):
    heads = x.shape[1] // width
    lane = lax.broadcasted_iota(jnp.int32, (1, x.shape[1]), 1)
    return jnp.concatenate(
        [jnp.where((lane >= h * width) & (lane < (h + 1) * width), x, 0.0).astype(BF16) for h in range(heads)],
        axis=0)


def _packed_mm(x, y):
    return jnp.dot(x.astype(BF16), _stack_heads(y, GDN_CHUNK), preferred_element_type=F32)


def _packed_inverses(mats):
    c = GDN_CHUNK
    ii = lax.broadcasted_iota(jnp.int32, (c, GDN_PACK), 0)
    jj = lax.broadcasted_iota(jnp.int32, (c, GDN_PACK), 1) % c
    eye = (ii == jj).astype(F32)

    def same_block(s):
        return (ii // s) == (jj // s)

    ds = [jnp.where(same_block(8), a, 0.0) for a in mats]
    d2s = [_packed_mm(d, d) for d in ds]
    d4s = [_packed_mm(d2, d2) for d2 in d2s]
    ts = [_packed_mm(eye - d, eye + d2) for d, d2 in zip(ds, d2s)]
    ts = [_packed_mm(t, eye + d4) for t, d4 in zip(ts, d4s)]
    s = 8
    while s < c:
        off = same_block(2 * s) & jnp.logical_not(same_block(s))
        ets = [_packed_mm(jnp.where(off, a, 0.0), t) for a, t in zip(mats, ts)]
        ts = [t - _packed_mm(t, et) for t, et in zip(ts, ets)]
        s *= 2
    return ts


def _gdn_wy_kernel(qkv_ref, small_ref, a_ref, dtb_ref, selg_ref, selk_ref, selb_ref, *out_refs):
    c = GDN_CHUNK
    n_chunks = qkv_ref.shape[0] // c
    ii = lax.broadcasted_iota(jnp.int32, (c, c), 0)
    jj = lax.broadcasted_iota(jnp.int32, (c, c), 1)
    pi = lax.broadcasted_iota(jnp.int32, (c, GDN_PACK), 0)
    pj = lax.broadcasted_iota(jnp.int32, (c, GDN_PACK), 1) % c
    eye_p = (pi == pj).astype(F32)
    ones_cc = jnp.ones((c, c), BF16)

    problems = [(ch, d) for ch in range(n_chunks) for d in range(2)]
    chunk_in = []
    for ch in range(n_chunks):
        rows = slice(ch * c, (ch + 1) * c)
        qkv = qkv_ref[rows, :]
        small = small_ref[rows, :]
        kf = qkv[:, GDN_KEY_WIDTH:2 * GDN_KEY_WIDTH].astype(F32)
        chunk_in.append(dict(
            qf=qkv[:, :GDN_KEY_WIDTH].astype(F32), kf=kf, vf=qkv[:, 2 * GDN_KEY_WIDTH:].astype(F32),
            kbd=_stack_heads(kf, HEAD_V),
            log_alpha=a_ref[...] * _softplus(small + dtb_ref[...]),
            beta_all=_sigmoid(small)))

    def sel3(x, sel):
        hi, mid, lo = _split_bf16(x, 3)
        return (jnp.dot(hi, sel, preferred_element_type=F32)
                + (jnp.dot(mid, sel, preferred_element_type=F32) + jnp.dot(lo, sel, preferred_element_type=F32)))

    g_all = [_cumsum_rows((jj >= ii) if d else (jj <= ii), chunk_in[ch]["log_alpha"]) for ch, d in problems]
    g_pack = [sel3(g, selg_ref[d]) for g, (ch, d) in zip(g_all, problems)]
    g_wide = [sel3(g, selk_ref[d]) for g, (ch, d) in zip(g_all, problems)]
    beta_w = [sel3(chunk_in[ch]["beta_all"], selb_ref[d]) for ch, d in problems]
    g_rowp = []
    for gp in g_pack:
        hi, mid, lo = _split_bf16(gp * eye_p, 3)
        g_rowp.append(jnp.dot(ones_cc, hi, preferred_element_type=F32)
                      + (jnp.dot(ones_cc, mid, preferred_element_type=F32)
                         + jnp.dot(ones_cc, lo, preferred_element_type=F32)))
    decays, k_betas = [], []
    for gp, gr, bw, (ch, d) in zip(g_pack, g_rowp, beta_w, problems):
        incl = (pj >= pi) if d else (pj <= pi)
        decays.append(jnp.where(incl, jnp.exp(jnp.where(incl, gp - gr, 0.0)), 0.0))
        k_betas.append(chunk_in[ch]["kf"] * bw)
    kq = [lax.dot_general(jnp.concatenate([kb, chunk_in[ch]["qf"]], axis=0).astype(BF16), chunk_in[ch]["kbd"],
                          (((1,), (1,)), ((), ())), preferred_element_type=F32)
          for kb, (ch, d) in zip(k_betas, problems)]
    a_mats = []
    for x, dec, (ch, d) in zip(kq, decays, problems):
        strict = (pj > pi) if d else (pj < pi)
        a_mats.append(jnp.where(strict, x[:c] * dec, 0.0))
    t_invs = _packed_inverses(a_mats)

    for idx, (ch, d) in enumerate(problems):
        u_ref, w_ref, attn_ref, qd_ref, kd_ref, gt_ref = out_refs[6 * d:6 * d + 6]
        rows = slice(ch * c, (ch + 1) * c)
        cin = chunk_in[ch]
        gw = g_wide[idx]
        eg = jnp.exp(gw)
        t_b = t_invs[idx].astype(BF16)
        u_ref[rows, :] = jnp.dot(t_b, _stack_heads(cin["vf"] * beta_w[idx], HEAD_V), preferred_element_type=F32)
        w_ref[rows, :] = jnp.dot(t_b, _stack_heads(k_betas[idx] * eg, HEAD_V),
                                 preferred_element_type=F32).astype(w_ref.dtype)
        attn_ref[rows, :] = (kq[idx][c:] * decays[idx]).astype(attn_ref.dtype)
        end = 0 if d else c - 1
        g_end = gw[end:end + 1, :]
        qd_ref[rows, :] = (cin["qf"] * eg).astype(qd_ref.dtype)
        kd_ref[rows, :] = (cin["kf"] * jnp.exp(g_end - gw)).astype(kd_ref.dtype)
        gt_ref[ch:ch + 1, :] = jnp.exp(g_end)


def gdn_wy(qkv, pf, a_scale, dtb):
    m = qkv.shape[0]
    t = GDN_WY_BLOCK
    cpb = t // GDN_CHUNK
    selg = np.zeros((2, LANES, GDN_PACK), np.float32)
    selk = np.zeros((2, LANES, GDN_VAL_WIDTH), np.float32)
    selb = np.zeros((2, LANES, GDN_VAL_WIDTH), np.float32)
    for d in range(2):
        for h in range(GDN_HEADS):
            selg[d, GDN_A_LANE + d * GDN_HEADS + h, h * GDN_CHUNK:(h + 1) * GDN_CHUNK] = 1.0
            selk[d, GDN_A_LANE + d * GDN_HEADS + h, h * HEAD_V:(h + 1) * HEAD_V] = 1.0
            selb[d, GDN_B_LANE + d * GDN_HEADS + h, h * HEAD_V:(h + 1) * HEAD_V] = 1.0
    wide = GDN_VAL_WIDTH
    out_shape, out_specs = [], []
    for _ in range(2):
        for width, dt in ((wide, F32), (wide, BF16), (GDN_PACK, BF16), (wide, BF16), (wide, BF16)):
            out_shape.append(jax.ShapeDtypeStruct((m, width), dt))
            out_specs.append(pl.BlockSpec((t, width), lambda i: (i, 0)))
        out_shape.append(jax.ShapeDtypeStruct((m // GDN_CHUNK, wide), F32))
        out_specs.append(pl.BlockSpec((cpb, wide), lambda i: (i, 0)))
    return pl.pallas_call(
        _gdn_wy_kernel,
        out_shape=tuple(out_shape),
        grid=(m // t,),
        in_specs=[
            pl.BlockSpec((t, GDN_QKV_WIDTH), lambda i: (i, 0)),
            pl.BlockSpec((t, LANES), lambda i: (i, PF_SMALL_COL // LANES)),
            pl.BlockSpec((1, LANES), lambda i: (0, 0)),
            pl.BlockSpec((1, LANES), lambda i: (0, 0)),
            pl.BlockSpec((2, LANES, GDN_PACK), lambda i: (0, 0, 0)),
            pl.BlockSpec((2, LANES, wide), lambda i: (0, 0, 0)),
            pl.BlockSpec((2, LANES, wide), lambda i: (0, 0, 0)),
        ],
        out_specs=tuple(out_specs),
        compiler_params=_params("parallel"),
        name="gdn_wy",
    )(qkv, pf, a_scale, dtb, jnp.asarray(selg, BF16), jnp.asarray(selk, BF16), jnp.asarray(selb, BF16))


GDN_PAIR = 2 * HEAD_V


def _gdn_scan2_kernel(*refs):
    groups = (refs[0:6], refs[6:12])
    out_refs = refs[12:14]
    state_refs = refs[14:16]

    @pl.when(pl.program_id(1) == 0)
    def _():
        for s_ref in state_refs:
            s_ref[...] = jnp.zeros_like(s_ref)

    n_chunks = out_refs[0].shape[0] // GDN_CHUNK
    pairs = GDN_HEADS // 2
    pair_cols = [slice(p * GDN_PAIR, (p + 1) * GDN_PAIR) for p in range(pairs)]
    ri = lax.broadcasted_iota(jnp.int32, (GDN_PAIR, GDN_PAIR), 0) // HEAD_V
    ci = lax.broadcasted_iota(jnp.int32, (GDN_PAIR, GDN_PAIR), 1) // HEAD_V
    diag = ri == ci

    def body(c, carry):
        chunks = (c, n_chunks - 1 - c)
        rows = [pl.ds(pl.multiple_of(ch * GDN_CHUNK, GDN_CHUNK), GDN_CHUNK) for ch in chunks]
        states = [[s_ref[p] for p in range(pairs)] for s_ref in state_refs]
        states_b = [[s.astype(BF16) for s in st] for st in states]
        ws = [[jnp.dot(groups[g][1][rows[g], cols], states_b[g][p], preferred_element_type=F32)
               for p, cols in enumerate(pair_cols)] for g in range(2)]
        qs = [[jnp.dot(groups[g][3][rows[g], cols], states_b[g][p], preferred_element_type=F32)
               for p, cols in enumerate(pair_cols)] for g in range(2)]
        v_new = [groups[g][0][rows[g], :] - jnp.concatenate(ws[g], axis=1) for g in range(2)]
        av = [jnp.dot(groups[g][2][rows[g], :], _stack_heads(v_new[g], HEAD_V), preferred_element_type=F32)
              for g in range(2)]
        v_new_b = [v.astype(BF16) for v in v_new]
        upd = [[lax.dot_general(groups[g][4][rows[g], cols], v_new_b[g][:, cols], (((0,), (0,)), ((), ())),
                                preferred_element_type=F32) for cols in pair_cols] for g in range(2)]
        for g in range(2):
            out_refs[g][rows[g], :] = jnp.concatenate(qs[g], axis=1) + av[g]
            gt = groups[g][5][pl.ds(chunks[g], 1), :]
            for p, cols in enumerate(pair_cols):
                state_refs[g][p] = states[g][p] * gt[:, cols] + jnp.where(diag, upd[g][p], 0.0)
        return carry

    lax.fori_loop(0, n_chunks, body, 0)


def gated_deltanet_branch(pb, pf, batch, conv_w, a_log, dt_bias, norm_gain):
    m = pb.shape[0]
    nb = m // batch // LIN_BLOCK
    cpb = LIN_BLOCK // GDN_CHUNK
    qkv = gdn_prep(pb, batch, conv_w)
    n_gate = 2 * GDN_HEADS
    a_scale = jnp.zeros((1, LANES), F32).at[0, GDN_A_LANE:GDN_A_LANE + n_gate].set(
        -jnp.exp(a_log.astype(F32)).reshape(n_gate))
    dtb = jnp.zeros((1, LANES), F32).at[0, GDN_A_LANE:GDN_A_LANE + n_gate].set(dt_bias.astype(F32).reshape(n_gate))
    wy = gdn_wy(qkv, pf, a_scale, dtb)
    widths = (GDN_VAL_WIDTH, GDN_VAL_WIDTH, GDN_PACK, GDN_VAL_WIDTH, GDN_VAL_WIDTH)

    def fwd(rows, width):
        return pl.BlockSpec((rows, width), lambda b, t: (b * nb + t, 0))

    def bwd(rows, width):
        return pl.BlockSpec((rows, width), lambda b, t: (b * nb + nb - 1 - t, 0))

    in_specs = [fwd(LIN_BLOCK, w) for w in widths] + [fwd(cpb, GDN_VAL_WIDTH)]
    in_specs += [bwd(LIN_BLOCK, w) for w in widths] + [bwd(cpb, GDN_VAL_WIDTH)]
    state = pltpu.VMEM((GDN_HEADS // 2, GDN_PAIR, GDN_PAIR), F32)
    o_f, o_b = pl.pallas_call(
        _gdn_scan2_kernel,
        out_shape=(jax.ShapeDtypeStruct((m, GDN_VAL_WIDTH), F32), jax.ShapeDtypeStruct((m, GDN_VAL_WIDTH), F32)),
        grid=(batch, nb),
        in_specs=in_specs,
        out_specs=(fwd(LIN_BLOCK, GDN_VAL_WIDTH), bwd(LIN_BLOCK, GDN_VAL_WIDTH)),
        scratch_shapes=[state, state],
        compiler_params=_params("parallel", "arbitrary"),
        name="gdn_scan",
    )(*wy)
    return RawBranch(o_f, o_b, "gdn_og", norm_gain, True)


def _dot3(m, x):
    hi, mid, lo = _split_bf16(x, 3)
    return (jnp.dot(m, hi, preferred_element_type=F32)
            + (jnp.dot(m, mid, preferred_element_type=F32) + jnp.dot(m, lo, preferred_element_type=F32)))


def _lin_masks(block, chunk):
    i = np.arange(block)[:, None]
    j = np.arange(block)[None, :]
    same = (i // chunk) == (j // chunk)
    big = np.zeros((2, 3, block, block), np.float32)
    ends = np.zeros((2, block // chunk, block), np.float32)
    for d in range(2):
        mid = chunk - 1 - chunk // 2 if d else chunk // 2
        end = 0 if d else chunk - 1
        big[d, 0] = same & ((j >= i) if d else (j <= i))
        big[d, 1] = j == (i // chunk) * chunk + mid
        big[d, 2] = j == (i // chunk) * chunk + end
        ends[d] = j == np.arange(block // chunk)[:, None] * chunk + end
    return jnp.asarray(big, BF16), jnp.asarray(ends, BF16)


LIN_CUM_ROWS = 256
LIN_SCORE_ROWS = 128


def _chunk_causal(n, chunk, reverse):
    i = lax.broadcasted_iota(jnp.int32, (n, n), 0)
    j = lax.broadcasted_iota(jnp.int32, (n, n), 1)
    return ((i // chunk) == (j // chunk)) & ((j >= i) if reverse else (j <= i))


def _lin_intra_kernel(*refs, load_inputs, n_in, n_params, heads):
    dir_refs = (refs[:n_in], refs[n_in:2 * n_in])
    params = refs[2 * n_in:2 * n_in + n_params]
    out_refs = refs[2 * n_in + n_params:]
    c = LIN_CHUNK
    dirs = (0, 1)
    loaded = [load_inputs(dir_refs[d], slice(None), d, params) for d in dirs]
    t, w = loaded[0][0].shape
    dk = w // heads
    nc = t // c
    cums = [jnp.where(_chunk_causal(LIN_CUM_ROWS, c, d == 1), 1.0, 0.0).astype(BF16) for d in dirs]
    bs = [jnp.concatenate([_dot3(cums[d], loaded[d][3][r:r + LIN_CUM_ROWS, :])
                           for r in range(0, t, LIN_CUM_ROWS)], axis=0) for d in dirs]
    qes, kes = [], []
    for d in dirs:
        oi_ref, qd_ref, kd_ref, gt_ref = out_refs[4 * d:4 * d + 4]
        qc, kc, vc, lg = loaded[d]
        b = bs[d]
        b3 = b.reshape(nc, c, w)
        mid = c - 1 - c // 2 if d else c // 2
        end = 0 if d else c - 1
        b_mid = jnp.broadcast_to(b3[:, mid:mid + 1, :], (nc, c, w)).reshape(t, w)
        b_end = jnp.broadcast_to(b3[:, end:end + 1, :], (nc, c, w)).reshape(t, w)
        qes.append((qc * jnp.exp(b - b_mid)).astype(BF16))
        kes.append((kc * jnp.exp(b_mid - b)).astype(BF16))
        qd_ref[...] = (qc * jnp.exp(b)).astype(qd_ref.dtype)
        kd_ref[...] = (kc * jnp.exp(b_end - b)).astype(kd_ref.dtype)
        gt_ref[...] = jnp.exp(b3[:, end, :])
    keeps = [_chunk_causal(LIN_SCORE_ROWS, c, d == 1) for d in dirs]
    lane = lax.broadcasted_iota(jnp.int32, (1, LANES), 1)
    for h in range(heads):
        win = slice((h * dk) // LANES * LANES, (h * dk) // LANES * LANES + LANES)
        lo = h * dk - win.start
        vcols = slice(h * HEAD_V, (h + 1) * HEAD_V)
        tiles = [(slice(r, r + LIN_SCORE_ROWS), d) for r in range(0, t, LIN_SCORE_ROWS) for d in dirs]
        scores = []
        for rows, d in tiles:
            qh = qes[d][rows, win]
            if dk < LANES:
                qh = jnp.where((lane >= lo) & (lane < lo + dk), qh, jnp.zeros_like(qh))
            scores.append(lax.dot_general(qh, kes[d][rows, win], (((1,), (1,)), ((), ())),
                                          preferred_element_type=F32))
        probs = [jnp.where(keeps[d], s, 0.0).astype(BF16) for s, (rows, d) in zip(scores, tiles)]
        for p, (rows, d) in zip(probs, tiles):
            out_refs[4 * d][rows, vcols] = jnp.dot(p, loaded[d][2][rows, vcols], preferred_element_type=F32)


def _lin_scan_kernel(*refs, heads, chunk, unroll):
    groups = (refs[0:5], refs[5:10])
    out_refs = refs[10:12]
    state_refs = refs[12:14]

    @pl.when(pl.program_id(1) == 0)
    def _():
        for s_ref in state_refs:
            s_ref[...] = jnp.zeros_like(s_ref)

    n_chunks = out_refs[0].shape[0] // chunk
    w = state_refs[0].shape[1]
    dk = w // heads
    lane = lax.broadcasted_iota(jnp.int32, (1, w), 1)
    masks = [(lane >= h * dk) & (lane < (h + 1) * dk) for h in range(heads)]

    def stack(x):
        return jnp.concatenate([jnp.where(m, x, jnp.zeros_like(x)) for m in masks], axis=0)

    def body(it, carry):
        steps = []
        for u in range(unroll):
            c = it * unroll + u
            steps += [(0, c), (1, n_chunks - 1 - c)]
        prepared = []
        for g, ch in steps:
            rows = pl.ds(pl.multiple_of(ch * chunk, chunk), chunk)
            oi_ref, qd_ref, kd_ref, v_ref, gt_ref = groups[g]
            vc = v_ref[rows, :]
            v4 = jnp.concatenate([vc[:, h * HEAD_V:(h + 1) * HEAD_V] for h in range(heads)], axis=0)
            upd = lax.dot_general(v4, stack(kd_ref[rows, :]), (((0,), (0,)), ((), ())),
                                  preferred_element_type=F32)
            prepared.append((rows, stack(qd_ref[rows, :]), upd, gt_ref[pl.ds(ch, 1), :]))
        states = [s_ref[...] for s_ref in state_refs]
        for (g, ch), (rows, q4, upd, gt) in zip(steps, prepared):
            o_inter = lax.dot_general(q4, states[g].astype(BF16), (((1,), (1,)), ((), ())),
                                      preferred_element_type=F32)
            out_refs[g][rows, :] = groups[g][0][rows, :] + jnp.concatenate(
                [o_inter[h * chunk:(h + 1) * chunk, :] for h in range(heads)], axis=1)
            states[g] = states[g] * gt + upd
        for s_ref, st in zip(state_refs, states):
            s_ref[...] = st
        return carry

    lax.fori_loop(0, n_chunks // unroll, body, 0)


def _bidir_lin_call(name, load_inputs, arrays, col_blocks, widths, params, batch, heads, key_width, v_col):
    m = arrays[0].shape[0]
    t = LIN_BLOCK
    nb = m // batch // t
    cpb = t // LIN_CHUNK
    out_w = heads * HEAD_V
    n_in = len(arrays)

    in_specs, operands = [], []
    for d in range(2):
        for a, wd, cb in zip(arrays, widths, col_blocks):
            in_specs.append(pl.BlockSpec((t, wd), functools.partial(lambda i, c: (i, c), c=cb[d])))
            operands.append(a)
    for p in params:
        in_specs.append(pl.BlockSpec(p.shape, functools.partial(lambda i, nd: (0,) * nd, nd=p.ndim)))
    out_shape, out_specs = [], []
    for _ in range(2):
        for rows_total, rows_blk, width, dt in ((m, t, out_w, F32), (m, t, key_width, BF16),
                                                (m, t, key_width, BF16), (m // LIN_CHUNK, cpb, key_width, F32)):
            out_shape.append(jax.ShapeDtypeStruct((rows_total, width), dt))
            out_specs.append(pl.BlockSpec((rows_blk, width), lambda i: (i, 0)))

    intra = pl.pallas_call(
        functools.partial(_lin_intra_kernel, load_inputs=load_inputs, n_in=n_in, n_params=len(params),
                          heads=heads),
        out_shape=tuple(out_shape),
        grid=(m // t,),
        in_specs=in_specs,
        out_specs=tuple(out_specs),
        compiler_params=_params("parallel"),
        name=name + "_intra",
    )(*operands, *params)

    def fwd(rows, width, col=0):
        return pl.BlockSpec((rows, width), lambda b, s: (b * nb + s, col))

    def bwd(rows, width, col=0):
        return pl.BlockSpec((rows, width), lambda b, s: (b * nb + nb - 1 - s, col))

    scan_specs, scan_ops = [], []
    for d, mk in enumerate((fwd, bwd)):
        oi, qd, kd, gt = intra[4 * d:4 * d + 4]
        scan_specs += [mk(t, out_w), mk(t, key_width), mk(t, key_width), mk(t, out_w, v_col), mk(cpb, key_width)]
        scan_ops += [oi, qd, kd, arrays[0], gt]
    state = pltpu.VMEM((HEAD_V, key_width), F32)
    return pl.pallas_call(
        functools.partial(_lin_scan_kernel, heads=heads, chunk=LIN_CHUNK, unroll=4),
        out_shape=(jax.ShapeDtypeStruct((m, out_w), F32), jax.ShapeDtypeStruct((m, out_w), F32)),
        grid=(batch, nb),
        in_specs=scan_specs,
        out_specs=(fwd(t, out_w), bwd(t, out_w)),
        scratch_shapes=[state, state],
        compiler_params=_params("parallel", "arbitrary"),
        name=name + "_scan",
    )(*scan_ops)


def gla_branch(pb, pf, batch, w_gate_up, b_gate, norm_gain):
    wpad = jnp.zeros((2, LANES, GLA_KEY_WIDTH), F32)
    for d in range(2):
        wpad = wpad.at[d, d * GLA_GATE_RANK:(d + 1) * GLA_GATE_RANK, :].set(w_gate_up[d].astype(F32))
    bias = b_gate.astype(F32).reshape(2, 1, GLA_KEY_WIDTH)
    v_col = PB_COL["gla_v"] // GLA_VAL_WIDTH
    cols = [(PB_COL["gla_q"] // GLA_KEY_WIDTH,) * 2, (PB_COL["gla_k"] // GLA_KEY_WIDTH,) * 2,
            (v_col,) * 2, (PF_SMALL_COL // LANES,) * 2]
    o_f, o_b = _bidir_lin_call("gla", _gla_inputs, [pb, pb, pb, pf], cols,
                               [GLA_KEY_WIDTH, GLA_KEY_WIDTH, GLA_VAL_WIDTH, LANES], [wpad, bias],
                               batch, GLA_HEADS, GLA_KEY_WIDTH, v_col)
    return RawBranch(o_f, o_b, "gla_og", norm_gain, True)


def hgrn2_branch(pb, pf, batch, lower_bound, norm_gain):
    lb = lower_bound.astype(F32).reshape(2, 1, HGRN_KEY_WIDTH)
    log_lb = jnp.log(jnp.maximum(lb, LB_FLOOR))
    log1m_lb = jnp.log1p(-lb)
    zc = PF_COL["hg_f"] // HGRN_KEY_WIDTH
    v_col = PB_COL["hg_i"] // HGRN_VAL_WIDTH
    cols = [(PB_COL["hg_q"] // HGRN_KEY_WIDTH,) * 2, (v_col,) * 2, (zc, zc + 1)]
    o_f, o_b = _bidir_lin_call("hgrn2", _hgrn_inputs, [pb, pb, pf], cols,
                               [HGRN_KEY_WIDTH, HGRN_VAL_WIDTH, HGRN_KEY_WIDTH], [lb, log_lb, log1m_lb],
                               batch, HGRN_HEADS, HGRN_KEY_WIDTH, v_col)
    return RawBranch(o_f, o_b, "hg_og", norm_gain, False)


def _rms_norm(x, gain, eps=RMS_EPS):
    xf = x.astype(F32)
    y = xf * lax.rsqrt(jnp.mean(xf * xf, axis=-1, keepdims=True) + eps)
    return (y * gain.astype(F32)).astype(x.dtype)


def _l2_norm(x, eps=1e-6):
    xf = x.astype(F32)
    return xf * lax.rsqrt(jnp.sum(xf * xf, axis=-1, keepdims=True) + eps)


def _rev(t):
    return jnp.flip(t, axis=1)


def _centred_depthwise_conv(x, w):
    width = w.shape[0]
    return lax.conv_general_dilated(
        x, w[:, None, :], window_strides=(1,), padding=[(width // 2, width // 2)],
        dimension_numbers=("NWC", "WIO", "NWC"), feature_group_count=x.shape[-1])


def _chunk_gla(q, k, v, log_g):
    B, S, H, K = q.shape
    V = v.shape[-1]
    C = LIN_CHUNK
    n = S // C

    def chunks(t):
        return t.reshape(B, n, C, H, t.shape[-1]).transpose(1, 0, 3, 2, 4)

    q, k, v, log_g = chunks(q), chunks(k), chunks(v), chunks(log_g)
    b = jnp.cumsum(log_g, axis=-2)
    b_ref = b[..., C // 2:C // 2 + 1, :]
    incl = jnp.tril(jnp.ones((C, C), dtype=bool))
    scores = jnp.einsum("nbhik,nbhjk->nbhij", q * jnp.exp(b - b_ref), k * jnp.exp(b_ref - b))
    o_intra = jnp.einsum("nbhij,nbhjv->nbhiv", jnp.where(incl, scores, 0.0), v)
    q_dec = q * jnp.exp(b)
    k_dec = k * jnp.exp(b[..., -1:, :] - b)
    g_tot = jnp.exp(b[..., -1, :])

    def step(state, xs):
        q_c, k_c, v_c, g_c = xs
        o_c = jnp.einsum("bhik,bhkv->bhiv", q_c, state)
        state = state * g_c[..., None] + jnp.einsum("bhjk,bhjv->bhkv", k_c, v_c)
        return state, o_c

    _, o_inter = lax.scan(step, jnp.zeros((B, H, K, V), F32), (q_dec, k_dec, v, g_tot))
    o = o_intra + o_inter
    return o.transpose(1, 0, 3, 2, 4).reshape(B, S, H, V)


def _chunk_gdn(q, k, v, log_alpha, beta):
    B, S, H, K = q.shape
    V = v.shape[-1]
    C = GDN_CHUNK
    n = S // C

    def chunks(t):
        return t.reshape(B, n, C, H, t.shape[-1]).transpose(1, 0, 3, 2, 4)

    q, k, v = chunks(q), chunks(k), chunks(v)
    g = jnp.cumsum(chunks(log_alpha[..., None])[..., 0], axis=-1)
    beta = chunks(beta[..., None])
    incl = jnp.tril(jnp.ones((C, C), dtype=bool))
    strict = jnp.tril(jnp.ones((C, C), dtype=bool), -1)
    diff = g[..., :, None] - g[..., None, :]
    decay = jnp.where(incl, jnp.exp(jnp.where(incl, diff, 0.0)), 0.0)
    k_beta = k * beta
    a = jnp.where(strict, jnp.einsum("nbhik,nbhjk->nbhij", k_beta, k) * decay, 0.0)
    rhs = jnp.concatenate([v * beta, k_beta * jnp.exp(g)[..., None]], axis=-1)
    sol = lax.linalg.triangular_solve(a + jnp.eye(C, dtype=F32), rhs, left_side=True, lower=True)
    u, w = sol[..., :V], sol[..., V:]
    attn = jnp.einsum("nbhik,nbhjk->nbhij", q, k) * decay
    q_dec = q * jnp.exp(g)[..., None]
    k_dec = k * jnp.exp(g[..., -1:] - g)[..., None]
    g_tot = jnp.exp(g[..., -1])

    def step(state, xs):
        u_c, w_c, attn_c, q_c, k_c, g_c = xs
        v_new = u_c - jnp.einsum("bhck,bhkv->bhcv", w_c, state)
        o_c = jnp.einsum("bhck,bhkv->bhcv", q_c, state) + jnp.einsum("bhij,bhjv->bhiv", attn_c, v_new)
        state = state * g_c[..., None, None] + jnp.einsum("bhck,bhcv->bhkv", k_c, v_new)
        return state, o_c

    _, o = lax.scan(step, jnp.zeros((B, H, K, V), F32), (u, w, attn, q_dec, k_dec, g_tot))
    return o.transpose(1, 0, 3, 2, 4).reshape(B, S, H, V)


def _neighbourhood_attention(q, k, v, q_gain, k_gain, rel_bias):
    B, S, _ = q.shape
    rows = S // GRID_W
    win_rows = min(NA_WIN_ROWS, rows)

    def grid(t):
        return t.reshape(B, rows, GRID_W, NA_HEADS, NA_HEAD_DIM)

    q = _rms_norm(grid(q), q_gain).astype(F32) * (NA_HEAD_DIM ** -0.5)
    k = _rms_norm(grid(k), k_gain).astype(F32)
    v = grid(v).astype(F32)
    r = jnp.arange(rows)
    c = jnp.arange(GRID_W)
    row_idx = jnp.clip(r - win_rows // 2, 0, rows - win_rows)[:, None] + jnp.arange(win_rows)[None, :]
    col_start = jnp.clip(c - NA_WIN_COLS // 2, 0, GRID_W - NA_WIN_COLS)
    col_in = (c[None, :] >= col_start[:, None]) & (c[None, :] < col_start[:, None] + NA_WIN_COLS)
    k_band = k[:, row_idx]
    v_band = v[:, row_idx]
    s = jnp.einsum("brqhd,brikhd->bhrqik", q, k_band)
    dr = row_idx - r[:, None] + (NA_WIN_ROWS - 1)
    dc = jnp.clip(c[None, :] - c[:, None], 1 - NA_WIN_COLS, NA_WIN_COLS - 1) + (NA_WIN_COLS - 1)
    bias = rel_bias.astype(F32)[:, dr[:, None, :, None], dc[None, :, None, :]]
    s = jnp.where(col_in[:, None, :], s + bias[None], MASK_VALUE)
    p = jax.nn.softmax(s, axis=(-2, -1))
    o = jnp.einsum("bhrqik,brikhd->brqhd", p, v_band)
    return o.reshape(B, S, NA_WIDTH)


def _gla_branch(q, k, v, gate_lr, out_gate, w_gate_up, b_gate, norm_gain):
    B, S, _ = q.shape
    q = q.astype(F32).reshape(B, S, GLA_HEADS, GLA_HEAD_K) * (GLA_HEAD_K ** -0.5)
    k = k.astype(F32).reshape(B, S, GLA_HEADS, GLA_HEAD_K)
    v = v.astype(F32).reshape(B, S, GLA_HEADS, GLA_HEAD_V)
    lr = gate_lr.astype(F32).reshape(B, S, 2, GLA_GATE_RANK)
    gk = jnp.einsum("bsdr,drk->bsdk", lr, w_gate_up.astype(F32)) + b_gate.astype(F32)
    log_g = (jax.nn.log_sigmoid(gk) / GLA_GATE_NORMALIZER).reshape(B, S, 2, GLA_HEADS, GLA_HEAD_K)
    o = (_chunk_gla(q, k, v, log_g[:, :, 0])
         + _rev(_chunk_gla(_rev(q), _rev(k), _rev(v), _rev(log_g[:, :, 1]))))
    o = _rms_norm(o, norm_gain) * jax.nn.silu(out_gate.astype(F32)).reshape(B, S, GLA_HEADS, GLA_HEAD_V)
    return o.reshape(B, S, GLA_VAL_WIDTH)


def _gdn_branch(qkv, a, b, out_gate, conv_w, a_log, dt_bias, norm_gain):
    B, S, _ = qkv.shape
    qkv = jax.nn.silu(_centred_depthwise_conv(qkv.astype(F32), conv_w.astype(F32)))
    q, k, v = jnp.split(qkv, [GDN_KEY_WIDTH, 2 * GDN_KEY_WIDTH], axis=-1)
    q = _l2_norm(q.reshape(B, S, GDN_HEADS, GDN_HEAD_K)) * (GDN_HEAD_K ** -0.5)
    k = _l2_norm(k.reshape(B, S, GDN_HEADS, GDN_HEAD_K))
    v = v.reshape(B, S, GDN_HEADS, GDN_HEAD_V)
    a = a.astype(F32).reshape(B, S, 2, GDN_HEADS)
    b = b.astype(F32).reshape(B, S, 2, GDN_HEADS)
    log_alpha = -jnp.exp(a_log.astype(F32)) * jax.nn.softplus(a + dt_bias.astype(F32))
    beta = _sigmoid(b)
    o = (_chunk_gdn(q, k, v, log_alpha[:, :, 0], beta[:, :, 0])
         + _rev(_chunk_gdn(_rev(q), _rev(k), _rev(v), _rev(log_alpha[:, :, 1]), _rev(beta[:, :, 1]))))
    o = _rms_norm(o, norm_gain) * jax.nn.silu(out_gate.astype(F32)).reshape(B, S, GDN_HEADS, GDN_HEAD_V)
    return o.reshape(B, S, GDN_VAL_WIDTH)


def _hgrn2_branch(q, f_pre, i, out_gate, lower_bound, norm_gain):
    B, S, _ = q.shape
    q = jax.nn.silu(q.astype(F32)).reshape(B, S, HGRN_HEADS, HGRN_HEAD_K)
    z = f_pre.astype(F32).reshape(B, S, 2, HGRN_KEY_WIDTH)
    lb = lower_bound.astype(F32)
    log_f = jnp.logaddexp(jnp.log(jnp.maximum(lb, LB_FLOOR)), jnp.log1p(-lb) + jax.nn.log_sigmoid(z))
    k_in = (1.0 - lb) * _sigmoid(-z)
    log_f = log_f.reshape(B, S, 2, HGRN_HEADS, HGRN_HEAD_K)
    k_in = k_in.reshape(B, S, 2, HGRN_HEADS, HGRN_HEAD_K)
    v = i.astype(F32).reshape(B, S, HGRN_HEADS, HGRN_HEAD_V)
    o = (_chunk_gla(q, k_in[:, :, 0], v, log_f[:, :, 0])
         + _rev(_chunk_gla(_rev(q), _rev(k_in[:, :, 1]), _rev(v), _rev(log_f[:, :, 1]))))
    o = _rms_norm(o, norm_gain) * _sigmoid(out_gate.astype(F32)).reshape(B, S, HGRN_HEADS, HGRN_HEAD_V)
    return o.reshape(B, S, HGRN_VAL_WIDTH)


def _memory_cross_attention(q, kv, q_gain, k_gain):
    B, S, _ = q.shape
    M = kv.shape[1]
    q = _rms_norm(q.reshape(B, S, MEM_HEADS, MEM_HEAD_DIM), q_gain).astype(F32)
    k, v = jnp.split(kv, 2, axis=-1)
    k = _rms_norm(k.reshape(B, M, MEM_HEADS, MEM_HEAD_DIM), k_gain).astype(F32)
    v = v.reshape(B, M, MEM_HEADS, MEM_HEAD_DIM).astype(F32)
    s = jnp.einsum("bshd,bmhd->bhsm", q, k) * (MEM_HEAD_DIM ** -0.5)
    p = jax.nn.softmax(s, axis=-1)
    o = jnp.einsum("bhsm,bmhd->bshd", p, v)
    return o.reshape(B, S, MEM_WIDTH)


def kernel(x, mem, g_mix, w_in, na_q_gain, na_k_gain, na_rel_bias, gla_w_gate_up, gla_b_gate, gla_norm_gain, gdn_conv_w, gdn_a_log, gdn_dt_bias, gdn_norm_gain, hgrn_lb_raw, hgrn_norm_gain, g_mem, w_mem_kv, mem_q_gain, mem_k_gain, w_branch, w_out, g_ffn, ffn_w_gate, ffn_w_up, ffn_w_down, moe_w_router, moe_b_router, moe_w_gate, moe_w_up, moe_w_down):
    B, S, D = x.shape
    n_tok = B * S
    lb_w = jax.nn.softmax(hgrn_lb_raw.astype(F32), axis=0)
    hgrn_lb = jnp.cumsum(lb_w, axis=0) - lb_w[0:1]
    x2 = x.reshape(n_tok, D)
    mem2 = mem.reshape(B * mem.shape[1], D)
    for layer in range(DEPTH):
        wb, wf = _split_w_in(w_in[layer])
        pb = rms_matmul(x2, g_mix[layer], wb, tm=1024, tn=512, out_dtype=BF16)
        pf = rms_matmul(x2, g_mix[layer], wf, tm=1024, tn=PF_WIDTH // 3, out_dtype=F32)
        kv = rms_matmul(mem2, g_mem[layer], w_mem_kv[layer].astype(BF16), tm=mem2.shape[0], tn=512,
                        out_dtype=BF16)
        branches = [
            neighbourhood_attention(pb, B, na_q_gain[layer], na_k_gain[layer], na_rel_bias[layer]),
            gla_branch(pb, pf, B, gla_w_gate_up[layer], gla_b_gate[layer], gla_norm_gain[layer]),
            gated_deltanet_branch(pb, pf, B, gdn_conv_w[layer], gdn_a_log[layer], gdn_dt_bias[layer],
                                  gdn_norm_gain[layer]),
            hgrn2_branch(pb, pf, B, hgrn_lb[layer], hgrn_norm_gain[layer]),
            memory_cross_attention(pb, kv, B, mem_q_gain[layer], mem_k_gain[layer]),
        ]
        merged = merge_branches(branches, pb, w_branch[layer].astype(BF16), tm=512, tn=512)
        x2 = matmul_residual(merged, w_out[layer].astype(BF16), x2, tm=1024, tn=512)

        j = layer // 2
        if layer % 2 == 0:
            act = rms_swiglu_up(x2, g_ffn[layer], ffn_w_gate[j].astype(BF16), ffn_w_up[j].astype(BF16),
                                tm=1024, tn=512)
            x2 = matmul_residual(act, ffn_w_down[j].astype(BF16), x2, tm=512, tn=512)
        else:
            x2 = moe_layer(x2, g_ffn[layer], moe_w_router[j], moe_b_router[j], moe_w_gate[j], moe_w_up[j],
                           moe_w_down[j])
    return x2.reshape(B, S, D)
```

```python
import functools

import jax
import jax.numpy as jnp
import numpy as np
from jax import lax
from jax.experimental import pallas as pl
from jax.experimental.pallas import tpu as pltpu

F32 = jnp.float32
BF16 = jnp.bfloat16

D_MODEL = 2048
DEPTH = 2
RMS_EPS = 1e-6
MASK_VALUE = -1e30
LB_FLOOR = 1e-30
GRID_W = 64

NA_HEADS = 8
NA_HEAD_DIM = 64
NA_WIDTH = 512
NA_WIN_ROWS = 8
NA_WIN_COLS = 16

GLA_HEADS = 4
GLA_HEAD_K = 64
GLA_HEAD_V = 128
GLA_KEY_WIDTH = 256
GLA_VAL_WIDTH = 512
GLA_GATE_RANK = 16
GLA_GATE_NORMALIZER = 16.0

GDN_HEADS = 4
GDN_HEAD_K = 128
GDN_HEAD_V = 128
GDN_KEY_WIDTH = 512
GDN_VAL_WIDTH = 512
GDN_CHUNK = 64

HGRN_HEADS = 4
HGRN_HEAD_K = 128
HGRN_HEAD_V = 128
HGRN_KEY_WIDTH = 512
HGRN_VAL_WIDTH = 512

LIN_CHUNK = 32

MEM_HEADS = 4
MEM_HEAD_DIM = 128
MEM_WIDTH = 512

N_BRANCH = 5
BRANCH_WIDTH = 512
N_EXPERTS = 8
MOE_TOP_K = 2

IN_WIDTHS = (
    NA_WIDTH, NA_WIDTH, NA_WIDTH,
    GLA_KEY_WIDTH, GLA_KEY_WIDTH, GLA_VAL_WIDTH,
    2 * GLA_GATE_RANK, GLA_VAL_WIDTH,
    2 * GDN_KEY_WIDTH + GDN_VAL_WIDTH,
    2 * GDN_HEADS, 2 * GDN_HEADS, GDN_VAL_WIDTH,
    HGRN_KEY_WIDTH, 2 * HGRN_KEY_WIDTH, HGRN_VAL_WIDTH, HGRN_VAL_WIDTH,
    MEM_WIDTH,
    N_BRANCH * D_MODEL,
)
P_IN = sum(IN_WIDTHS)

V7X_VMEM_BYTES = 64 * 1024 * 1024
VMEM_LIMIT_BYTES = V7X_VMEM_BYTES - 8 * 1024 * 1024
LANES = 128


def _params(*semantics):
    return pltpu.CompilerParams(dimension_semantics=semantics, vmem_limit_bytes=VMEM_LIMIT_BYTES)


def _sigmoid(x):
    return 0.5 * jnp.tanh(0.5 * x) + 0.5


def _rms_norm_rows(x, gain):
    ms = jnp.mean(x * x, axis=-1, keepdims=True)
    return x * lax.rsqrt(ms + RMS_EPS) * gain


def _rms_matmul_kernel(x_ref, g_ref, w_ref, o_ref, h_ref):
    @pl.when(pl.program_id(1) == 0)
    def _():
        h_ref[...] = _rms_norm_rows(x_ref[...], g_ref[...]).astype(BF16)

    o_ref[...] = jnp.dot(h_ref[...], w_ref[...], preferred_element_type=F32).astype(o_ref.dtype)


def rms_matmul(x, gain, w, *, tm, tn, out_dtype=F32):
    m, k = x.shape
    n = w.shape[1]
    return pl.pallas_call(
        _rms_matmul_kernel,
        out_shape=jax.ShapeDtypeStruct((m, n), out_dtype),
        grid=(m // tm, n // tn),
        in_specs=[
            pl.BlockSpec((tm, k), lambda i, j: (i, 0)),
            pl.BlockSpec((1, k), lambda i, j: (0, 0)),
            pl.BlockSpec((k, tn), lambda i, j: (0, j)),
        ],
        out_specs=pl.BlockSpec((tm, tn), lambda i, j: (i, j)),
        scratch_shapes=[pltpu.VMEM((tm, k), BF16)],
        compiler_params=_params("parallel", "arbitrary"),
        name="rms_matmul",
    )(x, gain.reshape(1, k), w)


IN_PROJ_TILE = 512


def _in_projection_kernel(x_ref, g_ref, w_ref, ob_ref, of_ref, h_ref, *, n_bf16_tiles):
    j = pl.program_id(1)

    @pl.when(j == 0)
    def _():
        h_ref[...] = _rms_norm_rows(x_ref[...], g_ref[...]).astype(BF16)

    r = jnp.dot(h_ref[...], w_ref[...], preferred_element_type=F32)

    @pl.when(j < n_bf16_tiles)
    def _():
        ob_ref[...] = r.astype(ob_ref.dtype)

    @pl.when(j >= n_bf16_tiles)
    def _():
        of_ref[...] = r


def in_projection(x, gain, w, n_bf16, *, tm=1024):
    m, k = x.shape
    tn = IN_PROJ_TILE
    nb = n_bf16 // tn
    nf = (w.shape[1] - n_bf16) // tn
    return pl.pallas_call(
        functools.partial(_in_projection_kernel, n_bf16_tiles=nb),
        out_shape=(jax.ShapeDtypeStruct((m, nb * tn), BF16), jax.ShapeDtypeStruct((m, nf * tn), F32)),
        grid=(m // tm, nb + nf),
        in_specs=[
            pl.BlockSpec((tm, k), lambda i, j: (i, 0)),
            pl.BlockSpec((1, k), lambda i, j: (0, 0)),
            pl.BlockSpec((k, tn), lambda i, j: (0, j)),
        ],
        out_specs=(pl.BlockSpec((tm, tn), lambda i, j: (i, jnp.minimum(j, nb - 1))),
                   pl.BlockSpec((tm, tn), lambda i, j: (i, jnp.maximum(j - nb, 0)))),
        scratch_shapes=[pltpu.VMEM((tm, k), BF16)],
        compiler_params=_params("parallel", "arbitrary"),
        name="in_projection",
    )(x, gain.reshape(1, k), w)


def _rms_swiglu_kernel(x_ref, g_ref, wg_ref, wu_ref, o_ref, h_ref):
    @pl.when(pl.program_id(1) == 0)
    def _():
        h_ref[...] = _rms_norm_rows(x_ref[...], g_ref[...]).astype(BF16)

    h = h_ref[...]
    a = jnp.dot(h, wg_ref[...], preferred_element_type=F32)
    b = jnp.dot(h, wu_ref[...], preferred_element_type=F32)
    o_ref[...] = (a * _sigmoid(a) * b).astype(o_ref.dtype)


def rms_swiglu_up(x, gain, wg, wu, *, tm, tn):
    m, k = x.shape
    n = wg.shape[1]
    return pl.pallas_call(
        _rms_swiglu_kernel,
        out_shape=jax.ShapeDtypeStruct((m, n), BF16),
        grid=(m // tm, n // tn),
        in_specs=[
            pl.BlockSpec((tm, k), lambda i, j: (i, 0)),
            pl.BlockSpec((1, k), lambda i, j: (0, 0)),
            pl.BlockSpec((k, tn), lambda i, j: (0, j)),
            pl.BlockSpec((k, tn), lambda i, j: (0, j)),
        ],
        out_specs=pl.BlockSpec((tm, tn), lambda i, j: (i, j)),
        scratch_shapes=[pltpu.VMEM((tm, k), BF16)],
        compiler_params=_params("parallel", "arbitrary"),
        name="rms_swiglu_up",
    )(x, gain.reshape(1, k), wg, wu)


def _matmul_residual_kernel(a_ref, w_ref, r_ref, o_ref):
    o_ref[...] = r_ref[...] + jnp.dot(a_ref[...], w_ref[...], preferred_element_type=F32)


def matmul_residual(a, w, res, *, tm, tn):
    m, k = a.shape
    n = w.shape[1]
    return pl.pallas_call(
        _matmul_residual_kernel,
        out_shape=jax.ShapeDtypeStruct((m, n), F32),
        grid=(m // tm, n // tn),
        in_specs=[
            pl.BlockSpec((tm, k), lambda i, j: (i, 0)),
            pl.BlockSpec((k, tn), lambda i, j: (0, j)),
            pl.BlockSpec((tm, tn), lambda i, j: (i, j)),
        ],
        out_specs=pl.BlockSpec((tm, tn), lambda i, j: (i, j)),
        compiler_params=_params("parallel", "arbitrary"),
        name="matmul_residual",
    )(a, w, res)


class RawBranch:
    def __init__(self, o_fwd, o_bwd, og_name, gain, silu_gate):
        self.o_fwd, self.o_bwd, self.og_name, self.gain, self.silu_gate = o_fwd, o_bwd, og_name, gain, silu_gate


def _merge_kernel(*refs, raw):
    pos = 0
    br = []
    for kind in raw:
        width = 1 if kind is None else 4
        br.append(refs[pos:pos + width])
        pos += width
    gl_refs = refs[pos:pos + N_BRANCH]
    wb_ref, o_ref, fin_ref = refs[pos + N_BRANCH:pos + N_BRANCH + 3]
    raw_slot = {n: s for s, n in enumerate(n for n, kind in enumerate(raw) if kind is not None)}

    @pl.when(pl.program_id(1) == 0)
    def _():
        for n, slot in raw_slot.items():
            of_ref, ob_ref, og_ref, gain_ref = br[n]
            for h in range(BRANCH_WIDTH // LANES):
                cols = slice(h * LANES, (h + 1) * LANES)
                y = _rms_norm_rows(of_ref[:, cols] + ob_ref[:, cols], gain_ref[...])
                g = og_ref[:, cols].astype(F32)
                gate = _sigmoid(g)
                if raw[n]:
                    gate = g * gate
                fin_ref[slot, :, cols] = (y * gate).astype(fin_ref.dtype)

    acc = None
    for n in range(N_BRANCH):
        b = br[n][0][...] if raw[n] is None else fin_ref[raw_slot[n]]
        y = jnp.dot(b, wb_ref[n], preferred_element_type=F32)
        t = _sigmoid(gl_refs[n][...].astype(F32)) * y
        acc = t if acc is None else acc + t
    o_ref[...] = acc.astype(o_ref.dtype)


def merge_branches(branches, pb, w_branch, *, tm, tn):
    m = pb.shape[0]
    d = D_MODEL
    tiles_per_branch = d // tn
    tile0 = PB_COL["gates"] // tn
    row_block = pl.BlockSpec((tm, BRANCH_WIDTH), lambda i, j: (i, 0))
    in_specs, operands, raw = [], [], []
    for b in branches:
        if isinstance(b, RawBranch):
            og_col = PB_COL[b.og_name] // BRANCH_WIDTH
            in_specs += [row_block, row_block,
                         pl.BlockSpec((tm, BRANCH_WIDTH), functools.partial(lambda i, j, c: (i, c), c=og_col)),
                         pl.BlockSpec((1, LANES), lambda i, j: (0, 0))]
            operands += [b.o_fwd, b.o_bwd, pb, b.gain.astype(F32).reshape(1, LANES)]
            raw.append(b.silu_gate)
        else:
            in_specs.append(row_block)
            operands.append(b)
            raw.append(None)
    in_specs += [
        pl.BlockSpec((tm, tn), functools.partial(lambda i, j, n: (i, tile0 + n * tiles_per_branch + j), n=n))
        for n in range(N_BRANCH)
    ]
    in_specs += [pl.BlockSpec((N_BRANCH, BRANCH_WIDTH, tn), lambda i, j: (0, 0, j))]
    n_raw = sum(kind is not None for kind in raw)
    return pl.pallas_call(
        functools.partial(_merge_kernel, raw=tuple(raw)),
        out_shape=jax.ShapeDtypeStruct((m, d), BF16),
        grid=(m // tm, d // tn),
        in_specs=in_specs,
        out_specs=pl.BlockSpec((tm, tn), lambda i, j: (i, j)),
        scratch_shapes=[pltpu.VMEM((max(n_raw, 1), tm, BRANCH_WIDTH), BF16)],
        compiler_params=_params("parallel", "arbitrary"),
        name="merge_branches",
    )(*operands, *([pb] * N_BRANCH), w_branch)


def _router_kernel(x_ref, g_ref, w_ref, b_ref, o_ref, h_ref, cnt_ref, run_ref, *, n_experts):
    @pl.when(pl.program_id(0) == 0)
    def _():
        run_ref[...] = jnp.zeros_like(run_ref)

    h = _rms_norm_rows(x_ref[...], g_ref[...])
    h_ref[...] = h.astype(h_ref.dtype)
    logits = _dot_f32(h, w_ref[...]) + b_ref[...]
    lane = lax.broadcasted_iota(jnp.int32, logits.shape, 1).astype(F32)
    neg = -jnp.inf
    lm = jnp.where(lane < n_experts, logits, neg)
    m1 = jnp.max(lm, axis=-1, keepdims=True)
    i1 = jnp.min(jnp.where(lm == m1, lane, float(LANES)), axis=-1, keepdims=True)
    lm2 = jnp.where(lane == i1, neg, lm)
    m2 = jnp.max(lm2, axis=-1, keepdims=True)
    i2 = jnp.min(jnp.where(lm2 == m2, lane, float(LANES)), axis=-1, keepdims=True)
    t = jnp.exp(m2 - m1)
    den = 1.0 + t

    tm = logits.shape[0]
    before = (lax.broadcasted_iota(jnp.int32, (tm, tm), 1)
              < lax.broadcasted_iota(jnp.int32, (tm, tm), 0))
    before = jnp.where(before, 1.0, 0.0).astype(BF16)
    pick1 = lane == i1
    pick2 = lane == i2
    oh1 = jnp.where(pick1, 1.0, 0.0)
    oh2 = jnp.where(pick2, 1.0, 0.0)
    pre1 = jnp.dot(before, oh1.astype(BF16), preferred_element_type=F32)
    pre2 = jnp.dot(before, oh2.astype(BF16), preferred_element_type=F32)
    tot1 = jnp.sum(oh1, axis=0, keepdims=True)
    tot2 = jnp.sum(oh2, axis=0, keepdims=True)
    run = run_ref[...]
    rank1 = jnp.sum(jnp.where(pick1, pre1 + run, 0.0), axis=-1, keepdims=True)
    rank2 = jnp.sum(jnp.where(pick2, pre2 + (run + tot1), 0.0), axis=-1, keepdims=True)
    run = run + tot1 + tot2
    run_ref[...] = run
    cnt_ref[...] = jnp.broadcast_to(run, cnt_ref.shape)

    out = jnp.where(lane == 0, 1.0 / den, jnp.where(lane == 1, t / den, jnp.where(lane == 2, i1, i2)))
    out = jnp.where(lane == 4, rank1, jnp.where(lane == 5, rank2, out))
    o_ref[...] = jnp.where(lane < 6, out, 0.0)


def router_top2(x, gain, w_router, b_router, *, tm=512):
    m, k = x.shape
    e = w_router.shape[1]
    w_pad = jnp.zeros((k, LANES), F32).at[:, :e].set(w_router.astype(F32))
    b_pad = jnp.zeros((1, LANES), F32).at[0, :e].set(b_router.astype(F32))
    route, h, cnt = pl.pallas_call(
        functools.partial(_router_kernel, n_experts=e),
        out_shape=(jax.ShapeDtypeStruct((m, LANES), F32), jax.ShapeDtypeStruct((m, k), BF16),
                   jax.ShapeDtypeStruct((8, LANES), F32)),
        grid=(m // tm,),
        in_specs=[
            pl.BlockSpec((tm, k), lambda i: (i, 0)),
            pl.BlockSpec((1, k), lambda i: (0, 0)),
            pl.BlockSpec((k, LANES), lambda i: (0, 0)),
            pl.BlockSpec((1, LANES), lambda i: (0, 0)),
        ],
        out_specs=(pl.BlockSpec((tm, LANES), lambda i: (i, 0)), pl.BlockSpec((tm, k), lambda i: (i, 0)),
                   pl.BlockSpec((8, LANES), lambda i: (0, 0))),
        scratch_shapes=[pltpu.VMEM((1, LANES), F32)],
        compiler_params=_params("arbitrary"),
        name="router_top2",
    )(x, gain.reshape(1, k), w_pad, b_pad)
    return route, h, cnt[0, :e].astype(jnp.int32)


MOE_TILE = 1024
MOE_SUB = 256
MOE_FF_TILE = 512


def _moe_kernel(tile_e_ref, tile_rows_ref, n_used_ref, x_ref, wg_ref, wu_ref, wd_ref, o_ref, acc_ref):
    i = pl.program_id(0)
    j = pl.program_id(1)
    last = pl.num_programs(1) - 1
    valid = tile_rows_ref[i]
    n_sub = (valid + (MOE_SUB - 1)) // MOE_SUB

    for k in range(1, MOE_TILE // MOE_SUB + 1):
        rows = slice(0, k * MOE_SUB)

        @pl.when(n_sub == k)
        def _(rows=rows):
            x = x_ref[rows, :]
            a = jnp.dot(x, wg_ref[0].astype(BF16), preferred_element_type=F32)
            b = jnp.dot(x, wu_ref[0].astype(BF16), preferred_element_type=F32)
            act = (a * _sigmoid(a) * b).astype(BF16)
            part = jnp.dot(act, wd_ref[0].astype(BF16), preferred_element_type=F32)

            @pl.when(j == 0)
            def _():
                acc_ref[rows, :] = part

            @pl.when(j > 0)
            def _():
                acc_ref[rows, :] += part

    for s in range(0, MOE_TILE, MOE_SUB):
        rows = slice(s, s + MOE_SUB)
        filled = s < valid

        @pl.when(jnp.logical_and(filled, j == last))
        def _(rows=rows):
            o_ref[rows, :] = acc_ref[rows, :].astype(o_ref.dtype)

        @pl.when(jnp.logical_and(jnp.logical_not(filled), j == last))
        def _(rows=rows):
            o_ref[rows, :] = jnp.zeros((MOE_SUB, o_ref.shape[1]), o_ref.dtype)


def moe_experts(xb, tile_e, tile_rows, n_used, wg, wu, wd):
    rows, d = xb.shape
    ff = wg.shape[2]
    tm, tf = MOE_TILE, MOE_FF_TILE
    n_tiles = rows // tm
    last_j = ff // tf - 1

    def x_map(i, j, te, tr, nu):
        return (jnp.minimum(i, nu[0] - 1), 0)

    def up_map(i, j, te, tr, nu):
        return (te[i], 0, jnp.where(i < nu[0], j, last_j))

    def down_map(i, j, te, tr, nu):
        return (te[i], jnp.where(i < nu[0], j, last_j), 0)

    grid_spec = pltpu.PrefetchScalarGridSpec(
        num_scalar_prefetch=3,
        grid=(n_tiles, ff // tf),
        in_specs=[
            pl.BlockSpec((tm, d), x_map, pipeline_mode=pl.Buffered(1)),
            pl.BlockSpec((1, d, tf), up_map),
            pl.BlockSpec((1, d, tf), up_map),
            pl.BlockSpec((1, tf, d), down_map),
        ],
        out_specs=pl.BlockSpec((tm, d), lambda i, j, te, tr, nu: (i, 0)),
        scratch_shapes=[pltpu.VMEM((tm, d), F32)],
    )
    return pl.pallas_call(
        _moe_kernel,
        out_shape=jax.ShapeDtypeStruct((rows, d), BF16),
        grid_spec=grid_spec,
        compiler_params=_params("arbitrary", "arbitrary"),
        name="moe_experts",
    )(tile_e, tile_rows, n_used, xb, wg, wu, wd)


def _moe_combine_kernel(x_ref, y0_ref, y1_ref, r_ref, o_ref):
    w = r_ref[...]
    o_ref[...] = x_ref[...] + w[:, 0:1] * y0_ref[...].astype(F32) + w[:, 1:2] * y1_ref[...].astype(F32)


def moe_combine(x2d, y0, y1, route, *, tm=512):
    n, d = x2d.shape
    row_block = pl.BlockSpec((tm, d), lambda i: (i, 0))
    return pl.pallas_call(
        _moe_combine_kernel,
        out_shape=jax.ShapeDtypeStruct((n, d), F32),
        grid=(n // tm,),
        in_specs=[row_block, row_block, row_block, pl.BlockSpec((tm, LANES), lambda i: (i, 0))],
        out_specs=row_block,
        compiler_params=_params("parallel"),
        name="moe_combine",
    )(x2d, y0, y1, route)


def moe_layer(x2d, gain, w_router, b_router, wg, wu, wd):
    n, d = x2d.shape
    e = N_EXPERTS
    route, h, counts = router_top2(x2d, gain, w_router, b_router)
    nk = n * MOE_TOP_K
    n_tiles = -(-nk // MOE_TILE) + e
    flat_e = route[:, 2:2 + MOE_TOP_K].astype(jnp.int32).reshape(nk)
    rank = route[:, 4:4 + MOE_TOP_K].astype(jnp.int32).reshape(nk)
    flat_tok = jnp.repeat(jnp.arange(n, dtype=jnp.int32), MOE_TOP_K)
    padded = (counts + MOE_TILE - 1) // MOE_TILE * MOE_TILE
    pad_end = jnp.cumsum(padded)
    pad_start = pad_end - padded
    slot = (pad_start[flat_e] + rank).astype(jnp.int32)
    n_slots = n_tiles * MOE_TILE
    slot_tok = (jnp.arange(n_slots, dtype=jnp.int32) % n).at[slot].set(flat_tok)
    tile_start = jnp.arange(n_tiles, dtype=jnp.int32) * MOE_TILE
    tile_e = jnp.minimum(jnp.searchsorted(pad_end, tile_start, side="right"), e - 1).astype(jnp.int32)
    tile_rows = jnp.clip(pad_start[tile_e] + counts[tile_e] - tile_start, 0, MOE_TILE).astype(jnp.int32)
    tile_rows = jnp.where(tile_start < pad_end[-1], tile_rows, 0)
    n_used = (pad_end[-1] // MOE_TILE).astype(jnp.int32).reshape(1)
    tile_e = jnp.where(tile_start < pad_end[-1], tile_e, tile_e[jnp.maximum(n_used[0] - 1, 0)])

    xb = h[slot_tok]
    yb = moe_experts(xb, tile_e, tile_rows, n_used, wg, wu, wd)
    slot2 = slot.reshape(n, MOE_TOP_K)
    return moe_combine(x2d, yb[slot2[:, 0]], yb[slot2[:, 1]], route)


_SRC = dict(zip(
    ("na_q", "na_k", "na_v", "gla_q", "gla_k", "gla_v", "gla_lr", "gla_og", "gdn_qkv", "gdn_a", "gdn_b",
     "gdn_og", "hg_q", "hg_f", "hg_i", "hg_og", "mem_q", "gates"),
    zip(np.cumsum((0,) + IN_WIDTHS[:-1]).tolist(), IN_WIDTHS)))
_PB_ORDER = ("na_q", "na_k", "na_v", "gla_q", "gla_k", "gla_v", "gla_og", "gdn_qkv", "gdn_og", "hg_q", "hg_i",
             "hg_og", "mem_q", "gates")
_PF_ORDER = ("hg_f", "gla_lr", "gdn_a", "gdn_b")
PB_COL = {}
_c = 0
for _name in _PB_ORDER:
    PB_COL[_name] = _c
    _c += _SRC[_name][1]
PB_WIDTH = _c
PF_COL = {}
_c = 0
for _name in _PF_ORDER:
    PF_COL[_name] = _c
    _c += _SRC[_name][1]
PF_WIDTH = -(-_c // IN_PROJ_TILE) * IN_PROJ_TILE
PF_SMALL_COL = PF_COL["gla_lr"]
GDN_A_LANE = PF_COL["gdn_a"] - PF_SMALL_COL
GDN_B_LANE = PF_COL["gdn_b"] - PF_SMALL_COL


def _rearrange_w_in(w):
    cols = [w[:, _SRC[n][0]:_SRC[n][0] + _SRC[n][1]] for n in _PB_ORDER + _PF_ORDER]
    out = jnp.concatenate(cols, axis=1)
    return jnp.pad(out, ((0, 0), (0, PB_WIDTH + PF_WIDTH - out.shape[1]))).astype(BF16)


def _segment_rms(x, gain, seg_ones, seg_width):
    sq = x * x
    hi = sq.astype(BF16)
    lo = (sq - hi.astype(F32)).astype(BF16)
    ss = (jnp.dot(hi, seg_ones, preferred_element_type=F32)
          + jnp.dot(lo, seg_ones, preferred_element_type=F32))
    return x * lax.rsqrt(ss * (1.0 / seg_width) + RMS_EPS) * gain


NA_ROWS_PER_STEP = 8
NA_BAND = NA_WIN_ROWS * GRID_W


def _na_bias_table(rel_bias):
    c = np.arange(GRID_W)
    dc = np.clip(c[None, :] - c[:, None], 1 - NA_WIN_COLS, NA_WIN_COLS - 1) + (NA_WIN_COLS - 1)
    col_start = np.clip(c - NA_WIN_COLS // 2, 0, GRID_W - NA_WIN_COLS)
    col_in = (c[None, :] >= col_start[:, None]) & (c[None, :] < col_start[:, None] + NA_WIN_COLS)
    cfg = np.arange(NA_WIN_ROWS)[:, None]
    dr = np.arange(NA_WIN_ROWS)[None, :] - cfg + (NA_WIN_ROWS - 1)
    t = rel_bias.astype(F32)[:, dr][:, :, :, dc]
    t = jnp.where(col_in[None, None, None], t, MASK_VALUE)
    return t.transpose(1, 0, 3, 2, 4).reshape(NA_WIN_ROWS, NA_HEADS, GRID_W, NA_BAND)


def _na_kernel(q_ref, k_ref, v_ref, qg_ref, kg_ref, seg_ref, bias_ref, o_ref, kn_ref):
    step = pl.program_id(1)
    rows_total = k_ref.shape[0] // GRID_W
    seg = seg_ref[...]

    @pl.when(step == 0)
    def _():
        def norm_keys(t, carry):
            rows = pl.ds(pl.multiple_of(t * 256, 256), 256)
            kn_ref[rows, :] = _segment_rms(k_ref[rows, :].astype(F32), kg_ref[...], seg, NA_HEAD_DIM).astype(BF16)
            return carry
        lax.fori_loop(0, k_ref.shape[0] // 256, norm_keys, 0)

    lane = lax.broadcasted_iota(jnp.int32, (1, LANES), 1)
    low_half = lane < NA_HEAD_DIM

    def one_row(rr, carry):
        r = step * NA_ROWS_PER_STEP + rr
        row_start = jnp.clip(r - NA_WIN_ROWS // 2, 0, rows_total - NA_WIN_ROWS)
        cfg = r - row_start
        qrows = pl.ds(pl.multiple_of(rr * GRID_W, GRID_W), GRID_W)
        band = pl.ds(pl.multiple_of(row_start * GRID_W, GRID_W), NA_BAND)
        qn = (_segment_rms(q_ref[qrows, :].astype(F32), qg_ref[...], seg, NA_HEAD_DIM)
              * (NA_HEAD_DIM ** -0.5)).astype(BF16)
        heads = [(pair, half) for pair in range(NA_HEADS // 2) for half in range(2)]
        scores = []
        for pair, half in heads:
            cols = slice(pair * LANES, (pair + 1) * LANES)
            qp = qn[:, cols]
            keep = low_half if half == 0 else jnp.logical_not(low_half)
            qm = jnp.where(keep, qp, jnp.zeros_like(qp))
            scores.append(lax.dot_general(qm, kn_ref[band, cols], (((1,), (1,)), ((), ())),
                                          preferred_element_type=F32))
        exps, sums = [], []
        for s, (pair, half) in zip(scores, heads):
            s = s + bias_ref[cfg, 2 * pair + half]
            e = jnp.exp(s - jnp.max(s, axis=-1, keepdims=True))
            sums.append(jnp.sum(e, axis=-1, keepdims=True))
            exps.append(e.astype(BF16))
        outs = [jnp.dot(e, v_ref[band, slice(pair * LANES, (pair + 1) * LANES)], preferred_element_type=F32) / l
                for e, l, (pair, half) in zip(exps, sums, heads)]
        for pair in range(NA_HEADS // 2):
            cols = slice(pair * LANES, (pair + 1) * LANES)
            o_ref[qrows, cols] = jnp.where(low_half, outs[2 * pair], outs[2 * pair + 1]).astype(o_ref.dtype)
        return carry

    lax.fori_loop(0, NA_ROWS_PER_STEP, one_row, 0)


def neighbourhood_attention(pb, batch, q_gain, k_gain, rel_bias):
    m = pb.shape[0]
    s = m // batch
    tq = NA_ROWS_PER_STEP * GRID_W
    steps = s // tq
    qg = jnp.tile(q_gain.astype(F32), NA_HEADS).reshape(1, NA_WIDTH)
    kg = jnp.tile(k_gain.astype(F32), NA_HEADS).reshape(1, NA_WIDTH)
    seg = jnp.asarray(np.kron(np.eye(NA_HEADS), np.ones((NA_HEAD_DIM, NA_HEAD_DIM))), BF16)
    bias = _na_bias_table(rel_bias)
    cq, ck, cv = (PB_COL[n] // NA_WIDTH for n in ("na_q", "na_k", "na_v"))
    return pl.pallas_call(
        _na_kernel,
        out_shape=jax.ShapeDtypeStruct((m, NA_WIDTH), BF16),
        grid=(batch, steps),
        in_specs=[
            pl.BlockSpec((tq, NA_WIDTH), lambda b, t: (b * steps + t, cq)),
            pl.BlockSpec((s, NA_WIDTH), lambda b, t: (b, ck)),
            pl.BlockSpec((s, NA_WIDTH), lambda b, t: (b, cv)),
            pl.BlockSpec((1, NA_WIDTH), lambda b, t: (0, 0)),
            pl.BlockSpec((1, NA_WIDTH), lambda b, t: (0, 0)),
            pl.BlockSpec((NA_WIDTH, NA_WIDTH), lambda b, t: (0, 0)),
            pl.BlockSpec((NA_WIN_ROWS, NA_HEADS, GRID_W, NA_BAND), lambda b, t: (0, 0, 0, 0)),
        ],
        out_specs=pl.BlockSpec((tq, NA_WIDTH), lambda b, t: (b * steps + t, 0)),
        scratch_shapes=[pltpu.VMEM((s, NA_WIDTH), BF16)],
        compiler_params=_params("parallel", "arbitrary"),
        name="neighbourhood_attention",
    )(pb, pb, pb, qg, kg, seg, bias)


def _mem_attn_kernel(q_ref, kv_ref, qg_ref, kg_ref, o_ref, kn_ref):
    @pl.when(pl.program_id(1) == 0)
    def _():
        for h in range(MEM_HEADS):
            cols = slice(h * MEM_HEAD_DIM, (h + 1) * MEM_HEAD_DIM)
            kn_ref[:, cols] = _rms_norm_rows(kv_ref[:, cols].astype(F32), kg_ref[...]).astype(BF16)

    head_cols = [slice(h * MEM_HEAD_DIM, (h + 1) * MEM_HEAD_DIM) for h in range(MEM_HEADS)]
    qns = [_rms_norm_rows(q_ref[:, cols].astype(F32), qg_ref[...]).astype(BF16) for cols in head_cols]
    scores = [lax.dot_general(qn, kn_ref[:, cols], (((1,), (1,)), ((), ())), preferred_element_type=F32)
              for qn, cols in zip(qns, head_cols)]
    exps, sums = [], []
    for s in scores:
        s = s * (MEM_HEAD_DIM ** -0.5)
        e = jnp.exp(s - jnp.max(s, axis=-1, keepdims=True))
        sums.append(jnp.sum(e, axis=-1, keepdims=True))
        exps.append(e.astype(BF16))
    outs = [jnp.dot(e, kv_ref[:, MEM_WIDTH + cols.start:MEM_WIDTH + cols.stop], preferred_element_type=F32)
            for e, cols in zip(exps, head_cols)]
    for o, l, cols in zip(outs, sums, head_cols):
        o_ref[:, cols] = (o / l).astype(o_ref.dtype)


def memory_cross_attention(pb, kv, batch, q_gain, k_gain, *, tq=512):
    m = pb.shape[0]
    steps = m // batch // tq
    n_mem = kv.shape[0] // batch
    cq = PB_COL["mem_q"] // MEM_WIDTH
    return pl.pallas_call(
        _mem_attn_kernel,
        out_shape=jax.ShapeDtypeStruct((m, MEM_WIDTH), BF16),
        grid=(batch, steps),
        in_specs=[
            pl.BlockSpec((tq, MEM_WIDTH), lambda b, t: (b * steps + t, cq)),
            pl.BlockSpec((n_mem, 2 * MEM_WIDTH), lambda b, t: (b, 0)),
            pl.BlockSpec((1, MEM_HEAD_DIM), lambda b, t: (0, 0)),
            pl.BlockSpec((1, MEM_HEAD_DIM), lambda b, t: (0, 0)),
        ],
        out_specs=pl.BlockSpec((tq, MEM_WIDTH), lambda b, t: (b * steps + t, 0)),
        scratch_shapes=[pltpu.VMEM((n_mem, MEM_WIDTH), BF16)],
        compiler_params=_params("parallel", "arbitrary"),
        name="memory_cross_attention",
    )(pb, kv, q_gain.astype(F32).reshape(1, MEM_HEAD_DIM), k_gain.astype(F32).reshape(1, MEM_HEAD_DIM))


LIN_BLOCK = 512
HEAD_V = 128


def _log1p_exp_neg(t):
    return jnp.log(1.0 + jnp.exp(-t))


def _log_sigmoid(x):
    return jnp.minimum(x, 0.0) - _log1p_exp_neg(jnp.abs(x))


def _logaddexp(a, b):
    return jnp.maximum(a, b) + _log1p_exp_neg(jnp.abs(a - b))


def _split_bf16(x, terms):
    parts = []
    for _ in range(terms):
        p = x.astype(BF16)
        parts.append(p)
        x = x - p.astype(F32)
    return parts


def _dot_f32(a, b):
    a_hi, a_lo = _split_bf16(a, 2)
    b_hi, b_lo = _split_bf16(b, 2)
    return (jnp.dot(a_hi, b_hi, preferred_element_type=F32)
            + (jnp.dot(a_hi, b_lo, preferred_element_type=F32) + jnp.dot(a_lo, b_hi, preferred_element_type=F32)))


def _cumsum_rows(mask, x):
    m = jnp.where(mask, 1.0, 0.0).astype(BF16)
    hi, mid, lo = _split_bf16(x, 3)
    return (jnp.dot(m, hi, preferred_element_type=F32)
            + (jnp.dot(m, mid, preferred_element_type=F32) + jnp.dot(m, lo, preferred_element_type=F32)))


def _gla_inputs(refs, rows, direction, params):
    q_ref, k_ref, v_ref, g_ref = refs
    wpad_ref, bias_ref = params
    qc = q_ref[rows, :].astype(F32) * (GLA_HEAD_K ** -0.5)
    kc = k_ref[rows, :].astype(F32)
    gk = _dot_f32(g_ref[rows, :], wpad_ref[direction]) + bias_ref[direction]
    lg = _log_sigmoid(gk) * (1.0 / GLA_GATE_NORMALIZER)
    return qc, kc, v_ref[rows, :], lg


def _hgrn_inputs(refs, rows, direction, params):
    q_ref, v_ref, z_ref = refs
    lb_ref, log_lb_ref, log1m_lb_ref = params
    qr = q_ref[rows, :].astype(F32)
    qc = qr * _sigmoid(qr)
    z = z_ref[rows, :]
    lg = _logaddexp(log_lb_ref[direction], log1m_lb_ref[direction] + _log_sigmoid(z))
    kc = (1.0 - lb_ref[direction]) * _sigmoid(-z)
    return qc, kc, v_ref[rows, :], lg


GDN_CONV_WIDTH = 5
GDN_QKV_WIDTH = 2 * GDN_KEY_WIDTH + GDN_VAL_WIDTH
GDN_HALO = 16


def _gdn_prep_kernel(prev_ref, cur_ref, next_ref, w_ref, o_ref, xp_ref, *, blocks_per_seq):
    i = pl.program_id(0)
    t = cur_ref.shape[0]
    pos = i % blocks_per_seq
    prev = prev_ref[...].astype(F32)
    nxt = next_ref[...].astype(F32)
    xp_ref[0:GDN_HALO, :] = jnp.where(pos == 0, jnp.zeros_like(prev), prev)
    xp_ref[GDN_HALO:GDN_HALO + t, :] = cur_ref[...].astype(F32)
    xp_ref[GDN_HALO + t:, :] = jnp.where(pos == blocks_per_seq - 1, jnp.zeros_like(nxt), nxt)
    half = GDN_CONV_WIDTH // 2
    for g in range(GDN_QKV_WIDTH // LANES):
        cols = slice(g * LANES, (g + 1) * LANES)
        acc = None
        for j in range(GDN_CONV_WIDTH):
            term = xp_ref[GDN_HALO - half + j:GDN_HALO - half + j + t, cols] * w_ref[j:j + 1, cols]
            acc = term if acc is None else acc + term
        y = acc * _sigmoid(acc)
        if g < 2 * GDN_HEADS:
            y = y * lax.rsqrt(jnp.sum(y * y, axis=-1, keepdims=True) + 1e-6)
            if g < GDN_HEADS:
                y = y * (GDN_HEAD_K ** -0.5)
        o_ref[:, cols] = y.astype(o_ref.dtype)


def gdn_prep(pb, batch, conv_w, *, t=512):
    m = pb.shape[0]
    blocks_per_seq = m // batch // t
    halo_per_block = t // GDN_HALO
    col = PB_COL["gdn_qkv"] // GDN_QKV_WIDTH
    last_halo = m // GDN_HALO - 1
    return pl.pallas_call(
        functools.partial(_gdn_prep_kernel, blocks_per_seq=blocks_per_seq),
        out_shape=jax.ShapeDtypeStruct((m, GDN_QKV_WIDTH), BF16),
        grid=(m // t,),
        in_specs=[
            pl.BlockSpec((GDN_HALO, GDN_QKV_WIDTH), lambda i: (jnp.maximum(i * halo_per_block - 1, 0), col)),
            pl.BlockSpec((t, GDN_QKV_WIDTH), lambda i: (i, col)),
            pl.BlockSpec((GDN_HALO, GDN_QKV_WIDTH),
                         lambda i: (jnp.minimum((i + 1) * halo_per_block, last_halo), col)),
            pl.BlockSpec((GDN_CONV_WIDTH, GDN_QKV_WIDTH), lambda i: (0, 0)),
        ],
        out_specs=pl.BlockSpec((t, GDN_QKV_WIDTH), lambda i: (i, 0)),
        scratch_shapes=[pltpu.VMEM((t + 2 * GDN_HALO, GDN_QKV_WIDTH), F32)],
        compiler_params=_params("parallel"),
        name="gdn_prep",
    )(pb, pb, pb, conv_w.astype(F32))


def _softplus(x):
    return jnp.maximum(x, 0.0) + _log1p_exp_neg(jnp.abs(x))


GDN_PACK = GDN_HEADS * GDN_CHUNK
GDN_WY_BLOCK = 512


def _stack_heads(x, width):
    heads = x.shape[1] // width
    lane = lax.broadcasted_iota(jnp.int32, (1, x.shape[1]), 1)
    return jnp.concatenate(
        [jnp.where((lane >= h * width) & (lane < (h + 1) * width), x, 0.0).astype(BF16) for h in range(heads)],
        axis=0)


def _packed_mm(x, y):
    return jnp.dot(x.astype(BF16), _stack_heads(y, GDN_CHUNK), preferred_element_type=F32)


def _packed_inverses(mats):
    c = GDN_CHUNK
    ii = lax.broadcasted_iota(jnp.int32, (c, GDN_PACK), 0)
    jj = lax.broadcasted_iota(jnp.int32, (c, GDN_PACK), 1) % c
    eye = (ii == jj).astype(F32)

    def same_block(s):
        return (ii // s) == (jj // s)

    ds = [jnp.where(same_block(8), a, 0.0) for a in mats]
    d2s = [_packed_mm(d, d) for d in ds]
    d4s = [_packed_mm(d2, d2) for d2 in d2s]
    ts = [_packed_mm(eye - d, eye + d2) for d, d2 in zip(ds, d2s)]
    ts = [_packed_mm(t, eye + d4) for t, d4 in zip(ts, d4s)]
    s = 8
    while s < c:
        off = same_block(2 * s) & jnp.logical_not(same_block(s))
        ets = [_packed_mm(jnp.where(off, a, 0.0), t) for a, t in zip(mats, ts)]
        ts = [t - _packed_mm(t, et) for t, et in zip(ts, ets)]
        s *= 2
    return ts


def _gdn_wy_kernel(qkv_ref, small_ref, a_ref, dtb_ref, selg_ref, selk_ref, selb_ref, *out_refs):
    c = GDN_CHUNK
    n_chunks = qkv_ref.shape[0] // c
    ii = lax.broadcasted_iota(jnp.int32, (c, c), 0)
    jj = lax.broadcasted_iota(jnp.int32, (c, c), 1)
    pi = lax.broadcasted_iota(jnp.int32, (c, GDN_PACK), 0)
    pj = lax.broadcasted_iota(jnp.int32, (c, GDN_PACK), 1) % c
    eye_p = (pi == pj).astype(F32)
    ones_cc = jnp.ones((c, c), BF16)

    problems = [(ch, d) for ch in range(n_chunks) for d in range(2)]
    chunk_in = []
    for ch in range(n_chunks):
        rows = slice(ch * c, (ch + 1) * c)
        qkv = qkv_ref[rows, :]
        small = small_ref[rows, :]
        kf = qkv[:, GDN_KEY_WIDTH:2 * GDN_KEY_WIDTH].astype(F32)
        chunk_in.append(dict(
            qf=qkv[:, :GDN_KEY_WIDTH].astype(F32), kf=kf, vf=qkv[:, 2 * GDN_KEY_WIDTH:].astype(F32),
            kbd=_stack_heads(kf, HEAD_V),
            log_alpha=a_ref[...] * _softplus(small + dtb_ref[...]),
            beta_all=_sigmoid(small)))

    def sel3(x, sel):
        hi, mid, lo = _split_bf16(x, 3)
        return (jnp.dot(hi, sel, preferred_element_type=F32)
                + (jnp.dot(mid, sel, preferred_element_type=F32) + jnp.dot(lo, sel, preferred_element_type=F32)))

    g_all = [_cumsum_rows((jj >= ii) if d else (jj <= ii), chunk_in[ch]["log_alpha"]) for ch, d in problems]
    g_pack = [sel3(g, selg_ref[d]) for g, (ch, d) in zip(g_all, problems)]
    g_wide = [sel3(g, selk_ref[d]) for g, (ch, d) in zip(g_all, problems)]
    beta_w = [sel3(chunk_in[ch]["beta_all"], selb_ref[d]) for ch, d in problems]
    g_rowp = []
    for gp in g_pack:
        hi, mid, lo = _split_bf16(gp * eye_p, 3)
        g_rowp.append(jnp.dot(ones_cc, hi, preferred_element_type=F32)
                      + (jnp.dot(ones_cc, mid, preferred_element_type=F32)
                         + jnp.dot(ones_cc, lo, preferred_element_type=F32)))
    decays, k_betas = [], []
    for gp, gr, bw, (ch, d) in zip(g_pack, g_rowp, beta_w, problems):
        incl = (pj >= pi) if d else (pj <= pi)
        decays.append(jnp.where(incl, jnp.exp(jnp.where(incl, gp - gr, 0.0)), 0.0))
        k_betas.append(chunk_in[ch]["kf"] * bw)
    kq = [lax.dot_general(jnp.concatenate([kb, chunk_in[ch]["qf"]], axis=0).astype(BF16), chunk_in[ch]["kbd"],
                          (((1,), (1,)), ((), ())), preferred_element_type=F32)
          for kb, (ch, d) in zip(k_betas, problems)]
    a_mats = []
    for x, dec, (ch, d) in zip(kq, decays, problems):
        strict = (pj > pi) if d else (pj < pi)
        a_mats.append(jnp.where(strict, x[:c] * dec, 0.0))
    t_invs = _packed_inverses(a_mats)

    for idx, (ch, d) in enumerate(problems):
        u_ref, w_ref, attn_ref, qd_ref, kd_ref, gt_ref = out_refs[6 * d:6 * d + 6]
        rows = slice(ch * c, (ch + 1) * c)
        cin = chunk_in[ch]
        gw = g_wide[idx]
        eg = jnp.exp(gw)
        t_b = t_invs[idx].astype(BF16)
        u_ref[rows, :] = jnp.dot(t_b, _stack_heads(cin["vf"] * beta_w[idx], HEAD_V), preferred_element_type=F32)
        w_ref[rows, :] = jnp.dot(t_b, _stack_heads(k_betas[idx] * eg, HEAD_V),
                                 preferred_element_type=F32).astype(w_ref.dtype)
        attn_ref[rows, :] = (kq[idx][c:] * decays[idx]).astype(attn_ref.dtype)
        end = 0 if d else c - 1
        g_end = gw[end:end + 1, :]
        qd_ref[rows, :] = (cin["qf"] * eg).astype(qd_ref.dtype)
        kd_ref[rows, :] = (cin["kf"] * jnp.exp(g_end - gw)).astype(kd_ref.dtype)
        gt_ref[ch:ch + 1, :] = jnp.exp(g_end)


def gdn_wy(qkv, pf, a_scale, dtb):
    m = qkv.shape[0]
    t = GDN_WY_BLOCK
    cpb = t // GDN_CHUNK
    selg = np.zeros((2, LANES, GDN_PACK), np.float32)
    selk = np.zeros((2, LANES, GDN_VAL_WIDTH), np.float32)
    selb = np.zeros((2, LANES, GDN_VAL_WIDTH), np.float32)
    for d in range(2):
        for h in range(GDN_HEADS):
            selg[d, GDN_A_LANE + d * GDN_HEADS + h, h * GDN_CHUNK:(h + 1) * GDN_CHUNK] = 1.0
            selk[d, GDN_A_LANE + d * GDN_HEADS + h, h * HEAD_V:(h + 1) * HEAD_V] = 1.0
            selb[d, GDN_B_LANE + d * GDN_HEADS + h, h * HEAD_V:(h + 1) * HEAD_V] = 1.0
    wide = GDN_VAL_WIDTH
    out_shape, out_specs = [], []
    for _ in range(2):
        for width, dt in ((wide, F32), (wide, BF16), (GDN_PACK, BF16), (wide, BF16), (wide, BF16)):
            out_shape.append(jax.ShapeDtypeStruct((m, width), dt))
            out_specs.append(pl.BlockSpec((t, width), lambda i: (i, 0)))
        out_shape.append(jax.ShapeDtypeStruct((m // GDN_CHUNK, wide), F32))
        out_specs.append(pl.BlockSpec((cpb, wide), lambda i: (i, 0)))
    return pl.pallas_call(
        _gdn_wy_kernel,
        out_shape=tuple(out_shape),
        grid=(m // t,),
        in_specs=[
            pl.BlockSpec((t, GDN_QKV_WIDTH), lambda i: (i, 0)),
            pl.BlockSpec((t, LANES), lambda i: (i, PF_SMALL_COL // LANES)),
            pl.BlockSpec((1, LANES), lambda i: (0, 0)),
            pl.BlockSpec((1, LANES), lambda i: (0, 0)),
            pl.BlockSpec((2, LANES, GDN_PACK), lambda i: (0, 0, 0)),
            pl.BlockSpec((2, LANES, wide), lambda i: (0, 0, 0)),
            pl.BlockSpec((2, LANES, wide), lambda i: (0, 0, 0)),
        ],
        out_specs=tuple(out_specs),
        compiler_params=_params("parallel"),
        name="gdn_wy",
    )(qkv, pf, a_scale, dtb, jnp.asarray(selg, BF16), jnp.asarray(selk, BF16), jnp.asarray(selb, BF16))


GDN_PAIR = 2 * HEAD_V


def _gdn_scan_kernel(*refs):
    groups = (refs[0:6], refs[6:12])
    out_refs = refs[12:14]
    state_refs = refs[14:16]

    @pl.when(pl.program_id(1) == 0)
    def _():
        for s_ref in state_refs:
            s_ref[...] = jnp.zeros_like(s_ref)

    n_chunks = out_refs[0].shape[0] // GDN_CHUNK
    pairs = GDN_HEADS // 2
    pair_cols = [slice(p * GDN_PAIR, (p + 1) * GDN_PAIR) for p in range(pairs)]
    ri = lax.broadcasted_iota(jnp.int32, (GDN_PAIR, GDN_PAIR), 0) // HEAD_V
    ci = lax.broadcasted_iota(jnp.int32, (GDN_PAIR, GDN_PAIR), 1) // HEAD_V
    diag = ri == ci

    def body(c, carry):
        chunks = (c, n_chunks - 1 - c)
        rows = [pl.ds(pl.multiple_of(ch * GDN_CHUNK, GDN_CHUNK), GDN_CHUNK) for ch in chunks]
        states = [[s_ref[p] for p in range(pairs)] for s_ref in state_refs]
        states_b = [[s.astype(BF16) for s in st] for st in states]
        ws = [[jnp.dot(groups[g][1][rows[g], cols], states_b[g][p], preferred_element_type=F32)
               for p, cols in enumerate(pair_cols)] for g in range(2)]
        qs = [[jnp.dot(groups[g][3][rows[g], cols], states_b[g][p], preferred_element_type=F32)
               for p, cols in enumerate(pair_cols)] for g in range(2)]
        v_new = [groups[g][0][rows[g], :] - jnp.concatenate(ws[g], axis=1) for g in range(2)]
        av = [jnp.dot(groups[g][2][rows[g], :], _stack_heads(v_new[g], HEAD_V), preferred_element_type=F32)
              for g in range(2)]
        v_new_b = [v.astype(BF16) for v in v_new]
        upd = [[lax.dot_general(groups[g][4][rows[g], cols], v_new_b[g][:, cols], (((0,), (0,)), ((), ())),
                                preferred_element_type=F32) for cols in pair_cols] for g in range(2)]
        for g in range(2):
            out_refs[g][rows[g], :] = jnp.concatenate(qs[g], axis=1) + av[g]
            gt = groups[g][5][pl.ds(chunks[g], 1), :]
            for p, cols in enumerate(pair_cols):
                state_refs[g][p] = states[g][p] * gt[:, cols] + jnp.where(diag, upd[g][p], 0.0)
        return carry

    lax.fori_loop(0, n_chunks, body, 0)


def gated_deltanet_branch(pb, pf, batch, conv_w, a_log, dt_bias, norm_gain):
    m = pb.shape[0]
    nb = m // batch // LIN_BLOCK
    cpb = LIN_BLOCK // GDN_CHUNK
    qkv = gdn_prep(pb, batch, conv_w)
    n_gate = 2 * GDN_HEADS
    a_scale = jnp.zeros((1, LANES), F32).at[0, GDN_A_LANE:GDN_A_LANE + n_gate].set(
        -jnp.exp(a_log.astype(F32)).reshape(n_gate))
    dtb = jnp.zeros((1, LANES), F32).at[0, GDN_A_LANE:GDN_A_LANE + n_gate].set(dt_bias.astype(F32).reshape(n_gate))
    wy = gdn_wy(qkv, pf, a_scale, dtb)
    widths = (GDN_VAL_WIDTH, GDN_VAL_WIDTH, GDN_PACK, GDN_VAL_WIDTH, GDN_VAL_WIDTH)

    def fwd(rows, width):
        return pl.BlockSpec((rows, width), lambda b, t: (b * nb + t, 0))

    def bwd(rows, width):
        return pl.BlockSpec((rows, width), lambda b, t: (b * nb + nb - 1 - t, 0))

    in_specs = [fwd(LIN_BLOCK, w) for w in widths] + [fwd(cpb, GDN_VAL_WIDTH)]
    in_specs += [bwd(LIN_BLOCK, w) for w in widths] + [bwd(cpb, GDN_VAL_WIDTH)]
    state = pltpu.VMEM((GDN_HEADS // 2, GDN_PAIR, GDN_PAIR), F32)
    o_f, o_b = pl.pallas_call(
        _gdn_scan_kernel,
        out_shape=(jax.ShapeDtypeStruct((m, GDN_VAL_WIDTH), F32), jax.ShapeDtypeStruct((m, GDN_VAL_WIDTH), F32)),
        grid=(batch, nb),
        in_specs=in_specs,
        out_specs=(fwd(LIN_BLOCK, GDN_VAL_WIDTH), bwd(LIN_BLOCK, GDN_VAL_WIDTH)),
        scratch_shapes=[state, state],
        compiler_params=_params("parallel", "arbitrary"),
        name="gdn_scan",
    )(*wy)
    return RawBranch(o_f, o_b, "gdn_og", norm_gain, True)


def _dot3(m, x):
    hi, mid, lo = _split_bf16(x, 3)
    return (jnp.dot(m, hi, preferred_element_type=F32)
            + (jnp.dot(m, mid, preferred_element_type=F32) + jnp.dot(m, lo, preferred_element_type=F32)))


LIN_CUM_ROWS = 256
LIN_SCORE_ROWS = 128


def _chunk_causal(n, chunk, reverse):
    i = lax.broadcasted_iota(jnp.int32, (n, n), 0)
    j = lax.broadcasted_iota(jnp.int32, (n, n), 1)
    return ((i // chunk) == (j // chunk)) & ((j >= i) if reverse else (j <= i))


def _lin_intra_kernel(*refs, load_inputs, n_in, n_params, heads):
    dir_refs = (refs[:n_in], refs[n_in:2 * n_in])
    params = refs[2 * n_in:2 * n_in + n_params]
    out_refs = refs[2 * n_in + n_params:]
    c = LIN_CHUNK
    dirs = (0, 1)
    loaded = [load_inputs(dir_refs[d], slice(None), d, params) for d in dirs]
    t, w = loaded[0][0].shape
    dk = w // heads
    nc = t // c
    cums = [jnp.where(_chunk_causal(LIN_CUM_ROWS, c, d == 1), 1.0, 0.0).astype(BF16) for d in dirs]
    bs = [jnp.concatenate([_dot3(cums[d], loaded[d][3][r:r + LIN_CUM_ROWS, :])
                           for r in range(0, t, LIN_CUM_ROWS)], axis=0) for d in dirs]
    qes, kes = [], []
    for d in dirs:
        oi_ref, qd_ref, kd_ref, gt_ref = out_refs[4 * d:4 * d + 4]
        qc, kc, vc, lg = loaded[d]
        b = bs[d]
        b3 = b.reshape(nc, c, w)
        mid = c - 1 - c // 2 if d else c // 2
        end = 0 if d else c - 1
        b_mid = jnp.broadcast_to(b3[:, mid:mid + 1, :], (nc, c, w)).reshape(t, w)
        b_end = jnp.broadcast_to(b3[:, end:end + 1, :], (nc, c, w)).reshape(t, w)
        qes.append((qc * jnp.exp(b - b_mid)).astype(BF16))
        kes.append((kc * jnp.exp(b_mid - b)).astype(BF16))
        qd_ref[...] = (qc * jnp.exp(b)).astype(qd_ref.dtype)
        kd_ref[...] = (kc * jnp.exp(b_end - b)).astype(kd_ref.dtype)
        gt_ref[...] = jnp.exp(b3[:, end, :])
    keeps = [_chunk_causal(LIN_SCORE_ROWS, c, d == 1) for d in dirs]
    lane = lax.broadcasted_iota(jnp.int32, (1, LANES), 1)
    for h in range(heads):
        win = slice((h * dk) // LANES * LANES, (h * dk) // LANES * LANES + LANES)
        lo = h * dk - win.start
        vcols = slice(h * HEAD_V, (h + 1) * HEAD_V)
        tiles = [(slice(r, r + LIN_SCORE_ROWS), d) for r in range(0, t, LIN_SCORE_ROWS) for d in dirs]
        scores = []
        for rows, d in tiles:
            qh = qes[d][rows, win]
            if dk < LANES:
                qh = jnp.where((lane >= lo) & (lane < lo + dk), qh, jnp.zeros_like(qh))
            scores.append(lax.dot_general(qh, kes[d][rows, win], (((1,), (1,)), ((), ())),
                                          preferred_element_type=F32))
        probs = [jnp.where(keeps[d], s, 0.0).astype(BF16) for s, (rows, d) in zip(scores, tiles)]
        for p, (rows, d) in zip(probs, tiles):
            out_refs[4 * d][rows, vcols] = jnp.dot(p, loaded[d][2][rows, vcols], preferred_element_type=F32)


def _lin_scan_kernel(*refs, heads, chunk, unroll):
    groups = (refs[0:5], refs[5:10])
    out_refs = refs[10:12]
    state_refs = refs[12:14]

    @pl.when(pl.program_id(1) == 0)
    def _():
        for s_ref in state_refs:
            s_ref[...] = jnp.zeros_like(s_ref)

    n_chunks = out_refs[0].shape[0] // chunk
    w = state_refs[0].shape[1]
    dk = w // heads
    lane = lax.broadcasted_iota(jnp.int32, (1, w), 1)
    masks = [(lane >= h * dk) & (lane < (h + 1) * dk) for h in range(heads)]

    def stack(x):
        return jnp.concatenate([jnp.where(m, x, jnp.zeros_like(x)) for m in masks], axis=0)

    def body(it, carry):
        steps = []
        for u in range(unroll):
            c = it * unroll + u
            steps += [(0, c), (1, n_chunks - 1 - c)]
        prepared = []
        for g, ch in steps:
            rows = pl.ds(pl.multiple_of(ch * chunk, chunk), chunk)
            oi_ref, qd_ref, kd_ref, v_ref, gt_ref = groups[g]
            vc = v_ref[rows, :]
            v4 = jnp.concatenate([vc[:, h * HEAD_V:(h + 1) * HEAD_V] for h in range(heads)], axis=0)
            upd = lax.dot_general(v4, stack(kd_ref[rows, :]), (((0,), (0,)), ((), ())),
                                  preferred_element_type=F32)
            prepared.append((rows, stack(qd_ref[rows, :]), upd, gt_ref[pl.ds(ch, 1), :]))
        states = [s_ref[...] for s_ref in state_refs]
        for (g, ch), (rows, q4, upd, gt) in zip(steps, prepared):
            o_inter = lax.dot_general(q4, states[g].astype(BF16), (((1,), (1,)), ((), ())),
                                      preferred_element_type=F32)
            out_refs[g][rows, :] = groups[g][0][rows, :] + jnp.concatenate(
                [o_inter[h * chunk:(h + 1) * chunk, :] for h in range(heads)], axis=1)
            states[g] = states[g] * gt + upd
        for s_ref, st in zip(state_refs, states):
            s_ref[...] = st
        return carry

    lax.fori_loop(0, n_chunks // unroll, body, 0)


def _bidir_lin_call(name, load_inputs, arrays, col_blocks, widths, params, batch, heads, key_width, v_col):
    m = arrays[0].shape[0]
    t = LIN_BLOCK
    nb = m // batch // t
    cpb = t // LIN_CHUNK
    out_w = heads * HEAD_V
    n_in = len(arrays)

    in_specs, operands = [], []
    for d in range(2):
        for a, wd, cb in zip(arrays, widths, col_blocks):
            in_specs.append(pl.BlockSpec((t, wd), functools.partial(lambda i, c: (i, c), c=cb[d])))
            operands.append(a)
    for p in params:
        in_specs.append(pl.BlockSpec(p.shape, functools.partial(lambda i, nd: (0,) * nd, nd=p.ndim)))
    out_shape, out_specs = [], []
    for _ in range(2):
        for rows_total, rows_blk, width, dt in ((m, t, out_w, F32), (m, t, key_width, BF16),
                                                (m, t, key_width, BF16), (m // LIN_CHUNK, cpb, key_width, F32)):
            out_shape.append(jax.ShapeDtypeStruct((rows_total, width), dt))
            out_specs.append(pl.BlockSpec((rows_blk, width), lambda i: (i, 0)))

    intra = pl.pallas_call(
        functools.partial(_lin_intra_kernel, load_inputs=load_inputs, n_in=n_in, n_params=len(params),
                          heads=heads),
        out_shape=tuple(out_shape),
        grid=(m // t,),
        in_specs=in_specs,
        out_specs=tuple(out_specs),
        compiler_params=_params("parallel"),
        name=name + "_intra",
    )(*operands, *params)

    def fwd(rows, width, col=0):
        return pl.BlockSpec((rows, width), lambda b, s: (b * nb + s, col))

    def bwd(rows, width, col=0):
        return pl.BlockSpec((rows, width), lambda b, s: (b * nb + nb - 1 - s, col))

    scan_specs, scan_ops = [], []
    for d, mk in enumerate((fwd, bwd)):
        oi, qd, kd, gt = intra[4 * d:4 * d + 4]
        scan_specs += [mk(t, out_w), mk(t, key_width), mk(t, key_width), mk(t, out_w, v_col), mk(cpb, key_width)]
        scan_ops += [oi, qd, kd, arrays[0], gt]
    state = pltpu.VMEM((HEAD_V, key_width), F32)
    return pl.pallas_call(
        functools.partial(_lin_scan_kernel, heads=heads, chunk=LIN_CHUNK, unroll=4),
        out_shape=(jax.ShapeDtypeStruct((m, out_w), F32), jax.ShapeDtypeStruct((m, out_w), F32)),
        grid=(batch, nb),
        in_specs=scan_specs,
        out_specs=(fwd(t, out_w), bwd(t, out_w)),
        scratch_shapes=[state, state],
        compiler_params=_params("parallel", "arbitrary"),
        name=name + "_scan",
    )(*scan_ops)


def gla_branch(pb, pf, batch, w_gate_up, b_gate, norm_gain):
    wpad = jnp.zeros((2, LANES, GLA_KEY_WIDTH), F32)
    for d in range(2):
        wpad = wpad.at[d, d * GLA_GATE_RANK:(d + 1) * GLA_GATE_RANK, :].set(w_gate_up[d].astype(F32))
    bias = b_gate.astype(F32).reshape(2, 1, GLA_KEY_WIDTH)
    v_col = PB_COL["gla_v"] // GLA_VAL_WIDTH
    cols = [(PB_COL["gla_q"] // GLA_KEY_WIDTH,) * 2, (PB_COL["gla_k"] // GLA_KEY_WIDTH,) * 2,
            (v_col,) * 2, (PF_SMALL_COL // LANES,) * 2]
    o_f, o_b = _bidir_lin_call("gla", _gla_inputs, [pb, pb, pb, pf], cols,
                               [GLA_KEY_WIDTH, GLA_KEY_WIDTH, GLA_VAL_WIDTH, LANES], [wpad, bias],
                               batch, GLA_HEADS, GLA_KEY_WIDTH, v_col)
    return RawBranch(o_f, o_b, "gla_og", norm_gain, True)


def hgrn2_branch(pb, pf, batch, lower_bound, norm_gain):
    lb = lower_bound.astype(F32).reshape(2, 1, HGRN_KEY_WIDTH)
    log_lb = jnp.log(jnp.maximum(lb, LB_FLOOR))
    log1m_lb = jnp.log1p(-lb)
    zc = PF_COL["hg_f"] // HGRN_KEY_WIDTH
    v_col = PB_COL["hg_i"] // HGRN_VAL_WIDTH
    cols = [(PB_COL["hg_q"] // HGRN_KEY_WIDTH,) * 2, (v_col,) * 2, (zc, zc + 1)]
    o_f, o_b = _bidir_lin_call("hgrn2", _hgrn_inputs, [pb, pb, pf], cols,
                               [HGRN_KEY_WIDTH, HGRN_VAL_WIDTH, HGRN_KEY_WIDTH], [lb, log_lb, log1m_lb],
                               batch, HGRN_HEADS, HGRN_KEY_WIDTH, v_col)
    return RawBranch(o_f, o_b, "hg_og", norm_gain, False)


def kernel(x, mem, g_mix, w_in, na_q_gain, na_k_gain, na_rel_bias, gla_w_gate_up, gla_b_gate, gla_norm_gain, gdn_conv_w, gdn_a_log, gdn_dt_bias, gdn_norm_gain, hgrn_lb_raw, hgrn_norm_gain, g_mem, w_mem_kv, mem_q_gain, mem_k_gain, w_branch, w_out, g_ffn, ffn_w_gate, ffn_w_up, ffn_w_down, moe_w_router, moe_b_router, moe_w_gate, moe_w_up, moe_w_down):
    B, S, D = x.shape
    n_tok = B * S
    lb_w = jax.nn.softmax(hgrn_lb_raw.astype(F32), axis=0)
    hgrn_lb = jnp.cumsum(lb_w, axis=0) - lb_w[0:1]
    x2 = x.reshape(n_tok, D)
    mem2 = mem.reshape(B * mem.shape[1], D)
    for layer in range(DEPTH):
        pb, pf = in_projection(x2, g_mix[layer], _rearrange_w_in(w_in[layer]), PB_WIDTH)
        kv = rms_matmul(mem2, g_mem[layer], w_mem_kv[layer].astype(BF16), tm=mem2.shape[0], tn=512,
                        out_dtype=BF16)
        branches = [
            neighbourhood_attention(pb, B, na_q_gain[layer], na_k_gain[layer], na_rel_bias[layer]),
            gla_branch(pb, pf, B, gla_w_gate_up[layer], gla_b_gate[layer], gla_norm_gain[layer]),
            gated_deltanet_branch(pb, pf, B, gdn_conv_w[layer], gdn_a_log[layer], gdn_dt_bias[layer],
                                  gdn_norm_gain[layer]),
            hgrn2_branch(pb, pf, B, hgrn_lb[layer], hgrn_norm_gain[layer]),
            memory_cross_attention(pb, kv, B, mem_q_gain[layer], mem_k_gain[layer]),
        ]
        merged = merge_branches(branches, pb, w_branch[layer].astype(BF16), tm=512, tn=512)
        x2 = matmul_residual(merged, w_out[layer].astype(BF16), x2, tm=1024, tn=512)

        j = layer // 2
        if layer % 2 == 0:
            act = rms_swiglu_up(x2, g_ffn[layer], ffn_w_gate[j].astype(BF16), ffn_w_up[j].astype(BF16),
                                tm=1024, tn=512)
            x2 = matmul_residual(act, ffn_w_down[j].astype(BF16), x2, tm=512, tn=512)
        else:
            x2 = moe_layer(x2, g_ffn[layer], moe_w_router[j], moe_b_router[j], moe_w_gate[j], moe_w_up[j],
                           moe_w_down[j])
    return x2.reshape(B, S, D)
```

```python
import functools

import jax
import jax.numpy as jnp
import numpy as np
from jax import lax
from jax.experimental import pallas as pl
from jax.experimental.pallas import tpu as pltpu

F32 = jnp.float32
BF16 = jnp.bfloat16

D_MODEL = 2048
DEPTH = 2
RMS_EPS = 1e-6
MASK_VALUE = -1e30
LB_FLOOR = 1e-30
GRID_W = 64

NA_HEADS = 8
NA_HEAD_DIM = 64
NA_WIDTH = 512
NA_WIN_ROWS = 8
NA_WIN_COLS = 16

GLA_HEADS = 4
GLA_HEAD_K = 64
GLA_HEAD_V = 128
GLA_KEY_WIDTH = 256
GLA_VAL_WIDTH = 512
GLA_GATE_RANK = 16
GLA_GATE_NORMALIZER = 16.0

GDN_HEADS = 4
GDN_HEAD_K = 128
GDN_HEAD_V = 128
GDN_KEY_WIDTH = 512
GDN_VAL_WIDTH = 512
GDN_CHUNK = 64

HGRN_HEADS = 4
HGRN_HEAD_K = 128
HGRN_HEAD_V = 128
HGRN_KEY_WIDTH = 512
HGRN_VAL_WIDTH = 512

LIN_CHUNK = 32

MEM_HEADS = 4
MEM_HEAD_DIM = 128
MEM_WIDTH = 512

N_BRANCH = 5
BRANCH_WIDTH = 512
N_EXPERTS = 8
MOE_TOP_K = 2

IN_WIDTHS = (
    NA_WIDTH, NA_WIDTH, NA_WIDTH,
    GLA_KEY_WIDTH, GLA_KEY_WIDTH, GLA_VAL_WIDTH,
    2 * GLA_GATE_RANK, GLA_VAL_WIDTH,
    2 * GDN_KEY_WIDTH + GDN_VAL_WIDTH,
    2 * GDN_HEADS, 2 * GDN_HEADS, GDN_VAL_WIDTH,
    HGRN_KEY_WIDTH, 2 * HGRN_KEY_WIDTH, HGRN_VAL_WIDTH, HGRN_VAL_WIDTH,
    MEM_WIDTH,
    N_BRANCH * D_MODEL,
)
P_IN = sum(IN_WIDTHS)

V7X_VMEM_BYTES = 64 * 1024 * 1024
VMEM_LIMIT_BYTES = V7X_VMEM_BYTES - 8 * 1024 * 1024
LANES = 128


def _params(*semantics):
    return pltpu.CompilerParams(dimension_semantics=semantics, vmem_limit_bytes=VMEM_LIMIT_BYTES)


def _sigmoid(x):
    return 0.5 * jnp.tanh(0.5 * x) + 0.5


def _rms_norm_rows(x, gain):
    ms = jnp.mean(x * x, axis=-1, keepdims=True)
    return x * lax.rsqrt(ms + RMS_EPS) * gain


def _rms_matmul_kernel(x_ref, g_ref, w_ref, o_ref, h_ref):
    @pl.when(pl.program_id(1) == 0)
    def _():
        h_ref[...] = _rms_norm_rows(x_ref[...], g_ref[...]).astype(BF16)

    o_ref[...] = jnp.dot(h_ref[...], w_ref[...], preferred_element_type=F32).astype(o_ref.dtype)


def rms_matmul(x, gain, w, *, tm, tn, out_dtype=F32):
    m, k = x.shape
    n = w.shape[1]
    return pl.pallas_call(
        _rms_matmul_kernel,
        out_shape=jax.ShapeDtypeStruct((m, n), out_dtype),
        grid=(m // tm, n // tn),
        in_specs=[
            pl.BlockSpec((tm, k), lambda i, j: (i, 0)),
            pl.BlockSpec((1, k), lambda i, j: (0, 0)),
            pl.BlockSpec((k, tn), lambda i, j: (0, j)),
        ],
        out_specs=pl.BlockSpec((tm, tn), lambda i, j: (i, j)),
        scratch_shapes=[pltpu.VMEM((tm, k), BF16)],
        compiler_params=_params("parallel", "arbitrary"),
        name="rms_matmul",
    )(x, gain.reshape(1, k), w)


IN_PROJ_TILE = 512


def _in_projection_kernel(x_ref, g_ref, w_ref, ob_ref, of_ref, h_ref, *, n_plain_tiles, n_bf16_tiles):
    j = pl.program_id(1)

    @pl.when(j == 0)
    def _():
        h_ref[...] = _rms_norm_rows(x_ref[...], g_ref[...]).astype(BF16)

    r = jnp.dot(h_ref[...], w_ref[...], preferred_element_type=F32)

    @pl.when(j < n_plain_tiles)
    def _():
        ob_ref[...] = r.astype(ob_ref.dtype)

    @pl.when(jnp.logical_and(j >= n_plain_tiles, j < n_bf16_tiles))
    def _():
        ob_ref[...] = _sigmoid(r).astype(ob_ref.dtype)

    @pl.when(j >= n_bf16_tiles)
    def _():
        of_ref[...] = r


def in_projection(x, gain, w, n_plain, n_bf16, *, tm=1024):
    m, k = x.shape
    tn = IN_PROJ_TILE
    nb = n_bf16 // tn
    nf = (w.shape[1] - n_bf16) // tn
    return pl.pallas_call(
        functools.partial(_in_projection_kernel, n_plain_tiles=n_plain // tn, n_bf16_tiles=nb),
        out_shape=(jax.ShapeDtypeStruct((m, nb * tn), BF16), jax.ShapeDtypeStruct((m, nf * tn), F32)),
        grid=(m // tm, nb + nf),
        in_specs=[
            pl.BlockSpec((tm, k), lambda i, j: (i, 0)),
            pl.BlockSpec((1, k), lambda i, j: (0, 0)),
            pl.BlockSpec((k, tn), lambda i, j: (0, j)),
        ],
        out_specs=(pl.BlockSpec((tm, tn), lambda i, j: (i, jnp.minimum(j, nb - 1))),
                   pl.BlockSpec((tm, tn), lambda i, j: (i, jnp.maximum(j - nb, 0)))),
        scratch_shapes=[pltpu.VMEM((tm, k), BF16)],
        compiler_params=_params("parallel", "arbitrary"),
        name="in_projection",
    )(x, gain.reshape(1, k), w)


def _rms_swiglu_kernel(x_ref, g_ref, wg_ref, wu_ref, o_ref, h_ref):
    @pl.when(pl.program_id(1) == 0)
    def _():
        h_ref[...] = _rms_norm_rows(x_ref[...], g_ref[...]).astype(BF16)

    h = h_ref[...]
    a = jnp.dot(h, wg_ref[...], preferred_element_type=F32)
    b = jnp.dot(h, wu_ref[...], preferred_element_type=F32)
    o_ref[...] = (a * _sigmoid(a) * b).astype(o_ref.dtype)


def rms_swiglu_up(x, gain, wg, wu, *, tm, tn):
    m, k = x.shape
    n = wg.shape[1]
    return pl.pallas_call(
        _rms_swiglu_kernel,
        out_shape=jax.ShapeDtypeStruct((m, n), BF16),
        grid=(m // tm, n // tn),
        in_specs=[
            pl.BlockSpec((tm, k), lambda i, j: (i, 0)),
            pl.BlockSpec((1, k), lambda i, j: (0, 0)),
            pl.BlockSpec((k, tn), lambda i, j: (0, j)),
            pl.BlockSpec((k, tn), lambda i, j: (0, j)),
        ],
        out_specs=pl.BlockSpec((tm, tn), lambda i, j: (i, j)),
        scratch_shapes=[pltpu.VMEM((tm, k), BF16)],
        compiler_params=_params("parallel", "arbitrary"),
        name="rms_swiglu_up",
    )(x, gain.reshape(1, k), wg, wu)


def _matmul_residual_kernel(a_ref, w_ref, r_ref, o_ref):
    o_ref[...] = r_ref[...] + jnp.dot(a_ref[...], w_ref[...], preferred_element_type=F32)


def matmul_residual(a, w, res, *, tm, tn):
    m, k = a.shape
    n = w.shape[1]
    return pl.pallas_call(
        _matmul_residual_kernel,
        out_shape=jax.ShapeDtypeStruct((m, n), F32),
        grid=(m // tm, n // tn),
        in_specs=[
            pl.BlockSpec((tm, k), lambda i, j: (i, 0)),
            pl.BlockSpec((k, tn), lambda i, j: (0, j)),
            pl.BlockSpec((tm, tn), lambda i, j: (i, j)),
        ],
        out_specs=pl.BlockSpec((tm, tn), lambda i, j: (i, j)),
        compiler_params=_params("parallel", "arbitrary"),
        name="matmul_residual",
    )(a, w, res)


class RawBranch:
    def __init__(self, o_fwd, o_bwd, og_name, gain, silu_gate):
        self.o_fwd, self.o_bwd, self.og_name, self.gain, self.silu_gate = o_fwd, o_bwd, og_name, gain, silu_gate


def _merge_kernel(*refs, raw):
    pos = 0
    br = []
    for kind in raw:
        width = 1 if kind is None else 4
        br.append(refs[pos:pos + width])
        pos += width
    gl_refs = refs[pos:pos + N_BRANCH]
    wb_ref, o_ref, fin_ref = refs[pos + N_BRANCH:pos + N_BRANCH + 3]
    raw_slot = {n: s for s, n in enumerate(n for n, kind in enumerate(raw) if kind is not None)}

    @pl.when(pl.program_id(1) == 0)
    def _():
        for n, slot in raw_slot.items():
            of_ref, ob_ref, og_ref, gain_ref = br[n]
            for h in range(BRANCH_WIDTH // LANES):
                cols = slice(h * LANES, (h + 1) * LANES)
                y = _rms_norm_rows(of_ref[:, cols] + ob_ref[:, cols], gain_ref[...])
                g = og_ref[:, cols].astype(F32)
                gate = _sigmoid(g)
                if raw[n]:
                    gate = g * gate
                fin_ref[slot, :, cols] = (y * gate).astype(fin_ref.dtype)

    acc = None
    for n in range(N_BRANCH):
        b = br[n][0][...] if raw[n] is None else fin_ref[raw_slot[n]]
        y = jnp.dot(b, wb_ref[n], preferred_element_type=F32)
        t = gl_refs[n][...].astype(F32) * y
        acc = t if acc is None else acc + t
    o_ref[...] = acc.astype(o_ref.dtype)


def merge_branches(branches, pb, w_branch, *, tm, tn):
    m = pb.shape[0]
    d = D_MODEL
    tiles_per_branch = d // tn
    tile0 = PB_COL["gates"] // tn
    row_block = pl.BlockSpec((tm, BRANCH_WIDTH), lambda i, j: (i, 0))
    in_specs, operands, raw = [], [], []
    for b in branches:
        if isinstance(b, RawBranch):
            og_col = PB_COL[b.og_name] // BRANCH_WIDTH
            in_specs += [row_block, row_block,
                         pl.BlockSpec((tm, BRANCH_WIDTH), functools.partial(lambda i, j, c: (i, c), c=og_col)),
                         pl.BlockSpec((1, LANES), lambda i, j: (0, 0))]
            operands += [b.o_fwd, b.o_bwd, pb, b.gain.astype(F32).reshape(1, LANES)]
            raw.append(b.silu_gate)
        else:
            in_specs.append(row_block)
            operands.append(b)
            raw.append(None)
    in_specs += [
        pl.BlockSpec((tm, tn), functools.partial(lambda i, j, n: (i, tile0 + n * tiles_per_branch + j), n=n))
        for n in range(N_BRANCH)
    ]
    in_specs += [pl.BlockSpec((N_BRANCH, BRANCH_WIDTH, tn), lambda i, j: (0, 0, j))]
    n_raw = sum(kind is not None for kind in raw)
    return pl.pallas_call(
        functools.partial(_merge_kernel, raw=tuple(raw)),
        out_shape=jax.ShapeDtypeStruct((m, d), BF16),
        grid=(m // tm, d // tn),
        in_specs=in_specs,
        out_specs=pl.BlockSpec((tm, tn), lambda i, j: (i, j)),
        scratch_shapes=[pltpu.VMEM((max(n_raw, 1), tm, BRANCH_WIDTH), BF16)],
        compiler_params=_params("parallel", "arbitrary"),
        name="merge_branches",
    )(*operands, *([pb] * N_BRANCH), w_branch)


def _router_kernel(x_ref, g_ref, w_ref, b_ref, o_ref, h_ref, cnt_ref, run_ref, *, n_experts):
    @pl.when(pl.program_id(0) == 0)
    def _():
        run_ref[...] = jnp.zeros_like(run_ref)

    h = _rms_norm_rows(x_ref[...], g_ref[...])
    h_ref[...] = h.astype(h_ref.dtype)
    logits = _dot_f32(h, w_ref[...]) + b_ref[...]
    lane = lax.broadcasted_iota(jnp.int32, logits.shape, 1).astype(F32)
    neg = -jnp.inf
    lm = jnp.where(lane < n_experts, logits, neg)
    m1 = jnp.max(lm, axis=-1, keepdims=True)
    i1 = jnp.min(jnp.where(lm == m1, lane, float(LANES)), axis=-1, keepdims=True)
    lm2 = jnp.where(lane == i1, neg, lm)
    m2 = jnp.max(lm2, axis=-1, keepdims=True)
    i2 = jnp.min(jnp.where(lm2 == m2, lane, float(LANES)), axis=-1, keepdims=True)
    t = jnp.exp(m2 - m1)
    den = 1.0 + t

    tm = logits.shape[0]
    before = (lax.broadcasted_iota(jnp.int32, (tm, tm), 1)
              < lax.broadcasted_iota(jnp.int32, (tm, tm), 0))
    before = jnp.where(before, 1.0, 0.0).astype(BF16)
    pick1 = lane == i1
    pick2 = lane == i2
    oh1 = jnp.where(pick1, 1.0, 0.0)
    oh2 = jnp.where(pick2, 1.0, 0.0)
    pre1 = jnp.dot(before, oh1.astype(BF16), preferred_element_type=F32)
    pre2 = jnp.dot(before, oh2.astype(BF16), preferred_element_type=F32)
    tot1 = jnp.sum(oh1, axis=0, keepdims=True)
    tot2 = jnp.sum(oh2, axis=0, keepdims=True)
    run = run_ref[...]
    rank1 = jnp.sum(jnp.where(pick1, pre1 + run, 0.0), axis=-1, keepdims=True)
    rank2 = jnp.sum(jnp.where(pick2, pre2 + (run + tot1), 0.0), axis=-1, keepdims=True)
    run = run + tot1 + tot2
    run_ref[...] = run
    cnt_ref[...] = jnp.broadcast_to(run, cnt_ref.shape)

    out = jnp.where(lane == 0, 1.0 / den, jnp.where(lane == 1, t / den, jnp.where(lane == 2, i1, i2)))
    out = jnp.where(lane == 4, rank1, jnp.where(lane == 5, rank2, out))
    o_ref[...] = jnp.where(lane < 6, out, 0.0)


def router_top2(x, gain, w_router, b_router, *, tm=512):
    m, k = x.shape
    e = w_router.shape[1]
    w_pad = jnp.zeros((k, LANES), F32).at[:, :e].set(w_router.astype(F32))
    b_pad = jnp.zeros((1, LANES), F32).at[0, :e].set(b_router.astype(F32))
    route, h, cnt = pl.pallas_call(
        functools.partial(_router_kernel, n_experts=e),
        out_shape=(jax.ShapeDtypeStruct((m, LANES), F32), jax.ShapeDtypeStruct((m, k), BF16),
                   jax.ShapeDtypeStruct((8, LANES), F32)),
        grid=(m // tm,),
        in_specs=[
            pl.BlockSpec((tm, k), lambda i: (i, 0)),
            pl.BlockSpec((1, k), lambda i: (0, 0)),
            pl.BlockSpec((k, LANES), lambda i: (0, 0)),
            pl.BlockSpec((1, LANES), lambda i: (0, 0)),
        ],
        out_specs=(pl.BlockSpec((tm, LANES), lambda i: (i, 0)), pl.BlockSpec((tm, k), lambda i: (i, 0)),
                   pl.BlockSpec((8, LANES), lambda i: (0, 0))),
        scratch_shapes=[pltpu.VMEM((1, LANES), F32)],
        compiler_params=_params("arbitrary"),
        name="router_top2",
    )(x, gain.reshape(1, k), w_pad, b_pad)
    return route, h, cnt[0, :e].astype(jnp.int32)


MOE_TILE = 1024
MOE_SUB = 256
MOE_FF_TILE = 512


def _moe_kernel(tile_e_ref, tile_rows_ref, n_used_ref, x_ref, wg_ref, wu_ref, wd_ref, o_ref, acc_ref):
    i = pl.program_id(0)
    j = pl.program_id(1)
    last = pl.num_programs(1) - 1
    valid = tile_rows_ref[i]
    n_sub = (valid + (MOE_SUB - 1)) // MOE_SUB

    for k in range(1, MOE_TILE // MOE_SUB + 1):
        rows = slice(0, k * MOE_SUB)

        @pl.when(n_sub == k)
        def _(rows=rows):
            x = x_ref[rows, :]
            a = jnp.dot(x, wg_ref[0].astype(BF16), preferred_element_type=F32)
            b = jnp.dot(x, wu_ref[0].astype(BF16), preferred_element_type=F32)
            act = (a * _sigmoid(a) * b).astype(BF16)
            part = jnp.dot(act, wd_ref[0].astype(BF16), preferred_element_type=F32)

            @pl.when(j == 0)
            def _():
                acc_ref[rows, :] = part

            @pl.when(j > 0)
            def _():
                acc_ref[rows, :] += part

    for s in range(0, MOE_TILE, MOE_SUB):
        rows = slice(s, s + MOE_SUB)
        filled = s < valid

        @pl.when(jnp.logical_and(filled, j == last))
        def _(rows=rows):
            o_ref[rows, :] = acc_ref[rows, :].astype(o_ref.dtype)

        @pl.when(jnp.logical_and(jnp.logical_not(filled), j == last))
        def _(rows=rows):
            o_ref[rows, :] = jnp.zeros((MOE_SUB, o_ref.shape[1]), o_ref.dtype)


def moe_experts(xb, tile_e, tile_rows, n_used, wg, wu, wd):
    rows, d = xb.shape
    ff = wg.shape[2]
    tm, tf = MOE_TILE, MOE_FF_TILE
    n_tiles = rows // tm
    last_j = ff // tf - 1

    def x_map(i, j, te, tr, nu):
        return (jnp.minimum(i, nu[0] - 1), 0)

    def up_map(i, j, te, tr, nu):
        return (te[i], 0, jnp.where(i < nu[0], j, last_j))

    def down_map(i, j, te, tr, nu):
        return (te[i], jnp.where(i < nu[0], j, last_j), 0)

    grid_spec = pltpu.PrefetchScalarGridSpec(
        num_scalar_prefetch=3,
        grid=(n_tiles, ff // tf),
        in_specs=[
            pl.BlockSpec((tm, d), x_map, pipeline_mode=pl.Buffered(1)),
            pl.BlockSpec((1, d, tf), up_map),
            pl.BlockSpec((1, d, tf), up_map),
            pl.BlockSpec((1, tf, d), down_map),
        ],
        out_specs=pl.BlockSpec((tm, d), lambda i, j, te, tr, nu: (i, 0)),
        scratch_shapes=[pltpu.VMEM((tm, d), F32)],
    )
    return pl.pallas_call(
        _moe_kernel,
        out_shape=jax.ShapeDtypeStruct((rows, d), BF16),
        grid_spec=grid_spec,
        compiler_params=_params("arbitrary", "arbitrary"),
        name="moe_experts",
    )(tile_e, tile_rows, n_used, xb, wg, wu, wd)


def _moe_combine_kernel(x_ref, y0_ref, y1_ref, r_ref, o_ref):
    w = r_ref[...]
    o_ref[...] = x_ref[...] + w[:, 0:1] * y0_ref[...].astype(F32) + w[:, 1:2] * y1_ref[...].astype(F32)


def moe_combine(x2d, y0, y1, route, *, tm=512):
    n, d = x2d.shape
    row_block = pl.BlockSpec((tm, d), lambda i: (i, 0))
    return pl.pallas_call(
        _moe_combine_kernel,
        out_shape=jax.ShapeDtypeStruct((n, d), F32),
        grid=(n // tm,),
        in_specs=[row_block, row_block, row_block, pl.BlockSpec((tm, LANES), lambda i: (i, 0))],
        out_specs=row_block,
        compiler_params=_params("parallel"),
        name="moe_combine",
    )(x2d, y0, y1, route)


def moe_layer(x2d, gain, w_router, b_router, wg, wu, wd):
    n, d = x2d.shape
    e = N_EXPERTS
    route, h, counts = router_top2(x2d, gain, w_router, b_router)
    nk = n * MOE_TOP_K
    n_tiles = -(-nk // MOE_TILE) + e
    flat_e = route[:, 2:2 + MOE_TOP_K].astype(jnp.int32).reshape(nk)
    rank = route[:, 4:4 + MOE_TOP_K].astype(jnp.int32).reshape(nk)
    flat_tok = jnp.repeat(jnp.arange(n, dtype=jnp.int32), MOE_TOP_K)
    padded = (counts + MOE_TILE - 1) // MOE_TILE * MOE_TILE
    pad_end = jnp.cumsum(padded)
    pad_start = pad_end - padded
    slot = (pad_start[flat_e] + rank).astype(jnp.int32)
    n_slots = n_tiles * MOE_TILE
    slot_tok = (jnp.arange(n_slots, dtype=jnp.int32) % n).at[slot].set(flat_tok)
    tile_start = jnp.arange(n_tiles, dtype=jnp.int32) * MOE_TILE
    tile_e = jnp.minimum(jnp.searchsorted(pad_end, tile_start, side="right"), e - 1).astype(jnp.int32)
    tile_rows = jnp.clip(pad_start[tile_e] + counts[tile_e] - tile_start, 0, MOE_TILE).astype(jnp.int32)
    tile_rows = jnp.where(tile_start < pad_end[-1], tile_rows, 0)
    n_used = (pad_end[-1] // MOE_TILE).astype(jnp.int32).reshape(1)
    tile_e = jnp.where(tile_start < pad_end[-1], tile_e, tile_e[jnp.maximum(n_used[0] - 1, 0)])

    xb = h[slot_tok]
    yb = moe_experts(xb, tile_e, tile_rows, n_used, wg, wu, wd)
    slot2 = slot.reshape(n, MOE_TOP_K)
    return moe_combine(x2d, yb[slot2[:, 0]], yb[slot2[:, 1]], route)


_SRC = dict(zip(
    ("na_q", "na_k", "na_v", "gla_q", "gla_k", "gla_v", "gla_lr", "gla_og", "gdn_qkv", "gdn_a", "gdn_b",
     "gdn_og", "hg_q", "hg_f", "hg_i", "hg_og", "mem_q", "gates"),
    zip(np.cumsum((0,) + IN_WIDTHS[:-1]).tolist(), IN_WIDTHS)))
_PB_ORDER = ("na_q", "na_k", "na_v", "gla_q", "gla_k", "gla_v", "gla_og", "gdn_qkv", "gdn_og", "hg_q", "hg_i",
             "hg_og", "mem_q", "gates")
_PF_ORDER = ("hg_f", "gla_lr", "gdn_a", "gdn_b")
PB_COL = {}
_c = 0
for _name in _PB_ORDER:
    PB_COL[_name] = _c
    _c += _SRC[_name][1]
PB_WIDTH = _c
PF_COL = {}
_c = 0
for _name in _PF_ORDER:
    PF_COL[_name] = _c
    _c += _SRC[_name][1]
PF_WIDTH = -(-_c // IN_PROJ_TILE) * IN_PROJ_TILE
PF_SMALL_COL = PF_COL["gla_lr"]
GDN_A_LANE = PF_COL["gdn_a"] - PF_SMALL_COL
GDN_B_LANE = PF_COL["gdn_b"] - PF_SMALL_COL


def _rearrange_w_in(w):
    w = w.astype(BF16)
    cols = [w[:, _SRC[n][0]:_SRC[n][0] + _SRC[n][1]] for n in _PB_ORDER + _PF_ORDER]
    cols.append(jnp.zeros((w.shape[0], PB_WIDTH + PF_WIDTH - P_IN), BF16))
    return jnp.concatenate(cols, axis=1)


def _segment_rms(x, gain, seg_ones, seg_width):
    sq = x * x
    hi = sq.astype(BF16)
    lo = (sq - hi.astype(F32)).astype(BF16)
    ss = (jnp.dot(hi, seg_ones, preferred_element_type=F32)
          + jnp.dot(lo, seg_ones, preferred_element_type=F32))
    return x * lax.rsqrt(ss * (1.0 / seg_width) + RMS_EPS) * gain


NA_ROWS_PER_STEP = 8
NA_BAND = NA_WIN_ROWS * GRID_W


def _na_bias_table(rel_bias):
    c = np.arange(GRID_W)
    dc = np.clip(c[None, :] - c[:, None], 1 - NA_WIN_COLS, NA_WIN_COLS - 1) + (NA_WIN_COLS - 1)
    col_start = np.clip(c - NA_WIN_COLS // 2, 0, GRID_W - NA_WIN_COLS)
    col_in = (c[None, :] >= col_start[:, None]) & (c[None, :] < col_start[:, None] + NA_WIN_COLS)
    onehot = (dc[None] == np.arange(2 * NA_WIN_COLS - 1)[:, None, None]).astype(np.float32)
    base = jnp.einsum("hrc,cqk->hrqk", rel_bias.astype(F32), onehot, precision=lax.Precision.HIGHEST)
    base = jnp.where(col_in[None, None], base, MASK_VALUE)
    tables = []
    for cfg in range(NA_WIN_ROWS):
        rows = base[:, NA_WIN_ROWS - 1 - cfg:2 * NA_WIN_ROWS - 1 - cfg]
        tables.append(rows.transpose(0, 2, 1, 3).reshape(NA_HEADS, GRID_W, NA_BAND))
    return jnp.stack(tables)


def _na_kernel(q_ref, k_ref, v_ref, qg_ref, kg_ref, seg_ref, bias_ref, o_ref, kn_ref):
    step = pl.program_id(1)
    rows_total = k_ref.shape[0] // GRID_W
    seg = seg_ref[...]

    @pl.when(step == 0)
    def _():
        def norm_keys(t, carry):
            rows = pl.ds(pl.multiple_of(t * 256, 256), 256)
            kn_ref[rows, :] = _segment_rms(k_ref[rows, :].astype(F32), kg_ref[...], seg, NA_HEAD_DIM).astype(BF16)
            return carry
        lax.fori_loop(0, k_ref.shape[0] // 256, norm_keys, 0)

    lane = lax.broadcasted_iota(jnp.int32, (1, LANES), 1)
    low_half = lane < NA_HEAD_DIM

    def one_row(rr, carry):
        r = step * NA_ROWS_PER_STEP + rr
        row_start = jnp.clip(r - NA_WIN_ROWS // 2, 0, rows_total - NA_WIN_ROWS)
        cfg = r - row_start
        qrows = pl.ds(pl.multiple_of(rr * GRID_W, GRID_W), GRID_W)
        band = pl.ds(pl.multiple_of(row_start * GRID_W, GRID_W), NA_BAND)
        qn = (_segment_rms(q_ref[qrows, :].astype(F32), qg_ref[...], seg, NA_HEAD_DIM)
              * (NA_HEAD_DIM ** -0.5)).astype(BF16)
        heads = [(pair, half) for pair in range(NA_HEADS // 2) for half in range(2)]
        scores = []
        for pair, half in heads:
            cols = slice(pair * LANES, (pair + 1) * LANES)
            qp = qn[:, cols]
            keep = low_half if half == 0 else jnp.logical_not(low_half)
            qm = jnp.where(keep, qp, jnp.zeros_like(qp))
            scores.append(lax.dot_general(qm, kn_ref[band, cols], (((1,), (1,)), ((), ())),
                                          preferred_element_type=F32))
        exps, sums = [], []
        for s, (pair, half) in zip(scores, heads):
            s = s + bias_ref[cfg, 2 * pair + half]
            e = jnp.exp(s - jnp.max(s, axis=-1, keepdims=True))
            sums.append(jnp.sum(e, axis=-1, keepdims=True))
            exps.append(e.astype(BF16))
        outs = [jnp.dot(e, v_ref[band, slice(pair * LANES, (pair + 1) * LANES)], preferred_element_type=F32) / l
                for e, l, (pair, half) in zip(exps, sums, heads)]
        for pair in range(NA_HEADS // 2):
            cols = slice(pair * LANES, (pair + 1) * LANES)
            o_ref[qrows, cols] = jnp.where(low_half, outs[2 * pair], outs[2 * pair + 1]).astype(o_ref.dtype)
        return carry

    lax.fori_loop(0, NA_ROWS_PER_STEP, one_row, 0)


def neighbourhood_attention(pb, batch, q_gain, k_gain, rel_bias):
    m = pb.shape[0]
    s = m // batch
    tq = NA_ROWS_PER_STEP * GRID_W
    steps = s // tq
    qg = jnp.tile(q_gain.astype(F32), NA_HEADS).reshape(1, NA_WIDTH)
    kg = jnp.tile(k_gain.astype(F32), NA_HEADS).reshape(1, NA_WIDTH)
    seg = jnp.asarray(np.kron(np.eye(NA_HEADS), np.ones((NA_HEAD_DIM, NA_HEAD_DIM))), BF16)
    bias = _na_bias_table(rel_bias)
    cq, ck, cv = (PB_COL[n] // NA_WIDTH for n in ("na_q", "na_k", "na_v"))
    return pl.pallas_call(
        _na_kernel,
        out_shape=jax.ShapeDtypeStruct((m, NA_WIDTH), BF16),
        grid=(batch, steps),
        in_specs=[
            pl.BlockSpec((tq, NA_WIDTH), lambda b, t: (b * steps + t, cq)),
            pl.BlockSpec((s, NA_WIDTH), lambda b, t: (b, ck)),
            pl.BlockSpec((s, NA_WIDTH), lambda b, t: (b, cv)),
            pl.BlockSpec((1, NA_WIDTH), lambda b, t: (0, 0)),
            pl.BlockSpec((1, NA_WIDTH), lambda b, t: (0, 0)),
            pl.BlockSpec((NA_WIDTH, NA_WIDTH), lambda b, t: (0, 0)),
            pl.BlockSpec((NA_WIN_ROWS, NA_HEADS, GRID_W, NA_BAND), lambda b, t: (0, 0, 0, 0)),
        ],
        out_specs=pl.BlockSpec((tq, NA_WIDTH), lambda b, t: (b * steps + t, 0)),
        scratch_shapes=[pltpu.VMEM((s, NA_WIDTH), BF16)],
        compiler_params=_params("parallel", "arbitrary"),
        name="neighbourhood_attention",
    )(pb, pb, pb, qg, kg, seg, bias)


def _mem_attn_kernel(q_ref, kv_ref, qg_ref, kg_ref, o_ref, kn_ref):
    @pl.when(pl.program_id(1) == 0)
    def _():
        for h in range(MEM_HEADS):
            cols = slice(h * MEM_HEAD_DIM, (h + 1) * MEM_HEAD_DIM)
            kn_ref[:, cols] = _rms_norm_rows(kv_ref[:, cols].astype(F32), kg_ref[...]).astype(BF16)

    head_cols = [slice(h * MEM_HEAD_DIM, (h + 1) * MEM_HEAD_DIM) for h in range(MEM_HEADS)]
    qns = [_rms_norm_rows(q_ref[:, cols].astype(F32), qg_ref[...]).astype(BF16) for cols in head_cols]
    scores = [lax.dot_general(qn, kn_ref[:, cols], (((1,), (1,)), ((), ())), preferred_element_type=F32)
              for qn, cols in zip(qns, head_cols)]
    exps, sums = [], []
    for s in scores:
        s = s * (MEM_HEAD_DIM ** -0.5)
        e = jnp.exp(s - jnp.max(s, axis=-1, keepdims=True))
        sums.append(jnp.sum(e, axis=-1, keepdims=True))
        exps.append(e.astype(BF16))
    outs = [jnp.dot(e, kv_ref[:, MEM_WIDTH + cols.start:MEM_WIDTH + cols.stop], preferred_element_type=F32)
            for e, cols in zip(exps, head_cols)]
    for o, l, cols in zip(outs, sums, head_cols):
        o_ref[:, cols] = (o / l).astype(o_ref.dtype)


def memory_cross_attention(pb, kv, batch, q_gain, k_gain, *, tq=512):
    m = pb.shape[0]
    steps = m // batch // tq
    n_mem = kv.shape[0] // batch
    cq = PB_COL["mem_q"] // MEM_WIDTH
    return pl.pallas_call(
        _mem_attn_kernel,
        out_shape=jax.ShapeDtypeStruct((m, MEM_WIDTH), BF16),
        grid=(batch, steps),
        in_specs=[
            pl.BlockSpec((tq, MEM_WIDTH), lambda b, t: (b * steps + t, cq)),
            pl.BlockSpec((n_mem, 2 * MEM_WIDTH), lambda b, t: (b, 0)),
            pl.BlockSpec((1, MEM_HEAD_DIM), lambda b, t: (0, 0)),
            pl.BlockSpec((1, MEM_HEAD_DIM), lambda b, t: (0, 0)),
        ],
        out_specs=pl.BlockSpec((tq, MEM_WIDTH), lambda b, t: (b * steps + t, 0)),
        scratch_shapes=[pltpu.VMEM((n_mem, MEM_WIDTH), BF16)],
        compiler_params=_params("parallel", "arbitrary"),
        name="memory_cross_attention",
    )(pb, kv, q_gain.astype(F32).reshape(1, MEM_HEAD_DIM), k_gain.astype(F32).reshape(1, MEM_HEAD_DIM))


LIN_BLOCK = 512
HEAD_V = 128


def _log1p_exp_neg(t):
    return jnp.log(1.0 + jnp.exp(-t))


def _log_sigmoid(x):
    return jnp.minimum(x, 0.0) - _log1p_exp_neg(jnp.abs(x))


def _logaddexp(a, b):
    return jnp.maximum(a, b) + _log1p_exp_neg(jnp.abs(a - b))


def _split_bf16(x, terms):
    parts = []
    for _ in range(terms):
        p = x.astype(BF16)
        parts.append(p)
        x = x - p.astype(F32)
    return parts


def _dot_f32(a, b):
    a_hi, a_lo = _split_bf16(a, 2)
    b_hi, b_lo = _split_bf16(b, 2)
    return (jnp.dot(a_hi, b_hi, preferred_element_type=F32)
            + (jnp.dot(a_hi, b_lo, preferred_element_type=F32) + jnp.dot(a_lo, b_hi, preferred_element_type=F32)))


def _cumsum_rows(mask, x):
    m = jnp.where(mask, 1.0, 0.0).astype(BF16)
    hi, mid, lo = _split_bf16(x, 3)
    return (jnp.dot(m, hi, preferred_element_type=F32)
            + (jnp.dot(m, mid, preferred_element_type=F32) + jnp.dot(m, lo, preferred_element_type=F32)))


def _gla_inputs(refs, rows, direction, params):
    q_ref, k_ref, v_ref, g_ref = refs
    wpad_ref, bias_ref = params
    qc = q_ref[rows, :].astype(F32) * (GLA_HEAD_K ** -0.5)
    kc = k_ref[rows, :].astype(F32)
    gk = _dot_f32(g_ref[rows, :], wpad_ref[direction]) + bias_ref[direction]
    lg = _log_sigmoid(gk) * (1.0 / GLA_GATE_NORMALIZER)
    return qc, kc, v_ref[rows, :], lg


def _hgrn_inputs(refs, rows, direction, params):
    q_ref, v_ref, z_ref = refs
    lb_ref, log_lb_ref, log1m_lb_ref = params
    qr = q_ref[rows, :].astype(F32)
    qc = qr * _sigmoid(qr)
    z = z_ref[rows, :]
    lg = _logaddexp(log_lb_ref[direction], log1m_lb_ref[direction] + _log_sigmoid(z))
    kc = (1.0 - lb_ref[direction]) * _sigmoid(-z)
    return qc, kc, v_ref[rows, :], lg


GDN_CONV_WIDTH = 5
GDN_QKV_WIDTH = 2 * GDN_KEY_WIDTH + GDN_VAL_WIDTH
GDN_HALO = 16


def _gdn_prep_kernel(prev_ref, cur_ref, next_ref, w_ref, o_ref, xp_ref, *, blocks_per_seq):
    i = pl.program_id(0)
    t = cur_ref.shape[0]
    pos = i % blocks_per_seq
    prev = prev_ref[...].astype(F32)
    nxt = next_ref[...].astype(F32)
    xp_ref[0:GDN_HALO, :] = jnp.where(pos == 0, jnp.zeros_like(prev), prev)
    xp_ref[GDN_HALO:GDN_HALO + t, :] = cur_ref[...].astype(F32)
    xp_ref[GDN_HALO + t:, :] = jnp.where(pos == blocks_per_seq - 1, jnp.zeros_like(nxt), nxt)
    half = GDN_CONV_WIDTH // 2
    for g in range(GDN_QKV_WIDTH // LANES):
        cols = slice(g * LANES, (g + 1) * LANES)
        acc = None
        for j in range(GDN_CONV_WIDTH):
            term = xp_ref[GDN_HALO - half + j:GDN_HALO - half + j + t, cols] * w_ref[j:j + 1, cols]
            acc = term if acc is None else acc + term
        y = acc * _sigmoid(acc)
        if g < 2 * GDN_HEADS:
            y = y * lax.rsqrt(jnp.sum(y * y, axis=-1, keepdims=True) + 1e-6)
            if g < GDN_HEADS:
                y = y * (GDN_HEAD_K ** -0.5)
        o_ref[:, cols] = y.astype(o_ref.dtype)


def gdn_prep(pb, batch, conv_w, *, t=512):
    m = pb.shape[0]
    blocks_per_seq = m // batch // t
    halo_per_block = t // GDN_HALO
    col = PB_COL["gdn_qkv"] // GDN_QKV_WIDTH
    last_halo = m // GDN_HALO - 1
    return pl.pallas_call(
        functools.partial(_gdn_prep_kernel, blocks_per_seq=blocks_per_seq),
        out_shape=jax.ShapeDtypeStruct((m, GDN_QKV_WIDTH), BF16),
        grid=(m // t,),
        in_specs=[
            pl.BlockSpec((GDN_HALO, GDN_QKV_WIDTH), lambda i: (jnp.maximum(i * halo_per_block - 1, 0), col)),
            pl.BlockSpec((t, GDN_QKV_WIDTH), lambda i: (i, col)),
            pl.BlockSpec((GDN_HALO, GDN_QKV_WIDTH),
                         lambda i: (jnp.minimum((i + 1) * halo_per_block, last_halo), col)),
            pl.BlockSpec((GDN_CONV_WIDTH, GDN_QKV_WIDTH), lambda i: (0, 0)),
        ],
        out_specs=pl.BlockSpec((t, GDN_QKV_WIDTH), lambda i: (i, 0)),
        scratch_shapes=[pltpu.VMEM((t + 2 * GDN_HALO, GDN_QKV_WIDTH), F32)],
        compiler_params=_params("parallel"),
        name="gdn_prep",
    )(pb, pb, pb, conv_w.astype(F32))


def _softplus(x):
    return jnp.maximum(x, 0.0) + _log1p_exp_neg(jnp.abs(x))


GDN_PACK = GDN_HEADS * GDN_CHUNK
GDN_WY_BLOCK = 512


def _stack_heads(x, width):
    heads = x.shape[1] // width
    lane = lax.broadcasted_iota(jnp.int32, (1, x.shape[1]), 1)
    return jnp.concatenate(
        [jnp.where((lane >= h * width) & (lane < (h + 1) * width), x, 0.0).astype(BF16) for h in range(heads)],
        axis=0)


def _packed_mm(x, y):
    return jnp.dot(x.astype(BF16), _stack_heads(y, GDN_CHUNK), preferred_element_type=F32)


def _packed_inverses(mats):
    c = GDN_CHUNK
    ii = lax.broadcasted_iota(jnp.int32, (c, GDN_PACK), 0)
    jj = lax.broadcasted_iota(jnp.int32, (c, GDN_PACK), 1) % c
    eye = (ii == jj).astype(F32)

    def same_block(s):
        return (ii // s) == (jj // s)

    ds = [jnp.where(same_block(8), a, 0.0) for a in mats]
    d2s = [_packed_mm(d, d) for d in ds]
    d4s = [_packed_mm(d2, d2) for d2 in d2s]
    ts = [_packed_mm(eye - d, eye + d2) for d, d2 in zip(ds, d2s)]
    ts = [_packed_mm(t, eye + d4) for t, d4 in zip(ts, d4s)]
    s = 8
    while s < c:
        off = same_block(2 * s) & jnp.logical_not(same_block(s))
        ets = [_packed_mm(jnp.where(off, a, 0.0), t) for a, t in zip(mats, ts)]
        ts = [t - _packed_mm(t, et) for t, et in zip(ts, ets)]
        s *= 2
    return ts


def _gdn_wy_kernel(qkv_ref, small_ref, a_ref, dtb_ref, selg_ref, selk_ref, selb_ref, *out_refs):
    c = GDN_CHUNK
    n_chunks = qkv_ref.shape[0] // c
    ii = lax.broadcasted_iota(jnp.int32, (c, c), 0)
    jj = lax.broadcasted_iota(jnp.int32, (c, c), 1)
    pi = lax.broadcasted_iota(jnp.int32, (c, GDN_PACK), 0)
    pj = lax.broadcasted_iota(jnp.int32, (c, GDN_PACK), 1) % c
    eye_p = (pi == pj).astype(F32)
    ones_cc = jnp.ones((c, c), BF16)

    problems = [(ch, d) for ch in range(n_chunks) for d in range(2)]
    chunk_in = []
    for ch in range(n_chunks):
        rows = slice(ch * c, (ch + 1) * c)
        qkv = qkv_ref[rows, :]
        small = small_ref[rows, :]
        kf = qkv[:, GDN_KEY_WIDTH:2 * GDN_KEY_WIDTH].astype(F32)
        chunk_in.append(dict(
            qf=qkv[:, :GDN_KEY_WIDTH].astype(F32), kf=kf, vf=qkv[:, 2 * GDN_KEY_WIDTH:].astype(F32),
            kbd=_stack_heads(kf, HEAD_V),
            log_alpha=a_ref[...] * _softplus(small + dtb_ref[...]),
            beta_all=_sigmoid(small)))

    def sel3(x, sel):
        hi, mid, lo = _split_bf16(x, 3)
        return (jnp.dot(hi, sel, preferred_element_type=F32)
                + (jnp.dot(mid, sel, preferred_element_type=F32) + jnp.dot(lo, sel, preferred_element_type=F32)))

    g_all = [_cumsum_rows((jj >= ii) if d else (jj <= ii), chunk_in[ch]["log_alpha"]) for ch, d in problems]
    g_pack = [sel3(g, selg_ref[d]) for g, (ch, d) in zip(g_all, problems)]
    g_wide = [sel3(g, selk_ref[d]) for g, (ch, d) in zip(g_all, problems)]
    beta_w = [sel3(chunk_in[ch]["beta_all"], selb_ref[d]) for ch, d in problems]
    g_rowp = []
    for gp in g_pack:
        hi, mid, lo = _split_bf16(gp * eye_p, 3)
        g_rowp.append(jnp.dot(ones_cc, hi, preferred_element_type=F32)
                      + (jnp.dot(ones_cc, mid, preferred_element_type=F32)
                         + jnp.dot(ones_cc, lo, preferred_element_type=F32)))
    decays, k_betas = [], []
    for gp, gr, bw, (ch, d) in zip(g_pack, g_rowp, beta_w, problems):
        incl = (pj >= pi) if d else (pj <= pi)
        decays.append(jnp.where(incl, jnp.exp(jnp.where(incl, gp - gr, 0.0)), 0.0))
        k_betas.append(chunk_in[ch]["kf"] * bw)
    kq = [lax.dot_general(jnp.concatenate([kb, chunk_in[ch]["qf"]], axis=0).astype(BF16), chunk_in[ch]["kbd"],
                          (((1,), (1,)), ((), ())), preferred_element_type=F32)
          for kb, (ch, d) in zip(k_betas, problems)]
    a_mats = []
    for x, dec, (ch, d) in zip(kq, decays, problems):
        strict = (pj > pi) if d else (pj < pi)
        a_mats.append(jnp.where(strict, x[:c] * dec, 0.0))
    t_invs = _packed_inverses(a_mats)

    for idx, (ch, d) in enumerate(problems):
        u_ref, w_ref, attn_ref, qd_ref, kd_ref, gt_ref = out_refs[6 * d:6 * d + 6]
        rows = slice(ch * c, (ch + 1) * c)
        cin = chunk_in[ch]
        gw = g_wide[idx]
        eg = jnp.exp(gw)
        t_b = t_invs[idx].astype(BF16)
        u_ref[rows, :] = jnp.dot(t_b, _stack_heads(cin["vf"] * beta_w[idx], HEAD_V), preferred_element_type=F32)
        w_ref[rows, :] = jnp.dot(t_b, _stack_heads(k_betas[idx] * eg, HEAD_V),
                                 preferred_element_type=F32).astype(w_ref.dtype)
        attn_ref[rows, :] = (kq[idx][c:] * decays[idx]).astype(attn_ref.dtype)
        end = 0 if d else c - 1
        g_end = gw[end:end + 1, :]
        qd_ref[rows, :] = (cin["qf"] * eg).astype(qd_ref.dtype)
        kd_ref[rows, :] = (cin["kf"] * jnp.exp(g_end - gw)).astype(kd_ref.dtype)
        gt_ref[ch:ch + 1, :] = jnp.exp(g_end)


def gdn_wy(qkv, pf, a_scale, dtb):
    m = qkv.shape[0]
    t = GDN_WY_BLOCK
    cpb = t // GDN_CHUNK
    selg = np.zeros((2, LANES, GDN_PACK), np.float32)
    selk = np.zeros((2, LANES, GDN_VAL_WIDTH), np.float32)
    selb = np.zeros((2, LANES, GDN_VAL_WIDTH), np.float32)
    for d in range(2):
        for h in range(GDN_HEADS):
            selg[d, GDN_A_LANE + d * GDN_HEADS + h, h * GDN_CHUNK:(h + 1) * GDN_CHUNK] = 1.0
            selk[d, GDN_A_LANE + d * GDN_HEADS + h, h * HEAD_V:(h + 1) * HEAD_V] = 1.0
            selb[d, GDN_B_LANE + d * GDN_HEADS + h, h * HEAD_V:(h + 1) * HEAD_V] = 1.0
    wide = GDN_VAL_WIDTH
    out_shape, out_specs = [], []
    for _ in range(2):
        for width, dt in ((wide, F32), (wide, BF16), (GDN_PACK, BF16), (wide, BF16), (wide, BF16)):
            out_shape.append(jax.ShapeDtypeStruct((m, width), dt))
            out_specs.append(pl.BlockSpec((t, width), lambda i: (i, 0)))
        out_shape.append(jax.ShapeDtypeStruct((m // GDN_CHUNK, wide), F32))
        out_specs.append(pl.BlockSpec((cpb, wide), lambda i: (i, 0)))
    return pl.pallas_call(
        _gdn_wy_kernel,
        out_shape=tuple(out_shape),
        grid=(m // t,),
        in_specs=[
            pl.BlockSpec((t, GDN_QKV_WIDTH), lambda i: (i, 0)),
            pl.BlockSpec((t, LANES), lambda i: (i, PF_SMALL_COL // LANES)),
            pl.BlockSpec((1, LANES), lambda i: (0, 0)),
            pl.BlockSpec((1, LANES), lambda i: (0, 0)),
            pl.BlockSpec((2, LANES, GDN_PACK), lambda i: (0, 0, 0)),
            pl.BlockSpec((2, LANES, wide), lambda i: (0, 0, 0)),
            pl.BlockSpec((2, LANES, wide), lambda i: (0, 0, 0)),
        ],
        out_specs=tuple(out_specs),
        compiler_params=_params("parallel"),
        name="gdn_wy",
    )(qkv, pf, a_scale, dtb, jnp.asarray(selg, BF16), jnp.asarray(selk, BF16), jnp.asarray(selb, BF16))


GDN_PAIR = 2 * HEAD_V


def _gdn_scan_kernel(*refs):
    groups = (refs[0:6], refs[6:12])
    out_refs = refs[12:14]
    state_refs = refs[14:16]

    @pl.when(pl.program_id(1) == 0)
    def _():
        for s_ref in state_refs:
            s_ref[...] = jnp.zeros_like(s_ref)

    n_chunks = out_refs[0].shape[0] // GDN_CHUNK
    pairs = GDN_HEADS // 2
    pair_cols = [slice(p * GDN_PAIR, (p + 1) * GDN_PAIR) for p in range(pairs)]
    ri = lax.broadcasted_iota(jnp.int32, (GDN_PAIR, GDN_PAIR), 0) // HEAD_V
    ci = lax.broadcasted_iota(jnp.int32, (GDN_PAIR, GDN_PAIR), 1) // HEAD_V
    diag = ri == ci

    def body(c, carry):
        chunks = (c, n_chunks - 1 - c)
        rows = [pl.ds(pl.multiple_of(ch * GDN_CHUNK, GDN_CHUNK), GDN_CHUNK) for ch in chunks]
        states = [[s_ref[p] for p in range(pairs)] for s_ref in state_refs]
        states_b = [[s.astype(BF16) for s in st] for st in states]
        ws = [[jnp.dot(groups[g][1][rows[g], cols], states_b[g][p], preferred_element_type=F32)
               for p, cols in enumerate(pair_cols)] for g in range(2)]
        qs = [[jnp.dot(groups[g][3][rows[g], cols], states_b[g][p], preferred_element_type=F32)
               for p, cols in enumerate(pair_cols)] for g in range(2)]
        v_new = [groups[g][0][rows[g], :] - jnp.concatenate(ws[g], axis=1) for g in range(2)]
        av = [jnp.dot(groups[g][2][rows[g], :], _stack_heads(v_new[g], HEAD_V), preferred_element_type=F32)
              for g in range(2)]
        v_new_b = [v.astype(BF16) for v in v_new]
        upd = [[lax.dot_general(groups[g][4][rows[g], cols], v_new_b[g][:, cols], (((0,), (0,)), ((), ())),
                                preferred_element_type=F32) for cols in pair_cols] for g in range(2)]
        for g in range(2):
            out_refs[g][rows[g], :] = jnp.concatenate(qs[g], axis=1) + av[g]
            gt = groups[g][5][pl.ds(chunks[g], 1), :]
            for p, cols in enumerate(pair_cols):
                state_refs[g][p] = states[g][p] * gt[:, cols] + jnp.where(diag, upd[g][p], 0.0)
        return carry

    lax.fori_loop(0, n_chunks, body, 0)


def gated_deltanet_branch(pb, pf, batch, conv_w, a_log, dt_bias, norm_gain):
    m = pb.shape[0]
    nb = m // batch // LIN_BLOCK
    cpb = LIN_BLOCK // GDN_CHUNK
    qkv = gdn_prep(pb, batch, conv_w)
    n_gate = 2 * GDN_HEADS
    a_scale = jnp.zeros((1, LANES), F32).at[0, GDN_A_LANE:GDN_A_LANE + n_gate].set(
        -jnp.exp(a_log.astype(F32)).reshape(n_gate))
    dtb = jnp.zeros((1, LANES), F32).at[0, GDN_A_LANE:GDN_A_LANE + n_gate].set(dt_bias.astype(F32).reshape(n_gate))
    wy = gdn_wy(qkv, pf, a_scale, dtb)
    widths = (GDN_VAL_WIDTH, GDN_VAL_WIDTH, GDN_PACK, GDN_VAL_WIDTH, GDN_VAL_WIDTH)

    def fwd(rows, width):
        return pl.BlockSpec((rows, width), lambda b, t: (b * nb + t, 0))

    def bwd(rows, width):
        return pl.BlockSpec((rows, width), lambda b, t: (b * nb + nb - 1 - t, 0))

    in_specs = [fwd(LIN_BLOCK, w) for w in widths] + [fwd(cpb, GDN_VAL_WIDTH)]
    in_specs += [bwd(LIN_BLOCK, w) for w in widths] + [bwd(cpb, GDN_VAL_WIDTH)]
    state = pltpu.VMEM((GDN_HEADS // 2, GDN_PAIR, GDN_PAIR), F32)
    o_f, o_b = pl.pallas_call(
        _gdn_scan_kernel,
        out_shape=(jax.ShapeDtypeStruct((m, GDN_VAL_WIDTH), F32), jax.ShapeDtypeStruct((m, GDN_VAL_WIDTH), F32)),
        grid=(batch, nb),
        in_specs=in_specs,
        out_specs=(fwd(LIN_BLOCK, GDN_VAL_WIDTH), bwd(LIN_BLOCK, GDN_VAL_WIDTH)),
        scratch_shapes=[state, state],
        compiler_params=_params("parallel", "arbitrary"),
        name="gdn_scan",
    )(*wy)
    return RawBranch(o_f, o_b, "gdn_og", norm_gain, True)


def _dot3(m, x):
    hi, mid, lo = _split_bf16(x, 3)
    return (jnp.dot(m, hi, preferred_element_type=F32)
            + (jnp.dot(m, mid, preferred_element_type=F32) + jnp.dot(m, lo, preferred_element_type=F32)))


LIN_CUM_ROWS = 256
LIN_SCORE_ROWS = 128


def _chunk_causal(n, chunk, reverse):
    i = lax.broadcasted_iota(jnp.int32, (n, n), 0)
    j = lax.broadcasted_iota(jnp.int32, (n, n), 1)
    return ((i // chunk) == (j // chunk)) & ((j >= i) if reverse else (j <= i))


def _lin_intra_kernel(*refs, load_inputs, n_in, n_params, heads):
    dir_refs = (refs[:n_in], refs[n_in:2 * n_in])
    params = refs[2 * n_in:2 * n_in + n_params]
    out_refs = refs[2 * n_in + n_params:]
    c = LIN_CHUNK
    dirs = (0, 1)
    loaded = [load_inputs(dir_refs[d], slice(None), d, params) for d in dirs]
    t, w = loaded[0][0].shape
    dk = w // heads
    nc = t // c
    cums = [jnp.where(_chunk_causal(LIN_CUM_ROWS, c, d == 1), 1.0, 0.0).astype(BF16) for d in dirs]
    bs = [jnp.concatenate([_dot3(cums[d], loaded[d][3][r:r + LIN_CUM_ROWS, :])
                           for r in range(0, t, LIN_CUM_ROWS)], axis=0) for d in dirs]
    qes, kes = [], []
    for d in dirs:
        oi_ref, qd_ref, kd_ref, gt_ref = out_refs[4 * d:4 * d + 4]
        qc, kc, vc, lg = loaded[d]
        b = bs[d]
        b3 = b.reshape(nc, c, w)
        mid = c - 1 - c // 2 if d else c // 2
        end = 0 if d else c - 1
        b_mid = jnp.broadcast_to(b3[:, mid:mid + 1, :], (nc, c, w)).reshape(t, w)
        b_end = jnp.broadcast_to(b3[:, end:end + 1, :], (nc, c, w)).reshape(t, w)
        qes.append((qc * jnp.exp(b - b_mid)).astype(BF16))
        kes.append((kc * jnp.exp(b_mid - b)).astype(BF16))
        qd_ref[...] = (qc * jnp.exp(b)).astype(qd_ref.dtype)
        kd_ref[...] = (kc * jnp.exp(b_end - b)).astype(kd_ref.dtype)
        gt_ref[...] = jnp.exp(b3[:, end, :])
    keeps = [_chunk_causal(LIN_SCORE_ROWS, c, d == 1) for d in dirs]
    lane = lax.broadcasted_iota(jnp.int32, (1, LANES), 1)
    for h in range(heads):
        win = slice((h * dk) // LANES * LANES, (h * dk) // LANES * LANES + LANES)
        lo = h * dk - win.start
        vcols = slice(h * HEAD_V, (h + 1) * HEAD_V)
        tiles = [(slice(r, r + LIN_SCORE_ROWS), d) for r in range(0, t, LIN_SCORE_ROWS) for d in dirs]
        scores = []
        for rows, d in tiles:
            qh = qes[d][rows, win]
            if dk < LANES:
                qh = jnp.where((lane >= lo) & (lane < lo + dk), qh, jnp.zeros_like(qh))
            scores.append(lax.dot_general(qh, kes[d][rows, win], (((1,), (1,)), ((), ())),
                                          preferred_element_type=F32))
        probs = [jnp.where(keeps[d], s, 0.0).astype(BF16) for s, (rows, d) in zip(scores, tiles)]
        for p, (rows, d) in zip(probs, tiles):
            out_refs[4 * d][rows, vcols] = jnp.dot(p, loaded[d][2][rows, vcols], preferred_element_type=F32)


def _lin_scan_kernel(*refs, heads, chunk, unroll):
    groups = (refs[0:5], refs[5:10])
    out_refs = refs[10:12]
    state_refs = refs[12:14]

    @pl.when(pl.program_id(1) == 0)
    def _():
        for s_ref in state_refs:
            s_ref[...] = jnp.zeros_like(s_ref)

    n_chunks = out_refs[0].shape[0] // chunk
    w = state_refs[0].shape[1]
    dk = w // heads
    lane = lax.broadcasted_iota(jnp.int32, (1, w), 1)
    masks = [(lane >= h * dk) & (lane < (h + 1) * dk) for h in range(heads)]

    def stack(x):
        return jnp.concatenate([jnp.where(m, x, jnp.zeros_like(x)) for m in masks], axis=0)

    def body(it, carry):
        steps = []
        for u in range(unroll):
            c = it * unroll + u
            steps += [(0, c), (1, n_chunks - 1 - c)]
        prepared = []
        for g, ch in steps:
            rows = pl.ds(pl.multiple_of(ch * chunk, chunk), chunk)
            oi_ref, qd_ref, kd_ref, v_ref, gt_ref = groups[g]
            vc = v_ref[rows, :]
            v4 = jnp.concatenate([vc[:, h * HEAD_V:(h + 1) * HEAD_V] for h in range(heads)], axis=0)
            upd = lax.dot_general(v4, stack(kd_ref[rows, :]), (((0,), (0,)), ((), ())),
                                  preferred_element_type=F32)
            prepared.append((rows, stack(qd_ref[rows, :]), upd, gt_ref[pl.ds(ch, 1), :]))
        states = [s_ref[...] for s_ref in state_refs]
        for (g, ch), (rows, q4, upd, gt) in zip(steps, prepared):
            o_inter = lax.dot_general(q4, states[g].astype(BF16), (((1,), (1,)), ((), ())),
                                      preferred_element_type=F32)
            out_refs[g][rows, :] = groups[g][0][rows, :] + jnp.concatenate(
                [o_inter[h * chunk:(h + 1) * chunk, :] for h in range(heads)], axis=1)
            states[g] = states[g] * gt + upd
        for s_ref, st in zip(state_refs, states):
            s_ref[...] = st
        return carry

    lax.fori_loop(0, n_chunks // unroll, body, 0)


def _bidir_lin_call(name, load_inputs, arrays, col_blocks, widths, params, batch, heads, key_width, v_col):
    m = arrays[0].shape[0]
    t = LIN_BLOCK
    nb = m // batch // t
    cpb = t // LIN_CHUNK
    out_w = heads * HEAD_V
    n_in = len(arrays)

    in_specs, operands = [], []
    for d in range(2):
        for a, wd, cb in zip(arrays, widths, col_blocks):
            in_specs.append(pl.BlockSpec((t, wd), functools.partial(lambda i, c: (i, c), c=cb[d])))
            operands.append(a)
    for p in params:
        in_specs.append(pl.BlockSpec(p.shape, functools.partial(lambda i, nd: (0,) * nd, nd=p.ndim)))
    out_shape, out_specs = [], []
    for _ in range(2):
        for rows_total, rows_blk, width, dt in ((m, t, out_w, F32), (m, t, key_width, BF16),
                                                (m, t, key_width, BF16), (m // LIN_CHUNK, cpb, key_width, F32)):
            out_shape.append(jax.ShapeDtypeStruct((rows_total, width), dt))
            out_specs.append(pl.BlockSpec((rows_blk, width), lambda i: (i, 0)))

    intra = pl.pallas_call(
        functools.partial(_lin_intra_kernel, load_inputs=load_inputs, n_in=n_in, n_params=len(params),
                          heads=heads),
        out_shape=tuple(out_shape),
        grid=(m // t,),
        in_specs=in_specs,
        out_specs=tuple(out_specs),
        compiler_params=_params("parallel"),
        name=name + "_intra",
    )(*operands, *params)

    def fwd(rows, width, col=0):
        return pl.BlockSpec((rows, width), lambda b, s: (b * nb + s, col))

    def bwd(rows, width, col=0):
        return pl.BlockSpec((rows, width), lambda b, s: (b * nb + nb - 1 - s, col))

    scan_specs, scan_ops = [], []
    for d, mk in enumerate((fwd, bwd)):
        oi, qd, kd, gt = intra[4 * d:4 * d + 4]
        scan_specs += [mk(t, out_w), mk(t, key_width), mk(t, key_width), mk(t, out_w, v_col), mk(cpb, key_width)]
        scan_ops += [oi, qd, kd, arrays[0], gt]
    state = pltpu.VMEM((HEAD_V, key_width), F32)
    return pl.pallas_call(
        functools.partial(_lin_scan_kernel, heads=heads, chunk=LIN_CHUNK, unroll=4),
        out_shape=(jax.ShapeDtypeStruct((m, out_w), F32), jax.ShapeDtypeStruct((m, out_w), F32)),
        grid=(batch, nb),
        in_specs=scan_specs,
        out_specs=(fwd(t, out_w), bwd(t, out_w)),
        scratch_shapes=[state, state],
        compiler_params=_params("parallel", "arbitrary"),
        name=name + "_scan",
    )(*scan_ops)


def gla_branch(pb, pf, batch, w_gate_up, b_gate, norm_gain):
    wpad = jnp.zeros((2, LANES, GLA_KEY_WIDTH), F32)
    for d in range(2):
        wpad = wpad.at[d, d * GLA_GATE_RANK:(d + 1) * GLA_GATE_RANK, :].set(w_gate_up[d].astype(F32))
    bias = b_gate.astype(F32).reshape(2, 1, GLA_KEY_WIDTH)
    v_col = PB_COL["gla_v"] // GLA_VAL_WIDTH
    cols = [(PB_COL["gla_q"] // GLA_KEY_WIDTH,) * 2, (PB_COL["gla_k"] // GLA_KEY_WIDTH,) * 2,
            (v_col,) * 2, (PF_SMALL_COL // LANES,) * 2]
    o_f, o_b = _bidir_lin_call("gla", _gla_inputs, [pb, pb, pb, pf], cols,
                               [GLA_KEY_WIDTH, GLA_KEY_WIDTH, GLA_VAL_WIDTH, LANES], [wpad, bias],
                               batch, GLA_HEADS, GLA_KEY_WIDTH, v_col)
    return RawBranch(o_f, o_b, "gla_og", norm_gain, True)


def hgrn2_branch(pb, pf, batch, lower_bound, norm_gain):
    lb = lower_bound.astype(F32).reshape(2, 1, HGRN_KEY_WIDTH)
    log_lb = jnp.log(jnp.maximum(lb, LB_FLOOR))
    log1m_lb = jnp.log1p(-lb)
    zc = PF_COL["hg_f"] // HGRN_KEY_WIDTH
    v_col = PB_COL["hg_i"] // HGRN_VAL_WIDTH
    cols = [(PB_COL["hg_q"] // HGRN_KEY_WIDTH,) * 2, (v_col,) * 2, (zc, zc + 1)]
    o_f, o_b = _bidir_lin_call("hgrn2", _hgrn_inputs, [pb, pb, pf], cols,
                               [HGRN_KEY_WIDTH, HGRN_VAL_WIDTH, HGRN_KEY_WIDTH], [lb, log_lb, log1m_lb],
                               batch, HGRN_HEADS, HGRN_KEY_WIDTH, v_col)
    return RawBranch(o_f, o_b, "hg_og", norm_gain, False)


def kernel(x, mem, g_mix, w_in, na_q_gain, na_k_gain, na_rel_bias, gla_w_gate_up, gla_b_gate, gla_norm_gain, gdn_conv_w, gdn_a_log, gdn_dt_bias, gdn_norm_gain, hgrn_lb_raw, hgrn_norm_gain, g_mem, w_mem_kv, mem_q_gain, mem_k_gain, w_branch, w_out, g_ffn, ffn_w_gate, ffn_w_up, ffn_w_down, moe_w_router, moe_b_router, moe_w_gate, moe_w_up, moe_w_down):
    B, S, D = x.shape
    n_tok = B * S
    lb_w = jax.nn.softmax(hgrn_lb_raw.astype(F32), axis=0)
    hgrn_lb = jnp.cumsum(lb_w, axis=0) - lb_w[0:1]
    x2 = x.reshape(n_tok, D)
    mem2 = mem.reshape(B * mem.shape[1], D)
    for layer in range(DEPTH):
        pb, pf = in_projection(x2, g_mix[layer], _rearrange_w_in(w_in[layer]), PB_COL["gates"], PB_WIDTH)
        kv = rms_matmul(mem2, g_mem[layer], w_mem_kv[layer].astype(BF16), tm=mem2.shape[0], tn=512,
                        out_dtype=BF16)
        branches = [
            neighbourhood_attention(pb, B, na_q_gain[layer], na_k_gain[layer], na_rel_bias[layer]),
            gla_branch(pb, pf, B, gla_w_gate_up[layer], gla_b_gate[layer], gla_norm_gain[layer]),
            gated_deltanet_branch(pb, pf, B, gdn_conv_w[layer], gdn_a_log[layer], gdn_dt_bias[layer],
                                  gdn_norm_gain[layer]),
            hgrn2_branch(pb, pf, B, hgrn_lb[layer], hgrn_norm_gain[layer]),
            memory_cross_attention(pb, kv, B, mem_q_gain[layer], mem_k_gain[layer]),
        ]
        merged = merge_branches(branches, pb, w_branch[layer].astype(BF16), tm=512, tn=512)
        x2 = matmul_residual(merged, w_out[layer].astype(BF16), x2, tm=1024, tn=512)

        j = layer // 2
        if layer % 2 == 0:
            act = rms_swiglu_up(x2, g_ffn[layer], ffn_w_gate[j].astype(BF16), ffn_w_up[j].astype(BF16),
                                tm=1024, tn=512)
            x2 = matmul_residual(act, ffn_w_down[j].astype(BF16), x2, tm=512, tn=512)
        else:
            x2 = moe_layer(x2, g_ffn[layer], moe_w_router[j], moe_b_router[j], moe_w_gate[j], moe_w_up[j],
                           moe_w_down[j])
    return x2.reshape(B, S, D)
```

```python
import functools

import jax
import jax.numpy as jnp
import numpy as np
from jax import lax
from jax.experimental import pallas as pl
from jax.experimental.pallas import tpu as pltpu

F32 = jnp.float32
BF16 = jnp.bfloat16

D_MODEL = 2048
DEPTH = 2
RMS_EPS = 1e-6
MASK_VALUE = -1e30
LB_FLOOR = 1e-30
GRID_W = 64

NA_HEADS = 8
NA_HEAD_DIM = 64
NA_WIDTH = 512
NA_WIN_ROWS = 8
NA_WIN_COLS = 16

GLA_HEADS = 4
GLA_HEAD_K = 64
GLA_HEAD_V = 128
GLA_KEY_WIDTH = 256
GLA_VAL_WIDTH = 512
GLA_GATE_RANK = 16
GLA_GATE_NORMALIZER = 16.0

GDN_HEADS = 4
GDN_HEAD_K = 128
GDN_HEAD_V = 128
GDN_KEY_WIDTH = 512
GDN_VAL_WIDTH = 512
GDN_CHUNK = 64

HGRN_HEADS = 4
HGRN_HEAD_K = 128
HGRN_HEAD_V = 128
HGRN_KEY_WIDTH = 512
HGRN_VAL_WIDTH = 512

LIN_CHUNK = 32

MEM_HEADS = 4
MEM_HEAD_DIM = 128
MEM_WIDTH = 512

N_BRANCH = 5
BRANCH_WIDTH = 512
N_EXPERTS = 8
MOE_TOP_K = 2

IN_WIDTHS = (
    NA_WIDTH, NA_WIDTH, NA_WIDTH,
    GLA_KEY_WIDTH, GLA_KEY_WIDTH, GLA_VAL_WIDTH,
    2 * GLA_GATE_RANK, GLA_VAL_WIDTH,
    2 * GDN_KEY_WIDTH + GDN_VAL_WIDTH,
    2 * GDN_HEADS, 2 * GDN_HEADS, GDN_VAL_WIDTH,
    HGRN_KEY_WIDTH, 2 * HGRN_KEY_WIDTH, HGRN_VAL_WIDTH, HGRN_VAL_WIDTH,
    MEM_WIDTH,
    N_BRANCH * D_MODEL,
)
P_IN = sum(IN_WIDTHS)

V7X_VMEM_BYTES = 64 * 1024 * 1024
VMEM_LIMIT_BYTES = V7X_VMEM_BYTES - 8 * 1024 * 1024
LANES = 128


def _params(*semantics):
    return pltpu.CompilerParams(dimension_semantics=semantics, vmem_limit_bytes=VMEM_LIMIT_BYTES)


def _sigmoid(x):
    return 0.5 * jnp.tanh(0.5 * x) + 0.5


def _rms_norm_rows(x, gain):
    ms = jnp.mean(x * x, axis=-1, keepdims=True)
    return x * lax.rsqrt(ms + RMS_EPS) * gain


def _rms_matmul_kernel(x_ref, g_ref, w_ref, o_ref, h_ref):
    @pl.when(pl.program_id(1) == 0)
    def _():
        h_ref[...] = _rms_norm_rows(x_ref[...], g_ref[...]).astype(BF16)

    o_ref[...] = jnp.dot(h_ref[...], w_ref[...], preferred_element_type=F32).astype(o_ref.dtype)


def rms_matmul(x, gain, w, *, tm, tn, out_dtype=F32):
    m, k = x.shape
    n = w.shape[1]
    return pl.pallas_call(
        _rms_matmul_kernel,
        out_shape=jax.ShapeDtypeStruct((m, n), out_dtype),
        grid=(m // tm, n // tn),
        in_specs=[
            pl.BlockSpec((tm, k), lambda i, j: (i, 0)),
            pl.BlockSpec((1, k), lambda i, j: (0, 0)),
            pl.BlockSpec((k, tn), lambda i, j: (0, j)),
        ],
        out_specs=pl.BlockSpec((tm, tn), lambda i, j: (i, j)),
        scratch_shapes=[pltpu.VMEM((tm, k), BF16)],
        compiler_params=_params("parallel", "arbitrary"),
        name="rms_matmul",
    )(x, gain.reshape(1, k), w)


IN_PROJ_TILE = 512


def _in_projection_kernel(x_ref, g_ref, w_ref, ob_ref, of_ref, h_ref, *, n_plain_tiles, n_bf16_tiles):
    j = pl.program_id(1)

    @pl.when(j == 0)
    def _():
        h_ref[...] = _rms_norm_rows(x_ref[...], g_ref[...]).astype(BF16)

    r = jnp.dot(h_ref[...], w_ref[...], preferred_element_type=F32)

    @pl.when(j < n_plain_tiles)
    def _():
        ob_ref[...] = r.astype(ob_ref.dtype)

    @pl.when(jnp.logical_and(j >= n_plain_tiles, j < n_bf16_tiles))
    def _():
        ob_ref[...] = _sigmoid(r).astype(ob_ref.dtype)

    @pl.when(j >= n_bf16_tiles)
    def _():
        of_ref[...] = r


def in_projection(x, gain, w, n_plain, n_bf16, *, tm=1024):
    m, k = x.shape
    tn = IN_PROJ_TILE
    nb = n_bf16 // tn
    nf = (w.shape[1] - n_bf16) // tn
    return pl.pallas_call(
        functools.partial(_in_projection_kernel, n_plain_tiles=n_plain // tn, n_bf16_tiles=nb),
        out_shape=(jax.ShapeDtypeStruct((m, nb * tn), BF16), jax.ShapeDtypeStruct((m, nf * tn), F32)),
        grid=(m // tm, nb + nf),
        in_specs=[
            pl.BlockSpec((tm, k), lambda i, j: (i, 0)),
            pl.BlockSpec((1, k), lambda i, j: (0, 0)),
            pl.BlockSpec((k, tn), lambda i, j: (0, j)),
        ],
        out_specs=(pl.BlockSpec((tm, tn), lambda i, j: (i, jnp.minimum(j, nb - 1))),
                   pl.BlockSpec((tm, tn), lambda i, j: (i, jnp.maximum(j - nb, 0)))),
        scratch_shapes=[pltpu.VMEM((tm, k), BF16)],
        compiler_params=_params("parallel", "arbitrary"),
        name="in_projection",
    )(x, gain.reshape(1, k), w)


def _rms_swiglu_kernel(x_ref, g_ref, wg_ref, wu_ref, o_ref, h_ref):
    @pl.when(pl.program_id(1) == 0)
    def _():
        h_ref[...] = _rms_norm_rows(x_ref[...], g_ref[...]).astype(BF16)

    h = h_ref[...]
    a = jnp.dot(h, wg_ref[...], preferred_element_type=F32)
    b = jnp.dot(h, wu_ref[...], preferred_element_type=F32)
    o_ref[...] = (a * _sigmoid(a) * b).astype(o_ref.dtype)


def rms_swiglu_up(x, gain, wg, wu, *, tm, tn):
    m, k = x.shape
    n = wg.shape[1]
    return pl.pallas_call(
        _rms_swiglu_kernel,
        out_shape=jax.ShapeDtypeStruct((m, n), BF16),
        grid=(m // tm, n // tn),
        in_specs=[
            pl.BlockSpec((tm, k), lambda i, j: (i, 0)),
            pl.BlockSpec((1, k), lambda i, j: (0, 0)),
            pl.BlockSpec((k, tn), lambda i, j: (0, j)),
            pl.BlockSpec((k, tn), lambda i, j: (0, j)),
        ],
        out_specs=pl.BlockSpec((tm, tn), lambda i, j: (i, j)),
        scratch_shapes=[pltpu.VMEM((tm, k), BF16)],
        compiler_params=_params("parallel", "arbitrary"),
        name="rms_swiglu_up",
    )(x, gain.reshape(1, k), wg, wu)


def _matmul_residual_kernel(a_ref, w_ref, r_ref, o_ref):
    o_ref[...] = r_ref[...] + jnp.dot(a_ref[...], w_ref[...], preferred_element_type=F32)


def matmul_residual(a, w, res, *, tm, tn):
    m, k = a.shape
    n = w.shape[1]
    return pl.pallas_call(
        _matmul_residual_kernel,
        out_shape=jax.ShapeDtypeStruct((m, n), F32),
        grid=(m // tm, n // tn),
        in_specs=[
            pl.BlockSpec((tm, k), lambda i, j: (i, 0)),
            pl.BlockSpec((k, tn), lambda i, j: (0, j)),
            pl.BlockSpec((tm, tn), lambda i, j: (i, j)),
        ],
        out_specs=pl.BlockSpec((tm, tn), lambda i, j: (i, j)),
        compiler_params=_params("parallel", "arbitrary"),
        name="matmul_residual",
    )(a, w, res)


class RawBranch:
    def __init__(self, o_fwd, o_bwd, og_name, gain, silu_gate):
        self.o_fwd, self.o_bwd, self.og_name, self.gain, self.silu_gate = o_fwd, o_bwd, og_name, gain, silu_gate


def _merge_kernel(*refs, raw):
    pos = 0
    br = []
    for kind in raw:
        width = 1 if kind is None else 4
        br.append(refs[pos:pos + width])
        pos += width
    gl_refs = refs[pos:pos + N_BRANCH]
    wb_ref, o_ref, fin_ref = refs[pos + N_BRANCH:pos + N_BRANCH + 3]
    raw_slot = {n: s for s, n in enumerate(n for n, kind in enumerate(raw) if kind is not None)}

    @pl.when(pl.program_id(1) == 0)
    def _():
        for n, slot in raw_slot.items():
            of_ref, ob_ref, og_ref, gain_ref = br[n]
            for h in range(BRANCH_WIDTH // LANES):
                cols = slice(h * LANES, (h + 1) * LANES)
                y = _rms_norm_rows(of_ref[:, cols] + ob_ref[:, cols], gain_ref[...])
                g = og_ref[:, cols].astype(F32)
                gate = _sigmoid(g)
                if raw[n]:
                    gate = g * gate
                fin_ref[slot, :, cols] = (y * gate).astype(fin_ref.dtype)

    acc = None
    for n in range(N_BRANCH):
        b = br[n][0][...] if raw[n] is None else fin_ref[raw_slot[n]]
        y = jnp.dot(b, wb_ref[n], preferred_element_type=F32)
        t = gl_refs[n][...].astype(F32) * y
        acc = t if acc is None else acc + t
    o_ref[...] = acc.astype(o_ref.dtype)


def merge_branches(branches, pb, w_branch, *, tm, tn):
    m = pb.shape[0]
    d = D_MODEL
    tiles_per_branch = d // tn
    tile0 = PB_COL["gates"] // tn
    row_block = pl.BlockSpec((tm, BRANCH_WIDTH), lambda i, j: (i, 0))
    in_specs, operands, raw = [], [], []
    for b in branches:
        if isinstance(b, RawBranch):
            og_col = PB_COL[b.og_name] // BRANCH_WIDTH
            in_specs += [row_block, row_block,
                         pl.BlockSpec((tm, BRANCH_WIDTH), functools.partial(lambda i, j, c: (i, c), c=og_col)),
                         pl.BlockSpec((1, LANES), lambda i, j: (0, 0))]
            operands += [b.o_fwd, b.o_bwd, pb, b.gain.astype(F32).reshape(1, LANES)]
            raw.append(b.silu_gate)
        else:
            in_specs.append(row_block)
            operands.append(b)
            raw.append(None)
    in_specs += [
        pl.BlockSpec((tm, tn), functools.partial(lambda i, j, n: (i, tile0 + n * tiles_per_branch + j), n=n))
        for n in range(N_BRANCH)
    ]
    in_specs += [pl.BlockSpec((N_BRANCH, BRANCH_WIDTH, tn), lambda i, j: (0, 0, j))]
    n_raw = sum(kind is not None for kind in raw)
    return pl.pallas_call(
        functools.partial(_merge_kernel, raw=tuple(raw)),
        out_shape=jax.ShapeDtypeStruct((m, d), BF16),
        grid=(m // tm, d // tn),
        in_specs=in_specs,
        out_specs=pl.BlockSpec((tm, tn), lambda i, j: (i, j)),
        scratch_shapes=[pltpu.VMEM((max(n_raw, 1), tm, BRANCH_WIDTH), BF16)],
        compiler_params=_params("parallel", "arbitrary"),
        name="merge_branches",
    )(*operands, *([pb] * N_BRANCH), w_branch)


def _router_kernel(x_ref, g_ref, w_ref, b_ref, o_ref, h_ref, cnt_ref, run_ref, *, n_experts):
    @pl.when(pl.program_id(0) == 0)
    def _():
        run_ref[...] = jnp.zeros_like(run_ref)

    h = _rms_norm_rows(x_ref[...], g_ref[...])
    h_ref[...] = h.astype(h_ref.dtype)
    logits = _dot_f32(h, w_ref[...]) + b_ref[...]
    lane = lax.broadcasted_iota(jnp.int32, logits.shape, 1).astype(F32)
    neg = -jnp.inf
    lm = jnp.where(lane < n_experts, logits, neg)
    m1 = jnp.max(lm, axis=-1, keepdims=True)
    i1 = jnp.min(jnp.where(lm == m1, lane, float(LANES)), axis=-1, keepdims=True)
    lm2 = jnp.where(lane == i1, neg, lm)
    m2 = jnp.max(lm2, axis=-1, keepdims=True)
    i2 = jnp.min(jnp.where(lm2 == m2, lane, float(LANES)), axis=-1, keepdims=True)
    t = jnp.exp(m2 - m1)
    den = 1.0 + t

    tm = logits.shape[0]
    before = (lax.broadcasted_iota(jnp.int32, (tm, tm), 1)
              < lax.broadcasted_iota(jnp.int32, (tm, tm), 0))
    before = jnp.where(before, 1.0, 0.0).astype(BF16)
    pick1 = lane == i1
    pick2 = lane == i2
    oh1 = jnp.where(pick1, 1.0, 0.0)
    oh2 = jnp.where(pick2, 1.0, 0.0)
    pre1 = jnp.dot(before, oh1.astype(BF16), preferred_element_type=F32)
    pre2 = jnp.dot(before, oh2.astype(BF16), preferred_element_type=F32)
    tot1 = jnp.sum(oh1, axis=0, keepdims=True)
    tot2 = jnp.sum(oh2, axis=0, keepdims=True)
    run = run_ref[...]
    rank1 = jnp.sum(jnp.where(pick1, pre1 + run, 0.0), axis=-1, keepdims=True)
    rank2 = jnp.sum(jnp.where(pick2, pre2 + (run + tot1), 0.0), axis=-1, keepdims=True)
    run = run + tot1 + tot2
    run_ref[...] = run
    cnt_ref[...] = jnp.broadcast_to(run, cnt_ref.shape)

    out = jnp.where(lane == 0, 1.0 / den, jnp.where(lane == 1, t / den, jnp.where(lane == 2, i1, i2)))
    out = jnp.where(lane == 4, rank1, jnp.where(lane == 5, rank2, out))
    o_ref[...] = jnp.where(lane < 6, out, 0.0)


def router_top2(x, gain, w_router, b_router, *, tm=512):
    m, k = x.shape
    e = w_router.shape[1]
    w_pad = jnp.zeros((k, LANES), F32).at[:, :e].set(w_router.astype(F32))
    b_pad = jnp.zeros((1, LANES), F32).at[0, :e].set(b_router.astype(F32))
    route, h, cnt = pl.pallas_call(
        functools.partial(_router_kernel, n_experts=e),
        out_shape=(jax.ShapeDtypeStruct((m, LANES), F32), jax.ShapeDtypeStruct((m, k), BF16),
                   jax.ShapeDtypeStruct((8, LANES), F32)),
        grid=(m // tm,),
        in_specs=[
            pl.BlockSpec((tm, k), lambda i: (i, 0)),
            pl.BlockSpec((1, k), lambda i: (0, 0)),
            pl.BlockSpec((k, LANES), lambda i: (0, 0)),
            pl.BlockSpec((1, LANES), lambda i: (0, 0)),
        ],
        out_specs=(pl.BlockSpec((tm, LANES), lambda i: (i, 0)), pl.BlockSpec((tm, k), lambda i: (i, 0)),
                   pl.BlockSpec((8, LANES), lambda i: (0, 0))),
        scratch_shapes=[pltpu.VMEM((1, LANES), F32)],
        compiler_params=_params("arbitrary"),
        name="router_top2",
    )(x, gain.reshape(1, k), w_pad, b_pad)
    return route, h, cnt[0, :e].astype(jnp.int32)


MOE_FILL = 1024
MOE_SUB = 256
MOE_TILE = MOE_FILL + MOE_SUB
MOE_FF_TILE = 512


def _moe_kernel(tile_e_ref, tile_rows_ref, n_used_ref, x_ref, wg_ref, wu_ref, wd_ref, o_ref, acc_ref):
    i = pl.program_id(0)
    j = pl.program_id(1)
    last = pl.num_programs(1) - 1
    valid = tile_rows_ref[i]
    n_sub = (valid + (MOE_SUB - 1)) // MOE_SUB

    for k in range(1, MOE_TILE // MOE_SUB + 1):
        rows = slice(0, k * MOE_SUB)

        @pl.when(n_sub == k)
        def _(rows=rows):
            x = x_ref[rows, :]
            a = jnp.dot(x, wg_ref[0].astype(BF16), preferred_element_type=F32)
            b = jnp.dot(x, wu_ref[0].astype(BF16), preferred_element_type=F32)
            act = (a * _sigmoid(a) * b).astype(BF16)
            part = jnp.dot(act, wd_ref[0].astype(BF16), preferred_element_type=F32)

            @pl.when(j == 0)
            def _():
                acc_ref[rows, :] = part

            @pl.when(j > 0)
            def _():
                acc_ref[rows, :] += part

    for s in range(0, MOE_TILE, MOE_SUB):
        rows = slice(s, s + MOE_SUB)
        filled = s < valid

        @pl.when(jnp.logical_and(filled, j == last))
        def _(rows=rows):
            o_ref[rows, :] = acc_ref[rows, :].astype(o_ref.dtype)

        @pl.when(jnp.logical_and(jnp.logical_not(filled), j == last))
        def _(rows=rows):
            o_ref[rows, :] = jnp.zeros((MOE_SUB, o_ref.shape[1]), o_ref.dtype)


def moe_experts(xb, tile_e, tile_rows, n_used, wg, wu, wd):
    rows, d = xb.shape
    ff = wg.shape[2]
    tm, tf = MOE_TILE, MOE_FF_TILE
    n_tiles = rows // tm
    last_j = ff // tf - 1

    def x_map(i, j, te, tr, nu):
        return (jnp.minimum(i, nu[0] - 1), 0)

    def up_map(i, j, te, tr, nu):
        return (te[i], 0, jnp.where(i < nu[0], j, last_j))

    def down_map(i, j, te, tr, nu):
        return (te[i], jnp.where(i < nu[0], j, last_j), 0)

    grid_spec = pltpu.PrefetchScalarGridSpec(
        num_scalar_prefetch=3,
        grid=(n_tiles, ff // tf),
        in_specs=[
            pl.BlockSpec((tm, d), x_map, pipeline_mode=pl.Buffered(1)),
            pl.BlockSpec((1, d, tf), up_map),
            pl.BlockSpec((1, d, tf), up_map),
            pl.BlockSpec((1, tf, d), down_map),
        ],
        out_specs=pl.BlockSpec((tm, d), lambda i, j, te, tr, nu: (i, 0), pipeline_mode=pl.Buffered(1)),
        scratch_shapes=[pltpu.VMEM((tm, d), F32)],
    )
    return pl.pallas_call(
        _moe_kernel,
        out_shape=jax.ShapeDtypeStruct((rows, d), BF16),
        grid_spec=grid_spec,
        compiler_params=_params("arbitrary", "arbitrary"),
        name="moe_experts",
    )(tile_e, tile_rows, n_used, xb, wg, wu, wd)


def _moe_combine_kernel(x_ref, y0_ref, y1_ref, r_ref, o_ref):
    w = r_ref[...]
    o_ref[...] = x_ref[...] + w[:, 0:1] * y0_ref[...].astype(F32) + w[:, 1:2] * y1_ref[...].astype(F32)


def moe_combine(x2d, y0, y1, route, *, tm=512):
    n, d = x2d.shape
    row_block = pl.BlockSpec((tm, d), lambda i: (i, 0))
    return pl.pallas_call(
        _moe_combine_kernel,
        out_shape=jax.ShapeDtypeStruct((n, d), F32),
        grid=(n // tm,),
        in_specs=[row_block, row_block, row_block, pl.BlockSpec((tm, LANES), lambda i: (i, 0))],
        out_specs=row_block,
        compiler_params=_params("parallel"),
        name="moe_combine",
    )(x2d, y0, y1, route)


def moe_layer(x2d, gain, w_router, b_router, wg, wu, wd):
    n, d = x2d.shape
    e = N_EXPERTS
    route, h, counts = router_top2(x2d, gain, w_router, b_router)
    nk = n * MOE_TOP_K
    n_tiles = -(-nk // MOE_FILL) + e
    flat_e = route[:, 2:2 + MOE_TOP_K].astype(jnp.int32).reshape(nk)
    rank = route[:, 4:4 + MOE_TOP_K].astype(jnp.int32).reshape(nk)
    flat_tok = jnp.repeat(jnp.arange(n, dtype=jnp.int32), MOE_TOP_K)
    n_full = counts // MOE_FILL
    rem = counts - n_full * MOE_FILL
    absorbed = (rem > 0) & (rem <= MOE_SUB) & (n_full >= 1)
    tiles_e = n_full + jnp.where((rem > 0) & jnp.logical_not(absorbed), 1, 0)
    tile_end = jnp.cumsum(tiles_e)
    tile_base = tile_end - tiles_e
    tile_in_e = jnp.minimum(rank // MOE_FILL, tiles_e[flat_e] - 1)
    slot = ((tile_base[flat_e] + tile_in_e) * MOE_TILE + rank - tile_in_e * MOE_FILL).astype(jnp.int32)
    n_slots = n_tiles * MOE_TILE
    slot_tok = (jnp.arange(n_slots, dtype=jnp.int32) % n).at[slot].set(flat_tok)
    tile_id = jnp.arange(n_tiles, dtype=jnp.int32)
    n_used = tile_end[-1].astype(jnp.int32).reshape(1)
    used = tile_id < n_used[0]
    tile_e = jnp.minimum(jnp.searchsorted(tile_end, jnp.minimum(tile_id, n_used[0] - 1), side="right"),
                         e - 1).astype(jnp.int32)
    k_in_e = tile_id - tile_base[tile_e]
    is_last = k_in_e == tiles_e[tile_e] - 1
    tile_rows = jnp.where(is_last, counts[tile_e] - k_in_e * MOE_FILL, MOE_FILL)
    tile_rows = jnp.where(used, tile_rows, 0).astype(jnp.int32)

    xb = h[slot_tok]
    yb = moe_experts(xb, tile_e, tile_rows, n_used, wg, wu, wd)
    slot2 = slot.reshape(n, MOE_TOP_K)
    return moe_combine(x2d, yb[slot2[:, 0]], yb[slot2[:, 1]], route)


_SRC = dict(zip(
    ("na_q", "na_k", "na_v", "gla_q", "gla_k", "gla_v", "gla_lr", "gla_og", "gdn_qkv", "gdn_a", "gdn_b",
     "gdn_og", "hg_q", "hg_f", "hg_i", "hg_og", "mem_q", "gates"),
    zip(np.cumsum((0,) + IN_WIDTHS[:-1]).tolist(), IN_WIDTHS)))
_PB_ORDER = ("na_q", "na_k", "na_v", "gla_q", "gla_k", "gla_v", "gla_og", "gdn_qkv", "gdn_og", "hg_q", "hg_i",
             "hg_og", "mem_q", "gates")
_PF_ORDER = ("hg_f", "gla_lr", "gdn_a", "gdn_b")
PB_COL = {}
_c = 0
for _name in _PB_ORDER:
    PB_COL[_name] = _c
    _c += _SRC[_name][1]
PB_WIDTH = _c
PF_COL = {}
_c = 0
for _name in _PF_ORDER:
    PF_COL[_name] = _c
    _c += _SRC[_name][1]
PF_WIDTH = -(-_c // IN_PROJ_TILE) * IN_PROJ_TILE
PF_SMALL_COL = PF_COL["gla_lr"]
GDN_A_LANE = PF_COL["gdn_a"] - PF_SMALL_COL
GDN_B_LANE = PF_COL["gdn_b"] - PF_SMALL_COL


def _rearrange_w_in(w):
    w = w.astype(BF16)
    cols = [w[:, _SRC[n][0]:_SRC[n][0] + _SRC[n][1]] for n in _PB_ORDER + _PF_ORDER]
    cols.append(jnp.zeros((w.shape[0], PB_WIDTH + PF_WIDTH - P_IN), BF16))
    return jnp.concatenate(cols, axis=1)


def _segment_rms(x, gain, seg_ones, seg_width):
    sq = x * x
    hi = sq.astype(BF16)
    lo = (sq - hi.astype(F32)).astype(BF16)
    ss = (jnp.dot(hi, seg_ones, preferred_element_type=F32)
          + jnp.dot(lo, seg_ones, preferred_element_type=F32))
    return x * lax.rsqrt(ss * (1.0 / seg_width) + RMS_EPS) * gain


NA_ROWS_PER_STEP = 8
NA_BAND = NA_WIN_ROWS * GRID_W


def _na_bias_table(rel_bias):
    c = np.arange(GRID_W)
    dc = np.clip(c[None, :] - c[:, None], 1 - NA_WIN_COLS, NA_WIN_COLS - 1) + (NA_WIN_COLS - 1)
    col_start = np.clip(c - NA_WIN_COLS // 2, 0, GRID_W - NA_WIN_COLS)
    col_in = (c[None, :] >= col_start[:, None]) & (c[None, :] < col_start[:, None] + NA_WIN_COLS)
    onehot = (dc[None] == np.arange(2 * NA_WIN_COLS - 1)[:, None, None]).astype(np.float32)
    base = jnp.einsum("hrc,cqk->hrqk", rel_bias.astype(F32), onehot, precision=lax.Precision.HIGHEST)
    base = jnp.where(col_in[None, None], base, MASK_VALUE)
    tables = []
    for cfg in range(NA_WIN_ROWS):
        rows = base[:, NA_WIN_ROWS - 1 - cfg:2 * NA_WIN_ROWS - 1 - cfg]
        tables.append(rows.transpose(0, 2, 1, 3).reshape(NA_HEADS, GRID_W, NA_BAND))
    return jnp.stack(tables)


def _na_kernel(q_ref, k_ref, v_ref, qg_ref, kg_ref, seg_ref, bias_ref, o_ref, kn_ref):
    step = pl.program_id(1)
    rows_total = k_ref.shape[0] // GRID_W
    seg = seg_ref[...]

    @pl.when(step == 0)
    def _():
        def norm_keys(t, carry):
            rows = pl.ds(pl.multiple_of(t * 256, 256), 256)
            kn_ref[rows, :] = _segment_rms(k_ref[rows, :].astype(F32), kg_ref[...], seg, NA_HEAD_DIM).astype(BF16)
            return carry
        lax.fori_loop(0, k_ref.shape[0] // 256, norm_keys, 0)

    lane = lax.broadcasted_iota(jnp.int32, (1, LANES), 1)
    low_half = lane < NA_HEAD_DIM

    def one_row(rr, carry):
        r = step * NA_ROWS_PER_STEP + rr
        row_start = jnp.clip(r - NA_WIN_ROWS // 2, 0, rows_total - NA_WIN_ROWS)
        cfg = r - row_start
        qrows = pl.ds(pl.multiple_of(rr * GRID_W, GRID_W), GRID_W)
        band = pl.ds(pl.multiple_of(row_start * GRID_W, GRID_W), NA_BAND)
        qn = (_segment_rms(q_ref[qrows, :].astype(F32), qg_ref[...], seg, NA_HEAD_DIM)
              * (NA_HEAD_DIM ** -0.5)).astype(BF16)
        heads = [(pair, half) for pair in range(NA_HEADS // 2) for half in range(2)]
        scores = []
        for pair, half in heads:
            cols = slice(pair * LANES, (pair + 1) * LANES)
            qp = qn[:, cols]
            keep = low_half if half == 0 else jnp.logical_not(low_half)
            qm = jnp.where(keep, qp, jnp.zeros_like(qp))
            scores.append(lax.dot_general(qm, kn_ref[band, cols], (((1,), (1,)), ((), ())),
                                          preferred_element_type=F32))
        exps, sums = [], []
        for s, (pair, half) in zip(scores, heads):
            s = s + bias_ref[cfg, 2 * pair + half]
            e = jnp.exp(s - jnp.max(s, axis=-1, keepdims=True))
            sums.append(jnp.sum(e, axis=-1, keepdims=True))
            exps.append(e.astype(BF16))
        outs = [jnp.dot(e, v_ref[band, slice(pair * LANES, (pair + 1) * LANES)], preferred_element_type=F32) / l
                for e, l, (pair, half) in zip(exps, sums, heads)]
        for pair in range(NA_HEADS // 2):
            cols = slice(pair * LANES, (pair + 1) * LANES)
            o_ref[qrows, cols] = jnp.where(low_half, outs[2 * pair], outs[2 * pair + 1]).astype(o_ref.dtype)
        return carry

    lax.fori_loop(0, NA_ROWS_PER_STEP, one_row, 0)


def neighbourhood_attention(pb, batch, q_gain, k_gain, rel_bias):
    m = pb.shape[0]
    s = m // batch
    tq = NA_ROWS_PER_STEP * GRID_W
    steps = s // tq
    qg = jnp.tile(q_gain.astype(F32), NA_HEADS).reshape(1, NA_WIDTH)
    kg = jnp.tile(k_gain.astype(F32), NA_HEADS).reshape(1, NA_WIDTH)
    seg = jnp.asarray(np.kron(np.eye(NA_HEADS), np.ones((NA_HEAD_DIM, NA_HEAD_DIM))), BF16)
    bias = _na_bias_table(rel_bias)
    cq, ck, cv = (PB_COL[n] // NA_WIDTH for n in ("na_q", "na_k", "na_v"))
    return pl.pallas_call(
        _na_kernel,
        out_shape=jax.ShapeDtypeStruct((m, NA_WIDTH), BF16),
        grid=(batch, steps),
        in_specs=[
            pl.BlockSpec((tq, NA_WIDTH), lambda b, t: (b * steps + t, cq)),
            pl.BlockSpec((s, NA_WIDTH), lambda b, t: (b, ck)),
            pl.BlockSpec((s, NA_WIDTH), lambda b, t: (b, cv)),
            pl.BlockSpec((1, NA_WIDTH), lambda b, t: (0, 0)),
            pl.BlockSpec((1, NA_WIDTH), lambda b, t: (0, 0)),
            pl.BlockSpec((NA_WIDTH, NA_WIDTH), lambda b, t: (0, 0)),
            pl.BlockSpec((NA_WIN_ROWS, NA_HEADS, GRID_W, NA_BAND), lambda b, t: (0, 0, 0, 0)),
        ],
        out_specs=pl.BlockSpec((tq, NA_WIDTH), lambda b, t: (b * steps + t, 0)),
        scratch_shapes=[pltpu.VMEM((s, NA_WIDTH), BF16)],
        compiler_params=_params("parallel", "arbitrary"),
        name="neighbourhood_attention",
    )(pb, pb, pb, qg, kg, seg, bias)


def _mem_attn_kernel(q_ref, kv_ref, qg_ref, kg_ref, o_ref, kn_ref):
    @pl.when(pl.program_id(1) == 0)
    def _():
        for h in range(MEM_HEADS):
            cols = slice(h * MEM_HEAD_DIM, (h + 1) * MEM_HEAD_DIM)
            kn_ref[:, cols] = _rms_norm_rows(kv_ref[:, cols].astype(F32), kg_ref[...]).astype(BF16)

    head_cols = [slice(h * MEM_HEAD_DIM, (h + 1) * MEM_HEAD_DIM) for h in range(MEM_HEADS)]
    qns = [_rms_norm_rows(q_ref[:, cols].astype(F32), qg_ref[...]).astype(BF16) for cols in head_cols]
    scores = [lax.dot_general(qn, kn_ref[:, cols], (((1,), (1,)), ((), ())), preferred_element_type=F32)
              for qn, cols in zip(qns, head_cols)]
    exps, sums = [], []
    for s in scores:
        s = s * (MEM_HEAD_DIM ** -0.5)
        e = jnp.exp(s - jnp.max(s, axis=-1, keepdims=True))
        sums.append(jnp.sum(e, axis=-1, keepdims=True))
        exps.append(e.astype(BF16))
    outs = [jnp.dot(e, kv_ref[:, MEM_WIDTH + cols.start:MEM_WIDTH + cols.stop], preferred_element_type=F32)
            for e, cols in zip(exps, head_cols)]
    for o, l, cols in zip(outs, sums, head_cols):
        o_ref[:, cols] = (o / l).astype(o_ref.dtype)


def memory_cross_attention(pb, kv, batch, q_gain, k_gain, *, tq=512):
    m = pb.shape[0]
    steps = m // batch // tq
    n_mem = kv.shape[0] // batch
    cq = PB_COL["mem_q"] // MEM_WIDTH
    return pl.pallas_call(
        _mem_attn_kernel,
        out_shape=jax.ShapeDtypeStruct((m, MEM_WIDTH), BF16),
        grid=(batch, steps),
        in_specs=[
            pl.BlockSpec((tq, MEM_WIDTH), lambda b, t: (b * steps + t, cq)),
            pl.BlockSpec((n_mem, 2 * MEM_WIDTH), lambda b, t: (b, 0)),
            pl.BlockSpec((1, MEM_HEAD_DIM), lambda b, t: (0, 0)),
            pl.BlockSpec((1, MEM_HEAD_DIM), lambda b, t: (0, 0)),
        ],
        out_specs=pl.BlockSpec((tq, MEM_WIDTH), lambda b, t: (b * steps + t, 0)),
        scratch_shapes=[pltpu.VMEM((n_mem, MEM_WIDTH), BF16)],
        compiler_params=_params("parallel", "arbitrary"),
        name="memory_cross_attention",
    )(pb, kv, q_gain.astype(F32).reshape(1, MEM_HEAD_DIM), k_gain.astype(F32).reshape(1, MEM_HEAD_DIM))


LIN_BLOCK = 512
HEAD_V = 128


def _log1p_exp_neg(t):
    return jnp.log(1.0 + jnp.exp(-t))


def _log_sigmoid(x):
    return jnp.minimum(x, 0.0) - _log1p_exp_neg(jnp.abs(x))


def _logaddexp(a, b):
    return jnp.maximum(a, b) + _log1p_exp_neg(jnp.abs(a - b))


def _split_bf16(x, terms):
    parts = []
    for _ in range(terms):
        p = x.astype(BF16)
        parts.append(p)
        x = x - p.astype(F32)
    return parts


def _dot_f32(a, b):
    a_hi, a_lo = _split_bf16(a, 2)
    b_hi, b_lo = _split_bf16(b, 2)
    return (jnp.dot(a_hi, b_hi, preferred_element_type=F32)
            + (jnp.dot(a_hi, b_lo, preferred_element_type=F32) + jnp.dot(a_lo, b_hi, preferred_element_type=F32)))


def _cumsum_rows(mask, x):
    m = jnp.where(mask, 1.0, 0.0).astype(BF16)
    hi, mid, lo = _split_bf16(x, 3)
    return (jnp.dot(m, hi, preferred_element_type=F32)
            + (jnp.dot(m, mid, preferred_element_type=F32) + jnp.dot(m, lo, preferred_element_type=F32)))


def _gla_inputs(refs, rows, direction, params):
    q_ref, k_ref, v_ref, g_ref = refs
    wpad_ref, bias_ref = params
    qc = q_ref[rows, :].astype(F32) * (GLA_HEAD_K ** -0.5)
    kc = k_ref[rows, :].astype(F32)
    gk = _dot_f32(g_ref[rows, :], wpad_ref[direction]) + bias_ref[direction]
    lg = _log_sigmoid(gk) * (1.0 / GLA_GATE_NORMALIZER)
    return qc, kc, v_ref[rows, :], lg


def _hgrn_inputs(refs, rows, direction, params):
    q_ref, v_ref, z_ref = refs
    lb_ref, log_lb_ref, log1m_lb_ref = params
    qr = q_ref[rows, :].astype(F32)
    qc = qr * _sigmoid(qr)
    z = z_ref[rows, :]
    lg = _logaddexp(log_lb_ref[direction], log1m_lb_ref[direction] + _log_sigmoid(z))
    kc = (1.0 - lb_ref[direction]) * _sigmoid(-z)
    return qc, kc, v_ref[rows, :], lg


GDN_CONV_WIDTH = 5
GDN_QKV_WIDTH = 2 * GDN_KEY_WIDTH + GDN_VAL_WIDTH
GDN_HALO = 16


def _gdn_prep_kernel(prev_ref, cur_ref, next_ref, w_ref, o_ref, xp_ref, *, blocks_per_seq):
    i = pl.program_id(0)
    t = cur_ref.shape[0]
    pos = i % blocks_per_seq
    prev = prev_ref[...].astype(F32)
    nxt = next_ref[...].astype(F32)
    xp_ref[0:GDN_HALO, :] = jnp.where(pos == 0, jnp.zeros_like(prev), prev)
    xp_ref[GDN_HALO:GDN_HALO + t, :] = cur_ref[...].astype(F32)
    xp_ref[GDN_HALO + t:, :] = jnp.where(pos == blocks_per_seq - 1, jnp.zeros_like(nxt), nxt)
    half = GDN_CONV_WIDTH // 2
    for g in range(GDN_QKV_WIDTH // LANES):
        cols = slice(g * LANES, (g + 1) * LANES)
        acc = None
        for j in range(GDN_CONV_WIDTH):
            term = xp_ref[GDN_HALO - half + j:GDN_HALO - half + j + t, cols] * w_ref[j:j + 1, cols]
            acc = term if acc is None else acc + term
        y = acc * _sigmoid(acc)
        if g < 2 * GDN_HEADS:
            y = y * lax.rsqrt(jnp.sum(y * y, axis=-1, keepdims=True) + 1e-6)
            if g < GDN_HEADS:
                y = y * (GDN_HEAD_K ** -0.5)
        o_ref[:, cols] = y.astype(o_ref.dtype)


def gdn_prep(pb, batch, conv_w, *, t=512):
    m = pb.shape[0]
    blocks_per_seq = m // batch // t
    halo_per_block = t // GDN_HALO
    col = PB_COL["gdn_qkv"] // GDN_QKV_WIDTH
    last_halo = m // GDN_HALO - 1
    return pl.pallas_call(
        functools.partial(_gdn_prep_kernel, blocks_per_seq=blocks_per_seq),
        out_shape=jax.ShapeDtypeStruct((m, GDN_QKV_WIDTH), BF16),
        grid=(m // t,),
        in_specs=[
            pl.BlockSpec((GDN_HALO, GDN_QKV_WIDTH), lambda i: (jnp.maximum(i * halo_per_block - 1, 0), col)),
            pl.BlockSpec((t, GDN_QKV_WIDTH), lambda i: (i, col)),
            pl.BlockSpec((GDN_HALO, GDN_QKV_WIDTH),
                         lambda i: (jnp.minimum((i + 1) * halo_per_block, last_halo), col)),
            pl.BlockSpec((GDN_CONV_WIDTH, GDN_QKV_WIDTH), lambda i: (0, 0)),
        ],
        out_specs=pl.BlockSpec((t, GDN_QKV_WIDTH), lambda i: (i, 0)),
        scratch_shapes=[pltpu.VMEM((t + 2 * GDN_HALO, GDN_QKV_WIDTH), F32)],
        compiler_params=_params("parallel"),
        name="gdn_prep",
    )(pb, pb, pb, conv_w.astype(F32))


def _softplus(x):
    return jnp.maximum(x, 0.0) + _log1p_exp_neg(jnp.abs(x))


GDN_PACK = GDN_HEADS * GDN_CHUNK
GDN_WY_BLOCK = 512


def _stack_heads(x, width):
    heads = x.shape[1] // width
    lane = lax.broadcasted_iota(jnp.int32, (1, x.shape[1]), 1)
    return jnp.concatenate(
        [jnp.where((lane >= h * width) & (lane < (h + 1) * width), x, 0.0).astype(BF16) for h in range(heads)],
        axis=0)


def _packed_mm(x, y):
    return jnp.dot(x.astype(BF16), _stack_heads(y, GDN_CHUNK), preferred_element_type=F32)


def _packed_inverses(mats):
    c = GDN_CHUNK
    ii = lax.broadcasted_iota(jnp.int32, (c, GDN_PACK), 0)
    jj = lax.broadcasted_iota(jnp.int32, (c, GDN_PACK), 1) % c
    eye = (ii == jj).astype(F32)

    def same_block(s):
        return (ii // s) == (jj // s)

    ds = [jnp.where(same_block(8), a, 0.0) for a in mats]
    d2s = [_packed_mm(d, d) for d in ds]
    d4s = [_packed_mm(d2, d2) for d2 in d2s]
    ts = [_packed_mm(eye - d, eye + d2) for d, d2 in zip(ds, d2s)]
    ts = [_packed_mm(t, eye + d4) for t, d4 in zip(ts, d4s)]
    s = 8
    while s < c:
        off = same_block(2 * s) & jnp.logical_not(same_block(s))
        ets = [_packed_mm(jnp.where(off, a, 0.0), t) for a, t in zip(mats, ts)]
        ts = [t - _packed_mm(t, et) for t, et in zip(ts, ets)]
        s *= 2
    return ts


def _gdn_wy_kernel(qkv_ref, small_ref, a_ref, dtb_ref, selg_ref, selk_ref, selb_ref, *out_refs):
    c = GDN_CHUNK
    n_chunks = qkv_ref.shape[0] // c
    ii = lax.broadcasted_iota(jnp.int32, (c, c), 0)
    jj = lax.broadcasted_iota(jnp.int32, (c, c), 1)
    pi = lax.broadcasted_iota(jnp.int32, (c, GDN_PACK), 0)
    pj = lax.broadcasted_iota(jnp.int32, (c, GDN_PACK), 1) % c
    eye_p = (pi == pj).astype(F32)
    ones_cc = jnp.ones((c, c), BF16)

    problems = [(ch, d) for ch in range(n_chunks) for d in range(2)]
    chunk_in = []
    for ch in range(n_chunks):
        rows = slice(ch * c, (ch + 1) * c)
        qkv = qkv_ref[rows, :]
        small = small_ref[rows, :]
        kf = qkv[:, GDN_KEY_WIDTH:2 * GDN_KEY_WIDTH].astype(F32)
        chunk_in.append(dict(
            qf=qkv[:, :GDN_KEY_WIDTH].astype(F32), kf=kf, vf=qkv[:, 2 * GDN_KEY_WIDTH:].astype(F32),
            kbd=_stack_heads(kf, HEAD_V),
            log_alpha=a_ref[...] * _softplus(small + dtb_ref[...]),
            beta_all=_sigmoid(small)))

    def sel3(x, sel):
        hi, mid, lo = _split_bf16(x, 3)
        return (jnp.dot(hi, sel, preferred_element_type=F32)
                + (jnp.dot(mid, sel, preferred_element_type=F32) + jnp.dot(lo, sel, preferred_element_type=F32)))

    g_all = [_cumsum_rows((jj >= ii) if d else (jj <= ii), chunk_in[ch]["log_alpha"]) for ch, d in problems]
    g_pack = [sel3(g, selg_ref[d]) for g, (ch, d) in zip(g_all, problems)]
    g_wide = [sel3(g, selk_ref[d]) for g, (ch, d) in zip(g_all, problems)]
    beta_w = [sel3(chunk_in[ch]["beta_all"], selb_ref[d]) for ch, d in problems]
    g_rowp = []
    for gp in g_pack:
        hi, mid, lo = _split_bf16(gp * eye_p, 3)
        g_rowp.append(jnp.dot(ones_cc, hi, preferred_element_type=F32)
                      + (jnp.dot(ones_cc, mid, preferred_element_type=F32)
                         + jnp.dot(ones_cc, lo, preferred_element_type=F32)))
    decays, k_betas = [], []
    for gp, gr, bw, (ch, d) in zip(g_pack, g_rowp, beta_w, problems):
        incl = (pj >= pi) if d else (pj <= pi)
        decays.append(jnp.where(incl, jnp.exp(jnp.where(incl, gp - gr, 0.0)), 0.0))
        k_betas.append(chunk_in[ch]["kf"] * bw)
    kq = [lax.dot_general(jnp.concatenate([kb, chunk_in[ch]["qf"]], axis=0).astype(BF16), chunk_in[ch]["kbd"],
                          (((1,), (1,)), ((), ())), preferred_element_type=F32)
          for kb, (ch, d) in zip(k_betas, problems)]
    a_mats = []
    for x, dec, (ch, d) in zip(kq, decays, problems):
        strict = (pj > pi) if d else (pj < pi)
        a_mats.append(jnp.where(strict, x[:c] * dec, 0.0))
    t_invs = _packed_inverses(a_mats)

    for idx, (ch, d) in enumerate(problems):
        u_ref, w_ref, attn_ref, qd_ref, kd_ref, gt_ref = out_refs[6 * d:6 * d + 6]
        rows = slice(ch * c, (ch + 1) * c)
        cin = chunk_in[ch]
        gw = g_wide[idx]
        eg = jnp.exp(gw)
        t_b = t_invs[idx].astype(BF16)
        u_ref[rows, :] = jnp.dot(t_b, _stack_heads(cin["vf"] * beta_w[idx], HEAD_V), preferred_element_type=F32)
        w_ref[rows, :] = jnp.dot(t_b, _stack_heads(k_betas[idx] * eg, HEAD_V),
                                 preferred_element_type=F32).astype(w_ref.dtype)
        attn_ref[rows, :] = (kq[idx][c:] * decays[idx]).astype(attn_ref.dtype)
        end = 0 if d else c - 1
        g_end = gw[end:end + 1, :]
        qd_ref[rows, :] = (cin["qf"] * eg).astype(qd_ref.dtype)
        kd_ref[rows, :] = (cin["kf"] * jnp.exp(g_end - gw)).astype(kd_ref.dtype)
        gt_ref[ch:ch + 1, :] = jnp.exp(g_end)


def gdn_wy(qkv, pf, a_scale, dtb):
    m = qkv.shape[0]
    t = GDN_WY_BLOCK
    cpb = t // GDN_CHUNK
    selg = np.zeros((2, LANES, GDN_PACK), np.float32)
    selk = np.zeros((2, LANES, GDN_VAL_WIDTH), np.float32)
    selb = np.zeros((2, LANES, GDN_VAL_WIDTH), np.float32)
    for d in range(2):
        for h in range(GDN_HEADS):
            selg[d, GDN_A_LANE + d * GDN_HEADS + h, h * GDN_CHUNK:(h + 1) * GDN_CHUNK] = 1.0
            selk[d, GDN_A_LANE + d * GDN_HEADS + h, h * HEAD_V:(h + 1) * HEAD_V] = 1.0
            selb[d, GDN_B_LANE + d * GDN_HEADS + h, h * HEAD_V:(h + 1) * HEAD_V] = 1.0
    wide = GDN_VAL_WIDTH
    out_shape, out_specs = [], []
    for _ in range(2):
        for width, dt in ((wide, F32), (wide, BF16), (GDN_PACK, BF16), (wide, BF16), (wide, BF16)):
            out_shape.append(jax.ShapeDtypeStruct((m, width), dt))
            out_specs.append(pl.BlockSpec((t, width), lambda i: (i, 0)))
        out_shape.append(jax.ShapeDtypeStruct((m // GDN_CHUNK, wide), F32))
        out_specs.append(pl.BlockSpec((cpb, wide), lambda i: (i, 0)))
    return pl.pallas_call(
        _gdn_wy_kernel,
        out_shape=tuple(out_shape),
        grid=(m // t,),
        in_specs=[
            pl.BlockSpec((t, GDN_QKV_WIDTH), lambda i: (i, 0)),
            pl.BlockSpec((t, LANES), lambda i: (i, PF_SMALL_COL // LANES)),
            pl.BlockSpec((1, LANES), lambda i: (0, 0)),
            pl.BlockSpec((1, LANES), lambda i: (0, 0)),
            pl.BlockSpec((2, LANES, GDN_PACK), lambda i: (0, 0, 0)),
            pl.BlockSpec((2, LANES, wide), lambda i: (0, 0, 0)),
            pl.BlockSpec((2, LANES, wide), lambda i: (0, 0, 0)),
        ],
        out_specs=tuple(out_specs),
        compiler_params=_params("parallel"),
        name="gdn_wy",
    )(qkv, pf, a_scale, dtb, jnp.asarray(selg, BF16), jnp.asarray(selk, BF16), jnp.asarray(selb, BF16))


GDN_PAIR = 2 * HEAD_V


def _gdn_scan_kernel(*refs):
    groups = (refs[0:6], refs[6:12])
    out_refs = refs[12:14]
    state_refs = refs[14:16]

    @pl.when(pl.program_id(1) == 0)
    def _():
        for s_ref in state_refs:
            s_ref[...] = jnp.zeros_like(s_ref)

    n_chunks = out_refs[0].shape[0] // GDN_CHUNK
    pairs = GDN_HEADS // 2
    pair_cols = [slice(p * GDN_PAIR, (p + 1) * GDN_PAIR) for p in range(pairs)]
    ri = lax.broadcasted_iota(jnp.int32, (GDN_PAIR, GDN_PAIR), 0) // HEAD_V
    ci = lax.broadcasted_iota(jnp.int32, (GDN_PAIR, GDN_PAIR), 1) // HEAD_V
    diag = ri == ci

    def body(c, carry):
        chunks = (c, n_chunks - 1 - c)
        rows = [pl.ds(pl.multiple_of(ch * GDN_CHUNK, GDN_CHUNK), GDN_CHUNK) for ch in chunks]
        states = [[s_ref[p] for p in range(pairs)] for s_ref in state_refs]
        states_b = [[s.astype(BF16) for s in st] for st in states]
        ws = [[jnp.dot(groups[g][1][rows[g], cols], states_b[g][p], preferred_element_type=F32)
               for p, cols in enumerate(pair_cols)] for g in range(2)]
        qs = [[jnp.dot(groups[g][3][rows[g], cols], states_b[g][p], preferred_element_type=F32)
               for p, cols in enumerate(pair_cols)] for g in range(2)]
        v_new = [groups[g][0][rows[g], :] - jnp.concatenate(ws[g], axis=1) for g in range(2)]
        av = [jnp.dot(groups[g][2][rows[g], :], _stack_heads(v_new[g], HEAD_V), preferred_element_type=F32)
              for g in range(2)]
        v_new_b = [v.astype(BF16) for v in v_new]
        upd = [[lax.dot_general(groups[g][4][rows[g], cols], v_new_b[g][:, cols], (((0,), (0,)), ((), ())),
                                preferred_element_type=F32) for cols in pair_cols] for g in range(2)]
        for g in range(2):
            out_refs[g][rows[g], :] = jnp.concatenate(qs[g], axis=1) + av[g]
            gt = groups[g][5][pl.ds(chunks[g], 1), :]
            for p, cols in enumerate(pair_cols):
                state_refs[g][p] = states[g][p] * gt[:, cols] + jnp.where(diag, upd[g][p], 0.0)
        return carry

    lax.fori_loop(0, n_chunks, body, 0)


def gated_deltanet_branch(pb, pf, batch, conv_w, a_log, dt_bias, norm_gain):
    m = pb.shape[0]
    nb = m // batch // LIN_BLOCK
    cpb = LIN_BLOCK // GDN_CHUNK
    qkv = gdn_prep(pb, batch, conv_w)
    n_gate = 2 * GDN_HEADS
    a_scale = jnp.zeros((1, LANES), F32).at[0, GDN_A_LANE:GDN_A_LANE + n_gate].set(
        -jnp.exp(a_log.astype(F32)).reshape(n_gate))
    dtb = jnp.zeros((1, LANES), F32).at[0, GDN_A_LANE:GDN_A_LANE + n_gate].set(dt_bias.astype(F32).reshape(n_gate))
    wy = gdn_wy(qkv, pf, a_scale, dtb)
    widths = (GDN_VAL_WIDTH, GDN_VAL_WIDTH, GDN_PACK, GDN_VAL_WIDTH, GDN_VAL_WIDTH)

    def fwd(rows, width):
        return pl.BlockSpec((rows, width), lambda b, t: (b * nb + t, 0))

    def bwd(rows, width):
        return pl.BlockSpec((rows, width), lambda b, t: (b * nb + nb - 1 - t, 0))

    in_specs = [fwd(LIN_BLOCK, w) for w in widths] + [fwd(cpb, GDN_VAL_WIDTH)]
    in_specs += [bwd(LIN_BLOCK, w) for w in widths] + [bwd(cpb, GDN_VAL_WIDTH)]
    state = pltpu.VMEM((GDN_HEADS // 2, GDN_PAIR, GDN_PAIR), F32)
    o_f, o_b = pl.pallas_call(
        _gdn_scan_kernel,
        out_shape=(jax.ShapeDtypeStruct((m, GDN_VAL_WIDTH), F32), jax.ShapeDtypeStruct((m, GDN_VAL_WIDTH), F32)),
        grid=(batch, nb),
        in_specs=in_specs,
        out_specs=(fwd(LIN_BLOCK, GDN_VAL_WIDTH), bwd(LIN_BLOCK, GDN_VAL_WIDTH)),
        scratch_shapes=[state, state],
        compiler_params=_params("parallel", "arbitrary"),
        name="gdn_scan",
    )(*wy)
    return RawBranch(o_f, o_b, "gdn_og", norm_gain, True)


def _dot3(m, x):
    hi, mid, lo = _split_bf16(x, 3)
    return (jnp.dot(m, hi, preferred_element_type=F32)
            + (jnp.dot(m, mid, preferred_element_type=F32) + jnp.dot(m, lo, preferred_element_type=F32)))


LIN_CUM_ROWS = 256
LIN_SCORE_ROWS = 128


def _chunk_causal(n, chunk, reverse):
    i = lax.broadcasted_iota(jnp.int32, (n, n), 0)
    j = lax.broadcasted_iota(jnp.int32, (n, n), 1)
    return ((i // chunk) == (j // chunk)) & ((j >= i) if reverse else (j <= i))


def _lin_intra_kernel(*refs, load_inputs, n_in, n_params, heads):
    dir_refs = (refs[:n_in], refs[n_in:2 * n_in])
    params = refs[2 * n_in:2 * n_in + n_params]
    out_refs = refs[2 * n_in + n_params:]
    c = LIN_CHUNK
    dirs = (0, 1)
    loaded = [load_inputs(dir_refs[d], slice(None), d, params) for d in dirs]
    t, w = loaded[0][0].shape
    dk = w // heads
    nc = t // c
    cums = [jnp.where(_chunk_causal(LIN_CUM_ROWS, c, d == 1), 1.0, 0.0).astype(BF16) for d in dirs]
    bs = [jnp.concatenate([_dot3(cums[d], loaded[d][3][r:r + LIN_CUM_ROWS, :])
                           for r in range(0, t, LIN_CUM_ROWS)], axis=0) for d in dirs]
    qes, kes = [], []
    for d in dirs:
        oi_ref, qd_ref, kd_ref, gt_ref = out_refs[4 * d:4 * d + 4]
        qc, kc, vc, lg = loaded[d]
        b = bs[d]
        b3 = b.reshape(nc, c, w)
        mid = c - 1 - c // 2 if d else c // 2
        end = 0 if d else c - 1
        b_mid = jnp.broadcast_to(b3[:, mid:mid + 1, :], (nc, c, w)).reshape(t, w)
        b_end = jnp.broadcast_to(b3[:, end:end + 1, :], (nc, c, w)).reshape(t, w)
        qes.append((qc * jnp.exp(b - b_mid)).astype(BF16))
        kes.append((kc * jnp.exp(b_mid - b)).astype(BF16))
        qd_ref[...] = (qc * jnp.exp(b)).astype(qd_ref.dtype)
        kd_ref[...] = (kc * jnp.exp(b_end - b)).astype(kd_ref.dtype)
        gt_ref[...] = jnp.exp(b3[:, end, :])
    keeps = [_chunk_causal(LIN_SCORE_ROWS, c, d == 1) for d in dirs]
    lane = lax.broadcasted_iota(jnp.int32, (1, LANES), 1)
    for h in range(heads):
        win = slice((h * dk) // LANES * LANES, (h * dk) // LANES * LANES + LANES)
        lo = h * dk - win.start
        vcols = slice(h * HEAD_V, (h + 1) * HEAD_V)
        tiles = [(slice(r, r + LIN_SCORE_ROWS), d) for r in range(0, t, LIN_SCORE_ROWS) for d in dirs]
        scores = []
        for rows, d in tiles:
            qh = qes[d][rows, win]
            if dk < LANES:
                qh = jnp.where((lane >= lo) & (lane < lo + dk), qh, jnp.zeros_like(qh))
            scores.append(lax.dot_general(qh, kes[d][rows, win], (((1,), (1,)), ((), ())),
                                          preferred_element_type=F32))
        probs = [jnp.where(keeps[d], s, 0.0).astype(BF16) for s, (rows, d) in zip(scores, tiles)]
        for p, (rows, d) in zip(probs, tiles):
            out_refs[4 * d][rows, vcols] = jnp.dot(p, loaded[d][2][rows, vcols], preferred_element_type=F32)


def _lin_scan_kernel(*refs, heads, chunk, unroll):
    groups = (refs[0:5], refs[5:10])
    out_refs = refs[10:12]
    state_refs = refs[12:14]

    @pl.when(pl.program_id(1) == 0)
    def _():
        for s_ref in state_refs:
            s_ref[...] = jnp.zeros_like(s_ref)

    n_chunks = out_refs[0].shape[0] // chunk
    w = state_refs[0].shape[1]
    dk = w // heads
    lane = lax.broadcasted_iota(jnp.int32, (1, w), 1)
    masks = [(lane >= h * dk) & (lane < (h + 1) * dk) for h in range(heads)]

    def stack(x):
        return jnp.concatenate([jnp.where(m, x, jnp.zeros_like(x)) for m in masks], axis=0)

    def body(it, carry):
        steps = []
        for u in range(unroll):
            c = it * unroll + u
            steps += [(0, c), (1, n_chunks - 1 - c)]
        prepared = []
        for g, ch in steps:
            rows = pl.ds(pl.multiple_of(ch * chunk, chunk), chunk)
            oi_ref, qd_ref, kd_ref, v_ref, gt_ref = groups[g]
            vc = v_ref[rows, :]
            v4 = jnp.concatenate([vc[:, h * HEAD_V:(h + 1) * HEAD_V] for h in range(heads)], axis=0)
            upd = lax.dot_general(v4, stack(kd_ref[rows, :]), (((0,), (0,)), ((), ())),
                                  preferred_element_type=F32)
            prepared.append((rows, stack(qd_ref[rows, :]), upd, gt_ref[pl.ds(ch, 1), :]))
        states = [s_ref[...] for s_ref in state_refs]
        for (g, ch), (rows, q4, upd, gt) in zip(steps, prepared):
            o_inter = lax.dot_general(q4, states[g].astype(BF16), (((1,), (1,)), ((), ())),
                                      preferred_element_type=F32)
            out_refs[g][rows, :] = groups[g][0][rows, :] + jnp.concatenate(
                [o_inter[h * chunk:(h + 1) * chunk, :] for h in range(heads)], axis=1)
            states[g] = states[g] * gt + upd
        for s_ref, st in zip(state_refs, states):
            s_ref[...] = st
        return carry

    lax.fori_loop(0, n_chunks // unroll, body, 0)


def _bidir_lin_call(name, load_inputs, arrays, col_blocks, widths, params, batch, heads, key_width, v_col):
    m = arrays[0].shape[0]
    t = LIN_BLOCK
    nb = m // batch // t
    cpb = t // LIN_CHUNK
    out_w = heads * HEAD_V
    n_in = len(arrays)

    in_specs, operands = [], []
    for d in range(2):
        for a, wd, cb in zip(arrays, widths, col_blocks):
            in_specs.append(pl.BlockSpec((t, wd), functools.partial(lambda i, c: (i, c), c=cb[d])))
            operands.append(a)
    for p in params:
        in_specs.append(pl.BlockSpec(p.shape, functools.partial(lambda i, nd: (0,) * nd, nd=p.ndim)))
    out_shape, out_specs = [], []
    for _ in range(2):
        for rows_total, rows_blk, width, dt in ((m, t, out_w, F32), (m, t, key_width, BF16),
                                                (m, t, key_width, BF16), (m // LIN_CHUNK, cpb, key_width, F32)):
            out_shape.append(jax.ShapeDtypeStruct((rows_total, width), dt))
            out_specs.append(pl.BlockSpec((rows_blk, width), lambda i: (i, 0)))

    intra = pl.pallas_call(
        functools.partial(_lin_intra_kernel, load_inputs=load_inputs, n_in=n_in, n_params=len(params),
                          heads=heads),
        out_shape=tuple(out_shape),
        grid=(m // t,),
        in_specs=in_specs,
        out_specs=tuple(out_specs),
        compiler_params=_params("parallel"),
        name=name + "_intra",
    )(*operands, *params)

    def fwd(rows, width, col=0):
        return pl.BlockSpec((rows, width), lambda b, s: (b * nb + s, col))

    def bwd(rows, width, col=0):
        return pl.BlockSpec((rows, width), lambda b, s: (b * nb + nb - 1 - s, col))

    scan_specs, scan_ops = [], []
    for d, mk in enumerate((fwd, bwd)):
        oi, qd, kd, gt = intra[4 * d:4 * d + 4]
        scan_specs += [mk(t, out_w), mk(t, key_width), mk(t, key_width), mk(t, out_w, v_col), mk(cpb, key_width)]
        scan_ops += [oi, qd, kd, arrays[0], gt]
    state = pltpu.VMEM((HEAD_V, key_width), F32)
    return pl.pallas_call(
        functools.partial(_lin_scan_kernel, heads=heads, chunk=LIN_CHUNK, unroll=4),
        out_shape=(jax.ShapeDtypeStruct((m, out_w), F32), jax.ShapeDtypeStruct((m, out_w), F32)),
        grid=(batch, nb),
        in_specs=scan_specs,
        out_specs=(fwd(t, out_w), bwd(t, out_w)),
        scratch_shapes=[state, state],
        compiler_params=_params("parallel", "arbitrary"),
        name=name + "_scan",
    )(*scan_ops)


def gla_branch(pb, pf, batch, w_gate_up, b_gate, norm_gain):
    wpad = jnp.zeros((2, LANES, GLA_KEY_WIDTH), F32)
    for d in range(2):
        wpad = wpad.at[d, d * GLA_GATE_RANK:(d + 1) * GLA_GATE_RANK, :].set(w_gate_up[d].astype(F32))
    bias = b_gate.astype(F32).reshape(2, 1, GLA_KEY_WIDTH)
    v_col = PB_COL["gla_v"] // GLA_VAL_WIDTH
    cols = [(PB_COL["gla_q"] // GLA_KEY_WIDTH,) * 2, (PB_COL["gla_k"] // GLA_KEY_WIDTH,) * 2,
            (v_col,) * 2, (PF_SMALL_COL // LANES,) * 2]
    o_f, o_b = _bidir_lin_call("gla", _gla_inputs, [pb, pb, pb, pf], cols,
                               [GLA_KEY_WIDTH, GLA_KEY_WIDTH, GLA_VAL_WIDTH, LANES], [wpad, bias],
                               batch, GLA_HEADS, GLA_KEY_WIDTH, v_col)
    return RawBranch(o_f, o_b, "gla_og", norm_gain, True)


def hgrn2_branch(pb, pf, batch, lower_bound, norm_gain):
    lb = lower_bound.astype(F32).reshape(2, 1, HGRN_KEY_WIDTH)
    log_lb = jnp.log(jnp.maximum(lb, LB_FLOOR))
    log1m_lb = jnp.log1p(-lb)
    zc = PF_COL["hg_f"] // HGRN_KEY_WIDTH
    v_col = PB_COL["hg_i"] // HGRN_VAL_WIDTH
    cols = [(PB_COL["hg_q"] // HGRN_KEY_WIDTH,) * 2, (v_col,) * 2, (zc, zc + 1)]
    o_f, o_b = _bidir_lin_call("hgrn2", _hgrn_inputs, [pb, pb, pf], cols,
                               [HGRN_KEY_WIDTH, HGRN_VAL_WIDTH, HGRN_KEY_WIDTH], [lb, log_lb, log1m_lb],
                               batch, HGRN_HEADS, HGRN_KEY_WIDTH, v_col)
    return RawBranch(o_f, o_b, "hg_og", norm_gain, False)


def kernel(x, mem, g_mix, w_in, na_q_gain, na_k_gain, na_rel_bias, gla_w_gate_up, gla_b_gate, gla_norm_gain, gdn_conv_w, gdn_a_log, gdn_dt_bias, gdn_norm_gain, hgrn_lb_raw, hgrn_norm_gain, g_mem, w_mem_kv, mem_q_gain, mem_k_gain, w_branch, w_out, g_ffn, ffn_w_gate, ffn_w_up, ffn_w_down, moe_w_router, moe_b_router, moe_w_gate, moe_w_up, moe_w_down):
    B, S, D = x.shape
    n_tok = B * S
    lb_w = jax.nn.softmax(hgrn_lb_raw.astype(F32), axis=0)
    hgrn_lb = jnp.cumsum(lb_w, axis=0) - lb_w[0:1]
    x2 = x.reshape(n_tok, D)
    mem2 = mem.reshape(B * mem.shape[1], D)
    for layer in range(DEPTH):
        pb, pf = in_projection(x2, g_mix[layer], _rearrange_w_in(w_in[layer]), PB_COL["gates"], PB_WIDTH)
        kv = rms_matmul(mem2, g_mem[layer], w_mem_kv[layer].astype(BF16), tm=mem2.shape[0], tn=512,
                        out_dtype=BF16)
        branches = [
            neighbourhood_attention(pb, B, na_q_gain[layer], na_k_gain[layer], na_rel_bias[layer]),
            gla_branch(pb, pf, B, gla_w_gate_up[layer], gla_b_gate[layer], gla_norm_gain[layer]),
            gated_deltanet_branch(pb, pf, B, gdn_conv_w[layer], gdn_a_log[layer], gdn_dt_bias[layer],
                                  gdn_norm_gain[layer]),
            hgrn2_branch(pb, pf, B, hgrn_lb[layer], hgrn_norm_gain[layer]),
            memory_cross_attention(pb, kv, B, mem_q_gain[layer], mem_k_gain[layer]),
        ]
        merged = merge_branches(branches, pb, w_branch[layer].astype(BF16), tm=512, tn=512)
        x2 = matmul_residual(merged, w_out[layer].astype(BF16), x2, tm=1024, tn=512)

        j = layer // 2
        if layer % 2 == 0:
            act = rms_swiglu_up(x2, g_ffn[layer], ffn_w_gate[j].astype(BF16), ffn_w_up[j].astype(BF16),
                                tm=1024, tn=512)
            x2 = matmul_residual(act, ffn_w_down[j].astype(BF16), x2, tm=512, tn=512)
        else:
            x2 = moe_layer(x2, g_ffn[layer], moe_w_router[j], moe_b_router[j], moe_w_gate[j], moe_w_up[j],
                           moe_w_down[j])
    return x2.reshape(B, S, D)
```

```python
import functools

import jax
import jax.numpy as jnp
import numpy as np
from jax import lax
from jax.experimental import pallas as pl
from jax.experimental.pallas import tpu as pltpu

F32 = jnp.float32
BF16 = jnp.bfloat16

D_MODEL = 2048
DEPTH = 2
RMS_EPS = 1e-6
MASK_VALUE = -1e30
LB_FLOOR = 1e-30
GRID_W = 64

NA_HEADS = 8
NA_HEAD_DIM = 64
NA_WIDTH = 512
NA_WIN_ROWS = 8
NA_WIN_COLS = 16

GLA_HEADS = 4
GLA_HEAD_K = 64
GLA_HEAD_V = 128
GLA_KEY_WIDTH = 256
GLA_VAL_WIDTH = 512
GLA_GATE_RANK = 16
GLA_GATE_NORMALIZER = 16.0

GDN_HEADS = 4
GDN_HEAD_K = 128
GDN_HEAD_V = 128
GDN_KEY_WIDTH = 512
GDN_VAL_WIDTH = 512
GDN_CHUNK = 64

HGRN_HEADS = 4
HGRN_HEAD_K = 128
HGRN_HEAD_V = 128
HGRN_KEY_WIDTH = 512
HGRN_VAL_WIDTH = 512

LIN_CHUNK = 32

MEM_HEADS = 4
MEM_HEAD_DIM = 128
MEM_WIDTH = 512

N_BRANCH = 5
BRANCH_WIDTH = 512
N_EXPERTS = 8
MOE_TOP_K = 2

IN_WIDTHS = (
    NA_WIDTH, NA_WIDTH, NA_WIDTH,
    GLA_KEY_WIDTH, GLA_KEY_WIDTH, GLA_VAL_WIDTH,
    2 * GLA_GATE_RANK, GLA_VAL_WIDTH,
    2 * GDN_KEY_WIDTH + GDN_VAL_WIDTH,
    2 * GDN_HEADS, 2 * GDN_HEADS, GDN_VAL_WIDTH,
    HGRN_KEY_WIDTH, 2 * HGRN_KEY_WIDTH, HGRN_VAL_WIDTH, HGRN_VAL_WIDTH,
    MEM_WIDTH,
    N_BRANCH * D_MODEL,
)
P_IN = sum(IN_WIDTHS)

V7X_VMEM_BYTES = 64 * 1024 * 1024
VMEM_LIMIT_BYTES = V7X_VMEM_BYTES - 8 * 1024 * 1024
LANES = 128


def _params(*semantics):
    return pltpu.CompilerParams(dimension_semantics=semantics, vmem_limit_bytes=VMEM_LIMIT_BYTES)


def _sigmoid(x):
    return 0.5 * jnp.tanh(0.5 * x) + 0.5


def _rms_norm_rows(x, gain):
    ms = jnp.mean(x * x, axis=-1, keepdims=True)
    return x * lax.rsqrt(ms + RMS_EPS) * gain


def _rms_matmul_kernel(x_ref, g_ref, w_ref, o_ref, h_ref):
    @pl.when(pl.program_id(1) == 0)
    def _():
        h_ref[...] = _rms_norm_rows(x_ref[...], g_ref[...]).astype(BF16)

    o_ref[...] = jnp.dot(h_ref[...], w_ref[...], preferred_element_type=F32).astype(o_ref.dtype)


def rms_matmul(x, gain, w, *, tm, tn, out_dtype=F32):
    m, k = x.shape
    n = w.shape[1]
    return pl.pallas_call(
        _rms_matmul_kernel,
        out_shape=jax.ShapeDtypeStruct((m, n), out_dtype),
        grid=(m // tm, n // tn),
        in_specs=[
            pl.BlockSpec((tm, k), lambda i, j: (i, 0)),
            pl.BlockSpec((1, k), lambda i, j: (0, 0)),
            pl.BlockSpec((k, tn), lambda i, j: (0, j)),
        ],
        out_specs=pl.BlockSpec((tm, tn), lambda i, j: (i, j)),
        scratch_shapes=[pltpu.VMEM((tm, k), BF16)],
        compiler_params=_params("parallel", "arbitrary"),
        name="rms_matmul",
    )(x, gain.reshape(1, k), w)


IN_PROJ_TILE = 512


def _in_projection_kernel(x_ref, g_ref, w_ref, ob_ref, of_ref, h_ref, *, n_plain_tiles, n_bf16_tiles):
    j = pl.program_id(1)

    @pl.when(j == 0)
    def _():
        h_ref[...] = _rms_norm_rows(x_ref[...], g_ref[...]).astype(BF16)

    r = jnp.dot(h_ref[...], w_ref[...], preferred_element_type=F32)

    @pl.when(j < n_plain_tiles)
    def _():
        ob_ref[...] = r.astype(ob_ref.dtype)

    @pl.when(jnp.logical_and(j >= n_plain_tiles, j < n_bf16_tiles))
    def _():
        ob_ref[...] = _sigmoid(r).astype(ob_ref.dtype)

    @pl.when(j >= n_bf16_tiles)
    def _():
        of_ref[...] = r


def in_projection(x, gain, w, n_plain, n_bf16, *, tm=1024):
    m, k = x.shape
    tn = IN_PROJ_TILE
    nb = n_bf16 // tn
    nf = (w.shape[1] - n_bf16) // tn
    return pl.pallas_call(
        functools.partial(_in_projection_kernel, n_plain_tiles=n_plain // tn, n_bf16_tiles=nb),
        out_shape=(jax.ShapeDtypeStruct((m, nb * tn), BF16), jax.ShapeDtypeStruct((m, nf * tn), F32)),
        grid=(m // tm, nb + nf),
        in_specs=[
            pl.BlockSpec((tm, k), lambda i, j: (i, 0)),
            pl.BlockSpec((1, k), lambda i, j: (0, 0)),
            pl.BlockSpec((k, tn), lambda i, j: (0, j)),
        ],
        out_specs=(pl.BlockSpec((tm, tn), lambda i, j: (i, jnp.minimum(j, nb - 1))),
                   pl.BlockSpec((tm, tn), lambda i, j: (i, jnp.maximum(j - nb, 0)))),
        scratch_shapes=[pltpu.VMEM((tm, k), BF16)],
        compiler_params=_params("parallel", "arbitrary"),
        name="in_projection",
    )(x, gain.reshape(1, k), w)


def _ffn_kernel(x_ref, g_ref, wg_ref, wu_ref, wd_ref, o_ref, h_ref, acc_ref):
    j = pl.program_id(1)

    @pl.when(j == 0)
    def _():
        h_ref[...] = _rms_norm_rows(x_ref[...], g_ref[...]).astype(BF16)

    h = h_ref[...]
    a = jnp.dot(h, wg_ref[...], preferred_element_type=F32)
    b = jnp.dot(h, wu_ref[...], preferred_element_type=F32)
    part = jnp.dot((a * _sigmoid(a) * b).astype(BF16), wd_ref[...], preferred_element_type=F32)

    @pl.when(j == 0)
    def _():
        acc_ref[...] = part

    @pl.when(j > 0)
    def _():
        acc_ref[...] += part

    @pl.when(j == pl.num_programs(1) - 1)
    def _():
        o_ref[...] = x_ref[...] + acc_ref[...]


def ffn_layer(x, gain, wg, wu, wd, *, tm, tf):
    m, k = x.shape
    ff = wg.shape[1]
    return pl.pallas_call(
        _ffn_kernel,
        out_shape=jax.ShapeDtypeStruct((m, k), F32),
        grid=(m // tm, ff // tf),
        in_specs=[
            pl.BlockSpec((tm, k), lambda i, j: (i, 0)),
            pl.BlockSpec((1, k), lambda i, j: (0, 0)),
            pl.BlockSpec((k, tf), lambda i, j: (0, j)),
            pl.BlockSpec((k, tf), lambda i, j: (0, j)),
            pl.BlockSpec((tf, k), lambda i, j: (j, 0)),
        ],
        out_specs=pl.BlockSpec((tm, k), lambda i, j: (i, 0)),
        scratch_shapes=[pltpu.VMEM((tm, k), BF16), pltpu.VMEM((tm, k), F32)],
        compiler_params=_params("parallel", "arbitrary"),
        name="ffn_layer",
    )(x, gain.reshape(1, k), wg, wu, wd)


def _matmul_residual_kernel(a_ref, w_ref, r_ref, o_ref):
    o_ref[...] = r_ref[...] + jnp.dot(a_ref[...], w_ref[...], preferred_element_type=F32)


def matmul_residual(a, w, res, *, tm, tn):
    m, k = a.shape
    n = w.shape[1]
    return pl.pallas_call(
        _matmul_residual_kernel,
        out_shape=jax.ShapeDtypeStruct((m, n), F32),
        grid=(m // tm, n // tn),
        in_specs=[
            pl.BlockSpec((tm, k), lambda i, j: (i, 0)),
            pl.BlockSpec((k, tn), lambda i, j: (0, j)),
            pl.BlockSpec((tm, tn), lambda i, j: (i, j)),
        ],
        out_specs=pl.BlockSpec((tm, tn), lambda i, j: (i, j)),
        compiler_params=_params("parallel", "arbitrary"),
        name="matmul_residual",
    )(a, w, res)


class RawBranch:
    def __init__(self, o_fwd, o_bwd, og_name, gain, silu_gate):
        self.o_fwd, self.o_bwd, self.og_name, self.gain, self.silu_gate = o_fwd, o_bwd, og_name, gain, silu_gate


def _merge_kernel(*refs, raw):
    pos = 0
    br = []
    for kind in raw:
        width = 1 if kind is None else 4
        br.append(refs[pos:pos + width])
        pos += width
    gl_refs = refs[pos:pos + N_BRANCH]
    wb_ref, o_ref, fin_ref = refs[pos + N_BRANCH:pos + N_BRANCH + 3]
    raw_slot = {n: s for s, n in enumerate(n for n, kind in enumerate(raw) if kind is not None)}

    @pl.when(pl.program_id(1) == 0)
    def _():
        for n, slot in raw_slot.items():
            of_ref, ob_ref, og_ref, gain_ref = br[n]
            for h in range(BRANCH_WIDTH // LANES):
                cols = slice(h * LANES, (h + 1) * LANES)
                y = _rms_norm_rows(of_ref[:, cols] + ob_ref[:, cols], gain_ref[...])
                g = og_ref[:, cols].astype(F32)
                gate = _sigmoid(g)
                if raw[n]:
                    gate = g * gate
                fin_ref[slot, :, cols] = (y * gate).astype(fin_ref.dtype)

    acc = None
    for n in range(N_BRANCH):
        b = br[n][0][...] if raw[n] is None else fin_ref[raw_slot[n]]
        y = jnp.dot(b, wb_ref[n], preferred_element_type=F32)
        t = gl_refs[n][...].astype(F32) * y
        acc = t if acc is None else acc + t
    o_ref[...] = acc.astype(o_ref.dtype)


def merge_branches(branches, pb, w_branch, *, tm, tn):
    m = pb.shape[0]
    d = D_MODEL
    tiles_per_branch = d // tn
    tile0 = PB_COL["gates"] // tn
    row_block = pl.BlockSpec((tm, BRANCH_WIDTH), lambda i, j: (i, 0))
    in_specs, operands, raw = [], [], []
    for b in branches:
        if isinstance(b, RawBranch):
            og_col = PB_COL[b.og_name] // BRANCH_WIDTH
            in_specs += [row_block, row_block,
                         pl.BlockSpec((tm, BRANCH_WIDTH), functools.partial(lambda i, j, c: (i, c), c=og_col)),
                         pl.BlockSpec((1, LANES), lambda i, j: (0, 0))]
            operands += [b.o_fwd, b.o_bwd, pb, b.gain.astype(F32).reshape(1, LANES)]
            raw.append(b.silu_gate)
        else:
            in_specs.append(row_block)
            operands.append(b)
            raw.append(None)
    in_specs += [
        pl.BlockSpec((tm, tn), functools.partial(lambda i, j, n: (i, tile0 + n * tiles_per_branch + j), n=n))
        for n in range(N_BRANCH)
    ]
    in_specs += [pl.BlockSpec((N_BRANCH, BRANCH_WIDTH, tn), lambda i, j: (0, 0, j))]
    n_raw = sum(kind is not None for kind in raw)
    return pl.pallas_call(
        functools.partial(_merge_kernel, raw=tuple(raw)),
        out_shape=jax.ShapeDtypeStruct((m, d), BF16),
        grid=(m // tm, d // tn),
        in_specs=in_specs,
        out_specs=pl.BlockSpec((tm, tn), lambda i, j: (i, j)),
        scratch_shapes=[pltpu.VMEM((max(n_raw, 1), tm, BRANCH_WIDTH), BF16)],
        compiler_params=_params("parallel", "arbitrary"),
        name="merge_branches",
    )(*operands, *([pb] * N_BRANCH), w_branch)


def _router_kernel(x_ref, g_ref, w_ref, b_ref, o_ref, h_ref, cnt_ref, run_ref, *, n_experts):
    @pl.when(pl.program_id(0) == 0)
    def _():
        run_ref[...] = jnp.zeros_like(run_ref)

    h = _rms_norm_rows(x_ref[...], g_ref[...])
    h_ref[...] = h.astype(h_ref.dtype)
    logits = _dot_f32(h, w_ref[...]) + b_ref[...]
    lane = lax.broadcasted_iota(jnp.int32, logits.shape, 1).astype(F32)
    neg = -jnp.inf
    lm = jnp.where(lane < n_experts, logits, neg)
    m1 = jnp.max(lm, axis=-1, keepdims=True)
    i1 = jnp.min(jnp.where(lm == m1, lane, float(LANES)), axis=-1, keepdims=True)
    lm2 = jnp.where(lane == i1, neg, lm)
    m2 = jnp.max(lm2, axis=-1, keepdims=True)
    i2 = jnp.min(jnp.where(lm2 == m2, lane, float(LANES)), axis=-1, keepdims=True)
    t = jnp.exp(m2 - m1)
    den = 1.0 + t

    tm = logits.shape[0]
    before = (lax.broadcasted_iota(jnp.int32, (tm, tm), 1)
              < lax.broadcasted_iota(jnp.int32, (tm, tm), 0))
    before = jnp.where(before, 1.0, 0.0).astype(BF16)
    pick1 = lane == i1
    pick2 = lane == i2
    oh1 = jnp.where(pick1, 1.0, 0.0)
    oh2 = jnp.where(pick2, 1.0, 0.0)
    pre1 = jnp.dot(before, oh1.astype(BF16), preferred_element_type=F32)
    pre2 = jnp.dot(before, oh2.astype(BF16), preferred_element_type=F32)
    tot1 = jnp.sum(oh1, axis=0, keepdims=True)
    tot2 = jnp.sum(oh2, axis=0, keepdims=True)
    run = run_ref[...]
    rank1 = jnp.sum(jnp.where(pick1, pre1 + run, 0.0), axis=-1, keepdims=True)
    rank2 = jnp.sum(jnp.where(pick2, pre2 + (run + tot1), 0.0), axis=-1, keepdims=True)
    run = run + tot1 + tot2
    run_ref[...] = run
    cnt_ref[...] = jnp.broadcast_to(run, cnt_ref.shape)

    out = jnp.where(lane == 0, 1.0 / den, jnp.where(lane == 1, t / den, jnp.where(lane == 2, i1, i2)))
    out = jnp.where(lane == 4, rank1, jnp.where(lane == 5, rank2, out))
    o_ref[...] = jnp.where(lane < 6, out, 0.0)


def router_top2(x, gain, w_router, b_router, *, tm=512):
    m, k = x.shape
    e = w_router.shape[1]
    w_pad = jnp.zeros((k, LANES), F32).at[:, :e].set(w_router.astype(F32))
    b_pad = jnp.zeros((1, LANES), F32).at[0, :e].set(b_router.astype(F32))
    route, h, cnt = pl.pallas_call(
        functools.partial(_router_kernel, n_experts=e),
        out_shape=(jax.ShapeDtypeStruct((m, LANES), F32), jax.ShapeDtypeStruct((m, k), BF16),
                   jax.ShapeDtypeStruct((8, LANES), F32)),
        grid=(m // tm,),
        in_specs=[
            pl.BlockSpec((tm, k), lambda i: (i, 0)),
            pl.BlockSpec((1, k), lambda i: (0, 0)),
            pl.BlockSpec((k, LANES), lambda i: (0, 0)),
            pl.BlockSpec((1, LANES), lambda i: (0, 0)),
        ],
        out_specs=(pl.BlockSpec((tm, LANES), lambda i: (i, 0)), pl.BlockSpec((tm, k), lambda i: (i, 0)),
                   pl.BlockSpec((8, LANES), lambda i: (0, 0))),
        scratch_shapes=[pltpu.VMEM((1, LANES), F32)],
        compiler_params=_params("arbitrary"),
        name="router_top2",
    )(x, gain.reshape(1, k), w_pad, b_pad)
    return route, h, cnt[0, :e].astype(jnp.int32)


MOE_TILE = 1024
MOE_SUB = 256
MOE_FF_TILE = 512


def _moe_kernel(tile_e_ref, tile_rows_ref, n_used_ref, x_ref, wg_ref, wu_ref, wd_ref, o_ref, acc_ref):
    i = pl.program_id(0)
    j = pl.program_id(1)
    last = pl.num_programs(1) - 1
    valid = tile_rows_ref[i]
    n_sub = (valid + (MOE_SUB - 1)) // MOE_SUB

    for k in range(1, MOE_TILE // MOE_SUB + 1):
        rows = slice(0, k * MOE_SUB)

        @pl.when(n_sub == k)
        def _(rows=rows):
            x = x_ref[rows, :]
            a = jnp.dot(x, wg_ref[0].astype(BF16), preferred_element_type=F32)
            b = jnp.dot(x, wu_ref[0].astype(BF16), preferred_element_type=F32)
            act = (a * _sigmoid(a) * b).astype(BF16)
            part = jnp.dot(act, wd_ref[0].astype(BF16), preferred_element_type=F32)

            @pl.when(j == 0)
            def _():
                acc_ref[rows, :] = part

            @pl.when(j > 0)
            def _():
                acc_ref[rows, :] += part

    for s in range(0, MOE_TILE, MOE_SUB):
        rows = slice(s, s + MOE_SUB)
        filled = s < valid

        @pl.when(jnp.logical_and(filled, j == last))
        def _(rows=rows):
            o_ref[rows, :] = acc_ref[rows, :].astype(o_ref.dtype)

        @pl.when(jnp.logical_and(jnp.logical_not(filled), j == last))
        def _(rows=rows):
            o_ref[rows, :] = jnp.zeros((MOE_SUB, o_ref.shape[1]), o_ref.dtype)


def moe_experts(xb, tile_e, tile_rows, n_used, wg, wu, wd):
    rows, d = xb.shape
    ff = wg.shape[2]
    tm, tf = MOE_TILE, MOE_FF_TILE
    n_tiles = rows // tm
    last_j = ff // tf - 1

    def x_map(i, j, te, tr, nu):
        return (jnp.minimum(i, nu[0] - 1), 0)

    def up_map(i, j, te, tr, nu):
        return (te[i], 0, jnp.where(i < nu[0], j, last_j))

    def down_map(i, j, te, tr, nu):
        return (te[i], jnp.where(i < nu[0], j, last_j), 0)

    grid_spec = pltpu.PrefetchScalarGridSpec(
        num_scalar_prefetch=3,
        grid=(n_tiles, ff // tf),
        in_specs=[
            pl.BlockSpec((tm, d), x_map, pipeline_mode=pl.Buffered(1)),
            pl.BlockSpec((1, d, tf), up_map),
            pl.BlockSpec((1, d, tf), up_map),
            pl.BlockSpec((1, tf, d), down_map),
        ],
        out_specs=pl.BlockSpec((tm, d), lambda i, j, te, tr, nu: (i, 0)),
        scratch_shapes=[pltpu.VMEM((tm, d), F32)],
    )
    return pl.pallas_call(
        _moe_kernel,
        out_shape=jax.ShapeDtypeStruct((rows, d), BF16),
        grid_spec=grid_spec,
        compiler_params=_params("arbitrary", "arbitrary"),
        name="moe_experts",
    )(tile_e, tile_rows, n_used, xb, wg, wu, wd)


def _moe_combine_kernel(x_ref, y0_ref, y1_ref, r_ref, o_ref):
    w = r_ref[...]
    o_ref[...] = x_ref[...] + w[:, 0:1] * y0_ref[...].astype(F32) + w[:, 1:2] * y1_ref[...].astype(F32)


def moe_combine(x2d, y0, y1, route, *, tm=512):
    n, d = x2d.shape
    row_block = pl.BlockSpec((tm, d), lambda i: (i, 0))
    return pl.pallas_call(
        _moe_combine_kernel,
        out_shape=jax.ShapeDtypeStruct((n, d), F32),
        grid=(n // tm,),
        in_specs=[row_block, row_block, row_block, pl.BlockSpec((tm, LANES), lambda i: (i, 0))],
        out_specs=row_block,
        compiler_params=_params("parallel"),
        name="moe_combine",
    )(x2d, y0, y1, route)


def moe_layer(x2d, gain, w_router, b_router, wg, wu, wd):
    n, d = x2d.shape
    e = N_EXPERTS
    route, h, counts = router_top2(x2d, gain, w_router, b_router)
    nk = n * MOE_TOP_K
    n_tiles = -(-nk // MOE_TILE) + e
    flat_e = route[:, 2:2 + MOE_TOP_K].astype(jnp.int32).reshape(nk)
    rank = route[:, 4:4 + MOE_TOP_K].astype(jnp.int32).reshape(nk)
    flat_tok = jnp.repeat(jnp.arange(n, dtype=jnp.int32), MOE_TOP_K)
    padded = (counts + MOE_TILE - 1) // MOE_TILE * MOE_TILE
    pad_end = jnp.cumsum(padded)
    pad_start = pad_end - padded
    slot = (pad_start[flat_e] + rank).astype(jnp.int32)
    n_slots = n_tiles * MOE_TILE
    slot_tok = (jnp.arange(n_slots, dtype=jnp.int32) % n).at[slot].set(flat_tok)
    tile_start = jnp.arange(n_tiles, dtype=jnp.int32) * MOE_TILE
    tile_e = jnp.minimum(jnp.searchsorted(pad_end, tile_start, side="right"), e - 1).astype(jnp.int32)
    tile_rows = jnp.clip(pad_start[tile_e] + counts[tile_e] - tile_start, 0, MOE_TILE).astype(jnp.int32)
    tile_rows = jnp.where(tile_start < pad_end[-1], tile_rows, 0)
    n_used = (pad_end[-1] // MOE_TILE).astype(jnp.int32).reshape(1)
    tile_e = jnp.where(tile_start < pad_end[-1], tile_e, tile_e[jnp.maximum(n_used[0] - 1, 0)])

    xb = h[slot_tok]
    yb = moe_experts(xb, tile_e, tile_rows, n_used, wg, wu, wd)
    slot2 = slot.reshape(n, MOE_TOP_K)
    return moe_combine(x2d, yb[slot2[:, 0]], yb[slot2[:, 1]], route)


_SRC = dict(zip(
    ("na_q", "na_k", "na_v", "gla_q", "gla_k", "gla_v", "gla_lr", "gla_og", "gdn_qkv", "gdn_a", "gdn_b",
     "gdn_og", "hg_q", "hg_f", "hg_i", "hg_og", "mem_q", "gates"),
    zip(np.cumsum((0,) + IN_WIDTHS[:-1]).tolist(), IN_WIDTHS)))
_PB_ORDER = ("na_q", "na_k", "na_v", "gla_q", "gla_k", "gla_v", "gla_og", "gdn_qkv", "gdn_og", "hg_q", "hg_i",
             "hg_og", "mem_q", "gates")
_PF_ORDER = ("hg_f", "gla_lr", "gdn_a", "gdn_b")
PB_COL = {}
_c = 0
for _name in _PB_ORDER:
    PB_COL[_name] = _c
    _c += _SRC[_name][1]
PB_WIDTH = _c
PF_COL = {}
_c = 0
for _name in _PF_ORDER:
    PF_COL[_name] = _c
    _c += _SRC[_name][1]
PF_WIDTH = -(-_c // IN_PROJ_TILE) * IN_PROJ_TILE
PF_SMALL_COL = PF_COL["gla_lr"]
GDN_A_LANE = PF_COL["gdn_a"] - PF_SMALL_COL
GDN_B_LANE = PF_COL["gdn_b"] - PF_SMALL_COL


def _rearrange_w_in(w):
    w = w.astype(BF16)
    cols = [w[:, _SRC[n][0]:_SRC[n][0] + _SRC[n][1]] for n in _PB_ORDER + _PF_ORDER]
    cols.append(jnp.zeros((w.shape[0], PB_WIDTH + PF_WIDTH - P_IN), BF16))
    return jnp.concatenate(cols, axis=1)


def _segment_rms(x, gain, seg_ones, seg_width):
    sq = x * x
    hi = sq.astype(BF16)
    lo = (sq - hi.astype(F32)).astype(BF16)
    ss = (jnp.dot(hi, seg_ones, preferred_element_type=F32)
          + jnp.dot(lo, seg_ones, preferred_element_type=F32))
    return x * lax.rsqrt(ss * (1.0 / seg_width) + RMS_EPS) * gain


NA_ROWS_PER_STEP = 8
NA_BAND = NA_WIN_ROWS * GRID_W


def _na_bias_table(rel_bias):
    c = np.arange(GRID_W)
    dc = np.clip(c[None, :] - c[:, None], 1 - NA_WIN_COLS, NA_WIN_COLS - 1) + (NA_WIN_COLS - 1)
    col_start = np.clip(c - NA_WIN_COLS // 2, 0, GRID_W - NA_WIN_COLS)
    col_in = (c[None, :] >= col_start[:, None]) & (c[None, :] < col_start[:, None] + NA_WIN_COLS)
    onehot = (dc[None] == np.arange(2 * NA_WIN_COLS - 1)[:, None, None]).astype(np.float32)
    base = jnp.einsum("hrc,cqk->hrqk", rel_bias.astype(F32), onehot, precision=lax.Precision.HIGHEST)
    base = jnp.where(col_in[None, None], base, MASK_VALUE)
    tables = []
    for cfg in range(NA_WIN_ROWS):
        rows = base[:, NA_WIN_ROWS - 1 - cfg:2 * NA_WIN_ROWS - 1 - cfg]
        tables.append(rows.transpose(0, 2, 1, 3).reshape(NA_HEADS // 2, 2 * GRID_W, NA_BAND))
    return jnp.stack(tables)


def _na_kernel(q_ref, k_ref, v_ref, qg_ref, kg_ref, seg_ref, bias_ref, o_ref, kn_ref):
    step = pl.program_id(1)
    rows_total = k_ref.shape[0] // GRID_W
    seg = seg_ref[...]

    @pl.when(step == 0)
    def _():
        def norm_keys(t, carry):
            rows = pl.ds(pl.multiple_of(t * 256, 256), 256)
            kn_ref[rows, :] = _segment_rms(k_ref[rows, :].astype(F32), kg_ref[...], seg, NA_HEAD_DIM).astype(BF16)
            return carry
        lax.fori_loop(0, k_ref.shape[0] // 256, norm_keys, 0)

    lane = lax.broadcasted_iota(jnp.int32, (1, LANES), 1)
    low_half = lane < NA_HEAD_DIM

    def one_row(rr, carry):
        r = step * NA_ROWS_PER_STEP + rr
        row_start = jnp.clip(r - NA_WIN_ROWS // 2, 0, rows_total - NA_WIN_ROWS)
        cfg = r - row_start
        qrows = pl.ds(pl.multiple_of(rr * GRID_W, GRID_W), GRID_W)
        band = pl.ds(pl.multiple_of(row_start * GRID_W, GRID_W), NA_BAND)
        qn = (_segment_rms(q_ref[qrows, :].astype(F32), qg_ref[...], seg, NA_HEAD_DIM)
              * (NA_HEAD_DIM ** -0.5)).astype(BF16)
        pairs = range(NA_HEADS // 2)
        pair_cols = [slice(pair * LANES, (pair + 1) * LANES) for pair in pairs]
        scores = []
        for cols in pair_cols:
            qp = qn[:, cols]
            q2 = jnp.concatenate([jnp.where(low_half, qp, jnp.zeros_like(qp)),
                                  jnp.where(low_half, jnp.zeros_like(qp), qp)], axis=0)
            scores.append(lax.dot_general(q2, kn_ref[band, cols], (((1,), (1,)), ((), ())),
                                          preferred_element_type=F32))
        exps, sums = [], []
        for s, pair in zip(scores, pairs):
            s = s + bias_ref[cfg, pair]
            e = jnp.exp(s - jnp.max(s, axis=-1, keepdims=True))
            sums.append(jnp.sum(e, axis=-1, keepdims=True))
            exps.append(e.astype(BF16))
        outs = [jnp.dot(e, v_ref[band, cols], preferred_element_type=F32) / l
                for e, l, cols in zip(exps, sums, pair_cols)]
        for o2, cols in zip(outs, pair_cols):
            o_ref[qrows, cols] = jnp.where(low_half, o2[:GRID_W], o2[GRID_W:]).astype(o_ref.dtype)
        return carry

    lax.fori_loop(0, NA_ROWS_PER_STEP, one_row, 0)


def neighbourhood_attention(pb, batch, q_gain, k_gain, rel_bias):
    m = pb.shape[0]
    s = m // batch
    tq = NA_ROWS_PER_STEP * GRID_W
    steps = s // tq
    qg = jnp.tile(q_gain.astype(F32), NA_HEADS).reshape(1, NA_WIDTH)
    kg = jnp.tile(k_gain.astype(F32), NA_HEADS).reshape(1, NA_WIDTH)
    seg = jnp.asarray(np.kron(np.eye(NA_HEADS), np.ones((NA_HEAD_DIM, NA_HEAD_DIM))), BF16)
    bias = _na_bias_table(rel_bias)
    cq, ck, cv = (PB_COL[n] // NA_WIDTH for n in ("na_q", "na_k", "na_v"))
    return pl.pallas_call(
        _na_kernel,
        out_shape=jax.ShapeDtypeStruct((m, NA_WIDTH), BF16),
        grid=(batch, steps),
        in_specs=[
            pl.BlockSpec((tq, NA_WIDTH), lambda b, t: (b * steps + t, cq)),
            pl.BlockSpec((s, NA_WIDTH), lambda b, t: (b, ck)),
            pl.BlockSpec((s, NA_WIDTH), lambda b, t: (b, cv)),
            pl.BlockSpec((1, NA_WIDTH), lambda b, t: (0, 0)),
            pl.BlockSpec((1, NA_WIDTH), lambda b, t: (0, 0)),
            pl.BlockSpec((NA_WIDTH, NA_WIDTH), lambda b, t: (0, 0)),
            pl.BlockSpec((NA_WIN_ROWS, NA_HEADS // 2, 2 * GRID_W, NA_BAND), lambda b, t: (0, 0, 0, 0)),
        ],
        out_specs=pl.BlockSpec((tq, NA_WIDTH), lambda b, t: (b * steps + t, 0)),
        scratch_shapes=[pltpu.VMEM((s, NA_WIDTH), BF16)],
        compiler_params=_params("parallel", "arbitrary"),
        name="neighbourhood_attention",
    )(pb, pb, pb, qg, kg, seg, bias)


def _mem_attn_kernel(q_ref, kv_ref, qg_ref, kg_ref, o_ref, kn_ref):
    @pl.when(pl.program_id(1) == 0)
    def _():
        for h in range(MEM_HEADS):
            cols = slice(h * MEM_HEAD_DIM, (h + 1) * MEM_HEAD_DIM)
            kn_ref[:, cols] = _rms_norm_rows(kv_ref[:, cols].astype(F32), kg_ref[...]).astype(BF16)

    head_cols = [slice(h * MEM_HEAD_DIM, (h + 1) * MEM_HEAD_DIM) for h in range(MEM_HEADS)]
    qns = [_rms_norm_rows(q_ref[:, cols].astype(F32), qg_ref[...]).astype(BF16) for cols in head_cols]
    scores = [lax.dot_general(qn, kn_ref[:, cols], (((1,), (1,)), ((), ())), preferred_element_type=F32)
              for qn, cols in zip(qns, head_cols)]
    exps, sums = [], []
    for s in scores:
        s = s * (MEM_HEAD_DIM ** -0.5)
        e = jnp.exp(s - jnp.max(s, axis=-1, keepdims=True))
        sums.append(jnp.sum(e, axis=-1, keepdims=True))
        exps.append(e.astype(BF16))
    outs = [jnp.dot(e, kv_ref[:, MEM_WIDTH + cols.start:MEM_WIDTH + cols.stop], preferred_element_type=F32)
            for e, cols in zip(exps, head_cols)]
    for o, l, cols in zip(outs, sums, head_cols):
        o_ref[:, cols] = (o / l).astype(o_ref.dtype)


def memory_cross_attention(pb, kv, batch, q_gain, k_gain, *, tq=512):
    m = pb.shape[0]
    steps = m // batch // tq
    n_mem = kv.shape[0] // batch
    cq = PB_COL["mem_q"] // MEM_WIDTH
    return pl.pallas_call(
        _mem_attn_kernel,
        out_shape=jax.ShapeDtypeStruct((m, MEM_WIDTH), BF16),
        grid=(batch, steps),
        in_specs=[
            pl.BlockSpec((tq, MEM_WIDTH), lambda b, t: (b * steps + t, cq)),
            pl.BlockSpec((n_mem, 2 * MEM_WIDTH), lambda b, t: (b, 0)),
            pl.BlockSpec((1, MEM_HEAD_DIM), lambda b, t: (0, 0)),
            pl.BlockSpec((1, MEM_HEAD_DIM), lambda b, t: (0, 0)),
        ],
        out_specs=pl.BlockSpec((tq, MEM_WIDTH), lambda b, t: (b * steps + t, 0)),
        scratch_shapes=[pltpu.VMEM((n_mem, MEM_WIDTH), BF16)],
        compiler_params=_params("parallel", "arbitrary"),
        name="memory_cross_attention",
    )(pb, kv, q_gain.astype(F32).reshape(1, MEM_HEAD_DIM), k_gain.astype(F32).reshape(1, MEM_HEAD_DIM))


LIN_BLOCK = 512
HEAD_V = 128


def _log1p_exp_neg(t):
    return jnp.log(1.0 + jnp.exp(-t))


def _log_sigmoid(x):
    return jnp.minimum(x, 0.0) - _log1p_exp_neg(jnp.abs(x))


def _logaddexp(a, b):
    return jnp.maximum(a, b) + _log1p_exp_neg(jnp.abs(a - b))


def _split_bf16(x, terms):
    parts = []
    for _ in range(terms):
        p = x.astype(BF16)
        parts.append(p)
        x = x - p.astype(F32)
    return parts


def _dot_f32(a, b):
    a_hi, a_lo = _split_bf16(a, 2)
    b_hi, b_lo = _split_bf16(b, 2)
    return (jnp.dot(a_hi, b_hi, preferred_element_type=F32)
            + (jnp.dot(a_hi, b_lo, preferred_element_type=F32) + jnp.dot(a_lo, b_hi, preferred_element_type=F32)))


def _cumsum_rows(mask, x):
    m = jnp.where(mask, 1.0, 0.0).astype(BF16)
    hi, mid, lo = _split_bf16(x, 3)
    return (jnp.dot(m, hi, preferred_element_type=F32)
            + (jnp.dot(m, mid, preferred_element_type=F32) + jnp.dot(m, lo, preferred_element_type=F32)))


def _gla_inputs(refs, rows, direction, params):
    q_ref, k_ref, v_ref, g_ref = refs
    wpad_ref, bias_ref = params
    qc = q_ref[rows, :].astype(F32) * (GLA_HEAD_K ** -0.5)
    kc = k_ref[rows, :].astype(F32)
    gk = _dot_f32(g_ref[rows, :], wpad_ref[direction]) + bias_ref[direction]
    lg = _log_sigmoid(gk) * (1.0 / GLA_GATE_NORMALIZER)
    return qc, kc, v_ref[rows, :], lg


def _hgrn_inputs(refs, rows, direction, params):
    q_ref, v_ref, z_ref = refs
    lb_ref, log_lb_ref, log1m_lb_ref = params
    qr = q_ref[rows, :].astype(F32)
    qc = qr * _sigmoid(qr)
    z = z_ref[rows, :]
    lg = _logaddexp(log_lb_ref[direction], log1m_lb_ref[direction] + _log_sigmoid(z))
    kc = (1.0 - lb_ref[direction]) * _sigmoid(-z)
    return qc, kc, v_ref[rows, :], lg


GDN_CONV_WIDTH = 5
GDN_QKV_WIDTH = 2 * GDN_KEY_WIDTH + GDN_VAL_WIDTH
GDN_HALO = 16


def _gdn_prep_kernel(prev_ref, cur_ref, next_ref, w_ref, o_ref, xp_ref, *, blocks_per_seq):
    i = pl.program_id(0)
    t = cur_ref.shape[0]
    pos = i % blocks_per_seq
    prev = prev_ref[...].astype(F32)
    nxt = next_ref[...].astype(F32)
    xp_ref[0:GDN_HALO, :] = jnp.where(pos == 0, jnp.zeros_like(prev), prev)
    xp_ref[GDN_HALO:GDN_HALO + t, :] = cur_ref[...].astype(F32)
    xp_ref[GDN_HALO + t:, :] = jnp.where(pos == blocks_per_seq - 1, jnp.zeros_like(nxt), nxt)
    half = GDN_CONV_WIDTH // 2
    for g in range(GDN_QKV_WIDTH // LANES):
        cols = slice(g * LANES, (g + 1) * LANES)
        acc = None
        for j in range(GDN_CONV_WIDTH):
            term = xp_ref[GDN_HALO - half + j:GDN_HALO - half + j + t, cols] * w_ref[j:j + 1, cols]
            acc = term if acc is None else acc + term
        y = acc * _sigmoid(acc)
        if g < 2 * GDN_HEADS:
            y = y * lax.rsqrt(jnp.sum(y * y, axis=-1, keepdims=True) + 1e-6)
            if g < GDN_HEADS:
                y = y * (GDN_HEAD_K ** -0.5)
        o_ref[:, cols] = y.astype(o_ref.dtype)


def gdn_prep(pb, batch, conv_w, *, t=512):
    m = pb.shape[0]
    blocks_per_seq = m // batch // t
    halo_per_block = t // GDN_HALO
    col = PB_COL["gdn_qkv"] // GDN_QKV_WIDTH
    last_halo = m // GDN_HALO - 1
    return pl.pallas_call(
        functools.partial(_gdn_prep_kernel, blocks_per_seq=blocks_per_seq),
        out_shape=jax.ShapeDtypeStruct((m, GDN_QKV_WIDTH), BF16),
        grid=(m // t,),
        in_specs=[
            pl.BlockSpec((GDN_HALO, GDN_QKV_WIDTH), lambda i: (jnp.maximum(i * halo_per_block - 1, 0), col)),
            pl.BlockSpec((t, GDN_QKV_WIDTH), lambda i: (i, col)),
            pl.BlockSpec((GDN_HALO, GDN_QKV_WIDTH),
                         lambda i: (jnp.minimum((i + 1) * halo_per_block, last_halo), col)),
            pl.BlockSpec((GDN_CONV_WIDTH, GDN_QKV_WIDTH), lambda i: (0, 0)),
        ],
        out_specs=pl.BlockSpec((t, GDN_QKV_WIDTH), lambda i: (i, 0)),
        scratch_shapes=[pltpu.VMEM((t + 2 * GDN_HALO, GDN_QKV_WIDTH), F32)],
        compiler_params=_params("parallel"),
        name="gdn_prep",
    )(pb, pb, pb, conv_w.astype(F32))


def _softplus(x):
    return jnp.maximum(x, 0.0) + _log1p_exp_neg(jnp.abs(x))


GDN_PACK = GDN_HEADS * GDN_CHUNK
GDN_WY_BLOCK = 512


def _stack_heads(x, width):
    heads = x.shape[1] // width
    lane = lax.broadcasted_iota(jnp.int32, (1, x.shape[1]), 1)
    return jnp.concatenate(
        [jnp.where((lane >= h * width) & (lane < (h + 1) * width), x, 0.0).astype(BF16) for h in range(heads)],
        axis=0)


def _packed_mm(x, y):
    return jnp.dot(x.astype(BF16), _stack_heads(y, GDN_CHUNK), preferred_element_type=F32)


def _packed_inverses(mats):
    c = GDN_CHUNK
    ii = lax.broadcasted_iota(jnp.int32, (c, GDN_PACK), 0)
    jj = lax.broadcasted_iota(jnp.int32, (c, GDN_PACK), 1) % c
    eye = (ii == jj).astype(F32)

    def same_block(s):
        return (ii // s) == (jj // s)

    ds = [jnp.where(same_block(8), a, 0.0) for a in mats]
    d2s = [_packed_mm(d, d) for d in ds]
    d4s = [_packed_mm(d2, d2) for d2 in d2s]
    ts = [_packed_mm(eye - d, eye + d2) for d, d2 in zip(ds, d2s)]
    ts = [_packed_mm(t, eye + d4) for t, d4 in zip(ts, d4s)]
    s = 8
    while s < c:
        off = same_block(2 * s) & jnp.logical_not(same_block(s))
        ets = [_packed_mm(jnp.where(off, a, 0.0), t) for a, t in zip(mats, ts)]
        ts = [t - _packed_mm(t, et) for t, et in zip(ts, ets)]
        s *= 2
    return ts


def _gdn_wy_kernel(qkv_ref, small_ref, a_ref, dtb_ref, selg_ref, selk_ref, selb_ref, *out_refs):
    c = GDN_CHUNK
    n_chunks = qkv_ref.shape[0] // c
    ii = lax.broadcasted_iota(jnp.int32, (c, c), 0)
    jj = lax.broadcasted_iota(jnp.int32, (c, c), 1)
    pi = lax.broadcasted_iota(jnp.int32, (c, GDN_PACK), 0)
    pj = lax.broadcasted_iota(jnp.int32, (c, GDN_PACK), 1) % c
    eye_p = (pi == pj).astype(F32)
    ones_cc = jnp.ones((c, c), BF16)

    problems = [(ch, d) for ch in range(n_chunks) for d in range(2)]
    chunk_in = []
    for ch in range(n_chunks):
        rows = slice(ch * c, (ch + 1) * c)
        qkv = qkv_ref[rows, :]
        small = small_ref[rows, :]
        kf = qkv[:, GDN_KEY_WIDTH:2 * GDN_KEY_WIDTH].astype(F32)
        chunk_in.append(dict(
            qf=qkv[:, :GDN_KEY_WIDTH].astype(F32), kf=kf, vf=qkv[:, 2 * GDN_KEY_WIDTH:].astype(F32),
            kbd=_stack_heads(kf, HEAD_V),
            log_alpha=a_ref[...] * _softplus(small + dtb_ref[...]),
            beta_all=_sigmoid(small)))

    def sel3(x, sel):
        hi, mid, lo = _split_bf16(x, 3)
        return (jnp.dot(hi, sel, preferred_element_type=F32)
                + (jnp.dot(mid, sel, preferred_element_type=F32) + jnp.dot(lo, sel, preferred_element_type=F32)))

    g_all = [_cumsum_rows((jj >= ii) if d else (jj <= ii), chunk_in[ch]["log_alpha"]) for ch, d in problems]
    g_pack = [sel3(g, selg_ref[d]) for g, (ch, d) in zip(g_all, problems)]
    g_wide = [sel3(g, selk_ref[d]) for g, (ch, d) in zip(g_all, problems)]
    beta_w = [sel3(chunk_in[ch]["beta_all"], selb_ref[d]) for ch, d in problems]
    g_rowp = []
    for gp in g_pack:
        hi, mid, lo = _split_bf16(gp * eye_p, 3)
        g_rowp.append(jnp.dot(ones_cc, hi, preferred_element_type=F32)
                      + (jnp.dot(ones_cc, mid, preferred_element_type=F32)
                         + jnp.dot(ones_cc, lo, preferred_element_type=F32)))
    decays, k_betas = [], []
    for gp, gr, bw, (ch, d) in zip(g_pack, g_rowp, beta_w, problems):
        incl = (pj >= pi) if d else (pj <= pi)
        decays.append(jnp.where(incl, jnp.exp(jnp.where(incl, gp - gr, 0.0)), 0.0))
        k_betas.append(chunk_in[ch]["kf"] * bw)
    kq = [lax.dot_general(jnp.concatenate([kb, chunk_in[ch]["qf"]], axis=0).astype(BF16), chunk_in[ch]["kbd"],
                          (((1,), (1,)), ((), ())), preferred_element_type=F32)
          for kb, (ch, d) in zip(k_betas, problems)]
    a_mats = []
    for x, dec, (ch, d) in zip(kq, decays, problems):
        strict = (pj > pi) if d else (pj < pi)
        a_mats.append(jnp.where(strict, x[:c] * dec, 0.0))
    t_invs = _packed_inverses(a_mats)

    for idx, (ch, d) in enumerate(problems):
        u_ref, w_ref, attn_ref, qd_ref, kd_ref, gt_ref = out_refs[6 * d:6 * d + 6]
        rows = slice(ch * c, (ch + 1) * c)
        cin = chunk_in[ch]
        gw = g_wide[idx]
        eg = jnp.exp(gw)
        t_b = t_invs[idx].astype(BF16)
        u_ref[rows, :] = jnp.dot(t_b, _stack_heads(cin["vf"] * beta_w[idx], HEAD_V), preferred_element_type=F32)
        w_ref[rows, :] = jnp.dot(t_b, _stack_heads(k_betas[idx] * eg, HEAD_V),
                                 preferred_element_type=F32).astype(w_ref.dtype)
        attn_ref[rows, :] = (kq[idx][c:] * decays[idx]).astype(attn_ref.dtype)
        end = 0 if d else c - 1
        g_end = gw[end:end + 1, :]
        qd_ref[rows, :] = (cin["qf"] * eg).astype(qd_ref.dtype)
        kd_ref[rows, :] = (cin["kf"] * jnp.exp(g_end - gw)).astype(kd_ref.dtype)
        gt_ref[ch:ch + 1, :] = jnp.exp(g_end)


def gdn_wy(qkv, pf, a_scale, dtb):
    m = qkv.shape[0]
    t = GDN_WY_BLOCK
    cpb = t // GDN_CHUNK
    selg = np.zeros((2, LANES, GDN_PACK), np.float32)
    selk = np.zeros((2, LANES, GDN_VAL_WIDTH), np.float32)
    selb = np.zeros((2, LANES, GDN_VAL_WIDTH), np.float32)
    for d in range(2):
        for h in range(GDN_HEADS):
            selg[d, GDN_A_LANE + d * GDN_HEADS + h, h * GDN_CHUNK:(h + 1) * GDN_CHUNK] = 1.0
            selk[d, GDN_A_LANE + d * GDN_HEADS + h, h * HEAD_V:(h + 1) * HEAD_V] = 1.0
            selb[d, GDN_B_LANE + d * GDN_HEADS + h, h * HEAD_V:(h + 1) * HEAD_V] = 1.0
    wide = GDN_VAL_WIDTH
    out_shape, out_specs = [], []
    for _ in range(2):
        for width, dt in ((wide, F32), (wide, BF16), (GDN_PACK, BF16), (wide, BF16), (wide, BF16)):
            out_shape.append(jax.ShapeDtypeStruct((m, width), dt))
            out_specs.append(pl.BlockSpec((t, width), lambda i: (i, 0)))
        out_shape.append(jax.ShapeDtypeStruct((m // GDN_CHUNK, wide), F32))
        out_specs.append(pl.BlockSpec((cpb, wide), lambda i: (i, 0)))
    return pl.pallas_call(
        _gdn_wy_kernel,
        out_shape=tuple(out_shape),
        grid=(m // t,),
        in_specs=[
            pl.BlockSpec((t, GDN_QKV_WIDTH), lambda i: (i, 0)),
            pl.BlockSpec((t, LANES), lambda i: (i, PF_SMALL_COL // LANES)),
            pl.BlockSpec((1, LANES), lambda i: (0, 0)),
            pl.BlockSpec((1, LANES), lambda i: (0, 0)),
            pl.BlockSpec((2, LANES, GDN_PACK), lambda i: (0, 0, 0)),
            pl.BlockSpec((2, LANES, wide), lambda i: (0, 0, 0)),
            pl.BlockSpec((2, LANES, wide), lambda i: (0, 0, 0)),
        ],
        out_specs=tuple(out_specs),
        compiler_params=_params("parallel"),
        name="gdn_wy",
    )(qkv, pf, a_scale, dtb, jnp.asarray(selg, BF16), jnp.asarray(selk, BF16), jnp.asarray(selb, BF16))


GDN_PAIR = 2 * HEAD_V


def _gdn_scan_kernel(*refs):
    groups = (refs[0:6], refs[6:12])
    out_refs = refs[12:14]
    state_refs = refs[14:16]

    @pl.when(pl.program_id(1) == 0)
    def _():
        for s_ref in state_refs:
            s_ref[...] = jnp.zeros_like(s_ref)

    n_chunks = out_refs[0].shape[0] // GDN_CHUNK
    pairs = GDN_HEADS // 2
    pair_cols = [slice(p * GDN_PAIR, (p + 1) * GDN_PAIR) for p in range(pairs)]
    ri = lax.broadcasted_iota(jnp.int32, (GDN_PAIR, GDN_PAIR), 0) // HEAD_V
    ci = lax.broadcasted_iota(jnp.int32, (GDN_PAIR, GDN_PAIR), 1) // HEAD_V
    diag = ri == ci

    def body(c, carry):
        chunks = (c, n_chunks - 1 - c)
        rows = [pl.ds(pl.multiple_of(ch * GDN_CHUNK, GDN_CHUNK), GDN_CHUNK) for ch in chunks]
        states = [[s_ref[p] for p in range(pairs)] for s_ref in state_refs]
        states_b = [[s.astype(BF16) for s in st] for st in states]
        ws = [[jnp.dot(groups[g][1][rows[g], cols], states_b[g][p], preferred_element_type=F32)
               for p, cols in enumerate(pair_cols)] for g in range(2)]
        qs = [[jnp.dot(groups[g][3][rows[g], cols], states_b[g][p], preferred_element_type=F32)
               for p, cols in enumerate(pair_cols)] for g in range(2)]
        v_new = [groups[g][0][rows[g], :] - jnp.concatenate(ws[g], axis=1) for g in range(2)]
        av = [jnp.dot(groups[g][2][rows[g], :], _stack_heads(v_new[g], HEAD_V), preferred_element_type=F32)
              for g in range(2)]
        v_new_b = [v.astype(BF16) for v in v_new]
        upd = [[lax.dot_general(groups[g][4][rows[g], cols], v_new_b[g][:, cols], (((0,), (0,)), ((), ())),
                                preferred_element_type=F32) for cols in pair_cols] for g in range(2)]
        for g in range(2):
            out_refs[g][rows[g], :] = jnp.concatenate(qs[g], axis=1) + av[g]
            gt = groups[g][5][pl.ds(chunks[g], 1), :]
            for p, cols in enumerate(pair_cols):
                state_refs[g][p] = states[g][p] * gt[:, cols] + jnp.where(diag, upd[g][p], 0.0)
        return carry

    lax.fori_loop(0, n_chunks, body, 0)


def gated_deltanet_branch(pb, pf, batch, conv_w, a_log, dt_bias, norm_gain):
    m = pb.shape[0]
    nb = m // batch // LIN_BLOCK
    cpb = LIN_BLOCK // GDN_CHUNK
    qkv = gdn_prep(pb, batch, conv_w)
    n_gate = 2 * GDN_HEADS
    a_scale = jnp.zeros((1, LANES), F32).at[0, GDN_A_LANE:GDN_A_LANE + n_gate].set(
        -jnp.exp(a_log.astype(F32)).reshape(n_gate))
    dtb = jnp.zeros((1, LANES), F32).at[0, GDN_A_LANE:GDN_A_LANE + n_gate].set(dt_bias.astype(F32).reshape(n_gate))
    wy = gdn_wy(qkv, pf, a_scale, dtb)
    widths = (GDN_VAL_WIDTH, GDN_VAL_WIDTH, GDN_PACK, GDN_VAL_WIDTH, GDN_VAL_WIDTH)

    def fwd(rows, width):
        return pl.BlockSpec((rows, width), lambda b, t: (b * nb + t, 0))

    def bwd(rows, width):
        return pl.BlockSpec((rows, width), lambda b, t: (b * nb + nb - 1 - t, 0))

    in_specs = [fwd(LIN_BLOCK, w) for w in widths] + [fwd(cpb, GDN_VAL_WIDTH)]
    in_specs += [bwd(LIN_BLOCK, w) for w in widths] + [bwd(cpb, GDN_VAL_WIDTH)]
    state = pltpu.VMEM((GDN_HEADS // 2, GDN_PAIR, GDN_PAIR), F32)
    o_f, o_b = pl.pallas_call(
        _gdn_scan_kernel,
        out_shape=(jax.ShapeDtypeStruct((m, GDN_VAL_WIDTH), F32), jax.ShapeDtypeStruct((m, GDN_VAL_WIDTH), F32)),
        grid=(batch, nb),
        in_specs=in_specs,
        out_specs=(fwd(LIN_BLOCK, GDN_VAL_WIDTH), bwd(LIN_BLOCK, GDN_VAL_WIDTH)),
        scratch_shapes=[state, state],
        compiler_params=_params("parallel", "arbitrary"),
        name="gdn_scan",
    )(*wy)
    return RawBranch(o_f, o_b, "gdn_og", norm_gain, True)


def _dot3(m, x):
    hi, mid, lo = _split_bf16(x, 3)
    return (jnp.dot(m, hi, preferred_element_type=F32)
            + (jnp.dot(m, mid, preferred_element_type=F32) + jnp.dot(m, lo, preferred_element_type=F32)))


LIN_CUM_ROWS = 256
LIN_SCORE_ROWS = 128


def _chunk_causal(n, chunk, reverse):
    i = lax.broadcasted_iota(jnp.int32, (n, n), 0)
    j = lax.broadcasted_iota(jnp.int32, (n, n), 1)
    return ((i // chunk) == (j // chunk)) & ((j >= i) if reverse else (j <= i))


def _lin_intra_kernel(*refs, load_inputs, n_in, n_params, heads):
    dir_refs = (refs[:n_in], refs[n_in:2 * n_in])
    params = refs[2 * n_in:2 * n_in + n_params]
    out_refs = refs[2 * n_in + n_params:]
    c = LIN_CHUNK
    dirs = (0, 1)
    loaded = [load_inputs(dir_refs[d], slice(None), d, params) for d in dirs]
    t, w = loaded[0][0].shape
    dk = w // heads
    nc = t // c
    cums = [jnp.where(_chunk_causal(LIN_CUM_ROWS, c, d == 1), 1.0, 0.0).astype(BF16) for d in dirs]
    bs = [jnp.concatenate([_dot3(cums[d], loaded[d][3][r:r + LIN_CUM_ROWS, :])
                           for r in range(0, t, LIN_CUM_ROWS)], axis=0) for d in dirs]
    qes, kes = [], []
    for d in dirs:
        oi_ref, qd_ref, kd_ref, gt_ref = out_refs[4 * d:4 * d + 4]
        qc, kc, vc, lg = loaded[d]
        b = bs[d]
        b3 = b.reshape(nc, c, w)
        mid = c - 1 - c // 2 if d else c // 2
        end = 0 if d else c - 1
        b_mid = jnp.broadcast_to(b3[:, mid:mid + 1, :], (nc, c, w)).reshape(t, w)
        b_end = jnp.broadcast_to(b3[:, end:end + 1, :], (nc, c, w)).reshape(t, w)
        qes.append((qc * jnp.exp(b - b_mid)).astype(BF16))
        kes.append((kc * jnp.exp(b_mid - b)).astype(BF16))
        qd_ref[...] = (qc * jnp.exp(b)).astype(qd_ref.dtype)
        kd_ref[...] = (kc * jnp.exp(b_end - b)).astype(kd_ref.dtype)
        gt_ref[...] = jnp.exp(b3[:, end, :])
    keeps = [_chunk_causal(LIN_SCORE_ROWS, c, d == 1) for d in dirs]
    lane = lax.broadcasted_iota(jnp.int32, (1, LANES), 1)
    for h in range(heads):
        win = slice((h * dk) // LANES * LANES, (h * dk) // LANES * LANES + LANES)
        lo = h * dk - win.start
        vcols = slice(h * HEAD_V, (h + 1) * HEAD_V)
        tiles = [(slice(r, r + LIN_SCORE_ROWS), d) for r in range(0, t, LIN_SCORE_ROWS) for d in dirs]
        scores = []
        for rows, d in tiles:
            qh = qes[d][rows, win]
            if dk < LANES:
                qh = jnp.where((lane >= lo) & (lane < lo + dk), qh, jnp.zeros_like(qh))
            scores.append(lax.dot_general(qh, kes[d][rows, win], (((1,), (1,)), ((), ())),
                                          preferred_element_type=F32))
        probs = [jnp.where(keeps[d], s, 0.0).astype(BF16) for s, (rows, d) in zip(scores, tiles)]
        for p, (rows, d) in zip(probs, tiles):
            out_refs[4 * d][rows, vcols] = jnp.dot(p, loaded[d][2][rows, vcols], preferred_element_type=F32)


def _lin_scan_kernel(*refs, heads, chunk, unroll):
    groups = (refs[0:5], refs[5:10])
    out_refs = refs[10:12]
    state_refs = refs[12:14]

    @pl.when(pl.program_id(1) == 0)
    def _():
        for s_ref in state_refs:
            s_ref[...] = jnp.zeros_like(s_ref)

    n_chunks = out_refs[0].shape[0] // chunk
    w = state_refs[0].shape[1]
    dk = w // heads
    lane = lax.broadcasted_iota(jnp.int32, (1, w), 1)
    masks = [(lane >= h * dk) & (lane < (h + 1) * dk) for h in range(heads)]

    def stack(x):
        return jnp.concatenate([jnp.where(m, x, jnp.zeros_like(x)) for m in masks], axis=0)

    def body(it, carry):
        steps = []
        for u in range(unroll):
            c = it * unroll + u
            steps += [(0, c), (1, n_chunks - 1 - c)]
        prepared = []
        for g, ch in steps:
            rows = pl.ds(pl.multiple_of(ch * chunk, chunk), chunk)
            oi_ref, qd_ref, kd_ref, v_ref, gt_ref = groups[g]
            vc = v_ref[rows, :]
            v4 = jnp.concatenate([vc[:, h * HEAD_V:(h + 1) * HEAD_V] for h in range(heads)], axis=0)
            upd = lax.dot_general(v4, stack(kd_ref[rows, :]), (((0,), (0,)), ((), ())),
                                  preferred_element_type=F32)
            prepared.append((rows, stack(qd_ref[rows, :]), upd, gt_ref[pl.ds(ch, 1), :]))
        states = [s_ref[...] for s_ref in state_refs]
        for (g, ch), (rows, q4, upd, gt) in zip(steps, prepared):
            o_inter = lax.dot_general(q4, states[g].astype(BF16), (((1,), (1,)), ((), ())),
                                      preferred_element_type=F32)
            out_refs[g][rows, :] = groups[g][0][rows, :] + jnp.concatenate(
                [o_inter[h * chunk:(h + 1) * chunk, :] for h in range(heads)], axis=1)
            states[g] = states[g] * gt + upd
        for s_ref, st in zip(state_refs, states):
            s_ref[...] = st
        return carry

    lax.fori_loop(0, n_chunks // unroll, body, 0)


def _bidir_lin_call(name, load_inputs, arrays, col_blocks, widths, params, batch, heads, key_width, v_col):
    m = arrays[0].shape[0]
    t = LIN_BLOCK
    nb = m // batch // t
    cpb = t // LIN_CHUNK
    out_w = heads * HEAD_V
    n_in = len(arrays)

    in_specs, operands = [], []
    for d in range(2):
        for a, wd, cb in zip(arrays, widths, col_blocks):
            in_specs.append(pl.BlockSpec((t, wd), functools.partial(lambda i, c: (i, c), c=cb[d])))
            operands.append(a)
    for p in params:
        in_specs.append(pl.BlockSpec(p.shape, functools.partial(lambda i, nd: (0,) * nd, nd=p.ndim)))
    out_shape, out_specs = [], []
    for _ in range(2):
        for rows_total, rows_blk, width, dt in ((m, t, out_w, F32), (m, t, key_width, BF16),
                                                (m, t, key_width, BF16), (m // LIN_CHUNK, cpb, key_width, F32)):
            out_shape.append(jax.ShapeDtypeStruct((rows_total, width), dt))
            out_specs.append(pl.BlockSpec((rows_blk, width), lambda i: (i, 0)))

    intra = pl.pallas_call(
        functools.partial(_lin_intra_kernel, load_inputs=load_inputs, n_in=n_in, n_params=len(params),
                          heads=heads),
        out_shape=tuple(out_shape),
        grid=(m // t,),
        in_specs=in_specs,
        out_specs=tuple(out_specs),
        compiler_params=_params("parallel"),
        name=name + "_intra",
    )(*operands, *params)

    def fwd(rows, width, col=0):
        return pl.BlockSpec((rows, width), lambda b, s: (b * nb + s, col))

    def bwd(rows, width, col=0):
        return pl.BlockSpec((rows, width), lambda b, s: (b * nb + nb - 1 - s, col))

    scan_specs, scan_ops = [], []
    for d, mk in enumerate((fwd, bwd)):
        oi, qd, kd, gt = intra[4 * d:4 * d + 4]
        scan_specs += [mk(t, out_w), mk(t, key_width), mk(t, key_width), mk(t, out_w, v_col), mk(cpb, key_width)]
        scan_ops += [oi, qd, kd, arrays[0], gt]
    state = pltpu.VMEM((HEAD_V, key_width), F32)
    return pl.pallas_call(
        functools.partial(_lin_scan_kernel, heads=heads, chunk=LIN_CHUNK, unroll=4),
        out_shape=(jax.ShapeDtypeStruct((m, out_w), F32), jax.ShapeDtypeStruct((m, out_w), F32)),
        grid=(batch, nb),
        in_specs=scan_specs,
        out_specs=(fwd(t, out_w), bwd(t, out_w)),
        scratch_shapes=[state, state],
        compiler_params=_params("parallel", "arbitrary"),
        name=name + "_scan",
    )(*scan_ops)


def gla_branch(pb, pf, batch, w_gate_up, b_gate, norm_gain):
    wpad = jnp.zeros((2, LANES, GLA_KEY_WIDTH), F32)
    for d in range(2):
        wpad = wpad.at[d, d * GLA_GATE_RANK:(d + 1) * GLA_GATE_RANK, :].set(w_gate_up[d].astype(F32))
    bias = b_gate.astype(F32).reshape(2, 1, GLA_KEY_WIDTH)
    v_col = PB_COL["gla_v"] // GLA_VAL_WIDTH
    cols = [(PB_COL["gla_q"] // GLA_KEY_WIDTH,) * 2, (PB_COL["gla_k"] // GLA_KEY_WIDTH,) * 2,
            (v_col,) * 2, (PF_SMALL_COL // LANES,) * 2]
    o_f, o_b = _bidir_lin_call("gla", _gla_inputs, [pb, pb, pb, pf], cols,
                               [GLA_KEY_WIDTH, GLA_KEY_WIDTH, GLA_VAL_WIDTH, LANES], [wpad, bias],
                               batch, GLA_HEADS, GLA_KEY_WIDTH, v_col)
    return RawBranch(o_f, o_b, "gla_og", norm_gain, True)


def hgrn2_branch(pb, pf, batch, lower_bound, norm_gain):
    lb = lower_bound.astype(F32).reshape(2, 1, HGRN_KEY_WIDTH)
    log_lb = jnp.log(jnp.maximum(lb, LB_FLOOR))
    log1m_lb = jnp.log1p(-lb)
    zc = PF_COL["hg_f"] // HGRN_KEY_WIDTH
    v_col = PB_COL["hg_i"] // HGRN_VAL_WIDTH
    cols = [(PB_COL["hg_q"] // HGRN_KEY_WIDTH,) * 2, (v_col,) * 2, (zc, zc + 1)]
    o_f, o_b = _bidir_lin_call("hgrn2", _hgrn_inputs, [pb, pb, pf], cols,
                               [HGRN_KEY_WIDTH, HGRN_VAL_WIDTH, HGRN_KEY_WIDTH], [lb, log_lb, log1m_lb],
                               batch, HGRN_HEADS, HGRN_KEY_WIDTH, v_col)
    return RawBranch(o_f, o_b, "hg_og", norm_gain, False)


def kernel(x, mem, g_mix, w_in, na_q_gain, na_k_gain, na_rel_bias, gla_w_gate_up, gla_b_gate, gla_norm_gain, gdn_conv_w, gdn_a_log, gdn_dt_bias, gdn_norm_gain, hgrn_lb_raw, hgrn_norm_gain, g_mem, w_mem_kv, mem_q_gain, mem_k_gain, w_branch, w_out, g_ffn, ffn_w_gate, ffn_w_up, ffn_w_down, moe_w_router, moe_b_router, moe_w_gate, moe_w_up, moe_w_down):
    B, S, D = x.shape
    n_tok = B * S
    lb_w = jax.nn.softmax(hgrn_lb_raw.astype(F32), axis=0)
    hgrn_lb = jnp.cumsum(lb_w, axis=0) - lb_w[0:1]
    x2 = x.reshape(n_tok, D)
    mem2 = mem.reshape(B * mem.shape[1], D)
    for layer in range(DEPTH):
        pb, pf = in_projection(x2, g_mix[layer], _rearrange_w_in(w_in[layer]), PB_COL["gates"], PB_WIDTH)
        kv = rms_matmul(mem2, g_mem[layer], w_mem_kv[layer].astype(BF16), tm=mem2.shape[0], tn=512,
                        out_dtype=BF16)
        branches = [
            neighbourhood_attention(pb, B, na_q_gain[layer], na_k_gain[layer], na_rel_bias[layer]),
            gla_branch(pb, pf, B, gla_w_gate_up[layer], gla_b_gate[layer], gla_norm_gain[layer]),
            gated_deltanet_branch(pb, pf, B, gdn_conv_w[layer], gdn_a_log[layer], gdn_dt_bias[layer],
                                  gdn_norm_gain[layer]),
            hgrn2_branch(pb, pf, B, hgrn_lb[layer], hgrn_norm_gain[layer]),
            memory_cross_attention(pb, kv, B, mem_q_gain[layer], mem_k_gain[layer]),
        ]
        merged = merge_branches(branches, pb, w_branch[layer].astype(BF16), tm=512, tn=512)
        x2 = matmul_residual(merged, w_out[layer].astype(BF16), x2, tm=1024, tn=512)

        j = layer // 2
        if layer % 2 == 0:
            x2 = ffn_layer(x2, g_ffn[layer], ffn_w_gate[j].astype(BF16), ffn_w_up[j].astype(BF16),
                           ffn_w_down[j].astype(BF16), tm=512, tf=512)
        else:
            x2 = moe_layer(x2, g_ffn[layer], moe_w_router[j], moe_b_router[j], moe_w_gate[j], moe_w_up[j],
                           moe_w_down[j])
    return x2.reshape(B, S, D)
```

```python
import functools

import jax
import jax.numpy as jnp
import numpy as np
from jax import lax
from jax.experimental import pallas as pl
from jax.experimental.pallas import tpu as pltpu

F32 = jnp.float32
BF16 = jnp.bfloat16

D_MODEL = 2048
DEPTH = 2
RMS_EPS = 1e-6
MASK_VALUE = -1e30
LB_FLOOR = 1e-30
GRID_W = 64

NA_HEADS = 8
NA_HEAD_DIM = 64
NA_WIDTH = 512
NA_WIN_ROWS = 8
NA_WIN_COLS = 16

GLA_HEADS = 4
GLA_HEAD_K = 64
GLA_HEAD_V = 128
GLA_KEY_WIDTH = 256
GLA_VAL_WIDTH = 512
GLA_GATE_RANK = 16
GLA_GATE_NORMALIZER = 16.0

GDN_HEADS = 4
GDN_HEAD_K = 128
GDN_HEAD_V = 128
GDN_KEY_WIDTH = 512
GDN_VAL_WIDTH = 512
GDN_CHUNK = 64

HGRN_HEADS = 4
HGRN_HEAD_K = 128
HGRN_HEAD_V = 128
HGRN_KEY_WIDTH = 512
HGRN_VAL_WIDTH = 512

LIN_CHUNK = 32

MEM_HEADS = 4
MEM_HEAD_DIM = 128
MEM_WIDTH = 512

N_BRANCH = 5
BRANCH_WIDTH = 512
N_EXPERTS = 8
MOE_TOP_K = 2

IN_WIDTHS = (
    NA_WIDTH, NA_WIDTH, NA_WIDTH,
    GLA_KEY_WIDTH, GLA_KEY_WIDTH, GLA_VAL_WIDTH,
    2 * GLA_GATE_RANK, GLA_VAL_WIDTH,
    2 * GDN_KEY_WIDTH + GDN_VAL_WIDTH,
    2 * GDN_HEADS, 2 * GDN_HEADS, GDN_VAL_WIDTH,
    HGRN_KEY_WIDTH, 2 * HGRN_KEY_WIDTH, HGRN_VAL_WIDTH, HGRN_VAL_WIDTH,
    MEM_WIDTH,
    N_BRANCH * D_MODEL,
)
P_IN = sum(IN_WIDTHS)

V7X_VMEM_BYTES = 64 * 1024 * 1024
VMEM_LIMIT_BYTES = V7X_VMEM_BYTES - 8 * 1024 * 1024
LANES = 128


def _params(*semantics):
    return pltpu.CompilerParams(dimension_semantics=semantics, vmem_limit_bytes=VMEM_LIMIT_BYTES)


def _sigmoid(x):
    return 0.5 * jnp.tanh(0.5 * x) + 0.5


def _rms_norm_rows(x, gain):
    ms = jnp.mean(x * x, axis=-1, keepdims=True)
    return x * lax.rsqrt(ms + RMS_EPS) * gain


def _rms_matmul_kernel(x_ref, g_ref, w_ref, o_ref, h_ref):
    @pl.when(pl.program_id(1) == 0)
    def _():
        h_ref[...] = _rms_norm_rows(x_ref[...], g_ref[...]).astype(BF16)

    o_ref[...] = jnp.dot(h_ref[...], w_ref[...], preferred_element_type=F32).astype(o_ref.dtype)


def rms_matmul(x, gain, w, *, tm, tn, out_dtype=F32):
    m, k = x.shape
    n = w.shape[1]
    return pl.pallas_call(
        _rms_matmul_kernel,
        out_shape=jax.ShapeDtypeStruct((m, n), out_dtype),
        grid=(m // tm, n // tn),
        in_specs=[
            pl.BlockSpec((tm, k), lambda i, j: (i, 0)),
            pl.BlockSpec((1, k), lambda i, j: (0, 0)),
            pl.BlockSpec((k, tn), lambda i, j: (0, j)),
        ],
        out_specs=pl.BlockSpec((tm, tn), lambda i, j: (i, j)),
        scratch_shapes=[pltpu.VMEM((tm, k), BF16)],
        compiler_params=_params("parallel", "arbitrary"),
        name="rms_matmul",
    )(x, gain.reshape(1, k), w)


IN_PROJ_TILE = 512


def _in_projection_kernel(x_ref, g_ref, w_ref, ob_ref, of_ref, h_ref, *, n_plain_tiles, n_bf16_tiles):
    j = pl.program_id(1)

    @pl.when(j == 0)
    def _():
        h_ref[...] = _rms_norm_rows(x_ref[...], g_ref[...]).astype(BF16)

    r = jnp.dot(h_ref[...], w_ref[...], preferred_element_type=F32)

    @pl.when(j < n_plain_tiles)
    def _():
        ob_ref[...] = r.astype(ob_ref.dtype)

    @pl.when(jnp.logical_and(j >= n_plain_tiles, j < n_bf16_tiles))
    def _():
        ob_ref[...] = _sigmoid(r).astype(ob_ref.dtype)

    @pl.when(j >= n_bf16_tiles)
    def _():
        of_ref[...] = r


def in_projection(x, gain, w, n_plain, n_bf16, *, tm=1024):
    m, k = x.shape
    tn = IN_PROJ_TILE
    nb = n_bf16 // tn
    nf = (w.shape[1] - n_bf16) // tn
    return pl.pallas_call(
        functools.partial(_in_projection_kernel, n_plain_tiles=n_plain // tn, n_bf16_tiles=nb),
        out_shape=(jax.ShapeDtypeStruct((m, nb * tn), BF16), jax.ShapeDtypeStruct((m, nf * tn), F32)),
        grid=(m // tm, nb + nf),
        in_specs=[
            pl.BlockSpec((tm, k), lambda i, j: (i, 0)),
            pl.BlockSpec((1, k), lambda i, j: (0, 0)),
            pl.BlockSpec((k, tn), lambda i, j: (0, j)),
        ],
        out_specs=(pl.BlockSpec((tm, tn), lambda i, j: (i, jnp.minimum(j, nb - 1))),
                   pl.BlockSpec((tm, tn), lambda i, j: (i, jnp.maximum(j - nb, 0)))),
        scratch_shapes=[pltpu.VMEM((tm, k), BF16)],
        compiler_params=_params("parallel", "arbitrary"),
        name="in_projection",
    )(x, gain.reshape(1, k), w)


def _rms_swiglu_kernel(x_ref, g_ref, wg_ref, wu_ref, o_ref, h_ref):
    @pl.when(pl.program_id(1) == 0)
    def _():
        h_ref[...] = _rms_norm_rows(x_ref[...], g_ref[...]).astype(BF16)

    h = h_ref[...]
    a = jnp.dot(h, wg_ref[...], preferred_element_type=F32)
    b = jnp.dot(h, wu_ref[...], preferred_element_type=F32)
    o_ref[...] = (a * _sigmoid(a) * b).astype(o_ref.dtype)


def rms_swiglu_up(x, gain, wg, wu, *, tm, tn):
    m, k = x.shape
    n = wg.shape[1]
    return pl.pallas_call(
        _rms_swiglu_kernel,
        out_shape=jax.ShapeDtypeStruct((m, n), BF16),
        grid=(m // tm, n // tn),
        in_specs=[
            pl.BlockSpec((tm, k), lambda i, j: (i, 0)),
            pl.BlockSpec((1, k), lambda i, j: (0, 0)),
            pl.BlockSpec((k, tn), lambda i, j: (0, j)),
            pl.BlockSpec((k, tn), lambda i, j: (0, j)),
        ],
        out_specs=pl.BlockSpec((tm, tn), lambda i, j: (i, j)),
        scratch_shapes=[pltpu.VMEM((tm, k), BF16)],
        compiler_params=_params("parallel", "arbitrary"),
        name="rms_swiglu_up",
    )(x, gain.reshape(1, k), wg, wu)


def _matmul_residual_kernel(a_ref, w_ref, r_ref, o_ref):
    o_ref[...] = r_ref[...] + jnp.dot(a_ref[...], w_ref[...], preferred_element_type=F32)


def matmul_residual(a, w, res, *, tm, tn):
    m, k = a.shape
    n = w.shape[1]
    return pl.pallas_call(
        _matmul_residual_kernel,
        out_shape=jax.ShapeDtypeStruct((m, n), F32),
        grid=(m // tm, n // tn),
        in_specs=[
            pl.BlockSpec((tm, k), lambda i, j: (i, 0)),
            pl.BlockSpec((k, tn), lambda i, j: (0, j)),
            pl.BlockSpec((tm, tn), lambda i, j: (i, j)),
        ],
        out_specs=pl.BlockSpec((tm, tn), lambda i, j: (i, j)),
        compiler_params=_params("parallel", "arbitrary"),
        name="matmul_residual",
    )(a, w, res)


class RawBranch:
    def __init__(self, o_fwd, o_bwd, og_name, gain, silu_gate):
        self.o_fwd, self.o_bwd, self.og_name, self.gain, self.silu_gate = o_fwd, o_bwd, og_name, gain, silu_gate


def _merge_kernel(*refs, raw):
    pos = 0
    br = []
    for kind in raw:
        width = 1 if kind is None else 4
        br.append(refs[pos:pos + width])
        pos += width
    gl_refs = refs[pos:pos + N_BRANCH]
    wb_ref, o_ref, fin_ref = refs[pos + N_BRANCH:pos + N_BRANCH + 3]
    raw_slot = {n: s for s, n in enumerate(n for n, kind in enumerate(raw) if kind is not None)}

    @pl.when(pl.program_id(1) == 0)
    def _():
        for n, slot in raw_slot.items():
            of_ref, ob_ref, og_ref, gain_ref = br[n]
            for h in range(BRANCH_WIDTH // LANES):
                cols = slice(h * LANES, (h + 1) * LANES)
                y = _rms_norm_rows(of_ref[:, cols] + ob_ref[:, cols], gain_ref[...])
                g = og_ref[:, cols].astype(F32)
                gate = _sigmoid(g)
                if raw[n]:
                    gate = g * gate
                fin_ref[slot, :, cols] = (y * gate).astype(fin_ref.dtype)

    acc = None
    for n in range(N_BRANCH):
        b = br[n][0][...] if raw[n] is None else fin_ref[raw_slot[n]]
        y = jnp.dot(b, wb_ref[n], preferred_element_type=F32)
        t = gl_refs[n][...].astype(F32) * y
        acc = t if acc is None else acc + t
    o_ref[...] = acc.astype(o_ref.dtype)


def merge_branches(branches, pb, w_branch, *, tm, tn):
    m = pb.shape[0]
    d = D_MODEL
    tiles_per_branch = d // tn
    tile0 = PB_COL["gates"] // tn
    row_block = pl.BlockSpec((tm, BRANCH_WIDTH), lambda i, j: (i, 0))
    in_specs, operands, raw = [], [], []
    for b in branches:
        if isinstance(b, RawBranch):
            og_col = PB_COL[b.og_name] // BRANCH_WIDTH
            in_specs += [row_block, row_block,
                         pl.BlockSpec((tm, BRANCH_WIDTH), functools.partial(lambda i, j, c: (i, c), c=og_col)),
                         pl.BlockSpec((1, LANES), lambda i, j: (0, 0))]
            operands += [b.o_fwd, b.o_bwd, pb, b.gain.astype(F32).reshape(1, LANES)]
            raw.append(b.silu_gate)
        else:
            in_specs.append(row_block)
            operands.append(b)
            raw.append(None)
    in_specs += [
        pl.BlockSpec((tm, tn), functools.partial(lambda i, j, n: (i, tile0 + n * tiles_per_branch + j), n=n))
        for n in range(N_BRANCH)
    ]
    in_specs += [pl.BlockSpec((N_BRANCH, BRANCH_WIDTH, tn), lambda i, j: (0, 0, j))]
    n_raw = sum(kind is not None for kind in raw)
    return pl.pallas_call(
        functools.partial(_merge_kernel, raw=tuple(raw)),
        out_shape=jax.ShapeDtypeStruct((m, d), BF16),
        grid=(m // tm, d // tn),
        in_specs=in_specs,
        out_specs=pl.BlockSpec((tm, tn), lambda i, j: (i, j)),
        scratch_shapes=[pltpu.VMEM((max(n_raw, 1), tm, BRANCH_WIDTH), BF16)],
        compiler_params=_params("parallel", "arbitrary"),
        name="merge_branches",
    )(*operands, *([pb] * N_BRANCH), w_branch)


def _router_kernel(x_ref, g_ref, w_ref, b_ref, o_ref, h_ref, cnt_ref, run_ref, *, n_experts):
    @pl.when(pl.program_id(0) == 0)
    def _():
        run_ref[...] = jnp.zeros_like(run_ref)

    h = _rms_norm_rows(x_ref[...], g_ref[...])
    h_ref[...] = h.astype(h_ref.dtype)
    logits = _dot_f32(h, w_ref[...]) + b_ref[...]
    lane = lax.broadcasted_iota(jnp.int32, logits.shape, 1).astype(F32)
    neg = -jnp.inf
    lm = jnp.where(lane < n_experts, logits, neg)
    m1 = jnp.max(lm, axis=-1, keepdims=True)
    i1 = jnp.min(jnp.where(lm == m1, lane, float(LANES)), axis=-1, keepdims=True)
    lm2 = jnp.where(lane == i1, neg, lm)
    m2 = jnp.max(lm2, axis=-1, keepdims=True)
    i2 = jnp.min(jnp.where(lm2 == m2, lane, float(LANES)), axis=-1, keepdims=True)
    t = jnp.exp(m2 - m1)
    den = 1.0 + t

    tm = logits.shape[0]
    before = (lax.broadcasted_iota(jnp.int32, (tm, tm), 1)
              < lax.broadcasted_iota(jnp.int32, (tm, tm), 0))
    before = jnp.where(before, 1.0, 0.0).astype(BF16)
    pick1 = lane == i1
    pick2 = lane == i2
    oh1 = jnp.where(pick1, 1.0, 0.0)
    oh2 = jnp.where(pick2, 1.0, 0.0)
    pre1 = jnp.dot(before, oh1.astype(BF16), preferred_element_type=F32)
    pre2 = jnp.dot(before, oh2.astype(BF16), preferred_element_type=F32)
    tot1 = jnp.sum(oh1, axis=0, keepdims=True)
    tot2 = jnp.sum(oh2, axis=0, keepdims=True)
    run = run_ref[...]
    rank1 = jnp.sum(jnp.where(pick1, pre1 + run, 0.0), axis=-1, keepdims=True)
    rank2 = jnp.sum(jnp.where(pick2, pre2 + (run + tot1), 0.0), axis=-1, keepdims=True)
    run = run + tot1 + tot2
    run_ref[...] = run
    cnt_ref[...] = jnp.broadcast_to(run, cnt_ref.shape)

    out = jnp.where(lane == 0, 1.0 / den, jnp.where(lane == 1, t / den, jnp.where(lane == 2, i1, i2)))
    out = jnp.where(lane == 4, rank1, jnp.where(lane == 5, rank2, out))
    o_ref[...] = jnp.where(lane < 6, out, 0.0)


def router_top2(x, gain, w_router, b_router, *, tm=512):
    m, k = x.shape
    e = w_router.shape[1]
    w_pad = jnp.zeros((k, LANES), F32).at[:, :e].set(w_router.astype(F32))
    b_pad = jnp.zeros((1, LANES), F32).at[0, :e].set(b_router.astype(F32))
    route, h, cnt = pl.pallas_call(
        functools.partial(_router_kernel, n_experts=e),
        out_shape=(jax.ShapeDtypeStruct((m, LANES), F32), jax.ShapeDtypeStruct((m, k), BF16),
                   jax.ShapeDtypeStruct((8, LANES), F32)),
        grid=(m // tm,),
        in_specs=[
            pl.BlockSpec((tm, k), lambda i: (i, 0)),
            pl.BlockSpec((1, k), lambda i: (0, 0)),
            pl.BlockSpec((k, LANES), lambda i: (0, 0)),
            pl.BlockSpec((1, LANES), lambda i: (0, 0)),
        ],
        out_specs=(pl.BlockSpec((tm, LANES), lambda i: (i, 0)), pl.BlockSpec((tm, k), lambda i: (i, 0)),
                   pl.BlockSpec((8, LANES), lambda i: (0, 0))),
        scratch_shapes=[pltpu.VMEM((1, LANES), F32)],
        compiler_params=_params("arbitrary"),
        name="router_top2",
    )(x, gain.reshape(1, k), w_pad, b_pad)
    return route, h, cnt[0, :e].astype(jnp.int32)


MOE_TILE = 1024
MOE_SUB = 256
MOE_FF_TILE = 512


def _moe_kernel(tile_e_ref, tile_rows_ref, n_used_ref, x_ref, wg_ref, wu_ref, wd_ref, o_ref, acc_ref):
    i = pl.program_id(0)
    j = pl.program_id(1)
    last = pl.num_programs(1) - 1
    valid = tile_rows_ref[i]
    n_sub = (valid + (MOE_SUB - 1)) // MOE_SUB

    for k in range(1, MOE_TILE // MOE_SUB + 1):
        rows = slice(0, k * MOE_SUB)

        @pl.when(n_sub == k)
        def _(rows=rows):
            x = x_ref[rows, :]
            a = jnp.dot(x, wg_ref[0].astype(BF16), preferred_element_type=F32)
            b = jnp.dot(x, wu_ref[0].astype(BF16), preferred_element_type=F32)
            act = (a * _sigmoid(a) * b).astype(BF16)
            part = jnp.dot(act, wd_ref[0].astype(BF16), preferred_element_type=F32)

            @pl.when(j == 0)
            def _():
                acc_ref[rows, :] = part

            @pl.when(j > 0)
            def _():
                acc_ref[rows, :] += part

    for s in range(0, MOE_TILE, MOE_SUB):
        rows = slice(s, s + MOE_SUB)
        filled = s < valid

        @pl.when(jnp.logical_and(filled, j == last))
        def _(rows=rows):
            o_ref[rows, :] = acc_ref[rows, :].astype(o_ref.dtype)

        @pl.when(jnp.logical_and(jnp.logical_not(filled), j == last))
        def _(rows=rows):
            o_ref[rows, :] = jnp.zeros((MOE_SUB, o_ref.shape[1]), o_ref.dtype)


def moe_experts(xb, tile_e, tile_rows, n_used, wg, wu, wd):
    rows, d = xb.shape
    ff = wg.shape[2]
    tm, tf = MOE_TILE, MOE_FF_TILE
    n_tiles = rows // tm
    last_j = ff // tf - 1

    def x_map(i, j, te, tr, nu):
        return (jnp.minimum(i, nu[0] - 1), 0)

    def up_map(i, j, te, tr, nu):
        return (te[i], 0, jnp.where(i < nu[0], j, last_j))

    def down_map(i, j, te, tr, nu):
        return (te[i], jnp.where(i < nu[0], j, last_j), 0)

    grid_spec = pltpu.PrefetchScalarGridSpec(
        num_scalar_prefetch=3,
        grid=(n_tiles, ff // tf),
        in_specs=[
            pl.BlockSpec((tm, d), x_map, pipeline_mode=pl.Buffered(1)),
            pl.BlockSpec((1, d, tf), up_map),
            pl.BlockSpec((1, d, tf), up_map),
            pl.BlockSpec((1, tf, d), down_map),
        ],
        out_specs=pl.BlockSpec((tm, d), lambda i, j, te, tr, nu: (i, 0)),
        scratch_shapes=[pltpu.VMEM((tm, d), F32)],
    )
    return pl.pallas_call(
        _moe_kernel,
        out_shape=jax.ShapeDtypeStruct((rows, d), BF16),
        grid_spec=grid_spec,
        compiler_params=_params("arbitrary", "arbitrary"),
        name="moe_experts",
    )(tile_e, tile_rows, n_used, xb, wg, wu, wd)


def _moe_combine_kernel(x_ref, y0_ref, y1_ref, r_ref, o_ref):
    w = r_ref[...]
    o_ref[...] = x_ref[...] + w[:, 0:1] * y0_ref[...].astype(F32) + w[:, 1:2] * y1_ref[...].astype(F32)


def moe_combine(x2d, y0, y1, route, *, tm=512):
    n, d = x2d.shape
    row_block = pl.BlockSpec((tm, d), lambda i: (i, 0))
    return pl.pallas_call(
        _moe_combine_kernel,
        out_shape=jax.ShapeDtypeStruct((n, d), F32),
        grid=(n // tm,),
        in_specs=[row_block, row_block, row_block, pl.BlockSpec((tm, LANES), lambda i: (i, 0))],
        out_specs=row_block,
        compiler_params=_params("parallel"),
        name="moe_combine",
    )(x2d, y0, y1, route)


def moe_layer(x2d, gain, w_router, b_router, wg, wu, wd):
    n, d = x2d.shape
    e = N_EXPERTS
    route, h, counts = router_top2(x2d, gain, w_router, b_router)
    nk = n * MOE_TOP_K
    n_tiles = -(-nk // MOE_TILE) + e
    flat_e = route[:, 2:2 + MOE_TOP_K].astype(jnp.int32).reshape(nk)
    rank = route[:, 4:4 + MOE_TOP_K].astype(jnp.int32).reshape(nk)
    flat_tok = jnp.repeat(jnp.arange(n, dtype=jnp.int32), MOE_TOP_K)
    padded = (counts + MOE_TILE - 1) // MOE_TILE * MOE_TILE
    pad_end = jnp.cumsum(padded)
    pad_start = pad_end - padded
    slot = (pad_start[flat_e] + rank).astype(jnp.int32)
    n_slots = n_tiles * MOE_TILE
    slot_tok = (jnp.arange(n_slots, dtype=jnp.int32) % n).at[slot].set(flat_tok)
    tile_start = jnp.arange(n_tiles, dtype=jnp.int32) * MOE_TILE
    tile_e = jnp.minimum(jnp.searchsorted(pad_end, tile_start, side="right"), e - 1).astype(jnp.int32)
    tile_rows = jnp.clip(pad_start[tile_e] + counts[tile_e] - tile_start, 0, MOE_TILE).astype(jnp.int32)
    tile_rows = jnp.where(tile_start < pad_end[-1], tile_rows, 0)
    n_used = (pad_end[-1] // MOE_TILE).astype(jnp.int32).reshape(1)
    tile_e = jnp.where(tile_start < pad_end[-1], tile_e, tile_e[jnp.maximum(n_used[0] - 1, 0)])

    xb = h[slot_tok]
    yb = moe_experts(xb, tile_e, tile_rows, n_used, wg, wu, wd)
    slot2 = slot.reshape(n, MOE_TOP_K)
    return moe_combine(x2d, yb[slot2[:, 0]], yb[slot2[:, 1]], route)


_SRC = dict(zip(
    ("na_q", "na_k", "na_v", "gla_q", "gla_k", "gla_v", "gla_lr", "gla_og", "gdn_qkv", "gdn_a", "gdn_b",
     "gdn_og", "hg_q", "hg_f", "hg_i", "hg_og", "mem_q", "gates"),
    zip(np.cumsum((0,) + IN_WIDTHS[:-1]).tolist(), IN_WIDTHS)))
_PB_ORDER = ("na_q", "na_k", "na_v", "gla_q", "gla_k", "gla_v", "gla_og", "gdn_qkv", "gdn_og", "hg_q", "hg_i",
             "hg_og", "mem_q", "gates")
_PF_ORDER = ("hg_f", "gla_lr", "gdn_a", "gdn_b")
PB_COL = {}
_c = 0
for _name in _PB_ORDER:
    PB_COL[_name] = _c
    _c += _SRC[_name][1]
PB_WIDTH = _c
PF_COL = {}
_c = 0
for _name in _PF_ORDER:
    PF_COL[_name] = _c
    _c += _SRC[_name][1]
PF_WIDTH = -(-_c // IN_PROJ_TILE) * IN_PROJ_TILE
PF_SMALL_COL = PF_COL["gla_lr"]
GDN_A_LANE = PF_COL["gdn_a"] - PF_SMALL_COL
GDN_B_LANE = PF_COL["gdn_b"] - PF_SMALL_COL


def _rearrange_w_in(w):
    w = w.astype(BF16)
    cols = [w[:, _SRC[n][0]:_SRC[n][0] + _SRC[n][1]] for n in _PB_ORDER + _PF_ORDER]
    cols.append(jnp.zeros((w.shape[0], PB_WIDTH + PF_WIDTH - P_IN), BF16))
    return jnp.concatenate(cols, axis=1)


def _segment_rms(x, gain, seg_ones, seg_width):
    sq = x * x
    hi = sq.astype(BF16)
    lo = (sq - hi.astype(F32)).astype(BF16)
    ss = (jnp.dot(hi, seg_ones, preferred_element_type=F32)
          + jnp.dot(lo, seg_ones, preferred_element_type=F32))
    return x * lax.rsqrt(ss * (1.0 / seg_width) + RMS_EPS) * gain


NA_ROWS_PER_STEP = 8
NA_BAND = NA_WIN_ROWS * GRID_W


def _na_bias_table(rel_bias):
    c = np.arange(GRID_W)
    dc = np.clip(c[None, :] - c[:, None], 1 - NA_WIN_COLS, NA_WIN_COLS - 1) + (NA_WIN_COLS - 1)
    col_start = np.clip(c - NA_WIN_COLS // 2, 0, GRID_W - NA_WIN_COLS)
    col_in = (c[None, :] >= col_start[:, None]) & (c[None, :] < col_start[:, None] + NA_WIN_COLS)
    onehot = (dc[None] == np.arange(2 * NA_WIN_COLS - 1)[:, None, None]).astype(np.float32)
    base = jnp.einsum("hrc,cqk->hrqk", rel_bias.astype(F32), onehot, precision=lax.Precision.HIGHEST)
    base = jnp.where(col_in[None, None], base, MASK_VALUE)
    tables = []
    for cfg in range(NA_WIN_ROWS):
        rows = base[:, NA_WIN_ROWS - 1 - cfg:2 * NA_WIN_ROWS - 1 - cfg]
        tables.append(rows.transpose(0, 2, 1, 3).reshape(NA_HEADS // 2, 2 * GRID_W, NA_BAND))
    return jnp.stack(tables)


def _na_kernel(q_ref, k_ref, v_ref, qg_ref, kg_ref, seg_ref, bias_ref, o_ref, kn_ref):
    step = pl.program_id(1)
    rows_total = k_ref.shape[0] // GRID_W
    seg = seg_ref[...]

    @pl.when(step == 0)
    def _():
        def norm_keys(t, carry):
            rows = pl.ds(pl.multiple_of(t * 256, 256), 256)
            kn_ref[rows, :] = _segment_rms(k_ref[rows, :].astype(F32), kg_ref[...], seg, NA_HEAD_DIM).astype(BF16)
            return carry
        lax.fori_loop(0, k_ref.shape[0] // 256, norm_keys, 0)

    lane = lax.broadcasted_iota(jnp.int32, (1, LANES), 1)
    low_half = lane < NA_HEAD_DIM

    def one_row(rr, carry):
        r = step * NA_ROWS_PER_STEP + rr
        row_start = jnp.clip(r - NA_WIN_ROWS // 2, 0, rows_total - NA_WIN_ROWS)
        cfg = r - row_start
        qrows = pl.ds(pl.multiple_of(rr * GRID_W, GRID_W), GRID_W)
        band = pl.ds(pl.multiple_of(row_start * GRID_W, GRID_W), NA_BAND)
        qn = (_segment_rms(q_ref[qrows, :].astype(F32), qg_ref[...], seg, NA_HEAD_DIM)
              * (NA_HEAD_DIM ** -0.5)).astype(BF16)
        pairs = range(NA_HEADS // 2)
        pair_cols = [slice(pair * LANES, (pair + 1) * LANES) for pair in pairs]
        scores = []
        for cols in pair_cols:
            qp = qn[:, cols]
            q2 = jnp.concatenate([jnp.where(low_half, qp, jnp.zeros_like(qp)),
                                  jnp.where(low_half, jnp.zeros_like(qp), qp)], axis=0)
            scores.append(lax.dot_general(q2, kn_ref[band, cols], (((1,), (1,)), ((), ())),
                                          preferred_element_type=F32))
        exps, sums = [], []
        for s, pair in zip(scores, pairs):
            s = s + bias_ref[cfg, pair]
            e = jnp.exp(s - jnp.max(s, axis=-1, keepdims=True))
            sums.append(jnp.sum(e, axis=-1, keepdims=True))
            exps.append(e.astype(BF16))
        outs = [jnp.dot(e, v_ref[band, cols], preferred_element_type=F32) / l
                for e, l, cols in zip(exps, sums, pair_cols)]
        for o2, cols in zip(outs, pair_cols):
            o_ref[qrows, cols] = jnp.where(low_half, o2[:GRID_W], o2[GRID_W:]).astype(o_ref.dtype)
        return carry

    lax.fori_loop(0, NA_ROWS_PER_STEP, one_row, 0)


def neighbourhood_attention(pb, batch, q_gain, k_gain, rel_bias):
    m = pb.shape[0]
    s = m // batch
    tq = NA_ROWS_PER_STEP * GRID_W
    steps = s // tq
    qg = jnp.tile(q_gain.astype(F32), NA_HEADS).reshape(1, NA_WIDTH)
    kg = jnp.tile(k_gain.astype(F32), NA_HEADS).reshape(1, NA_WIDTH)
    seg = jnp.asarray(np.kron(np.eye(NA_HEADS), np.ones((NA_HEAD_DIM, NA_HEAD_DIM))), BF16)
    bias = _na_bias_table(rel_bias)
    cq, ck, cv = (PB_COL[n] // NA_WIDTH for n in ("na_q", "na_k", "na_v"))
    return pl.pallas_call(
        _na_kernel,
        out_shape=jax.ShapeDtypeStruct((m, NA_WIDTH), BF16),
        grid=(batch, steps),
        in_specs=[
            pl.BlockSpec((tq, NA_WIDTH), lambda b, t: (b * steps + t, cq)),
            pl.BlockSpec((s, NA_WIDTH), lambda b, t: (b, ck)),
            pl.BlockSpec((s, NA_WIDTH), lambda b, t: (b, cv)),
            pl.BlockSpec((1, NA_WIDTH), lambda b, t: (0, 0)),
            pl.BlockSpec((1, NA_WIDTH), lambda b, t: (0, 0)),
            pl.BlockSpec((NA_WIDTH, NA_WIDTH), lambda b, t: (0, 0)),
            pl.BlockSpec((NA_WIN_ROWS, NA_HEADS // 2, 2 * GRID_W, NA_BAND), lambda b, t: (0, 0, 0, 0)),
        ],
        out_specs=pl.BlockSpec((tq, NA_WIDTH), lambda b, t: (b * steps + t, 0)),
        scratch_shapes=[pltpu.VMEM((s, NA_WIDTH), BF16)],
        compiler_params=_params("parallel", "arbitrary"),
        name="neighbourhood_attention",
    )(pb, pb, pb, qg, kg, seg, bias)


def _mem_attn_kernel(q_ref, kv_ref, qg_ref, kg_ref, o_ref, kn_ref):
    @pl.when(pl.program_id(1) == 0)
    def _():
        for h in range(MEM_HEADS):
            cols = slice(h * MEM_HEAD_DIM, (h + 1) * MEM_HEAD_DIM)
            kn_ref[:, cols] = _rms_norm_rows(kv_ref[:, cols].astype(F32), kg_ref[...]).astype(BF16)

    head_cols = [slice(h * MEM_HEAD_DIM, (h + 1) * MEM_HEAD_DIM) for h in range(MEM_HEADS)]
    qns = [_rms_norm_rows(q_ref[:, cols].astype(F32), qg_ref[...]).astype(BF16) for cols in head_cols]
    scores = [lax.dot_general(qn, kn_ref[:, cols], (((1,), (1,)), ((), ())), preferred_element_type=F32)
              for qn, cols in zip(qns, head_cols)]
    exps, sums = [], []
    for s in scores:
        s = s * (MEM_HEAD_DIM ** -0.5)
        e = jnp.exp(s - jnp.max(s, axis=-1, keepdims=True))
        sums.append(jnp.sum(e, axis=-1, keepdims=True))
        exps.append(e.astype(BF16))
    outs = [jnp.dot(e, kv_ref[:, MEM_WIDTH + cols.start:MEM_WIDTH + cols.stop], preferred_element_type=F32)
            for e, cols in zip(exps, head_cols)]
    for o, l, cols in zip(outs, sums, head_cols):
        o_ref[:, cols] = (o / l).astype(o_ref.dtype)


def memory_cross_attention(pb, kv, batch, q_gain, k_gain, *, tq=512):
    m = pb.shape[0]
    steps = m // batch // tq
    n_mem = kv.shape[0] // batch
    cq = PB_COL["mem_q"] // MEM_WIDTH
    return pl.pallas_call(
        _mem_attn_kernel,
        out_shape=jax.ShapeDtypeStruct((m, MEM_WIDTH), BF16),
        grid=(batch, steps),
        in_specs=[
            pl.BlockSpec((tq, MEM_WIDTH), lambda b, t: (b * steps + t, cq)),
            pl.BlockSpec((n_mem, 2 * MEM_WIDTH), lambda b, t: (b, 0)),
            pl.BlockSpec((1, MEM_HEAD_DIM), lambda b, t: (0, 0)),
            pl.BlockSpec((1, MEM_HEAD_DIM), lambda b, t: (0, 0)),
        ],
        out_specs=pl.BlockSpec((tq, MEM_WIDTH), lambda b, t: (b * steps + t, 0)),
        scratch_shapes=[pltpu.VMEM((n_mem, MEM_WIDTH), BF16)],
        compiler_params=_params("parallel", "arbitrary"),
        name="memory_cross_attention",
    )(pb, kv, q_gain.astype(F32).reshape(1, MEM_HEAD_DIM), k_gain.astype(F32).reshape(1, MEM_HEAD_DIM))


LIN_BLOCK = 512
HEAD_V = 128


def _log1p_exp_neg(t):
    return jnp.log(1.0 + jnp.exp(-t))


def _log_sigmoid(x):
    return jnp.minimum(x, 0.0) - _log1p_exp_neg(jnp.abs(x))


def _logaddexp(a, b):
    return jnp.maximum(a, b) + _log1p_exp_neg(jnp.abs(a - b))


def _split_bf16(x, terms):
    parts = []
    for _ in range(terms):
        p = x.astype(BF16)
        parts.append(p)
        x = x - p.astype(F32)
    return parts


def _dot_f32(a, b):
    a_hi, a_lo = _split_bf16(a, 2)
    b_hi, b_lo = _split_bf16(b, 2)
    return (jnp.dot(a_hi, b_hi, preferred_element_type=F32)
            + (jnp.dot(a_hi, b_lo, preferred_element_type=F32) + jnp.dot(a_lo, b_hi, preferred_element_type=F32)))


def _cumsum_rows(mask, x):
    m = jnp.where(mask, 1.0, 0.0).astype(BF16)
    hi, mid, lo = _split_bf16(x, 3)
    return (jnp.dot(m, hi, preferred_element_type=F32)
            + (jnp.dot(m, mid, preferred_element_type=F32) + jnp.dot(m, lo, preferred_element_type=F32)))


def _gla_inputs(refs, rows, direction, params):
    q_ref, k_ref, v_ref, g_ref = refs
    wpad_ref, bias_ref = params
    qc = q_ref[rows, :].astype(F32) * (GLA_HEAD_K ** -0.5)
    kc = k_ref[rows, :].astype(F32)
    gk = _dot_f32(g_ref[rows, :], wpad_ref[direction]) + bias_ref[direction]
    lg = _log_sigmoid(gk) * (1.0 / GLA_GATE_NORMALIZER)
    return qc, kc, v_ref[rows, :], lg


def _hgrn_inputs(refs, rows, direction, params):
    q_ref, v_ref, z_ref = refs
    lb_ref, log_lb_ref, log1m_lb_ref = params
    qr = q_ref[rows, :].astype(F32)
    qc = qr * _sigmoid(qr)
    z = z_ref[rows, :]
    lg = _logaddexp(log_lb_ref[direction], log1m_lb_ref[direction] + _log_sigmoid(z))
    kc = (1.0 - lb_ref[direction]) * _sigmoid(-z)
    return qc, kc, v_ref[rows, :], lg


GDN_CONV_WIDTH = 5
GDN_QKV_WIDTH = 2 * GDN_KEY_WIDTH + GDN_VAL_WIDTH
GDN_HALO = 16


def _gdn_prep_kernel(prev_ref, cur_ref, next_ref, w_ref, o_ref, xp_ref, *, blocks_per_seq):
    i = pl.program_id(0)
    t = cur_ref.shape[0]
    pos = i % blocks_per_seq
    prev = prev_ref[...].astype(F32)
    nxt = next_ref[...].astype(F32)
    xp_ref[0:GDN_HALO, :] = jnp.where(pos == 0, jnp.zeros_like(prev), prev)
    xp_ref[GDN_HALO:GDN_HALO + t, :] = cur_ref[...].astype(F32)
    xp_ref[GDN_HALO + t:, :] = jnp.where(pos == blocks_per_seq - 1, jnp.zeros_like(nxt), nxt)
    half = GDN_CONV_WIDTH // 2
    for g in range(GDN_QKV_WIDTH // LANES):
        cols = slice(g * LANES, (g + 1) * LANES)
        acc = None
        for j in range(GDN_CONV_WIDTH):
            term = xp_ref[GDN_HALO - half + j:GDN_HALO - half + j + t, cols] * w_ref[j:j + 1, cols]
            acc = term if acc is None else acc + term
        y = acc * _sigmoid(acc)
        if g < 2 * GDN_HEADS:
            y = y * lax.rsqrt(jnp.sum(y * y, axis=-1, keepdims=True) + 1e-6)
            if g < GDN_HEADS:
                y = y * (GDN_HEAD_K ** -0.5)
        o_ref[:, cols] = y.astype(o_ref.dtype)


def gdn_prep(pb, batch, conv_w, *, t=512):
    m = pb.shape[0]
    blocks_per_seq = m // batch // t
    halo_per_block = t // GDN_HALO
    col = PB_COL["gdn_qkv"] // GDN_QKV_WIDTH
    last_halo = m // GDN_HALO - 1
    return pl.pallas_call(
        functools.partial(_gdn_prep_kernel, blocks_per_seq=blocks_per_seq),
        out_shape=jax.ShapeDtypeStruct((m, GDN_QKV_WIDTH), BF16),
        grid=(m // t,),
        in_specs=[
            pl.BlockSpec((GDN_HALO, GDN_QKV_WIDTH), lambda i: (jnp.maximum(i * halo_per_block - 1, 0), col)),
            pl.BlockSpec((t, GDN_QKV_WIDTH), lambda i: (i, col)),
            pl.BlockSpec((GDN_HALO, GDN_QKV_WIDTH),
                         lambda i: (jnp.minimum((i + 1) * halo_per_block, last_halo), col)),
            pl.BlockSpec((GDN_CONV_WIDTH, GDN_QKV_WIDTH), lambda i: (0, 0)),
        ],
        out_specs=pl.BlockSpec((t, GDN_QKV_WIDTH), lambda i: (i, 0)),
        scratch_shapes=[pltpu.VMEM((t + 2 * GDN_HALO, GDN_QKV_WIDTH), F32)],
        compiler_params=_params("parallel"),
        name="gdn_prep",
    )(pb, pb, pb, conv_w.astype(F32))


def _softplus(x):
    return jnp.maximum(x, 0.0) + _log1p_exp_neg(jnp.abs(x))


GDN_PACK = GDN_HEADS * GDN_CHUNK
GDN_WY_BLOCK = 512


def _stack_heads(x, width):
    heads = x.shape[1] // width
    lane = lax.broadcasted_iota(jnp.int32, (1, x.shape[1]), 1)
    return jnp.concatenate(
        [jnp.where((lane >= h * width) & (lane < (h + 1) * width), x, 0.0).astype(BF16) for h in range(heads)],
        axis=0)


def _packed_mm(x, y):
    return jnp.dot(x.astype(BF16), _stack_heads(y, GDN_CHUNK), preferred_element_type=F32)


def _packed_inverses(mats):
    c = GDN_CHUNK
    ii = lax.broadcasted_iota(jnp.int32, (c, GDN_PACK), 0)
    jj = lax.broadcasted_iota(jnp.int32, (c, GDN_PACK), 1) % c
    eye = (ii == jj).astype(F32)

    def same_block(s):
        return (ii // s) == (jj // s)

    ds = [jnp.where(same_block(8), a, 0.0) for a in mats]
    d2s = [_packed_mm(d, d) for d in ds]
    d4s = [_packed_mm(d2, d2) for d2 in d2s]
    ts = [_packed_mm(eye - d, eye + d2) for d, d2 in zip(ds, d2s)]
    ts = [_packed_mm(t, eye + d4) for t, d4 in zip(ts, d4s)]
    s = 8
    while s < c:
        off = same_block(2 * s) & jnp.logical_not(same_block(s))
        ets = [_packed_mm(jnp.where(off, a, 0.0), t) for a, t in zip(mats, ts)]
        ts = [t - _packed_mm(t, et) for t, et in zip(ts, ets)]
        s *= 2
    return ts


def _gdn_wy_kernel(qkv_ref, small_ref, a_ref, dtb_ref, selg_ref, selk_ref, selb_ref, *out_refs):
    c = GDN_CHUNK
    n_chunks = qkv_ref.shape[0] // c
    ii = lax.broadcasted_iota(jnp.int32, (c, c), 0)
    jj = lax.broadcasted_iota(jnp.int32, (c, c), 1)
    pi = lax.broadcasted_iota(jnp.int32, (c, GDN_PACK), 0)
    pj = lax.broadcasted_iota(jnp.int32, (c, GDN_PACK), 1) % c
    eye_p = (pi == pj).astype(F32)
    ones_cc = jnp.ones((c, c), BF16)

    problems = [(ch, d) for ch in range(n_chunks) for d in range(2)]
    chunk_in = []
    for ch in range(n_chunks):
        rows = slice(ch * c, (ch + 1) * c)
        qkv = qkv_ref[rows, :]
        small = small_ref[rows, :]
        kf = qkv[:, GDN_KEY_WIDTH:2 * GDN_KEY_WIDTH].astype(F32)
        chunk_in.append(dict(
            qf=qkv[:, :GDN_KEY_WIDTH].astype(F32), kf=kf, vf=qkv[:, 2 * GDN_KEY_WIDTH:].astype(F32),
            kbd=_stack_heads(kf, HEAD_V),
            log_alpha=a_ref[...] * _softplus(small + dtb_ref[...]),
            beta_all=_sigmoid(small)))

    def sel3(x, sel):
        hi, mid, lo = _split_bf16(x, 3)
        return (jnp.dot(hi, sel, preferred_element_type=F32)
                + (jnp.dot(mid, sel, preferred_element_type=F32) + jnp.dot(lo, sel, preferred_element_type=F32)))

    g_all = [_cumsum_rows((jj >= ii) if d else (jj <= ii), chunk_in[ch]["log_alpha"]) for ch, d in problems]
    g_pack = [sel3(g, selg_ref[d]) for g, (ch, d) in zip(g_all, problems)]
    g_wide = [sel3(g, selk_ref[d]) for g, (ch, d) in zip(g_all, problems)]
    beta_w = [sel3(chunk_in[ch]["beta_all"], selb_ref[d]) for ch, d in problems]
    g_rowp = []
    for gp in g_pack:
        hi, mid, lo = _split_bf16(gp * eye_p, 3)
        g_rowp.append(jnp.dot(ones_cc, hi, preferred_element_type=F32)
                      + (jnp.dot(ones_cc, mid, preferred_element_type=F32)
                         + jnp.dot(ones_cc, lo, preferred_element_type=F32)))
    decays, k_betas = [], []
    for gp, gr, bw, (ch, d) in zip(g_pack, g_rowp, beta_w, problems):
        incl = (pj >= pi) if d else (pj <= pi)
        decays.append(jnp.where(incl, jnp.exp(jnp.where(incl, gp - gr, 0.0)), 0.0))
        k_betas.append(chunk_in[ch]["kf"] * bw)
    kq = [lax.dot_general(jnp.concatenate([kb, chunk_in[ch]["qf"]], axis=0).astype(BF16), chunk_in[ch]["kbd"],
                          (((1,), (1,)), ((), ())), preferred_element_type=F32)
          for kb, (ch, d) in zip(k_betas, problems)]
    a_mats = []
    for x, dec, (ch, d) in zip(kq, decays, problems):
        strict = (pj > pi) if d else (pj < pi)
        a_mats.append(jnp.where(strict, x[:c] * dec, 0.0))
    t_invs = _packed_inverses(a_mats)

    for idx, (ch, d) in enumerate(problems):
        u_ref, w_ref, attn_ref, qd_ref, kd_ref, gt_ref = out_refs[6 * d:6 * d + 6]
        rows = slice(ch * c, (ch + 1) * c)
        cin = chunk_in[ch]
        gw = g_wide[idx]
        eg = jnp.exp(gw)
        t_b = t_invs[idx].astype(BF16)
        u_ref[rows, :] = jnp.dot(t_b, _stack_heads(cin["vf"] * beta_w[idx], HEAD_V), preferred_element_type=F32)
        w_ref[rows, :] = jnp.dot(t_b, _stack_heads(k_betas[idx] * eg, HEAD_V),
                                 preferred_element_type=F32).astype(w_ref.dtype)
        attn_ref[rows, :] = (kq[idx][c:] * decays[idx]).astype(attn_ref.dtype)
        end = 0 if d else c - 1
        g_end = gw[end:end + 1, :]
        qd_ref[rows, :] = (cin["qf"] * eg).astype(qd_ref.dtype)
        kd_ref[rows, :] = (cin["kf"] * jnp.exp(g_end - gw)).astype(kd_ref.dtype)
        gt_ref[ch:ch + 1, :] = jnp.exp(g_end)


def gdn_wy(qkv, pf, a_scale, dtb):
    m = qkv.shape[0]
    t = GDN_WY_BLOCK
    cpb = t // GDN_CHUNK
    selg = np.zeros((2, LANES, GDN_PACK), np.float32)
    selk = np.zeros((2, LANES, GDN_VAL_WIDTH), np.float32)
    selb = np.zeros((2, LANES, GDN_VAL_WIDTH), np.float32)
    for d in range(2):
        for h in range(GDN_HEADS):
            selg[d, GDN_A_LANE + d * GDN_HEADS + h, h * GDN_CHUNK:(h + 1) * GDN_CHUNK] = 1.0
            selk[d, GDN_A_LANE + d * GDN_HEADS + h, h * HEAD_V:(h + 1) * HEAD_V] = 1.0
            selb[d, GDN_B_LANE + d * GDN_HEADS + h, h * HEAD_V:(h + 1) * HEAD_V] = 1.0
    wide = GDN_VAL_WIDTH
    out_shape, out_specs = [], []
    for _ in range(2):
        for width, dt in ((wide, F32), (wide, BF16), (GDN_PACK, BF16), (wide, BF16), (wide, BF16)):
            out_shape.append(jax.ShapeDtypeStruct((m, width), dt))
            out_specs.append(pl.BlockSpec((t, width), lambda i: (i, 0)))
        out_shape.append(jax.ShapeDtypeStruct((m // GDN_CHUNK, wide), F32))
        out_specs.append(pl.BlockSpec((cpb, wide), lambda i: (i, 0)))
    return pl.pallas_call(
        _gdn_wy_kernel,
        out_shape=tuple(out_shape),
        grid=(m // t,),
        in_specs=[
            pl.BlockSpec((t, GDN_QKV_WIDTH), lambda i: (i, 0)),
            pl.BlockSpec((t, LANES), lambda i: (i, PF_SMALL_COL // LANES)),
            pl.BlockSpec((1, LANES), lambda i: (0, 0)),
            pl.BlockSpec((1, LANES), lambda i: (0, 0)),
            pl.BlockSpec((2, LANES, GDN_PACK), lambda i: (0, 0, 0)),
            pl.BlockSpec((2, LANES, wide), lambda i: (0, 0, 0)),
            pl.BlockSpec((2, LANES, wide), lambda i: (0, 0, 0)),
        ],
        out_specs=tuple(out_specs),
        compiler_params=_params("parallel"),
        name="gdn_wy",
    )(qkv, pf, a_scale, dtb, jnp.asarray(selg, BF16), jnp.asarray(selk, BF16), jnp.asarray(selb, BF16))


GDN_PAIR = 2 * HEAD_V


def _gdn_scan_kernel(*refs):
    groups = (refs[0:6], refs[6:12])
    out_refs = refs[12:14]
    state_refs = refs[14:16]

    @pl.when(pl.program_id(1) == 0)
    def _():
        for s_ref in state_refs:
            s_ref[...] = jnp.zeros_like(s_ref)

    n_chunks = out_refs[0].shape[0] // GDN_CHUNK
    pairs = GDN_HEADS // 2
    pair_cols = [slice(p * GDN_PAIR, (p + 1) * GDN_PAIR) for p in range(pairs)]
    ri = lax.broadcasted_iota(jnp.int32, (GDN_PAIR, GDN_PAIR), 0) // HEAD_V
    ci = lax.broadcasted_iota(jnp.int32, (GDN_PAIR, GDN_PAIR), 1) // HEAD_V
    diag = ri == ci

    def body(c, carry):
        chunks = (c, n_chunks - 1 - c)
        rows = [pl.ds(pl.multiple_of(ch * GDN_CHUNK, GDN_CHUNK), GDN_CHUNK) for ch in chunks]
        states = [[s_ref[p] for p in range(pairs)] for s_ref in state_refs]
        states_b = [[s.astype(BF16) for s in st] for st in states]
        wq = [[jnp.dot(jnp.concatenate([groups[g][1][rows[g], cols], groups[g][3][rows[g], cols]], axis=0),
                       states_b[g][p], preferred_element_type=F32)
               for p, cols in enumerate(pair_cols)] for g in range(2)]
        ws = [[x[:GDN_CHUNK] for x in wq[g]] for g in range(2)]
        qs = [[x[GDN_CHUNK:] for x in wq[g]] for g in range(2)]
        v_new = [groups[g][0][rows[g], :] - jnp.concatenate(ws[g], axis=1) for g in range(2)]
        av = [jnp.dot(groups[g][2][rows[g], :], _stack_heads(v_new[g], HEAD_V), preferred_element_type=F32)
              for g in range(2)]
        v_new_b = [v.astype(BF16) for v in v_new]
        upd = [[lax.dot_general(groups[g][4][rows[g], cols], v_new_b[g][:, cols], (((0,), (0,)), ((), ())),
                                preferred_element_type=F32) for cols in pair_cols] for g in range(2)]
        for g in range(2):
            out_refs[g][rows[g], :] = jnp.concatenate(qs[g], axis=1) + av[g]
            gt = groups[g][5][pl.ds(chunks[g], 1), :]
            for p, cols in enumerate(pair_cols):
                state_refs[g][p] = states[g][p] * gt[:, cols] + jnp.where(diag, upd[g][p], 0.0)
        return carry

    lax.fori_loop(0, n_chunks, body, 0)


def gated_deltanet_branch(pb, pf, batch, conv_w, a_log, dt_bias, norm_gain):
    m = pb.shape[0]
    nb = m // batch // LIN_BLOCK
    cpb = LIN_BLOCK // GDN_CHUNK
    qkv = gdn_prep(pb, batch, conv_w)
    n_gate = 2 * GDN_HEADS
    a_scale = jnp.zeros((1, LANES), F32).at[0, GDN_A_LANE:GDN_A_LANE + n_gate].set(
        -jnp.exp(a_log.astype(F32)).reshape(n_gate))
    dtb = jnp.zeros((1, LANES), F32).at[0, GDN_A_LANE:GDN_A_LANE + n_gate].set(dt_bias.astype(F32).reshape(n_gate))
    wy = gdn_wy(qkv, pf, a_scale, dtb)
    widths = (GDN_VAL_WIDTH, GDN_VAL_WIDTH, GDN_PACK, GDN_VAL_WIDTH, GDN_VAL_WIDTH)

    def fwd(rows, width):
        return pl.BlockSpec((rows, width), lambda b, t: (b * nb + t, 0))

    def bwd(rows, width):
        return pl.BlockSpec((rows, width), lambda b, t: (b * nb + nb - 1 - t, 0))

    in_specs = [fwd(LIN_BLOCK, w) for w in widths] + [fwd(cpb, GDN_VAL_WIDTH)]
    in_specs += [bwd(LIN_BLOCK, w) for w in widths] + [bwd(cpb, GDN_VAL_WIDTH)]
    state = pltpu.VMEM((GDN_HEADS // 2, GDN_PAIR, GDN_PAIR), F32)
    o_f, o_b = pl.pallas_call(
        _gdn_scan_kernel,
        out_shape=(jax.ShapeDtypeStruct((m, GDN_VAL_WIDTH), F32), jax.ShapeDtypeStruct((m, GDN_VAL_WIDTH), F32)),
        grid=(batch, nb),
        in_specs=in_specs,
        out_specs=(fwd(LIN_BLOCK, GDN_VAL_WIDTH), bwd(LIN_BLOCK, GDN_VAL_WIDTH)),
        scratch_shapes=[state, state],
        compiler_params=_params("parallel", "arbitrary"),
        name="gdn_scan",
    )(*wy)
    return RawBranch(o_f, o_b, "gdn_og", norm_gain, True)


def _dot3(m, x):
    hi, mid, lo = _split_bf16(x, 3)
    return (jnp.dot(m, hi, preferred_element_type=F32)
            + (jnp.dot(m, mid, preferred_element_type=F32) + jnp.dot(m, lo, preferred_element_type=F32)))


LIN_CUM_ROWS = 256
LIN_SCORE_ROWS = 128


def _chunk_causal(n, chunk, reverse):
    i = lax.broadcasted_iota(jnp.int32, (n, n), 0)
    j = lax.broadcasted_iota(jnp.int32, (n, n), 1)
    return ((i // chunk) == (j // chunk)) & ((j >= i) if reverse else (j <= i))


def _lin_intra_kernel(*refs, load_inputs, n_in, n_params, heads):
    dir_refs = (refs[:n_in], refs[n_in:2 * n_in])
    params = refs[2 * n_in:2 * n_in + n_params]
    out_refs = refs[2 * n_in + n_params:]
    c = LIN_CHUNK
    dirs = (0, 1)
    loaded = [load_inputs(dir_refs[d], slice(None), d, params) for d in dirs]
    t, w = loaded[0][0].shape
    dk = w // heads
    nc = t // c
    cums = [jnp.where(_chunk_causal(LIN_CUM_ROWS, c, d == 1), 1.0, 0.0).astype(BF16) for d in dirs]
    bs = [jnp.concatenate([_dot3(cums[d], loaded[d][3][r:r + LIN_CUM_ROWS, :])
                           for r in range(0, t, LIN_CUM_ROWS)], axis=0) for d in dirs]
    qes, kes = [], []
    for d in dirs:
        oi_ref, qd_ref, kd_ref, gt_ref = out_refs[4 * d:4 * d + 4]
        qc, kc, vc, lg = loaded[d]
        b = bs[d]
        b3 = b.reshape(nc, c, w)
        mid = c - 1 - c // 2 if d else c // 2
        end = 0 if d else c - 1
        b_mid = jnp.broadcast_to(b3[:, mid:mid + 1, :], (nc, c, w)).reshape(t, w)
        b_end = jnp.broadcast_to(b3[:, end:end + 1, :], (nc, c, w)).reshape(t, w)
        qes.append((qc * jnp.exp(b - b_mid)).astype(BF16))
        kes.append((kc * jnp.exp(b_mid - b)).astype(BF16))
        qd_ref[...] = (qc * jnp.exp(b)).astype(qd_ref.dtype)
        kd_ref[...] = (kc * jnp.exp(b_end - b)).astype(kd_ref.dtype)
        gt_ref[...] = jnp.exp(b3[:, end, :])
    keeps = [_chunk_causal(LIN_SCORE_ROWS, c, d == 1) for d in dirs]
    lane = lax.broadcasted_iota(jnp.int32, (1, LANES), 1)
    for h in range(heads):
        win = slice((h * dk) // LANES * LANES, (h * dk) // LANES * LANES + LANES)
        lo = h * dk - win.start
        vcols = slice(h * HEAD_V, (h + 1) * HEAD_V)
        tiles = [(slice(r, r + LIN_SCORE_ROWS), d) for r in range(0, t, LIN_SCORE_ROWS) for d in dirs]
        scores = []
        for rows, d in tiles:
            qh = qes[d][rows, win]
            if dk < LANES:
                qh = jnp.where((lane >= lo) & (lane < lo + dk), qh, jnp.zeros_like(qh))
            scores.append(lax.dot_general(qh, kes[d][rows, win], (((1,), (1,)), ((), ())),
                                          preferred_element_type=F32))
        probs = [jnp.where(keeps[d], s, 0.0).astype(BF16) for s, (rows, d) in zip(scores, tiles)]
        for p, (rows, d) in zip(probs, tiles):
            out_refs[4 * d][rows, vcols] = jnp.dot(p, loaded[d][2][rows, vcols], preferred_element_type=F32)


def _lin_scan_kernel(*refs, heads, chunk, unroll):
    groups = (refs[0:5], refs[5:10])
    out_refs = refs[10:12]
    state_refs = refs[12:14]

    @pl.when(pl.program_id(1) == 0)
    def _():
        for s_ref in state_refs:
            s_ref[...] = jnp.zeros_like(s_ref)

    n_chunks = out_refs[0].shape[0] // chunk
    w = state_refs[0].shape[1]
    dk = w // heads
    lane = lax.broadcasted_iota(jnp.int32, (1, w), 1)
    masks = [(lane >= h * dk) & (lane < (h + 1) * dk) for h in range(heads)]

    def stack(x):
        return jnp.concatenate([jnp.where(m, x, jnp.zeros_like(x)) for m in masks], axis=0)

    def body(it, carry):
        steps = []
        for u in range(unroll):
            c = it * unroll + u
            steps += [(0, c), (1, n_chunks - 1 - c)]
        prepared = []
        for g, ch in steps:
            rows = pl.ds(pl.multiple_of(ch * chunk, chunk), chunk)
            oi_ref, qd_ref, kd_ref, v_ref, gt_ref = groups[g]
            vc = v_ref[rows, :]
            v4 = jnp.concatenate([vc[:, h * HEAD_V:(h + 1) * HEAD_V] for h in range(heads)], axis=0)
            upd = lax.dot_general(v4, stack(kd_ref[rows, :]), (((0,), (0,)), ((), ())),
                                  preferred_element_type=F32)
            prepared.append((rows, stack(qd_ref[rows, :]), upd, gt_ref[pl.ds(ch, 1), :]))
        states = [s_ref[...] for s_ref in state_refs]
        for (g, ch), (rows, q4, upd, gt) in zip(steps, prepared):
            o_inter = lax.dot_general(q4, states[g].astype(BF16), (((1,), (1,)), ((), ())),
                                      preferred_element_type=F32)
            out_refs[g][rows, :] = groups[g][0][rows, :] + jnp.concatenate(
                [o_inter[h * chunk:(h + 1) * chunk, :] for h in range(heads)], axis=1)
            states[g] = states[g] * gt + upd
        for s_ref, st in zip(state_refs, states):
            s_ref[...] = st
        return carry

    lax.fori_loop(0, n_chunks // unroll, body, 0)


def _bidir_lin_call(name, load_inputs, arrays, col_blocks, widths, params, batch, heads, key_width, v_col):
    m = arrays[0].shape[0]
    t = LIN_BLOCK
    nb = m // batch // t
    cpb = t // LIN_CHUNK
    out_w = heads * HEAD_V
    n_in = len(arrays)

    in_specs, operands = [], []
    for d in range(2):
        for a, wd, cb in zip(arrays, widths, col_blocks):
            in_specs.append(pl.BlockSpec((t, wd), functools.partial(lambda i, c: (i, c), c=cb[d])))
            operands.append(a)
    for p in params:
        in_specs.append(pl.BlockSpec(p.shape, functools.partial(lambda i, nd: (0,) * nd, nd=p.ndim)))
    out_shape, out_specs = [], []
    for _ in range(2):
        for rows_total, rows_blk, width, dt in ((m, t, out_w, F32), (m, t, key_width, BF16),
                                                (m, t, key_width, BF16), (m // LIN_CHUNK, cpb, key_width, F32)):
            out_shape.append(jax.ShapeDtypeStruct((rows_total, width), dt))
            out_specs.append(pl.BlockSpec((rows_blk, width), lambda i: (i, 0)))

    intra = pl.pallas_call(
        functools.partial(_lin_intra_kernel, load_inputs=load_inputs, n_in=n_in, n_params=len(params),
                          heads=heads),
        out_shape=tuple(out_shape),
        grid=(m // t,),
        in_specs=in_specs,
        out_specs=tuple(out_specs),
        compiler_params=_params("parallel"),
        name=name + "_intra",
    )(*operands, *params)

    def fwd(rows, width, col=0):
        return pl.BlockSpec((rows, width), lambda b, s: (b * nb + s, col))

    def bwd(rows, width, col=0):
        return pl.BlockSpec((rows, width), lambda b, s: (b * nb + nb - 1 - s, col))

    scan_specs, scan_ops = [], []
    for d, mk in enumerate((fwd, bwd)):
        oi, qd, kd, gt = intra[4 * d:4 * d + 4]
        scan_specs += [mk(t, out_w), mk(t, key_width), mk(t, key_width), mk(t, out_w, v_col), mk(cpb, key_width)]
        scan_ops += [oi, qd, kd, arrays[0], gt]
    state = pltpu.VMEM((HEAD_V, key_width), F32)
    return pl.pallas_call(
        functools.partial(_lin_scan_kernel, heads=heads, chunk=LIN_CHUNK, unroll=4),
        out_shape=(jax.ShapeDtypeStruct((m, out_w), F32), jax.ShapeDtypeStruct((m, out_w), F32)),
        grid=(batch, nb),
        in_specs=scan_specs,
        out_specs=(fwd(t, out_w), bwd(t, out_w)),
        scratch_shapes=[state, state],
        compiler_params=_params("parallel", "arbitrary"),
        name=name + "_scan",
    )(*scan_ops)


def gla_branch(pb, pf, batch, w_gate_up, b_gate, norm_gain):
    wpad = jnp.zeros((2, LANES, GLA_KEY_WIDTH), F32)
    for d in range(2):
        wpad = wpad.at[d, d * GLA_GATE_RANK:(d + 1) * GLA_GATE_RANK, :].set(w_gate_up[d].astype(F32))
    bias = b_gate.astype(F32).reshape(2, 1, GLA_KEY_WIDTH)
    v_col = PB_COL["gla_v"] // GLA_VAL_WIDTH
    cols = [(PB_COL["gla_q"] // GLA_KEY_WIDTH,) * 2, (PB_COL["gla_k"] // GLA_KEY_WIDTH,) * 2,
            (v_col,) * 2, (PF_SMALL_COL // LANES,) * 2]
    o_f, o_b = _bidir_lin_call("gla", _gla_inputs, [pb, pb, pb, pf], cols,
                               [GLA_KEY_WIDTH, GLA_KEY_WIDTH, GLA_VAL_WIDTH, LANES], [wpad, bias],
                               batch, GLA_HEADS, GLA_KEY_WIDTH, v_col)
    return RawBranch(o_f, o_b, "gla_og", norm_gain, True)


def hgrn2_branch(pb, pf, batch, lower_bound, norm_gain):
    lb = lower_bound.astype(F32).reshape(2, 1, HGRN_KEY_WIDTH)
    log_lb = jnp.log(jnp.maximum(lb, LB_FLOOR))
    log1m_lb = jnp.log1p(-lb)
    zc = PF_COL["hg_f"] // HGRN_KEY_WIDTH
    v_col = PB_COL["hg_i"] // HGRN_VAL_WIDTH
    cols = [(PB_COL["hg_q"] // HGRN_KEY_WIDTH,) * 2, (v_col,) * 2, (zc, zc + 1)]
    o_f, o_b = _bidir_lin_call("hgrn2", _hgrn_inputs, [pb, pb, pf], cols,
                               [HGRN_KEY_WIDTH, HGRN_VAL_WIDTH, HGRN_KEY_WIDTH], [lb, log_lb, log1m_lb],
                               batch, HGRN_HEADS, HGRN_KEY_WIDTH, v_col)
    return RawBranch(o_f, o_b, "hg_og", norm_gain, False)


def kernel(x, mem, g_mix, w_in, na_q_gain, na_k_gain, na_rel_bias, gla_w_gate_up, gla_b_gate, gla_norm_gain, gdn_conv_w, gdn_a_log, gdn_dt_bias, gdn_norm_gain, hgrn_lb_raw, hgrn_norm_gain, g_mem, w_mem_kv, mem_q_gain, mem_k_gain, w_branch, w_out, g_ffn, ffn_w_gate, ffn_w_up, ffn_w_down, moe_w_router, moe_b_router, moe_w_gate, moe_w_up, moe_w_down):
    B, S, D = x.shape
    n_tok = B * S
    lb_w = jax.nn.softmax(hgrn_lb_raw.astype(F32), axis=0)
    hgrn_lb = jnp.cumsum(lb_w, axis=0) - lb_w[0:1]
    x2 = x.reshape(n_tok, D)
    mem2 = mem.reshape(B * mem.shape[1], D)
    for layer in range(DEPTH):
        pb, pf = in_projection(x2, g_mix[layer], _rearrange_w_in(w_in[layer]), PB_COL["gates"], PB_WIDTH)
        kv = rms_matmul(mem2, g_mem[layer], w_mem_kv[layer].astype(BF16), tm=mem2.shape[0], tn=512,
                        out_dtype=BF16)
        branches = [
            neighbourhood_attention(pb, B, na_q_gain[layer], na_k_gain[layer], na_rel_bias[layer]),
            gla_branch(pb, pf, B, gla_w_gate_up[layer], gla_b_gate[layer], gla_norm_gain[layer]),
            gated_deltanet_branch(pb, pf, B, gdn_conv_w[layer], gdn_a_log[layer], gdn_dt_bias[layer],
                                  gdn_norm_gain[layer]),
            hgrn2_branch(pb, pf, B, hgrn_lb[layer], hgrn_norm_gain[layer]),
            memory_cross_attention(pb, kv, B, mem_q_gain[layer], mem_k_gain[layer]),
        ]
        merged = merge_branches(branches, pb, w_branch[layer].astype(BF16), tm=512, tn=512)
        x2 = matmul_residual(merged, w_out[layer].astype(BF16), x2, tm=1024, tn=512)

        j = layer // 2
        if layer % 2 == 0:
            act = rms_swiglu_up(x2, g_ffn[layer], ffn_w_gate[j].astype(BF16), ffn_w_up[j].astype(BF16),
                                tm=1024, tn=512)
            x2 = matmul_residual(act, ffn_w_down[j].astype(BF16), x2, tm=512, tn=512)
        else:
            x2 = moe_layer(x2, g_ffn[layer], moe_w_router[j], moe_b_router[j], moe_w_gate[j], moe_w_up[j],
                           moe_w_down[j])
    return x2.reshape(B, S, D)
```

```python
import functools

import jax
import jax.numpy as jnp
import numpy as np
from jax import lax
from jax.experimental import pallas as pl
from jax.experimental.pallas import tpu as pltpu

F32 = jnp.float32
BF16 = jnp.bfloat16

D_MODEL = 2048
DEPTH = 2
RMS_EPS = 1e-6
MASK_VALUE = -1e30
LB_FLOOR = 1e-30
GRID_W = 64

NA_HEADS = 8
NA_HEAD_DIM = 64
NA_WIDTH = 512
NA_WIN_ROWS = 8
NA_WIN_COLS = 16

GLA_HEADS = 4
GLA_HEAD_K = 64
GLA_HEAD_V = 128
GLA_KEY_WIDTH = 256
GLA_VAL_WIDTH = 512
GLA_GATE_RANK = 16
GLA_GATE_NORMALIZER = 16.0

GDN_HEADS = 4
GDN_HEAD_K = 128
GDN_HEAD_V = 128
GDN_KEY_WIDTH = 512
GDN_VAL_WIDTH = 512
GDN_CHUNK = 64

HGRN_HEADS = 4
HGRN_HEAD_K = 128
HGRN_HEAD_V = 128
HGRN_KEY_WIDTH = 512
HGRN_VAL_WIDTH = 512

LIN_CHUNK = 32

MEM_HEADS = 4
MEM_HEAD_DIM = 128
MEM_WIDTH = 512

N_BRANCH = 5
BRANCH_WIDTH = 512
N_EXPERTS = 8
MOE_TOP_K = 2

IN_WIDTHS = (
    NA_WIDTH, NA_WIDTH, NA_WIDTH,
    GLA_KEY_WIDTH, GLA_KEY_WIDTH, GLA_VAL_WIDTH,
    2 * GLA_GATE_RANK, GLA_VAL_WIDTH,
    2 * GDN_KEY_WIDTH + GDN_VAL_WIDTH,
    2 * GDN_HEADS, 2 * GDN_HEADS, GDN_VAL_WIDTH,
    HGRN_KEY_WIDTH, 2 * HGRN_KEY_WIDTH, HGRN_VAL_WIDTH, HGRN_VAL_WIDTH,
    MEM_WIDTH,
    N_BRANCH * D_MODEL,
)
P_IN = sum(IN_WIDTHS)

V7X_VMEM_BYTES = 64 * 1024 * 1024
VMEM_LIMIT_BYTES = V7X_VMEM_BYTES - 8 * 1024 * 1024
LANES = 128


def _params(*semantics):
    return pltpu.CompilerParams(dimension_semantics=semantics, vmem_limit_bytes=VMEM_LIMIT_BYTES)


def _sigmoid(x):
    return 0.5 * jnp.tanh(0.5 * x) + 0.5


def _rms_norm_rows(x, gain):
    ms = jnp.mean(x * x, axis=-1, keepdims=True)
    return x * lax.rsqrt(ms + RMS_EPS) * gain


def _rms_matmul_kernel(x_ref, g_ref, w_ref, o_ref, h_ref):
    @pl.when(pl.program_id(1) == 0)
    def _():
        h_ref[...] = _rms_norm_rows(x_ref[...], g_ref[...]).astype(BF16)

    o_ref[...] = jnp.dot(h_ref[...], w_ref[...], preferred_element_type=F32).astype(o_ref.dtype)


def rms_matmul(x, gain, w, *, tm, tn, out_dtype=F32):
    m, k = x.shape
    n = w.shape[1]
    return pl.pallas_call(
        _rms_matmul_kernel,
        out_shape=jax.ShapeDtypeStruct((m, n), out_dtype),
        grid=(m // tm, n // tn),
        in_specs=[
            pl.BlockSpec((tm, k), lambda i, j: (i, 0)),
            pl.BlockSpec((1, k), lambda i, j: (0, 0)),
            pl.BlockSpec((k, tn), lambda i, j: (0, j)),
        ],
        out_specs=pl.BlockSpec((tm, tn), lambda i, j: (i, j)),
        scratch_shapes=[pltpu.VMEM((tm, k), BF16)],
        compiler_params=_params("parallel", "arbitrary"),
        name="rms_matmul",
    )(x, gain.reshape(1, k), w)


IN_PROJ_TILE = 512


def _in_projection_kernel(x_ref, g_ref, w_ref, ob_ref, of_ref, h_ref, *, n_plain_tiles, n_bf16_tiles):
    j = pl.program_id(1)

    @pl.when(j == 0)
    def _():
        h_ref[...] = _rms_norm_rows(x_ref[...], g_ref[...]).astype(BF16)

    r = jnp.dot(h_ref[...], w_ref[...], preferred_element_type=F32)

    @pl.when(j < n_plain_tiles)
    def _():
        ob_ref[...] = r.astype(ob_ref.dtype)

    @pl.when(jnp.logical_and(j >= n_plain_tiles, j < n_bf16_tiles))
    def _():
        ob_ref[...] = _sigmoid(r).astype(ob_ref.dtype)

    @pl.when(j >= n_bf16_tiles)
    def _():
        of_ref[...] = r


def in_projection(x, gain, w, n_plain, n_bf16, *, tm=1024):
    m, k = x.shape
    tn = IN_PROJ_TILE
    nb = n_bf16 // tn
    nf = (w.shape[1] - n_bf16) // tn
    return pl.pallas_call(
        functools.partial(_in_projection_kernel, n_plain_tiles=n_plain // tn, n_bf16_tiles=nb),
        out_shape=(jax.ShapeDtypeStruct((m, nb * tn), BF16), jax.ShapeDtypeStruct((m, nf * tn), F32)),
        grid=(m // tm, nb + nf),
        in_specs=[
            pl.BlockSpec((tm, k), lambda i, j: (i, 0)),
            pl.BlockSpec((1, k), lambda i, j: (0, 0)),
            pl.BlockSpec((k, tn), lambda i, j: (0, j)),
        ],
        out_specs=(pl.BlockSpec((tm, tn), lambda i, j: (i, jnp.minimum(j, nb - 1))),
                   pl.BlockSpec((tm, tn), lambda i, j: (i, jnp.maximum(j - nb, 0)))),
        scratch_shapes=[pltpu.VMEM((tm, k), BF16)],
        compiler_params=_params("parallel", "arbitrary"),
        name="in_projection",
    )(x, gain.reshape(1, k), w)


def _rms_swiglu_kernel(x_ref, g_ref, wg_ref, wu_ref, o_ref, h_ref):
    @pl.when(pl.program_id(1) == 0)
    def _():
        h_ref[...] = _rms_norm_rows(x_ref[...], g_ref[...]).astype(BF16)

    h = h_ref[...]
    a = jnp.dot(h, wg_ref[...], preferred_element_type=F32)
    b = jnp.dot(h, wu_ref[...], preferred_element_type=F32)
    o_ref[...] = (a * _sigmoid(a) * b).astype(o_ref.dtype)


def rms_swiglu_up(x, gain, wg, wu, *, tm, tn):
    m, k = x.shape
    n = wg.shape[1]
    return pl.pallas_call(
        _rms_swiglu_kernel,
        out_shape=jax.ShapeDtypeStruct((m, n), BF16),
        grid=(m // tm, n // tn),
        in_specs=[
            pl.BlockSpec((tm, k), lambda i, j: (i, 0)),
            pl.BlockSpec((1, k), lambda i, j: (0, 0)),
            pl.BlockSpec((k, tn), lambda i, j: (0, j)),
            pl.BlockSpec((k, tn), lambda i, j: (0, j)),
        ],
        out_specs=pl.BlockSpec((tm, tn), lambda i, j: (i, j)),
        scratch_shapes=[pltpu.VMEM((tm, k), BF16)],
        compiler_params=_params("parallel", "arbitrary"),
        name="rms_swiglu_up",
    )(x, gain.reshape(1, k), wg, wu)


def _matmul_residual_kernel(a_ref, w_ref, r_ref, o_ref):
    o_ref[...] = r_ref[...] + jnp.dot(a_ref[...], w_ref[...], preferred_element_type=F32)


def matmul_residual(a, w, res, *, tm, tn):
    m, k = a.shape
    n = w.shape[1]
    return pl.pallas_call(
        _matmul_residual_kernel,
        out_shape=jax.ShapeDtypeStruct((m, n), F32),
        grid=(m // tm, n // tn),
        in_specs=[
            pl.BlockSpec((tm, k), lambda i, j: (i, 0)),
            pl.BlockSpec((k, tn), lambda i, j: (0, j)),
            pl.BlockSpec((tm, tn), lambda i, j: (i, j)),
        ],
        out_specs=pl.BlockSpec((tm, tn), lambda i, j: (i, j)),
        compiler_params=_params("parallel", "arbitrary"),
        name="matmul_residual",
    )(a, w, res)


class RawBranch:
    def __init__(self, o_fwd, o_bwd, og_name, gain, silu_gate):
        self.o_fwd, self.o_bwd, self.og_name, self.gain, self.silu_gate = o_fwd, o_bwd, og_name, gain, silu_gate


def _merge_kernel(*refs, raw):
    pos = 0
    br = []
    for kind in raw:
        width = 1 if kind is None else 4
        br.append(refs[pos:pos + width])
        pos += width
    gl_refs = refs[pos:pos + N_BRANCH]
    wb_ref, o_ref, fin_ref = refs[pos + N_BRANCH:pos + N_BRANCH + 3]
    raw_slot = {n: s for s, n in enumerate(n for n, kind in enumerate(raw) if kind is not None)}

    @pl.when(pl.program_id(1) == 0)
    def _():
        for n, slot in raw_slot.items():
            of_ref, ob_ref, og_ref, gain_ref = br[n]
            for h in range(BRANCH_WIDTH // LANES):
                cols = slice(h * LANES, (h + 1) * LANES)
                y = _rms_norm_rows(of_ref[:, cols] + ob_ref[:, cols], gain_ref[...])
                g = og_ref[:, cols].astype(F32)
                gate = _sigmoid(g)
                if raw[n]:
                    gate = g * gate
                fin_ref[slot, :, cols] = (y * gate).astype(fin_ref.dtype)

    acc = None
    for n in range(N_BRANCH):
        b = br[n][0][...] if raw[n] is None else fin_ref[raw_slot[n]]
        y = jnp.dot(b, wb_ref[n], preferred_element_type=F32)
        t = gl_refs[n][...].astype(F32) * y
        acc = t if acc is None else acc + t
    o_ref[...] = acc.astype(o_ref.dtype)


def merge_branches(branches, pb, w_branch, *, tm, tn):
    m = pb.shape[0]
    d = D_MODEL
    tiles_per_branch = d // tn
    tile0 = PB_COL["gates"] // tn
    row_block = pl.BlockSpec((tm, BRANCH_WIDTH), lambda i, j: (i, 0))
    in_specs, operands, raw = [], [], []
    for b in branches:
        if isinstance(b, RawBranch):
            og_col = PB_COL[b.og_name] // BRANCH_WIDTH
            in_specs += [row_block, row_block,
                         pl.BlockSpec((tm, BRANCH_WIDTH), functools.partial(lambda i, j, c: (i, c), c=og_col)),
                         pl.BlockSpec((1, LANES), lambda i, j: (0, 0))]
            operands += [b.o_fwd, b.o_bwd, pb, b.gain.astype(F32).reshape(1, LANES)]
            raw.append(b.silu_gate)
        else:
            in_specs.append(row_block)
            operands.append(b)
            raw.append(None)
    in_specs += [
        pl.BlockSpec((tm, tn), functools.partial(lambda i, j, n: (i, tile0 + n * tiles_per_branch + j), n=n))
        for n in range(N_BRANCH)
    ]
    in_specs += [pl.BlockSpec((N_BRANCH, BRANCH_WIDTH, tn), lambda i, j: (0, 0, j))]
    n_raw = sum(kind is not None for kind in raw)
    return pl.pallas_call(
        functools.partial(_merge_kernel, raw=tuple(raw)),
        out_shape=jax.ShapeDtypeStruct((m, d), BF16),
        grid=(m // tm, d // tn),
        in_specs=in_specs,
        out_specs=pl.BlockSpec((tm, tn), lambda i, j: (i, j)),
        scratch_shapes=[pltpu.VMEM((max(n_raw, 1), tm, BRANCH_WIDTH), BF16)],
        compiler_params=_params("parallel", "arbitrary"),
        name="merge_branches",
    )(*operands, *([pb] * N_BRANCH), w_branch)


def _router_kernel(x_ref, g_ref, w_ref, b_ref, o_ref, h_ref, cnt_ref, run_ref, *, n_experts):
    @pl.when(pl.program_id(0) == 0)
    def _():
        run_ref[...] = jnp.zeros_like(run_ref)

    h = _rms_norm_rows(x_ref[...], g_ref[...])
    h_ref[...] = h.astype(h_ref.dtype)
    logits = _dot_f32(h, w_ref[...]) + b_ref[...]
    lane = lax.broadcasted_iota(jnp.int32, logits.shape, 1).astype(F32)
    neg = -jnp.inf
    lm = jnp.where(lane < n_experts, logits, neg)
    m1 = jnp.max(lm, axis=-1, keepdims=True)
    i1 = jnp.min(jnp.where(lm == m1, lane, float(LANES)), axis=-1, keepdims=True)
    lm2 = jnp.where(lane == i1, neg, lm)
    m2 = jnp.max(lm2, axis=-1, keepdims=True)
    i2 = jnp.min(jnp.where(lm2 == m2, lane, float(LANES)), axis=-1, keepdims=True)
    t = jnp.exp(m2 - m1)
    den = 1.0 + t

    tm = logits.shape[0]
    before = (lax.broadcasted_iota(jnp.int32, (tm, tm), 1)
              < lax.broadcasted_iota(jnp.int32, (tm, tm), 0))
    before = jnp.where(before, 1.0, 0.0).astype(BF16)
    pick1 = lane == i1
    pick2 = lane == i2
    oh1 = jnp.where(pick1, 1.0, 0.0)
    oh2 = jnp.where(pick2, 1.0, 0.0)
    pre1 = jnp.dot(before, oh1.astype(BF16), preferred_element_type=F32)
    pre2 = jnp.dot(before, oh2.astype(BF16), preferred_element_type=F32)
    tot1 = jnp.sum(oh1, axis=0, keepdims=True)
    tot2 = jnp.sum(oh2, axis=0, keepdims=True)
    run = run_ref[...]
    rank1 = jnp.sum(jnp.where(pick1, pre1 + run, 0.0), axis=-1, keepdims=True)
    rank2 = jnp.sum(jnp.where(pick2, pre2 + (run + tot1), 0.0), axis=-1, keepdims=True)
    run = run + tot1 + tot2
    run_ref[...] = run
    cnt_ref[...] = jnp.broadcast_to(run, cnt_ref.shape)

    out = jnp.where(lane == 0, 1.0 / den, jnp.where(lane == 1, t / den, jnp.where(lane == 2, i1, i2)))
    out = jnp.where(lane == 4, rank1, jnp.where(lane == 5, rank2, out))
    o_ref[...] = jnp.where(lane < 6, out, 0.0)


def router_top2(x, gain, w_router, b_router, *, tm=512):
    m, k = x.shape
    e = w_router.shape[1]
    w_pad = jnp.zeros((k, LANES), F32).at[:, :e].set(w_router.astype(F32))
    b_pad = jnp.zeros((1, LANES), F32).at[0, :e].set(b_router.astype(F32))
    route, h, cnt = pl.pallas_call(
        functools.partial(_router_kernel, n_experts=e),
        out_shape=(jax.ShapeDtypeStruct((m, LANES), F32), jax.ShapeDtypeStruct((m, k), BF16),
                   jax.ShapeDtypeStruct((8, LANES), F32)),
        grid=(m // tm,),
        in_specs=[
            pl.BlockSpec((tm, k), lambda i: (i, 0)),
            pl.BlockSpec((1, k), lambda i: (0, 0)),
            pl.BlockSpec((k, LANES), lambda i: (0, 0)),
            pl.BlockSpec((1, LANES), lambda i: (0, 0)),
        ],
        out_specs=(pl.BlockSpec((tm, LANES), lambda i: (i, 0)), pl.BlockSpec((tm, k), lambda i: (i, 0)),
                   pl.BlockSpec((8, LANES), lambda i: (0, 0))),
        scratch_shapes=[pltpu.VMEM((1, LANES), F32)],
        compiler_params=_params("arbitrary"),
        name="router_top2",
    )(x, gain.reshape(1, k), w_pad, b_pad)
    return route, h, cnt[0, :e].astype(jnp.int32)


MOE_TILE = 1024
MOE_SUB = 256
MOE_FF_TILE = 512


def _moe_kernel(tile_e_ref, tile_rows_ref, n_used_ref, x_ref, wg_ref, wu_ref, wd_ref, o_ref, acc_ref):
    i = pl.program_id(0)
    j = pl.program_id(1)
    last = pl.num_programs(1) - 1
    valid = tile_rows_ref[i]
    n_sub = (valid + (MOE_SUB - 1)) // MOE_SUB

    for k in range(1, MOE_TILE // MOE_SUB + 1):
        rows = slice(0, k * MOE_SUB)

        @pl.when(n_sub == k)
        def _(rows=rows):
            x = x_ref[rows, :]
            a = jnp.dot(x, wg_ref[0].astype(BF16), preferred_element_type=F32)
            b = jnp.dot(x, wu_ref[0].astype(BF16), preferred_element_type=F32)
            act = (a * _sigmoid(a) * b).astype(BF16)
            part = jnp.dot(act, wd_ref[0].astype(BF16), preferred_element_type=F32)

            @pl.when(j == 0)
            def _():
                acc_ref[rows, :] = part

            @pl.when(j > 0)
            def _():
                acc_ref[rows, :] += part

    for s in range(0, MOE_TILE, MOE_SUB):
        rows = slice(s, s + MOE_SUB)
        filled = s < valid

        @pl.when(jnp.logical_and(filled, j == last))
        def _(rows=rows):
            o_ref[rows, :] = acc_ref[rows, :].astype(o_ref.dtype)

        @pl.when(jnp.logical_and(jnp.logical_not(filled), j == last))
        def _(rows=rows):
            o_ref[rows, :] = jnp.zeros((MOE_SUB, o_ref.shape[1]), o_ref.dtype)


def moe_experts(xb, tile_e, tile_rows, n_used, wg, wu, wd):
    rows, d = xb.shape
    ff = wg.shape[2]
    tm, tf = MOE_TILE, MOE_FF_TILE
    n_tiles = rows // tm
    last_j = ff // tf - 1

    def x_map(i, j, te, tr, nu):
        return (jnp.minimum(i, nu[0] - 1), 0)

    def up_map(i, j, te, tr, nu):
        return (te[i], 0, jnp.where(i < nu[0], j, last_j))

    def down_map(i, j, te, tr, nu):
        return (te[i], jnp.where(i < nu[0], j, last_j), 0)

    grid_spec = pltpu.PrefetchScalarGridSpec(
        num_scalar_prefetch=3,
        grid=(n_tiles, ff // tf),
        in_specs=[
            pl.BlockSpec((tm, d), x_map, pipeline_mode=pl.Buffered(1)),
            pl.BlockSpec((1, d, tf), up_map),
            pl.BlockSpec((1, d, tf), up_map),
            pl.BlockSpec((1, tf, d), down_map),
        ],
        out_specs=pl.BlockSpec((tm, d), lambda i, j, te, tr, nu: (i, 0)),
        scratch_shapes=[pltpu.VMEM((tm, d), F32)],
    )
    return pl.pallas_call(
        _moe_kernel,
        out_shape=jax.ShapeDtypeStruct((rows, d), BF16),
        grid_spec=grid_spec,
        compiler_params=_params("arbitrary", "arbitrary"),
        name="moe_experts",
    )(tile_e, tile_rows, n_used, xb, wg, wu, wd)


def _moe_combine_kernel(x_ref, y0_ref, y1_ref, r_ref, o_ref):
    w = r_ref[...]
    o_ref[...] = x_ref[...] + w[:, 0:1] * y0_ref[...].astype(F32) + w[:, 1:2] * y1_ref[...].astype(F32)


def moe_combine(x2d, y0, y1, route, *, tm=512):
    n, d = x2d.shape
    row_block = pl.BlockSpec((tm, d), lambda i: (i, 0))
    return pl.pallas_call(
        _moe_combine_kernel,
        out_shape=jax.ShapeDtypeStruct((n, d), F32),
        grid=(n // tm,),
        in_specs=[row_block, row_block, row_block, pl.BlockSpec((tm, LANES), lambda i: (i, 0))],
        out_specs=row_block,
        compiler_params=_params("parallel"),
        name="moe_combine",
    )(x2d, y0, y1, route)


def moe_layer(x2d, gain, w_router, b_router, wg, wu, wd):
    n, d = x2d.shape
    e = N_EXPERTS
    route, h, counts = router_top2(x2d, gain, w_router, b_router)
    nk = n * MOE_TOP_K
    n_tiles = -(-nk // MOE_TILE) + e
    flat_e = route[:, 2:2 + MOE_TOP_K].astype(jnp.int32).reshape(nk)
    rank = route[:, 4:4 + MOE_TOP_K].astype(jnp.int32).reshape(nk)
    flat_tok = jnp.repeat(jnp.arange(n, dtype=jnp.int32), MOE_TOP_K)
    padded = (counts + MOE_TILE - 1) // MOE_TILE * MOE_TILE
    pad_end = jnp.cumsum(padded)
    pad_start = pad_end - padded
    slot = (pad_start[flat_e] + rank).astype(jnp.int32)
    n_slots = n_tiles * MOE_TILE
    slot_tok = (jnp.arange(n_slots, dtype=jnp.int32) % n).at[slot].set(flat_tok)
    tile_start = jnp.arange(n_tiles, dtype=jnp.int32) * MOE_TILE
    tile_e = jnp.minimum(jnp.searchsorted(pad_end, tile_start, side="right"), e - 1).astype(jnp.int32)
    tile_rows = jnp.clip(pad_start[tile_e] + counts[tile_e] - tile_start, 0, MOE_TILE).astype(jnp.int32)
    tile_rows = jnp.where(tile_start < pad_end[-1], tile_rows, 0)
    n_used = (pad_end[-1] // MOE_TILE).astype(jnp.int32).reshape(1)
    tile_e = jnp.where(tile_start < pad_end[-1], tile_e, tile_e[jnp.maximum(n_used[0] - 1, 0)])

    xb = h[slot_tok]
    yb = moe_experts(xb, tile_e, tile_rows, n_used, wg, wu, wd)
    slot2 = slot.reshape(n, MOE_TOP_K)
    return moe_combine(x2d, yb[slot2[:, 0]], yb[slot2[:, 1]], route)


_SRC = dict(zip(
    ("na_q", "na_k", "na_v", "gla_q", "gla_k", "gla_v", "gla_lr", "gla_og", "gdn_qkv", "gdn_a", "gdn_b",
     "gdn_og", "hg_q", "hg_f", "hg_i", "hg_og", "mem_q", "gates"),
    zip(np.cumsum((0,) + IN_WIDTHS[:-1]).tolist(), IN_WIDTHS)))
_PB_ORDER = ("na_q", "na_k", "na_v", "gla_q", "gla_k", "gla_v", "gla_og", "gdn_qkv", "gdn_og", "hg_q", "hg_i",
             "hg_og", "mem_q", "gates")
_PF_ORDER = ("hg_f", "gla_lr", "gdn_a", "gdn_b")
PB_COL = {}
_c = 0
for _name in _PB_ORDER:
    PB_COL[_name] = _c
    _c += _SRC[_name][1]
PB_WIDTH = _c
PF_COL = {}
_c = 0
for _name in _PF_ORDER:
    PF_COL[_name] = _c
    _c += _SRC[_name][1]
PF_WIDTH = -(-_c // IN_PROJ_TILE) * IN_PROJ_TILE
PF_SMALL_COL = PF_COL["gla_lr"]
GDN_A_LANE = PF_COL["gdn_a"] - PF_SMALL_COL
GDN_B_LANE = PF_COL["gdn_b"] - PF_SMALL_COL


def _rearrange_w_in(w):
    w = w.astype(BF16)
    cols = [w[:, _SRC[n][0]:_SRC[n][0] + _SRC[n][1]] for n in _PB_ORDER + _PF_ORDER]
    cols.append(jnp.zeros((w.shape[0], PB_WIDTH + PF_WIDTH - P_IN), BF16))
    return jnp.concatenate(cols, axis=1)


def _segment_rms(x, gain, seg_ones, seg_width):
    sq = x * x
    hi = sq.astype(BF16)
    lo = (sq - hi.astype(F32)).astype(BF16)
    ss = (jnp.dot(hi, seg_ones, preferred_element_type=F32)
          + jnp.dot(lo, seg_ones, preferred_element_type=F32))
    return x * lax.rsqrt(ss * (1.0 / seg_width) + RMS_EPS) * gain


NA_ROWS_PER_STEP = 8
NA_BAND = NA_WIN_ROWS * GRID_W


def _na_bias_table(rel_bias):
    c = np.arange(GRID_W)
    dc = np.clip(c[None, :] - c[:, None], 1 - NA_WIN_COLS, NA_WIN_COLS - 1) + (NA_WIN_COLS - 1)
    col_start = np.clip(c - NA_WIN_COLS // 2, 0, GRID_W - NA_WIN_COLS)
    col_in = (c[None, :] >= col_start[:, None]) & (c[None, :] < col_start[:, None] + NA_WIN_COLS)
    onehot = (dc[None] == np.arange(2 * NA_WIN_COLS - 1)[:, None, None]).astype(np.float32)
    base = jnp.einsum("hrc,cqk->hrqk", rel_bias.astype(F32), onehot, precision=lax.Precision.HIGHEST)
    base = jnp.where(col_in[None, None], base, MASK_VALUE)
    tables = []
    for cfg in range(NA_WIN_ROWS):
        rows = base[:, NA_WIN_ROWS - 1 - cfg:2 * NA_WIN_ROWS - 1 - cfg]
        tables.append(rows.transpose(0, 2, 1, 3).reshape(NA_HEADS // 2, 2 * GRID_W, NA_BAND))
    return jnp.stack(tables)


def _na_kernel(q_ref, k_ref, v_ref, qg_ref, kg_ref, seg_ref, bias_ref, o_ref, kn_ref):
    step = pl.program_id(1)
    rows_total = k_ref.shape[0] // GRID_W
    seg = seg_ref[...]

    @pl.when(step == 0)
    def _():
        def norm_keys(t, carry):
            rows = pl.ds(pl.multiple_of(t * 256, 256), 256)
            kn_ref[rows, :] = _segment_rms(k_ref[rows, :].astype(F32), kg_ref[...], seg, NA_HEAD_DIM).astype(BF16)
            return carry
        lax.fori_loop(0, k_ref.shape[0] // 256, norm_keys, 0)

    lane = lax.broadcasted_iota(jnp.int32, (1, LANES), 1)
    low_half = lane < NA_HEAD_DIM

    def one_row(rr, carry):
        r = step * NA_ROWS_PER_STEP + rr
        row_start = jnp.clip(r - NA_WIN_ROWS // 2, 0, rows_total - NA_WIN_ROWS)
        cfg = r - row_start
        qrows = pl.ds(pl.multiple_of(rr * GRID_W, GRID_W), GRID_W)
        band = pl.ds(pl.multiple_of(row_start * GRID_W, GRID_W), NA_BAND)
        qn = (_segment_rms(q_ref[qrows, :].astype(F32), qg_ref[...], seg, NA_HEAD_DIM)
              * (NA_HEAD_DIM ** -0.5)).astype(BF16)
        pairs = range(NA_HEADS // 2)
        pair_cols = [slice(pair * LANES, (pair + 1) * LANES) for pair in pairs]
        scores = []
        for cols in pair_cols:
            qp = qn[:, cols]
            q2 = jnp.concatenate([jnp.where(low_half, qp, jnp.zeros_like(qp)),
                                  jnp.where(low_half, jnp.zeros_like(qp), qp)], axis=0)
            scores.append(lax.dot_general(q2, kn_ref[band, cols], (((1,), (1,)), ((), ())),
                                          preferred_element_type=F32))
        exps, sums = [], []
        for s, pair in zip(scores, pairs):
            s = s + bias_ref[cfg, pair]
            e = jnp.exp(s - jnp.max(s, axis=-1, keepdims=True))
            sums.append(jnp.sum(e, axis=-1, keepdims=True))
            exps.append(e.astype(BF16))
        outs = [jnp.dot(e, v_ref[band, cols], preferred_element_type=F32) / l
                for e, l, cols in zip(exps, sums, pair_cols)]
        for o2, cols in zip(outs, pair_cols):
            o_ref[qrows, cols] = jnp.where(low_half, o2[:GRID_W], o2[GRID_W:]).astype(o_ref.dtype)
        return carry

    lax.fori_loop(0, NA_ROWS_PER_STEP, one_row, 0)


def neighbourhood_attention(pb, batch, q_gain, k_gain, rel_bias):
    m = pb.shape[0]
    s = m // batch
    tq = NA_ROWS_PER_STEP * GRID_W
    steps = s // tq
    qg = jnp.tile(q_gain.astype(F32), NA_HEADS).reshape(1, NA_WIDTH)
    kg = jnp.tile(k_gain.astype(F32), NA_HEADS).reshape(1, NA_WIDTH)
    seg = jnp.asarray(np.kron(np.eye(NA_HEADS), np.ones((NA_HEAD_DIM, NA_HEAD_DIM))), BF16)
    bias = _na_bias_table(rel_bias)
    cq, ck, cv = (PB_COL[n] // NA_WIDTH for n in ("na_q", "na_k", "na_v"))
    return pl.pallas_call(
        _na_kernel,
        out_shape=jax.ShapeDtypeStruct((m, NA_WIDTH), BF16),
        grid=(batch, steps),
        in_specs=[
            pl.BlockSpec((tq, NA_WIDTH), lambda b, t: (b * steps + t, cq)),
            pl.BlockSpec((s, NA_WIDTH), lambda b, t: (b, ck)),
            pl.BlockSpec((s, NA_WIDTH), lambda b, t: (b, cv)),
            pl.BlockSpec((1, NA_WIDTH), lambda b, t: (0, 0)),
            pl.BlockSpec((1, NA_WIDTH), lambda b, t: (0, 0)),
            pl.BlockSpec((NA_WIDTH, NA_WIDTH), lambda b, t: (0, 0)),
            pl.BlockSpec((NA_WIN_ROWS, NA_HEADS // 2, 2 * GRID_W, NA_BAND), lambda b, t: (0, 0, 0, 0)),
        ],
        out_specs=pl.BlockSpec((tq, NA_WIDTH), lambda b, t: (b * steps + t, 0)),
        scratch_shapes=[pltpu.VMEM((s, NA_WIDTH), BF16)],
        compiler_params=_params("parallel", "arbitrary"),
        name="neighbourhood_attention",
    )(pb, pb, pb, qg, kg, seg, bias)


def _mem_attn_kernel(q_ref, kv_ref, qg_ref, kg_ref, o_ref, kn_ref):
    @pl.when(pl.program_id(1) == 0)
    def _():
        for h in range(MEM_HEADS):
            cols = slice(h * MEM_HEAD_DIM, (h + 1) * MEM_HEAD_DIM)
            kn_ref[:, cols] = _rms_norm_rows(kv_ref[:, cols].astype(F32), kg_ref[...]).astype(BF16)

    head_cols = [slice(h * MEM_HEAD_DIM, (h + 1) * MEM_HEAD_DIM) for h in range(MEM_HEADS)]
    qns = [_rms_norm_rows(q_ref[:, cols].astype(F32), qg_ref[...]).astype(BF16) for cols in head_cols]
    scores = [lax.dot_general(qn, kn_ref[:, cols], (((1,), (1,)), ((), ())), preferred_element_type=F32)
              for qn, cols in zip(qns, head_cols)]
    exps, sums = [], []
    for s in scores:
        s = s * (MEM_HEAD_DIM ** -0.5)
        e = jnp.exp(s - jnp.max(s, axis=-1, keepdims=True))
        sums.append(jnp.sum(e, axis=-1, keepdims=True))
        exps.append(e.astype(BF16))
    outs = [jnp.dot(e, kv_ref[:, MEM_WIDTH + cols.start:MEM_WIDTH + cols.stop], preferred_element_type=F32)
            for e, cols in zip(exps, head_cols)]
    for o, l, cols in zip(outs, sums, head_cols):
        o_ref[:, cols] = (o / l).astype(o_ref.dtype)


def memory_cross_attention(pb, kv, batch, q_gain, k_gain, *, tq=512):
    m = pb.shape[0]
    steps = m // batch // tq
    n_mem = kv.shape[0] // batch
    cq = PB_COL["mem_q"] // MEM_WIDTH
    return pl.pallas_call(
        _mem_attn_kernel,
        out_shape=jax.ShapeDtypeStruct((m, MEM_WIDTH), BF16),
        grid=(batch, steps),
        in_specs=[
            pl.BlockSpec((tq, MEM_WIDTH), lambda b, t: (b * steps + t, cq)),
            pl.BlockSpec((n_mem, 2 * MEM_WIDTH), lambda b, t: (b, 0)),
            pl.BlockSpec((1, MEM_HEAD_DIM), lambda b, t: (0, 0)),
            pl.BlockSpec((1, MEM_HEAD_DIM), lambda b, t: (0, 0)),
        ],
        out_specs=pl.BlockSpec((tq, MEM_WIDTH), lambda b, t: (b * steps + t, 0)),
        scratch_shapes=[pltpu.VMEM((n_mem, MEM_WIDTH), BF16)],
        compiler_params=_params("parallel", "arbitrary"),
        name="memory_cross_attention",
    )(pb, kv, q_gain.astype(F32).reshape(1, MEM_HEAD_DIM), k_gain.astype(F32).reshape(1, MEM_HEAD_DIM))


LIN_BLOCK = 512
HEAD_V = 128


def _log1p_exp_neg(t):
    return jnp.log(1.0 + jnp.exp(-t))


def _log_sigmoid(x):
    return jnp.minimum(x, 0.0) - _log1p_exp_neg(jnp.abs(x))


def _logaddexp(a, b):
    return jnp.maximum(a, b) + _log1p_exp_neg(jnp.abs(a - b))


def _split_bf16(x, terms):
    parts = []
    for _ in range(terms):
        p = x.astype(BF16)
        parts.append(p)
        x = x - p.astype(F32)
    return parts


def _dot_f32(a, b):
    a_hi, a_lo = _split_bf16(a, 2)
    b_hi, b_lo = _split_bf16(b, 2)
    return (jnp.dot(a_hi, b_hi, preferred_element_type=F32)
            + (jnp.dot(a_hi, b_lo, preferred_element_type=F32) + jnp.dot(a_lo, b_hi, preferred_element_type=F32)))


def _cumsum_rows(mask, x):
    m = jnp.where(mask, 1.0, 0.0).astype(BF16)
    hi, mid, lo = _split_bf16(x, 3)
    return (jnp.dot(m, hi, preferred_element_type=F32)
            + (jnp.dot(m, mid, preferred_element_type=F32) + jnp.dot(m, lo, preferred_element_type=F32)))


def _gla_inputs(refs, rows, direction, params):
    q_ref, k_ref, v_ref, g_ref = refs
    wpad_ref, bias_ref = params
    qc = q_ref[rows, :].astype(F32) * (GLA_HEAD_K ** -0.5)
    kc = k_ref[rows, :].astype(F32)
    gk = _dot_f32(g_ref[rows, :], wpad_ref[direction]) + bias_ref[direction]
    lg = _log_sigmoid(gk) * (1.0 / GLA_GATE_NORMALIZER)
    return qc, kc, v_ref[rows, :], lg


def _hgrn_inputs(refs, rows, direction, params):
    q_ref, v_ref, z_ref = refs
    lb_ref, log_lb_ref, log1m_lb_ref = params
    qr = q_ref[rows, :].astype(F32)
    qc = qr * _sigmoid(qr)
    z = z_ref[rows, :]
    lg = _logaddexp(log_lb_ref[direction], log1m_lb_ref[direction] + _log_sigmoid(z))
    kc = (1.0 - lb_ref[direction]) * _sigmoid(-z)
    return qc, kc, v_ref[rows, :], lg


GDN_CONV_WIDTH = 5
GDN_QKV_WIDTH = 2 * GDN_KEY_WIDTH + GDN_VAL_WIDTH
GDN_HALO = 16


def _gdn_prep_kernel(prev_ref, cur_ref, next_ref, w_ref, o_ref, xp_ref, *, blocks_per_seq):
    i = pl.program_id(0)
    t = cur_ref.shape[0]
    pos = i % blocks_per_seq
    prev = prev_ref[...].astype(F32)
    nxt = next_ref[...].astype(F32)
    xp_ref[0:GDN_HALO, :] = jnp.where(pos == 0, jnp.zeros_like(prev), prev)
    xp_ref[GDN_HALO:GDN_HALO + t, :] = cur_ref[...].astype(F32)
    xp_ref[GDN_HALO + t:, :] = jnp.where(pos == blocks_per_seq - 1, jnp.zeros_like(nxt), nxt)
    half = GDN_CONV_WIDTH // 2
    for g in range(GDN_QKV_WIDTH // LANES):
        cols = slice(g * LANES, (g + 1) * LANES)
        acc = None
        for j in range(GDN_CONV_WIDTH):
            term = xp_ref[GDN_HALO - half + j:GDN_HALO - half + j + t, cols] * w_ref[j:j + 1, cols]
            acc = term if acc is None else acc + term
        y = acc * _sigmoid(acc)
        if g < 2 * GDN_HEADS:
            y = y * lax.rsqrt(jnp.sum(y * y, axis=-1, keepdims=True) + 1e-6)
            if g < GDN_HEADS:
                y = y * (GDN_HEAD_K ** -0.5)
        o_ref[:, cols] = y.astype(o_ref.dtype)


def gdn_prep(pb, batch, conv_w, *, t=512):
    m = pb.shape[0]
    blocks_per_seq = m // batch // t
    halo_per_block = t // GDN_HALO
    col = PB_COL["gdn_qkv"] // GDN_QKV_WIDTH
    last_halo = m // GDN_HALO - 1
    return pl.pallas_call(
        functools.partial(_gdn_prep_kernel, blocks_per_seq=blocks_per_seq),
        out_shape=jax.ShapeDtypeStruct((m, GDN_QKV_WIDTH), BF16),
        grid=(m // t,),
        in_specs=[
            pl.BlockSpec((GDN_HALO, GDN_QKV_WIDTH), lambda i: (jnp.maximum(i * halo_per_block - 1, 0), col)),
            pl.BlockSpec((t, GDN_QKV_WIDTH), lambda i: (i, col)),
            pl.BlockSpec((GDN_HALO, GDN_QKV_WIDTH),
                         lambda i: (jnp.minimum((i + 1) * halo_per_block, last_halo), col)),
            pl.BlockSpec((GDN_CONV_WIDTH, GDN_QKV_WIDTH), lambda i: (0, 0)),
        ],
        out_specs=pl.BlockSpec((t, GDN_QKV_WIDTH), lambda i: (i, 0)),
        scratch_shapes=[pltpu.VMEM((t + 2 * GDN_HALO, GDN_QKV_WIDTH), F32)],
        compiler_params=_params("parallel"),
        name="gdn_prep",
    )(pb, pb, pb, conv_w.astype(F32))


def _softplus(x):
    return jnp.maximum(x, 0.0) + _log1p_exp_neg(jnp.abs(x))


GDN_PACK = GDN_HEADS * GDN_CHUNK
GDN_WY_BLOCK = 512


def _stack_heads(x, width):
    heads = x.shape[1] // width
    lane = lax.broadcasted_iota(jnp.int32, (1, x.shape[1]), 1)
    return jnp.concatenate(
        [jnp.where((lane >= h * width) & (lane < (h + 1) * width), x, 0.0).astype(BF16) for h in range(heads)],
        axis=0)


def _packed_mm(x, y):
    return jnp.dot(x.astype(BF16), _stack_heads(y, GDN_CHUNK), preferred_element_type=F32)


def _packed_inverses(mats):
    c = GDN_CHUNK
    ii = lax.broadcasted_iota(jnp.int32, (c, GDN_PACK), 0)
    jj = lax.broadcasted_iota(jnp.int32, (c, GDN_PACK), 1) % c
    eye = (ii == jj).astype(F32)

    def same_block(s):
        return (ii // s) == (jj // s)

    ds = [jnp.where(same_block(8), a, 0.0) for a in mats]
    d2s = [_packed_mm(d, d) for d in ds]
    d4s = [_packed_mm(d2, d2) for d2 in d2s]
    ts = [_packed_mm(eye - d, eye + d2) for d, d2 in zip(ds, d2s)]
    ts = [_packed_mm(t, eye + d4) for t, d4 in zip(ts, d4s)]
    s = 8
    while s < c:
        off = same_block(2 * s) & jnp.logical_not(same_block(s))
        ets = [_packed_mm(jnp.where(off, a, 0.0), t) for a, t in zip(mats, ts)]
        ts = [t - _packed_mm(t, et) for t, et in zip(ts, ets)]
        s *= 2
    return ts


def _gdn_wy_kernel(qkv_ref, small_ref, a_ref, dtb_ref, selg_ref, selk_ref, selb_ref, *out_refs):
    c = GDN_CHUNK
    n_chunks = qkv_ref.shape[0] // c
    ii = lax.broadcasted_iota(jnp.int32, (c, c), 0)
    jj = lax.broadcasted_iota(jnp.int32, (c, c), 1)
    pi = lax.broadcasted_iota(jnp.int32, (c, GDN_PACK), 0)
    pj = lax.broadcasted_iota(jnp.int32, (c, GDN_PACK), 1) % c
    eye_p = (pi == pj).astype(F32)
    ones_cc = jnp.ones((c, c), BF16)

    problems = [(ch, d) for ch in range(n_chunks) for d in range(2)]
    chunk_in = []
    for ch in range(n_chunks):
        rows = slice(ch * c, (ch + 1) * c)
        qkv = qkv_ref[rows, :]
        small = small_ref[rows, :]
        kf = qkv[:, GDN_KEY_WIDTH:2 * GDN_KEY_WIDTH].astype(F32)
        chunk_in.append(dict(
            qf=qkv[:, :GDN_KEY_WIDTH].astype(F32), kf=kf, vf=qkv[:, 2 * GDN_KEY_WIDTH:].astype(F32),
            kbd=_stack_heads(kf, HEAD_V),
            log_alpha=a_ref[...] * _softplus(small + dtb_ref[...]),
            beta_all=_sigmoid(small)))

    def sel3(x, sel):
        hi, mid, lo = _split_bf16(x, 3)
        return (jnp.dot(hi, sel, preferred_element_type=F32)
                + (jnp.dot(mid, sel, preferred_element_type=F32) + jnp.dot(lo, sel, preferred_element_type=F32)))

    g_all = [_cumsum_rows((jj >= ii) if d else (jj <= ii), chunk_in[ch]["log_alpha"]) for ch, d in problems]
    g_pack = [sel3(g, selg_ref[d]) for g, (ch, d) in zip(g_all, problems)]
    g_wide = [sel3(g, selk_ref[d]) for g, (ch, d) in zip(g_all, problems)]
    beta_w = [sel3(chunk_in[ch]["beta_all"], selb_ref[d]) for ch, d in problems]
    g_rowp = []
    for gp in g_pack:
        hi, mid, lo = _split_bf16(gp * eye_p, 3)
        g_rowp.append(jnp.dot(ones_cc, hi, preferred_element_type=F32)
                      + (jnp.dot(ones_cc, mid, preferred_element_type=F32)
                         + jnp.dot(ones_cc, lo, preferred_element_type=F32)))
    decays, k_betas = [], []
    for gp, gr, bw, (ch, d) in zip(g_pack, g_rowp, beta_w, problems):
        incl = (pj >= pi) if d else (pj <= pi)
        decays.append(jnp.where(incl, jnp.exp(jnp.where(incl, gp - gr, 0.0)), 0.0))
        k_betas.append(chunk_in[ch]["kf"] * bw)
    kq = [lax.dot_general(jnp.concatenate([kb, chunk_in[ch]["qf"]], axis=0).astype(BF16), chunk_in[ch]["kbd"],
                          (((1,), (1,)), ((), ())), preferred_element_type=F32)
          for kb, (ch, d) in zip(k_betas, problems)]
    a_mats = []
    for x, dec, (ch, d) in zip(kq, decays, problems):
        strict = (pj > pi) if d else (pj < pi)
        a_mats.append(jnp.where(strict, x[:c] * dec, 0.0))
    t_invs = _packed_inverses(a_mats)

    for idx, (ch, d) in enumerate(problems):
        u_ref, w_ref, attn_ref, qd_ref, kd_ref, gt_ref = out_refs[6 * d:6 * d + 6]
        rows = slice(ch * c, (ch + 1) * c)
        cin = chunk_in[ch]
        gw = g_wide[idx]
        eg = jnp.exp(gw)
        t_b = t_invs[idx].astype(BF16)
        u_ref[rows, :] = jnp.dot(t_b, _stack_heads(cin["vf"] * beta_w[idx], HEAD_V), preferred_element_type=F32)
        w_ref[rows, :] = jnp.dot(t_b, _stack_heads(k_betas[idx] * eg, HEAD_V),
                                 preferred_element_type=F32).astype(w_ref.dtype)
        attn_ref[rows, :] = (kq[idx][c:] * decays[idx]).astype(attn_ref.dtype)
        end = 0 if d else c - 1
        g_end = gw[end:end + 1, :]
        qd_ref[rows, :] = (cin["qf"] * eg).astype(qd_ref.dtype)
        kd_ref[rows, :] = (cin["kf"] * jnp.exp(g_end - gw)).astype(kd_ref.dtype)
        gt_ref[ch:ch + 1, :] = jnp.exp(g_end)


def gdn_wy(qkv, pf, a_scale, dtb):
    m = qkv.shape[0]
    t = GDN_WY_BLOCK
    cpb = t // GDN_CHUNK
    selg = np.zeros((2, LANES, GDN_PACK), np.float32)
    selk = np.zeros((2, LANES, GDN_VAL_WIDTH), np.float32)
    selb = np.zeros((2, LANES, GDN_VAL_WIDTH), np.float32)
    for d in range(2):
        for h in range(GDN_HEADS):
            selg[d, GDN_A_LANE + d * GDN_HEADS + h, h * GDN_CHUNK:(h + 1) * GDN_CHUNK] = 1.0
            selk[d, GDN_A_LANE + d * GDN_HEADS + h, h * HEAD_V:(h + 1) * HEAD_V] = 1.0
            selb[d, GDN_B_LANE + d * GDN_HEADS + h, h * HEAD_V:(h + 1) * HEAD_V] = 1.0
    wide = GDN_VAL_WIDTH
    out_shape, out_specs = [], []
    for _ in range(2):
        for width, dt in ((wide, F32), (wide, BF16), (GDN_PACK, BF16), (wide, BF16), (wide, BF16)):
            out_shape.append(jax.ShapeDtypeStruct((m, width), dt))
            out_specs.append(pl.BlockSpec((t, width), lambda i: (i, 0)))
        out_shape.append(jax.ShapeDtypeStruct((m // GDN_CHUNK, wide), F32))
        out_specs.append(pl.BlockSpec((cpb, wide), lambda i: (i, 0)))
    return pl.pallas_call(
        _gdn_wy_kernel,
        out_shape=tuple(out_shape),
        grid=(m // t,),
        in_specs=[
            pl.BlockSpec((t, GDN_QKV_WIDTH), lambda i: (i, 0)),
            pl.BlockSpec((t, LANES), lambda i: (i, PF_SMALL_COL // LANES)),
            pl.BlockSpec((1, LANES), lambda i: (0, 0)),
            pl.BlockSpec((1, LANES), lambda i: (0, 0)),
            pl.BlockSpec((2, LANES, GDN_PACK), lambda i: (0, 0, 0)),
            pl.BlockSpec((2, LANES, wide), lambda i: (0, 0, 0)),
            pl.BlockSpec((2, LANES, wide), lambda i: (0, 0, 0)),
        ],
        out_specs=tuple(out_specs),
        compiler_params=_params("parallel"),
        name="gdn_wy",
    )(qkv, pf, a_scale, dtb, jnp.asarray(selg, BF16), jnp.asarray(selk, BF16), jnp.asarray(selb, BF16))


GDN_PAIR = 2 * HEAD_V


def _gdn_scan_kernel(*refs):
    groups = (refs[0:6], refs[6:12])
    out_refs = refs[12:14]
    state_refs = refs[14:16]

    @pl.when(pl.program_id(1) == 0)
    def _():
        for s_ref in state_refs:
            s_ref[...] = jnp.zeros_like(s_ref)

    n_chunks = out_refs[0].shape[0] // GDN_CHUNK
    pairs = GDN_HEADS // 2
    pair_cols = [slice(p * GDN_PAIR, (p + 1) * GDN_PAIR) for p in range(pairs)]
    ri = lax.broadcasted_iota(jnp.int32, (GDN_PAIR, GDN_PAIR), 0) // HEAD_V
    ci = lax.broadcasted_iota(jnp.int32, (GDN_PAIR, GDN_PAIR), 1) // HEAD_V
    diag = ri == ci

    def body(c, carry):
        chunks = (c, n_chunks - 1 - c)
        rows = [pl.ds(pl.multiple_of(ch * GDN_CHUNK, GDN_CHUNK), GDN_CHUNK) for ch in chunks]
        states = [[s_ref[p] for p in range(pairs)] for s_ref in state_refs]
        states_b = [[s.astype(BF16) for s in st] for st in states]
        wq = [[jnp.dot(jnp.concatenate([groups[g][1][rows[g], cols], groups[g][3][rows[g], cols]], axis=0),
                       states_b[g][p], preferred_element_type=F32)
               for p, cols in enumerate(pair_cols)] for g in range(2)]
        ws = [[x[:GDN_CHUNK] for x in wq[g]] for g in range(2)]
        qs = [[x[GDN_CHUNK:] for x in wq[g]] for g in range(2)]
        v_new = [groups[g][0][rows[g], :] - jnp.concatenate(ws[g], axis=1) for g in range(2)]
        av = [jnp.dot(groups[g][2][rows[g], :], _stack_heads(v_new[g], HEAD_V), preferred_element_type=F32)
              for g in range(2)]
        v_new_b = [v.astype(BF16) for v in v_new]
        upd = [[lax.dot_general(groups[g][4][rows[g], cols], v_new_b[g][:, cols], (((0,), (0,)), ((), ())),
                                preferred_element_type=F32) for cols in pair_cols] for g in range(2)]
        for g in range(2):
            out_refs[g][rows[g], :] = jnp.concatenate(qs[g], axis=1) + av[g]
            gt = groups[g][5][pl.ds(chunks[g], 1), :]
            for p, cols in enumerate(pair_cols):
                state_refs[g][p] = states[g][p] * gt[:, cols] + jnp.where(diag, upd[g][p], 0.0)
        return carry

    lax.fori_loop(0, n_chunks, body, 0)


def gated_deltanet_branch(pb, pf, batch, conv_w, a_log, dt_bias, norm_gain):
    m = pb.shape[0]
    nb = m // batch // LIN_BLOCK
    cpb = LIN_BLOCK // GDN_CHUNK
    qkv = gdn_prep(pb, batch, conv_w)
    n_gate = 2 * GDN_HEADS
    a_scale = jnp.zeros((1, LANES), F32).at[0, GDN_A_LANE:GDN_A_LANE + n_gate].set(
        -jnp.exp(a_log.astype(F32)).reshape(n_gate))
    dtb = jnp.zeros((1, LANES), F32).at[0, GDN_A_LANE:GDN_A_LANE + n_gate].set(dt_bias.astype(F32).reshape(n_gate))
    wy = gdn_wy(qkv, pf, a_scale, dtb)
    widths = (GDN_VAL_WIDTH, GDN_VAL_WIDTH, GDN_PACK, GDN_VAL_WIDTH, GDN_VAL_WIDTH)

    def fwd(rows, width):
        return pl.BlockSpec((rows, width), lambda b, t: (b * nb + t, 0))

    def bwd(rows, width):
        return pl.BlockSpec((rows, width), lambda b, t: (b * nb + nb - 1 - t, 0))

    in_specs = [fwd(LIN_BLOCK, w) for w in widths] + [fwd(cpb, GDN_VAL_WIDTH)]
    in_specs += [bwd(LIN_BLOCK, w) for w in widths] + [bwd(cpb, GDN_VAL_WIDTH)]
    state = pltpu.VMEM((GDN_HEADS // 2, GDN_PAIR, GDN_PAIR), F32)
    o_f, o_b = pl.pallas_call(
        _gdn_scan_kernel,
        out_shape=(jax.ShapeDtypeStruct((m, GDN_VAL_WIDTH), F32), jax.ShapeDtypeStruct((m, GDN_VAL_WIDTH), F32)),
        grid=(batch, nb),
        in_specs=in_specs,
        out_specs=(fwd(LIN_BLOCK, GDN_VAL_WIDTH), bwd(LIN_BLOCK, GDN_VAL_WIDTH)),
        scratch_shapes=[state, state],
        compiler_params=_params("parallel", "arbitrary"),
        name="gdn_scan",
    )(*wy)
    return RawBranch(o_f, o_b, "gdn_og", norm_gain, True)


def _dot3(m, x):
    hi, mid, lo = _split_bf16(x, 3)
    return (jnp.dot(m, hi, preferred_element_type=F32)
            + (jnp.dot(m, mid, preferred_element_type=F32) + jnp.dot(m, lo, preferred_element_type=F32)))


LIN_CUM_ROWS = 256
LIN_SCORE_ROWS = 128


def _chunk_causal(n, chunk, reverse):
    i = lax.broadcasted_iota(jnp.int32, (n, n), 0)
    j = lax.broadcasted_iota(jnp.int32, (n, n), 1)
    return ((i // chunk) == (j // chunk)) & ((j >= i) if reverse else (j <= i))


def _lin_intra_kernel(*refs, load_inputs, n_in, n_params, heads):
    dir_refs = (refs[:n_in], refs[n_in:2 * n_in])
    params = refs[2 * n_in:2 * n_in + n_params]
    out_refs = refs[2 * n_in + n_params:]
    c = LIN_CHUNK
    dirs = (0, 1)
    loaded = [load_inputs(dir_refs[d], slice(None), d, params) for d in dirs]
    t, w = loaded[0][0].shape
    dk = w // heads
    nc = t // c
    cums = [jnp.where(_chunk_causal(LIN_CUM_ROWS, c, d == 1), 1.0, 0.0).astype(BF16) for d in dirs]
    bs = [jnp.concatenate([_dot3(cums[d], loaded[d][3][r:r + LIN_CUM_ROWS, :])
                           for r in range(0, t, LIN_CUM_ROWS)], axis=0) for d in dirs]
    qes, kes = [], []
    for d in dirs:
        oi_ref, qd_ref, kd_ref, gt_ref = out_refs[4 * d:4 * d + 4]
        qc, kc, vc, lg = loaded[d]
        b = bs[d]
        b3 = b.reshape(nc, c, w)
        mid = c - 1 - c // 2 if d else c // 2
        end = 0 if d else c - 1
        b_mid = jnp.broadcast_to(b3[:, mid:mid + 1, :], (nc, c, w)).reshape(t, w)
        b_end = jnp.broadcast_to(b3[:, end:end + 1, :], (nc, c, w)).reshape(t, w)
        qes.append((qc * jnp.exp(b - b_mid)).astype(BF16))
        kes.append((kc * jnp.exp(b_mid - b)).astype(BF16))
        qd_ref[...] = (qc * jnp.exp(b)).astype(qd_ref.dtype)
        kd_ref[...] = (kc * jnp.exp(b_end - b)).astype(kd_ref.dtype)
        gt_ref[...] = jnp.exp(b3[:, end, :])
    keeps = [_chunk_causal(LIN_SCORE_ROWS, c, d == 1) for d in dirs]
    lane = lax.broadcasted_iota(jnp.int32, (1, LANES), 1)
    for h in range(heads):
        win = slice((h * dk) // LANES * LANES, (h * dk) // LANES * LANES + LANES)
        lo = h * dk - win.start
        vcols = slice(h * HEAD_V, (h + 1) * HEAD_V)
        tiles = [(slice(r, r + LIN_SCORE_ROWS), d) for r in range(0, t, LIN_SCORE_ROWS) for d in dirs]
        scores = []
        for rows, d in tiles:
            qh = qes[d][rows, win]
            if dk < LANES:
                qh = jnp.where((lane >= lo) & (lane < lo + dk), qh, jnp.zeros_like(qh))
            scores.append(lax.dot_general(qh, kes[d][rows, win], (((1,), (1,)), ((), ())),
                                          preferred_element_type=F32))
        probs = [jnp.where(keeps[d], s, 0.0).astype(BF16) for s, (rows, d) in zip(scores, tiles)]
        for p, (rows, d) in zip(probs, tiles):
            out_refs[4 * d][rows, vcols] = jnp.dot(p, loaded[d][2][rows, vcols], preferred_element_type=F32)


def _lin_scan_kernel(*refs, heads, chunk, unroll):
    groups = (refs[0:5], refs[5:10])
    out_refs = refs[10:12]
    state_refs = refs[12:14]

    @pl.when(pl.program_id(1) == 0)
    def _():
        for s_ref in state_refs:
            s_ref[...] = jnp.zeros_like(s_ref)

    n_chunks = out_refs[0].shape[0] // chunk
    w = state_refs[0].shape[1]
    dk = w // heads
    lane = lax.broadcasted_iota(jnp.int32, (1, w), 1)
    masks = [(lane >= h * dk) & (lane < (h + 1) * dk) for h in range(heads)]

    def stack(x):
        return jnp.concatenate([jnp.where(m, x, jnp.zeros_like(x)) for m in masks], axis=0)

    def body(it, carry):
        steps = []
        for u in range(unroll):
            c = it * unroll + u
            steps += [(0, c), (1, n_chunks - 1 - c)]
        prepared = []
        for g, ch in steps:
            rows = pl.ds(pl.multiple_of(ch * chunk, chunk), chunk)
            oi_ref, qd_ref, kd_ref, v_ref, gt_ref = groups[g]
            vc = v_ref[rows, :]
            v4 = jnp.concatenate([vc[:, h * HEAD_V:(h + 1) * HEAD_V] for h in range(heads)], axis=0)
            upd = lax.dot_general(v4, stack(kd_ref[rows, :]), (((0,), (0,)), ((), ())),
                                  preferred_element_type=F32)
            prepared.append((rows, stack(qd_ref[rows, :]), upd, gt_ref[pl.ds(ch, 1), :]))
        states = [s_ref[...] for s_ref in state_refs]
        for (g, ch), (rows, q4, upd, gt) in zip(steps, prepared):
            o_inter = lax.dot_general(q4, states[g].astype(BF16), (((1,), (1,)), ((), ())),
                                      preferred_element_type=F32)
            out_refs[g][rows, :] = groups[g][0][rows, :] + jnp.concatenate(
                [o_inter[h * chunk:(h + 1) * chunk, :] for h in range(heads)], axis=1)
            states[g] = states[g] * gt + upd
        for s_ref, st in zip(state_refs, states):
            s_ref[...] = st
        return carry

    lax.fori_loop(0, n_chunks // unroll, body, 0)


def _bidir_lin_call(name, load_inputs, arrays, col_blocks, widths, params, batch, heads, key_width, v_col):
    m = arrays[0].shape[0]
    t = LIN_BLOCK
    nb = m // batch // t
    cpb = t // LIN_CHUNK
    out_w = heads * HEAD_V
    n_in = len(arrays)

    in_specs, operands = [], []
    for d in range(2):
        for a, wd, cb in zip(arrays, widths, col_blocks):
            in_specs.append(pl.BlockSpec((t, wd), functools.partial(lambda i, c: (i, c), c=cb[d])))
            operands.append(a)
    for p in params:
        in_specs.append(pl.BlockSpec(p.shape, functools.partial(lambda i, nd: (0,) * nd, nd=p.ndim)))
    out_shape, out_specs = [], []
    for _ in range(2):
        for rows_total, rows_blk, width, dt in ((m, t, out_w, F32), (m, t, key_width, BF16),
                                                (m, t, key_width, BF16), (m // LIN_CHUNK, cpb, key_width, F32)):
            out_shape.append(jax.ShapeDtypeStruct((rows_total, width), dt))
            out_specs.append(pl.BlockSpec((rows_blk, width), lambda i: (i, 0)))

    intra = pl.pallas_call(
        functools.partial(_lin_intra_kernel, load_inputs=load_inputs, n_in=n_in, n_params=len(params),
                          heads=heads),
        out_shape=tuple(out_shape),
        grid=(m // t,),
        in_specs=in_specs,
        out_specs=tuple(out_specs),
        compiler_params=_params("parallel"),
        name=name + "_intra",
    )(*operands, *params)

    def fwd(rows, width, col=0):
        return pl.BlockSpec((rows, width), lambda b, s: (b * nb + s, col))

    def bwd(rows, width, col=0):
        return pl.BlockSpec((rows, width), lambda b, s: (b * nb + nb - 1 - s, col))

    scan_specs, scan_ops = [], []
    for d, mk in enumerate((fwd, bwd)):
        oi, qd, kd, gt = intra[4 * d:4 * d + 4]
        scan_specs += [mk(t, out_w), mk(t, key_width), mk(t, key_width), mk(t, out_w, v_col), mk(cpb, key_width)]
        scan_ops += [oi, qd, kd, arrays[0], gt]
    state = pltpu.VMEM((HEAD_V, key_width), F32)
    return pl.pallas_call(
        functools.partial(_lin_scan_kernel, heads=heads, chunk=LIN_CHUNK, unroll=4),
        out_shape=(jax.ShapeDtypeStruct((m, out_w), F32), jax.ShapeDtypeStruct((m, out_w), F32)),
        grid=(batch, nb),
        in_specs=scan_specs,
        out_specs=(fwd(t, out_w), bwd(t, out_w)),
        scratch_shapes=[state, state],
        compiler_params=_params("parallel", "arbitrary"),
        name=name + "_scan",
    )(*scan_ops)


def gla_branch(pb, pf, batch, w_gate_up, b_gate, norm_gain):
    wpad = jnp.zeros((2, LANES, GLA_KEY_WIDTH), F32)
    for d in range(2):
        wpad = wpad.at[d, d * GLA_GATE_RANK:(d + 1) * GLA_GATE_RANK, :].set(w_gate_up[d].astype(F32))
    bias = b_gate.astype(F32).reshape(2, 1, GLA_KEY_WIDTH)
    v_col = PB_COL["gla_v"] // GLA_VAL_WIDTH
    cols = [(PB_COL["gla_q"] // GLA_KEY_WIDTH,) * 2, (PB_COL["gla_k"] // GLA_KEY_WIDTH,) * 2,
            (v_col,) * 2, (PF_SMALL_COL // LANES,) * 2]
    o_f, o_b = _bidir_lin_call("gla", _gla_inputs, [pb, pb, pb, pf], cols,
                               [GLA_KEY_WIDTH, GLA_KEY_WIDTH, GLA_VAL_WIDTH, LANES], [wpad, bias],
                               batch, GLA_HEADS, GLA_KEY_WIDTH, v_col)
    return RawBranch(o_f, o_b, "gla_og", norm_gain, True)


def hgrn2_branch(pb, pf, batch, lower_bound, norm_gain):
    lb = lower_bound.astype(F32).reshape(2, 1, HGRN_KEY_WIDTH)
    log_lb = jnp.log(jnp.maximum(lb, LB_FLOOR))
    log1m_lb = jnp.log1p(-lb)
    zc = PF_COL["hg_f"] // HGRN_KEY_WIDTH
    v_col = PB_COL["hg_i"] // HGRN_VAL_WIDTH
    cols = [(PB_COL["hg_q"] // HGRN_KEY_WIDTH,) * 2, (v_col,) * 2, (zc, zc + 1)]
    o_f, o_b = _bidir_lin_call("hgrn2", _hgrn_inputs, [pb, pb, pf], cols,
                               [HGRN_KEY_WIDTH, HGRN_VAL_WIDTH, HGRN_KEY_WIDTH], [lb, log_lb, log1m_lb],
                               batch, HGRN_HEADS, HGRN_KEY_WIDTH, v_col)
    return RawBranch(o_f, o_b, "hg_og", norm_gain, False)


def kernel(x, mem, g_mix, w_in, na_q_gain, na_k_gain, na_rel_bias, gla_w_gate_up, gla_b_gate, gla_norm_gain, gdn_conv_w, gdn_a_log, gdn_dt_bias, gdn_norm_gain, hgrn_lb_raw, hgrn_norm_gain, g_mem, w_mem_kv, mem_q_gain, mem_k_gain, w_branch, w_out, g_ffn, ffn_w_gate, ffn_w_up, ffn_w_down, moe_w_router, moe_b_router, moe_w_gate, moe_w_up, moe_w_down):
    B, S, D = x.shape
    n_tok = B * S
    lb_w = jax.nn.softmax(hgrn_lb_raw.astype(F32), axis=0)
    hgrn_lb = jnp.cumsum(lb_w, axis=0) - lb_w[0:1]
    x2 = x.reshape(n_tok, D)
    mem2 = mem.reshape(B * mem.shape[1], D)
    for layer in range(DEPTH):
        pb, pf = in_projection(x2, g_mix[layer], _rearrange_w_in(w_in[layer]), PB_COL["gates"], PB_WIDTH)
        kv = rms_matmul(mem2, g_mem[layer], w_mem_kv[layer].astype(BF16), tm=mem2.shape[0], tn=512,
                        out_dtype=BF16)
        branches = [
            neighbourhood_attention(pb, B, na_q_gain[layer], na_k_gain[layer], na_rel_bias[layer]),
            gla_branch(pb, pf, B, gla_w_gate_up[layer], gla_b_gate[layer], gla_norm_gain[layer]),
            gated_deltanet_branch(pb, pf, B, gdn_conv_w[layer], gdn_a_log[layer], gdn_dt_bias[layer],
                                  gdn_norm_gain[layer]),
            hgrn2_branch(pb, pf, B, hgrn_lb[layer], hgrn_norm_gain[layer]),
            memory_cross_attention(pb, kv, B, mem_q_gain[layer], mem_k_gain[layer]),
        ]
        merged = merge_branches(branches, pb, w_branch[layer].astype(BF16), tm=512, tn=1024)
        x2 = matmul_residual(merged, w_out[layer].astype(BF16), x2, tm=1024, tn=512)

        j = layer // 2
        if layer % 2 == 0:
            act = rms_swiglu_up(x2, g_ffn[layer], ffn_w_gate[j].astype(BF16), ffn_w_up[j].astype(BF16),
                                tm=1024, tn=512)
            x2 = matmul_residual(act, ffn_w_down[j].astype(BF16), x2, tm=512, tn=512)
        else:
            x2 = moe_layer(x2, g_ffn[layer], moe_w_router[j], moe_b_router[j], moe_w_gate[j], moe_w_up[j],
                           moe_w_down[j])
    return x2.reshape(B, S, D)
```

```python
import functools

import jax
import jax.numpy as jnp
import numpy as np
from jax import lax
from jax.experimental import pallas as pl
from jax.experimental.pallas import tpu as pltpu

F32 = jnp.float32
BF16 = jnp.bfloat16

D_MODEL = 2048
DEPTH = 2
RMS_EPS = 1e-6
MASK_VALUE = -1e30
LB_FLOOR = 1e-30
GRID_W = 64

NA_HEADS = 8
NA_HEAD_DIM = 64
NA_WIDTH = 512
NA_WIN_ROWS = 8
NA_WIN_COLS = 16

GLA_HEADS = 4
GLA_HEAD_K = 64
GLA_HEAD_V = 128
GLA_KEY_WIDTH = 256
GLA_VAL_WIDTH = 512
GLA_GATE_RANK = 16
GLA_GATE_NORMALIZER = 16.0

GDN_HEADS = 4
GDN_HEAD_K = 128
GDN_HEAD_V = 128
GDN_KEY_WIDTH = 512
GDN_VAL_WIDTH = 512
GDN_CHUNK = 64

HGRN_HEADS = 4
HGRN_HEAD_K = 128
HGRN_HEAD_V = 128
HGRN_KEY_WIDTH = 512
HGRN_VAL_WIDTH = 512

LIN_CHUNK = 32

MEM_HEADS = 4
MEM_HEAD_DIM = 128
MEM_WIDTH = 512

N_BRANCH = 5
BRANCH_WIDTH = 512
N_EXPERTS = 8
MOE_TOP_K = 2

IN_WIDTHS = (
    NA_WIDTH, NA_WIDTH, NA_WIDTH,
    GLA_KEY_WIDTH, GLA_KEY_WIDTH, GLA_VAL_WIDTH,
    2 * GLA_GATE_RANK, GLA_VAL_WIDTH,
    2 * GDN_KEY_WIDTH + GDN_VAL_WIDTH,
    2 * GDN_HEADS, 2 * GDN_HEADS, GDN_VAL_WIDTH,
    HGRN_KEY_WIDTH, 2 * HGRN_KEY_WIDTH, HGRN_VAL_WIDTH, HGRN_VAL_WIDTH,
    MEM_WIDTH,
    N_BRANCH * D_MODEL,
)
P_IN = sum(IN_WIDTHS)

V7X_VMEM_BYTES = 64 * 1024 * 1024
VMEM_LIMIT_BYTES = V7X_VMEM_BYTES - 8 * 1024 * 1024
LANES = 128


def _params(*semantics):
    return pltpu.CompilerParams(dimension_semantics=semantics, vmem_limit_bytes=VMEM_LIMIT_BYTES)


def _sigmoid(x):
    return 0.5 * jnp.tanh(0.5 * x) + 0.5


def _rms_norm_rows(x, gain):
    ms = jnp.mean(x * x, axis=-1, keepdims=True)
    return x * lax.rsqrt(ms + RMS_EPS) * gain


def _rms_matmul_kernel(x_ref, g_ref, w_ref, o_ref, h_ref):
    @pl.when(pl.program_id(1) == 0)
    def _():
        h_ref[...] = _rms_norm_rows(x_ref[...], g_ref[...]).astype(BF16)

    o_ref[...] = jnp.dot(h_ref[...], w_ref[...], preferred_element_type=F32).astype(o_ref.dtype)


def rms_matmul(x, gain, w, *, tm, tn, out_dtype=F32):
    m, k = x.shape
    n = w.shape[1]
    return pl.pallas_call(
        _rms_matmul_kernel,
        out_shape=jax.ShapeDtypeStruct((m, n), out_dtype),
        grid=(m // tm, n // tn),
        in_specs=[
            pl.BlockSpec((tm, k), lambda i, j: (i, 0)),
            pl.BlockSpec((1, k), lambda i, j: (0, 0)),
            pl.BlockSpec((k, tn), lambda i, j: (0, j)),
        ],
        out_specs=pl.BlockSpec((tm, tn), lambda i, j: (i, j)),
        scratch_shapes=[pltpu.VMEM((tm, k), BF16)],
        compiler_params=_params("parallel", "arbitrary"),
        name="rms_matmul",
    )(x, gain.reshape(1, k), w)


IN_PROJ_TILE = 512


def _in_projection_kernel(x_ref, g_ref, w_ref, ob_ref, of_ref, h_ref, *, n_plain_tiles, n_bf16_tiles):
    j = pl.program_id(1)

    @pl.when(j == 0)
    def _():
        h_ref[...] = _rms_norm_rows(x_ref[...], g_ref[...]).astype(BF16)

    r = jnp.dot(h_ref[...], w_ref[...], preferred_element_type=F32)

    @pl.when(j < n_plain_tiles)
    def _():
        ob_ref[...] = r.astype(ob_ref.dtype)

    @pl.when(jnp.logical_and(j >= n_plain_tiles, j < n_bf16_tiles))
    def _():
        ob_ref[...] = _sigmoid(r).astype(ob_ref.dtype)

    @pl.when(j >= n_bf16_tiles)
    def _():
        of_ref[...] = r


def in_projection(x, gain, w, n_plain, n_bf16, *, tm=1024):
    m, k = x.shape
    tn = IN_PROJ_TILE
    nb = n_bf16 // tn
    nf = (w.shape[1] - n_bf16) // tn
    return pl.pallas_call(
        functools.partial(_in_projection_kernel, n_plain_tiles=n_plain // tn, n_bf16_tiles=nb),
        out_shape=(jax.ShapeDtypeStruct((m, nb * tn), BF16), jax.ShapeDtypeStruct((m, nf * tn), F32)),
        grid=(m // tm, nb + nf),
        in_specs=[
            pl.BlockSpec((tm, k), lambda i, j: (i, 0)),
            pl.BlockSpec((1, k), lambda i, j: (0, 0)),
            pl.BlockSpec((k, tn), lambda i, j: (0, j)),
        ],
        out_specs=(pl.BlockSpec((tm, tn), lambda i, j: (i, jnp.minimum(j, nb - 1))),
                   pl.BlockSpec((tm, tn), lambda i, j: (i, jnp.maximum(j - nb, 0)))),
        scratch_shapes=[pltpu.VMEM((tm, k), BF16)],
        compiler_params=_params("parallel", "arbitrary"),
        name="in_projection",
    )(x, gain.reshape(1, k), w)


def _rms_swiglu_kernel(x_ref, g_ref, wg_ref, wu_ref, o_ref, h_ref):
    @pl.when(pl.program_id(1) == 0)
    def _():
        h_ref[...] = _rms_norm_rows(x_ref[...], g_ref[...]).astype(BF16)

    h = h_ref[...]
    a = jnp.dot(h, wg_ref[...], preferred_element_type=F32)
    b = jnp.dot(h, wu_ref[...], preferred_element_type=F32)
    o_ref[...] = (a * _sigmoid(a) * b).astype(o_ref.dtype)


def rms_swiglu_up(x, gain, wg, wu, *, tm, tn):
    m, k = x.shape
    n = wg.shape[1]
    return pl.pallas_call(
        _rms_swiglu_kernel,
        out_shape=jax.ShapeDtypeStruct((m, n), BF16),
        grid=(m // tm, n // tn),
        in_specs=[
            pl.BlockSpec((tm, k), lambda i, j: (i, 0)),
            pl.BlockSpec((1, k), lambda i, j: (0, 0)),
            pl.BlockSpec((k, tn), lambda i, j: (0, j)),
            pl.BlockSpec((k, tn), lambda i, j: (0, j)),
        ],
        out_specs=pl.BlockSpec((tm, tn), lambda i, j: (i, j)),
        scratch_shapes=[pltpu.VMEM((tm, k), BF16)],
        compiler_params=_params("parallel", "arbitrary"),
        name="rms_swiglu_up",
    )(x, gain.reshape(1, k), wg, wu)


def _matmul_residual_kernel(a_ref, w_ref, r_ref, o_ref):
    o_ref[...] = r_ref[...] + jnp.dot(a_ref[...], w_ref[...], preferred_element_type=F32)


def matmul_residual(a, w, res, *, tm, tn):
    m, k = a.shape
    n = w.shape[1]
    return pl.pallas_call(
        _matmul_residual_kernel,
        out_shape=jax.ShapeDtypeStruct((m, n), F32),
        grid=(m // tm, n // tn),
        in_specs=[
            pl.BlockSpec((tm, k), lambda i, j: (i, 0)),
            pl.BlockSpec((k, tn), lambda i, j: (0, j)),
            pl.BlockSpec((tm, tn), lambda i, j: (i, j)),
        ],
        out_specs=pl.BlockSpec((tm, tn), lambda i, j: (i, j)),
        compiler_params=_params("parallel", "arbitrary"),
        name="matmul_residual",
    )(a, w, res)


class RawBranch:
    def __init__(self, o_fwd, o_bwd, og_name, gain, silu_gate):
        self.o_fwd, self.o_bwd, self.og_name, self.gain, self.silu_gate = o_fwd, o_bwd, og_name, gain, silu_gate


def _merge_kernel(*refs, raw):
    pos = 0
    br = []
    for kind in raw:
        width = 1 if kind is None else 4
        br.append(refs[pos:pos + width])
        pos += width
    gl_refs = refs[pos:pos + N_BRANCH]
    wb_ref, o_ref, fin_ref = refs[pos + N_BRANCH:pos + N_BRANCH + 3]
    raw_slot = {n: s for s, n in enumerate(n for n, kind in enumerate(raw) if kind is not None)}

    @pl.when(pl.program_id(1) == 0)
    def _():
        for n, slot in raw_slot.items():
            of_ref, ob_ref, og_ref, gain_ref = br[n]
            for h in range(BRANCH_WIDTH // LANES):
                cols = slice(h * LANES, (h + 1) * LANES)
                y = _rms_norm_rows(of_ref[:, cols] + ob_ref[:, cols], gain_ref[...])
                g = og_ref[:, cols].astype(F32)
                gate = _sigmoid(g)
                if raw[n]:
                    gate = g * gate
                fin_ref[slot, :, cols] = (y * gate).astype(fin_ref.dtype)

    acc = None
    for n in range(N_BRANCH):
        b = br[n][0][...] if raw[n] is None else fin_ref[raw_slot[n]]
        y = jnp.dot(b, wb_ref[n], preferred_element_type=F32)
        t = gl_refs[n][...].astype(F32) * y
        acc = t if acc is None else acc + t
    o_ref[...] = acc.astype(o_ref.dtype)


def merge_branches(branches, pb, w_branch, *, tm, tn):
    m = pb.shape[0]
    d = D_MODEL
    tiles_per_branch = d // tn
    tile0 = PB_COL["gates"] // tn
    row_block = pl.BlockSpec((tm, BRANCH_WIDTH), lambda i, j: (i, 0))
    in_specs, operands, raw = [], [], []
    for b in branches:
        if isinstance(b, RawBranch):
            og_col = PB_COL[b.og_name] // BRANCH_WIDTH
            in_specs += [row_block, row_block,
                         pl.BlockSpec((tm, BRANCH_WIDTH), functools.partial(lambda i, j, c: (i, c), c=og_col)),
                         pl.BlockSpec((1, LANES), lambda i, j: (0, 0))]
            operands += [b.o_fwd, b.o_bwd, pb, b.gain.astype(F32).reshape(1, LANES)]
            raw.append(b.silu_gate)
        else:
            in_specs.append(row_block)
            operands.append(b)
            raw.append(None)
    in_specs += [
        pl.BlockSpec((tm, tn), functools.partial(lambda i, j, n: (i, tile0 + n * tiles_per_branch + j), n=n))
        for n in range(N_BRANCH)
    ]
    in_specs += [pl.BlockSpec((N_BRANCH, BRANCH_WIDTH, tn), lambda i, j: (0, 0, j))]
    n_raw = sum(kind is not None for kind in raw)
    return pl.pallas_call(
        functools.partial(_merge_kernel, raw=tuple(raw)),
        out_shape=jax.ShapeDtypeStruct((m, d), BF16),
        grid=(m // tm, d // tn),
        in_specs=in_specs,
        out_specs=pl.BlockSpec((tm, tn), lambda i, j: (i, j)),
        scratch_shapes=[pltpu.VMEM((max(n_raw, 1), tm, BRANCH_WIDTH), BF16)],
        compiler_params=_params("parallel", "arbitrary"),
        name="merge_branches",
    )(*operands, *([pb] * N_BRANCH), w_branch)


def _router_kernel(x_ref, g_ref, w_ref, b_ref, o_ref, h_ref, cnt_ref, run_ref, *, n_experts):
    @pl.when(pl.program_id(0) == 0)
    def _():
        run_ref[...] = jnp.zeros_like(run_ref)

    h = _rms_norm_rows(x_ref[...], g_ref[...])
    h_ref[...] = h.astype(h_ref.dtype)
    logits = _dot_f32(h, w_ref[...]) + b_ref[...]
    lane = lax.broadcasted_iota(jnp.int32, logits.shape, 1).astype(F32)
    neg = -jnp.inf
    lm = jnp.where(lane < n_experts, logits, neg)
    m1 = jnp.max(lm, axis=-1, keepdims=True)
    i1 = jnp.min(jnp.where(lm == m1, lane, float(LANES)), axis=-1, keepdims=True)
    lm2 = jnp.where(lane == i1, neg, lm)
    m2 = jnp.max(lm2, axis=-1, keepdims=True)
    i2 = jnp.min(jnp.where(lm2 == m2, lane, float(LANES)), axis=-1, keepdims=True)
    t = jnp.exp(m2 - m1)
    den = 1.0 + t

    tm = logits.shape[0]
    before = (lax.broadcasted_iota(jnp.int32, (tm, tm), 1)
              < lax.broadcasted_iota(jnp.int32, (tm, tm), 0))
    before = jnp.where(before, 1.0, 0.0).astype(BF16)
    pick1 = lane == i1
    pick2 = lane == i2
    oh1 = jnp.where(pick1, 1.0, 0.0)
    oh2 = jnp.where(pick2, 1.0, 0.0)
    pre1 = jnp.dot(before, oh1.astype(BF16), preferred_element_type=F32)
    pre2 = jnp.dot(before, oh2.astype(BF16), preferred_element_type=F32)
    tot1 = jnp.sum(oh1, axis=0, keepdims=True)
    tot2 = jnp.sum(oh2, axis=0, keepdims=True)
    run = run_ref[...]
    rank1 = jnp.sum(jnp.where(pick1, pre1 + run, 0.0), axis=-1, keepdims=True)
    rank2 = jnp.sum(jnp.where(pick2, pre2 + (run + tot1), 0.0), axis=-1, keepdims=True)
    run = run + tot1 + tot2
    run_ref[...] = run
    cnt_ref[...] = jnp.broadcast_to(run, cnt_ref.shape)

    out = jnp.where(lane == 0, 1.0 / den, jnp.where(lane == 1, t / den, jnp.where(lane == 2, i1, i2)))
    out = jnp.where(lane == 4, rank1, jnp.where(lane == 5, rank2, out))
    o_ref[...] = jnp.where(lane < 6, out, 0.0)


def router_top2(x, gain, w_router, b_router, *, tm=512):
    m, k = x.shape
    e = w_router.shape[1]
    w_pad = jnp.zeros((k, LANES), F32).at[:, :e].set(w_router.astype(F32))
    b_pad = jnp.zeros((1, LANES), F32).at[0, :e].set(b_router.astype(F32))
    route, h, cnt = pl.pallas_call(
        functools.partial(_router_kernel, n_experts=e),
        out_shape=(jax.ShapeDtypeStruct((m, LANES), F32), jax.ShapeDtypeStruct((m, k), BF16),
                   jax.ShapeDtypeStruct((8, LANES), F32)),
        grid=(m // tm,),
        in_specs=[
            pl.BlockSpec((tm, k), lambda i: (i, 0)),
            pl.BlockSpec((1, k), lambda i: (0, 0)),
            pl.BlockSpec((k, LANES), lambda i: (0, 0)),
            pl.BlockSpec((1, LANES), lambda i: (0, 0)),
        ],
        out_specs=(pl.BlockSpec((tm, LANES), lambda i: (i, 0)), pl.BlockSpec((tm, k), lambda i: (i, 0)),
                   pl.BlockSpec((8, LANES), lambda i: (0, 0))),
        scratch_shapes=[pltpu.VMEM((1, LANES), F32)],
        compiler_params=_params("arbitrary"),
        name="router_top2",
    )(x, gain.reshape(1, k), w_pad, b_pad)
    return route, h, cnt[0, :e].astype(jnp.int32)


MOE_TILE = 1024
MOE_SUB = 256
MOE_FF_TILE = 512


def _moe_kernel(tile_e_ref, tile_rows_ref, n_used_ref, x_ref, wg_ref, wu_ref, wd_ref, o_ref, acc_ref):
    i = pl.program_id(0)
    j = pl.program_id(1)
    last = pl.num_programs(1) - 1
    valid = tile_rows_ref[i]
    n_sub = (valid + (MOE_SUB - 1)) // MOE_SUB

    for k in range(1, MOE_TILE // MOE_SUB + 1):
        rows = slice(0, k * MOE_SUB)

        @pl.when(n_sub == k)
        def _(rows=rows):
            x = x_ref[rows, :]
            a = jnp.dot(x, wg_ref[0].astype(BF16), preferred_element_type=F32)
            b = jnp.dot(x, wu_ref[0].astype(BF16), preferred_element_type=F32)
            act = (a * _sigmoid(a) * b).astype(BF16)
            part = jnp.dot(act, wd_ref[0].astype(BF16), preferred_element_type=F32)

            @pl.when(j == 0)
            def _():
                acc_ref[rows, :] = part

            @pl.when(j > 0)
            def _():
                acc_ref[rows, :] += part

    for s in range(0, MOE_TILE, MOE_SUB):
        rows = slice(s, s + MOE_SUB)
        filled = s < valid

        @pl.when(jnp.logical_and(filled, j == last))
        def _(rows=rows):
            o_ref[rows, :] = acc_ref[rows, :].astype(o_ref.dtype)

        @pl.when(jnp.logical_and(jnp.logical_not(filled), j == last))
        def _(rows=rows):
            o_ref[rows, :] = jnp.zeros((MOE_SUB, o_ref.shape[1]), o_ref.dtype)


def moe_experts(xb, tile_e, tile_rows, n_used, wg, wu, wd):
    rows, d = xb.shape
    ff = wg.shape[2]
    tm, tf = MOE_TILE, MOE_FF_TILE
    n_tiles = rows // tm
    last_j = ff // tf - 1

    def x_map(i, j, te, tr, nu):
        return (jnp.minimum(i, nu[0] - 1), 0)

    def up_map(i, j, te, tr, nu):
        return (te[i], 0, jnp.where(i < nu[0], j, last_j))

    def down_map(i, j, te, tr, nu):
        return (te[i], jnp.where(i < nu[0], j, last_j), 0)

    grid_spec = pltpu.PrefetchScalarGridSpec(
        num_scalar_prefetch=3,
        grid=(n_tiles, ff // tf),
        in_specs=[
            pl.BlockSpec((tm, d), x_map, pipeline_mode=pl.Buffered(1)),
            pl.BlockSpec((1, d, tf), up_map),
            pl.BlockSpec((1, d, tf), up_map),
            pl.BlockSpec((1, tf, d), down_map),
        ],
        out_specs=pl.BlockSpec((tm, d), lambda i, j, te, tr, nu: (i, 0)),
        scratch_shapes=[pltpu.VMEM((tm, d), F32)],
    )
    return pl.pallas_call(
        _moe_kernel,
        out_shape=jax.ShapeDtypeStruct((rows, d), BF16),
        grid_spec=grid_spec,
        compiler_params=_params("arbitrary", "arbitrary"),
        name="moe_experts",
    )(tile_e, tile_rows, n_used, xb, wg, wu, wd)


def _moe_combine_kernel(x_ref, y0_ref, y1_ref, r_ref, o_ref):
    w = r_ref[...]
    o_ref[...] = x_ref[...] + w[:, 0:1] * y0_ref[...].astype(F32) + w[:, 1:2] * y1_ref[...].astype(F32)


def moe_combine(x2d, y0, y1, route, *, tm=512):
    n, d = x2d.shape
    row_block = pl.BlockSpec((tm, d), lambda i: (i, 0))
    return pl.pallas_call(
        _moe_combine_kernel,
        out_shape=jax.ShapeDtypeStruct((n, d), F32),
        grid=(n // tm,),
        in_specs=[row_block, row_block, row_block, pl.BlockSpec((tm, LANES), lambda i: (i, 0))],
        out_specs=row_block,
        compiler_params=_params("parallel"),
        name="moe_combine",
    )(x2d, y0, y1, route)


def moe_layer(x2d, gain, w_router, b_router, wg, wu, wd):
    n, d = x2d.shape
    e = N_EXPERTS
    route, h, counts = router_top2(x2d, gain, w_router, b_router)
    nk = n * MOE_TOP_K
    n_tiles = -(-nk // MOE_TILE) + e
    flat_e = route[:, 2:2 + MOE_TOP_K].astype(jnp.int32).reshape(nk)
    rank = route[:, 4:4 + MOE_TOP_K].astype(jnp.int32).reshape(nk)
    flat_tok = jnp.repeat(jnp.arange(n, dtype=jnp.int32), MOE_TOP_K)
    padded = (counts + MOE_TILE - 1) // MOE_TILE * MOE_TILE
    pad_end = jnp.cumsum(padded)
    pad_start = pad_end - padded
    slot = (pad_start[flat_e] + rank).astype(jnp.int32)
    n_slots = n_tiles * MOE_TILE
    slot_tok = (jnp.arange(n_slots, dtype=jnp.int32) % n).at[slot].set(flat_tok)
    tile_start = jnp.arange(n_tiles, dtype=jnp.int32) * MOE_TILE
    tile_e = jnp.minimum(jnp.searchsorted(pad_end, tile_start, side="right"), e - 1).astype(jnp.int32)
    tile_rows = jnp.clip(pad_start[tile_e] + counts[tile_e] - tile_start, 0, MOE_TILE).astype(jnp.int32)
    tile_rows = jnp.where(tile_start < pad_end[-1], tile_rows, 0)
    n_used = (pad_end[-1] // MOE_TILE).astype(jnp.int32).reshape(1)
    tile_e = jnp.where(tile_start < pad_end[-1], tile_e, tile_e[jnp.maximum(n_used[0] - 1, 0)])

    xb = h[slot_tok]
    yb = moe_experts(xb, tile_e, tile_rows, n_used, wg, wu, wd)
    slot2 = slot.reshape(n, MOE_TOP_K)
    return moe_combine(x2d, yb[slot2[:, 0]], yb[slot2[:, 1]], route)


_SRC = dict(zip(
    ("na_q", "na_k", "na_v", "gla_q", "gla_k", "gla_v", "gla_lr", "gla_og", "gdn_qkv", "gdn_a", "gdn_b",
     "gdn_og", "hg_q", "hg_f", "hg_i", "hg_og", "mem_q", "gates"),
    zip(np.cumsum((0,) + IN_WIDTHS[:-1]).tolist(), IN_WIDTHS)))
_PB_ORDER = ("na_q", "na_k", "na_v", "gla_q", "gla_k", "gla_v", "gla_og", "gdn_qkv", "gdn_og", "hg_q", "hg_i",
             "hg_og", "mem_q", "gates")
_PF_ORDER = ("hg_f", "gla_lr", "gdn_a", "gdn_b")
PB_COL = {}
_c = 0
for _name in _PB_ORDER:
    PB_COL[_name] = _c
    _c += _SRC[_name][1]
PB_WIDTH = _c
PF_COL = {}
_c = 0
for _name in _PF_ORDER:
    PF_COL[_name] = _c
    _c += _SRC[_name][1]
PF_WIDTH = -(-_c // IN_PROJ_TILE) * IN_PROJ_TILE
PF_SMALL_COL = PF_COL["gla_lr"]
GDN_A_LANE = PF_COL["gdn_a"] - PF_SMALL_COL
GDN_B_LANE = PF_COL["gdn_b"] - PF_SMALL_COL


def _rearrange_w_in(w):
    w = w.astype(BF16)
    cols = [w[:, _SRC[n][0]:_SRC[n][0] + _SRC[n][1]] for n in _PB_ORDER + _PF_ORDER]
    cols.append(jnp.zeros((w.shape[0], PB_WIDTH + PF_WIDTH - P_IN), BF16))
    return jnp.concatenate(cols, axis=1)


def _segment_rms(x, gain, seg_ones, seg_width):
    sq = x * x
    hi = sq.astype(BF16)
    lo = (sq - hi.astype(F32)).astype(BF16)
    ss = (jnp.dot(hi, seg_ones, preferred_element_type=F32)
          + jnp.dot(lo, seg_ones, preferred_element_type=F32))
    return x * lax.rsqrt(ss * (1.0 / seg_width) + RMS_EPS) * gain


NA_ROWS_PER_STEP = 8
NA_BAND = NA_WIN_ROWS * GRID_W


def _na_bias_table(rel_bias):
    c = np.arange(GRID_W)
    dc = np.clip(c[None, :] - c[:, None], 1 - NA_WIN_COLS, NA_WIN_COLS - 1) + (NA_WIN_COLS - 1)
    col_start = np.clip(c - NA_WIN_COLS // 2, 0, GRID_W - NA_WIN_COLS)
    col_in = (c[None, :] >= col_start[:, None]) & (c[None, :] < col_start[:, None] + NA_WIN_COLS)
    onehot = (dc[None] == np.arange(2 * NA_WIN_COLS - 1)[:, None, None]).astype(np.float32)
    base = jnp.einsum("hrc,cqk->hrqk", rel_bias.astype(F32), onehot, precision=lax.Precision.HIGHEST)
    base = jnp.where(col_in[None, None], base, MASK_VALUE)
    tables = []
    for cfg in range(NA_WIN_ROWS):
        rows = base[:, NA_WIN_ROWS - 1 - cfg:2 * NA_WIN_ROWS - 1 - cfg]
        tables.append(rows.transpose(0, 2, 1, 3).reshape(NA_HEADS // 2, 2 * GRID_W, NA_BAND))
    return jnp.stack(tables)


def _na_kernel(q_ref, k_ref, v_ref, qg_ref, kg_ref, seg_ref, bias_ref, o_ref, kn_ref):
    step = pl.program_id(1)
    rows_total = k_ref.shape[0] // GRID_W
    seg = seg_ref[...]

    @pl.when(step == 0)
    def _():
        def norm_keys(t, carry):
            rows = pl.ds(pl.multiple_of(t * 256, 256), 256)
            kn_ref[rows, :] = _segment_rms(k_ref[rows, :].astype(F32), kg_ref[...], seg, NA_HEAD_DIM).astype(BF16)
            return carry
        lax.fori_loop(0, k_ref.shape[0] // 256, norm_keys, 0)

    lane = lax.broadcasted_iota(jnp.int32, (1, LANES), 1)
    low_half = lane < NA_HEAD_DIM

    def one_row(rr, carry):
        r = step * NA_ROWS_PER_STEP + rr
        row_start = jnp.clip(r - NA_WIN_ROWS // 2, 0, rows_total - NA_WIN_ROWS)
        cfg = r - row_start
        qrows = pl.ds(pl.multiple_of(rr * GRID_W, GRID_W), GRID_W)
        band = pl.ds(pl.multiple_of(row_start * GRID_W, GRID_W), NA_BAND)
        qn = (_segment_rms(q_ref[qrows, :].astype(F32), qg_ref[...], seg, NA_HEAD_DIM)
              * (NA_HEAD_DIM ** -0.5)).astype(BF16)
        pairs = range(NA_HEADS // 2)
        pair_cols = [slice(pair * LANES, (pair + 1) * LANES) for pair in pairs]
        scores = []
        for cols in pair_cols:
            qp = qn[:, cols]
            q2 = jnp.concatenate([jnp.where(low_half, qp, jnp.zeros_like(qp)),
                                  jnp.where(low_half, jnp.zeros_like(qp), qp)], axis=0)
            scores.append(lax.dot_general(q2, kn_ref[band, cols], (((1,), (1,)), ((), ())),
                                          preferred_element_type=F32))
        exps, sums = [], []
        for s, pair in zip(scores, pairs):
            s = s + bias_ref[cfg, pair]
            e = jnp.exp(s - jnp.max(s, axis=-1, keepdims=True))
            sums.append(jnp.sum(e, axis=-1, keepdims=True))
            exps.append(e.astype(BF16))
        outs = [jnp.dot(e, v_ref[band, cols], preferred_element_type=F32) / l
                for e, l, cols in zip(exps, sums, pair_cols)]
        for o2, cols in zip(outs, pair_cols):
            o_ref[qrows, cols] = jnp.where(low_half, o2[:GRID_W], o2[GRID_W:]).astype(o_ref.dtype)
        return carry

    lax.fori_loop(0, NA_ROWS_PER_STEP, one_row, 0)


def neighbourhood_attention(pb, batch, q_gain, k_gain, rel_bias):
    m = pb.shape[0]
    s = m // batch
    tq = NA_ROWS_PER_STEP * GRID_W
    steps = s // tq
    qg = jnp.tile(q_gain.astype(F32), NA_HEADS).reshape(1, NA_WIDTH)
    kg = jnp.tile(k_gain.astype(F32), NA_HEADS).reshape(1, NA_WIDTH)
    seg = jnp.asarray(np.kron(np.eye(NA_HEADS), np.ones((NA_HEAD_DIM, NA_HEAD_DIM))), BF16)
    bias = _na_bias_table(rel_bias)
    cq, ck, cv = (PB_COL[n] // NA_WIDTH for n in ("na_q", "na_k", "na_v"))
    return pl.pallas_call(
        _na_kernel,
        out_shape=jax.ShapeDtypeStruct((m, NA_WIDTH), BF16),
        grid=(batch, steps),
        in_specs=[
            pl.BlockSpec((tq, NA_WIDTH), lambda b, t: (b * steps + t, cq)),
            pl.BlockSpec((s, NA_WIDTH), lambda b, t: (b, ck)),
            pl.BlockSpec((s, NA_WIDTH), lambda b, t: (b, cv)),
            pl.BlockSpec((1, NA_WIDTH), lambda b, t: (0, 0)),
            pl.BlockSpec((1, NA_WIDTH), lambda b, t: (0, 0)),
            pl.BlockSpec((NA_WIDTH, NA_WIDTH), lambda b, t: (0, 0)),
            pl.BlockSpec((NA_WIN_ROWS, NA_HEADS // 2, 2 * GRID_W, NA_BAND), lambda b, t: (0, 0, 0, 0)),
        ],
        out_specs=pl.BlockSpec((tq, NA_WIDTH), lambda b, t: (b * steps + t, 0)),
        scratch_shapes=[pltpu.VMEM((s, NA_WIDTH), BF16)],
        compiler_params=_params("parallel", "arbitrary"),
        name="neighbourhood_attention",
    )(pb, pb, pb, qg, kg, seg, bias)


def _mem_attn_kernel(q_ref, kv_ref, qg_ref, kg_ref, o_ref, kn_ref):
    @pl.when(pl.program_id(1) == 0)
    def _():
        for h in range(MEM_HEADS):
            cols = slice(h * MEM_HEAD_DIM, (h + 1) * MEM_HEAD_DIM)
            kn_ref[:, cols] = _rms_norm_rows(kv_ref[:, cols].astype(F32), kg_ref[...]).astype(BF16)

    head_cols = [slice(h * MEM_HEAD_DIM, (h + 1) * MEM_HEAD_DIM) for h in range(MEM_HEADS)]
    qns = [_rms_norm_rows(q_ref[:, cols].astype(F32), qg_ref[...]).astype(BF16) for cols in head_cols]
    scores = [lax.dot_general(qn, kn_ref[:, cols], (((1,), (1,)), ((), ())), preferred_element_type=F32)
              for qn, cols in zip(qns, head_cols)]
    exps, sums = [], []
    for s in scores:
        s = s * (MEM_HEAD_DIM ** -0.5)
        e = jnp.exp(s - jnp.max(s, axis=-1, keepdims=True))
        sums.append(jnp.sum(e, axis=-1, keepdims=True))
        exps.append(e.astype(BF16))
    outs = [jnp.dot(e, kv_ref[:, MEM_WIDTH + cols.start:MEM_WIDTH + cols.stop], preferred_element_type=F32)
            for e, cols in zip(exps, head_cols)]
    for o, l, cols in zip(outs, sums, head_cols):
        o_ref[:, cols] = (o / l).astype(o_ref.dtype)


def memory_cross_attention(pb, kv, batch, q_gain, k_gain, *, tq=512):
    m = pb.shape[0]
    steps = m // batch // tq
    n_mem = kv.shape[0] // batch
    cq = PB_COL["mem_q"] // MEM_WIDTH
    return pl.pallas_call(
        _mem_attn_kernel,
        out_shape=jax.ShapeDtypeStruct((m, MEM_WIDTH), BF16),
        grid=(batch, steps),
        in_specs=[
            pl.BlockSpec((tq, MEM_WIDTH), lambda b, t: (b * steps + t, cq)),
            pl.BlockSpec((n_mem, 2 * MEM_WIDTH), lambda b, t: (b, 0)),
            pl.BlockSpec((1, MEM_HEAD_DIM), lambda b, t: (0, 0)),
            pl.BlockSpec((1, MEM_HEAD_DIM), lambda b, t: (0, 0)),
        ],
        out_specs=pl.BlockSpec((tq, MEM_WIDTH), lambda b, t: (b * steps + t, 0)),
        scratch_shapes=[pltpu.VMEM((n_mem, MEM_WIDTH), BF16)],
        compiler_params=_params("parallel", "arbitrary"),
        name="memory_cross_attention",
    )(pb, kv, q_gain.astype(F32).reshape(1, MEM_HEAD_DIM), k_gain.astype(F32).reshape(1, MEM_HEAD_DIM))


LIN_BLOCK = 512
HEAD_V = 128


def _log1p_exp_neg(t):
    return jnp.log(1.0 + jnp.exp(-t))


def _log_sigmoid(x):
    return jnp.minimum(x, 0.0) - _log1p_exp_neg(jnp.abs(x))


def _logaddexp(a, b):
    return jnp.maximum(a, b) + _log1p_exp_neg(jnp.abs(a - b))


def _split_bf16(x, terms):
    parts = []
    for _ in range(terms):
        p = x.astype(BF16)
        parts.append(p)
        x = x - p.astype(F32)
    return parts


def _dot_f32(a, b):
    a_hi, a_lo = _split_bf16(a, 2)
    b_hi, b_lo = _split_bf16(b, 2)
    return (jnp.dot(a_hi, b_hi, preferred_element_type=F32)
            + (jnp.dot(a_hi, b_lo, preferred_element_type=F32) + jnp.dot(a_lo, b_hi, preferred_element_type=F32)))


def _cumsum_rows(mask, x):
    m = jnp.where(mask, 1.0, 0.0).astype(BF16)
    hi, mid, lo = _split_bf16(x, 3)
    return (jnp.dot(m, hi, preferred_element_type=F32)
            + (jnp.dot(m, mid, preferred_element_type=F32) + jnp.dot(m, lo, preferred_element_type=F32)))


def _gla_inputs(refs, rows, direction, params):
    q_ref, k_ref, v_ref, g_ref = refs
    wpad_ref, bias_ref = params
    qc = q_ref[rows, :].astype(F32) * (GLA_HEAD_K ** -0.5)
    kc = k_ref[rows, :].astype(F32)
    gk = _dot_f32(g_ref[rows, :], wpad_ref[direction]) + bias_ref[direction]
    lg = _log_sigmoid(gk) * (1.0 / GLA_GATE_NORMALIZER)
    return qc, kc, v_ref[rows, :], lg


def _hgrn_inputs(refs, rows, direction, params):
    q_ref, v_ref, z_ref = refs
    lb_ref, log_lb_ref, log1m_lb_ref = params
    qr = q_ref[rows, :].astype(F32)
    qc = qr * _sigmoid(qr)
    z = z_ref[rows, :]
    lg = _logaddexp(log_lb_ref[direction], log1m_lb_ref[direction] + _log_sigmoid(z))
    kc = (1.0 - lb_ref[direction]) * _sigmoid(-z)
    return qc, kc, v_ref[rows, :], lg


GDN_CONV_WIDTH = 5
GDN_QKV_WIDTH = 2 * GDN_KEY_WIDTH + GDN_VAL_WIDTH
GDN_HALO = 16


def _gdn_prep_kernel(prev_ref, cur_ref, next_ref, w_ref, o_ref, xp_ref, *, blocks_per_seq):
    i = pl.program_id(0)
    t = cur_ref.shape[0]
    pos = i % blocks_per_seq
    prev = prev_ref[...].astype(F32)
    nxt = next_ref[...].astype(F32)
    xp_ref[0:GDN_HALO, :] = jnp.where(pos == 0, jnp.zeros_like(prev), prev)
    xp_ref[GDN_HALO:GDN_HALO + t, :] = cur_ref[...].astype(F32)
    xp_ref[GDN_HALO + t:, :] = jnp.where(pos == blocks_per_seq - 1, jnp.zeros_like(nxt), nxt)
    half = GDN_CONV_WIDTH // 2
    for g in range(GDN_QKV_WIDTH // LANES):
        cols = slice(g * LANES, (g + 1) * LANES)
        acc = None
        for j in range(GDN_CONV_WIDTH):
            term = xp_ref[GDN_HALO - half + j:GDN_HALO - half + j + t, cols] * w_ref[j:j + 1, cols]
            acc = term if acc is None else acc + term
        y = acc * _sigmoid(acc)
        if g < 2 * GDN_HEADS:
            y = y * lax.rsqrt(jnp.sum(y * y, axis=-1, keepdims=True) + 1e-6)
            if g < GDN_HEADS:
                y = y * (GDN_HEAD_K ** -0.5)
        o_ref[:, cols] = y.astype(o_ref.dtype)


def gdn_prep(pb, batch, conv_w, *, t=512):
    m = pb.shape[0]
    blocks_per_seq = m // batch // t
    halo_per_block = t // GDN_HALO
    col = PB_COL["gdn_qkv"] // GDN_QKV_WIDTH
    last_halo = m // GDN_HALO - 1
    return pl.pallas_call(
        functools.partial(_gdn_prep_kernel, blocks_per_seq=blocks_per_seq),
        out_shape=jax.ShapeDtypeStruct((m, GDN_QKV_WIDTH), BF16),
        grid=(m // t,),
        in_specs=[
            pl.BlockSpec((GDN_HALO, GDN_QKV_WIDTH), lambda i: (jnp.maximum(i * halo_per_block - 1, 0), col)),
            pl.BlockSpec((t, GDN_QKV_WIDTH), lambda i: (i, col)),
            pl.BlockSpec((GDN_HALO, GDN_QKV_WIDTH),
                         lambda i: (jnp.minimum((i + 1) * halo_per_block, last_halo), col)),
            pl.BlockSpec((GDN_CONV_WIDTH, GDN_QKV_WIDTH), lambda i: (0, 0)),
        ],
        out_specs=pl.BlockSpec((t, GDN_QKV_WIDTH), lambda i: (i, 0)),
        scratch_shapes=[pltpu.VMEM((t + 2 * GDN_HALO, GDN_QKV_WIDTH), F32)],
        compiler_params=_params("parallel"),
        name="gdn_prep",
    )(pb, pb, pb, conv_w.astype(F32))


def _softplus(x):
    return jnp.maximum(x, 0.0) + _log1p_exp_neg(jnp.abs(x))


GDN_PACK = GDN_HEADS * GDN_CHUNK
GDN_WY_BLOCK = 512


def _stack_heads(x, width):
    heads = x.shape[1] // width
    lane = lax.broadcasted_iota(jnp.int32, (1, x.shape[1]), 1)
    return jnp.concatenate(
        [jnp.where((lane >= h * width) & (lane < (h + 1) * width), x, 0.0).astype(BF16) for h in range(heads)],
        axis=0)


def _packed_mm(x, y):
    return jnp.dot(x.astype(BF16), _stack_heads(y, GDN_CHUNK), preferred_element_type=F32)


def _packed_inverses(mats):
    c = GDN_CHUNK
    ii = lax.broadcasted_iota(jnp.int32, (c, GDN_PACK), 0)
    jj = lax.broadcasted_iota(jnp.int32, (c, GDN_PACK), 1) % c
    eye = (ii == jj).astype(F32)

    def same_block(s):
        return (ii // s) == (jj // s)

    ds = [jnp.where(same_block(8), a, 0.0) for a in mats]
    d2s = [_packed_mm(d, d) for d in ds]
    d4s = [_packed_mm(d2, d2) for d2 in d2s]
    ts = [_packed_mm(eye - d, eye + d2) for d, d2 in zip(ds, d2s)]
    ts = [_packed_mm(t, eye + d4) for t, d4 in zip(ts, d4s)]
    s = 8
    while s < c:
        off = same_block(2 * s) & jnp.logical_not(same_block(s))
        ets = [_packed_mm(jnp.where(off, a, 0.0), t) for a, t in zip(mats, ts)]
        ts = [t - _packed_mm(t, et) for t, et in zip(ts, ets)]
        s *= 2
    return ts


def _gdn_wy_kernel(qkv_ref, small_ref, a_ref, dtb_ref, selg_ref, selk_ref, selb_ref, *out_refs):
    c = GDN_CHUNK
    n_chunks = qkv_ref.shape[0] // c
    ii = lax.broadcasted_iota(jnp.int32, (c, c), 0)
    jj = lax.broadcasted_iota(jnp.int32, (c, c), 1)
    pi = lax.broadcasted_iota(jnp.int32, (c, GDN_PACK), 0)
    pj = lax.broadcasted_iota(jnp.int32, (c, GDN_PACK), 1) % c
    eye_p = (pi == pj).astype(F32)
    ones_cc = jnp.ones((c, c), BF16)

    problems = [(ch, d) for ch in range(n_chunks) for d in range(2)]
    chunk_in = []
    for ch in range(n_chunks):
        rows = slice(ch * c, (ch + 1) * c)
        qkv = qkv_ref[rows, :]
        small = small_ref[rows, :]
        kf = qkv[:, GDN_KEY_WIDTH:2 * GDN_KEY_WIDTH].astype(F32)
        chunk_in.append(dict(
            qf=qkv[:, :GDN_KEY_WIDTH].astype(F32), kf=kf, vf=qkv[:, 2 * GDN_KEY_WIDTH:].astype(F32),
            kbd=_stack_heads(kf, HEAD_V),
            log_alpha=a_ref[...] * _softplus(small + dtb_ref[...]),
            beta_all=_sigmoid(small)))

    def sel3(x, sel):
        hi, mid, lo = _split_bf16(x, 3)
        return (jnp.dot(hi, sel, preferred_element_type=F32)
                + (jnp.dot(mid, sel, preferred_element_type=F32) + jnp.dot(lo, sel, preferred_element_type=F32)))

    g_all = [_cumsum_rows((jj >= ii) if d else (jj <= ii), chunk_in[ch]["log_alpha"]) for ch, d in problems]
    g_pack = [sel3(g, selg_ref[d]) for g, (ch, d) in zip(g_all, problems)]
    g_wide = [sel3(g, selk_ref[d]) for g, (ch, d) in zip(g_all, problems)]
    beta_w = [sel3(chunk_in[ch]["beta_all"], selb_ref[d]) for ch, d in problems]
    g_rowp = []
    for gp in g_pack:
        hi, mid, lo = _split_bf16(gp * eye_p, 3)
        g_rowp.append(jnp.dot(ones_cc, hi, preferred_element_type=F32)
                      + (jnp.dot(ones_cc, mid, preferred_element_type=F32)
                         + jnp.dot(ones_cc, lo, preferred_element_type=F32)))
    decays, k_betas = [], []
    for gp, gr, bw, (ch, d) in zip(g_pack, g_rowp, beta_w, problems):
        incl = (pj >= pi) if d else (pj <= pi)
        decays.append(jnp.where(incl, jnp.exp(jnp.where(incl, gp - gr, 0.0)), 0.0))
        k_betas.append(chunk_in[ch]["kf"] * bw)
    kq = [lax.dot_general(jnp.concatenate([kb, chunk_in[ch]["qf"]], axis=0).astype(BF16), chunk_in[ch]["kbd"],
                          (((1,), (1,)), ((), ())), preferred_element_type=F32)
          for kb, (ch, d) in zip(k_betas, problems)]
    a_mats = []
    for x, dec, (ch, d) in zip(kq, decays, problems):
        strict = (pj > pi) if d else (pj < pi)
        a_mats.append(jnp.where(strict, x[:c] * dec, 0.0))
    t_invs = _packed_inverses(a_mats)

    for idx, (ch, d) in enumerate(problems):
        u_ref, w_ref, attn_ref, qd_ref, kd_ref, gt_ref = out_refs[6 * d:6 * d + 6]
        rows = slice(ch * c, (ch + 1) * c)
        cin = chunk_in[ch]
        gw = g_wide[idx]
        eg = jnp.exp(gw)
        t_b = t_invs[idx].astype(BF16)
        u_ref[rows, :] = jnp.dot(t_b, _stack_heads(cin["vf"] * beta_w[idx], HEAD_V), preferred_element_type=F32)
        w_ref[rows, :] = jnp.dot(t_b, _stack_heads(k_betas[idx] * eg, HEAD_V),
                                 preferred_element_type=F32).astype(w_ref.dtype)
        attn_ref[rows, :] = (kq[idx][c:] * decays[idx]).astype(attn_ref.dtype)
        end = 0 if d else c - 1
        g_end = gw[end:end + 1, :]
        qd_ref[rows, :] = (cin["qf"] * eg).astype(qd_ref.dtype)
        kd_ref[rows, :] = (cin["kf"] * jnp.exp(g_end - gw)).astype(kd_ref.dtype)
        gt_ref[ch:ch + 1, :] = jnp.exp(g_end)


def gdn_wy(qkv, pf, a_scale, dtb):
    m = qkv.shape[0]
    t = GDN_WY_BLOCK
    cpb = t // GDN_CHUNK
    selg = np.zeros((2, LANES, GDN_PACK), np.float32)
    selk = np.zeros((2, LANES, GDN_VAL_WIDTH), np.float32)
    selb = np.zeros((2, LANES, GDN_VAL_WIDTH), np.float32)
    for d in range(2):
        for h in range(GDN_HEADS):
            selg[d, GDN_A_LANE + d * GDN_HEADS + h, h * GDN_CHUNK:(h + 1) * GDN_CHUNK] = 1.0
            selk[d, GDN_A_LANE + d * GDN_HEADS + h, h * HEAD_V:(h + 1) * HEAD_V] = 1.0
            selb[d, GDN_B_LANE + d * GDN_HEADS + h, h * HEAD_V:(h + 1) * HEAD_V] = 1.0
    wide = GDN_VAL_WIDTH
    out_shape, out_specs = [], []
    for _ in range(2):
        for width, dt in ((wide, F32), (wide, BF16), (GDN_PACK, BF16), (wide, BF16), (wide, BF16)):
            out_shape.append(jax.ShapeDtypeStruct((m, width), dt))
            out_specs.append(pl.BlockSpec((t, width), lambda i: (i, 0)))
        out_shape.append(jax.ShapeDtypeStruct((m // GDN_CHUNK, wide), F32))
        out_specs.append(pl.BlockSpec((cpb, wide), lambda i: (i, 0)))
    return pl.pallas_call(
        _gdn_wy_kernel,
        out_shape=tuple(out_shape),
        grid=(m // t,),
        in_specs=[
            pl.BlockSpec((t, GDN_QKV_WIDTH), lambda i: (i, 0)),
            pl.BlockSpec((t, LANES), lambda i: (i, PF_SMALL_COL // LANES)),
            pl.BlockSpec((1, LANES), lambda i: (0, 0)),
            pl.BlockSpec((1, LANES), lambda i: (0, 0)),
            pl.BlockSpec((2, LANES, GDN_PACK), lambda i: (0, 0, 0)),
            pl.BlockSpec((2, LANES, wide), lambda i: (0, 0, 0)),
            pl.BlockSpec((2, LANES, wide), lambda i: (0, 0, 0)),
        ],
        out_specs=tuple(out_specs),
        compiler_params=_params("parallel"),
        name="gdn_wy",
    )(qkv, pf, a_scale, dtb, jnp.asarray(selg, BF16), jnp.asarray(selk, BF16), jnp.asarray(selb, BF16))


GDN_PAIR = 2 * HEAD_V


def _gdn_scan_kernel(*refs):
    groups = (refs[0:6], refs[6:12])
    out_refs = refs[12:14]
    state_refs = refs[14:16]

    @pl.when(pl.program_id(1) == 0)
    def _():
        for s_ref in state_refs:
            s_ref[...] = jnp.zeros_like(s_ref)

    n_chunks = out_refs[0].shape[0] // GDN_CHUNK
    pairs = GDN_HEADS // 2
    pair_cols = [slice(p * GDN_PAIR, (p + 1) * GDN_PAIR) for p in range(pairs)]
    ri = lax.broadcasted_iota(jnp.int32, (GDN_PAIR, GDN_PAIR), 0) // HEAD_V
    ci = lax.broadcasted_iota(jnp.int32, (GDN_PAIR, GDN_PAIR), 1) // HEAD_V
    diag = ri == ci

    def body(c, carry):
        chunks = (c, n_chunks - 1 - c)
        rows = [pl.ds(pl.multiple_of(ch * GDN_CHUNK, GDN_CHUNK), GDN_CHUNK) for ch in chunks]
        states = [[s_ref[p] for p in range(pairs)] for s_ref in state_refs]
        states_b = [[s.astype(BF16) for s in st] for st in states]
        wq = [[jnp.dot(jnp.concatenate([groups[g][1][rows[g], cols], groups[g][3][rows[g], cols]], axis=0),
                       states_b[g][p], preferred_element_type=F32)
               for p, cols in enumerate(pair_cols)] for g in range(2)]
        ws = [[x[:GDN_CHUNK] for x in wq[g]] for g in range(2)]
        qs = [[x[GDN_CHUNK:] for x in wq[g]] for g in range(2)]
        v_new = [groups[g][0][rows[g], :] - jnp.concatenate(ws[g], axis=1) for g in range(2)]
        av = [jnp.dot(groups[g][2][rows[g], :], _stack_heads(v_new[g], HEAD_V), preferred_element_type=F32)
              for g in range(2)]
        v_new_b = [v.astype(BF16) for v in v_new]
        upd = [[lax.dot_general(groups[g][4][rows[g], cols], v_new_b[g][:, cols], (((0,), (0,)), ((), ())),
                                preferred_element_type=F32) for cols in pair_cols] for g in range(2)]
        for g in range(2):
            out_refs[g][rows[g], :] = jnp.concatenate(qs[g], axis=1) + av[g]
            gt = groups[g][5][pl.ds(chunks[g], 1), :]
            for p, cols in enumerate(pair_cols):
                state_refs[g][p] = states[g][p] * gt[:, cols] + jnp.where(diag, upd[g][p], 0.0)
        return carry

    lax.fori_loop(0, n_chunks, body, 0)


def gated_deltanet_branch(pb, pf, batch, conv_w, a_log, dt_bias, norm_gain):
    m = pb.shape[0]
    nb = m // batch // LIN_BLOCK
    cpb = LIN_BLOCK // GDN_CHUNK
    qkv = gdn_prep(pb, batch, conv_w)
    n_gate = 2 * GDN_HEADS
    a_scale = jnp.zeros((1, LANES), F32).at[0, GDN_A_LANE:GDN_A_LANE + n_gate].set(
        -jnp.exp(a_log.astype(F32)).reshape(n_gate))
    dtb = jnp.zeros((1, LANES), F32).at[0, GDN_A_LANE:GDN_A_LANE + n_gate].set(dt_bias.astype(F32).reshape(n_gate))
    wy = gdn_wy(qkv, pf, a_scale, dtb)
    widths = (GDN_VAL_WIDTH, GDN_VAL_WIDTH, GDN_PACK, GDN_VAL_WIDTH, GDN_VAL_WIDTH)

    def fwd(rows, width):
        return pl.BlockSpec((rows, width), lambda b, t: (b * nb + t, 0))

    def bwd(rows, width):
        return pl.BlockSpec((rows, width), lambda b, t: (b * nb + nb - 1 - t, 0))

    in_specs = [fwd(LIN_BLOCK, w) for w in widths] + [fwd(cpb, GDN_VAL_WIDTH)]
    in_specs += [bwd(LIN_BLOCK, w) for w in widths] + [bwd(cpb, GDN_VAL_WIDTH)]
    state = pltpu.VMEM((GDN_HEADS // 2, GDN_PAIR, GDN_PAIR), F32)
    o_f, o_b = pl.pallas_call(
        _gdn_scan_kernel,
        out_shape=(jax.ShapeDtypeStruct((m, GDN_VAL_WIDTH), F32), jax.ShapeDtypeStruct((m, GDN_VAL_WIDTH), F32)),
        grid=(batch, nb),
        in_specs=in_specs,
        out_specs=(fwd(LIN_BLOCK, GDN_VAL_WIDTH), bwd(LIN_BLOCK, GDN_VAL_WIDTH)),
        scratch_shapes=[state, state],
        compiler_params=_params("parallel", "arbitrary"),
        name="gdn_scan",
    )(*wy)
    return RawBranch(o_f, o_b, "gdn_og", norm_gain, True)


def _dot3(m, x):
    hi, mid, lo = _split_bf16(x, 3)
    return (jnp.dot(m, hi, preferred_element_type=F32)
            + (jnp.dot(m, mid, preferred_element_type=F32) + jnp.dot(m, lo, preferred_element_type=F32)))


LIN_CUM_ROWS = 256
LIN_SCORE_ROWS = 128


def _chunk_causal(n, chunk, reverse):
    i = lax.broadcasted_iota(jnp.int32, (n, n), 0)
    j = lax.broadcasted_iota(jnp.int32, (n, n), 1)
    return ((i // chunk) == (j // chunk)) & ((j >= i) if reverse else (j <= i))


def _lin_intra_kernel(*refs, load_inputs, n_in, n_params, heads):
    dir_refs = (refs[:n_in], refs[n_in:2 * n_in])
    params = refs[2 * n_in:2 * n_in + n_params]
    out_refs = refs[2 * n_in + n_params:]
    c = LIN_CHUNK
    dirs = (0, 1)
    loaded = [load_inputs(dir_refs[d], slice(None), d, params) for d in dirs]
    t, w = loaded[0][0].shape
    dk = w // heads
    nc = t // c
    cums = [jnp.where(_chunk_causal(LIN_CUM_ROWS, c, d == 1), 1.0, 0.0).astype(BF16) for d in dirs]
    bs = [jnp.concatenate([_dot3(cums[d], loaded[d][3][r:r + LIN_CUM_ROWS, :])
                           for r in range(0, t, LIN_CUM_ROWS)], axis=0) for d in dirs]
    qes, kes = [], []
    for d in dirs:
        oi_ref, qd_ref, kd_ref, gt_ref = out_refs[4 * d:4 * d + 4]
        qc, kc, vc, lg = loaded[d]
        b = bs[d]
        b3 = b.reshape(nc, c, w)
        mid = c - 1 - c // 2 if d else c // 2
        end = 0 if d else c - 1
        b_mid = jnp.broadcast_to(b3[:, mid:mid + 1, :], (nc, c, w)).reshape(t, w)
        b_end = jnp.broadcast_to(b3[:, end:end + 1, :], (nc, c, w)).reshape(t, w)
        qes.append((qc * jnp.exp(b - b_mid)).astype(BF16))
        kes.append((kc * jnp.exp(b_mid - b)).astype(BF16))
        qd_ref[...] = (qc * jnp.exp(b)).astype(qd_ref.dtype)
        kd_ref[...] = (kc * jnp.exp(b_end - b)).astype(kd_ref.dtype)
        gt_ref[...] = jnp.exp(b3[:, end, :])
    keeps = [_chunk_causal(LIN_SCORE_ROWS, c, d == 1) for d in dirs]
    lane = lax.broadcasted_iota(jnp.int32, (1, LANES), 1)
    for h in range(heads):
        win = slice((h * dk) // LANES * LANES, (h * dk) // LANES * LANES + LANES)
        lo = h * dk - win.start
        vcols = slice(h * HEAD_V, (h + 1) * HEAD_V)
        tiles = [(slice(r, r + LIN_SCORE_ROWS), d) for r in range(0, t, LIN_SCORE_ROWS) for d in dirs]
        scores = []
        for rows, d in tiles:
            qh = qes[d][rows, win]
            if dk < LANES:
                qh = jnp.where((lane >= lo) & (lane < lo + dk), qh, jnp.zeros_like(qh))
            scores.append(lax.dot_general(qh, kes[d][rows, win], (((1,), (1,)), ((), ())),
                                          preferred_element_type=F32))
        probs = [jnp.where(keeps[d], s, 0.0).astype(BF16) for s, (rows, d) in zip(scores, tiles)]
        for p, (rows, d) in zip(probs, tiles):
            out_refs[4 * d][rows, vcols] = jnp.dot(p, loaded[d][2][rows, vcols], preferred_element_type=F32)


def _lin_scan_kernel(*refs, heads, chunk, unroll):
    groups = (refs[0:5], refs[5:10])
    out_refs = refs[10:12]
    state_refs = refs[12:14]

    @pl.when(pl.program_id(1) == 0)
    def _():
        for s_ref in state_refs:
            s_ref[...] = jnp.zeros_like(s_ref)

    n_chunks = out_refs[0].shape[0] // chunk
    w = state_refs[0].shape[1]
    dk = w // heads
    lane = lax.broadcasted_iota(jnp.int32, (1, w), 1)
    masks = [(lane >= h * dk) & (lane < (h + 1) * dk) for h in range(heads)]

    def stack(x):
        return jnp.concatenate([jnp.where(m, x, jnp.zeros_like(x)) for m in masks], axis=0)

    def body(it, carry):
        steps = []
        for u in range(unroll):
            c = it * unroll + u
            steps += [(0, c), (1, n_chunks - 1 - c)]
        prepared = []
        for g, ch in steps:
            rows = pl.ds(pl.multiple_of(ch * chunk, chunk), chunk)
            oi_ref, qd_ref, kd_ref, v_ref, gt_ref = groups[g]
            vc = v_ref[rows, :]
            v4 = jnp.concatenate([vc[:, h * HEAD_V:(h + 1) * HEAD_V] for h in range(heads)], axis=0)
            upd = lax.dot_general(v4, stack(kd_ref[rows, :]), (((0,), (0,)), ((), ())),
                                  preferred_element_type=F32)
            prepared.append((rows, stack(qd_ref[rows, :]), upd, gt_ref[pl.ds(ch, 1), :]))
        states = [s_ref[...] for s_ref in state_refs]
        for (g, ch), (rows, q4, upd, gt) in zip(steps, prepared):
            o_inter = lax.dot_general(q4, states[g].astype(BF16), (((1,), (1,)), ((), ())),
                                      preferred_element_type=F32)
            out_refs[g][rows, :] = groups[g][0][rows, :] + jnp.concatenate(
                [o_inter[h * chunk:(h + 1) * chunk, :] for h in range(heads)], axis=1)
            states[g] = states[g] * gt + upd
        for s_ref, st in zip(state_refs, states):
            s_ref[...] = st
        return carry

    lax.fori_loop(0, n_chunks // unroll, body, 0)


def _bidir_lin_call(name, load_inputs, arrays, col_blocks, widths, params, batch, heads, key_width, v_col):
    m = arrays[0].shape[0]
    t = LIN_BLOCK
    nb = m // batch // t
    cpb = t // LIN_CHUNK
    out_w = heads * HEAD_V
    n_in = len(arrays)

    in_specs, operands = [], []
    for d in range(2):
        for a, wd, cb in zip(arrays, widths, col_blocks):
            in_specs.append(pl.BlockSpec((t, wd), functools.partial(lambda i, c: (i, c), c=cb[d])))
            operands.append(a)
    for p in params:
        in_specs.append(pl.BlockSpec(p.shape, functools.partial(lambda i, nd: (0,) * nd, nd=p.ndim)))
    out_shape, out_specs = [], []
    for _ in range(2):
        for rows_total, rows_blk, width, dt in ((m, t, out_w, F32), (m, t, key_width, BF16),
                                                (m, t, key_width, BF16), (m // LIN_CHUNK, cpb, key_width, F32)):
            out_shape.append(jax.ShapeDtypeStruct((rows_total, width), dt))
            out_specs.append(pl.BlockSpec((rows_blk, width), lambda i: (i, 0)))

    intra = pl.pallas_call(
        functools.partial(_lin_intra_kernel, load_inputs=load_inputs, n_in=n_in, n_params=len(params),
                          heads=heads),
        out_shape=tuple(out_shape),
        grid=(m // t,),
        in_specs=in_specs,
        out_specs=tuple(out_specs),
        compiler_params=_params("parallel"),
        name=name + "_intra",
    )(*operands, *params)

    def fwd(rows, width, col=0):
        return pl.BlockSpec((rows, width), lambda b, s: (b * nb + s, col))

    def bwd(rows, width, col=0):
        return pl.BlockSpec((rows, width), lambda b, s: (b * nb + nb - 1 - s, col))

    scan_specs, scan_ops = [], []
    for d, mk in enumerate((fwd, bwd)):
        oi, qd, kd, gt = intra[4 * d:4 * d + 4]
        scan_specs += [mk(t, out_w), mk(t, key_width), mk(t, key_width), mk(t, out_w, v_col), mk(cpb, key_width)]
        scan_ops += [oi, qd, kd, arrays[0], gt]
    state = pltpu.VMEM((HEAD_V, key_width), F32)
    return pl.pallas_call(
        functools.partial(_lin_scan_kernel, heads=heads, chunk=LIN_CHUNK, unroll=4),
        out_shape=(jax.ShapeDtypeStruct((m, out_w), F32), jax.ShapeDtypeStruct((m, out_w), F32)),
        grid=(batch, nb),
        in_specs=scan_specs,
        out_specs=(fwd(t, out_w), bwd(t, out_w)),
        scratch_shapes=[state, state],
        compiler_params=_params("parallel", "arbitrary"),
        name=name + "_scan",
    )(*scan_ops)


def gla_branch(pb, pf, batch, w_gate_up, b_gate, norm_gain):
    wpad = jnp.zeros((2, LANES, GLA_KEY_WIDTH), F32)
    for d in range(2):
        wpad = wpad.at[d, d * GLA_GATE_RANK:(d + 1) * GLA_GATE_RANK, :].set(w_gate_up[d].astype(F32))
    bias = b_gate.astype(F32).reshape(2, 1, GLA_KEY_WIDTH)
    v_col = PB_COL["gla_v"] // GLA_VAL_WIDTH
    cols = [(PB_COL["gla_q"] // GLA_KEY_WIDTH,) * 2, (PB_COL["gla_k"] // GLA_KEY_WIDTH,) * 2,
            (v_col,) * 2, (PF_SMALL_COL // LANES,) * 2]
    o_f, o_b = _bidir_lin_call("gla", _gla_inputs, [pb, pb, pb, pf], cols,
                               [GLA_KEY_WIDTH, GLA_KEY_WIDTH, GLA_VAL_WIDTH, LANES], [wpad, bias],
                               batch, GLA_HEADS, GLA_KEY_WIDTH, v_col)
    return RawBranch(o_f, o_b, "gla_og", norm_gain, True)


def hgrn2_branch(pb, pf, batch, lower_bound, norm_gain):
    lb = lower_bound.astype(F32).reshape(2, 1, HGRN_KEY_WIDTH)
    log_lb = jnp.log(jnp.maximum(lb, LB_FLOOR))
    log1m_lb = jnp.log1p(-lb)
    zc = PF_COL["hg_f"] // HGRN_KEY_WIDTH
    v_col = PB_COL["hg_i"] // HGRN_VAL_WIDTH
    cols = [(PB_COL["hg_q"] // HGRN_KEY_WIDTH,) * 2, (v_col,) * 2, (zc, zc + 1)]
    o_f, o_b = _bidir_lin_call("hgrn2", _hgrn_inputs, [pb, pb, pf], cols,
                               [HGRN_KEY_WIDTH, HGRN_VAL_WIDTH, HGRN_KEY_WIDTH], [lb, log_lb, log1m_lb],
                               batch, HGRN_HEADS, HGRN_KEY_WIDTH, v_col)
    return RawBranch(o_f, o_b, "hg_og", norm_gain, False)


def kernel(x, mem, g_mix, w_in, na_q_gain, na_k_gain, na_rel_bias, gla_w_gate_up, gla_b_gate, gla_norm_gain, gdn_conv_w, gdn_a_log, gdn_dt_bias, gdn_norm_gain, hgrn_lb_raw, hgrn_norm_gain, g_mem, w_mem_kv, mem_q_gain, mem_k_gain, w_branch, w_out, g_ffn, ffn_w_gate, ffn_w_up, ffn_w_down, moe_w_router, moe_b_router, moe_w_gate, moe_w_up, moe_w_down):
    B, S, D = x.shape
    n_tok = B * S
    lb_w = jax.nn.softmax(hgrn_lb_raw.astype(F32), axis=0)
    hgrn_lb = jnp.cumsum(lb_w, axis=0) - lb_w[0:1]
    x2 = x.reshape(n_tok, D)
    mem2 = mem.reshape(B * mem.shape[1], D)
    for layer in range(DEPTH):
        pb, pf = in_projection(x2, g_mix[layer], _rearrange_w_in(w_in[layer]), PB_COL["gates"], PB_WIDTH)
        kv = rms_matmul(mem2, g_mem[layer], w_mem_kv[layer].astype(BF16), tm=mem2.shape[0], tn=512,
                        out_dtype=BF16)
        branches = [
            neighbourhood_attention(pb, B, na_q_gain[layer], na_k_gain[layer], na_rel_bias[layer]),
            gla_branch(pb, pf, B, gla_w_gate_up[layer], gla_b_gate[layer], gla_norm_gain[layer]),
            gated_deltanet_branch(pb, pf, B, gdn_conv_w[layer], gdn_a_log[layer], gdn_dt_bias[layer],
                                  gdn_norm_gain[layer]),
            hgrn2_branch(pb, pf, B, hgrn_lb[layer], hgrn_norm_gain[layer]),
            memory_cross_attention(pb, kv, B, mem_q_gain[layer], mem_k_gain[layer]),
        ]
        merged = merge_branches(branches, pb, w_branch[layer].astype(BF16), tm=512, tn=1024)
        x2 = matmul_residual(merged, w_out[layer].astype(BF16), x2, tm=1024, tn=1024)

        j = layer // 2
        if layer % 2 == 0:
            act = rms_swiglu_up(x2, g_ffn[layer], ffn_w_gate[j].astype(BF16), ffn_w_up[j].astype(BF16),
                                tm=1024, tn=512)
            x2 = matmul_residual(act, ffn_w_down[j].astype(BF16), x2, tm=512, tn=1024)
        else:
            x2 = moe_layer(x2, g_ffn[layer], moe_w_router[j], moe_b_router[j], moe_w_gate[j], moe_w_up[j],
                           moe_w_down[j])
    return x2.reshape(B, S, D)
```

```python
import functools

import jax
import jax.numpy as jnp
import numpy as np
from jax import lax
from jax.experimental import pallas as pl
from jax.experimental.pallas import tpu as pltpu

F32 = jnp.float32
BF16 = jnp.bfloat16

D_MODEL = 2048
DEPTH = 2
RMS_EPS = 1e-6
MASK_VALUE = -1e30
LB_FLOOR = 1e-30
GRID_W = 64

NA_HEADS = 8
NA_HEAD_DIM = 64
NA_WIDTH = 512
NA_WIN_ROWS = 8
NA_WIN_COLS = 16

GLA_HEADS = 4
GLA_HEAD_K = 64
GLA_HEAD_V = 128
GLA_KEY_WIDTH = 256
GLA_VAL_WIDTH = 512
GLA_GATE_RANK = 16
GLA_GATE_NORMALIZER = 16.0

GDN_HEADS = 4
GDN_HEAD_K = 128
GDN_HEAD_V = 128
GDN_KEY_WIDTH = 512
GDN_VAL_WIDTH = 512
GDN_CHUNK = 64

HGRN_HEADS = 4
HGRN_HEAD_K = 128
HGRN_HEAD_V = 128
HGRN_KEY_WIDTH = 512
HGRN_VAL_WIDTH = 512

LIN_CHUNK = 32

MEM_HEADS = 4
MEM_HEAD_DIM = 128
MEM_WIDTH = 512

N_BRANCH = 5
BRANCH_WIDTH = 512
N_EXPERTS = 8
MOE_TOP_K = 2

IN_WIDTHS = (
    NA_WIDTH, NA_WIDTH, NA_WIDTH,
    GLA_KEY_WIDTH, GLA_KEY_WIDTH, GLA_VAL_WIDTH,
    2 * GLA_GATE_RANK, GLA_VAL_WIDTH,
    2 * GDN_KEY_WIDTH + GDN_VAL_WIDTH,
    2 * GDN_HEADS, 2 * GDN_HEADS, GDN_VAL_WIDTH,
    HGRN_KEY_WIDTH, 2 * HGRN_KEY_WIDTH, HGRN_VAL_WIDTH, HGRN_VAL_WIDTH,
    MEM_WIDTH,
    N_BRANCH * D_MODEL,
)
P_IN = sum(IN_WIDTHS)

V7X_VMEM_BYTES = 64 * 1024 * 1024
VMEM_LIMIT_BYTES = V7X_VMEM_BYTES - 8 * 1024 * 1024
LANES = 128


def _params(*semantics):
    return pltpu.CompilerParams(dimension_semantics=semantics, vmem_limit_bytes=VMEM_LIMIT_BYTES)


def _sigmoid(x):
    return 0.5 * jnp.tanh(0.5 * x) + 0.5


def _rms_norm_rows(x, gain):
    ms = jnp.mean(x * x, axis=-1, keepdims=True)
    return x * lax.rsqrt(ms + RMS_EPS) * gain


def _rms_matmul_kernel(x_ref, g_ref, w_ref, o_ref, h_ref):
    @pl.when(pl.program_id(1) == 0)
    def _():
        h_ref[...] = _rms_norm_rows(x_ref[...], g_ref[...]).astype(BF16)

    o_ref[...] = jnp.dot(h_ref[...], w_ref[...], preferred_element_type=F32).astype(o_ref.dtype)


def rms_matmul(x, gain, w, *, tm, tn, out_dtype=F32):
    m, k = x.shape
    n = w.shape[1]
    return pl.pallas_call(
        _rms_matmul_kernel,
        out_shape=jax.ShapeDtypeStruct((m, n), out_dtype),
        grid=(m // tm, n // tn),
        in_specs=[
            pl.BlockSpec((tm, k), lambda i, j: (i, 0)),
            pl.BlockSpec((1, k), lambda i, j: (0, 0)),
            pl.BlockSpec((k, tn), lambda i, j: (0, j)),
        ],
        out_specs=pl.BlockSpec((tm, tn), lambda i, j: (i, j)),
        scratch_shapes=[pltpu.VMEM((tm, k), BF16)],
        compiler_params=_params("parallel", "arbitrary"),
        name="rms_matmul",
    )(x, gain.reshape(1, k), w)


IN_PROJ_TILE = 512


def _in_projection_kernel(x_ref, g_ref, w_ref, ob_ref, of_ref, h_ref, *, n_plain_tiles, n_bf16_tiles):
    j = pl.program_id(1)

    @pl.when(j == 0)
    def _():
        h_ref[...] = _rms_norm_rows(x_ref[...], g_ref[...]).astype(BF16)

    r = jnp.dot(h_ref[...], w_ref[...], preferred_element_type=F32)

    @pl.when(j < n_plain_tiles)
    def _():
        ob_ref[...] = r.astype(ob_ref.dtype)

    @pl.when(jnp.logical_and(j >= n_plain_tiles, j < n_bf16_tiles))
    def _():
        ob_ref[...] = _sigmoid(r).astype(ob_ref.dtype)

    @pl.when(j >= n_bf16_tiles)
    def _():
        of_ref[...] = r


def in_projection(x, gain, w, n_plain, n_bf16, *, tm=1024):
    m, k = x.shape
    tn = IN_PROJ_TILE
    nb = n_bf16 // tn
    nf = (w.shape[1] - n_bf16) // tn
    return pl.pallas_call(
        functools.partial(_in_projection_kernel, n_plain_tiles=n_plain // tn, n_bf16_tiles=nb),
        out_shape=(jax.ShapeDtypeStruct((m, nb * tn), BF16), jax.ShapeDtypeStruct((m, nf * tn), F32)),
        grid=(m // tm, nb + nf),
        in_specs=[
            pl.BlockSpec((tm, k), lambda i, j: (i, 0)),
            pl.BlockSpec((1, k), lambda i, j: (0, 0)),
            pl.BlockSpec((k, tn), lambda i, j: (0, j)),
        ],
        out_specs=(pl.BlockSpec((tm, tn), lambda i, j: (i, jnp.minimum(j, nb - 1))),
                   pl.BlockSpec((tm, tn), lambda i, j: (i, jnp.maximum(j - nb, 0)))),
        scratch_shapes=[pltpu.VMEM((tm, k), BF16)],
        compiler_params=_params("parallel", "arbitrary"),
        name="in_projection",
    )(x, gain.reshape(1, k), w)


def _rms_swiglu_kernel(x_ref, g_ref, wg_ref, wu_ref, o_ref, h_ref):
    @pl.when(pl.program_id(1) == 0)
    def _():
        h_ref[...] = _rms_norm_rows(x_ref[...], g_ref[...]).astype(BF16)

    h = h_ref[...]
    a = jnp.dot(h, wg_ref[...], preferred_element_type=F32)
    b = jnp.dot(h, wu_ref[...], preferred_element_type=F32)
    o_ref[...] = (a * _sigmoid(a) * b).astype(o_ref.dtype)


def rms_swiglu_up(x, gain, wg, wu, *, tm, tn):
    m, k = x.shape
    n = wg.shape[1]
    return pl.pallas_call(
        _rms_swiglu_kernel,
        out_shape=jax.ShapeDtypeStruct((m, n), BF16),
        grid=(m // tm, n // tn),
        in_specs=[
            pl.BlockSpec((tm, k), lambda i, j: (i, 0)),
            pl.BlockSpec((1, k), lambda i, j: (0, 0)),
            pl.BlockSpec((k, tn), lambda i, j: (0, j)),
            pl.BlockSpec((k, tn), lambda i, j: (0, j)),
        ],
        out_specs=pl.BlockSpec((tm, tn), lambda i, j: (i, j)),
        scratch_shapes=[pltpu.VMEM((tm, k), BF16)],
        compiler_params=_params("parallel", "arbitrary"),
        name="rms_swiglu_up",
    )(x, gain.reshape(1, k), wg, wu)


def _matmul_residual_kernel(a_ref, w_ref, r_ref, o_ref):
    o_ref[...] = r_ref[...] + jnp.dot(a_ref[...], w_ref[...], preferred_element_type=F32)


def matmul_residual(a, w, res, *, tm, tn):
    m, k = a.shape
    n = w.shape[1]
    return pl.pallas_call(
        _matmul_residual_kernel,
        out_shape=jax.ShapeDtypeStruct((m, n), F32),
        grid=(m // tm, n // tn),
        in_specs=[
            pl.BlockSpec((tm, k), lambda i, j: (i, 0)),
            pl.BlockSpec((k, tn), lambda i, j: (0, j)),
            pl.BlockSpec((tm, tn), lambda i, j: (i, j)),
        ],
        out_specs=pl.BlockSpec((tm, tn), lambda i, j: (i, j)),
        compiler_params=_params("parallel", "arbitrary"),
        name="matmul_residual",
    )(a, w, res)


class RawBranch:
    def __init__(self, o_fwd, o_bwd, og_name, gain, silu_gate):
        self.o_fwd, self.o_bwd, self.og_name, self.gain, self.silu_gate = o_fwd, o_bwd, og_name, gain, silu_gate


def _merge_kernel(*refs, raw):
    pos = 0
    br = []
    for kind in raw:
        width = 1 if kind is None else 4
        br.append(refs[pos:pos + width])
        pos += width
    gl_refs = refs[pos:pos + N_BRANCH]
    wb_ref, o_ref, fin_ref = refs[pos + N_BRANCH:pos + N_BRANCH + 3]
    raw_slot = {n: s for s, n in enumerate(n for n, kind in enumerate(raw) if kind is not None)}

    @pl.when(pl.program_id(1) == 0)
    def _():
        for n, slot in raw_slot.items():
            of_ref, ob_ref, og_ref, gain_ref = br[n]
            for h in range(BRANCH_WIDTH // LANES):
                cols = slice(h * LANES, (h + 1) * LANES)
                y = _rms_norm_rows(of_ref[:, cols] + ob_ref[:, cols], gain_ref[...])
                g = og_ref[:, cols].astype(F32)
                gate = _sigmoid(g)
                if raw[n]:
                    gate = g * gate
                fin_ref[slot, :, cols] = (y * gate).astype(fin_ref.dtype)

    acc = None
    for n in range(N_BRANCH):
        b = br[n][0][...] if raw[n] is None else fin_ref[raw_slot[n]]
        y = jnp.dot(b, wb_ref[n], preferred_element_type=F32)
        t = gl_refs[n][...].astype(F32) * y
        acc = t if acc is None else acc + t
    o_ref[...] = acc.astype(o_ref.dtype)


def merge_branches(branches, pb, w_branch, *, tm, tn):
    m = pb.shape[0]
    d = D_MODEL
    tiles_per_branch = d // tn
    tile0 = PB_COL["gates"] // tn
    row_block = pl.BlockSpec((tm, BRANCH_WIDTH), lambda i, j: (i, 0))
    in_specs, operands, raw = [], [], []
    for b in branches:
        if isinstance(b, RawBranch):
            og_col = PB_COL[b.og_name] // BRANCH_WIDTH
            in_specs += [row_block, row_block,
                         pl.BlockSpec((tm, BRANCH_WIDTH), functools.partial(lambda i, j, c: (i, c), c=og_col)),
                         pl.BlockSpec((1, LANES), lambda i, j: (0, 0))]
            operands += [b.o_fwd, b.o_bwd, pb, b.gain.astype(F32).reshape(1, LANES)]
            raw.append(b.silu_gate)
        else:
            in_specs.append(row_block)
            operands.append(b)
            raw.append(None)
    in_specs += [
        pl.BlockSpec((tm, tn), functools.partial(lambda i, j, n: (i, tile0 + n * tiles_per_branch + j), n=n))
        for n in range(N_BRANCH)
    ]
    in_specs += [pl.BlockSpec((N_BRANCH, BRANCH_WIDTH, tn), lambda i, j: (0, 0, j))]
    n_raw = sum(kind is not None for kind in raw)
    return pl.pallas_call(
        functools.partial(_merge_kernel, raw=tuple(raw)),
        out_shape=jax.ShapeDtypeStruct((m, d), BF16),
        grid=(m // tm, d // tn),
        in_specs=in_specs,
        out_specs=pl.BlockSpec((tm, tn), lambda i, j: (i, j)),
        scratch_shapes=[pltpu.VMEM((max(n_raw, 1), tm, BRANCH_WIDTH), BF16)],
        compiler_params=_params("parallel", "arbitrary"),
        name="merge_branches",
    )(*operands, *([pb] * N_BRANCH), w_branch)


def _router_kernel(x_ref, g_ref, w_ref, b_ref, o_ref, h_ref, cnt_ref, run_ref, *, n_experts):
    @pl.when(pl.program_id(0) == 0)
    def _():
        run_ref[...] = jnp.zeros_like(run_ref)

    h = _rms_norm_rows(x_ref[...], g_ref[...])
    h_ref[...] = h.astype(h_ref.dtype)
    logits = _dot_f32(h, w_ref[...]) + b_ref[...]
    lane = lax.broadcasted_iota(jnp.int32, logits.shape, 1).astype(F32)
    neg = -jnp.inf
    lm = jnp.where(lane < n_experts, logits, neg)
    m1 = jnp.max(lm, axis=-1, keepdims=True)
    i1 = jnp.min(jnp.where(lm == m1, lane, float(LANES)), axis=-1, keepdims=True)
    lm2 = jnp.where(lane == i1, neg, lm)
    m2 = jnp.max(lm2, axis=-1, keepdims=True)
    i2 = jnp.min(jnp.where(lm2 == m2, lane, float(LANES)), axis=-1, keepdims=True)
    t = jnp.exp(m2 - m1)
    den = 1.0 + t

    tm = logits.shape[0]
    before = (lax.broadcasted_iota(jnp.int32, (tm, tm), 1)
              < lax.broadcasted_iota(jnp.int32, (tm, tm), 0))
    before = jnp.where(before, 1.0, 0.0).astype(BF16)
    pick1 = lane == i1
    pick2 = lane == i2
    oh1 = jnp.where(pick1, 1.0, 0.0)
    oh2 = jnp.where(pick2, 1.0, 0.0)
    pre1 = jnp.dot(before, oh1.astype(BF16), preferred_element_type=F32)
    pre2 = jnp.dot(before, oh2.astype(BF16), preferred_element_type=F32)
    tot1 = jnp.sum(oh1, axis=0, keepdims=True)
    tot2 = jnp.sum(oh2, axis=0, keepdims=True)
    run = run_ref[...]
    rank1 = jnp.sum(jnp.where(pick1, pre1 + run, 0.0), axis=-1, keepdims=True)
    rank2 = jnp.sum(jnp.where(pick2, pre2 + (run + tot1), 0.0), axis=-1, keepdims=True)
    run = run + tot1 + tot2
    run_ref[...] = run
    cnt_ref[...] = jnp.broadcast_to(run, cnt_ref.shape)

    out = jnp.where(lane == 0, 1.0 / den, jnp.where(lane == 1, t / den, jnp.where(lane == 2, i1, i2)))
    out = jnp.where(lane == 4, rank1, jnp.where(lane == 5, rank2, out))
    o_ref[...] = jnp.where(lane < 6, out, 0.0)


def router_top2(x, gain, w_router, b_router, *, tm=512):
    m, k = x.shape
    e = w_router.shape[1]
    w_pad = jnp.zeros((k, LANES), F32).at[:, :e].set(w_router.astype(F32))
    b_pad = jnp.zeros((1, LANES), F32).at[0, :e].set(b_router.astype(F32))
    route, h, cnt = pl.pallas_call(
        functools.partial(_router_kernel, n_experts=e),
        out_shape=(jax.ShapeDtypeStruct((m, LANES), F32), jax.ShapeDtypeStruct((m, k), BF16),
                   jax.ShapeDtypeStruct((8, LANES), F32)),
        grid=(m // tm,),
        in_specs=[
            pl.BlockSpec((tm, k), lambda i: (i, 0)),
            pl.BlockSpec((1, k), lambda i: (0, 0)),
            pl.BlockSpec((k, LANES), lambda i: (0, 0)),
            pl.BlockSpec((1, LANES), lambda i: (0, 0)),
        ],
        out_specs=(pl.BlockSpec((tm, LANES), lambda i: (i, 0)), pl.BlockSpec((tm, k), lambda i: (i, 0)),
                   pl.BlockSpec((8, LANES), lambda i: (0, 0))),
        scratch_shapes=[pltpu.VMEM((1, LANES), F32)],
        compiler_params=_params("arbitrary"),
        name="router_top2",
    )(x, gain.reshape(1, k), w_pad, b_pad)
    return route, h, cnt[0, :e].astype(jnp.int32)


MOE_TILE = 1024
MOE_SUB = 256
MOE_FF_TILE = 512


def _moe_kernel(tile_e_ref, tile_rows_ref, n_used_ref, x_ref, wg_ref, wu_ref, wd_ref, o_ref, acc_ref):
    i = pl.program_id(0)
    j = pl.program_id(1)
    last = pl.num_programs(1) - 1
    valid = tile_rows_ref[i]
    n_sub = (valid + (MOE_SUB - 1)) // MOE_SUB

    for k in range(1, MOE_TILE // MOE_SUB + 1):
        rows = slice(0, k * MOE_SUB)

        @pl.when(n_sub == k)
        def _(rows=rows):
            x = x_ref[rows, :]
            a = jnp.dot(x, wg_ref[0].astype(BF16), preferred_element_type=F32)
            b = jnp.dot(x, wu_ref[0].astype(BF16), preferred_element_type=F32)
            act = (a * _sigmoid(a) * b).astype(BF16)
            part = jnp.dot(act, wd_ref[0].astype(BF16), preferred_element_type=F32)

            @pl.when(j == 0)
            def _():
                acc_ref[rows, :] = part

            @pl.when(j > 0)
            def _():
                acc_ref[rows, :] += part

    for s in range(0, MOE_TILE, MOE_SUB):
        rows = slice(s, s + MOE_SUB)
        filled = s < valid

        @pl.when(jnp.logical_and(filled, j == last))
        def _(rows=rows):
            o_ref[rows, :] = acc_ref[rows, :].astype(o_ref.dtype)

        @pl.when(jnp.logical_and(jnp.logical_not(filled), j == last))
        def _(rows=rows):
            o_ref[rows, :] = jnp.zeros((MOE_SUB, o_ref.shape[1]), o_ref.dtype)


def moe_experts(xb, tile_e, tile_rows, n_used, wg, wu, wd):
    rows, d = xb.shape
    ff = wg.shape[2]
    tm, tf = MOE_TILE, MOE_FF_TILE
    n_tiles = rows // tm
    last_j = ff // tf - 1

    def x_map(i, j, te, tr, nu):
        return (jnp.minimum(i, nu[0] - 1), 0)

    def up_map(i, j, te, tr, nu):
        return (te[i], 0, jnp.where(i < nu[0], j, last_j))

    def down_map(i, j, te, tr, nu):
        return (te[i], jnp.where(i < nu[0], j, last_j), 0)

    grid_spec = pltpu.PrefetchScalarGridSpec(
        num_scalar_prefetch=3,
        grid=(n_tiles, ff // tf),
        in_specs=[
            pl.BlockSpec((tm, d), x_map, pipeline_mode=pl.Buffered(1)),
            pl.BlockSpec((1, d, tf), up_map),
            pl.BlockSpec((1, d, tf), up_map),
            pl.BlockSpec((1, tf, d), down_map),
        ],
        out_specs=pl.BlockSpec((tm, d), lambda i, j, te, tr, nu: (i, 0)),
        scratch_shapes=[pltpu.VMEM((tm, d), F32)],
    )
    return pl.pallas_call(
        _moe_kernel,
        out_shape=jax.ShapeDtypeStruct((rows, d), BF16),
        grid_spec=grid_spec,
        compiler_params=_params("arbitrary", "arbitrary"),
        name="moe_experts",
    )(tile_e, tile_rows, n_used, xb, wg, wu, wd)


def _moe_combine_kernel(x_ref, y0_ref, y1_ref, r_ref, o_ref):
    w = r_ref[...]
    o_ref[...] = x_ref[...] + w[:, 0:1] * y0_ref[...].astype(F32) + w[:, 1:2] * y1_ref[...].astype(F32)


def moe_combine(x2d, y0, y1, route, *, tm=512):
    n, d = x2d.shape
    row_block = pl.BlockSpec((tm, d), lambda i: (i, 0))
    return pl.pallas_call(
        _moe_combine_kernel,
        out_shape=jax.ShapeDtypeStruct((n, d), F32),
        grid=(n // tm,),
        in_specs=[row_block, row_block, row_block, pl.BlockSpec((tm, LANES), lambda i: (i, 0))],
        out_specs=row_block,
        compiler_params=_params("parallel"),
        name="moe_combine",
    )(x2d, y0, y1, route)


def moe_layer(x2d, gain, w_router, b_router, wg, wu, wd):
    n, d = x2d.shape
    e = N_EXPERTS
    route, h, counts = router_top2(x2d, gain, w_router, b_router)
    nk = n * MOE_TOP_K
    n_tiles = -(-nk // MOE_TILE) + e
    flat_e = route[:, 2:2 + MOE_TOP_K].astype(jnp.int32).reshape(nk)
    rank = route[:, 4:4 + MOE_TOP_K].astype(jnp.int32).reshape(nk)
    flat_tok = jnp.repeat(jnp.arange(n, dtype=jnp.int32), MOE_TOP_K)
    padded = (counts + MOE_TILE - 1) // MOE_TILE * MOE_TILE
    pad_end = jnp.cumsum(padded)
    pad_start = pad_end - padded
    slot = (pad_start[flat_e] + rank).astype(jnp.int32)
    n_slots = n_tiles * MOE_TILE
    slot_tok = (jnp.arange(n_slots, dtype=jnp.int32) % n).at[slot].set(flat_tok)
    tile_start = jnp.arange(n_tiles, dtype=jnp.int32) * MOE_TILE
    tile_e = jnp.minimum(jnp.searchsorted(pad_end, tile_start, side="right"), e - 1).astype(jnp.int32)
    tile_rows = jnp.clip(pad_start[tile_e] + counts[tile_e] - tile_start, 0, MOE_TILE).astype(jnp.int32)
    tile_rows = jnp.where(tile_start < pad_end[-1], tile_rows, 0)
    n_used = (pad_end[-1] // MOE_TILE).astype(jnp.int32).reshape(1)
    tile_e = jnp.where(tile_start < pad_end[-1], tile_e, tile_e[jnp.maximum(n_used[0] - 1, 0)])

    xb = h[slot_tok]
    yb = moe_experts(xb, tile_e, tile_rows, n_used, wg, wu, wd)
    slot2 = slot.reshape(n, MOE_TOP_K)
    return moe_combine(x2d, yb[slot2[:, 0]], yb[slot2[:, 1]], route)


_SRC = dict(zip(
    ("na_q", "na_k", "na_v", "gla_q", "gla_k", "gla_v", "gla_lr", "gla_og", "gdn_qkv", "gdn_a", "gdn_b",
     "gdn_og", "hg_q", "hg_f", "hg_i", "hg_og", "mem_q", "gates"),
    zip(np.cumsum((0,) + IN_WIDTHS[:-1]).tolist(), IN_WIDTHS)))
_PB_ORDER = ("na_q", "na_k", "na_v", "gla_q", "gla_k", "gla_v", "gla_og", "gdn_qkv", "gdn_og", "hg_q", "hg_i",
             "hg_og", "mem_q", "gates")
_PF_ORDER = ("hg_f", "gla_lr", "gdn_a", "gdn_b")
PB_COL = {}
_c = 0
for _name in _PB_ORDER:
    PB_COL[_name] = _c
    _c += _SRC[_name][1]
PB_WIDTH = _c
PF_COL = {}
_c = 0
for _name in _PF_ORDER:
    PF_COL[_name] = _c
    _c += _SRC[_name][1]
PF_WIDTH = -(-_c // IN_PROJ_TILE) * IN_PROJ_TILE
PF_SMALL_COL = PF_COL["gla_lr"]
GDN_A_LANE = PF_COL["gdn_a"] - PF_SMALL_COL
GDN_B_LANE = PF_COL["gdn_b"] - PF_SMALL_COL


def _rearrange_w_in(w):
    w = w.astype(BF16)
    cols = [w[:, _SRC[n][0]:_SRC[n][0] + _SRC[n][1]] for n in _PB_ORDER + _PF_ORDER]
    cols.append(jnp.zeros((w.shape[0], PB_WIDTH + PF_WIDTH - P_IN), BF16))
    return jnp.concatenate(cols, axis=1)


def _segment_rms(x, gain, seg_ones, seg_width):
    sq = x * x
    hi = sq.astype(BF16)
    lo = (sq - hi.astype(F32)).astype(BF16)
    ss = (jnp.dot(hi, seg_ones, preferred_element_type=F32)
          + jnp.dot(lo, seg_ones, preferred_element_type=F32))
    return x * lax.rsqrt(ss * (1.0 / seg_width) + RMS_EPS) * gain


NA_ROWS_PER_STEP = 8
NA_BAND = NA_WIN_ROWS * GRID_W


def _na_bias_table(rel_bias):
    c = np.arange(GRID_W)
    dc = np.clip(c[None, :] - c[:, None], 1 - NA_WIN_COLS, NA_WIN_COLS - 1) + (NA_WIN_COLS - 1)
    col_start = np.clip(c - NA_WIN_COLS // 2, 0, GRID_W - NA_WIN_COLS)
    col_in = (c[None, :] >= col_start[:, None]) & (c[None, :] < col_start[:, None] + NA_WIN_COLS)
    onehot = (dc[None] == np.arange(2 * NA_WIN_COLS - 1)[:, None, None]).astype(np.float32)
    base = jnp.einsum("hrc,cqk->hrqk", rel_bias.astype(F32), onehot, precision=lax.Precision.HIGHEST)
    base = jnp.where(col_in[None, None], base, MASK_VALUE)
    tables = []
    for cfg in range(NA_WIN_ROWS):
        rows = base[:, NA_WIN_ROWS - 1 - cfg:2 * NA_WIN_ROWS - 1 - cfg]
        tables.append(rows.transpose(0, 2, 1, 3).reshape(NA_HEADS // 2, 2 * GRID_W, NA_BAND))
    return jnp.stack(tables)


def _na_kernel(q_ref, k_ref, v_ref, qg_ref, kg_ref, seg_ref, bias_ref, o_ref, kn_ref):
    step = pl.program_id(1)
    rows_total = k_ref.shape[0] // GRID_W
    seg = seg_ref[...]

    @pl.when(step == 0)
    def _():
        def norm_keys(t, carry):
            rows = pl.ds(pl.multiple_of(t * 256, 256), 256)
            kn_ref[rows, :] = _segment_rms(k_ref[rows, :].astype(F32), kg_ref[...], seg, NA_HEAD_DIM).astype(BF16)
            return carry
        lax.fori_loop(0, k_ref.shape[0] // 256, norm_keys, 0)

    lane = lax.broadcasted_iota(jnp.int32, (1, LANES), 1)
    low_half = lane < NA_HEAD_DIM

    def one_row(rr, carry):
        r = step * NA_ROWS_PER_STEP + rr
        row_start = jnp.clip(r - NA_WIN_ROWS // 2, 0, rows_total - NA_WIN_ROWS)
        cfg = r - row_start
        qrows = pl.ds(pl.multiple_of(rr * GRID_W, GRID_W), GRID_W)
        band = pl.ds(pl.multiple_of(row_start * GRID_W, GRID_W), NA_BAND)
        qn = (_segment_rms(q_ref[qrows, :].astype(F32), qg_ref[...], seg, NA_HEAD_DIM)
              * (NA_HEAD_DIM ** -0.5)).astype(BF16)
        pairs = range(NA_HEADS // 2)
        pair_cols = [slice(pair * LANES, (pair + 1) * LANES) for pair in pairs]
        scores = []
        for cols in pair_cols:
            qp = qn[:, cols]
            q2 = jnp.concatenate([jnp.where(low_half, qp, jnp.zeros_like(qp)),
                                  jnp.where(low_half, jnp.zeros_like(qp), qp)], axis=0)
            scores.append(lax.dot_general(q2, kn_ref[band, cols], (((1,), (1,)), ((), ())),
                                          preferred_element_type=F32))
        exps, sums = [], []
        for s, pair in zip(scores, pairs):
            s = s + bias_ref[cfg, pair]
            e = jnp.exp(s - jnp.max(s, axis=-1, keepdims=True))
            sums.append(jnp.sum(e, axis=-1, keepdims=True))
            exps.append(e.astype(BF16))
        outs = [jnp.dot(e, v_ref[band, cols], preferred_element_type=F32) / l
                for e, l, cols in zip(exps, sums, pair_cols)]
        for o2, cols in zip(outs, pair_cols):
            o_ref[qrows, cols] = jnp.where(low_half, o2[:GRID_W], o2[GRID_W:]).astype(o_ref.dtype)
        return carry

    lax.fori_loop(0, NA_ROWS_PER_STEP, one_row, 0)


def neighbourhood_attention(pb, batch, q_gain, k_gain, rel_bias):
    m = pb.shape[0]
    s = m // batch
    tq = NA_ROWS_PER_STEP * GRID_W
    steps = s // tq
    qg = jnp.tile(q_gain.astype(F32), NA_HEADS).reshape(1, NA_WIDTH)
    kg = jnp.tile(k_gain.astype(F32), NA_HEADS).reshape(1, NA_WIDTH)
    seg = jnp.asarray(np.kron(np.eye(NA_HEADS), np.ones((NA_HEAD_DIM, NA_HEAD_DIM))), BF16)
    bias = _na_bias_table(rel_bias)
    cq, ck, cv = (PB_COL[n] // NA_WIDTH for n in ("na_q", "na_k", "na_v"))
    return pl.pallas_call(
        _na_kernel,
        out_shape=jax.ShapeDtypeStruct((m, NA_WIDTH), BF16),
        grid=(batch, steps),
        in_specs=[
            pl.BlockSpec((tq, NA_WIDTH), lambda b, t: (b * steps + t, cq)),
            pl.BlockSpec((s, NA_WIDTH), lambda b, t: (b, ck)),
            pl.BlockSpec((s, NA_WIDTH), lambda b, t: (b, cv)),
            pl.BlockSpec((1, NA_WIDTH), lambda b, t: (0, 0)),
            pl.BlockSpec((1, NA_WIDTH), lambda b, t: (0, 0)),
            pl.BlockSpec((NA_WIDTH, NA_WIDTH), lambda b, t: (0, 0)),
            pl.BlockSpec((NA_WIN_ROWS, NA_HEADS // 2, 2 * GRID_W, NA_BAND), lambda b, t: (0, 0, 0, 0)),
        ],
        out_specs=pl.BlockSpec((tq, NA_WIDTH), lambda b, t: (b * steps + t, 0)),
        scratch_shapes=[pltpu.VMEM((s, NA_WIDTH), BF16)],
        compiler_params=_params("parallel", "arbitrary"),
        name="neighbourhood_attention",
    )(pb, pb, pb, qg, kg, seg, bias)


def _mem_attn_kernel(q_ref, kv_ref, qg_ref, kg_ref, o_ref, kn_ref):
    @pl.when(pl.program_id(1) == 0)
    def _():
        for h in range(MEM_HEADS):
            cols = slice(h * MEM_HEAD_DIM, (h + 1) * MEM_HEAD_DIM)
            kn_ref[:, cols] = _rms_norm_rows(kv_ref[:, cols].astype(F32), kg_ref[...]).astype(BF16)

    head_cols = [slice(h * MEM_HEAD_DIM, (h + 1) * MEM_HEAD_DIM) for h in range(MEM_HEADS)]
    qns = [_rms_norm_rows(q_ref[:, cols].astype(F32), qg_ref[...]).astype(BF16) for cols in head_cols]
    scores = [lax.dot_general(qn, kn_ref[:, cols], (((1,), (1,)), ((), ())), preferred_element_type=F32)
              for qn, cols in zip(qns, head_cols)]
    exps, sums = [], []
    for s in scores:
        s = s * (MEM_HEAD_DIM ** -0.5)
        e = jnp.exp(s - jnp.max(s, axis=-1, keepdims=True))
        sums.append(jnp.sum(e, axis=-1, keepdims=True))
        exps.append(e.astype(BF16))
    outs = [jnp.dot(e, kv_ref[:, MEM_WIDTH + cols.start:MEM_WIDTH + cols.stop], preferred_element_type=F32)
            for e, cols in zip(exps, head_cols)]
    for o, l, cols in zip(outs, sums, head_cols):
        o_ref[:, cols] = (o / l).astype(o_ref.dtype)


def memory_cross_attention(pb, kv, batch, q_gain, k_gain, *, tq=512):
    m = pb.shape[0]
    steps = m // batch // tq
    n_mem = kv.shape[0] // batch
    cq = PB_COL["mem_q"] // MEM_WIDTH
    return pl.pallas_call(
        _mem_attn_kernel,
        out_shape=jax.ShapeDtypeStruct((m, MEM_WIDTH), BF16),
        grid=(batch, steps),
        in_specs=[
            pl.BlockSpec((tq, MEM_WIDTH), lambda b, t: (b * steps + t, cq)),
            pl.BlockSpec((n_mem, 2 * MEM_WIDTH), lambda b, t: (b, 0)),
            pl.BlockSpec((1, MEM_HEAD_DIM), lambda b, t: (0, 0)),
            pl.BlockSpec((1, MEM_HEAD_DIM), lambda b, t: (0, 0)),
        ],
        out_specs=pl.BlockSpec((tq, MEM_WIDTH), lambda b, t: (b * steps + t, 0)),
        scratch_shapes=[pltpu.VMEM((n_mem, MEM_WIDTH), BF16)],
        compiler_params=_params("parallel", "arbitrary"),
        name="memory_cross_attention",
    )(pb, kv, q_gain.astype(F32).reshape(1, MEM_HEAD_DIM), k_gain.astype(F32).reshape(1, MEM_HEAD_DIM))


LIN_BLOCK = 512
HEAD_V = 128


def _log1p_exp_neg(t):
    return jnp.log(1.0 + jnp.exp(-t))


def _log_sigmoid(x):
    return jnp.minimum(x, 0.0) - _log1p_exp_neg(jnp.abs(x))


def _logaddexp(a, b):
    return jnp.maximum(a, b) + _log1p_exp_neg(jnp.abs(a - b))


def _split_bf16(x, terms):
    parts = []
    for _ in range(terms):
        p = x.astype(BF16)
        parts.append(p)
        x = x - p.astype(F32)
    return parts


def _dot_f32(a, b):
    a_hi, a_lo = _split_bf16(a, 2)
    b_hi, b_lo = _split_bf16(b, 2)
    return (jnp.dot(a_hi, b_hi, preferred_element_type=F32)
            + (jnp.dot(a_hi, b_lo, preferred_element_type=F32) + jnp.dot(a_lo, b_hi, preferred_element_type=F32)))


def _cumsum_rows(mask, x):
    m = jnp.where(mask, 1.0, 0.0).astype(BF16)
    hi, mid, lo = _split_bf16(x, 3)
    return (jnp.dot(m, hi, preferred_element_type=F32)
            + (jnp.dot(m, mid, preferred_element_type=F32) + jnp.dot(m, lo, preferred_element_type=F32)))


def _gla_inputs(refs, rows, direction, params):
    q_ref, k_ref, v_ref, g_ref = refs
    wpad_ref, bias_ref = params
    qc = q_ref[rows, :].astype(F32) * (GLA_HEAD_K ** -0.5)
    kc = k_ref[rows, :].astype(F32)
    gk = _dot_f32(g_ref[rows, :], wpad_ref[direction]) + bias_ref[direction]
    lg = _log_sigmoid(gk) * (1.0 / GLA_GATE_NORMALIZER)
    return qc, kc, v_ref[rows, :], lg


def _hgrn_inputs(refs, rows, direction, params):
    q_ref, v_ref, z_ref = refs
    lb_ref, log_lb_ref, log1m_lb_ref = params
    qr = q_ref[rows, :].astype(F32)
    qc = qr * _sigmoid(qr)
    z = z_ref[rows, :]
    lg = _logaddexp(log_lb_ref[direction], log1m_lb_ref[direction] + _log_sigmoid(z))
    kc = (1.0 - lb_ref[direction]) * _sigmoid(-z)
    return qc, kc, v_ref[rows, :], lg


GDN_CONV_WIDTH = 5
GDN_QKV_WIDTH = 2 * GDN_KEY_WIDTH + GDN_VAL_WIDTH
GDN_HALO = 16


def _gdn_prep_kernel(prev_ref, cur_ref, next_ref, w_ref, o_ref, xp_ref, *, blocks_per_seq):
    i = pl.program_id(0)
    t = cur_ref.shape[0]
    pos = i % blocks_per_seq
    prev = prev_ref[...].astype(F32)
    nxt = next_ref[...].astype(F32)
    xp_ref[0:GDN_HALO, :] = jnp.where(pos == 0, jnp.zeros_like(prev), prev)
    xp_ref[GDN_HALO:GDN_HALO + t, :] = cur_ref[...].astype(F32)
    xp_ref[GDN_HALO + t:, :] = jnp.where(pos == blocks_per_seq - 1, jnp.zeros_like(nxt), nxt)
    half = GDN_CONV_WIDTH // 2
    for g in range(GDN_QKV_WIDTH // LANES):
        cols = slice(g * LANES, (g + 1) * LANES)
        acc = None
        for j in range(GDN_CONV_WIDTH):
            term = xp_ref[GDN_HALO - half + j:GDN_HALO - half + j + t, cols] * w_ref[j:j + 1, cols]
            acc = term if acc is None else acc + term
        y = acc * _sigmoid(acc)
        if g < 2 * GDN_HEADS:
            y = y * lax.rsqrt(jnp.sum(y * y, axis=-1, keepdims=True) + 1e-6)
            if g < GDN_HEADS:
                y = y * (GDN_HEAD_K ** -0.5)
        o_ref[:, cols] = y.astype(o_ref.dtype)


def gdn_prep(pb, batch, conv_w, *, t=512):
    m = pb.shape[0]
    blocks_per_seq = m // batch // t
    halo_per_block = t // GDN_HALO
    col = PB_COL["gdn_qkv"] // GDN_QKV_WIDTH
    last_halo = m // GDN_HALO - 1
    return pl.pallas_call(
        functools.partial(_gdn_prep_kernel, blocks_per_seq=blocks_per_seq),
        out_shape=jax.ShapeDtypeStruct((m, GDN_QKV_WIDTH), BF16),
        grid=(m // t,),
        in_specs=[
            pl.BlockSpec((GDN_HALO, GDN_QKV_WIDTH), lambda i: (jnp.maximum(i * halo_per_block - 1, 0), col)),
            pl.BlockSpec((t, GDN_QKV_WIDTH), lambda i: (i, col)),
            pl.BlockSpec((GDN_HALO, GDN_QKV_WIDTH),
                         lambda i: (jnp.minimum((i + 1) * halo_per_block, last_halo), col)),
            pl.BlockSpec((GDN_CONV_WIDTH, GDN_QKV_WIDTH), lambda i: (0, 0)),
        ],
        out_specs=pl.BlockSpec((t, GDN_QKV_WIDTH), lambda i: (i, 0)),
        scratch_shapes=[pltpu.VMEM((t + 2 * GDN_HALO, GDN_QKV_WIDTH), F32)],
        compiler_params=_params("parallel"),
        name="gdn_prep",
    )(pb, pb, pb, conv_w.astype(F32))


def _softplus(x):
    return jnp.maximum(x, 0.0) + _log1p_exp_neg(jnp.abs(x))


GDN_PACK = GDN_HEADS * GDN_CHUNK
GDN_WY_BLOCK = 512


def _stack_heads(x, width):
    heads = x.shape[1] // width
    lane = lax.broadcasted_iota(jnp.int32, (1, x.shape[1]), 1)
    return jnp.concatenate(
        [jnp.where((lane >= h * width) & (lane < (h + 1) * width), x, 0.0).astype(BF16) for h in range(heads)],
        axis=0)


def _packed_mm(x, y):
    return jnp.dot(x.astype(BF16), _stack_heads(y, GDN_CHUNK), preferred_element_type=F32)


def _packed_inverses(mats):
    c = GDN_CHUNK
    ii = lax.broadcasted_iota(jnp.int32, (c, GDN_PACK), 0)
    jj = lax.broadcasted_iota(jnp.int32, (c, GDN_PACK), 1) % c
    eye = (ii == jj).astype(F32)

    def same_block(s):
        return (ii // s) == (jj // s)

    ds = [jnp.where(same_block(8), a, 0.0) for a in mats]
    d2s = [_packed_mm(d, d) for d in ds]
    d4s = [_packed_mm(d2, d2) for d2 in d2s]
    ts = [_packed_mm(eye - d, eye + d2) for d, d2 in zip(ds, d2s)]
    ts = [_packed_mm(t, eye + d4) for t, d4 in zip(ts, d4s)]
    s = 8
    while s < c:
        off = same_block(2 * s) & jnp.logical_not(same_block(s))
        ets = [_packed_mm(jnp.where(off, a, 0.0), t) for a, t in zip(mats, ts)]
        ts = [t - _packed_mm(t, et) for t, et in zip(ts, ets)]
        s *= 2
    return ts


def _gdn_wy_kernel(qkv_ref, small_ref, a_ref, dtb_ref, selg_ref, selk_ref, selb_ref, *out_refs):
    c = GDN_CHUNK
    n_chunks = qkv_ref.shape[0] // c
    ii = lax.broadcasted_iota(jnp.int32, (c, c), 0)
    jj = lax.broadcasted_iota(jnp.int32, (c, c), 1)
    pi = lax.broadcasted_iota(jnp.int32, (c, GDN_PACK), 0)
    pj = lax.broadcasted_iota(jnp.int32, (c, GDN_PACK), 1) % c
    eye_p = (pi == pj).astype(F32)
    ones_cc = jnp.ones((c, c), BF16)

    problems = [(ch, d) for ch in range(n_chunks) for d in range(2)]
    chunk_in = []
    for ch in range(n_chunks):
        rows = slice(ch * c, (ch + 1) * c)
        qkv = qkv_ref[rows, :]
        small = small_ref[rows, :]
        kf = qkv[:, GDN_KEY_WIDTH:2 * GDN_KEY_WIDTH].astype(F32)
        chunk_in.append(dict(
            qf=qkv[:, :GDN_KEY_WIDTH].astype(F32), kf=kf, vf=qkv[:, 2 * GDN_KEY_WIDTH:].astype(F32),
            kbd=_stack_heads(kf, HEAD_V),
            log_alpha=a_ref[...] * _softplus(small + dtb_ref[...]),
            beta_all=_sigmoid(small)))

    def sel3(x, sel):
        hi, mid, lo = _split_bf16(x, 3)
        return (jnp.dot(hi, sel, preferred_element_type=F32)
                + (jnp.dot(mid, sel, preferred_element_type=F32) + jnp.dot(lo, sel, preferred_element_type=F32)))

    g_all = [_cumsum_rows((jj >= ii) if d else (jj <= ii), chunk_in[ch]["log_alpha"]) for ch, d in problems]
    g_pack = [sel3(g, selg_ref[d]) for g, (ch, d) in zip(g_all, problems)]
    g_wide = [sel3(g, selk_ref[d]) for g, (ch, d) in zip(g_all, problems)]
    beta_w = [sel3(chunk_in[ch]["beta_all"], selb_ref[d]) for ch, d in problems]
    g_rowp = []
    for gp in g_pack:
        hi, mid, lo = _split_bf16(gp * eye_p, 3)
        g_rowp.append(jnp.dot(ones_cc, hi, preferred_element_type=F32)
                      + (jnp.dot(ones_cc, mid, preferred_element_type=F32)
                         + jnp.dot(ones_cc, lo, preferred_element_type=F32)))
    decays, k_betas = [], []
    for gp, gr, bw, (ch, d) in zip(g_pack, g_rowp, beta_w, problems):
        incl = (pj >= pi) if d else (pj <= pi)
        decays.append(jnp.where(incl, jnp.exp(jnp.where(incl, gp - gr, 0.0)), 0.0))
        k_betas.append(chunk_in[ch]["kf"] * bw)
    kq = [lax.dot_general(jnp.concatenate([kb, chunk_in[ch]["qf"]], axis=0).astype(BF16), chunk_in[ch]["kbd"],
                          (((1,), (1,)), ((), ())), preferred_element_type=F32)
          for kb, (ch, d) in zip(k_betas, problems)]
    a_mats = []
    for x, dec, (ch, d) in zip(kq, decays, problems):
        strict = (pj > pi) if d else (pj < pi)
        a_mats.append(jnp.where(strict, x[:c] * dec, 0.0))
    t_invs = _packed_inverses(a_mats)

    for idx, (ch, d) in enumerate(problems):
        u_ref, w_ref, attn_ref, qd_ref, kd_ref, gt_ref = out_refs[6 * d:6 * d + 6]
        rows = slice(ch * c, (ch + 1) * c)
        cin = chunk_in[ch]
        gw = g_wide[idx]
        eg = jnp.exp(gw)
        t_b = t_invs[idx].astype(BF16)
        u_ref[rows, :] = jnp.dot(t_b, _stack_heads(cin["vf"] * beta_w[idx], HEAD_V), preferred_element_type=F32)
        w_ref[rows, :] = jnp.dot(t_b, _stack_heads(k_betas[idx] * eg, HEAD_V),
                                 preferred_element_type=F32).astype(w_ref.dtype)
        attn_ref[rows, :] = (kq[idx][c:] * decays[idx]).astype(attn_ref.dtype)
        end = 0 if d else c - 1
        g_end = gw[end:end + 1, :]
        qd_ref[rows, :] = (cin["qf"] * eg).astype(qd_ref.dtype)
        kd_ref[rows, :] = (cin["kf"] * jnp.exp(g_end - gw)).astype(kd_ref.dtype)
        gt_ref[ch:ch + 1, :] = jnp.exp(g_end)


def gdn_wy(qkv, pf, a_scale, dtb):
    m = qkv.shape[0]
    t = GDN_WY_BLOCK
    cpb = t // GDN_CHUNK
    selg = np.zeros((2, LANES, GDN_PACK), np.float32)
    selk = np.zeros((2, LANES, GDN_VAL_WIDTH), np.float32)
    selb = np.zeros((2, LANES, GDN_VAL_WIDTH), np.float32)
    for d in range(2):
        for h in range(GDN_HEADS):
            selg[d, GDN_A_LANE + d * GDN_HEADS + h, h * GDN_CHUNK:(h + 1) * GDN_CHUNK] = 1.0
            selk[d, GDN_A_LANE + d * GDN_HEADS + h, h * HEAD_V:(h + 1) * HEAD_V] = 1.0
            selb[d, GDN_B_LANE + d * GDN_HEADS + h, h * HEAD_V:(h + 1) * HEAD_V] = 1.0
    wide = GDN_VAL_WIDTH
    out_shape, out_specs = [], []
    for _ in range(2):
        for width, dt in ((wide, F32), (wide, BF16), (GDN_PACK, BF16), (wide, BF16), (wide, BF16)):
            out_shape.append(jax.ShapeDtypeStruct((m, width), dt))
            out_specs.append(pl.BlockSpec((t, width), lambda i: (i, 0)))
        out_shape.append(jax.ShapeDtypeStruct((m // GDN_CHUNK, wide), F32))
        out_specs.append(pl.BlockSpec((cpb, wide), lambda i: (i, 0)))
    return pl.pallas_call(
        _gdn_wy_kernel,
        out_shape=tuple(out_shape),
        grid=(m // t,),
        in_specs=[
            pl.BlockSpec((t, GDN_QKV_WIDTH), lambda i: (i, 0)),
            pl.BlockSpec((t, LANES), lambda i: (i, PF_SMALL_COL // LANES)),
            pl.BlockSpec((1, LANES), lambda i: (0, 0)),
            pl.BlockSpec((1, LANES), lambda i: (0, 0)),
            pl.BlockSpec((2, LANES, GDN_PACK), lambda i: (0, 0, 0)),
            pl.BlockSpec((2, LANES, wide), lambda i: (0, 0, 0)),
            pl.BlockSpec((2, LANES, wide), lambda i: (0, 0, 0)),
        ],
        out_specs=tuple(out_specs),
        compiler_params=_params("parallel"),
        name="gdn_wy",
    )(qkv, pf, a_scale, dtb, jnp.asarray(selg, BF16), jnp.asarray(selk, BF16), jnp.asarray(selb, BF16))


GDN_PAIR = 2 * HEAD_V


def _gdn_scan_kernel(*refs):
    groups = (refs[0:6], refs[6:12])
    out_refs = refs[12:14]
    state_refs = refs[14:16]

    @pl.when(pl.program_id(1) == 0)
    def _():
        for s_ref in state_refs:
            s_ref[...] = jnp.zeros_like(s_ref)

    n_chunks = out_refs[0].shape[0] // GDN_CHUNK
    pairs = GDN_HEADS // 2
    pair_cols = [slice(p * GDN_PAIR, (p + 1) * GDN_PAIR) for p in range(pairs)]
    ri = lax.broadcasted_iota(jnp.int32, (GDN_PAIR, GDN_PAIR), 0) // HEAD_V
    ci = lax.broadcasted_iota(jnp.int32, (GDN_PAIR, GDN_PAIR), 1) // HEAD_V
    diag = ri == ci

    def body(c, carry):
        chunks = (c, n_chunks - 1 - c)
        rows = [pl.ds(pl.multiple_of(ch * GDN_CHUNK, GDN_CHUNK), GDN_CHUNK) for ch in chunks]
        states = [[s_ref[p] for p in range(pairs)] for s_ref in state_refs]
        states_b = [[s.astype(BF16) for s in st] for st in states]
        wq = [[jnp.dot(jnp.concatenate([groups[g][1][rows[g], cols], groups[g][3][rows[g], cols]], axis=0),
                       states_b[g][p], preferred_element_type=F32)
               for p, cols in enumerate(pair_cols)] for g in range(2)]
        ws = [[x[:GDN_CHUNK] for x in wq[g]] for g in range(2)]
        qs = [[x[GDN_CHUNK:] for x in wq[g]] for g in range(2)]
        v_new = [groups[g][0][rows[g], :] - jnp.concatenate(ws[g], axis=1) for g in range(2)]
        av = [jnp.dot(groups[g][2][rows[g], :], _stack_heads(v_new[g], HEAD_V), preferred_element_type=F32)
              for g in range(2)]
        v_new_b = [v.astype(BF16) for v in v_new]
        upd = [[lax.dot_general(groups[g][4][rows[g], cols], v_new_b[g][:, cols], (((0,), (0,)), ((), ())),
                                preferred_element_type=F32) for cols in pair_cols] for g in range(2)]
        for g in range(2):
            out_refs[g][rows[g], :] = jnp.concatenate(qs[g], axis=1) + av[g]
            gt = groups[g][5][pl.ds(chunks[g], 1), :]
            for p, cols in enumerate(pair_cols):
                state_refs[g][p] = states[g][p] * gt[:, cols] + jnp.where(diag, upd[g][p], 0.0)
        return carry

    lax.fori_loop(0, n_chunks, body, 0)


def gated_deltanet_branch(pb, pf, batch, conv_w, a_log, dt_bias, norm_gain):
    m = pb.shape[0]
    nb = m // batch // LIN_BLOCK
    cpb = LIN_BLOCK // GDN_CHUNK
    qkv = gdn_prep(pb, batch, conv_w)
    n_gate = 2 * GDN_HEADS
    a_scale = jnp.zeros((1, LANES), F32).at[0, GDN_A_LANE:GDN_A_LANE + n_gate].set(
        -jnp.exp(a_log.astype(F32)).reshape(n_gate))
    dtb = jnp.zeros((1, LANES), F32).at[0, GDN_A_LANE:GDN_A_LANE + n_gate].set(dt_bias.astype(F32).reshape(n_gate))
    wy = gdn_wy(qkv, pf, a_scale, dtb)
    widths = (GDN_VAL_WIDTH, GDN_VAL_WIDTH, GDN_PACK, GDN_VAL_WIDTH, GDN_VAL_WIDTH)

    def fwd(rows, width):
        return pl.BlockSpec((rows, width), lambda b, t: (b * nb + t, 0))

    def bwd(rows, width):
        return pl.BlockSpec((rows, width), lambda b, t: (b * nb + nb - 1 - t, 0))

    in_specs = [fwd(LIN_BLOCK, w) for w in widths] + [fwd(cpb, GDN_VAL_WIDTH)]
    in_specs += [bwd(LIN_BLOCK, w) for w in widths] + [bwd(cpb, GDN_VAL_WIDTH)]
    state = pltpu.VMEM((GDN_HEADS // 2, GDN_PAIR, GDN_PAIR), F32)
    o_f, o_b = pl.pallas_call(
        _gdn_scan_kernel,
        out_shape=(jax.ShapeDtypeStruct((m, GDN_VAL_WIDTH), F32), jax.ShapeDtypeStruct((m, GDN_VAL_WIDTH), F32)),
        grid=(batch, nb),
        in_specs=in_specs,
        out_specs=(fwd(LIN_BLOCK, GDN_VAL_WIDTH), bwd(LIN_BLOCK, GDN_VAL_WIDTH)),
        scratch_shapes=[state, state],
        compiler_params=_params("parallel", "arbitrary"),
        name="gdn_scan",
    )(*wy)
    return RawBranch(o_f, o_b, "gdn_og", norm_gain, True)


def _dot3(m, x):
    hi, mid, lo = _split_bf16(x, 3)
    return (jnp.dot(m, hi, preferred_element_type=F32)
            + (jnp.dot(m, mid, preferred_element_type=F32) + jnp.dot(m, lo, preferred_element_type=F32)))


LIN_CUM_ROWS = 256
LIN_SCORE_ROWS = 128


def _chunk_causal(n, chunk, reverse):
    i = lax.broadcasted_iota(jnp.int32, (n, n), 0)
    j = lax.broadcasted_iota(jnp.int32, (n, n), 1)
    return ((i // chunk) == (j // chunk)) & ((j >= i) if reverse else (j <= i))


def _lin_intra_kernel(*refs, load_inputs, n_in, n_params, heads):
    dir_refs = (refs[:n_in], refs[n_in:2 * n_in])
    params = refs[2 * n_in:2 * n_in + n_params]
    out_refs = refs[2 * n_in + n_params:]
    c = LIN_CHUNK
    dirs = (0, 1)
    loaded = [load_inputs(dir_refs[d], slice(None), d, params) for d in dirs]
    t, w = loaded[0][0].shape
    dk = w // heads
    nc = t // c
    cums = [jnp.where(_chunk_causal(LIN_CUM_ROWS, c, d == 1), 1.0, 0.0).astype(BF16) for d in dirs]
    bs = [jnp.concatenate([_dot3(cums[d], loaded[d][3][r:r + LIN_CUM_ROWS, :])
                           for r in range(0, t, LIN_CUM_ROWS)], axis=0) for d in dirs]
    qes, kes = [], []
    for d in dirs:
        oi_ref, qd_ref, kd_ref, gt_ref = out_refs[4 * d:4 * d + 4]
        qc, kc, vc, lg = loaded[d]
        b = bs[d]
        b3 = b.reshape(nc, c, w)
        mid = c - 1 - c // 2 if d else c // 2
        end = 0 if d else c - 1
        b_mid = jnp.broadcast_to(b3[:, mid:mid + 1, :], (nc, c, w)).reshape(t, w)
        b_end = jnp.broadcast_to(b3[:, end:end + 1, :], (nc, c, w)).reshape(t, w)
        qes.append((qc * jnp.exp(b - b_mid)).astype(BF16))
        kes.append((kc * jnp.exp(b_mid - b)).astype(BF16))
        qd_ref[...] = (qc * jnp.exp(b)).astype(qd_ref.dtype)
        kd_ref[...] = (kc * jnp.exp(b_end - b)).astype(kd_ref.dtype)
        gt_ref[...] = jnp.exp(b3[:, end, :])
    keeps = [_chunk_causal(LIN_SCORE_ROWS, c, d == 1) for d in dirs]
    lane = lax.broadcasted_iota(jnp.int32, (1, LANES), 1)
    for h in range(heads):
        win = slice((h * dk) // LANES * LANES, (h * dk) // LANES * LANES + LANES)
        lo = h * dk - win.start
        vcols = slice(h * HEAD_V, (h + 1) * HEAD_V)
        tiles = [(slice(r, r + LIN_SCORE_ROWS), d) for r in range(0, t, LIN_SCORE_ROWS) for d in dirs]
        scores = []
        for rows, d in tiles:
            qh = qes[d][rows, win]
            if dk < LANES:
                qh = jnp.where((lane >= lo) & (lane < lo + dk), qh, jnp.zeros_like(qh))
            scores.append(lax.dot_general(qh, kes[d][rows, win], (((1,), (1,)), ((), ())),
                                          preferred_element_type=F32))
        probs = [jnp.where(keeps[d], s, 0.0).astype(BF16) for s, (rows, d) in zip(scores, tiles)]
        for p, (rows, d) in zip(probs, tiles):
            out_refs[4 * d][rows, vcols] = jnp.dot(p, loaded[d][2][rows, vcols], preferred_element_type=F32)


def _lin_scan_kernel(*refs, heads, chunk, unroll):
    groups = (refs[0:5], refs[5:10])
    out_refs = refs[10:12]
    state_refs = refs[12:14]

    @pl.when(pl.program_id(1) == 0)
    def _():
        for s_ref in state_refs:
            s_ref[...] = jnp.zeros_like(s_ref)

    n_chunks = out_refs[0].shape[0] // chunk
    w = state_refs[0].shape[1]
    dk = w // heads
    lane = lax.broadcasted_iota(jnp.int32, (1, w), 1)
    masks = [(lane >= h * dk) & (lane < (h + 1) * dk) for h in range(heads)]

    def stack(x):
        return jnp.concatenate([jnp.where(m, x, jnp.zeros_like(x)) for m in masks], axis=0)

    def body(it, carry):
        steps = []
        for u in range(unroll):
            c = it * unroll + u
            steps += [(0, c), (1, n_chunks - 1 - c)]
        prepared = []
        for g, ch in steps:
            rows = pl.ds(pl.multiple_of(ch * chunk, chunk), chunk)
            oi_ref, qd_ref, kd_ref, v_ref, gt_ref = groups[g]
            vc = v_ref[rows, :]
            v4 = jnp.concatenate([vc[:, h * HEAD_V:(h + 1) * HEAD_V] for h in range(heads)], axis=0)
            upd = lax.dot_general(v4, stack(kd_ref[rows, :]), (((0,), (0,)), ((), ())),
                                  preferred_element_type=F32)
            prepared.append((rows, stack(qd_ref[rows, :]), upd, gt_ref[pl.ds(ch, 1), :]))
        states = [s_ref[...] for s_ref in state_refs]
        for (g, ch), (rows, q4, upd, gt) in zip(steps, prepared):
            o_inter = lax.dot_general(q4, states[g].astype(BF16), (((1,), (1,)), ((), ())),
                                      preferred_element_type=F32)
            out_refs[g][rows, :] = groups[g][0][rows, :] + jnp.concatenate(
                [o_inter[h * chunk:(h + 1) * chunk, :] for h in range(heads)], axis=1)
            states[g] = states[g] * gt + upd
        for s_ref, st in zip(state_refs, states):
            s_ref[...] = st
        return carry

    lax.fori_loop(0, n_chunks // unroll, body, 0)


def _bidir_lin_call(name, load_inputs, arrays, col_blocks, widths, params, batch, heads, key_width, v_col):
    m = arrays[0].shape[0]
    t = LIN_BLOCK
    nb = m // batch // t
    cpb = t // LIN_CHUNK
    out_w = heads * HEAD_V
    n_in = len(arrays)

    in_specs, operands = [], []
    for d in range(2):
        for a, wd, cb in zip(arrays, widths, col_blocks):
            in_specs.append(pl.BlockSpec((t, wd), functools.partial(lambda i, c: (i, c), c=cb[d])))
            operands.append(a)
    for p in params:
        in_specs.append(pl.BlockSpec(p.shape, functools.partial(lambda i, nd: (0,) * nd, nd=p.ndim)))
    out_shape, out_specs = [], []
    for _ in range(2):
        for rows_total, rows_blk, width, dt in ((m, t, out_w, F32), (m, t, key_width, BF16),
                                                (m, t, key_width, BF16), (m // LIN_CHUNK, cpb, key_width, F32)):
            out_shape.append(jax.ShapeDtypeStruct((rows_total, width), dt))
            out_specs.append(pl.BlockSpec((rows_blk, width), lambda i: (i, 0)))

    intra = pl.pallas_call(
        functools.partial(_lin_intra_kernel, load_inputs=load_inputs, n_in=n_in, n_params=len(params),
                          heads=heads),
        out_shape=tuple(out_shape),
        grid=(m // t,),
        in_specs=in_specs,
        out_specs=tuple(out_specs),
        compiler_params=_params("parallel"),
        name=name + "_intra",
    )(*operands, *params)

    def fwd(rows, width, col=0):
        return pl.BlockSpec((rows, width), lambda b, s: (b * nb + s, col))

    def bwd(rows, width, col=0):
        return pl.BlockSpec((rows, width), lambda b, s: (b * nb + nb - 1 - s, col))

    scan_specs, scan_ops = [], []
    for d, mk in enumerate((fwd, bwd)):
        oi, qd, kd, gt = intra[4 * d:4 * d + 4]
        scan_specs += [mk(t, out_w), mk(t, key_width), mk(t, key_width), mk(t, out_w, v_col), mk(cpb, key_width)]
        scan_ops += [oi, qd, kd, arrays[0], gt]
    state = pltpu.VMEM((HEAD_V, key_width), F32)
    return pl.pallas_call(
        functools.partial(_lin_scan_kernel, heads=heads, chunk=LIN_CHUNK, unroll=4),
        out_shape=(jax.ShapeDtypeStruct((m, out_w), F32), jax.ShapeDtypeStruct((m, out_w), F32)),
        grid=(batch, nb),
        in_specs=scan_specs,
        out_specs=(fwd(t, out_w), bwd(t, out_w)),
        scratch_shapes=[state, state],
        compiler_params=_params("parallel", "arbitrary"),
        name=name + "_scan",
    )(*scan_ops)


def gla_branch(pb, pf, batch, w_gate_up, b_gate, norm_gain):
    wpad = jnp.zeros((2, LANES, GLA_KEY_WIDTH), F32)
    for d in range(2):
        wpad = wpad.at[d, d * GLA_GATE_RANK:(d + 1) * GLA_GATE_RANK, :].set(w_gate_up[d].astype(F32))
    bias = b_gate.astype(F32).reshape(2, 1, GLA_KEY_WIDTH)
    v_col = PB_COL["gla_v"] // GLA_VAL_WIDTH
    cols = [(PB_COL["gla_q"] // GLA_KEY_WIDTH,) * 2, (PB_COL["gla_k"] // GLA_KEY_WIDTH,) * 2,
            (v_col,) * 2, (PF_SMALL_COL // LANES,) * 2]
    o_f, o_b = _bidir_lin_call("gla", _gla_inputs, [pb, pb, pb, pf], cols,
                               [GLA_KEY_WIDTH, GLA_KEY_WIDTH, GLA_VAL_WIDTH, LANES], [wpad, bias],
                               batch, GLA_HEADS, GLA_KEY_WIDTH, v_col)
    return RawBranch(o_f, o_b, "gla_og", norm_gain, True)


def hgrn2_branch(pb, pf, batch, lower_bound, norm_gain):
    lb = lower_bound.astype(F32).reshape(2, 1, HGRN_KEY_WIDTH)
    log_lb = jnp.log(jnp.maximum(lb, LB_FLOOR))
    log1m_lb = jnp.log1p(-lb)
    zc = PF_COL["hg_f"] // HGRN_KEY_WIDTH
    v_col = PB_COL["hg_i"] // HGRN_VAL_WIDTH
    cols = [(PB_COL["hg_q"] // HGRN_KEY_WIDTH,) * 2, (v_col,) * 2, (zc, zc + 1)]
    o_f, o_b = _bidir_lin_call("hgrn2", _hgrn_inputs, [pb, pb, pf], cols,
                               [HGRN_KEY_WIDTH, HGRN_VAL_WIDTH, HGRN_KEY_WIDTH], [lb, log_lb, log1m_lb],
                               batch, HGRN_HEADS, HGRN_KEY_WIDTH, v_col)
    return RawBranch(o_f, o_b, "hg_og", norm_gain, False)


def kernel(x, mem, g_mix, w_in, na_q_gain, na_k_gain, na_rel_bias, gla_w_gate_up, gla_b_gate, gla_norm_gain, gdn_conv_w, gdn_a_log, gdn_dt_bias, gdn_norm_gain, hgrn_lb_raw, hgrn_norm_gain, g_mem, w_mem_kv, mem_q_gain, mem_k_gain, w_branch, w_out, g_ffn, ffn_w_gate, ffn_w_up, ffn_w_down, moe_w_router, moe_b_router, moe_w_gate, moe_w_up, moe_w_down):
    B, S, D = x.shape
    n_tok = B * S
    lb_w = jax.nn.softmax(hgrn_lb_raw.astype(F32), axis=0)
    hgrn_lb = jnp.cumsum(lb_w, axis=0) - lb_w[0:1]
    x2 = x.reshape(n_tok, D)
    mem2 = mem.reshape(B * mem.shape[1], D)
    for layer in range(DEPTH):
        pb, pf = in_projection(x2, g_mix[layer], _rearrange_w_in(w_in[layer]), PB_COL["gates"], PB_WIDTH)
        kv = rms_matmul(mem2, g_mem[layer], w_mem_kv[layer].astype(BF16), tm=mem2.shape[0], tn=512,
                        out_dtype=BF16)
        branches = [
            neighbourhood_attention(pb, B, na_q_gain[layer], na_k_gain[layer], na_rel_bias[layer]),
            gla_branch(pb, pf, B, gla_w_gate_up[layer], gla_b_gate[layer], gla_norm_gain[layer]),
            gated_deltanet_branch(pb, pf, B, gdn_conv_w[layer], gdn_a_log[layer], gdn_dt_bias[layer],
                                  gdn_norm_gain[layer]),
            hgrn2_branch(pb, pf, B, hgrn_lb[layer], hgrn_norm_gain[layer]),
            memory_cross_attention(pb, kv, B, mem_q_gain[layer], mem_k_gain[layer]),
        ]
        merged = merge_branches(branches, pb, w_branch[layer].astype(BF16), tm=512, tn=1024)
        x2 = matmul_residual(merged, w_out[layer].astype(BF16), x2, tm=512, tn=D)

        j = layer // 2
        if layer % 2 == 0:
            act = rms_swiglu_up(x2, g_ffn[layer], ffn_w_gate[j].astype(BF16), ffn_w_up[j].astype(BF16),
                                tm=1024, tn=512)
            x2 = matmul_residual(act, ffn_w_down[j].astype(BF16), x2, tm=512, tn=1024)
        else:
            x2 = moe_layer(x2, g_ffn[layer], moe_w_router[j], moe_b_router[j], moe_w_gate[j], moe_w_up[j],
                           moe_w_down[j])
    return x2.reshape(B, S, D)
```

```python
import functools

import jax
import jax.numpy as jnp
import numpy as np
from jax import lax
from jax.experimental import pallas as pl
from jax.experimental.pallas import tpu as pltpu

F32 = jnp.float32
BF16 = jnp.bfloat16

D_MODEL = 2048
DEPTH = 2
RMS_EPS = 1e-6
MASK_VALUE = -1e30
LB_FLOOR = 1e-30
GRID_W = 64

NA_HEADS = 8
NA_HEAD_DIM = 64
NA_WIDTH = 512
NA_WIN_ROWS = 8
NA_WIN_COLS = 16

GLA_HEADS = 4
GLA_HEAD_K = 64
GLA_HEAD_V = 128
GLA_KEY_WIDTH = 256
GLA_VAL_WIDTH = 512
GLA_GATE_RANK = 16
GLA_GATE_NORMALIZER = 16.0

GDN_HEADS = 4
GDN_HEAD_K = 128
GDN_HEAD_V = 128
GDN_KEY_WIDTH = 512
GDN_VAL_WIDTH = 512
GDN_CHUNK = 64

HGRN_HEADS = 4
HGRN_HEAD_K = 128
HGRN_HEAD_V = 128
HGRN_KEY_WIDTH = 512
HGRN_VAL_WIDTH = 512

LIN_CHUNK = 32

MEM_HEADS = 4
MEM_HEAD_DIM = 128
MEM_WIDTH = 512

N_BRANCH = 5
BRANCH_WIDTH = 512
N_EXPERTS = 8
MOE_TOP_K = 2

IN_WIDTHS = (
    NA_WIDTH, NA_WIDTH, NA_WIDTH,
    GLA_KEY_WIDTH, GLA_KEY_WIDTH, GLA_VAL_WIDTH,
    2 * GLA_GATE_RANK, GLA_VAL_WIDTH,
    2 * GDN_KEY_WIDTH + GDN_VAL_WIDTH,
    2 * GDN_HEADS, 2 * GDN_HEADS, GDN_VAL_WIDTH,
    HGRN_KEY_WIDTH, 2 * HGRN_KEY_WIDTH, HGRN_VAL_WIDTH, HGRN_VAL_WIDTH,
    MEM_WIDTH,
    N_BRANCH * D_MODEL,
)
P_IN = sum(IN_WIDTHS)

V7X_VMEM_BYTES = 64 * 1024 * 1024
VMEM_LIMIT_BYTES = V7X_VMEM_BYTES - 8 * 1024 * 1024
LANES = 128


def _params(*semantics):
    return pltpu.CompilerParams(dimension_semantics=semantics, vmem_limit_bytes=VMEM_LIMIT_BYTES)


def _sigmoid(x):
    return 0.5 * jnp.tanh(0.5 * x) + 0.5


def _rms_norm_rows(x, gain):
    ms = jnp.mean(x * x, axis=-1, keepdims=True)
    return x * lax.rsqrt(ms + RMS_EPS) * gain


def _rms_matmul_kernel(x_ref, g_ref, w_ref, o_ref, h_ref):
    @pl.when(pl.program_id(1) == 0)
    def _():
        h_ref[...] = _rms_norm_rows(x_ref[...], g_ref[...]).astype(BF16)

    o_ref[...] = jnp.dot(h_ref[...], w_ref[...], preferred_element_type=F32).astype(o_ref.dtype)


def rms_matmul(x, gain, w, *, tm, tn, out_dtype=F32):
    m, k = x.shape
    n = w.shape[1]
    return pl.pallas_call(
        _rms_matmul_kernel,
        out_shape=jax.ShapeDtypeStruct((m, n), out_dtype),
        grid=(m // tm, n // tn),
        in_specs=[
            pl.BlockSpec((tm, k), lambda i, j: (i, 0)),
            pl.BlockSpec((1, k), lambda i, j: (0, 0)),
            pl.BlockSpec((k, tn), lambda i, j: (0, j)),
        ],
        out_specs=pl.BlockSpec((tm, tn), lambda i, j: (i, j)),
        scratch_shapes=[pltpu.VMEM((tm, k), BF16)],
        compiler_params=_params("parallel", "arbitrary"),
        name="rms_matmul",
    )(x, gain.reshape(1, k), w)


IN_PROJ_TILE = 512


def _in_projection_kernel(x_ref, g_ref, w_ref, ob_ref, of_ref, h_ref, *, n_plain_tiles, n_bf16_tiles):
    j = pl.program_id(1)

    @pl.when(j == 0)
    def _():
        h_ref[...] = _rms_norm_rows(x_ref[...], g_ref[...]).astype(BF16)

    r = jnp.dot(h_ref[...], w_ref[...], preferred_element_type=F32)

    @pl.when(j < n_plain_tiles)
    def _():
        ob_ref[...] = r.astype(ob_ref.dtype)

    @pl.when(jnp.logical_and(j >= n_plain_tiles, j < n_bf16_tiles))
    def _():
        ob_ref[...] = _sigmoid(r).astype(ob_ref.dtype)

    @pl.when(j >= n_bf16_tiles)
    def _():
        of_ref[...] = r


def in_projection(x, gain, w, n_plain, n_bf16, *, tm=1024):
    m, k = x.shape
    tn = IN_PROJ_TILE
    nb = n_bf16 // tn
    nf = (w.shape[1] - n_bf16) // tn
    return pl.pallas_call(
        functools.partial(_in_projection_kernel, n_plain_tiles=n_plain // tn, n_bf16_tiles=nb),
        out_shape=(jax.ShapeDtypeStruct((m, nb * tn), BF16), jax.ShapeDtypeStruct((m, nf * tn), F32)),
        grid=(m // tm, nb + nf),
        in_specs=[
            pl.BlockSpec((tm, k), lambda i, j: (i, 0)),
            pl.BlockSpec((1, k), lambda i, j: (0, 0)),
            pl.BlockSpec((k, tn), lambda i, j: (0, j)),
        ],
        out_specs=(pl.BlockSpec((tm, tn), lambda i, j: (i, jnp.minimum(j, nb - 1))),
                   pl.BlockSpec((tm, tn), lambda i, j: (i, jnp.maximum(j - nb, 0)))),
        scratch_shapes=[pltpu.VMEM((tm, k), BF16)],
        compiler_params=_params("parallel", "arbitrary"),
        name="in_projection",
    )(x, gain.reshape(1, k), w)


def _rms_swiglu_kernel(x_ref, g_ref, wg_ref, wu_ref, o_ref, h_ref):
    @pl.when(pl.program_id(1) == 0)
    def _():
        h_ref[...] = _rms_norm_rows(x_ref[...], g_ref[...]).astype(BF16)

    h = h_ref[...]
    a = jnp.dot(h, wg_ref[...], preferred_element_type=F32)
    b = jnp.dot(h, wu_ref[...], preferred_element_type=F32)
    o_ref[...] = (a * _sigmoid(a) * b).astype(o_ref.dtype)


def rms_swiglu_up(x, gain, wg, wu, *, tm, tn):
    m, k = x.shape
    n = wg.shape[1]
    return pl.pallas_call(
        _rms_swiglu_kernel,
        out_shape=jax.ShapeDtypeStruct((m, n), BF16),
        grid=(m // tm, n // tn),
        in_specs=[
            pl.BlockSpec((tm, k), lambda i, j: (i, 0)),
            pl.BlockSpec((1, k), lambda i, j: (0, 0)),
            pl.BlockSpec((k, tn), lambda i, j: (0, j)),
            pl.BlockSpec((k, tn), lambda i, j: (0, j)),
        ],
        out_specs=pl.BlockSpec((tm, tn), lambda i, j: (i, j)),
        scratch_shapes=[pltpu.VMEM((tm, k), BF16)],
        compiler_params=_params("parallel", "arbitrary"),
        name="rms_swiglu_up",
    )(x, gain.reshape(1, k), wg, wu)


def _matmul_residual_kernel(a_ref, w_ref, r_ref, o_ref):
    o_ref[...] = r_ref[...] + jnp.dot(a_ref[...], w_ref[...], preferred_element_type=F32)


def matmul_residual(a, w, res, *, tm, tn):
    m, k = a.shape
    n = w.shape[1]
    return pl.pallas_call(
        _matmul_residual_kernel,
        out_shape=jax.ShapeDtypeStruct((m, n), F32),
        grid=(m // tm, n // tn),
        in_specs=[
            pl.BlockSpec((tm, k), lambda i, j: (i, 0)),
            pl.BlockSpec((k, tn), lambda i, j: (0, j)),
            pl.BlockSpec((tm, tn), lambda i, j: (i, j)),
        ],
        out_specs=pl.BlockSpec((tm, tn), lambda i, j: (i, j)),
        compiler_params=_params("parallel", "arbitrary"),
        name="matmul_residual",
    )(a, w, res)


class RawBranch:
    def __init__(self, o_fwd, o_bwd, og_name, gain, silu_gate):
        self.o_fwd, self.o_bwd, self.og_name, self.gain, self.silu_gate = o_fwd, o_bwd, og_name, gain, silu_gate


def _merge_kernel(*refs, raw):
    pos = 0
    br = []
    for kind in raw:
        width = 1 if kind is None else 4
        br.append(refs[pos:pos + width])
        pos += width
    gl_refs = refs[pos:pos + N_BRANCH]
    wb_ref, o_ref, fin_ref = refs[pos + N_BRANCH:pos + N_BRANCH + 3]
    raw_slot = {n: s for s, n in enumerate(n for n, kind in enumerate(raw) if kind is not None)}

    @pl.when(pl.program_id(1) == 0)
    def _():
        for n, slot in raw_slot.items():
            of_ref, ob_ref, og_ref, gain_ref = br[n]
            for h in range(BRANCH_WIDTH // LANES):
                cols = slice(h * LANES, (h + 1) * LANES)
                y = _rms_norm_rows(of_ref[:, cols] + ob_ref[:, cols], gain_ref[...])
                g = og_ref[:, cols].astype(F32)
                gate = _sigmoid(g)
                if raw[n]:
                    gate = g * gate
                fin_ref[slot, :, cols] = (y * gate).astype(fin_ref.dtype)

    acc = None
    for n in range(N_BRANCH):
        b = br[n][0][...] if raw[n] is None else fin_ref[raw_slot[n]]
        y = jnp.dot(b, wb_ref[n], preferred_element_type=F32)
        t = gl_refs[n][...].astype(F32) * y
        acc = t if acc is None else acc + t
    o_ref[...] = acc.astype(o_ref.dtype)


def merge_branches(branches, pb, w_branch, *, tm, tn):
    m = pb.shape[0]
    d = D_MODEL
    tiles_per_branch = d // tn
    tile0 = PB_COL["gates"] // tn
    row_block = pl.BlockSpec((tm, BRANCH_WIDTH), lambda i, j: (i, 0))
    in_specs, operands, raw = [], [], []
    for b in branches:
        if isinstance(b, RawBranch):
            og_col = PB_COL[b.og_name] // BRANCH_WIDTH
            in_specs += [row_block, row_block,
                         pl.BlockSpec((tm, BRANCH_WIDTH), functools.partial(lambda i, j, c: (i, c), c=og_col)),
                         pl.BlockSpec((1, LANES), lambda i, j: (0, 0))]
            operands += [b.o_fwd, b.o_bwd, pb, b.gain.astype(F32).reshape(1, LANES)]
            raw.append(b.silu_gate)
        else:
            in_specs.append(row_block)
            operands.append(b)
            raw.append(None)
    in_specs += [
        pl.BlockSpec((tm, tn), functools.partial(lambda i, j, n: (i, tile0 + n * tiles_per_branch + j), n=n))
        for n in range(N_BRANCH)
    ]
    in_specs += [pl.BlockSpec((N_BRANCH, BRANCH_WIDTH, tn), lambda i, j: (0, 0, j))]
    n_raw = sum(kind is not None for kind in raw)
    return pl.pallas_call(
        functools.partial(_merge_kernel, raw=tuple(raw)),
        out_shape=jax.ShapeDtypeStruct((m, d), BF16),
        grid=(m // tm, d // tn),
        in_specs=in_specs,
        out_specs=pl.BlockSpec((tm, tn), lambda i, j: (i, j)),
        scratch_shapes=[pltpu.VMEM((max(n_raw, 1), tm, BRANCH_WIDTH), BF16)],
        compiler_params=_params("parallel", "arbitrary"),
        name="merge_branches",
    )(*operands, *([pb] * N_BRANCH), w_branch)


def _router_kernel(x_ref, g_ref, w_ref, b_ref, o_ref, h_ref, cnt_ref, run_ref, *, n_experts):
    @pl.when(pl.program_id(0) == 0)
    def _():
        run_ref[...] = jnp.zeros_like(run_ref)

    h = _rms_norm_rows(x_ref[...], g_ref[...])
    h_ref[...] = h.astype(h_ref.dtype)
    logits = _dot_f32(h, w_ref[...]) + b_ref[...]
    lane = lax.broadcasted_iota(jnp.int32, logits.shape, 1).astype(F32)
    neg = -jnp.inf
    lm = jnp.where(lane < n_experts, logits, neg)
    m1 = jnp.max(lm, axis=-1, keepdims=True)
    i1 = jnp.min(jnp.where(lm == m1, lane, float(LANES)), axis=-1, keepdims=True)
    lm2 = jnp.where(lane == i1, neg, lm)
    m2 = jnp.max(lm2, axis=-1, keepdims=True)
    i2 = jnp.min(jnp.where(lm2 == m2, lane, float(LANES)), axis=-1, keepdims=True)
    t = jnp.exp(m2 - m1)
    den = 1.0 + t

    tm = logits.shape[0]
    before = (lax.broadcasted_iota(jnp.int32, (tm, tm), 1)
              < lax.broadcasted_iota(jnp.int32, (tm, tm), 0))
    before = jnp.where(before, 1.0, 0.0).astype(BF16)
    pick1 = lane == i1
    pick2 = lane == i2
    oh1 = jnp.where(pick1, 1.0, 0.0)
    oh2 = jnp.where(pick2, 1.0, 0.0)
    pre1 = jnp.dot(before, oh1.astype(BF16), preferred_element_type=F32)
    pre2 = jnp.dot(before, oh2.astype(BF16), preferred_element_type=F32)
    tot1 = jnp.sum(oh1, axis=0, keepdims=True)
    tot2 = jnp.sum(oh2, axis=0, keepdims=True)
    run = run_ref[...]
    rank1 = jnp.sum(jnp.where(pick1, pre1 + run, 0.0), axis=-1, keepdims=True)
    rank2 = jnp.sum(jnp.where(pick2, pre2 + (run + tot1), 0.0), axis=-1, keepdims=True)
    run = run + tot1 + tot2
    run_ref[...] = run
    cnt_ref[...] = jnp.broadcast_to(run, cnt_ref.shape)

    out = jnp.where(lane == 0, 1.0 / den, jnp.where(lane == 1, t / den, jnp.where(lane == 2, i1, i2)))
    out = jnp.where(lane == 4, rank1, jnp.where(lane == 5, rank2, out))
    o_ref[...] = jnp.where(lane < 6, out, 0.0)


def router_top2(x, gain, w_router, b_router, *, tm=512):
    m, k = x.shape
    e = w_router.shape[1]
    w_pad = jnp.zeros((k, LANES), F32).at[:, :e].set(w_router.astype(F32))
    b_pad = jnp.zeros((1, LANES), F32).at[0, :e].set(b_router.astype(F32))
    route, h, cnt = pl.pallas_call(
        functools.partial(_router_kernel, n_experts=e),
        out_shape=(jax.ShapeDtypeStruct((m, LANES), F32), jax.ShapeDtypeStruct((m, k), BF16),
                   jax.ShapeDtypeStruct((8, LANES), F32)),
        grid=(m // tm,),
        in_specs=[
            pl.BlockSpec((tm, k), lambda i: (i, 0)),
            pl.BlockSpec((1, k), lambda i: (0, 0)),
            pl.BlockSpec((k, LANES), lambda i: (0, 0)),
            pl.BlockSpec((1, LANES), lambda i: (0, 0)),
        ],
        out_specs=(pl.BlockSpec((tm, LANES), lambda i: (i, 0)), pl.BlockSpec((tm, k), lambda i: (i, 0)),
                   pl.BlockSpec((8, LANES), lambda i: (0, 0))),
        scratch_shapes=[pltpu.VMEM((1, LANES), F32)],
        compiler_params=_params("arbitrary"),
        name="router_top2",
    )(x, gain.reshape(1, k), w_pad, b_pad)
    return route, h, cnt[0, :e].astype(jnp.int32)


MOE_TILE = 1024
MOE_SUB = 256
MOE_FF_TILE = 512


def _moe_kernel(tile_e_ref, tile_rows_ref, n_used_ref, x_ref, wg_ref, wu_ref, wd_ref, o_ref, acc_ref):
    i = pl.program_id(0)
    j = pl.program_id(1)
    last = pl.num_programs(1) - 1
    valid = tile_rows_ref[i]
    n_sub = (valid + (MOE_SUB - 1)) // MOE_SUB

    for k in range(1, MOE_TILE // MOE_SUB + 1):
        rows = slice(0, k * MOE_SUB)

        @pl.when(n_sub == k)
        def _(rows=rows):
            x = x_ref[rows, :]
            a = jnp.dot(x, wg_ref[0].astype(BF16), preferred_element_type=F32)
            b = jnp.dot(x, wu_ref[0].astype(BF16), preferred_element_type=F32)
            act = (a * _sigmoid(a) * b).astype(BF16)
            part = jnp.dot(act, wd_ref[0].astype(BF16), preferred_element_type=F32)

            @pl.when(j == 0)
            def _():
                acc_ref[rows, :] = part

            @pl.when(j > 0)
            def _():
                acc_ref[rows, :] += part

    for s in range(0, MOE_TILE, MOE_SUB):
        rows = slice(s, s + MOE_SUB)
        filled = s < valid

        @pl.when(jnp.logical_and(filled, j == last))
        def _(rows=rows):
            o_ref[rows, :] = acc_ref[rows, :].astype(o_ref.dtype)

        @pl.when(jnp.logical_and(jnp.logical_not(filled), j == last))
        def _(rows=rows):
            o_ref[rows, :] = jnp.zeros((MOE_SUB, o_ref.shape[1]), o_ref.dtype)


def moe_experts(xb, tile_e, tile_rows, n_used, wg, wu, wd):
    rows, d = xb.shape
    ff = wg.shape[2]
    tm, tf = MOE_TILE, MOE_FF_TILE
    n_tiles = rows // tm
    last_j = ff // tf - 1

    def x_map(i, j, te, tr, nu):
        return (jnp.minimum(i, nu[0] - 1), 0)

    def up_map(i, j, te, tr, nu):
        return (te[i], 0, jnp.where(i < nu[0], j, last_j))

    def down_map(i, j, te, tr, nu):
        return (te[i], jnp.where(i < nu[0], j, last_j), 0)

    grid_spec = pltpu.PrefetchScalarGridSpec(
        num_scalar_prefetch=3,
        grid=(n_tiles, ff // tf),
        in_specs=[
            pl.BlockSpec((tm, d), x_map),
            pl.BlockSpec((1, d, tf), up_map),
            pl.BlockSpec((1, d, tf), up_map),
            pl.BlockSpec((1, tf, d), down_map),
        ],
        out_specs=pl.BlockSpec((tm, d), lambda i, j, te, tr, nu: (i, 0)),
        scratch_shapes=[pltpu.VMEM((tm, d), F32)],
    )
    return pl.pallas_call(
        _moe_kernel,
        out_shape=jax.ShapeDtypeStruct((rows, d), BF16),
        grid_spec=grid_spec,
        compiler_params=_params("arbitrary", "arbitrary"),
        name="moe_experts",
    )(tile_e, tile_rows, n_used, xb, wg, wu, wd)


def _moe_combine_kernel(x_ref, y0_ref, y1_ref, r_ref, o_ref):
    w = r_ref[...]
    o_ref[...] = x_ref[...] + w[:, 0:1] * y0_ref[...].astype(F32) + w[:, 1:2] * y1_ref[...].astype(F32)


def moe_combine(x2d, y0, y1, route, *, tm=512):
    n, d = x2d.shape
    row_block = pl.BlockSpec((tm, d), lambda i: (i, 0))
    return pl.pallas_call(
        _moe_combine_kernel,
        out_shape=jax.ShapeDtypeStruct((n, d), F32),
        grid=(n // tm,),
        in_specs=[row_block, row_block, row_block, pl.BlockSpec((tm, LANES), lambda i: (i, 0))],
        out_specs=row_block,
        compiler_params=_params("parallel"),
        name="moe_combine",
    )(x2d, y0, y1, route)


def moe_layer(x2d, gain, w_router, b_router, wg, wu, wd):
    n, d = x2d.shape
    e = N_EXPERTS
    route, h, counts = router_top2(x2d, gain, w_router, b_router)
    nk = n * MOE_TOP_K
    n_tiles = -(-nk // MOE_TILE) + e
    flat_e = route[:, 2:2 + MOE_TOP_K].astype(jnp.int32).reshape(nk)
    rank = route[:, 4:4 + MOE_TOP_K].astype(jnp.int32).reshape(nk)
    flat_tok = jnp.repeat(jnp.arange(n, dtype=jnp.int32), MOE_TOP_K)
    padded = (counts + MOE_TILE - 1) // MOE_TILE * MOE_TILE
    pad_end = jnp.cumsum(padded)
    pad_start = pad_end - padded
    slot = (pad_start[flat_e] + rank).astype(jnp.int32)
    n_slots = n_tiles * MOE_TILE
    slot_tok = (jnp.arange(n_slots, dtype=jnp.int32) % n).at[slot].set(flat_tok)
    tile_start = jnp.arange(n_tiles, dtype=jnp.int32) * MOE_TILE
    tile_e = jnp.minimum(jnp.searchsorted(pad_end, tile_start, side="right"), e - 1).astype(jnp.int32)
    tile_rows = jnp.clip(pad_start[tile_e] + counts[tile_e] - tile_start, 0, MOE_TILE).astype(jnp.int32)
    tile_rows = jnp.where(tile_start < pad_end[-1], tile_rows, 0)
    n_used = (pad_end[-1] // MOE_TILE).astype(jnp.int32).reshape(1)
    tile_e = jnp.where(tile_start < pad_end[-1], tile_e, tile_e[jnp.maximum(n_used[0] - 1, 0)])

    xb = h[slot_tok]
    yb = moe_experts(xb, tile_e, tile_rows, n_used, wg, wu, wd)
    slot2 = slot.reshape(n, MOE_TOP_K)
    return moe_combine(x2d, yb[slot2[:, 0]], yb[slot2[:, 1]], route)


_SRC = dict(zip(
    ("na_q", "na_k", "na_v", "gla_q", "gla_k", "gla_v", "gla_lr", "gla_og", "gdn_qkv", "gdn_a", "gdn_b",
     "gdn_og", "hg_q", "hg_f", "hg_i", "hg_og", "mem_q", "gates"),
    zip(np.cumsum((0,) + IN_WIDTHS[:-1]).tolist(), IN_WIDTHS)))
_PB_ORDER = ("na_q", "na_k", "na_v", "gla_q", "gla_k", "gla_v", "gla_og", "gdn_qkv", "gdn_og", "hg_q", "hg_i",
             "hg_og", "mem_q", "gates")
_PF_ORDER = ("hg_f", "gla_lr", "gdn_a", "gdn_b")
PB_COL = {}
_c = 0
for _name in _PB_ORDER:
    PB_COL[_name] = _c
    _c += _SRC[_name][1]
PB_WIDTH = _c
PF_COL = {}
_c = 0
for _name in _PF_ORDER:
    PF_COL[_name] = _c
    _c += _SRC[_name][1]
PF_WIDTH = -(-_c // IN_PROJ_TILE) * IN_PROJ_TILE
PF_SMALL_COL = PF_COL["gla_lr"]
GDN_A_LANE = PF_COL["gdn_a"] - PF_SMALL_COL
GDN_B_LANE = PF_COL["gdn_b"] - PF_SMALL_COL


def _rearrange_w_in(w):
    w = w.astype(BF16)
    cols = [w[:, _SRC[n][0]:_SRC[n][0] + _SRC[n][1]] for n in _PB_ORDER + _PF_ORDER]
    cols.append(jnp.zeros((w.shape[0], PB_WIDTH + PF_WIDTH - P_IN), BF16))
    return jnp.concatenate(cols, axis=1)


def _segment_rms(x, gain, seg_ones, seg_width):
    sq = x * x
    hi = sq.astype(BF16)
    lo = (sq - hi.astype(F32)).astype(BF16)
    ss = (jnp.dot(hi, seg_ones, preferred_element_type=F32)
          + jnp.dot(lo, seg_ones, preferred_element_type=F32))
    return x * lax.rsqrt(ss * (1.0 / seg_width) + RMS_EPS) * gain


NA_ROWS_PER_STEP = 8
NA_BAND = NA_WIN_ROWS * GRID_W


def _na_bias_table(rel_bias):
    c = np.arange(GRID_W)
    dc = np.clip(c[None, :] - c[:, None], 1 - NA_WIN_COLS, NA_WIN_COLS - 1) + (NA_WIN_COLS - 1)
    col_start = np.clip(c - NA_WIN_COLS // 2, 0, GRID_W - NA_WIN_COLS)
    col_in = (c[None, :] >= col_start[:, None]) & (c[None, :] < col_start[:, None] + NA_WIN_COLS)
    onehot = (dc[None] == np.arange(2 * NA_WIN_COLS - 1)[:, None, None]).astype(np.float32)
    base = jnp.einsum("hrc,cqk->hrqk", rel_bias.astype(F32), onehot, precision=lax.Precision.HIGHEST)
    base = jnp.where(col_in[None, None], base, MASK_VALUE)
    tables = []
    for cfg in range(NA_WIN_ROWS):
        rows = base[:, NA_WIN_ROWS - 1 - cfg:2 * NA_WIN_ROWS - 1 - cfg]
        tables.append(rows.transpose(0, 2, 1, 3).reshape(NA_HEADS // 2, 2 * GRID_W, NA_BAND))
    return jnp.stack(tables)


def _na_kernel(q_ref, k_ref, v_ref, qg_ref, kg_ref, seg_ref, bias_ref, o_ref, kn_ref):
    step = pl.program_id(1)
    rows_total = k_ref.shape[0] // GRID_W
    seg = seg_ref[...]

    @pl.when(step == 0)
    def _():
        def norm_keys(t, carry):
            rows = pl.ds(pl.multiple_of(t * 256, 256), 256)
            kn_ref[rows, :] = _segment_rms(k_ref[rows, :].astype(F32), kg_ref[...], seg, NA_HEAD_DIM).astype(BF16)
            return carry
        lax.fori_loop(0, k_ref.shape[0] // 256, norm_keys, 0)

    lane = lax.broadcasted_iota(jnp.int32, (1, LANES), 1)
    low_half = lane < NA_HEAD_DIM

    def one_row(rr, carry):
        r = step * NA_ROWS_PER_STEP + rr
        row_start = jnp.clip(r - NA_WIN_ROWS // 2, 0, rows_total - NA_WIN_ROWS)
        cfg = r - row_start
        qrows = pl.ds(pl.multiple_of(rr * GRID_W, GRID_W), GRID_W)
        band = pl.ds(pl.multiple_of(row_start * GRID_W, GRID_W), NA_BAND)
        qn = (_segment_rms(q_ref[qrows, :].astype(F32), qg_ref[...], seg, NA_HEAD_DIM)
              * (NA_HEAD_DIM ** -0.5)).astype(BF16)
        pairs = range(NA_HEADS // 2)
        pair_cols = [slice(pair * LANES, (pair + 1) * LANES) for pair in pairs]
        scores = []
        for cols in pair_cols:
            qp = qn[:, cols]
            q2 = jnp.concatenate([jnp.where(low_half, qp, jnp.zeros_like(qp)),
                                  jnp.where(low_half, jnp.zeros_like(qp), qp)], axis=0)
            scores.append(lax.dot_general(q2, kn_ref[band, cols], (((1,), (1,)), ((), ())),
                                          preferred_element_type=F32))
        exps, sums = [], []
        for s, pair in zip(scores, pairs):
            s = s + bias_ref[cfg, pair]
            e = jnp.exp(s - jnp.max(s, axis=-1, keepdims=True))
            sums.append(jnp.sum(e, axis=-1, keepdims=True))
            exps.append(e.astype(BF16))
        outs = [jnp.dot(e, v_ref[band, cols], preferred_element_type=F32) / l
                for e, l, cols in zip(exps, sums, pair_cols)]
        for o2, cols in zip(outs, pair_cols):
            o_ref[qrows, cols] = jnp.where(low_half, o2[:GRID_W], o2[GRID_W:]).astype(o_ref.dtype)
        return carry

    lax.fori_loop(0, NA_ROWS_PER_STEP, one_row, 0)


def neighbourhood_attention(pb, batch, q_gain, k_gain, rel_bias):
    m = pb.shape[0]
    s = m // batch
    tq = NA_ROWS_PER_STEP * GRID_W
    steps = s // tq
    qg = jnp.tile(q_gain.astype(F32), NA_HEADS).reshape(1, NA_WIDTH)
    kg = jnp.tile(k_gain.astype(F32), NA_HEADS).reshape(1, NA_WIDTH)
    seg = jnp.asarray(np.kron(np.eye(NA_HEADS), np.ones((NA_HEAD_DIM, NA_HEAD_DIM))), BF16)
    bias = _na_bias_table(rel_bias)
    cq, ck, cv = (PB_COL[n] // NA_WIDTH for n in ("na_q", "na_k", "na_v"))
    return pl.pallas_call(
        _na_kernel,
        out_shape=jax.ShapeDtypeStruct((m, NA_WIDTH), BF16),
        grid=(batch, steps),
        in_specs=[
            pl.BlockSpec((tq, NA_WIDTH), lambda b, t: (b * steps + t, cq)),
            pl.BlockSpec((s, NA_WIDTH), lambda b, t: (b, ck)),
            pl.BlockSpec((s, NA_WIDTH), lambda b, t: (b, cv)),
            pl.BlockSpec((1, NA_WIDTH), lambda b, t: (0, 0)),
            pl.BlockSpec((1, NA_WIDTH), lambda b, t: (0, 0)),
            pl.BlockSpec((NA_WIDTH, NA_WIDTH), lambda b, t: (0, 0)),
            pl.BlockSpec((NA_WIN_ROWS, NA_HEADS // 2, 2 * GRID_W, NA_BAND), lambda b, t: (0, 0, 0, 0)),
        ],
        out_specs=pl.BlockSpec((tq, NA_WIDTH), lambda b, t: (b * steps + t, 0)),
        scratch_shapes=[pltpu.VMEM((s, NA_WIDTH), BF16)],
        compiler_params=_params("parallel", "arbitrary"),
        name="neighbourhood_attention",
    )(pb, pb, pb, qg, kg, seg, bias)


def _mem_attn_kernel(q_ref, kv_ref, qg_ref, kg_ref, o_ref, kn_ref):
    @pl.when(pl.program_id(1) == 0)
    def _():
        for h in range(MEM_HEADS):
            cols = slice(h * MEM_HEAD_DIM, (h + 1) * MEM_HEAD_DIM)
            kn_ref[:, cols] = _rms_norm_rows(kv_ref[:, cols].astype(F32), kg_ref[...]).astype(BF16)

    head_cols = [slice(h * MEM_HEAD_DIM, (h + 1) * MEM_HEAD_DIM) for h in range(MEM_HEADS)]
    qns = [_rms_norm_rows(q_ref[:, cols].astype(F32), qg_ref[...]).astype(BF16) for cols in head_cols]
    scores = [lax.dot_general(qn, kn_ref[:, cols], (((1,), (1,)), ((), ())), preferred_element_type=F32)
              for qn, cols in zip(qns, head_cols)]
    exps, sums = [], []
    for s in scores:
        s = s * (MEM_HEAD_DIM ** -0.5)
        e = jnp.exp(s - jnp.max(s, axis=-1, keepdims=True))
        sums.append(jnp.sum(e, axis=-1, keepdims=True))
        exps.append(e.astype(BF16))
    outs = [jnp.dot(e, kv_ref[:, MEM_WIDTH + cols.start:MEM_WIDTH + cols.stop], preferred_element_type=F32)
            for e, cols in zip(exps, head_cols)]
    for o, l, cols in zip(outs, sums, head_cols):
        o_ref[:, cols] = (o / l).astype(o_ref.dtype)


def memory_cross_attention(pb, kv, batch, q_gain, k_gain, *, tq=512):
    m = pb.shape[0]
    steps = m // batch // tq
    n_mem = kv.shape[0] // batch
    cq = PB_COL["mem_q"] // MEM_WIDTH
    return pl.pallas_call(
        _mem_attn_kernel,
        out_shape=jax.ShapeDtypeStruct((m, MEM_WIDTH), BF16),
        grid=(batch, steps),
        in_specs=[
            pl.BlockSpec((tq, MEM_WIDTH), lambda b, t: (b * steps + t, cq)),
            pl.BlockSpec((n_mem, 2 * MEM_WIDTH), lambda b, t: (b, 0)),
            pl.BlockSpec((1, MEM_HEAD_DIM), lambda b, t: (0, 0)),
            pl.BlockSpec((1, MEM_HEAD_DIM), lambda b, t: (0, 0)),
        ],
        out_specs=pl.BlockSpec((tq, MEM_WIDTH), lambda b, t: (b * steps + t, 0)),
        scratch_shapes=[pltpu.VMEM((n_mem, MEM_WIDTH), BF16)],
        compiler_params=_params("parallel", "arbitrary"),
        name="memory_cross_attention",
    )(pb, kv, q_gain.astype(F32).reshape(1, MEM_HEAD_DIM), k_gain.astype(F32).reshape(1, MEM_HEAD_DIM))


LIN_BLOCK = 512
HEAD_V = 128


def _log1p_exp_neg(t):
    return jnp.log(1.0 + jnp.exp(-t))


def _log_sigmoid(x):
    return jnp.minimum(x, 0.0) - _log1p_exp_neg(jnp.abs(x))


def _logaddexp(a, b):
    return jnp.maximum(a, b) + _log1p_exp_neg(jnp.abs(a - b))


def _split_bf16(x, terms):
    parts = []
    for _ in range(terms):
        p = x.astype(BF16)
        parts.append(p)
        x = x - p.astype(F32)
    return parts


def _dot_f32(a, b):
    a_hi, a_lo = _split_bf16(a, 2)
    b_hi, b_lo = _split_bf16(b, 2)
    return (jnp.dot(a_hi, b_hi, preferred_element_type=F32)
            + (jnp.dot(a_hi, b_lo, preferred_element_type=F32) + jnp.dot(a_lo, b_hi, preferred_element_type=F32)))


def _cumsum_rows(mask, x):
    m = jnp.where(mask, 1.0, 0.0).astype(BF16)
    hi, mid, lo = _split_bf16(x, 3)
    return (jnp.dot(m, hi, preferred_element_type=F32)
            + (jnp.dot(m, mid, preferred_element_type=F32) + jnp.dot(m, lo, preferred_element_type=F32)))


def _gla_inputs(refs, rows, direction, params):
    q_ref, k_ref, v_ref, g_ref = refs
    wpad_ref, bias_ref = params
    qc = q_ref[rows, :].astype(F32) * (GLA_HEAD_K ** -0.5)
    kc = k_ref[rows, :].astype(F32)
    gk = _dot_f32(g_ref[rows, :], wpad_ref[direction]) + bias_ref[direction]
    lg = _log_sigmoid(gk) * (1.0 / GLA_GATE_NORMALIZER)
    return qc, kc, v_ref[rows, :], lg


def _hgrn_inputs(refs, rows, direction, params):
    q_ref, v_ref, z_ref = refs
    lb_ref, log_lb_ref, log1m_lb_ref = params
    qr = q_ref[rows, :].astype(F32)
    qc = qr * _sigmoid(qr)
    z = z_ref[rows, :]
    lg = _logaddexp(log_lb_ref[direction], log1m_lb_ref[direction] + _log_sigmoid(z))
    kc = (1.0 - lb_ref[direction]) * _sigmoid(-z)
    return qc, kc, v_ref[rows, :], lg


GDN_CONV_WIDTH = 5
GDN_QKV_WIDTH = 2 * GDN_KEY_WIDTH + GDN_VAL_WIDTH
GDN_HALO = 16


def _gdn_prep_kernel(prev_ref, cur_ref, next_ref, w_ref, o_ref, xp_ref, *, blocks_per_seq):
    i = pl.program_id(0)
    t = cur_ref.shape[0]
    pos = i % blocks_per_seq
    prev = prev_ref[...].astype(F32)
    nxt = next_ref[...].astype(F32)
    xp_ref[0:GDN_HALO, :] = jnp.where(pos == 0, jnp.zeros_like(prev), prev)
    xp_ref[GDN_HALO:GDN_HALO + t, :] = cur_ref[...].astype(F32)
    xp_ref[GDN_HALO + t:, :] = jnp.where(pos == blocks_per_seq - 1, jnp.zeros_like(nxt), nxt)
    half = GDN_CONV_WIDTH // 2
    for g in range(GDN_QKV_WIDTH // LANES):
        cols = slice(g * LANES, (g + 1) * LANES)
        acc = None
        for j in range(GDN_CONV_WIDTH):
            term = xp_ref[GDN_HALO - half + j:GDN_HALO - half + j + t, cols] * w_ref[j:j + 1, cols]
            acc = term if acc is None else acc + term
        y = acc * _sigmoid(acc)
        if g < 2 * GDN_HEADS:
            y = y * lax.rsqrt(jnp.sum(y * y, axis=-1, keepdims=True) + 1e-6)
            if g < GDN_HEADS:
                y = y * (GDN_HEAD_K ** -0.5)
        o_ref[:, cols] = y.astype(o_ref.dtype)


def gdn_prep(pb, batch, conv_w, *, t=512):
    m = pb.shape[0]
    blocks_per_seq = m // batch // t
    halo_per_block = t // GDN_HALO
    col = PB_COL["gdn_qkv"] // GDN_QKV_WIDTH
    last_halo = m // GDN_HALO - 1
    return pl.pallas_call(
        functools.partial(_gdn_prep_kernel, blocks_per_seq=blocks_per_seq),
        out_shape=jax.ShapeDtypeStruct((m, GDN_QKV_WIDTH), BF16),
        grid=(m // t,),
        in_specs=[
            pl.BlockSpec((GDN_HALO, GDN_QKV_WIDTH), lambda i: (jnp.maximum(i * halo_per_block - 1, 0), col)),
            pl.BlockSpec((t, GDN_QKV_WIDTH), lambda i: (i, col)),
            pl.BlockSpec((GDN_HALO, GDN_QKV_WIDTH),
                         lambda i: (jnp.minimum((i + 1) * halo_per_block, last_halo), col)),
            pl.BlockSpec((GDN_CONV_WIDTH, GDN_QKV_WIDTH), lambda i: (0, 0)),
        ],
        out_specs=pl.BlockSpec((t, GDN_QKV_WIDTH), lambda i: (i, 0)),
        scratch_shapes=[pltpu.VMEM((t + 2 * GDN_HALO, GDN_QKV_WIDTH), F32)],
        compiler_params=_params("parallel"),
        name="gdn_prep",
    )(pb, pb, pb, conv_w.astype(F32))


def _softplus(x):
    return jnp.maximum(x, 0.0) + _log1p_exp_neg(jnp.abs(x))


GDN_PACK = GDN_HEADS * GDN_CHUNK
GDN_WY_BLOCK = 512


def _stack_heads(x, width):
    heads = x.shape[1] // width
    lane = lax.broadcasted_iota(jnp.int32, (1, x.shape[1]), 1)
    return jnp.concatenate(
        [jnp.where((lane >= h * width) & (lane < (h + 1) * width), x, 0.0).astype(BF16) for h in range(heads)],
        axis=0)


def _packed_mm(x, y):
    return jnp.dot(x.astype(BF16), _stack_heads(y, GDN_CHUNK), preferred_element_type=F32)


def _packed_inverses(mats):
    c = GDN_CHUNK
    ii = lax.broadcasted_iota(jnp.int32, (c, GDN_PACK), 0)
    jj = lax.broadcasted_iota(jnp.int32, (c, GDN_PACK), 1) % c
    eye = (ii == jj).astype(F32)

    def same_block(s):
        return (ii // s) == (jj // s)

    ds = [jnp.where(same_block(8), a, 0.0) for a in mats]
    d2s = [_packed_mm(d, d) for d in ds]
    d4s = [_packed_mm(d2, d2) for d2 in d2s]
    ts = [_packed_mm(eye - d, eye + d2) for d, d2 in zip(ds, d2s)]
    ts = [_packed_mm(t, eye + d4) for t, d4 in zip(ts, d4s)]
    s = 8
    while s < c:
        off = same_block(2 * s) & jnp.logical_not(same_block(s))
        ets = [_packed_mm(jnp.where(off, a, 0.0), t) for a, t in zip(mats, ts)]
        ts = [t - _packed_mm(t, et) for t, et in zip(ts, ets)]
        s *= 2
    return ts


def _gdn_wy_kernel(qkv_ref, small_ref, a_ref, dtb_ref, selg_ref, selk_ref, selb_ref, *out_refs):
    c = GDN_CHUNK
    n_chunks = qkv_ref.shape[0] // c
    ii = lax.broadcasted_iota(jnp.int32, (c, c), 0)
    jj = lax.broadcasted_iota(jnp.int32, (c, c), 1)
    pi = lax.broadcasted_iota(jnp.int32, (c, GDN_PACK), 0)
    pj = lax.broadcasted_iota(jnp.int32, (c, GDN_PACK), 1) % c
    eye_p = (pi == pj).astype(F32)
    ones_cc = jnp.ones((c, c), BF16)

    problems = [(ch, d) for ch in range(n_chunks) for d in range(2)]
    chunk_in = []
    for ch in range(n_chunks):
        rows = slice(ch * c, (ch + 1) * c)
        qkv = qkv_ref[rows, :]
        small = small_ref[rows, :]
        kf = qkv[:, GDN_KEY_WIDTH:2 * GDN_KEY_WIDTH].astype(F32)
        chunk_in.append(dict(
            qf=qkv[:, :GDN_KEY_WIDTH].astype(F32), kf=kf, vf=qkv[:, 2 * GDN_KEY_WIDTH:].astype(F32),
            kbd=_stack_heads(kf, HEAD_V),
            log_alpha=a_ref[...] * _softplus(small + dtb_ref[...]),
            beta_all=_sigmoid(small)))

    def sel3(x, sel):
        hi, mid, lo = _split_bf16(x, 3)
        return (jnp.dot(hi, sel, preferred_element_type=F32)
                + (jnp.dot(mid, sel, preferred_element_type=F32) + jnp.dot(lo, sel, preferred_element_type=F32)))

    g_all = [_cumsum_rows((jj >= ii) if d else (jj <= ii), chunk_in[ch]["log_alpha"]) for ch, d in problems]
    g_pack = [sel3(g, selg_ref[d]) for g, (ch, d) in zip(g_all, problems)]
    g_wide = [sel3(g, selk_ref[d]) for g, (ch, d) in zip(g_all, problems)]
    beta_w = [sel3(chunk_in[ch]["beta_all"], selb_ref[d]) for ch, d in problems]
    g_rowp = []
    for gp in g_pack:
        hi, mid, lo = _split_bf16(gp * eye_p, 3)
        g_rowp.append(jnp.dot(ones_cc, hi, preferred_element_type=F32)
                      + (jnp.dot(ones_cc, mid, preferred_element_type=F32)
                         + jnp.dot(ones_cc, lo, preferred_element_type=F32)))
    decays, k_betas = [], []
    for gp, gr, bw, (ch, d) in zip(g_pack, g_rowp, beta_w, problems):
        incl = (pj >= pi) if d else (pj <= pi)
        decays.append(jnp.where(incl, jnp.exp(jnp.where(incl, gp - gr, 0.0)), 0.0))
        k_betas.append(chunk_in[ch]["kf"] * bw)
    kq = [lax.dot_general(jnp.concatenate([kb, chunk_in[ch]["qf"]], axis=0).astype(BF16), chunk_in[ch]["kbd"],
                          (((1,), (1,)), ((), ())), preferred_element_type=F32)
          for kb, (ch, d) in zip(k_betas, problems)]
    a_mats = []
    for x, dec, (ch, d) in zip(kq, decays, problems):
        strict = (pj > pi) if d else (pj < pi)
        a_mats.append(jnp.where(strict, x[:c] * dec, 0.0))
    t_invs = _packed_inverses(a_mats)

    for idx, (ch, d) in enumerate(problems):
        u_ref, w_ref, attn_ref, qd_ref, kd_ref, gt_ref = out_refs[6 * d:6 * d + 6]
        rows = slice(ch * c, (ch + 1) * c)
        cin = chunk_in[ch]
        gw = g_wide[idx]
        eg = jnp.exp(gw)
        t_b = t_invs[idx].astype(BF16)
        u_ref[rows, :] = jnp.dot(t_b, _stack_heads(cin["vf"] * beta_w[idx], HEAD_V), preferred_element_type=F32)
        w_ref[rows, :] = jnp.dot(t_b, _stack_heads(k_betas[idx] * eg, HEAD_V),
                                 preferred_element_type=F32).astype(w_ref.dtype)
        attn_ref[rows, :] = (kq[idx][c:] * decays[idx]).astype(attn_ref.dtype)
        end = 0 if d else c - 1
        g_end = gw[end:end + 1, :]
        qd_ref[rows, :] = (cin["qf"] * eg).astype(qd_ref.dtype)
        kd_ref[rows, :] = (cin["kf"] * jnp.exp(g_end - gw)).astype(kd_ref.dtype)
        gt_ref[ch:ch + 1, :] = jnp.exp(g_end)


def gdn_wy(qkv, pf, a_scale, dtb):
    m = qkv.shape[0]
    t = GDN_WY_BLOCK
    cpb = t // GDN_CHUNK
    selg = np.zeros((2, LANES, GDN_PACK), np.float32)
    selk = np.zeros((2, LANES, GDN_VAL_WIDTH), np.float32)
    selb = np.zeros((2, LANES, GDN_VAL_WIDTH), np.float32)
    for d in range(2):
        for h in range(GDN_HEADS):
            selg[d, GDN_A_LANE + d * GDN_HEADS + h, h * GDN_CHUNK:(h + 1) * GDN_CHUNK] = 1.0
            selk[d, GDN_A_LANE + d * GDN_HEADS + h, h * HEAD_V:(h + 1) * HEAD_V] = 1.0
            selb[d, GDN_B_LANE + d * GDN_HEADS + h, h * HEAD_V:(h + 1) * HEAD_V] = 1.0
    wide = GDN_VAL_WIDTH
    out_shape, out_specs = [], []
    for _ in range(2):
        for width, dt in ((wide, F32), (wide, BF16), (GDN_PACK, BF16), (wide, BF16), (wide, BF16)):
            out_shape.append(jax.ShapeDtypeStruct((m, width), dt))
            out_specs.append(pl.BlockSpec((t, width), lambda i: (i, 0)))
        out_shape.append(jax.ShapeDtypeStruct((m // GDN_CHUNK, wide), F32))
        out_specs.append(pl.BlockSpec((cpb, wide), lambda i: (i, 0)))
    return pl.pallas_call(
        _gdn_wy_kernel,
        out_shape=tuple(out_shape),
        grid=(m // t,),
        in_specs=[
            pl.BlockSpec((t, GDN_QKV_WIDTH), lambda i: (i, 0)),
            pl.BlockSpec((t, LANES), lambda i: (i, PF_SMALL_COL // LANES)),
            pl.BlockSpec((1, LANES), lambda i: (0, 0)),
            pl.BlockSpec((1, LANES), lambda i: (0, 0)),
            pl.BlockSpec((2, LANES, GDN_PACK), lambda i: (0, 0, 0)),
            pl.BlockSpec((2, LANES, wide), lambda i: (0, 0, 0)),
            pl.BlockSpec((2, LANES, wide), lambda i: (0, 0, 0)),
        ],
        out_specs=tuple(out_specs),
        compiler_params=_params("parallel"),
        name="gdn_wy",
    )(qkv, pf, a_scale, dtb, jnp.asarray(selg, BF16), jnp.asarray(selk, BF16), jnp.asarray(selb, BF16))


GDN_PAIR = 2 * HEAD_V


def _gdn_scan_kernel(*refs):
    groups = (refs[0:6], refs[6:12])
    out_refs = refs[12:14]
    state_refs = refs[14:16]

    @pl.when(pl.program_id(1) == 0)
    def _():
        for s_ref in state_refs:
            s_ref[...] = jnp.zeros_like(s_ref)

    n_chunks = out_refs[0].shape[0] // GDN_CHUNK
    pairs = GDN_HEADS // 2
    pair_cols = [slice(p * GDN_PAIR, (p + 1) * GDN_PAIR) for p in range(pairs)]
    ri = lax.broadcasted_iota(jnp.int32, (GDN_PAIR, GDN_PAIR), 0) // HEAD_V
    ci = lax.broadcasted_iota(jnp.int32, (GDN_PAIR, GDN_PAIR), 1) // HEAD_V
    diag = ri == ci

    def body(c, carry):
        chunks = (c, n_chunks - 1 - c)
        rows = [pl.ds(pl.multiple_of(ch * GDN_CHUNK, GDN_CHUNK), GDN_CHUNK) for ch in chunks]
        states = [[s_ref[p] for p in range(pairs)] for s_ref in state_refs]
        states_b = [[s.astype(BF16) for s in st] for st in states]
        wq = [[jnp.dot(jnp.concatenate([groups[g][1][rows[g], cols], groups[g][3][rows[g], cols]], axis=0),
                       states_b[g][p], preferred_element_type=F32)
               for p, cols in enumerate(pair_cols)] for g in range(2)]
        ws = [[x[:GDN_CHUNK] for x in wq[g]] for g in range(2)]
        qs = [[x[GDN_CHUNK:] for x in wq[g]] for g in range(2)]
        v_new = [groups[g][0][rows[g], :] - jnp.concatenate(ws[g], axis=1) for g in range(2)]
        av = [jnp.dot(groups[g][2][rows[g], :], _stack_heads(v_new[g], HEAD_V), preferred_element_type=F32)
              for g in range(2)]
        v_new_b = [v.astype(BF16) for v in v_new]
        upd = [[lax.dot_general(groups[g][4][rows[g], cols], v_new_b[g][:, cols], (((0,), (0,)), ((), ())),
                                preferred_element_type=F32) for cols in pair_cols] for g in range(2)]
        for g in range(2):
            out_refs[g][rows[g], :] = jnp.concatenate(qs[g], axis=1) + av[g]
            gt = groups[g][5][pl.ds(chunks[g], 1), :]
            for p, cols in enumerate(pair_cols):
                state_refs[g][p] = states[g][p] * gt[:, cols] + jnp.where(diag, upd[g][p], 0.0)
        return carry

    lax.fori_loop(0, n_chunks, body, 0)


def gated_deltanet_branch(pb, pf, batch, conv_w, a_log, dt_bias, norm_gain):
    m = pb.shape[0]
    nb = m // batch // LIN_BLOCK
    cpb = LIN_BLOCK // GDN_CHUNK
    qkv = gdn_prep(pb, batch, conv_w)
    n_gate = 2 * GDN_HEADS
    a_scale = jnp.zeros((1, LANES), F32).at[0, GDN_A_LANE:GDN_A_LANE + n_gate].set(
        -jnp.exp(a_log.astype(F32)).reshape(n_gate))
    dtb = jnp.zeros((1, LANES), F32).at[0, GDN_A_LANE:GDN_A_LANE + n_gate].set(dt_bias.astype(F32).reshape(n_gate))
    wy = gdn_wy(qkv, pf, a_scale, dtb)
    widths = (GDN_VAL_WIDTH, GDN_VAL_WIDTH, GDN_PACK, GDN_VAL_WIDTH, GDN_VAL_WIDTH)

    def fwd(rows, width):
        return pl.BlockSpec((rows, width), lambda b, t: (b * nb + t, 0))

    def bwd(rows, width):
        return pl.BlockSpec((rows, width), lambda b, t: (b * nb + nb - 1 - t, 0))

    in_specs = [fwd(LIN_BLOCK, w) for w in widths] + [fwd(cpb, GDN_VAL_WIDTH)]
    in_specs += [bwd(LIN_BLOCK, w) for w in widths] + [bwd(cpb, GDN_VAL_WIDTH)]
    state = pltpu.VMEM((GDN_HEADS // 2, GDN_PAIR, GDN_PAIR), F32)
    o_f, o_b = pl.pallas_call(
        _gdn_scan_kernel,
        out_shape=(jax.ShapeDtypeStruct((m, GDN_VAL_WIDTH), F32), jax.ShapeDtypeStruct((m, GDN_VAL_WIDTH), F32)),
        grid=(batch, nb),
        in_specs=in_specs,
        out_specs=(fwd(LIN_BLOCK, GDN_VAL_WIDTH), bwd(LIN_BLOCK, GDN_VAL_WIDTH)),
        scratch_shapes=[state, state],
        compiler_params=_params("parallel", "arbitrary"),
        name="gdn_scan",
    )(*wy)
    return RawBranch(o_f, o_b, "gdn_og", norm_gain, True)


def _dot3(m, x):
    hi, mid, lo = _split_bf16(x, 3)
    return (jnp.dot(m, hi, preferred_element_type=F32)
            + (jnp.dot(m, mid, preferred_element_type=F32) + jnp.dot(m, lo, preferred_element_type=F32)))


LIN_CUM_ROWS = 256
LIN_SCORE_ROWS = 128


def _chunk_causal(n, chunk, reverse):
    i = lax.broadcasted_iota(jnp.int32, (n, n), 0)
    j = lax.broadcasted_iota(jnp.int32, (n, n), 1)
    return ((i // chunk) == (j // chunk)) & ((j >= i) if reverse else (j <= i))


def _lin_intra_kernel(*refs, load_inputs, n_in, n_params, heads):
    dir_refs = (refs[:n_in], refs[n_in:2 * n_in])
    params = refs[2 * n_in:2 * n_in + n_params]
    out_refs = refs[2 * n_in + n_params:]
    c = LIN_CHUNK
    dirs = (0, 1)
    loaded = [load_inputs(dir_refs[d], slice(None), d, params) for d in dirs]
    t, w = loaded[0][0].shape
    dk = w // heads
    nc = t // c
    cums = [jnp.where(_chunk_causal(LIN_CUM_ROWS, c, d == 1), 1.0, 0.0).astype(BF16) for d in dirs]
    bs = [jnp.concatenate([_dot3(cums[d], loaded[d][3][r:r + LIN_CUM_ROWS, :])
                           for r in range(0, t, LIN_CUM_ROWS)], axis=0) for d in dirs]
    qes, kes = [], []
    for d in dirs:
        oi_ref, qd_ref, kd_ref, gt_ref = out_refs[4 * d:4 * d + 4]
        qc, kc, vc, lg = loaded[d]
        b = bs[d]
        b3 = b.reshape(nc, c, w)
        mid = c - 1 - c // 2 if d else c // 2
        end = 0 if d else c - 1
        b_mid = jnp.broadcast_to(b3[:, mid:mid + 1, :], (nc, c, w)).reshape(t, w)
        b_end = jnp.broadcast_to(b3[:, end:end + 1, :], (nc, c, w)).reshape(t, w)
        qes.append((qc * jnp.exp(b - b_mid)).astype(BF16))
        kes.append((kc * jnp.exp(b_mid - b)).astype(BF16))
        qd_ref[...] = (qc * jnp.exp(b)).astype(qd_ref.dtype)
        kd_ref[...] = (kc * jnp.exp(b_end - b)).astype(kd_ref.dtype)
        gt_ref[...] = jnp.exp(b3[:, end, :])
    keeps = [_chunk_causal(LIN_SCORE_ROWS, c, d == 1) for d in dirs]
    lane = lax.broadcasted_iota(jnp.int32, (1, LANES), 1)
    for h in range(heads):
        win = slice((h * dk) // LANES * LANES, (h * dk) // LANES * LANES + LANES)
        lo = h * dk - win.start
        vcols = slice(h * HEAD_V, (h + 1) * HEAD_V)
        tiles = [(slice(r, r + LIN_SCORE_ROWS), d) for r in range(0, t, LIN_SCORE_ROWS) for d in dirs]
        scores = []
        for rows, d in tiles:
            qh = qes[d][rows, win]
            if dk < LANES:
                qh = jnp.where((lane >= lo) & (lane < lo + dk), qh, jnp.zeros_like(qh))
            scores.append(lax.dot_general(qh, kes[d][rows, win], (((1,), (1,)), ((), ())),
                                          preferred_element_type=F32))
        probs = [jnp.where(keeps[d], s, 0.0).astype(BF16) for s, (rows, d) in zip(scores, tiles)]
        for p, (rows, d) in zip(probs, tiles):
            out_refs[4 * d][rows, vcols] = jnp.dot(p, loaded[d][2][rows, vcols], preferred_element_type=F32)


def _lin_scan_kernel(*refs, heads, chunk, unroll):
    groups = (refs[0:5], refs[5:10])
    out_refs = refs[10:12]
    state_refs = refs[12:14]

    @pl.when(pl.program_id(1) == 0)
    def _():
        for s_ref in state_refs:
            s_ref[...] = jnp.zeros_like(s_ref)

    n_chunks = out_refs[0].shape[0] // chunk
    w = state_refs[0].shape[1]
    dk = w // heads
    lane = lax.broadcasted_iota(jnp.int32, (1, w), 1)
    masks = [(lane >= h * dk) & (lane < (h + 1) * dk) for h in range(heads)]

    def stack(x):
        return jnp.concatenate([jnp.where(m, x, jnp.zeros_like(x)) for m in masks], axis=0)

    def body(it, carry):
        steps = []
        for u in range(unroll):
            c = it * unroll + u
            steps += [(0, c), (1, n_chunks - 1 - c)]
        prepared = []
        for g, ch in steps:
            rows = pl.ds(pl.multiple_of(ch * chunk, chunk), chunk)
            oi_ref, qd_ref, kd_ref, v_ref, gt_ref = groups[g]
            vc = v_ref[rows, :]
            v4 = jnp.concatenate([vc[:, h * HEAD_V:(h + 1) * HEAD_V] for h in range(heads)], axis=0)
            upd = lax.dot_general(v4, stack(kd_ref[rows, :]), (((0,), (0,)), ((), ())),
                                  preferred_element_type=F32)
            prepared.append((rows, stack(qd_ref[rows, :]), upd, gt_ref[pl.ds(ch, 1), :]))
        states = [s_ref[...] for s_ref in state_refs]
        for (g, ch), (rows, q4, upd, gt) in zip(steps, prepared):
            o_inter = lax.dot_general(q4, states[g].astype(BF16), (((1,), (1,)), ((), ())),
                                      preferred_element_type=F32)
            out_refs[g][rows, :] = groups[g][0][rows, :] + jnp.concatenate(
                [o_inter[h * chunk:(h + 1) * chunk, :] for h in range(heads)], axis=1)
            states[g] = states[g] * gt + upd
        for s_ref, st in zip(state_refs, states):
            s_ref[...] = st
        return carry

    lax.fori_loop(0, n_chunks // unroll, body, 0)


def _bidir_lin_call(name, load_inputs, arrays, col_blocks, widths, params, batch, heads, key_width, v_col):
    m = arrays[0].shape[0]
    t = LIN_BLOCK
    nb = m // batch // t
    cpb = t // LIN_CHUNK
    out_w = heads * HEAD_V
    n_in = len(arrays)

    in_specs, operands = [], []
    for d in range(2):
        for a, wd, cb in zip(arrays, widths, col_blocks):
            in_specs.append(pl.BlockSpec((t, wd), functools.partial(lambda i, c: (i, c), c=cb[d])))
            operands.append(a)
    for p in params:
        in_specs.append(pl.BlockSpec(p.shape, functools.partial(lambda i, nd: (0,) * nd, nd=p.ndim)))
    out_shape, out_specs = [], []
    for _ in range(2):
        for rows_total, rows_blk, width, dt in ((m, t, out_w, F32), (m, t, key_width, BF16),
                                                (m, t, key_width, BF16), (m // LIN_CHUNK, cpb, key_width, F32)):
            out_shape.append(jax.ShapeDtypeStruct((rows_total, width), dt))
            out_specs.append(pl.BlockSpec((rows_blk, width), lambda i: (i, 0)))

    intra = pl.pallas_call(
        functools.partial(_lin_intra_kernel, load_inputs=load_inputs, n_in=n_in, n_params=len(params),
                          heads=heads),
        out_shape=tuple(out_shape),
        grid=(m // t,),
        in_specs=in_specs,
        out_specs=tuple(out_specs),
        compiler_params=_params("parallel"),
        name=name + "_intra",
    )(*operands, *params)

    def fwd(rows, width, col=0):
        return pl.BlockSpec((rows, width), lambda b, s: (b * nb + s, col))

    def bwd(rows, width, col=0):
        return pl.BlockSpec((rows, width), lambda b, s: (b * nb + nb - 1 - s, col))

    scan_specs, scan_ops = [], []
    for d, mk in enumerate((fwd, bwd)):
        oi, qd, kd, gt = intra[4 * d:4 * d + 4]
        scan_specs += [mk(t, out_w), mk(t, key_width), mk(t, key_width), mk(t, out_w, v_col), mk(cpb, key_width)]
        scan_ops += [oi, qd, kd, arrays[0], gt]
    state = pltpu.VMEM((HEAD_V, key_width), F32)
    return pl.pallas_call(
        functools.partial(_lin_scan_kernel, heads=heads, chunk=LIN_CHUNK, unroll=4),
        out_shape=(jax.ShapeDtypeStruct((m, out_w), F32), jax.ShapeDtypeStruct((m, out_w), F32)),
        grid=(batch, nb),
        in_specs=scan_specs,
        out_specs=(fwd(t, out_w), bwd(t, out_w)),
        scratch_shapes=[state, state],
        compiler_params=_params("parallel", "arbitrary"),
        name=name + "_scan",
    )(*scan_ops)


def gla_branch(pb, pf, batch, w_gate_up, b_gate, norm_gain):
    wpad = jnp.zeros((2, LANES, GLA_KEY_WIDTH), F32)
    for d in range(2):
        wpad = wpad.at[d, d * GLA_GATE_RANK:(d + 1) * GLA_GATE_RANK, :].set(w_gate_up[d].astype(F32))
    bias = b_gate.astype(F32).reshape(2, 1, GLA_KEY_WIDTH)
    v_col = PB_COL["gla_v"] // GLA_VAL_WIDTH
    cols = [(PB_COL["gla_q"] // GLA_KEY_WIDTH,) * 2, (PB_COL["gla_k"] // GLA_KEY_WIDTH,) * 2,
            (v_col,) * 2, (PF_SMALL_COL // LANES,) * 2]
    o_f, o_b = _bidir_lin_call("gla", _gla_inputs, [pb, pb, pb, pf], cols,
                               [GLA_KEY_WIDTH, GLA_KEY_WIDTH, GLA_VAL_WIDTH, LANES], [wpad, bias],
                               batch, GLA_HEADS, GLA_KEY_WIDTH, v_col)
    return RawBranch(o_f, o_b, "gla_og", norm_gain, True)


def hgrn2_branch(pb, pf, batch, lower_bound, norm_gain):
    lb = lower_bound.astype(F32).reshape(2, 1, HGRN_KEY_WIDTH)
    log_lb = jnp.log(jnp.maximum(lb, LB_FLOOR))
    log1m_lb = jnp.log1p(-lb)
    zc = PF_COL["hg_f"] // HGRN_KEY_WIDTH
    v_col = PB_COL["hg_i"] // HGRN_VAL_WIDTH
    cols = [(PB_COL["hg_q"] // HGRN_KEY_WIDTH,) * 2, (v_col,) * 2, (zc, zc + 1)]
    o_f, o_b = _bidir_lin_call("hgrn2", _hgrn_inputs, [pb, pb, pf], cols,
                               [HGRN_KEY_WIDTH, HGRN_VAL_WIDTH, HGRN_KEY_WIDTH], [lb, log_lb, log1m_lb],
                               batch, HGRN_HEADS, HGRN_KEY_WIDTH, v_col)
    return RawBranch(o_f, o_b, "hg_og", norm_gain, False)


def kernel(x, mem, g_mix, w_in, na_q_gain, na_k_gain, na_rel_bias, gla_w_gate_up, gla_b_gate, gla_norm_gain, gdn_conv_w, gdn_a_log, gdn_dt_bias, gdn_norm_gain, hgrn_lb_raw, hgrn_norm_gain, g_mem, w_mem_kv, mem_q_gain, mem_k_gain, w_branch, w_out, g_ffn, ffn_w_gate, ffn_w_up, ffn_w_down, moe_w_router, moe_b_router, moe_w_gate, moe_w_up, moe_w_down):
    B, S, D = x.shape
    n_tok = B * S
    lb_w = jax.nn.softmax(hgrn_lb_raw.astype(F32), axis=0)
    hgrn_lb = jnp.cumsum(lb_w, axis=0) - lb_w[0:1]
    x2 = x.reshape(n_tok, D)
    mem2 = mem.reshape(B * mem.shape[1], D)
    for layer in range(DEPTH):
        pb, pf = in_projection(x2, g_mix[layer], _rearrange_w_in(w_in[layer]), PB_COL["gates"], PB_WIDTH)
        kv = rms_matmul(mem2, g_mem[layer], w_mem_kv[layer].astype(BF16), tm=mem2.shape[0], tn=512,
                        out_dtype=BF16)
        branches = [
            neighbourhood_attention(pb, B, na_q_gain[layer], na_k_gain[layer], na_rel_bias[layer]),
            gla_branch(pb, pf, B, gla_w_gate_up[layer], gla_b_gate[layer], gla_norm_gain[layer]),
            gated_deltanet_branch(pb, pf, B, gdn_conv_w[layer], gdn_a_log[layer], gdn_dt_bias[layer],
                                  gdn_norm_gain[layer]),
            hgrn2_branch(pb, pf, B, hgrn_lb[layer], hgrn_norm_gain[layer]),
            memory_cross_attention(pb, kv, B, mem_q_gain[layer], mem_k_gain[layer]),
        ]
        merged = merge_branches(branches, pb, w_branch[layer].astype(BF16), tm=512, tn=1024)
        x2 = matmul_residual(merged, w_out[layer].astype(BF16), x2, tm=512, tn=D)

        j = layer // 2
        if layer % 2 == 0:
            act = rms_swiglu_up(x2, g_ffn[layer], ffn_w_gate[j].astype(BF16), ffn_w_up[j].astype(BF16),
                                tm=1024, tn=512)
            x2 = matmul_residual(act, ffn_w_down[j].astype(BF16), x2, tm=512, tn=1024)
        else:
            x2 = moe_layer(x2, g_ffn[layer], moe_w_router[j], moe_b_router[j], moe_w_gate[j], moe_w_up[j],
                           moe_w_down[j])
    return x2.reshape(B, S, D)
```

```python
import functools

import jax
import jax.numpy as jnp
import numpy as np
from jax import lax
from jax.experimental import pallas as pl
from jax.experimental.pallas import tpu as pltpu

F32 = jnp.float32
BF16 = jnp.bfloat16

D_MODEL = 2048
DEPTH = 2
RMS_EPS = 1e-6
MASK_VALUE = -1e30
LB_FLOOR = 1e-30
GRID_W = 64

NA_HEADS = 8
NA_HEAD_DIM = 64
NA_WIDTH = 512
NA_WIN_ROWS = 8
NA_WIN_COLS = 16

GLA_HEADS = 4
GLA_HEAD_K = 64
GLA_HEAD_V = 128
GLA_KEY_WIDTH = 256
GLA_VAL_WIDTH = 512
GLA_GATE_RANK = 16
GLA_GATE_NORMALIZER = 16.0

GDN_HEADS = 4
GDN_HEAD_K = 128
GDN_HEAD_V = 128
GDN_KEY_WIDTH = 512
GDN_VAL_WIDTH = 512
GDN_CHUNK = 64

HGRN_HEADS = 4
HGRN_HEAD_K = 128
HGRN_HEAD_V = 128
HGRN_KEY_WIDTH = 512
HGRN_VAL_WIDTH = 512

LIN_CHUNK = 32

MEM_HEADS = 4
MEM_HEAD_DIM = 128
MEM_WIDTH = 512

N_BRANCH = 5
BRANCH_WIDTH = 512
N_EXPERTS = 8
MOE_TOP_K = 2

IN_WIDTHS = (
    NA_WIDTH, NA_WIDTH, NA_WIDTH,
    GLA_KEY_WIDTH, GLA_KEY_WIDTH, GLA_VAL_WIDTH,
    2 * GLA_GATE_RANK, GLA_VAL_WIDTH,
    2 * GDN_KEY_WIDTH + GDN_VAL_WIDTH,
    2 * GDN_HEADS, 2 * GDN_HEADS, GDN_VAL_WIDTH,
    HGRN_KEY_WIDTH, 2 * HGRN_KEY_WIDTH, HGRN_VAL_WIDTH, HGRN_VAL_WIDTH,
    MEM_WIDTH,
    N_BRANCH * D_MODEL,
)
P_IN = sum(IN_WIDTHS)

V7X_VMEM_BYTES = 64 * 1024 * 1024
VMEM_LIMIT_BYTES = V7X_VMEM_BYTES - 8 * 1024 * 1024
LANES = 128


def _params(*semantics):
    return pltpu.CompilerParams(dimension_semantics=semantics, vmem_limit_bytes=VMEM_LIMIT_BYTES)


def _sigmoid(x):
    return 0.5 * jnp.tanh(0.5 * x) + 0.5


def _rms_norm_rows(x, gain):
    ms = jnp.mean(x * x, axis=-1, keepdims=True)
    return x * lax.rsqrt(ms + RMS_EPS) * gain


def _rms_matmul_kernel(x_ref, g_ref, w_ref, o_ref, h_ref):
    @pl.when(pl.program_id(1) == 0)
    def _():
        h_ref[...] = _rms_norm_rows(x_ref[...], g_ref[...]).astype(BF16)

    o_ref[...] = jnp.dot(h_ref[...], w_ref[...], preferred_element_type=F32).astype(o_ref.dtype)


def rms_matmul(x, gain, w, *, tm, tn, out_dtype=F32):
    m, k = x.shape
    n = w.shape[1]
    return pl.pallas_call(
        _rms_matmul_kernel,
        out_shape=jax.ShapeDtypeStruct((m, n), out_dtype),
        grid=(m // tm, n // tn),
        in_specs=[
            pl.BlockSpec((tm, k), lambda i, j: (i, 0)),
            pl.BlockSpec((1, k), lambda i, j: (0, 0)),
            pl.BlockSpec((k, tn), lambda i, j: (0, j)),
        ],
        out_specs=pl.BlockSpec((tm, tn), lambda i, j: (i, j)),
        scratch_shapes=[pltpu.VMEM((tm, k), BF16)],
        compiler_params=_params("parallel", "arbitrary"),
        name="rms_matmul",
    )(x, gain.reshape(1, k), w)


IN_PROJ_TILE = 512


def _in_projection_kernel(x_ref, g_ref, w_ref, ob_ref, of_ref, h_ref, *, n_plain_tiles, n_bf16_tiles):
    j = pl.program_id(1)

    @pl.when(j == 0)
    def _():
        h_ref[...] = _rms_norm_rows(x_ref[...], g_ref[...]).astype(BF16)

    r = jnp.dot(h_ref[...], w_ref[...], preferred_element_type=F32)

    @pl.when(j < n_plain_tiles)
    def _():
        ob_ref[...] = r.astype(ob_ref.dtype)

    @pl.when(jnp.logical_and(j >= n_plain_tiles, j < n_bf16_tiles))
    def _():
        ob_ref[...] = _sigmoid(r).astype(ob_ref.dtype)

    @pl.when(j >= n_bf16_tiles)
    def _():
        of_ref[...] = r


def in_projection(x, gain, w, n_plain, n_bf16, *, tm=1024):
    m, k = x.shape
    tn = IN_PROJ_TILE
    nb = n_bf16 // tn
    nf = (w.shape[1] - n_bf16) // tn
    return pl.pallas_call(
        functools.partial(_in_projection_kernel, n_plain_tiles=n_plain // tn, n_bf16_tiles=nb),
        out_shape=(jax.ShapeDtypeStruct((m, nb * tn), BF16), jax.ShapeDtypeStruct((m, nf * tn), F32)),
        grid=(m // tm, nb + nf),
        in_specs=[
            pl.BlockSpec((tm, k), lambda i, j: (i, 0)),
            pl.BlockSpec((1, k), lambda i, j: (0, 0)),
            pl.BlockSpec((k, tn), lambda i, j: (0, j)),
        ],
        out_specs=(pl.BlockSpec((tm, tn), lambda i, j: (i, jnp.minimum(j, nb - 1))),
                   pl.BlockSpec((tm, tn), lambda i, j: (i, jnp.maximum(j - nb, 0)))),
        scratch_shapes=[pltpu.VMEM((tm, k), BF16)],
        compiler_params=_params("parallel", "arbitrary"),
        name="in_projection",
    )(x, gain.reshape(1, k), w)


def _rms_swiglu_kernel(x_ref, g_ref, wg_ref, wu_ref, o_ref, h_ref):
    @pl.when(pl.program_id(1) == 0)
    def _():
        h_ref[...] = _rms_norm_rows(x_ref[...], g_ref[...]).astype(BF16)

    h = h_ref[...]
    a = jnp.dot(h, wg_ref[...], preferred_element_type=F32)
    b = jnp.dot(h, wu_ref[...], preferred_element_type=F32)
    o_ref[...] = (a * _sigmoid(a) * b).astype(o_ref.dtype)


def rms_swiglu_up(x, gain, wg, wu, *, tm, tn):
    m, k = x.shape
    n = wg.shape[1]
    return pl.pallas_call(
        _rms_swiglu_kernel,
        out_shape=jax.ShapeDtypeStruct((m, n), BF16),
        grid=(m // tm, n // tn),
        in_specs=[
            pl.BlockSpec((tm, k), lambda i, j: (i, 0)),
            pl.BlockSpec((1, k), lambda i, j: (0, 0)),
            pl.BlockSpec((k, tn), lambda i, j: (0, j)),
            pl.BlockSpec((k, tn), lambda i, j: (0, j)),
        ],
        out_specs=pl.BlockSpec((tm, tn), lambda i, j: (i, j)),
        scratch_shapes=[pltpu.VMEM((tm, k), BF16)],
        compiler_params=_params("parallel", "arbitrary"),
        name="rms_swiglu_up",
    )(x, gain.reshape(1, k), wg, wu)


def _matmul_residual_kernel(a_ref, w_ref, r_ref, o_ref):
    o_ref[...] = r_ref[...] + jnp.dot(a_ref[...], w_ref[...], preferred_element_type=F32)


def matmul_residual(a, w, res, *, tm, tn):
    m, k = a.shape
    n = w.shape[1]
    return pl.pallas_call(
        _matmul_residual_kernel,
        out_shape=jax.ShapeDtypeStruct((m, n), F32),
        grid=(m // tm, n // tn),
        in_specs=[
            pl.BlockSpec((tm, k), lambda i, j: (i, 0)),
            pl.BlockSpec((k, tn), lambda i, j: (0, j)),
            pl.BlockSpec((tm, tn), lambda i, j: (i, j)),
        ],
        out_specs=pl.BlockSpec((tm, tn), lambda i, j: (i, j)),
        compiler_params=_params("parallel", "arbitrary"),
        name="matmul_residual",
    )(a, w, res)


class RawBranch:
    def __init__(self, o_fwd, o_bwd, og_name, gain, silu_gate):
        self.o_fwd, self.o_bwd, self.og_name, self.gain, self.silu_gate = o_fwd, o_bwd, og_name, gain, silu_gate


def _merge_kernel(*refs, raw):
    pos = 0
    br = []
    for kind in raw:
        width = 1 if kind is None else 4
        br.append(refs[pos:pos + width])
        pos += width
    gl_refs = refs[pos:pos + N_BRANCH]
    wb_ref, o_ref, fin_ref = refs[pos + N_BRANCH:pos + N_BRANCH + 3]
    raw_slot = {n: s for s, n in enumerate(n for n, kind in enumerate(raw) if kind is not None)}

    @pl.when(pl.program_id(1) == 0)
    def _():
        for n, slot in raw_slot.items():
            of_ref, ob_ref, og_ref, gain_ref = br[n]
            for h in range(BRANCH_WIDTH // LANES):
                cols = slice(h * LANES, (h + 1) * LANES)
                y = _rms_norm_rows(of_ref[:, cols] + ob_ref[:, cols], gain_ref[...])
                g = og_ref[:, cols].astype(F32)
                gate = _sigmoid(g)
                if raw[n]:
                    gate = g * gate
                fin_ref[slot, :, cols] = (y * gate).astype(fin_ref.dtype)

    acc = None
    for n in range(N_BRANCH):
        b = br[n][0][...] if raw[n] is None else fin_ref[raw_slot[n]]
        y = jnp.dot(b, wb_ref[n], preferred_element_type=F32)
        t = gl_refs[n][...].astype(F32) * y
        acc = t if acc is None else acc + t
    o_ref[...] = acc.astype(o_ref.dtype)


def merge_branches(branches, pb, w_branch, *, tm, tn):
    m = pb.shape[0]
    d = D_MODEL
    tiles_per_branch = d // tn
    tile0 = PB_COL["gates"] // tn
    row_block = pl.BlockSpec((tm, BRANCH_WIDTH), lambda i, j: (i, 0))
    in_specs, operands, raw = [], [], []
    for b in branches:
        if isinstance(b, RawBranch):
            og_col = PB_COL[b.og_name] // BRANCH_WIDTH
            in_specs += [row_block, row_block,
                         pl.BlockSpec((tm, BRANCH_WIDTH), functools.partial(lambda i, j, c: (i, c), c=og_col)),
                         pl.BlockSpec((1, LANES), lambda i, j: (0, 0))]
            operands += [b.o_fwd, b.o_bwd, pb, b.gain.astype(F32).reshape(1, LANES)]
            raw.append(b.silu_gate)
        else:
            in_specs.append(row_block)
            operands.append(b)
            raw.append(None)
    in_specs += [
        pl.BlockSpec((tm, tn), functools.partial(lambda i, j, n: (i, tile0 + n * tiles_per_branch + j), n=n))
        for n in range(N_BRANCH)
    ]
    in_specs += [pl.BlockSpec((N_BRANCH, BRANCH_WIDTH, tn), lambda i, j: (0, 0, j))]
    n_raw = sum(kind is not None for kind in raw)
    return pl.pallas_call(
        functools.partial(_merge_kernel, raw=tuple(raw)),
        out_shape=jax.ShapeDtypeStruct((m, d), BF16),
        grid=(m // tm, d // tn),
        in_specs=in_specs,
        out_specs=pl.BlockSpec((tm, tn), lambda i, j: (i, j)),
        scratch_shapes=[pltpu.VMEM((max(n_raw, 1), tm, BRANCH_WIDTH), BF16)],
        compiler_params=_params("parallel", "arbitrary"),
        name="merge_branches",
    )(*operands, *([pb] * N_BRANCH), w_branch)


def _router_kernel(x_ref, g_ref, w_ref, b_ref, o_ref, h_ref, cnt_ref, run_ref, *, n_experts):
    @pl.when(pl.program_id(0) == 0)
    def _():
        run_ref[...] = jnp.zeros_like(run_ref)

    h = _rms_norm_rows(x_ref[...], g_ref[...])
    h_ref[...] = h.astype(h_ref.dtype)
    logits = _dot_f32(h, w_ref[...]) + b_ref[...]
    lane = lax.broadcasted_iota(jnp.int32, logits.shape, 1).astype(F32)
    neg = -jnp.inf
    lm = jnp.where(lane < n_experts, logits, neg)
    m1 = jnp.max(lm, axis=-1, keepdims=True)
    i1 = jnp.min(jnp.where(lm == m1, lane, float(LANES)), axis=-1, keepdims=True)
    lm2 = jnp.where(lane == i1, neg, lm)
    m2 = jnp.max(lm2, axis=-1, keepdims=True)
    i2 = jnp.min(jnp.where(lm2 == m2, lane, float(LANES)), axis=-1, keepdims=True)
    t = jnp.exp(m2 - m1)
    den = 1.0 + t

    tm = logits.shape[0]
    before = (lax.broadcasted_iota(jnp.int32, (tm, tm), 1)
              < lax.broadcasted_iota(jnp.int32, (tm, tm), 0))
    before = jnp.where(before, 1.0, 0.0).astype(BF16)
    pick1 = lane == i1
    pick2 = lane == i2
    oh1 = jnp.where(pick1, 1.0, 0.0)
    oh2 = jnp.where(pick2, 1.0, 0.0)
    pre1 = jnp.dot(before, oh1.astype(BF16), preferred_element_type=F32)
    pre2 = jnp.dot(before, oh2.astype(BF16), preferred_element_type=F32)
    tot1 = jnp.sum(oh1, axis=0, keepdims=True)
    tot2 = jnp.sum(oh2, axis=0, keepdims=True)
    run = run_ref[...]
    rank1 = jnp.sum(jnp.where(pick1, pre1 + run, 0.0), axis=-1, keepdims=True)
    rank2 = jnp.sum(jnp.where(pick2, pre2 + (run + tot1), 0.0), axis=-1, keepdims=True)
    run = run + tot1 + tot2
    run_ref[...] = run
    cnt_ref[...] = jnp.broadcast_to(run, cnt_ref.shape)

    out = jnp.where(lane == 0, 1.0 / den, jnp.where(lane == 1, t / den, jnp.where(lane == 2, i1, i2)))
    out = jnp.where(lane == 4, rank1, jnp.where(lane == 5, rank2, out))
    o_ref[...] = jnp.where(lane < 6, out, 0.0)


def router_top2(x, gain, w_router, b_router, *, tm=512):
    m, k = x.shape
    e = w_router.shape[1]
    w_pad = jnp.zeros((k, LANES), F32).at[:, :e].set(w_router.astype(F32))
    b_pad = jnp.zeros((1, LANES), F32).at[0, :e].set(b_router.astype(F32))
    route, h, cnt = pl.pallas_call(
        functools.partial(_router_kernel, n_experts=e),
        out_shape=(jax.ShapeDtypeStruct((m, LANES), F32), jax.ShapeDtypeStruct((m, k), BF16),
                   jax.ShapeDtypeStruct((8, LANES), F32)),
        grid=(m // tm,),
        in_specs=[
            pl.BlockSpec((tm, k), lambda i: (i, 0)),
            pl.BlockSpec((1, k), lambda i: (0, 0)),
            pl.BlockSpec((k, LANES), lambda i: (0, 0)),
            pl.BlockSpec((1, LANES), lambda i: (0, 0)),
        ],
        out_specs=(pl.BlockSpec((tm, LANES), lambda i: (i, 0)), pl.BlockSpec((tm, k), lambda i: (i, 0)),
                   pl.BlockSpec((8, LANES), lambda i: (0, 0))),
        scratch_shapes=[pltpu.VMEM((1, LANES), F32)],
        compiler_params=_params("arbitrary"),
        name="router_top2",
    )(x, gain.reshape(1, k), w_pad, b_pad)
    return route, h, cnt[0, :e].astype(jnp.int32)


MOE_TILE = 1024
MOE_SUB = 256
MOE_FF_TILE = 512


def _moe_kernel(tile_e_ref, tile_rows_ref, n_used_ref, x_ref, wg_ref, wu_ref, wd_ref, o_ref, acc_ref):
    i = pl.program_id(0)
    j = pl.program_id(1)
    last = pl.num_programs(1) - 1
    valid = tile_rows_ref[i]
    n_sub = (valid + (MOE_SUB - 1)) // MOE_SUB

    for k in range(1, MOE_TILE // MOE_SUB + 1):
        rows = slice(0, k * MOE_SUB)

        @pl.when(n_sub == k)
        def _(rows=rows):
            x = x_ref[rows, :]
            a = jnp.dot(x, wg_ref[0].astype(BF16), preferred_element_type=F32)
            b = jnp.dot(x, wu_ref[0].astype(BF16), preferred_element_type=F32)
            act = (a * _sigmoid(a) * b).astype(BF16)
            part = jnp.dot(act, wd_ref[0].astype(BF16), preferred_element_type=F32)

            @pl.when(j == 0)
            def _():
                acc_ref[rows, :] = part

            @pl.when(j > 0)
            def _():
                acc_ref[rows, :] += part

    for s in range(0, MOE_TILE, MOE_SUB):
        rows = slice(s, s + MOE_SUB)
        filled = s < valid

        @pl.when(jnp.logical_and(filled, j == last))
        def _(rows=rows):
            o_ref[rows, :] = acc_ref[rows, :].astype(o_ref.dtype)

        @pl.when(jnp.logical_and(jnp.logical_not(filled), j == last))
        def _(rows=rows):
            o_ref[rows, :] = jnp.zeros((MOE_SUB, o_ref.shape[1]), o_ref.dtype)


def moe_experts(xb, tile_e, tile_rows, n_used, wg, wu, wd):
    rows, d = xb.shape
    ff = wg.shape[2]
    tm, tf = MOE_TILE, MOE_FF_TILE
    n_tiles = rows // tm
    last_j = ff // tf - 1

    def x_map(i, j, te, tr, nu):
        return (jnp.minimum(i, nu[0] - 1), 0)

    def up_map(i, j, te, tr, nu):
        return (te[i], 0, jnp.where(i < nu[0], j, last_j))

    def down_map(i, j, te, tr, nu):
        return (te[i], jnp.where(i < nu[0], j, last_j), 0)

    grid_spec = pltpu.PrefetchScalarGridSpec(
        num_scalar_prefetch=3,
        grid=(n_tiles, ff // tf),
        in_specs=[
            pl.BlockSpec((tm, d), x_map),
            pl.BlockSpec((1, d, tf), up_map),
            pl.BlockSpec((1, d, tf), up_map),
            pl.BlockSpec((1, tf, d), down_map),
        ],
        out_specs=pl.BlockSpec((tm, d), lambda i, j, te, tr, nu: (i, 0)),
        scratch_shapes=[pltpu.VMEM((tm, d), F32)],
    )
    return pl.pallas_call(
        _moe_kernel,
        out_shape=jax.ShapeDtypeStruct((rows, d), BF16),
        grid_spec=grid_spec,
        compiler_params=_params("arbitrary", "arbitrary"),
        name="moe_experts",
    )(tile_e, tile_rows, n_used, xb, wg, wu, wd)


def _moe_combine_kernel(x_ref, y0_ref, y1_ref, r_ref, o_ref):
    w = r_ref[...]
    o_ref[...] = x_ref[...] + w[:, 0:1] * y0_ref[...].astype(F32) + w[:, 1:2] * y1_ref[...].astype(F32)


def moe_combine(x2d, y0, y1, route, *, tm=512):
    n, d = x2d.shape
    row_block = pl.BlockSpec((tm, d), lambda i: (i, 0))
    return pl.pallas_call(
        _moe_combine_kernel,
        out_shape=jax.ShapeDtypeStruct((n, d), F32),
        grid=(n // tm,),
        in_specs=[row_block, row_block, row_block, pl.BlockSpec((tm, LANES), lambda i: (i, 0))],
        out_specs=row_block,
        compiler_params=_params("parallel"),
        name="moe_combine",
    )(x2d, y0, y1, route)


def moe_layer(x2d, gain, w_router, b_router, wg, wu, wd):
    n, d = x2d.shape
    e = N_EXPERTS
    route, h, counts = router_top2(x2d, gain, w_router, b_router)
    nk = n * MOE_TOP_K
    n_tiles = -(-nk // MOE_TILE) + e
    flat_e = route[:, 2:2 + MOE_TOP_K].astype(jnp.int32).reshape(nk)
    rank = route[:, 4:4 + MOE_TOP_K].astype(jnp.int32).reshape(nk)
    flat_tok = jnp.repeat(jnp.arange(n, dtype=jnp.int32), MOE_TOP_K)
    padded = (counts + MOE_TILE - 1) // MOE_TILE * MOE_TILE
    pad_end = jnp.cumsum(padded)
    pad_start = pad_end - padded
    slot = (pad_start[flat_e] + rank).astype(jnp.int32)
    n_slots = n_tiles * MOE_TILE
    slot_tok = (jnp.arange(n_slots, dtype=jnp.int32) % n).at[slot].set(flat_tok)
    tile_start = jnp.arange(n_tiles, dtype=jnp.int32) * MOE_TILE
    tile_e = jnp.minimum(jnp.searchsorted(pad_end, tile_start, side="right"), e - 1).astype(jnp.int32)
    tile_rows = jnp.clip(pad_start[tile_e] + counts[tile_e] - tile_start, 0, MOE_TILE).astype(jnp.int32)
    tile_rows = jnp.where(tile_start < pad_end[-1], tile_rows, 0)
    n_used = (pad_end[-1] // MOE_TILE).astype(jnp.int32).reshape(1)
    tile_e = jnp.where(tile_start < pad_end[-1], tile_e, tile_e[jnp.maximum(n_used[0] - 1, 0)])

    xb = h[slot_tok]
    yb = moe_experts(xb, tile_e, tile_rows, n_used, wg, wu, wd)
    slot2 = slot.reshape(n, MOE_TOP_K)
    return moe_combine(x2d, yb[slot2[:, 0]], yb[slot2[:, 1]], route)


_SRC = dict(zip(
    ("na_q", "na_k", "na_v", "gla_q", "gla_k", "gla_v", "gla_lr", "gla_og", "gdn_qkv", "gdn_a", "gdn_b",
     "gdn_og", "hg_q", "hg_f", "hg_i", "hg_og", "mem_q", "gates"),
    zip(np.cumsum((0,) + IN_WIDTHS[:-1]).tolist(), IN_WIDTHS)))
_PB_ORDER = ("na_q", "na_k", "na_v", "gla_q", "gla_k", "gla_v", "gla_og", "gdn_qkv", "gdn_og", "hg_q", "hg_i",
             "hg_og", "mem_q", "gates")
_PF_ORDER = ("hg_f", "gla_lr", "gdn_a", "gdn_b")
PB_COL = {}
_c = 0
for _name in _PB_ORDER:
    PB_COL[_name] = _c
    _c += _SRC[_name][1]
PB_WIDTH = _c
PF_COL = {}
_c = 0
for _name in _PF_ORDER:
    PF_COL[_name] = _c
    _c += _SRC[_name][1]
PF_WIDTH = -(-_c // IN_PROJ_TILE) * IN_PROJ_TILE
PF_SMALL_COL = PF_COL["gla_lr"]
GDN_A_LANE = PF_COL["gdn_a"] - PF_SMALL_COL
GDN_B_LANE = PF_COL["gdn_b"] - PF_SMALL_COL


def _rearrange_w_in(w):
    w = w.astype(BF16)
    cols = [w[:, _SRC[n][0]:_SRC[n][0] + _SRC[n][1]] for n in _PB_ORDER + _PF_ORDER]
    cols.append(jnp.zeros((w.shape[0], PB_WIDTH + PF_WIDTH - P_IN), BF16))
    return jnp.concatenate(cols, axis=1)


def _segment_rms(x, gain, seg_ones, seg_width):
    sq = x * x
    hi = sq.astype(BF16)
    lo = (sq - hi.astype(F32)).astype(BF16)
    ss = (jnp.dot(hi, seg_ones, preferred_element_type=F32)
          + jnp.dot(lo, seg_ones, preferred_element_type=F32))
    return x * lax.rsqrt(ss * (1.0 / seg_width) + RMS_EPS) * gain


NA_ROWS_PER_STEP = 8
NA_BAND = NA_WIN_ROWS * GRID_W


def _na_bias_table(rel_bias):
    c = np.arange(GRID_W)
    dc = np.clip(c[None, :] - c[:, None], 1 - NA_WIN_COLS, NA_WIN_COLS - 1) + (NA_WIN_COLS - 1)
    col_start = np.clip(c - NA_WIN_COLS // 2, 0, GRID_W - NA_WIN_COLS)
    col_in = (c[None, :] >= col_start[:, None]) & (c[None, :] < col_start[:, None] + NA_WIN_COLS)
    onehot = (dc[None] == np.arange(2 * NA_WIN_COLS - 1)[:, None, None]).astype(np.float32)
    base = jnp.einsum("hrc,cqk->hrqk", rel_bias.astype(F32), onehot, precision=lax.Precision.HIGHEST)
    base = jnp.where(col_in[None, None], base, MASK_VALUE)
    tables = []
    for cfg in range(NA_WIN_ROWS):
        rows = base[:, NA_WIN_ROWS - 1 - cfg:2 * NA_WIN_ROWS - 1 - cfg]
        tables.append(rows.transpose(0, 2, 1, 3).reshape(NA_HEADS // 2, 2 * GRID_W, NA_BAND))
    return jnp.stack(tables)


def _na_kernel(q_ref, k_ref, v_ref, qg_ref, kg_ref, seg_ref, bias_ref, o_ref, kn_ref):
    step = pl.program_id(1)
    rows_total = k_ref.shape[0] // GRID_W
    seg = seg_ref[...]

    @pl.when(step == 0)
    def _():
        def norm_keys(t, carry):
            rows = pl.ds(pl.multiple_of(t * 256, 256), 256)
            kn_ref[rows, :] = _segment_rms(k_ref[rows, :].astype(F32), kg_ref[...], seg, NA_HEAD_DIM).astype(BF16)
            return carry
        lax.fori_loop(0, k_ref.shape[0] // 256, norm_keys, 0)

    lane = lax.broadcasted_iota(jnp.int32, (1, LANES), 1)
    low_half = lane < NA_HEAD_DIM

    def one_row(rr, carry):
        r = step * NA_ROWS_PER_STEP + rr
        row_start = jnp.clip(r - NA_WIN_ROWS // 2, 0, rows_total - NA_WIN_ROWS)
        cfg = r - row_start
        qrows = pl.ds(pl.multiple_of(rr * GRID_W, GRID_W), GRID_W)
        band = pl.ds(pl.multiple_of(row_start * GRID_W, GRID_W), NA_BAND)
        qn = (_segment_rms(q_ref[qrows, :].astype(F32), qg_ref[...], seg, NA_HEAD_DIM)
              * (NA_HEAD_DIM ** -0.5)).astype(BF16)
        pairs = range(NA_HEADS // 2)
        pair_cols = [slice(pair * LANES, (pair + 1) * LANES) for pair in pairs]
        scores = []
        for cols in pair_cols:
            qp = qn[:, cols]
            q2 = jnp.concatenate([jnp.where(low_half, qp, jnp.zeros_like(qp)),
                                  jnp.where(low_half, jnp.zeros_like(qp), qp)], axis=0)
            scores.append(lax.dot_general(q2, kn_ref[band, cols], (((1,), (1,)), ((), ())),
                                          preferred_element_type=F32))
        exps, sums = [], []
        for s, pair in zip(scores, pairs):
            s = s + bias_ref[cfg, pair]
            e = jnp.exp(s - jnp.max(s, axis=-1, keepdims=True))
            sums.append(jnp.sum(e, axis=-1, keepdims=True))
            exps.append(e.astype(BF16))
        outs = [jnp.dot(e, v_ref[band, cols], preferred_element_type=F32) / l
                for e, l, cols in zip(exps, sums, pair_cols)]
        for o2, cols in zip(outs, pair_cols):
            o_ref[qrows, cols] = jnp.where(low_half, o2[:GRID_W], o2[GRID_W:]).astype(o_ref.dtype)
        return carry

    lax.fori_loop(0, NA_ROWS_PER_STEP, one_row, 0)


def neighbourhood_attention(pb, batch, q_gain, k_gain, rel_bias):
    m = pb.shape[0]
    s = m // batch
    tq = NA_ROWS_PER_STEP * GRID_W
    steps = s // tq
    qg = jnp.tile(q_gain.astype(F32), NA_HEADS).reshape(1, NA_WIDTH)
    kg = jnp.tile(k_gain.astype(F32), NA_HEADS).reshape(1, NA_WIDTH)
    seg = jnp.asarray(np.kron(np.eye(NA_HEADS), np.ones((NA_HEAD_DIM, NA_HEAD_DIM))), BF16)
    bias = _na_bias_table(rel_bias)
    cq, ck, cv = (PB_COL[n] // NA_WIDTH for n in ("na_q", "na_k", "na_v"))
    return pl.pallas_call(
        _na_kernel,
        out_shape=jax.ShapeDtypeStruct((m, NA_WIDTH), BF16),
        grid=(batch, steps),
        in_specs=[
            pl.BlockSpec((tq, NA_WIDTH), lambda b, t: (b * steps + t, cq)),
            pl.BlockSpec((s, NA_WIDTH), lambda b, t: (b, ck)),
            pl.BlockSpec((s, NA_WIDTH), lambda b, t: (b, cv)),
            pl.BlockSpec((1, NA_WIDTH), lambda b, t: (0, 0)),
            pl.BlockSpec((1, NA_WIDTH), lambda b, t: (0, 0)),
            pl.BlockSpec((NA_WIDTH, NA_WIDTH), lambda b, t: (0, 0)),
            pl.BlockSpec((NA_WIN_ROWS, NA_HEADS // 2, 2 * GRID_W, NA_BAND), lambda b, t: (0, 0, 0, 0)),
        ],
        out_specs=pl.BlockSpec((tq, NA_WIDTH), lambda b, t: (b * steps + t, 0)),
        scratch_shapes=[pltpu.VMEM((s, NA_WIDTH), BF16)],
        compiler_params=_params("parallel", "arbitrary"),
        name="neighbourhood_attention",
    )(pb, pb, pb, qg, kg, seg, bias)


def _mem_attn_kernel(q_ref, kv_ref, qg_ref, kg_ref, o_ref, kn_ref):
    @pl.when(pl.program_id(1) == 0)
    def _():
        for h in range(MEM_HEADS):
            cols = slice(h * MEM_HEAD_DIM, (h + 1) * MEM_HEAD_DIM)
            kn_ref[:, cols] = _rms_norm_rows(kv_ref[:, cols].astype(F32), kg_ref[...]).astype(BF16)

    head_cols = [slice(h * MEM_HEAD_DIM, (h + 1) * MEM_HEAD_DIM) for h in range(MEM_HEADS)]
    qns = [_rms_norm_rows(q_ref[:, cols].astype(F32), qg_ref[...]).astype(BF16) for cols in head_cols]
    scores = [lax.dot_general(qn, kn_ref[:, cols], (((1,), (1,)), ((), ())), preferred_element_type=F32)
              for qn, cols in zip(qns, head_cols)]
    exps, sums = [], []
    for s in scores:
        s = s * (MEM_HEAD_DIM ** -0.5)
        e = jnp.exp(s - jnp.max(s, axis=-1, keepdims=True))
        sums.append(jnp.sum(e, axis=-1, keepdims=True))
        exps.append(e.astype(BF16))
    outs = [jnp.dot(e, kv_ref[:, MEM_WIDTH + cols.start:MEM_WIDTH + cols.stop], preferred_element_type=F32)
            for e, cols in zip(exps, head_cols)]
    for o, l, cols in zip(outs, sums, head_cols):
        o_ref[:, cols] = (o / l).astype(o_ref.dtype)


def memory_cross_attention(pb, kv, batch, q_gain, k_gain, *, tq=512):
    m = pb.shape[0]
    steps = m // batch // tq
    n_mem = kv.shape[0] // batch
    cq = PB_COL["mem_q"] // MEM_WIDTH
    return pl.pallas_call(
        _mem_attn_kernel,
        out_shape=jax.ShapeDtypeStruct((m, MEM_WIDTH), BF16),
        grid=(batch, steps),
        in_specs=[
            pl.BlockSpec((tq, MEM_WIDTH), lambda b, t: (b * steps + t, cq)),
            pl.BlockSpec((n_mem, 2 * MEM_WIDTH), lambda b, t: (b, 0)),
            pl.BlockSpec((1, MEM_HEAD_DIM), lambda b, t: (0, 0)),
            pl.BlockSpec((1, MEM_HEAD_DIM), lambda b, t: (0, 0)),
        ],
        out_specs=pl.BlockSpec((tq, MEM_WIDTH), lambda b, t: (b * steps + t, 0)),
        scratch_shapes=[pltpu.VMEM((n_mem, MEM_WIDTH), BF16)],
        compiler_params=_params("parallel", "arbitrary"),
        name="memory_cross_attention",
    )(pb, kv, q_gain.astype(F32).reshape(1, MEM_HEAD_DIM), k_gain.astype(F32).reshape(1, MEM_HEAD_DIM))


LIN_BLOCK = 512
HEAD_V = 128


def _log1p_exp_neg(t):
    return jnp.log(1.0 + jnp.exp(-t))


def _log_sigmoid(x):
    return jnp.minimum(x, 0.0) - _log1p_exp_neg(jnp.abs(x))


def _logaddexp(a, b):
    return jnp.maximum(a, b) + _log1p_exp_neg(jnp.abs(a - b))


def _split_bf16(x, terms):
    parts = []
    for _ in range(terms):
        p = x.astype(BF16)
        parts.append(p)
        x = x - p.astype(F32)
    return parts


def _dot_f32(a, b):
    a_hi, a_lo = _split_bf16(a, 2)
    b_hi, b_lo = _split_bf16(b, 2)
    return (jnp.dot(a_hi, b_hi, preferred_element_type=F32)
            + (jnp.dot(a_hi, b_lo, preferred_element_type=F32) + jnp.dot(a_lo, b_hi, preferred_element_type=F32)))


def _cumsum_rows(mask, x):
    m = jnp.where(mask, 1.0, 0.0).astype(BF16)
    hi, mid, lo = _split_bf16(x, 3)
    return (jnp.dot(m, hi, preferred_element_type=F32)
            + (jnp.dot(m, mid, preferred_element_type=F32) + jnp.dot(m, lo, preferred_element_type=F32)))


def _gla_inputs(refs, rows, direction, params):
    q_ref, k_ref, v_ref, g_ref = refs
    wpad_ref, bias_ref = params
    qc = q_ref[rows, :].astype(F32) * (GLA_HEAD_K ** -0.5)
    kc = k_ref[rows, :].astype(F32)
    gk = _dot_f32(g_ref[rows, :], wpad_ref[direction]) + bias_ref[direction]
    lg = _log_sigmoid(gk) * (1.0 / GLA_GATE_NORMALIZER)
    return qc, kc, v_ref[rows, :], lg


def _hgrn_inputs(refs, rows, direction, params):
    q_ref, v_ref, z_ref = refs
    lb_ref, log_lb_ref, log1m_lb_ref = params
    qr = q_ref[rows, :].astype(F32)
    qc = qr * _sigmoid(qr)
    z = z_ref[rows, :]
    lg = _logaddexp(log_lb_ref[direction], log1m_lb_ref[direction] + _log_sigmoid(z))
    kc = (1.0 - lb_ref[direction]) * _sigmoid(-z)
    return qc, kc, v_ref[rows, :], lg


GDN_CONV_WIDTH = 5
GDN_QKV_WIDTH = 2 * GDN_KEY_WIDTH + GDN_VAL_WIDTH
GDN_HALO = 16


def _gdn_prep_kernel(prev_ref, cur_ref, next_ref, w_ref, o_ref, xp_ref, *, blocks_per_seq):
    i = pl.program_id(0)
    t = cur_ref.shape[0]
    pos = i % blocks_per_seq
    prev = prev_ref[...].astype(F32)
    nxt = next_ref[...].astype(F32)
    xp_ref[0:GDN_HALO, :] = jnp.where(pos == 0, jnp.zeros_like(prev), prev)
    xp_ref[GDN_HALO:GDN_HALO + t, :] = cur_ref[...].astype(F32)
    xp_ref[GDN_HALO + t:, :] = jnp.where(pos == blocks_per_seq - 1, jnp.zeros_like(nxt), nxt)
    half = GDN_CONV_WIDTH // 2
    for g in range(GDN_QKV_WIDTH // LANES):
        cols = slice(g * LANES, (g + 1) * LANES)
        acc = None
        for j in range(GDN_CONV_WIDTH):
            term = xp_ref[GDN_HALO - half + j:GDN_HALO - half + j + t, cols] * w_ref[j:j + 1, cols]
            acc = term if acc is None else acc + term
        y = acc * _sigmoid(acc)
        if g < 2 * GDN_HEADS:
            y = y * lax.rsqrt(jnp.sum(y * y, axis=-1, keepdims=True) + 1e-6)
            if g < GDN_HEADS:
                y = y * (GDN_HEAD_K ** -0.5)
        o_ref[:, cols] = y.astype(o_ref.dtype)


def gdn_prep(pb, batch, conv_w, *, t=512):
    m = pb.shape[0]
    blocks_per_seq = m // batch // t
    halo_per_block = t // GDN_HALO
    col = PB_COL["gdn_qkv"] // GDN_QKV_WIDTH
    last_halo = m // GDN_HALO - 1
    return pl.pallas_call(
        functools.partial(_gdn_prep_kernel, blocks_per_seq=blocks_per_seq),
        out_shape=jax.ShapeDtypeStruct((m, GDN_QKV_WIDTH), BF16),
        grid=(m // t,),
        in_specs=[
            pl.BlockSpec((GDN_HALO, GDN_QKV_WIDTH), lambda i: (jnp.maximum(i * halo_per_block - 1, 0), col)),
            pl.BlockSpec((t, GDN_QKV_WIDTH), lambda i: (i, col)),
            pl.BlockSpec((GDN_HALO, GDN_QKV_WIDTH),
                         lambda i: (jnp.minimum((i + 1) * halo_per_block, last_halo), col)),
            pl.BlockSpec((GDN_CONV_WIDTH, GDN_QKV_WIDTH), lambda i: (0, 0)),
        ],
        out_specs=pl.BlockSpec((t, GDN_QKV_WIDTH), lambda i: (i, 0)),
        scratch_shapes=[pltpu.VMEM((t + 2 * GDN_HALO, GDN_QKV_WIDTH), F32)],
        compiler_params=_params("parallel"),
        name="gdn_prep",
    )(pb, pb, pb, conv_w.astype(F32))


def _softplus(x):
    return jnp.maximum(x, 0.0) + _log1p_exp_neg(jnp.abs(x))


GDN_PACK = GDN_HEADS * GDN_CHUNK
GDN_WY_BLOCK = 512


def _stack_heads(x, width):
    heads = x.shape[1] // width
    lane = lax.broadcasted_iota(jnp.int32, (1, x.shape[1]), 1)
    return jnp.concatenate(
        [jnp.where((lane >= h * width) & (lane < (h + 1) * width), x, 0.0).astype(BF16) for h in range(heads)],
        axis=0)


def _packed_mm(x, y):
    return jnp.dot(x.astype(BF16), _stack_heads(y, GDN_CHUNK), preferred_element_type=F32)


def _packed_inverses(mats):
    c = GDN_CHUNK
    ii = lax.broadcasted_iota(jnp.int32, (c, GDN_PACK), 0)
    jj = lax.broadcasted_iota(jnp.int32, (c, GDN_PACK), 1) % c
    eye = (ii == jj).astype(F32)

    def same_block(s):
        return (ii // s) == (jj // s)

    ds = [jnp.where(same_block(8), a, 0.0) for a in mats]
    d2s = [_packed_mm(d, d) for d in ds]
    d4s = [_packed_mm(d2, d2) for d2 in d2s]
    ts = [_packed_mm(eye - d, eye + d2) for d, d2 in zip(ds, d2s)]
    ts = [_packed_mm(t, eye + d4) for t, d4 in zip(ts, d4s)]
    s = 8
    while s < c:
        off = same_block(2 * s) & jnp.logical_not(same_block(s))
        ets = [_packed_mm(jnp.where(off, a, 0.0), t) for a, t in zip(mats, ts)]
        ts = [t - _packed_mm(t, et) for t, et in zip(ts, ets)]
        s *= 2
    return ts


def _gdn_wy_kernel(qkv_ref, small_ref, a_ref, dtb_ref, selg_ref, selk_ref, selb_ref, *out_refs):
    c = GDN_CHUNK
    n_chunks = qkv_ref.shape[0] // c
    ii = lax.broadcasted_iota(jnp.int32, (c, c), 0)
    jj = lax.broadcasted_iota(jnp.int32, (c, c), 1)
    pi = lax.broadcasted_iota(jnp.int32, (c, GDN_PACK), 0)
    pj = lax.broadcasted_iota(jnp.int32, (c, GDN_PACK), 1) % c
    eye_p = (pi == pj).astype(F32)
    ones_cc = jnp.ones((c, c), BF16)

    problems = [(ch, d) for ch in range(n_chunks) for d in range(2)]
    chunk_in = []
    for ch in range(n_chunks):
        rows = slice(ch * c, (ch + 1) * c)
        qkv = qkv_ref[rows, :]
        small = small_ref[rows, :]
        kf = qkv[:, GDN_KEY_WIDTH:2 * GDN_KEY_WIDTH].astype(F32)
        chunk_in.append(dict(
            qf=qkv[:, :GDN_KEY_WIDTH].astype(F32), kf=kf, vf=qkv[:, 2 * GDN_KEY_WIDTH:].astype(F32),
            kbd=_stack_heads(kf, HEAD_V),
            log_alpha=a_ref[...] * _softplus(small + dtb_ref[...]),
            beta_all=_sigmoid(small)))

    def sel3(x, sel):
        hi, mid, lo = _split_bf16(x, 3)
        return (jnp.dot(hi, sel, preferred_element_type=F32)
                + (jnp.dot(mid, sel, preferred_element_type=F32) + jnp.dot(lo, sel, preferred_element_type=F32)))

    g_all = [_cumsum_rows((jj >= ii) if d else (jj <= ii), chunk_in[ch]["log_alpha"]) for ch, d in problems]
    g_pack = [sel3(g, selg_ref[d]) for g, (ch, d) in zip(g_all, problems)]
    g_wide = [sel3(g, selk_ref[d]) for g, (ch, d) in zip(g_all, problems)]
    beta_w = [sel3(chunk_in[ch]["beta_all"], selb_ref[d]) for ch, d in problems]
    g_rowp = []
    for gp in g_pack:
        hi, mid, lo = _split_bf16(gp * eye_p, 3)
        g_rowp.append(jnp.dot(ones_cc, hi, preferred_element_type=F32)
                      + (jnp.dot(ones_cc, mid, preferred_element_type=F32)
                         + jnp.dot(ones_cc, lo, preferred_element_type=F32)))
    decays, k_betas = [], []
    for gp, gr, bw, (ch, d) in zip(g_pack, g_rowp, beta_w, problems):
        incl = (pj >= pi) if d else (pj <= pi)
        decays.append(jnp.where(incl, jnp.exp(jnp.where(incl, gp - gr, 0.0)), 0.0))
        k_betas.append(chunk_in[ch]["kf"] * bw)
    kq = [lax.dot_general(jnp.concatenate([kb, chunk_in[ch]["qf"]], axis=0).astype(BF16), chunk_in[ch]["kbd"],
                          (((1,), (1,)), ((), ())), preferred_element_type=F32)
          for kb, (ch, d) in zip(k_betas, problems)]
    a_mats = []
    for x, dec, (ch, d) in zip(kq, decays, problems):
        strict = (pj > pi) if d else (pj < pi)
        a_mats.append(jnp.where(strict, x[:c] * dec, 0.0))
    t_invs = _packed_inverses(a_mats)

    for idx, (ch, d) in enumerate(problems):
        u_ref, w_ref, attn_ref, qd_ref, kd_ref, gt_ref = out_refs[6 * d:6 * d + 6]
        rows = slice(ch * c, (ch + 1) * c)
        cin = chunk_in[ch]
        gw = g_wide[idx]
        eg = jnp.exp(gw)
        t_b = t_invs[idx].astype(BF16)
        u_ref[rows, :] = jnp.dot(t_b, _stack_heads(cin["vf"] * beta_w[idx], HEAD_V), preferred_element_type=F32)
        w_ref[rows, :] = jnp.dot(t_b, _stack_heads(k_betas[idx] * eg, HEAD_V),
                                 preferred_element_type=F32).astype(w_ref.dtype)
        attn_ref[rows, :] = (kq[idx][c:] * decays[idx]).astype(attn_ref.dtype)
        end = 0 if d else c - 1
        g_end = gw[end:end + 1, :]
        qd_ref[rows, :] = (cin["qf"] * eg).astype(qd_ref.dtype)
        kd_ref[rows, :] = (cin["kf"] * jnp.exp(g_end - gw)).astype(kd_ref.dtype)
        gt_ref[ch:ch + 1, :] = jnp.exp(g_end)


def gdn_wy(qkv, pf, a_scale, dtb):
    m = qkv.shape[0]
    t = GDN_WY_BLOCK
    cpb = t // GDN_CHUNK
    selg = np.zeros((2, LANES, GDN_PACK), np.float32)
    selk = np.zeros((2, LANES, GDN_VAL_WIDTH), np.float32)
    selb = np.zeros((2, LANES, GDN_VAL_WIDTH), np.float32)
    for d in range(2):
        for h in range(GDN_HEADS):
            selg[d, GDN_A_LANE + d * GDN_HEADS + h, h * GDN_CHUNK:(h + 1) * GDN_CHUNK] = 1.0
            selk[d, GDN_A_LANE + d * GDN_HEADS + h, h * HEAD_V:(h + 1) * HEAD_V] = 1.0
            selb[d, GDN_B_LANE + d * GDN_HEADS + h, h * HEAD_V:(h + 1) * HEAD_V] = 1.0
    wide = GDN_VAL_WIDTH
    out_shape, out_specs = [], []
    for _ in range(2):
        for width, dt in ((wide, F32), (wide, BF16), (GDN_PACK, BF16), (wide, BF16), (wide, BF16)):
            out_shape.append(jax.ShapeDtypeStruct((m, width), dt))
            out_specs.append(pl.BlockSpec((t, width), lambda i: (i, 0)))
        out_shape.append(jax.ShapeDtypeStruct((m // GDN_CHUNK, wide), F32))
        out_specs.append(pl.BlockSpec((cpb, wide), lambda i: (i, 0)))
    return pl.pallas_call(
        _gdn_wy_kernel,
        out_shape=tuple(out_shape),
        grid=(m // t,),
        in_specs=[
            pl.BlockSpec((t, GDN_QKV_WIDTH), lambda i: (i, 0)),
            pl.BlockSpec((t, LANES), lambda i: (i, PF_SMALL_COL // LANES)),
            pl.BlockSpec((1, LANES), lambda i: (0, 0)),
            pl.BlockSpec((1, LANES), lambda i: (0, 0)),
            pl.BlockSpec((2, LANES, GDN_PACK), lambda i: (0, 0, 0)),
            pl.BlockSpec((2, LANES, wide), lambda i: (0, 0, 0)),
            pl.BlockSpec((2, LANES, wide), lambda i: (0, 0, 0)),
        ],
        out_specs=tuple(out_specs),
        compiler_params=_params("parallel"),
        name="gdn_wy",
    )(qkv, pf, a_scale, dtb, jnp.asarray(selg, BF16), jnp.asarray(selk, BF16), jnp.asarray(selb, BF16))


GDN_PAIR = 2 * HEAD_V


def _gdn_scan_kernel(*refs):
    groups = (refs[0:6], refs[6:12])
    out_refs = refs[12:14]
    state_refs = refs[14:16]

    @pl.when(pl.program_id(1) == 0)
    def _():
        for s_ref in state_refs:
            s_ref[...] = jnp.zeros_like(s_ref)

    n_chunks = out_refs[0].shape[0] // GDN_CHUNK
    pairs = GDN_HEADS // 2
    pair_cols = [slice(p * GDN_PAIR, (p + 1) * GDN_PAIR) for p in range(pairs)]
    ri = lax.broadcasted_iota(jnp.int32, (GDN_PAIR, GDN_PAIR), 0) // HEAD_V
    ci = lax.broadcasted_iota(jnp.int32, (GDN_PAIR, GDN_PAIR), 1) // HEAD_V
    diag = ri == ci

    def body(c, carry):
        chunks = (c, n_chunks - 1 - c)
        rows = [pl.ds(pl.multiple_of(ch * GDN_CHUNK, GDN_CHUNK), GDN_CHUNK) for ch in chunks]
        states = [[s_ref[p] for p in range(pairs)] for s_ref in state_refs]
        states_b = [[s.astype(BF16) for s in st] for st in states]
        wq = [[jnp.dot(jnp.concatenate([groups[g][1][rows[g], cols], groups[g][3][rows[g], cols]], axis=0),
                       states_b[g][p], preferred_element_type=F32)
               for p, cols in enumerate(pair_cols)] for g in range(2)]
        ws = [[x[:GDN_CHUNK] for x in wq[g]] for g in range(2)]
        qs = [[x[GDN_CHUNK:] for x in wq[g]] for g in range(2)]
        v_new = [groups[g][0][rows[g], :] - jnp.concatenate(ws[g], axis=1) for g in range(2)]
        av = [jnp.dot(groups[g][2][rows[g], :], _stack_heads(v_new[g], HEAD_V), preferred_element_type=F32)
              for g in range(2)]
        v_new_b = [v.astype(BF16) for v in v_new]
        upd = [[lax.dot_general(groups[g][4][rows[g], cols], v_new_b[g][:, cols], (((0,), (0,)), ((), ())),
                                preferred_element_type=F32) for cols in pair_cols] for g in range(2)]
        for g in range(2):
            out_refs[g][rows[g], :] = jnp.concatenate(qs[g], axis=1) + av[g]
            gt = groups[g][5][pl.ds(chunks[g], 1), :]
            for p, cols in enumerate(pair_cols):
                state_refs[g][p] = states[g][p] * gt[:, cols] + jnp.where(diag, upd[g][p], 0.0)
        return carry

    lax.fori_loop(0, n_chunks, body, 0)


def gated_deltanet_branch(pb, pf, batch, conv_w, a_log, dt_bias, norm_gain):
    m = pb.shape[0]
    nb = m // batch // LIN_BLOCK
    cpb = LIN_BLOCK // GDN_CHUNK
    qkv = gdn_prep(pb, batch, conv_w)
    n_gate = 2 * GDN_HEADS
    a_scale = jnp.zeros((1, LANES), F32).at[0, GDN_A_LANE:GDN_A_LANE + n_gate].set(
        -jnp.exp(a_log.astype(F32)).reshape(n_gate))
    dtb = jnp.zeros((1, LANES), F32).at[0, GDN_A_LANE:GDN_A_LANE + n_gate].set(dt_bias.astype(F32).reshape(n_gate))
    wy = gdn_wy(qkv, pf, a_scale, dtb)
    widths = (GDN_VAL_WIDTH, GDN_VAL_WIDTH, GDN_PACK, GDN_VAL_WIDTH, GDN_VAL_WIDTH)

    def fwd(rows, width):
        return pl.BlockSpec((rows, width), lambda b, t: (b * nb + t, 0))

    def bwd(rows, width):
        return pl.BlockSpec((rows, width), lambda b, t: (b * nb + nb - 1 - t, 0))

    in_specs = [fwd(LIN_BLOCK, w) for w in widths] + [fwd(cpb, GDN_VAL_WIDTH)]
    in_specs += [bwd(LIN_BLOCK, w) for w in widths] + [bwd(cpb, GDN_VAL_WIDTH)]
    state = pltpu.VMEM((GDN_HEADS // 2, GDN_PAIR, GDN_PAIR), F32)
    o_f, o_b = pl.pallas_call(
        _gdn_scan_kernel,
        out_shape=(jax.ShapeDtypeStruct((m, GDN_VAL_WIDTH), F32), jax.ShapeDtypeStruct((m, GDN_VAL_WIDTH), F32)),
        grid=(batch, nb),
        in_specs=in_specs,
        out_specs=(fwd(LIN_BLOCK, GDN_VAL_WIDTH), bwd(LIN_BLOCK, GDN_VAL_WIDTH)),
        scratch_shapes=[state, state],
        compiler_params=_params("parallel", "arbitrary"),
        name="gdn_scan",
    )(*wy)
    return RawBranch(o_f, o_b, "gdn_og", norm_gain, True)


def _dot3(m, x):
    hi, mid, lo = _split_bf16(x, 3)
    return (jnp.dot(m, hi, preferred_element_type=F32)
            + (jnp.dot(m, mid, preferred_element_type=F32) + jnp.dot(m, lo, preferred_element_type=F32)))


LIN_CUM_ROWS = 256
LIN_SCORE_ROWS = 128


def _chunk_causal(n, chunk, reverse):
    i = lax.broadcasted_iota(jnp.int32, (n, n), 0)
    j = lax.broadcasted_iota(jnp.int32, (n, n), 1)
    return ((i // chunk) == (j // chunk)) & ((j >= i) if reverse else (j <= i))


def _lin_intra_kernel(*refs, load_inputs, n_in, n_params, heads):
    dir_refs = (refs[:n_in], refs[n_in:2 * n_in])
    params = refs[2 * n_in:2 * n_in + n_params]
    out_refs = refs[2 * n_in + n_params:]
    c = LIN_CHUNK
    dirs = (0, 1)
    loaded = [load_inputs(dir_refs[d], slice(None), d, params) for d in dirs]
    t, w = loaded[0][0].shape
    dk = w // heads
    nc = t // c
    cums = [jnp.where(_chunk_causal(LIN_CUM_ROWS, c, d == 1), 1.0, 0.0).astype(BF16) for d in dirs]
    bs = [jnp.concatenate([_dot3(cums[d], loaded[d][3][r:r + LIN_CUM_ROWS, :])
                           for r in range(0, t, LIN_CUM_ROWS)], axis=0) for d in dirs]
    qes, kes = [], []
    for d in dirs:
        oi_ref, qd_ref, kd_ref, gt_ref = out_refs[4 * d:4 * d + 4]
        qc, kc, vc, lg = loaded[d]
        b = bs[d]
        b3 = b.reshape(nc, c, w)
        mid = c - 1 - c // 2 if d else c // 2
        end = 0 if d else c - 1
        b_mid = jnp.broadcast_to(b3[:, mid:mid + 1, :], (nc, c, w)).reshape(t, w)
        b_end = jnp.broadcast_to(b3[:, end:end + 1, :], (nc, c, w)).reshape(t, w)
        qes.append((qc * jnp.exp(b - b_mid)).astype(BF16))
        kes.append((kc * jnp.exp(b_mid - b)).astype(BF16))
        qd_ref[...] = (qc * jnp.exp(b)).astype(qd_ref.dtype)
        kd_ref[...] = (kc * jnp.exp(b_end - b)).astype(kd_ref.dtype)
        gt_ref[...] = jnp.exp(b3[:, end, :])
    keeps = [_chunk_causal(LIN_SCORE_ROWS, c, d == 1) for d in dirs]
    lane = lax.broadcasted_iota(jnp.int32, (1, LANES), 1)
    for h in range(heads):
        win = slice((h * dk) // LANES * LANES, (h * dk) // LANES * LANES + LANES)
        lo = h * dk - win.start
        vcols = slice(h * HEAD_V, (h + 1) * HEAD_V)
        tiles = [(slice(r, r + LIN_SCORE_ROWS), d) for r in range(0, t, LIN_SCORE_ROWS) for d in dirs]
        scores = []
        for rows, d in tiles:
            qh = qes[d][rows, win]
            if dk < LANES:
                qh = jnp.where((lane >= lo) & (lane < lo + dk), qh, jnp.zeros_like(qh))
            scores.append(lax.dot_general(qh, kes[d][rows, win], (((1,), (1,)), ((), ())),
                                          preferred_element_type=F32))
        probs = [jnp.where(keeps[d], s, 0.0).astype(BF16) for s, (rows, d) in zip(scores, tiles)]
        for p, (rows, d) in zip(probs, tiles):
            out_refs[4 * d][rows, vcols] = jnp.dot(p, loaded[d][2][rows, vcols], preferred_element_type=F32)


def _lin_scan_kernel(*refs, heads, chunk, unroll):
    groups = (refs[0:5], refs[5:10])
    out_refs = refs[10:12]
    state_refs = refs[12:14]

    @pl.when(pl.program_id(1) == 0)
    def _():
        for s_ref in state_refs:
            s_ref[...] = jnp.zeros_like(s_ref)

    n_chunks = out_refs[0].shape[0] // chunk
    w = state_refs[0].shape[1]
    dk = w // heads
    lane = lax.broadcasted_iota(jnp.int32, (1, w), 1)
    masks = [(lane >= h * dk) & (lane < (h + 1) * dk) for h in range(heads)]

    def stack(x):
        return jnp.concatenate([jnp.where(m, x, jnp.zeros_like(x)) for m in masks], axis=0)

    def body(it, carry):
        steps = []
        for u in range(unroll):
            c = it * unroll + u
            steps += [(0, c), (1, n_chunks - 1 - c)]
        prepared = []
        for g, ch in steps:
            rows = pl.ds(pl.multiple_of(ch * chunk, chunk), chunk)
            oi_ref, qd_ref, kd_ref, v_ref, gt_ref = groups[g]
            vc = v_ref[rows, :]
            v4 = jnp.concatenate([vc[:, h * HEAD_V:(h + 1) * HEAD_V] for h in range(heads)], axis=0)
            upd = lax.dot_general(v4, stack(kd_ref[rows, :]), (((0,), (0,)), ((), ())),
                                  preferred_element_type=F32)
            prepared.append((rows, stack(qd_ref[rows, :]), upd, gt_ref[pl.ds(ch, 1), :]))
        states = [s_ref[...] for s_ref in state_refs]
        for (g, ch), (rows, q4, upd, gt) in zip(steps, prepared):
            o_inter = lax.dot_general(q4, states[g].astype(BF16), (((1,), (1,)), ((), ())),
                                      preferred_element_type=F32)
            out_refs[g][rows, :] = groups[g][0][rows, :] + jnp.concatenate(
                [o_inter[h * chunk:(h + 1) * chunk, :] for h in range(heads)], axis=1)
            states[g] = states[g] * gt + upd
        for s_ref, st in zip(state_refs, states):
            s_ref[...] = st
        return carry

    lax.fori_loop(0, n_chunks // unroll, body, 0)


def _bidir_lin_call(name, load_inputs, arrays, col_blocks, widths, params, batch, heads, key_width, v_col):
    m = arrays[0].shape[0]
    t = LIN_BLOCK
    nb = m // batch // t
    cpb = t // LIN_CHUNK
    out_w = heads * HEAD_V
    n_in = len(arrays)

    in_specs, operands = [], []
    for d in range(2):
        for a, wd, cb in zip(arrays, widths, col_blocks):
            in_specs.append(pl.BlockSpec((t, wd), functools.partial(lambda i, c: (i, c), c=cb[d])))
            operands.append(a)
    for p in params:
        in_specs.append(pl.BlockSpec(p.shape, functools.partial(lambda i, nd: (0,) * nd, nd=p.ndim)))
    out_shape, out_specs = [], []
    for _ in range(2):
        for rows_total, rows_blk, width, dt in ((m, t, out_w, F32), (m, t, key_width, BF16),
                                                (m, t, key_width, BF16), (m // LIN_CHUNK, cpb, key_width, F32)):
            out_shape.append(jax.ShapeDtypeStruct((rows_total, width), dt))
            out_specs.append(pl.BlockSpec((rows_blk, width), lambda i: (i, 0)))

    intra = pl.pallas_call(
        functools.partial(_lin_intra_kernel, load_inputs=load_inputs, n_in=n_in, n_params=len(params),
                          heads=heads),
        out_shape=tuple(out_shape),
        grid=(m // t,),
        in_specs=in_specs,
        out_specs=tuple(out_specs),
        compiler_params=_params("parallel"),
        name=name + "_intra",
    )(*operands, *params)

    def fwd(rows, width, col=0):
        return pl.BlockSpec((rows, width), lambda b, s: (b * nb + s, col))

    def bwd(rows, width, col=0):
        return pl.BlockSpec((rows, width), lambda b, s: (b * nb + nb - 1 - s, col))

    scan_specs, scan_ops = [], []
    for d, mk in enumerate((fwd, bwd)):
        oi, qd, kd, gt = intra[4 * d:4 * d + 4]
        scan_specs += [mk(t, out_w), mk(t, key_width), mk(t, key_width), mk(t, out_w, v_col), mk(cpb, key_width)]
        scan_ops += [oi, qd, kd, arrays[0], gt]
    state = pltpu.VMEM((HEAD_V, key_width), F32)
    return pl.pallas_call(
        functools.partial(_lin_scan_kernel, heads=heads, chunk=LIN_CHUNK, unroll=8),
        out_shape=(jax.ShapeDtypeStruct((m, out_w), F32), jax.ShapeDtypeStruct((m, out_w), F32)),
        grid=(batch, nb),
        in_specs=scan_specs,
        out_specs=(fwd(t, out_w), bwd(t, out_w)),
        scratch_shapes=[state, state],
        compiler_params=_params("parallel", "arbitrary"),
        name=name + "_scan",
    )(*scan_ops)


def gla_branch(pb, pf, batch, w_gate_up, b_gate, norm_gain):
    wpad = jnp.zeros((2, LANES, GLA_KEY_WIDTH), F32)
    for d in range(2):
        wpad = wpad.at[d, d * GLA_GATE_RANK:(d + 1) * GLA_GATE_RANK, :].set(w_gate_up[d].astype(F32))
    bias = b_gate.astype(F32).reshape(2, 1, GLA_KEY_WIDTH)
    v_col = PB_COL["gla_v"] // GLA_VAL_WIDTH
    cols = [(PB_COL["gla_q"] // GLA_KEY_WIDTH,) * 2, (PB_COL["gla_k"] // GLA_KEY_WIDTH,) * 2,
            (v_col,) * 2, (PF_SMALL_COL // LANES,) * 2]
    o_f, o_b = _bidir_lin_call("gla", _gla_inputs, [pb, pb, pb, pf], cols,
                               [GLA_KEY_WIDTH, GLA_KEY_WIDTH, GLA_VAL_WIDTH, LANES], [wpad, bias],
                               batch, GLA_HEADS, GLA_KEY_WIDTH, v_col)
    return RawBranch(o_f, o_b, "gla_og", norm_gain, True)


def hgrn2_branch(pb, pf, batch, lower_bound, norm_gain):
    lb = lower_bound.astype(F32).reshape(2, 1, HGRN_KEY_WIDTH)
    log_lb = jnp.log(jnp.maximum(lb, LB_FLOOR))
    log1m_lb = jnp.log1p(-lb)
    zc = PF_COL["hg_f"] // HGRN_KEY_WIDTH
    v_col = PB_COL["hg_i"] // HGRN_VAL_WIDTH
    cols = [(PB_COL["hg_q"] // HGRN_KEY_WIDTH,) * 2, (v_col,) * 2, (zc, zc + 1)]
    o_f, o_b = _bidir_lin_call("hgrn2", _hgrn_inputs, [pb, pb, pf], cols,
                               [HGRN_KEY_WIDTH, HGRN_VAL_WIDTH, HGRN_KEY_WIDTH], [lb, log_lb, log1m_lb],
                               batch, HGRN_HEADS, HGRN_KEY_WIDTH, v_col)
    return RawBranch(o_f, o_b, "hg_og", norm_gain, False)


def kernel(x, mem, g_mix, w_in, na_q_gain, na_k_gain, na_rel_bias, gla_w_gate_up, gla_b_gate, gla_norm_gain, gdn_conv_w, gdn_a_log, gdn_dt_bias, gdn_norm_gain, hgrn_lb_raw, hgrn_norm_gain, g_mem, w_mem_kv, mem_q_gain, mem_k_gain, w_branch, w_out, g_ffn, ffn_w_gate, ffn_w_up, ffn_w_down, moe_w_router, moe_b_router, moe_w_gate, moe_w_up, moe_w_down):
    B, S, D = x.shape
    n_tok = B * S
    lb_w = jax.nn.softmax(hgrn_lb_raw.astype(F32), axis=0)
    hgrn_lb = jnp.cumsum(lb_w, axis=0) - lb_w[0:1]
    x2 = x.reshape(n_tok, D)
    mem2 = mem.reshape(B * mem.shape[1], D)
    for layer in range(DEPTH):
        pb, pf = in_projection(x2, g_mix[layer], _rearrange_w_in(w_in[layer]), PB_COL["gates"], PB_WIDTH)
        kv = rms_matmul(mem2, g_mem[layer], w_mem_kv[layer].astype(BF16), tm=mem2.shape[0], tn=512,
                        out_dtype=BF16)
        branches = [
            neighbourhood_attention(pb, B, na_q_gain[layer], na_k_gain[layer], na_rel_bias[layer]),
            gla_branch(pb, pf, B, gla_w_gate_up[layer], gla_b_gate[layer], gla_norm_gain[layer]),
            gated_deltanet_branch(pb, pf, B, gdn_conv_w[layer], gdn_a_log[layer], gdn_dt_bias[layer],
                                  gdn_norm_gain[layer]),
            hgrn2_branch(pb, pf, B, hgrn_lb[layer], hgrn_norm_gain[layer]),
            memory_cross_attention(pb, kv, B, mem_q_gain[layer], mem_k_gain[layer]),
        ]
        merged = merge_branches(branches, pb, w_branch[layer].astype(BF16), tm=512, tn=1024)
        x2 = matmul_residual(merged, w_out[layer].astype(BF16), x2, tm=512, tn=D)

        j = layer // 2
        if layer % 2 == 0:
            act = rms_swiglu_up(x2, g_ffn[layer], ffn_w_gate[j].astype(BF16), ffn_w_up[j].astype(BF16),
                                tm=1024, tn=512)
            x2 = matmul_residual(act, ffn_w_down[j].astype(BF16), x2, tm=512, tn=1024)
        else:
            x2 = moe_layer(x2, g_ffn[layer], moe_w_router[j], moe_b_router[j], moe_w_gate[j], moe_w_up[j],
                           moe_w_down[j])
    return x2.reshape(B, S, D)
```

```python
import functools

import jax
import jax.numpy as jnp
import numpy as np
from jax import lax
from jax.experimental import pallas as pl
from jax.experimental.pallas import tpu as pltpu

F32 = jnp.float32
BF16 = jnp.bfloat16

D_MODEL = 2048
DEPTH = 2
RMS_EPS = 1e-6
MASK_VALUE = -1e30
LB_FLOOR = 1e-30
GRID_W = 64

NA_HEADS = 8
NA_HEAD_DIM = 64
NA_WIDTH = 512
NA_WIN_ROWS = 8
NA_WIN_COLS = 16

GLA_HEADS = 4
GLA_HEAD_K = 64
GLA_HEAD_V = 128
GLA_KEY_WIDTH = 256
GLA_VAL_WIDTH = 512
GLA_GATE_RANK = 16
GLA_GATE_NORMALIZER = 16.0

GDN_HEADS = 4
GDN_HEAD_K = 128
GDN_HEAD_V = 128
GDN_KEY_WIDTH = 512
GDN_VAL_WIDTH = 512
GDN_CHUNK = 64

HGRN_HEADS = 4
HGRN_HEAD_K = 128
HGRN_HEAD_V = 128
HGRN_KEY_WIDTH = 512
HGRN_VAL_WIDTH = 512

LIN_CHUNK = 32

MEM_HEADS = 4
MEM_HEAD_DIM = 128
MEM_WIDTH = 512

N_BRANCH = 5
BRANCH_WIDTH = 512
N_EXPERTS = 8
MOE_TOP_K = 2

IN_WIDTHS = (
    NA_WIDTH, NA_WIDTH, NA_WIDTH,
    GLA_KEY_WIDTH, GLA_KEY_WIDTH, GLA_VAL_WIDTH,
    2 * GLA_GATE_RANK, GLA_VAL_WIDTH,
    2 * GDN_KEY_WIDTH + GDN_VAL_WIDTH,
    2 * GDN_HEADS, 2 * GDN_HEADS, GDN_VAL_WIDTH,
    HGRN_KEY_WIDTH, 2 * HGRN_KEY_WIDTH, HGRN_VAL_WIDTH, HGRN_VAL_WIDTH,
    MEM_WIDTH,
    N_BRANCH * D_MODEL,
)
P_IN = sum(IN_WIDTHS)

V7X_VMEM_BYTES = 64 * 1024 * 1024
VMEM_LIMIT_BYTES = V7X_VMEM_BYTES - 8 * 1024 * 1024
LANES = 128


def _params(*semantics):
    return pltpu.CompilerParams(dimension_semantics=semantics, vmem_limit_bytes=VMEM_LIMIT_BYTES)


def _sigmoid(x):
    return 0.5 * jnp.tanh(0.5 * x) + 0.5


def _rms_norm_rows(x, gain):
    ms = jnp.mean(x * x, axis=-1, keepdims=True)
    return x * lax.rsqrt(ms + RMS_EPS) * gain


def _rms_matmul_kernel(x_ref, g_ref, w_ref, o_ref, h_ref):
    @pl.when(pl.program_id(1) == 0)
    def _():
        h_ref[...] = _rms_norm_rows(x_ref[...], g_ref[...]).astype(BF16)

    o_ref[...] = jnp.dot(h_ref[...], w_ref[...], preferred_element_type=F32).astype(o_ref.dtype)


def rms_matmul(x, gain, w, *, tm, tn, out_dtype=F32):
    m, k = x.shape
    n = w.shape[1]
    return pl.pallas_call(
        _rms_matmul_kernel,
        out_shape=jax.ShapeDtypeStruct((m, n), out_dtype),
        grid=(m // tm, n // tn),
        in_specs=[
            pl.BlockSpec((tm, k), lambda i, j: (i, 0)),
            pl.BlockSpec((1, k), lambda i, j: (0, 0)),
            pl.BlockSpec((k, tn), lambda i, j: (0, j)),
        ],
        out_specs=pl.BlockSpec((tm, tn), lambda i, j: (i, j)),
        scratch_shapes=[pltpu.VMEM((tm, k), BF16)],
        compiler_params=_params("parallel", "arbitrary"),
        name="rms_matmul",
    )(x, gain.reshape(1, k), w)


IN_PROJ_TILE = 512


def _in_projection_kernel(x_ref, g_ref, w_ref, ob_ref, of_ref, h_ref, *, n_plain_tiles, n_bf16_tiles):
    j = pl.program_id(1)

    @pl.when(j == 0)
    def _():
        h_ref[...] = _rms_norm_rows(x_ref[...], g_ref[...]).astype(BF16)

    r = jnp.dot(h_ref[...], w_ref[...], preferred_element_type=F32)

    @pl.when(j < n_plain_tiles)
    def _():
        ob_ref[...] = r.astype(ob_ref.dtype)

    @pl.when(jnp.logical_and(j >= n_plain_tiles, j < n_bf16_tiles))
    def _():
        ob_ref[...] = _sigmoid(r).astype(ob_ref.dtype)

    @pl.when(j >= n_bf16_tiles)
    def _():
        of_ref[...] = r


def in_projection(x, gain, w, n_plain, n_bf16, *, tm=1024):
    m, k = x.shape
    tn = IN_PROJ_TILE
    nb = n_bf16 // tn
    nf = (w.shape[1] - n_bf16) // tn
    return pl.pallas_call(
        functools.partial(_in_projection_kernel, n_plain_tiles=n_plain // tn, n_bf16_tiles=nb),
        out_shape=(jax.ShapeDtypeStruct((m, nb * tn), BF16), jax.ShapeDtypeStruct((m, nf * tn), F32)),
        grid=(m // tm, nb + nf),
        in_specs=[
            pl.BlockSpec((tm, k), lambda i, j: (i, 0)),
            pl.BlockSpec((1, k), lambda i, j: (0, 0)),
            pl.BlockSpec((k, tn), lambda i, j: (0, j)),
        ],
        out_specs=(pl.BlockSpec((tm, tn), lambda i, j: (i, jnp.minimum(j, nb - 1))),
                   pl.BlockSpec((tm, tn), lambda i, j: (i, jnp.maximum(j - nb, 0)))),
        scratch_shapes=[pltpu.VMEM((tm, k), BF16)],
        compiler_params=_params("parallel", "arbitrary"),
        name="in_projection",
    )(x, gain.reshape(1, k), w)


def _rms_swiglu_kernel(x_ref, g_ref, wg_ref, wu_ref, o_ref, h_ref):
    @pl.when(pl.program_id(1) == 0)
    def _():
        h_ref[...] = _rms_norm_rows(x_ref[...], g_ref[...]).astype(BF16)

    h = h_ref[...]
    a = jnp.dot(h, wg_ref[...], preferred_element_type=F32)
    b = jnp.dot(h, wu_ref[...], preferred_element_type=F32)
    o_ref[...] = (a * _sigmoid(a) * b).astype(o_ref.dtype)


def rms_swiglu_up(x, gain, wg, wu, *, tm, tn):
    m, k = x.shape
    n = wg.shape[1]
    return pl.pallas_call(
        _rms_swiglu_kernel,
        out_shape=jax.ShapeDtypeStruct((m, n), BF16),
        grid=(m // tm, n // tn),
        in_specs=[
            pl.BlockSpec((tm, k), lambda i, j: (i, 0)),
            pl.BlockSpec((1, k), lambda i, j: (0, 0)),
            pl.BlockSpec((k, tn), lambda i, j: (0, j)),
            pl.BlockSpec((k, tn), lambda i, j: (0, j)),
        ],
        out_specs=pl.BlockSpec((tm, tn), lambda i, j: (i, j)),
        scratch_shapes=[pltpu.VMEM((tm, k), BF16)],
        compiler_params=_params("parallel", "arbitrary"),
        name="rms_swiglu_up",
    )(x, gain.reshape(1, k), wg, wu)


def _matmul_residual_kernel(a_ref, w_ref, r_ref, o_ref):
    o_ref[...] = r_ref[...] + jnp.dot(a_ref[...], w_ref[...], preferred_element_type=F32)


def matmul_residual(a, w, res, *, tm, tn):
    m, k = a.shape
    n = w.shape[1]
    return pl.pallas_call(
        _matmul_residual_kernel,
        out_shape=jax.ShapeDtypeStruct((m, n), F32),
        grid=(m // tm, n // tn),
        in_specs=[
            pl.BlockSpec((tm, k), lambda i, j: (i, 0)),
            pl.BlockSpec((k, tn), lambda i, j: (0, j)),
            pl.BlockSpec((tm, tn), lambda i, j: (i, j)),
        ],
        out_specs=pl.BlockSpec((tm, tn), lambda i, j: (i, j)),
        compiler_params=_params("parallel", "arbitrary"),
        name="matmul_residual",
    )(a, w, res)


class RawBranch:
    def __init__(self, o_fwd, o_bwd, og_name, gain, silu_gate):
        self.o_fwd, self.o_bwd, self.og_name, self.gain, self.silu_gate = o_fwd, o_bwd, og_name, gain, silu_gate


def _merge_kernel(*refs, raw):
    pos = 0
    br = []
    for kind in raw:
        width = 1 if kind is None else 4
        br.append(refs[pos:pos + width])
        pos += width
    gl_refs = refs[pos:pos + N_BRANCH]
    wb_ref, o_ref, fin_ref = refs[pos + N_BRANCH:pos + N_BRANCH + 3]
    raw_slot = {n: s for s, n in enumerate(n for n, kind in enumerate(raw) if kind is not None)}

    @pl.when(pl.program_id(1) == 0)
    def _():
        for n, slot in raw_slot.items():
            of_ref, ob_ref, og_ref, gain_ref = br[n]
            for h in range(BRANCH_WIDTH // LANES):
                cols = slice(h * LANES, (h + 1) * LANES)
                y = _rms_norm_rows(of_ref[:, cols] + ob_ref[:, cols], gain_ref[...])
                g = og_ref[:, cols].astype(F32)
                gate = _sigmoid(g)
                if raw[n]:
                    gate = g * gate
                fin_ref[slot, :, cols] = (y * gate).astype(fin_ref.dtype)

    acc = None
    for n in range(N_BRANCH):
        b = br[n][0][...] if raw[n] is None else fin_ref[raw_slot[n]]
        y = jnp.dot(b, wb_ref[n], preferred_element_type=F32)
        t = gl_refs[n][...].astype(F32) * y
        acc = t if acc is None else acc + t
    o_ref[...] = acc.astype(o_ref.dtype)


def merge_branches(branches, pb, w_branch, *, tm, tn):
    m = pb.shape[0]
    d = D_MODEL
    tiles_per_branch = d // tn
    tile0 = PB_COL["gates"] // tn
    row_block = pl.BlockSpec((tm, BRANCH_WIDTH), lambda i, j: (i, 0))
    in_specs, operands, raw = [], [], []
    for b in branches:
        if isinstance(b, RawBranch):
            og_col = PB_COL[b.og_name] // BRANCH_WIDTH
            in_specs += [row_block, row_block,
                         pl.BlockSpec((tm, BRANCH_WIDTH), functools.partial(lambda i, j, c: (i, c), c=og_col)),
                         pl.BlockSpec((1, LANES), lambda i, j: (0, 0))]
            operands += [b.o_fwd, b.o_bwd, pb, b.gain.astype(F32).reshape(1, LANES)]
            raw.append(b.silu_gate)
        else:
            in_specs.append(row_block)
            operands.append(b)
            raw.append(None)
    in_specs += [
        pl.BlockSpec((tm, tn), functools.partial(lambda i, j, n: (i, tile0 + n * tiles_per_branch + j), n=n))
        for n in range(N_BRANCH)
    ]
    in_specs += [pl.BlockSpec((N_BRANCH, BRANCH_WIDTH, tn), lambda i, j: (0, 0, j))]
    n_raw = sum(kind is not None for kind in raw)
    return pl.pallas_call(
        functools.partial(_merge_kernel, raw=tuple(raw)),
        out_shape=jax.ShapeDtypeStruct((m, d), BF16),
        grid=(m // tm, d // tn),
        in_specs=in_specs,
        out_specs=pl.BlockSpec((tm, tn), lambda i, j: (i, j)),
        scratch_shapes=[pltpu.VMEM((max(n_raw, 1), tm, BRANCH_WIDTH), BF16)],
        compiler_params=_params("parallel", "arbitrary"),
        name="merge_branches",
    )(*operands, *([pb] * N_BRANCH), w_branch)


def _router_kernel(x_ref, g_ref, w_ref, b_ref, o_ref, h_ref, cnt_ref, run_ref, *, n_experts):
    @pl.when(pl.program_id(0) == 0)
    def _():
        run_ref[...] = jnp.zeros_like(run_ref)

    h = _rms_norm_rows(x_ref[...], g_ref[...])
    h_ref[...] = h.astype(h_ref.dtype)
    logits = _dot_f32(h, w_ref[...]) + b_ref[...]
    lane = lax.broadcasted_iota(jnp.int32, logits.shape, 1).astype(F32)
    neg = -jnp.inf
    lm = jnp.where(lane < n_experts, logits, neg)
    m1 = jnp.max(lm, axis=-1, keepdims=True)
    i1 = jnp.min(jnp.where(lm == m1, lane, float(LANES)), axis=-1, keepdims=True)
    lm2 = jnp.where(lane == i1, neg, lm)
    m2 = jnp.max(lm2, axis=-1, keepdims=True)
    i2 = jnp.min(jnp.where(lm2 == m2, lane, float(LANES)), axis=-1, keepdims=True)
    t = jnp.exp(m2 - m1)
    den = 1.0 + t

    tm = logits.shape[0]
    before = (lax.broadcasted_iota(jnp.int32, (tm, tm), 1)
              < lax.broadcasted_iota(jnp.int32, (tm, tm), 0))
    before = jnp.where(before, 1.0, 0.0).astype(BF16)
    pick1 = lane == i1
    pick2 = lane == i2
    oh1 = jnp.where(pick1, 1.0, 0.0)
    oh2 = jnp.where(pick2, 1.0, 0.0)
    pre1 = jnp.dot(before, oh1.astype(BF16), preferred_element_type=F32)
    pre2 = jnp.dot(before, oh2.astype(BF16), preferred_element_type=F32)
    tot1 = jnp.sum(oh1, axis=0, keepdims=True)
    tot2 = jnp.sum(oh2, axis=0, keepdims=True)
    run = run_ref[...]
    rank1 = jnp.sum(jnp.where(pick1, pre1 + run, 0.0), axis=-1, keepdims=True)
    rank2 = jnp.sum(jnp.where(pick2, pre2 + (run + tot1), 0.0), axis=-1, keepdims=True)
    run = run + tot1 + tot2
    run_ref[...] = run
    cnt_ref[...] = jnp.broadcast_to(run, cnt_ref.shape)

    out = jnp.where(lane == 0, 1.0 / den, jnp.where(lane == 1, t / den, jnp.where(lane == 2, i1, i2)))
    out = jnp.where(lane == 4, rank1, jnp.where(lane == 5, rank2, out))
    o_ref[...] = jnp.where(lane < 6, out, 0.0)


def router_top2(x, gain, w_router, b_router, *, tm=512):
    m, k = x.shape
    e = w_router.shape[1]
    w_pad = jnp.zeros((k, LANES), F32).at[:, :e].set(w_router.astype(F32))
    b_pad = jnp.zeros((1, LANES), F32).at[0, :e].set(b_router.astype(F32))
    route, h, cnt = pl.pallas_call(
        functools.partial(_router_kernel, n_experts=e),
        out_shape=(jax.ShapeDtypeStruct((m, LANES), F32), jax.ShapeDtypeStruct((m, k), BF16),
                   jax.ShapeDtypeStruct((8, LANES), F32)),
        grid=(m // tm,),
        in_specs=[
            pl.BlockSpec((tm, k), lambda i: (i, 0)),
            pl.BlockSpec((1, k), lambda i: (0, 0)),
            pl.BlockSpec((k, LANES), lambda i: (0, 0)),
            pl.BlockSpec((1, LANES), lambda i: (0, 0)),
        ],
        out_specs=(pl.BlockSpec((tm, LANES), lambda i: (i, 0)), pl.BlockSpec((tm, k), lambda i: (i, 0)),
                   pl.BlockSpec((8, LANES), lambda i: (0, 0))),
        scratch_shapes=[pltpu.VMEM((1, LANES), F32)],
        compiler_params=_params("arbitrary"),
        name="router_top2",
    )(x, gain.reshape(1, k), w_pad, b_pad)
    return route, h, cnt[0, :e].astype(jnp.int32)


MOE_TILE = 1024
MOE_SUB = 256
MOE_FF_TILE = 512


def _moe_kernel(tile_e_ref, tile_rows_ref, n_used_ref, x_ref, wg_ref, wu_ref, wd_ref, o_ref, acc_ref):
    i = pl.program_id(0)
    j = pl.program_id(1)
    last = pl.num_programs(1) - 1
    valid = tile_rows_ref[i]
    n_sub = (valid + (MOE_SUB - 1)) // MOE_SUB

    for k in range(1, MOE_TILE // MOE_SUB + 1):
        rows = slice(0, k * MOE_SUB)

        @pl.when(n_sub == k)
        def _(rows=rows):
            x = x_ref[rows, :]
            a = jnp.dot(x, wg_ref[0].astype(BF16), preferred_element_type=F32)
            b = jnp.dot(x, wu_ref[0].astype(BF16), preferred_element_type=F32)
            act = (a * _sigmoid(a) * b).astype(BF16)
            part = jnp.dot(act, wd_ref[0].astype(BF16), preferred_element_type=F32)

            @pl.when(j == 0)
            def _():
                acc_ref[rows, :] = part

            @pl.when(j > 0)
            def _():
                acc_ref[rows, :] += part

    for s in range(0, MOE_TILE, MOE_SUB):
        rows = slice(s, s + MOE_SUB)
        filled = s < valid

        @pl.when(jnp.logical_and(filled, j == last))
        def _(rows=rows):
            o_ref[rows, :] = acc_ref[rows, :].astype(o_ref.dtype)

        @pl.when(jnp.logical_and(jnp.logical_not(filled), j == last))
        def _(rows=rows):
            o_ref[rows, :] = jnp.zeros((MOE_SUB, o_ref.shape[1]), o_ref.dtype)


def moe_experts(xb, tile_e, tile_rows, n_used, wg, wu, wd):
    rows, d = xb.shape
    ff = wg.shape[2]
    tm, tf = MOE_TILE, MOE_FF_TILE
    n_tiles = rows // tm
    last_j = ff // tf - 1

    def x_map(i, j, te, tr, nu):
        return (jnp.minimum(i, nu[0] - 1), 0)

    def up_map(i, j, te, tr, nu):
        return (te[i], 0, jnp.where(i < nu[0], j, last_j))

    def down_map(i, j, te, tr, nu):
        return (te[i], jnp.where(i < nu[0], j, last_j), 0)

    grid_spec = pltpu.PrefetchScalarGridSpec(
        num_scalar_prefetch=3,
        grid=(n_tiles, ff // tf),
        in_specs=[
            pl.BlockSpec((tm, d), x_map),
            pl.BlockSpec((1, d, tf), up_map),
            pl.BlockSpec((1, d, tf), up_map),
            pl.BlockSpec((1, tf, d), down_map),
        ],
        out_specs=pl.BlockSpec((tm, d), lambda i, j, te, tr, nu: (i, 0)),
        scratch_shapes=[pltpu.VMEM((tm, d), F32)],
    )
    return pl.pallas_call(
        _moe_kernel,
        out_shape=jax.ShapeDtypeStruct((rows, d), BF16),
        grid_spec=grid_spec,
        compiler_params=_params("arbitrary", "arbitrary"),
        name="moe_experts",
    )(tile_e, tile_rows, n_used, xb, wg, wu, wd)


def _moe_combine_kernel(x_ref, y0_ref, y1_ref, r_ref, o_ref):
    w = r_ref[...]
    o_ref[...] = x_ref[...] + w[:, 0:1] * y0_ref[...].astype(F32) + w[:, 1:2] * y1_ref[...].astype(F32)


def moe_combine(x2d, y0, y1, route, *, tm=512):
    n, d = x2d.shape
    row_block = pl.BlockSpec((tm, d), lambda i: (i, 0))
    return pl.pallas_call(
        _moe_combine_kernel,
        out_shape=jax.ShapeDtypeStruct((n, d), F32),
        grid=(n // tm,),
        in_specs=[row_block, row_block, row_block, pl.BlockSpec((tm, LANES), lambda i: (i, 0))],
        out_specs=row_block,
        compiler_params=_params("parallel"),
        name="moe_combine",
    )(x2d, y0, y1, route)


def moe_layer(x2d, gain, w_router, b_router, wg, wu, wd):
    n, d = x2d.shape
    e = N_EXPERTS
    route, h, counts = router_top2(x2d, gain, w_router, b_router)
    nk = n * MOE_TOP_K
    n_tiles = -(-nk // MOE_TILE) + e
    flat_e = route[:, 2:2 + MOE_TOP_K].astype(jnp.int32).reshape(nk)
    rank = route[:, 4:4 + MOE_TOP_K].astype(jnp.int32).reshape(nk)
    flat_tok = jnp.repeat(jnp.arange(n, dtype=jnp.int32), MOE_TOP_K)
    padded = (counts + MOE_TILE - 1) // MOE_TILE * MOE_TILE
    pad_end = jnp.cumsum(padded)
    pad_start = pad_end - padded
    slot = (pad_start[flat_e] + rank).astype(jnp.int32)
    n_slots = n_tiles * MOE_TILE
    slot_tok = (jnp.arange(n_slots, dtype=jnp.int32) % n).at[slot].set(flat_tok)
    tile_start = jnp.arange(n_tiles, dtype=jnp.int32) * MOE_TILE
    tile_e = jnp.minimum(jnp.searchsorted(pad_end, tile_start, side="right"), e - 1).astype(jnp.int32)
    tile_rows = jnp.clip(pad_start[tile_e] + counts[tile_e] - tile_start, 0, MOE_TILE).astype(jnp.int32)
    tile_rows = jnp.where(tile_start < pad_end[-1], tile_rows, 0)
    n_used = (pad_end[-1] // MOE_TILE).astype(jnp.int32).reshape(1)
    tile_e = jnp.where(tile_start < pad_end[-1], tile_e, tile_e[jnp.maximum(n_used[0] - 1, 0)])

    xb = h[slot_tok]
    yb = moe_experts(xb, tile_e, tile_rows, n_used, wg, wu, wd)
    slot2 = slot.reshape(n, MOE_TOP_K)
    return moe_combine(x2d, yb[slot2[:, 0]], yb[slot2[:, 1]], route)


_SRC = dict(zip(
    ("na_q", "na_k", "na_v", "gla_q", "gla_k", "gla_v", "gla_lr", "gla_og", "gdn_qkv", "gdn_a", "gdn_b",
     "gdn_og", "hg_q", "hg_f", "hg_i", "hg_og", "mem_q", "gates"),
    zip(np.cumsum((0,) + IN_WIDTHS[:-1]).tolist(), IN_WIDTHS)))
_PB_ORDER = ("na_q", "na_k", "na_v", "gla_q", "gla_k", "gla_v", "gla_og", "gdn_qkv", "gdn_og", "hg_q", "hg_i",
             "hg_og", "mem_q", "gates")
_PF_ORDER = ("hg_f", "gla_lr", "gdn_a", "gdn_b")
PB_COL = {}
_c = 0
for _name in _PB_ORDER:
    PB_COL[_name] = _c
    _c += _SRC[_name][1]
PB_WIDTH = _c
PF_COL = {}
_c = 0
for _name in _PF_ORDER:
    PF_COL[_name] = _c
    _c += _SRC[_name][1]
PF_WIDTH = -(-_c // IN_PROJ_TILE) * IN_PROJ_TILE
PF_SMALL_COL = PF_COL["gla_lr"]
GDN_A_LANE = PF_COL["gdn_a"] - PF_SMALL_COL
GDN_B_LANE = PF_COL["gdn_b"] - PF_SMALL_COL


def _rearrange_w_in(w):
    w = w.astype(BF16)
    cols = [w[:, _SRC[n][0]:_SRC[n][0] + _SRC[n][1]] for n in _PB_ORDER + _PF_ORDER]
    cols.append(jnp.zeros((w.shape[0], PB_WIDTH + PF_WIDTH - P_IN), BF16))
    return jnp.concatenate(cols, axis=1)


def _segment_rms(x, gain, seg_ones, seg_width):
    sq = x * x
    hi = sq.astype(BF16)
    lo = (sq - hi.astype(F32)).astype(BF16)
    ss = (jnp.dot(hi, seg_ones, preferred_element_type=F32)
          + jnp.dot(lo, seg_ones, preferred_element_type=F32))
    return x * lax.rsqrt(ss * (1.0 / seg_width) + RMS_EPS) * gain


NA_ROWS_PER_STEP = 8
NA_BAND = NA_WIN_ROWS * GRID_W


def _na_bias_table(rel_bias):
    c = np.arange(GRID_W)
    dc = np.clip(c[None, :] - c[:, None], 1 - NA_WIN_COLS, NA_WIN_COLS - 1) + (NA_WIN_COLS - 1)
    col_start = np.clip(c - NA_WIN_COLS // 2, 0, GRID_W - NA_WIN_COLS)
    col_in = (c[None, :] >= col_start[:, None]) & (c[None, :] < col_start[:, None] + NA_WIN_COLS)
    onehot = (dc[None] == np.arange(2 * NA_WIN_COLS - 1)[:, None, None]).astype(np.float32)
    base = jnp.einsum("hrc,cqk->hrqk", rel_bias.astype(F32), onehot, precision=lax.Precision.HIGHEST)
    base = jnp.where(col_in[None, None], base, MASK_VALUE)
    tables = []
    for cfg in range(NA_WIN_ROWS):
        rows = base[:, NA_WIN_ROWS - 1 - cfg:2 * NA_WIN_ROWS - 1 - cfg]
        tables.append(rows.transpose(0, 2, 1, 3).reshape(NA_HEADS // 2, 2 * GRID_W, NA_BAND))
    return jnp.stack(tables)


def _na_kernel(q_ref, k_ref, v_ref, qg_ref, kg_ref, seg_ref, bias_ref, o_ref, kn_ref):
    step = pl.program_id(1)
    rows_total = k_ref.shape[0] // GRID_W
    seg = seg_ref[...]

    @pl.when(step == 0)
    def _():
        def norm_keys(t, carry):
            rows = pl.ds(pl.multiple_of(t * 256, 256), 256)
            kn_ref[rows, :] = _segment_rms(k_ref[rows, :].astype(F32), kg_ref[...], seg, NA_HEAD_DIM).astype(BF16)
            return carry
        lax.fori_loop(0, k_ref.shape[0] // 256, norm_keys, 0)

    lane = lax.broadcasted_iota(jnp.int32, (1, LANES), 1)
    low_half = lane < NA_HEAD_DIM

    def one_row(rr, carry):
        r = step * NA_ROWS_PER_STEP + rr
        row_start = jnp.clip(r - NA_WIN_ROWS // 2, 0, rows_total - NA_WIN_ROWS)
        cfg = r - row_start
        qrows = pl.ds(pl.multiple_of(rr * GRID_W, GRID_W), GRID_W)
        band = pl.ds(pl.multiple_of(row_start * GRID_W, GRID_W), NA_BAND)
        qn = (_segment_rms(q_ref[qrows, :].astype(F32), qg_ref[...], seg, NA_HEAD_DIM)
              * (NA_HEAD_DIM ** -0.5)).astype(BF16)
        pairs = range(NA_HEADS // 2)
        pair_cols = [slice(pair * LANES, (pair + 1) * LANES) for pair in pairs]
        scores = []
        for cols in pair_cols:
            qp = qn[:, cols]
            q2 = jnp.concatenate([jnp.where(low_half, qp, jnp.zeros_like(qp)),
                                  jnp.where(low_half, jnp.zeros_like(qp), qp)], axis=0)
            scores.append(lax.dot_general(q2, kn_ref[band, cols], (((1,), (1,)), ((), ())),
                                          preferred_element_type=F32))
        exps, sums = [], []
        for s, pair in zip(scores, pairs):
            s = s + bias_ref[cfg, pair]
            e = jnp.exp(s - jnp.max(s, axis=-1, keepdims=True))
            sums.append(jnp.sum(e, axis=-1, keepdims=True))
            exps.append(e.astype(BF16))
        outs = [jnp.dot(e, v_ref[band, cols], preferred_element_type=F32) / l
                for e, l, cols in zip(exps, sums, pair_cols)]
        for o2, cols in zip(outs, pair_cols):
            o_ref[qrows, cols] = jnp.where(low_half, o2[:GRID_W], o2[GRID_W:]).astype(o_ref.dtype)
        return carry

    lax.fori_loop(0, NA_ROWS_PER_STEP, one_row, 0)


def neighbourhood_attention(pb, batch, q_gain, k_gain, rel_bias):
    m = pb.shape[0]
    s = m // batch
    tq = NA_ROWS_PER_STEP * GRID_W
    steps = s // tq
    qg = jnp.tile(q_gain.astype(F32), NA_HEADS).reshape(1, NA_WIDTH)
    kg = jnp.tile(k_gain.astype(F32), NA_HEADS).reshape(1, NA_WIDTH)
    seg = jnp.asarray(np.kron(np.eye(NA_HEADS), np.ones((NA_HEAD_DIM, NA_HEAD_DIM))), BF16)
    bias = _na_bias_table(rel_bias)
    cq, ck, cv = (PB_COL[n] // NA_WIDTH for n in ("na_q", "na_k", "na_v"))
    return pl.pallas_call(
        _na_kernel,
        out_shape=jax.ShapeDtypeStruct((m, NA_WIDTH), BF16),
        grid=(batch, steps),
        in_specs=[
            pl.BlockSpec((tq, NA_WIDTH), lambda b, t: (b * steps + t, cq)),
            pl.BlockSpec((s, NA_WIDTH), lambda b, t: (b, ck)),
            pl.BlockSpec((s, NA_WIDTH), lambda b, t: (b, cv)),
            pl.BlockSpec((1, NA_WIDTH), lambda b, t: (0, 0)),
            pl.BlockSpec((1, NA_WIDTH), lambda b, t: (0, 0)),
            pl.BlockSpec((NA_WIDTH, NA_WIDTH), lambda b, t: (0, 0)),
            pl.BlockSpec((NA_WIN_ROWS, NA_HEADS // 2, 2 * GRID_W, NA_BAND), lambda b, t: (0, 0, 0, 0)),
        ],
        out_specs=pl.BlockSpec((tq, NA_WIDTH), lambda b, t: (b * steps + t, 0)),
        scratch_shapes=[pltpu.VMEM((s, NA_WIDTH), BF16)],
        compiler_params=_params("parallel", "arbitrary"),
        name="neighbourhood_attention",
    )(pb, pb, pb, qg, kg, seg, bias)


def _mem_attn_kernel(q_ref, kv_ref, qg_ref, kg_ref, o_ref, kn_ref):
    @pl.when(pl.program_id(1) == 0)
    def _():
        for h in range(MEM_HEADS):
            cols = slice(h * MEM_HEAD_DIM, (h + 1) * MEM_HEAD_DIM)
            kn_ref[:, cols] = _rms_norm_rows(kv_ref[:, cols].astype(F32), kg_ref[...]).astype(BF16)

    head_cols = [slice(h * MEM_HEAD_DIM, (h + 1) * MEM_HEAD_DIM) for h in range(MEM_HEADS)]
    qns = [_rms_norm_rows(q_ref[:, cols].astype(F32), qg_ref[...]).astype(BF16) for cols in head_cols]
    scores = [lax.dot_general(qn, kn_ref[:, cols], (((1,), (1,)), ((), ())), preferred_element_type=F32)
              for qn, cols in zip(qns, head_cols)]
    exps, sums = [], []
    for s in scores:
        s = s * (MEM_HEAD_DIM ** -0.5)
        e = jnp.exp(s - jnp.max(s, axis=-1, keepdims=True))
        sums.append(jnp.sum(e, axis=-1, keepdims=True))
        exps.append(e.astype(BF16))
    outs = [jnp.dot(e, kv_ref[:, MEM_WIDTH + cols.start:MEM_WIDTH + cols.stop], preferred_element_type=F32)
            for e, cols in zip(exps, head_cols)]
    for o, l, cols in zip(outs, sums, head_cols):
        o_ref[:, cols] = (o / l).astype(o_ref.dtype)


def memory_cross_attention(pb, kv, batch, q_gain, k_gain, *, tq=512):
    m = pb.shape[0]
    steps = m // batch // tq
    n_mem = kv.shape[0] // batch
    cq = PB_COL["mem_q"] // MEM_WIDTH
    return pl.pallas_call(
        _mem_attn_kernel,
        out_shape=jax.ShapeDtypeStruct((m, MEM_WIDTH), BF16),
        grid=(batch, steps),
        in_specs=[
            pl.BlockSpec((tq, MEM_WIDTH), lambda b, t: (b * steps + t, cq)),
            pl.BlockSpec((n_mem, 2 * MEM_WIDTH), lambda b, t: (b, 0)),
            pl.BlockSpec((1, MEM_HEAD_DIM), lambda b, t: (0, 0)),
            pl.BlockSpec((1, MEM_HEAD_DIM), lambda b, t: (0, 0)),
        ],
        out_specs=pl.BlockSpec((tq, MEM_WIDTH), lambda b, t: (b * steps + t, 0)),
        scratch_shapes=[pltpu.VMEM((n_mem, MEM_WIDTH), BF16)],
        compiler_params=_params("parallel", "arbitrary"),
        name="memory_cross_attention",
    )(pb, kv, q_gain.astype(F32).reshape(1, MEM_HEAD_DIM), k_gain.astype(F32).reshape(1, MEM_HEAD_DIM))


LIN_BLOCK = 512
HEAD_V = 128


def _log1p_exp_neg(t):
    return jnp.log(1.0 + jnp.exp(-t))


def _log_sigmoid(x):
    return jnp.minimum(x, 0.0) - _log1p_exp_neg(jnp.abs(x))


def _logaddexp(a, b):
    return jnp.maximum(a, b) + _log1p_exp_neg(jnp.abs(a - b))


def _split_bf16(x, terms):
    parts = []
    for _ in range(terms):
        p = x.astype(BF16)
        parts.append(p)
        x = x - p.astype(F32)
    return parts


def _dot_f32(a, b):
    a_hi, a_lo = _split_bf16(a, 2)
    b_hi, b_lo = _split_bf16(b, 2)
    return (jnp.dot(a_hi, b_hi, preferred_element_type=F32)
            + (jnp.dot(a_hi, b_lo, preferred_element_type=F32) + jnp.dot(a_lo, b_hi, preferred_element_type=F32)))


def _cumsum_rows(mask, x):
    m = jnp.where(mask, 1.0, 0.0).astype(BF16)
    hi, mid, lo = _split_bf16(x, 3)
    return (jnp.dot(m, hi, preferred_element_type=F32)
            + (jnp.dot(m, mid, preferred_element_type=F32) + jnp.dot(m, lo, preferred_element_type=F32)))


def _gla_inputs(refs, rows, direction, params):
    q_ref, k_ref, v_ref, g_ref = refs
    wpad_ref, bias_ref = params
    qc = q_ref[rows, :].astype(F32) * (GLA_HEAD_K ** -0.5)
    kc = k_ref[rows, :].astype(F32)
    gk = _dot_f32(g_ref[rows, :], wpad_ref[direction]) + bias_ref[direction]
    lg = _log_sigmoid(gk) * (1.0 / GLA_GATE_NORMALIZER)
    return qc, kc, v_ref[rows, :], lg


def _hgrn_inputs(refs, rows, direction, params):
    q_ref, v_ref, z_ref = refs
    lb_ref, log_lb_ref, log1m_lb_ref = params
    qr = q_ref[rows, :].astype(F32)
    qc = qr * _sigmoid(qr)
    z = z_ref[rows, :]
    lg = _logaddexp(log_lb_ref[direction], log1m_lb_ref[direction] + _log_sigmoid(z))
    kc = (1.0 - lb_ref[direction]) * _sigmoid(-z)
    return qc, kc, v_ref[rows, :], lg


GDN_CONV_WIDTH = 5
GDN_QKV_WIDTH = 2 * GDN_KEY_WIDTH + GDN_VAL_WIDTH
GDN_HALO = 16


def _gdn_prep_kernel(prev_ref, cur_ref, next_ref, w_ref, o_ref, xp_ref, *, blocks_per_seq):
    i = pl.program_id(0)
    t = cur_ref.shape[0]
    pos = i % blocks_per_seq
    prev = prev_ref[...].astype(F32)
    nxt = next_ref[...].astype(F32)
    xp_ref[0:GDN_HALO, :] = jnp.where(pos == 0, jnp.zeros_like(prev), prev)
    xp_ref[GDN_HALO:GDN_HALO + t, :] = cur_ref[...].astype(F32)
    xp_ref[GDN_HALO + t:, :] = jnp.where(pos == blocks_per_seq - 1, jnp.zeros_like(nxt), nxt)
    half = GDN_CONV_WIDTH // 2
    for g in range(GDN_QKV_WIDTH // LANES):
        cols = slice(g * LANES, (g + 1) * LANES)
        acc = None
        for j in range(GDN_CONV_WIDTH):
            term = xp_ref[GDN_HALO - half + j:GDN_HALO - half + j + t, cols] * w_ref[j:j + 1, cols]
            acc = term if acc is None else acc + term
        y = acc * _sigmoid(acc)
        if g < 2 * GDN_HEADS:
            y = y * lax.rsqrt(jnp.sum(y * y, axis=-1, keepdims=True) + 1e-6)
            if g < GDN_HEADS:
                y = y * (GDN_HEAD_K ** -0.5)
        o_ref[:, cols] = y.astype(o_ref.dtype)


def gdn_prep(pb, batch, conv_w, *, t=512):
    m = pb.shape[0]
    blocks_per_seq = m // batch // t
    halo_per_block = t // GDN_HALO
    col = PB_COL["gdn_qkv"] // GDN_QKV_WIDTH
    last_halo = m // GDN_HALO - 1
    return pl.pallas_call(
        functools.partial(_gdn_prep_kernel, blocks_per_seq=blocks_per_seq),
        out_shape=jax.ShapeDtypeStruct((m, GDN_QKV_WIDTH), BF16),
        grid=(m // t,),
        in_specs=[
            pl.BlockSpec((GDN_HALO, GDN_QKV_WIDTH), lambda i: (jnp.maximum(i * halo_per_block - 1, 0), col)),
            pl.BlockSpec((t, GDN_QKV_WIDTH), lambda i: (i, col)),
            pl.BlockSpec((GDN_HALO, GDN_QKV_WIDTH),
                         lambda i: (jnp.minimum((i + 1) * halo_per_block, last_halo), col)),
            pl.BlockSpec((GDN_CONV_WIDTH, GDN_QKV_WIDTH), lambda i: (0, 0)),
        ],
        out_specs=pl.BlockSpec((t, GDN_QKV_WIDTH), lambda i: (i, 0)),
        scratch_shapes=[pltpu.VMEM((t + 2 * GDN_HALO, GDN_QKV_WIDTH), F32)],
        compiler_params=_params("parallel"),
        name="gdn_prep",
    )(pb, pb, pb, conv_w.astype(F32))


def _softplus(x):
    return jnp.maximum(x, 0.0) + _log1p_exp_neg(jnp.abs(x))


GDN_PACK = GDN_HEADS * GDN_CHUNK
GDN_WY_BLOCK = 512


def _stack_heads(x, width):
    heads = x.shape[1] // width
    lane = lax.broadcasted_iota(jnp.int32, (1, x.shape[1]), 1)
    return jnp.concatenate(
        [jnp.where((lane >= h * width) & (lane < (h + 1) * width), x, 0.0).astype(BF16) for h in range(heads)],
        axis=0)


def _packed_mm(x, y):
    return jnp.dot(x.astype(BF16), _stack_heads(y, GDN_CHUNK), preferred_element_type=F32)


def _packed_inverses(mats):
    c = GDN_CHUNK
    ii = lax.broadcasted_iota(jnp.int32, (c, GDN_PACK), 0)
    jj = lax.broadcasted_iota(jnp.int32, (c, GDN_PACK), 1) % c
    eye = (ii == jj).astype(F32)

    def same_block(s):
        return (ii // s) == (jj // s)

    ds = [jnp.where(same_block(8), a, 0.0) for a in mats]
    d2s = [_packed_mm(d, d) for d in ds]
    d4s = [_packed_mm(d2, d2) for d2 in d2s]
    ts = [_packed_mm(eye - d, eye + d2) for d, d2 in zip(ds, d2s)]
    ts = [_packed_mm(t, eye + d4) for t, d4 in zip(ts, d4s)]
    s = 8
    while s < c:
        off = same_block(2 * s) & jnp.logical_not(same_block(s))
        ets = [_packed_mm(jnp.where(off, a, 0.0), t) for a, t in zip(mats, ts)]
        ts = [t - _packed_mm(t, et) for t, et in zip(ts, ets)]
        s *= 2
    return ts


def _gdn_wy_kernel(qkv_ref, small_ref, a_ref, dtb_ref, selg_ref, selk_ref, selb_ref, *out_refs):
    c = GDN_CHUNK
    n_chunks = qkv_ref.shape[0] // c
    ii = lax.broadcasted_iota(jnp.int32, (c, c), 0)
    jj = lax.broadcasted_iota(jnp.int32, (c, c), 1)
    pi = lax.broadcasted_iota(jnp.int32, (c, GDN_PACK), 0)
    pj = lax.broadcasted_iota(jnp.int32, (c, GDN_PACK), 1) % c
    eye_p = (pi == pj).astype(F32)
    ones_cc = jnp.ones((c, c), BF16)

    problems = [(ch, d) for ch in range(n_chunks) for d in range(2)]
    chunk_in = []
    for ch in range(n_chunks):
        rows = slice(ch * c, (ch + 1) * c)
        qkv = qkv_ref[rows, :]
        small = small_ref[rows, :]
        kf = qkv[:, GDN_KEY_WIDTH:2 * GDN_KEY_WIDTH].astype(F32)
        chunk_in.append(dict(
            qf=qkv[:, :GDN_KEY_WIDTH].astype(F32), kf=kf, vf=qkv[:, 2 * GDN_KEY_WIDTH:].astype(F32),
            kbd=_stack_heads(kf, HEAD_V),
            log_alpha=a_ref[...] * _softplus(small + dtb_ref[...]),
            beta_all=_sigmoid(small)))

    def sel3(x, sel):
        hi, mid, lo = _split_bf16(x, 3)
        return (jnp.dot(hi, sel, preferred_element_type=F32)
                + (jnp.dot(mid, sel, preferred_element_type=F32) + jnp.dot(lo, sel, preferred_element_type=F32)))

    g_all = [_cumsum_rows((jj >= ii) if d else (jj <= ii), chunk_in[ch]["log_alpha"]) for ch, d in problems]
    g_pack = [sel3(g, selg_ref[d]) for g, (ch, d) in zip(g_all, problems)]
    g_wide = [sel3(g, selk_ref[d]) for g, (ch, d) in zip(g_all, problems)]
    beta_w = [sel3(chunk_in[ch]["beta_all"], selb_ref[d]) for ch, d in problems]
    g_rowp = []
    for gp in g_pack:
        hi, mid, lo = _split_bf16(gp * eye_p, 3)
        g_rowp.append(jnp.dot(ones_cc, hi, preferred_element_type=F32)
                      + (jnp.dot(ones_cc, mid, preferred_element_type=F32)
                         + jnp.dot(ones_cc, lo, preferred_element_type=F32)))
    decays, k_betas = [], []
    for gp, gr, bw, (ch, d) in zip(g_pack, g_rowp, beta_w, problems):
        incl = (pj >= pi) if d else (pj <= pi)
        decays.append(jnp.where(incl, jnp.exp(jnp.where(incl, gp - gr, 0.0)), 0.0))
        k_betas.append(chunk_in[ch]["kf"] * bw)
    kq = [lax.dot_general(jnp.concatenate([kb, chunk_in[ch]["qf"]], axis=0).astype(BF16), chunk_in[ch]["kbd"],
                          (((1,), (1,)), ((), ())), preferred_element_type=F32)
          for kb, (ch, d) in zip(k_betas, problems)]
    a_mats = []
    for x, dec, (ch, d) in zip(kq, decays, problems):
        strict = (pj > pi) if d else (pj < pi)
        a_mats.append(jnp.where(strict, x[:c] * dec, 0.0))
    t_invs = _packed_inverses(a_mats)

    for idx, (ch, d) in enumerate(problems):
        u_ref, w_ref, attn_ref, qd_ref, kd_ref, gt_ref = out_refs[6 * d:6 * d + 6]
        rows = slice(ch * c, (ch + 1) * c)
        cin = chunk_in[ch]
        gw = g_wide[idx]
        eg = jnp.exp(gw)
        t_b = t_invs[idx].astype(BF16)
        u_ref[rows, :] = jnp.dot(t_b, _stack_heads(cin["vf"] * beta_w[idx], HEAD_V), preferred_element_type=F32)
        w_ref[rows, :] = jnp.dot(t_b, _stack_heads(k_betas[idx] * eg, HEAD_V),
                                 preferred_element_type=F32).astype(w_ref.dtype)
        attn_ref[rows, :] = (kq[idx][c:] * decays[idx]).astype(attn_ref.dtype)
        end = 0 if d else c - 1
        g_end = gw[end:end + 1, :]
        qd_ref[rows, :] = (cin["qf"] * eg).astype(qd_ref.dtype)
        kd_ref[rows, :] = (cin["kf"] * jnp.exp(g_end - gw)).astype(kd_ref.dtype)
        gt_ref[ch:ch + 1, :] = jnp.exp(g_end)


def gdn_wy(qkv, pf, a_scale, dtb):
    m = qkv.shape[0]
    t = GDN_WY_BLOCK
    cpb = t // GDN_CHUNK
    selg = np.zeros((2, LANES, GDN_PACK), np.float32)
    selk = np.zeros((2, LANES, GDN_VAL_WIDTH), np.float32)
    selb = np.zeros((2, LANES, GDN_VAL_WIDTH), np.float32)
    for d in range(2):
        for h in range(GDN_HEADS):
            selg[d, GDN_A_LANE + d * GDN_HEADS + h, h * GDN_CHUNK:(h + 1) * GDN_CHUNK] = 1.0
            selk[d, GDN_A_LANE + d * GDN_HEADS + h, h * HEAD_V:(h + 1) * HEAD_V] = 1.0
            selb[d, GDN_B_LANE + d * GDN_HEADS + h, h * HEAD_V:(h + 1) * HEAD_V] = 1.0
    wide = GDN_VAL_WIDTH
    out_shape, out_specs = [], []
    for _ in range(2):
        for width, dt in ((wide, F32), (wide, BF16), (GDN_PACK, BF16), (wide, BF16), (wide, BF16)):
            out_shape.append(jax.ShapeDtypeStruct((m, width), dt))
            out_specs.append(pl.BlockSpec((t, width), lambda i: (i, 0)))
        out_shape.append(jax.ShapeDtypeStruct((m // GDN_CHUNK, wide), F32))
        out_specs.append(pl.BlockSpec((cpb, wide), lambda i: (i, 0)))
    return pl.pallas_call(
        _gdn_wy_kernel,
        out_shape=tuple(out_shape),
        grid=(m // t,),
        in_specs=[
            pl.BlockSpec((t, GDN_QKV_WIDTH), lambda i: (i, 0)),
            pl.BlockSpec((t, LANES), lambda i: (i, PF_SMALL_COL // LANES)),
            pl.BlockSpec((1, LANES), lambda i: (0, 0)),
            pl.BlockSpec((1, LANES), lambda i: (0, 0)),
            pl.BlockSpec((2, LANES, GDN_PACK), lambda i: (0, 0, 0)),
            pl.BlockSpec((2, LANES, wide), lambda i: (0, 0, 0)),
            pl.BlockSpec((2, LANES, wide), lambda i: (0, 0, 0)),
        ],
        out_specs=tuple(out_specs),
        compiler_params=_params("parallel"),
        name="gdn_wy",
    )(qkv, pf, a_scale, dtb, jnp.asarray(selg, BF16), jnp.asarray(selk, BF16), jnp.asarray(selb, BF16))


GDN_PAIR = 2 * HEAD_V


def _gdn_scan_kernel(*refs):
    groups = (refs[0:6], refs[6:12])
    out_refs = refs[12:14]
    state_refs = refs[14:16]

    @pl.when(pl.program_id(1) == 0)
    def _():
        for s_ref in state_refs:
            s_ref[...] = jnp.zeros_like(s_ref)

    n_chunks = out_refs[0].shape[0] // GDN_CHUNK
    pairs = GDN_HEADS // 2
    pair_cols = [slice(p * GDN_PAIR, (p + 1) * GDN_PAIR) for p in range(pairs)]
    ri = lax.broadcasted_iota(jnp.int32, (GDN_PAIR, GDN_PAIR), 0) // HEAD_V
    ci = lax.broadcasted_iota(jnp.int32, (GDN_PAIR, GDN_PAIR), 1) // HEAD_V
    diag = ri == ci

    def body(c, carry):
        chunks = (c, n_chunks - 1 - c)
        rows = [pl.ds(pl.multiple_of(ch * GDN_CHUNK, GDN_CHUNK), GDN_CHUNK) for ch in chunks]
        states = [[s_ref[p] for p in range(pairs)] for s_ref in state_refs]
        states_b = [[s.astype(BF16) for s in st] for st in states]
        wq = [[jnp.dot(jnp.concatenate([groups[g][1][rows[g], cols], groups[g][3][rows[g], cols]], axis=0),
                       states_b[g][p], preferred_element_type=F32)
               for p, cols in enumerate(pair_cols)] for g in range(2)]
        ws = [[x[:GDN_CHUNK] for x in wq[g]] for g in range(2)]
        qs = [[x[GDN_CHUNK:] for x in wq[g]] for g in range(2)]
        v_new = [groups[g][0][rows[g], :] - jnp.concatenate(ws[g], axis=1) for g in range(2)]
        av = [jnp.dot(groups[g][2][rows[g], :], _stack_heads(v_new[g], HEAD_V), preferred_element_type=F32)
              for g in range(2)]
        v_new_b = [v.astype(BF16) for v in v_new]
        upd = [[lax.dot_general(groups[g][4][rows[g], cols], v_new_b[g][:, cols], (((0,), (0,)), ((), ())),
                                preferred_element_type=F32) for cols in pair_cols] for g in range(2)]
        for g in range(2):
            out_refs[g][rows[g], :] = jnp.concatenate(qs[g], axis=1) + av[g]
            gt = groups[g][5][pl.ds(chunks[g], 1), :]
            for p, cols in enumerate(pair_cols):
                state_refs[g][p] = states[g][p] * gt[:, cols] + jnp.where(diag, upd[g][p], 0.0)
        return carry

    lax.fori_loop(0, n_chunks, body, 0)


def gated_deltanet_branch(pb, pf, batch, conv_w, a_log, dt_bias, norm_gain):
    m = pb.shape[0]
    nb = m // batch // LIN_BLOCK
    cpb = LIN_BLOCK // GDN_CHUNK
    qkv = gdn_prep(pb, batch, conv_w)
    n_gate = 2 * GDN_HEADS
    a_scale = jnp.zeros((1, LANES), F32).at[0, GDN_A_LANE:GDN_A_LANE + n_gate].set(
        -jnp.exp(a_log.astype(F32)).reshape(n_gate))
    dtb = jnp.zeros((1, LANES), F32).at[0, GDN_A_LANE:GDN_A_LANE + n_gate].set(dt_bias.astype(F32).reshape(n_gate))
    wy = gdn_wy(qkv, pf, a_scale, dtb)
    widths = (GDN_VAL_WIDTH, GDN_VAL_WIDTH, GDN_PACK, GDN_VAL_WIDTH, GDN_VAL_WIDTH)

    def fwd(rows, width):
        return pl.BlockSpec((rows, width), lambda b, t: (b * nb + t, 0))

    def bwd(rows, width):
        return pl.BlockSpec((rows, width), lambda b, t: (b * nb + nb - 1 - t, 0))

    in_specs = [fwd(LIN_BLOCK, w) for w in widths] + [fwd(cpb, GDN_VAL_WIDTH)]
    in_specs += [bwd(LIN_BLOCK, w) for w in widths] + [bwd(cpb, GDN_VAL_WIDTH)]
    state = pltpu.VMEM((GDN_HEADS // 2, GDN_PAIR, GDN_PAIR), F32)
    o_f, o_b = pl.pallas_call(
        _gdn_scan_kernel,
        out_shape=(jax.ShapeDtypeStruct((m, GDN_VAL_WIDTH), F32), jax.ShapeDtypeStruct((m, GDN_VAL_WIDTH), F32)),
        grid=(batch, nb),
        in_specs=in_specs,
        out_specs=(fwd(LIN_BLOCK, GDN_VAL_WIDTH), bwd(LIN_BLOCK, GDN_VAL_WIDTH)),
        scratch_shapes=[state, state],
        compiler_params=_params("parallel", "arbitrary"),
        name="gdn_scan",
    )(*wy)
    return RawBranch(o_f, o_b, "gdn_og", norm_gain, True)


def _dot3(m, x):
    hi, mid, lo = _split_bf16(x, 3)
    return (jnp.dot(m, hi, preferred_element_type=F32)
            + (jnp.dot(m, mid, preferred_element_type=F32) + jnp.dot(m, lo, preferred_element_type=F32)))


LIN_CUM_ROWS = 256
LIN_SCORE_ROWS = 128


def _chunk_causal(n, chunk, reverse):
    i = lax.broadcasted_iota(jnp.int32, (n, n), 0)
    j = lax.broadcasted_iota(jnp.int32, (n, n), 1)
    return ((i // chunk) == (j // chunk)) & ((j >= i) if reverse else (j <= i))


def _lin_intra_kernel(*refs, load_inputs, n_in, n_params, heads):
    dir_refs = (refs[:n_in], refs[n_in:2 * n_in])
    params = refs[2 * n_in:2 * n_in + n_params]
    out_refs = refs[2 * n_in + n_params:]
    c = LIN_CHUNK
    dirs = (0, 1)
    loaded = [load_inputs(dir_refs[d], slice(None), d, params) for d in dirs]
    t, w = loaded[0][0].shape
    dk = w // heads
    nc = t // c
    cums = [jnp.where(_chunk_causal(LIN_CUM_ROWS, c, d == 1), 1.0, 0.0).astype(BF16) for d in dirs]
    bs = [jnp.concatenate([_dot3(cums[d], loaded[d][3][r:r + LIN_CUM_ROWS, :])
                           for r in range(0, t, LIN_CUM_ROWS)], axis=0) for d in dirs]
    qes, kes = [], []
    for d in dirs:
        oi_ref, qd_ref, kd_ref, gt_ref = out_refs[4 * d:4 * d + 4]
        qc, kc, vc, lg = loaded[d]
        b = bs[d]
        b3 = b.reshape(nc, c, w)
        mid = c - 1 - c // 2 if d else c // 2
        end = 0 if d else c - 1
        b_mid = jnp.broadcast_to(b3[:, mid:mid + 1, :], (nc, c, w)).reshape(t, w)
        b_end = jnp.broadcast_to(b3[:, end:end + 1, :], (nc, c, w)).reshape(t, w)
        qes.append((qc * jnp.exp(b - b_mid)).astype(BF16))
        kes.append((kc * jnp.exp(b_mid - b)).astype(BF16))
        qd_ref[...] = (qc * jnp.exp(b)).astype(qd_ref.dtype)
        kd_ref[...] = (kc * jnp.exp(b_end - b)).astype(kd_ref.dtype)
        gt_ref[...] = jnp.exp(b3[:, end, :])
    keeps = [_chunk_causal(LIN_SCORE_ROWS, c, d == 1) for d in dirs]
    lane = lax.broadcasted_iota(jnp.int32, (1, LANES), 1)
    per_win = max(LANES // dk, 1)
    n_rows = LIN_SCORE_ROWS
    for w0 in range(0, heads, per_win):
        win = slice((w0 * dk) // LANES * LANES, (w0 * dk) // LANES * LANES + LANES)
        tiles = [(slice(r, r + n_rows), d) for r in range(0, t, n_rows) for d in dirs]
        scores = []
        for rows, d in tiles:
            qw = qes[d][rows, win]
            if per_win > 1:
                qw = jnp.concatenate(
                    [jnp.where((lane >= i * dk) & (lane < (i + 1) * dk), qw, jnp.zeros_like(qw))
                     for i in range(per_win)], axis=0)
            scores.append(lax.dot_general(qw, kes[d][rows, win], (((1,), (1,)), ((), ())),
                                          preferred_element_type=F32))
        for i in range(per_win):
            vcols = slice((w0 + i) * HEAD_V, (w0 + i + 1) * HEAD_V)
            probs = [jnp.where(keeps[d], s[i * n_rows:(i + 1) * n_rows], 0.0).astype(BF16)
                     for s, (rows, d) in zip(scores, tiles)]
            for p, (rows, d) in zip(probs, tiles):
                out_refs[4 * d][rows, vcols] = jnp.dot(p, loaded[d][2][rows, vcols], preferred_element_type=F32)


def _lin_scan_kernel(*refs, heads, chunk, unroll):
    groups = (refs[0:5], refs[5:10])
    out_refs = refs[10:12]
    state_refs = refs[12:14]

    @pl.when(pl.program_id(1) == 0)
    def _():
        for s_ref in state_refs:
            s_ref[...] = jnp.zeros_like(s_ref)

    n_chunks = out_refs[0].shape[0] // chunk
    w = state_refs[0].shape[1]
    dk = w // heads
    lane = lax.broadcasted_iota(jnp.int32, (1, w), 1)
    masks = [(lane >= h * dk) & (lane < (h + 1) * dk) for h in range(heads)]

    def stack(x):
        return jnp.concatenate([jnp.where(m, x, jnp.zeros_like(x)) for m in masks], axis=0)

    def body(it, carry):
        steps = []
        for u in range(unroll):
            c = it * unroll + u
            steps += [(0, c), (1, n_chunks - 1 - c)]
        prepared = []
        for g, ch in steps:
            rows = pl.ds(pl.multiple_of(ch * chunk, chunk), chunk)
            oi_ref, qd_ref, kd_ref, v_ref, gt_ref = groups[g]
            vc = v_ref[rows, :]
            v4 = jnp.concatenate([vc[:, h * HEAD_V:(h + 1) * HEAD_V] for h in range(heads)], axis=0)
            upd = lax.dot_general(v4, stack(kd_ref[rows, :]), (((0,), (0,)), ((), ())),
                                  preferred_element_type=F32)
            prepared.append((rows, stack(qd_ref[rows, :]), upd, gt_ref[pl.ds(ch, 1), :]))
        states = [s_ref[...] for s_ref in state_refs]
        for (g, ch), (rows, q4, upd, gt) in zip(steps, prepared):
            o_inter = lax.dot_general(q4, states[g].astype(BF16), (((1,), (1,)), ((), ())),
                                      preferred_element_type=F32)
            out_refs[g][rows, :] = groups[g][0][rows, :] + jnp.concatenate(
                [o_inter[h * chunk:(h + 1) * chunk, :] for h in range(heads)], axis=1)
            states[g] = states[g] * gt + upd
        for s_ref, st in zip(state_refs, states):
            s_ref[...] = st
        return carry

    lax.fori_loop(0, n_chunks // unroll, body, 0)


def _bidir_lin_call(name, load_inputs, arrays, col_blocks, widths, params, batch, heads, key_width, v_col):
    m = arrays[0].shape[0]
    t = LIN_BLOCK
    nb = m // batch // t
    cpb = t // LIN_CHUNK
    out_w = heads * HEAD_V
    n_in = len(arrays)

    in_specs, operands = [], []
    for d in range(2):
        for a, wd, cb in zip(arrays, widths, col_blocks):
            in_specs.append(pl.BlockSpec((t, wd), functools.partial(lambda i, c: (i, c), c=cb[d])))
            operands.append(a)
    for p in params:
        in_specs.append(pl.BlockSpec(p.shape, functools.partial(lambda i, nd: (0,) * nd, nd=p.ndim)))
    out_shape, out_specs = [], []
    for _ in range(2):
        for rows_total, rows_blk, width, dt in ((m, t, out_w, F32), (m, t, key_width, BF16),
                                                (m, t, key_width, BF16), (m // LIN_CHUNK, cpb, key_width, F32)):
            out_shape.append(jax.ShapeDtypeStruct((rows_total, width), dt))
            out_specs.append(pl.BlockSpec((rows_blk, width), lambda i: (i, 0)))

    intra = pl.pallas_call(
        functools.partial(_lin_intra_kernel, load_inputs=load_inputs, n_in=n_in, n_params=len(params),
                          heads=heads),
        out_shape=tuple(out_shape),
        grid=(m // t,),
        in_specs=in_specs,
        out_specs=tuple(out_specs),
        compiler_params=_params("parallel"),
        name=name + "_intra",
    )(*operands, *params)

    def fwd(rows, width, col=0):
        return pl.BlockSpec((rows, width), lambda b, s: (b * nb + s, col))

    def bwd(rows, width, col=0):
        return pl.BlockSpec((rows, width), lambda b, s: (b * nb + nb - 1 - s, col))

    scan_specs, scan_ops = [], []
    for d, mk in enumerate((fwd, bwd)):
        oi, qd, kd, gt = intra[4 * d:4 * d + 4]
        scan_specs += [mk(t, out_w), mk(t, key_width), mk(t, key_width), mk(t, out_w, v_col), mk(cpb, key_width)]
        scan_ops += [oi, qd, kd, arrays[0], gt]
    state = pltpu.VMEM((HEAD_V, key_width), F32)
    return pl.pallas_call(
        functools.partial(_lin_scan_kernel, heads=heads, chunk=LIN_CHUNK, unroll=8),
        out_shape=(jax.ShapeDtypeStruct((m, out_w), F32), jax.ShapeDtypeStruct((m, out_w), F32)),
        grid=(batch, nb),
        in_specs=scan_specs,
        out_specs=(fwd(t, out_w), bwd(t, out_w)),
        scratch_shapes=[state, state],
        compiler_params=_params("parallel", "arbitrary"),
        name=name + "_scan",
    )(*scan_ops)


def gla_branch(pb, pf, batch, w_gate_up, b_gate, norm_gain):
    wpad = jnp.zeros((2, LANES, GLA_KEY_WIDTH), F32)
    for d in range(2):
        wpad = wpad.at[d, d * GLA_GATE_RANK:(d + 1) * GLA_GATE_RANK, :].set(w_gate_up[d].astype(F32))
    bias = b_gate.astype(F32).reshape(2, 1, GLA_KEY_WIDTH)
    v_col = PB_COL["gla_v"] // GLA_VAL_WIDTH
    cols = [(PB_COL["gla_q"] // GLA_KEY_WIDTH,) * 2, (PB_COL["gla_k"] // GLA_KEY_WIDTH,) * 2,
            (v_col,) * 2, (PF_SMALL_COL // LANES,) * 2]
    o_f, o_b = _bidir_lin_call("gla", _gla_inputs, [pb, pb, pb, pf], cols,
                               [GLA_KEY_WIDTH, GLA_KEY_WIDTH, GLA_VAL_WIDTH, LANES], [wpad, bias],
                               batch, GLA_HEADS, GLA_KEY_WIDTH, v_col)
    return RawBranch(o_f, o_b, "gla_og", norm_gain, True)


def hgrn2_branch(pb, pf, batch, lower_bound, norm_gain):
    lb = lower_bound.astype(F32).reshape(2, 1, HGRN_KEY_WIDTH)
    log_lb = jnp.log(jnp.maximum(lb, LB_FLOOR))
    log1m_lb = jnp.log1p(-lb)
    zc = PF_COL["hg_f"] // HGRN_KEY_WIDTH
    v_col = PB_COL["hg_i"] // HGRN_VAL_WIDTH
    cols = [(PB_COL["hg_q"] // HGRN_KEY_WIDTH,) * 2, (v_col,) * 2, (zc, zc + 1)]
    o_f, o_b = _bidir_lin_call("hgrn2", _hgrn_inputs, [pb, pb, pf], cols,
                               [HGRN_KEY_WIDTH, HGRN_VAL_WIDTH, HGRN_KEY_WIDTH], [lb, log_lb, log1m_lb],
                               batch, HGRN_HEADS, HGRN_KEY_WIDTH, v_col)
    return RawBranch(o_f, o_b, "hg_og", norm_gain, False)


def kernel(x, mem, g_mix, w_in, na_q_gain, na_k_gain, na_rel_bias, gla_w_gate_up, gla_b_gate, gla_norm_gain, gdn_conv_w, gdn_a_log, gdn_dt_bias, gdn_norm_gain, hgrn_lb_raw, hgrn_norm_gain, g_mem, w_mem_kv, mem_q_gain, mem_k_gain, w_branch, w_out, g_ffn, ffn_w_gate, ffn_w_up, ffn_w_down, moe_w_router, moe_b_router, moe_w_gate, moe_w_up, moe_w_down):
    B, S, D = x.shape
    n_tok = B * S
    lb_w = jax.nn.softmax(hgrn_lb_raw.astype(F32), axis=0)
    hgrn_lb = jnp.cumsum(lb_w, axis=0) - lb_w[0:1]
    x2 = x.reshape(n_tok, D)
    mem2 = mem.reshape(B * mem.shape[1], D)
    for layer in range(DEPTH):
        pb, pf = in_projection(x2, g_mix[layer], _rearrange_w_in(w_in[layer]), PB_COL["gates"], PB_WIDTH)
        kv = rms_matmul(mem2, g_mem[layer], w_mem_kv[layer].astype(BF16), tm=mem2.shape[0], tn=512,
                        out_dtype=BF16)
        branches = [
            neighbourhood_attention(pb, B, na_q_gain[layer], na_k_gain[layer], na_rel_bias[layer]),
            gla_branch(pb, pf, B, gla_w_gate_up[layer], gla_b_gate[layer], gla_norm_gain[layer]),
            gated_deltanet_branch(pb, pf, B, gdn_conv_w[layer], gdn_a_log[layer], gdn_dt_bias[layer],
                                  gdn_norm_gain[layer]),
            hgrn2_branch(pb, pf, B, hgrn_lb[layer], hgrn_norm_gain[layer]),
            memory_cross_attention(pb, kv, B, mem_q_gain[layer], mem_k_gain[layer]),
        ]
        merged = merge_branches(branches, pb, w_branch[layer].astype(BF16), tm=512, tn=1024)
        x2 = matmul_residual(merged, w_out[layer].astype(BF16), x2, tm=512, tn=D)

        j = layer // 2
        if layer % 2 == 0:
            act = rms_swiglu_up(x2, g_ffn[layer], ffn_w_gate[j].astype(BF16), ffn_w_up[j].astype(BF16),
                                tm=1024, tn=512)
            x2 = matmul_residual(act, ffn_w_down[j].astype(BF16), x2, tm=512, tn=1024)
        else:
            x2 = moe_layer(x2, g_ffn[layer], moe_w_router[j], moe_b_router[j], moe_w_gate[j], moe_w_up[j],
                           moe_w_down[j])
    return x2.reshape(B, S, D)
```
